```python
import jax
import jax.numpy as jnp
from jax import lax
import numpy as np

D_MODEL = 2048
BATCH = 8
SEQ = 8192
DEPTH = 1

CTX_LEN = 256
GRID_W = 64
HEAD_DIM = 128
ATTN_Q_HEADS = 8
ATTN_KV_HEADS = 2
ATTN_GROUPS = ATTN_Q_HEADS // ATTN_KV_HEADS
ATTN_WIDTH = ATTN_Q_HEADS * HEAD_DIM
KV_WIDTH = ATTN_KV_HEADS * HEAD_DIM
RET_HEADS = 8
RET_QK_DIM = 128
RET_V_DIM = 128
RET_QK_WIDTH = RET_HEADS * RET_QK_DIM
RET_V_WIDTH = RET_HEADS * RET_V_DIM
D_FF = 5632
Q_BLOCK = 128
RET_CHUNK = 128
ROPE_THETA = 10000.0
NORM_EPS = 1e-6
N_MOD = 9
PROJ_SPLITS = (ATTN_WIDTH, KV_WIDTH, KV_WIDTH, RET_QK_WIDTH, RET_QK_WIDTH, RET_V_WIDTH, RET_V_WIDTH, D_MODEL, D_MODEL)
PROJ_WIDTH = ATTN_WIDTH + 2 * KV_WIDTH + 2 * RET_QK_WIDTH + 2 * RET_V_WIDTH + 2 * D_MODEL

kernel_name = "hybrid_gqa_retention_macaron_dit_block"


def _rms(x):
    xf = x.astype(jnp.float32)
    return (xf * lax.rsqrt(jnp.mean(xf * xf, axis=-1, keepdims=True) + NORM_EPS)).astype(x.dtype)


def _modulate(n, shift, scale):
    return n * (1 + scale) + shift


def _adaln(cond, w_ada, b_ada):
    return jnp.split(jax.nn.silu(cond) @ w_ada + b_ada, N_MOD, axis=-1)


def _swiglu(x, w_in, w_out):
    a, b = jnp.split(x @ w_in, 2, axis=-1)
    return (jax.nn.silu(a) * b) @ w_out


def _split_proj(p):
    cuts = []
    acc = 0
    for w in PROJ_SPLITS[:-1]:
        acc += w
        cuts.append(acc)
    return jnp.split(p, cuts, axis=-1)


def _heads(t, n_heads, d):
    return t.reshape(t.shape[:2] + (n_heads, d))


def _grid_rope(seq_len):
    rows = seq_len // GRID_W
    row = jnp.repeat(jnp.arange(rows, dtype=jnp.float32), GRID_W)
    col = jnp.tile(jnp.arange(GRID_W, dtype=jnp.float32), rows)
    half = HEAD_DIM // 2
    inv_freq = ROPE_THETA ** (-jnp.arange(0, half, 2, dtype=jnp.float32) / half)
    ang = jnp.concatenate([row[:, None] * inv_freq, col[:, None] * inv_freq], axis=-1)
    return jnp.cos(ang), jnp.sin(ang)


def _apply_rope(x, cos, sin):
    xf = x.astype(jnp.float32).reshape(x.shape[:-1] + (HEAD_DIM // 2, 2))
    x0, x1 = xf[..., 0], xf[..., 1]
    c, s = cos[:, None, :], sin[:, None, :]
    out = jnp.stack([x0 * c - x1 * s, x0 * s + x1 * c], axis=-1)
    return out.reshape(x.shape).astype(x.dtype)


def _attend_blocks(q, k, v):
    b, l = q.shape[:2]
    nb = l // Q_BLOCK
    qb = jnp.moveaxis(q.reshape(b, nb, Q_BLOCK, ATTN_KV_HEADS, ATTN_GROUPS, HEAD_DIM), 1, 0)
    scale = HEAD_DIM ** -0.5

    def one_block(qblk):
        s = jnp.einsum("bqkgd,bskd->bkgqs", qblk, k).astype(jnp.float32) * scale
        p = jax.nn.softmax(s, axis=-1)
        return jnp.einsum("bkgqs,bskd->bqkgd", p.astype(v.dtype), v)

    out = lax.map(one_block, qb)
    return jnp.moveaxis(out, 0, 1).reshape(b, l, ATTN_WIDTH)


def _retention_chunkwise(q, k, v, log_gamma, state0):
    b, h, l, _ = q.shape
    n = l // RET_CHUNK
    idx = jnp.arange(RET_CHUNK, dtype=jnp.float32)
    diff = idx[:, None] - idx[None, :]
    lower = diff >= 0
    intra = jnp.where(lower[None], jnp.exp(jnp.where(lower, diff, 0.0)[None] * log_gamma[:, None, None]), 0.0)
    q_dec = jnp.exp((idx + 1.0)[None, :] * log_gamma[:, None])
    k_dec = jnp.exp((RET_CHUNK - 1.0 - idx)[None, :] * log_gamma[:, None])
    chunk_dec = jnp.exp(RET_CHUNK * log_gamma)

    def chunks(t):
        return jnp.moveaxis(t.reshape(b, h, n, RET_CHUNK, t.shape[-1]), 2, 0)

    def step(state, qkv):
        qc, kc, vc = qkv
        inner = jnp.einsum("bhid,bhjd->bhij", qc, kc) * intra
        y = jnp.einsum("bhij,bhje->bhie", inner, vc) + jnp.einsum("bhid,bhde->bhie", qc, state) * q_dec[..., None]
        state = state * chunk_dec[:, None, None] + jnp.einsum("bhjd,bhje->bhde", kc * k_dec[..., None], vc)
        return state, y

    _, ys = lax.scan(step, state0, (chunks(q), chunks(k), chunks(v)))
    return jnp.moveaxis(ys, 0, 2).reshape(b, h, l, v.shape[-1])


def _retention_state(k, v, log_gamma, reverse):
    l = k.shape[2]
    m = jnp.arange(l, dtype=jnp.float32)
    expo = m if reverse else (l - 1.0 - m)
    w = jnp.exp(expo[None, :] * log_gamma[:, None])
    return jnp.einsum("bhld,bhle,hl->bhde", k, v, w)


def _retention_bidir(q, k, v, lg_fwd, lg_bwd, state_fwd, state_bwd):
    flip = lambda t: jnp.flip(t, axis=2)
    y_f = _retention_chunkwise(q, k, v, lg_fwd, state_fwd)
    y_b = flip(_retention_chunkwise(flip(q), flip(k), flip(v), lg_bwd, state_bwd))
    return y_f + y_b


def _ret_heads(t, d):
    return jnp.transpose(_heads(t, RET_HEADS, d), (0, 2, 1, 3)).astype(jnp.float32)


def _retention_out(y, gate):
    b, h, l, d = y.shape
    y = jnp.transpose(_rms(y), (0, 2, 1, 3)).reshape(b, l, h * d).astype(gate.dtype)
    return jax.nn.silu(gate) * y


def _merge(y_attn, y_ret, g_attn, g_ret, w_proj_attn, w_proj_ret, w_out):
    return (jax.nn.sigmoid(g_attn) * (y_attn @ w_proj_attn) + jax.nn.sigmoid(g_ret) * (y_ret @ w_proj_ret)) @ w_out


def _mixer(n_x, n_c, w_in, q_gain, k_gain, decay_logit, w_proj_attn, w_proj_ret, w_out, with_ctx_out):
    seq_len = n_x.shape[1]
    qa_x, ka_x, va_x, qr_x, kr_x, vr_x, gr_x, ga_x, gb_x = _split_proj(n_x @ w_in)
    qa_c, ka_c, va_c, qr_c, kr_c, vr_c, gr_c, ga_c, gb_c = _split_proj(n_c @ w_in)

    cos, sin = _grid_rope(seq_len)
    q_x = _apply_rope(_rms(_heads(qa_x, ATTN_Q_HEADS, HEAD_DIM)) * q_gain, cos, sin)
    k_x = _apply_rope(_rms(_heads(ka_x, ATTN_KV_HEADS, HEAD_DIM)) * k_gain, cos, sin)
    k_c = _rms(_heads(ka_c, ATTN_KV_HEADS, HEAD_DIM)) * k_gain
    v_x = _heads(va_x, ATTN_KV_HEADS, HEAD_DIM)
    v_c = _heads(va_c, ATTN_KV_HEADS, HEAD_DIM)
    k_all = jnp.concatenate([k_c, k_x], axis=1)
    v_all = jnp.concatenate([v_c, v_x], axis=1)
    ya_x = _attend_blocks(q_x, k_all, v_all)

    log_gamma = jax.nn.log_sigmoid(decay_logit.astype(jnp.float32))
    k_scale = RET_QK_DIM ** -0.5
    qr_xh, kr_xh, vr_xh = _ret_heads(qr_x, RET_QK_DIM), _ret_heads(kr_x, RET_QK_DIM) * k_scale, _ret_heads(vr_x, RET_V_DIM)
    kr_ch, vr_ch = _ret_heads(kr_c, RET_QK_DIM) * k_scale, _ret_heads(vr_c, RET_V_DIM)
    state_f = _retention_state(kr_ch, vr_ch, log_gamma[0], False)
    state_b = _retention_state(kr_ch, vr_ch, log_gamma[1], True)
    yr_x = _retention_out(_retention_bidir(qr_xh, kr_xh, vr_xh, log_gamma[0], log_gamma[1], state_f, state_b), gr_x)

    out_x = _merge(ya_x, yr_x, ga_x, gb_x, w_proj_attn, w_proj_ret, w_out)
    if not with_ctx_out:
        return out_x, None

    q_c = _rms(_heads(qa_c, ATTN_Q_HEADS, HEAD_DIM)) * q_gain
    ya_c = _attend_blocks(q_c, k_c, v_c)
    zeros = jnp.zeros_like(state_f)
    qr_ch = _ret_heads(qr_c, RET_QK_DIM)
    yr_c = _retention_out(_retention_bidir(qr_ch, kr_ch, vr_ch, log_gamma[0], log_gamma[1], zeros, zeros), gr_c)
    out_c = _merge(ya_c, yr_c, ga_c, gb_c, w_proj_attn, w_proj_ret, w_out)
    return out_x, out_c


def _fwd_setup_inputs(seed: int = 0) -> dict:
    key = jax.random.key(seed)
    ks = jax.random.split(key, 18)

    def nrm(k, shape, std):
        return jax.random.normal(k, shape, jnp.float32) * std

    heads_idx = jnp.arange(RET_HEADS, dtype=jnp.float32)
    decay_base = jnp.log1p(-(2.0 ** (-(5.0 + heads_idx)))) + (5.0 + heads_idx) * jnp.log(2.0)
    return {
        "x": nrm(ks[0], (BATCH, SEQ, D_MODEL), 1.0),
        "c": nrm(ks[1], (BATCH, D_MODEL), 1.0),
        "ctx": nrm(ks[2], (BATCH, CTX_LEN, D_MODEL), 1.0),
        "c_ctx": nrm(ks[3], (D_MODEL,), 1.0),
        "w_ada": nrm(ks[4], (DEPTH, D_MODEL, N_MOD * D_MODEL), 0.5 * D_MODEL ** -0.5),
        "b_ada": nrm(ks[5], (DEPTH, N_MOD * D_MODEL), 0.01),
        "ffn1_w_in": nrm(ks[6], (DEPTH, D_MODEL, 2 * D_FF), D_MODEL ** -0.5),
        "ffn1_w_out": nrm(ks[7], (DEPTH, D_FF, D_MODEL), D_FF ** -0.5),
        "mix_w_in": nrm(ks[8], (DEPTH, D_MODEL, PROJ_WIDTH), D_MODEL ** -0.5),
        "attn_q_gain": 1.0 + nrm(ks[9], (DEPTH, HEAD_DIM), 0.02),
        "attn_k_gain": 1.0 + nrm(ks[10], (DEPTH, HEAD_DIM), 0.02),
        "ret_decay_logit": decay_base[None, None, :] + nrm(ks[11], (DEPTH, 2, RET_HEADS), 0.1),
        "w_proj_attn": nrm(ks[12], (DEPTH, ATTN_WIDTH, D_MODEL), ATTN_WIDTH ** -0.5),
        "w_proj_ret": nrm(ks[13], (DEPTH, RET_V_WIDTH, D_MODEL), RET_V_WIDTH ** -0.5),
        "mix_w_out": nrm(ks[14], (DEPTH, D_MODEL, D_MODEL), D_MODEL ** -0.5),
        "ffn2_w_in": nrm(ks[15], (DEPTH, D_MODEL, 2 * D_FF), D_MODEL ** -0.5),
        "ffn2_w_out": nrm(ks[16], (DEPTH, D_FF, D_MODEL), D_FF ** -0.5),
        "final_norm": 1.0 + nrm(ks[17], (D_MODEL,), 0.02),
    }


def _fwd_reference(x, c, ctx, c_ctx, w_ada, b_ada, ffn1_w_in, ffn1_w_out, mix_w_in, attn_q_gain, attn_k_gain,
              ret_decay_logit, w_proj_attn, w_proj_ret, mix_w_out, ffn2_w_in, ffn2_w_out, final_norm):
    h_x = x
    h_c = ctx
    for layer in range(DEPTH):
        last = layer == DEPTH - 1
        sh1, sc1, g1, sh2, sc2, g2, sh3, sc3, g3 = [t[:, None, :] for t in _adaln(c, w_ada[layer], b_ada[layer])]
        csh1, csc1, cg1, csh2, csc2, cg2, csh3, csc3, cg3 = _adaln(c_ctx, w_ada[layer], b_ada[layer])

        h_x = h_x + 0.5 * g1 * _swiglu(_modulate(_rms(h_x), sh1, sc1), ffn1_w_in[layer], ffn1_w_out[layer])
        h_c = h_c + 0.5 * cg1 * _swiglu(_modulate(_rms(h_c), csh1, csc1), ffn1_w_in[layer], ffn1_w_out[layer])

        y_x, y_c = _mixer(_modulate(_rms(h_x), sh2, sc2), _modulate(_rms(h_c), csh2, csc2),
                          mix_w_in[layer], attn_q_gain[layer], attn_k_gain[layer], ret_decay_logit[layer],
                          w_proj_attn[layer], w_proj_ret[layer], mix_w_out[layer], not last)
        h_x = h_x + g2 * y_x

        h_x = h_x + 0.5 * g3 * _swiglu(_modulate(_rms(h_x), sh3, sc3), ffn2_w_in[layer], ffn2_w_out[layer])
        if not last:
            h_c = h_c + cg2 * y_c
            h_c = h_c + 0.5 * cg3 * _swiglu(_modulate(_rms(h_c), csh3, csc3), ffn2_w_in[layer], ffn2_w_out[layer])
    return _rms(h_x) * final_norm


import jax as _jax
import jax.numpy as _jnp

TWIN_FORMAT = 'train_step'
FWD_PARAMS = ['x', 'c', 'ctx', 'c_ctx', 'w_ada', 'b_ada', 'ffn1_w_in', 'ffn1_w_out', 'mix_w_in', 'attn_q_gain', 'attn_k_gain', 'ret_decay_logit', 'w_proj_attn', 'w_proj_ret', 'mix_w_out', 'ffn2_w_in', 'ffn2_w_out', 'final_norm']
TWIN_WEIGHTS = ['c_ctx', 'w_ada', 'b_ada', 'ffn1_w_in', 'ffn1_w_out', 'mix_w_in', 'attn_q_gain', 'attn_k_gain', 'ret_decay_logit', 'w_proj_attn', 'w_proj_ret', 'mix_w_out', 'ffn2_w_in', 'ffn2_w_out', 'final_norm']
TWIN_DIFF_INPUT = 'x'
TWIN_INPUTS = ['x', 'c', 'ctx', 'c_ctx', 'w_ada', 'b_ada', 'ffn1_w_in', 'ffn1_w_out', 'mix_w_in', 'attn_q_gain', 'attn_k_gain', 'ret_decay_logit', 'w_proj_attn', 'w_proj_ret', 'mix_w_out', 'ffn2_w_in', 'ffn2_w_out', 'final_norm', 'loss_target', 'm_c_ctx', 'm_w_ada', 'm_b_ada', 'm_ffn1_w_in', 'm_ffn1_w_out', 'm_mix_w_in', 'm_attn_q_gain', 'm_attn_k_gain', 'm_ret_decay_logit', 'm_w_proj_attn', 'm_w_proj_ret', 'm_mix_w_out', 'm_ffn2_w_in', 'm_ffn2_w_out', 'm_final_norm', 'v_c_ctx', 'v_w_ada', 'v_b_ada', 'v_ffn1_w_in', 'v_ffn1_w_out', 'v_mix_w_in', 'v_attn_q_gain', 'v_attn_k_gain', 'v_ret_decay_logit', 'v_w_proj_attn', 'v_w_proj_ret', 'v_mix_w_out', 'v_ffn2_w_in', 'v_ffn2_w_out', 'v_final_norm']
TWIN_OUTPUTS = ['loss', 'grad_x', 'grad_c_ctx', 'grad_w_ada', 'grad_b_ada', 'grad_ffn1_w_in', 'grad_ffn1_w_out', 'grad_mix_w_in', 'grad_attn_q_gain', 'grad_attn_k_gain', 'grad_ret_decay_logit', 'grad_w_proj_attn', 'grad_w_proj_ret', 'grad_mix_w_out', 'grad_ffn2_w_in', 'grad_ffn2_w_out', 'grad_final_norm', 'delta_c_ctx', 'delta_w_ada', 'delta_b_ada', 'delta_ffn1_w_in', 'delta_ffn1_w_out', 'delta_mix_w_in', 'delta_attn_q_gain', 'delta_attn_k_gain', 'delta_ret_decay_logit', 'delta_w_proj_attn', 'delta_w_proj_ret', 'delta_mix_w_out', 'delta_ffn2_w_in', 'delta_ffn2_w_out', 'delta_final_norm', 'new_m_c_ctx', 'new_m_w_ada', 'new_m_b_ada', 'new_m_ffn1_w_in', 'new_m_ffn1_w_out', 'new_m_mix_w_in', 'new_m_attn_q_gain', 'new_m_attn_k_gain', 'new_m_ret_decay_logit', 'new_m_w_proj_attn', 'new_m_w_proj_ret', 'new_m_mix_w_out', 'new_m_ffn2_w_in', 'new_m_ffn2_w_out', 'new_m_final_norm', 'new_v_c_ctx', 'new_v_w_ada', 'new_v_b_ada', 'new_v_ffn1_w_in', 'new_v_ffn1_w_out', 'new_v_mix_w_in', 'new_v_attn_q_gain', 'new_v_attn_k_gain', 'new_v_ret_decay_logit', 'new_v_w_proj_attn', 'new_v_w_proj_ret', 'new_v_mix_w_out', 'new_v_ffn2_w_in', 'new_v_ffn2_w_out', 'new_v_final_norm']
TWIN_LEAF_KINDS = {'loss': 'loss', 'grad_x': 'grad_x', 'grad_c_ctx': 'grad_w', 'grad_w_ada': 'grad_w', 'grad_b_ada': 'grad_w', 'grad_ffn1_w_in': 'grad_w', 'grad_ffn1_w_out': 'grad_w', 'grad_mix_w_in': 'grad_w', 'grad_attn_q_gain': 'grad_w', 'grad_attn_k_gain': 'grad_w', 'grad_ret_decay_logit': 'grad_w', 'grad_w_proj_attn': 'grad_w', 'grad_w_proj_ret': 'grad_w', 'grad_mix_w_out': 'grad_w', 'grad_ffn2_w_in': 'grad_w', 'grad_ffn2_w_out': 'grad_w', 'grad_final_norm': 'grad_w', 'delta_c_ctx': 'delta_w', 'delta_w_ada': 'delta_w', 'delta_b_ada': 'delta_w', 'delta_ffn1_w_in': 'delta_w', 'delta_ffn1_w_out': 'delta_w', 'delta_mix_w_in': 'delta_w', 'delta_attn_q_gain': 'delta_w', 'delta_attn_k_gain': 'delta_w', 'delta_ret_decay_logit': 'delta_w', 'delta_w_proj_attn': 'delta_w', 'delta_w_proj_ret': 'delta_w', 'delta_mix_w_out': 'delta_w', 'delta_ffn2_w_in': 'delta_w', 'delta_ffn2_w_out': 'delta_w', 'delta_final_norm': 'delta_w', 'new_m_c_ctx': 'new_m', 'new_m_w_ada': 'new_m', 'new_m_b_ada': 'new_m', 'new_m_ffn1_w_in': 'new_m', 'new_m_ffn1_w_out': 'new_m', 'new_m_mix_w_in': 'new_m', 'new_m_attn_q_gain': 'new_m', 'new_m_attn_k_gain': 'new_m', 'new_m_ret_decay_logit': 'new_m', 'new_m_w_proj_attn': 'new_m', 'new_m_w_proj_ret': 'new_m', 'new_m_mix_w_out': 'new_m', 'new_m_ffn2_w_in': 'new_m', 'new_m_ffn2_w_out': 'new_m', 'new_m_final_norm': 'new_m', 'new_v_c_ctx': 'new_v', 'new_v_w_ada': 'new_v', 'new_v_b_ada': 'new_v', 'new_v_ffn1_w_in': 'new_v', 'new_v_ffn1_w_out': 'new_v', 'new_v_mix_w_in': 'new_v', 'new_v_attn_q_gain': 'new_v', 'new_v_attn_k_gain': 'new_v', 'new_v_ret_decay_logit': 'new_v', 'new_v_w_proj_attn': 'new_v', 'new_v_w_proj_ret': 'new_v', 'new_v_mix_w_out': 'new_v', 'new_v_ffn2_w_in': 'new_v', 'new_v_ffn2_w_out': 'new_v', 'new_v_final_norm': 'new_v'}


def _forward(args):
    return _fwd_reference(*[args[k] for k in FWD_PARAMS])


def _output_shape():
    def fwd():
        inp = _fwd_setup_inputs(0)
        return _fwd_reference(*[inp[k] for k in FWD_PARAMS])
    out = _jax.eval_shape(fwd)
    return out.shape, out.dtype

N_MICROBATCH = 1
ADAM_LR = 0.001
ADAM_B1 = 0.9
ADAM_B2 = 0.999
ADAM_EPS = 1e-08
ADAM_WD = 0.01
ADAM_STEP = 10
PER_EXAMPLE_BATCH_AXIS = {'x': 0, 'c': 0, 'ctx': 0, 'loss_target': 0}
SHARED_INPUTS = []
_WEIGHT_DTYPES = {'c_ctx': _jnp.float32, 'w_ada': _jnp.float32, 'b_ada': _jnp.float32, 'ffn1_w_in': _jnp.float32, 'ffn1_w_out': _jnp.float32, 'mix_w_in': _jnp.float32, 'attn_q_gain': _jnp.float32, 'attn_k_gain': _jnp.float32, 'ret_decay_logit': _jnp.float32, 'w_proj_attn': _jnp.float32, 'w_proj_ret': _jnp.float32, 'mix_w_out': _jnp.float32, 'ffn2_w_in': _jnp.float32, 'ffn2_w_out': _jnp.float32, 'final_norm': _jnp.float32}
MOMENT_SCALE = {'c_ctx': 1.028327e-02, 'w_ada': 2.283128e-02, 'b_ada': 3.634411e-02, 'ffn1_w_in': 8.260083e-03, 'ffn1_w_out': 1.349439e-02, 'mix_w_in': 1.456358e-02, 'attn_q_gain': 5.860196e-03, 'attn_k_gain': 5.674641e-03, 'ret_decay_logit': 4.821835e-02, 'w_proj_attn': 5.967781e-03, 'w_proj_ret': 1.336719e-02, 'mix_w_out': 1.449321e-02, 'ffn2_w_in': 8.132865e-03, 'ffn2_w_out': 1.325360e-02, 'final_norm': 3.197468e+01}


def _to_microbatches(a, axis):
    t = _jnp.moveaxis(a, axis, 0)
    t = t.reshape((N_MICROBATCH, t.shape[0] // N_MICROBATCH) + t.shape[1:])
    return _jnp.moveaxis(t, 1, axis + 1)


def setup_inputs(seed: int = 0) -> dict:
    inp = _fwd_setup_inputs(seed)
    key = _jax.random.fold_in(_jax.random.key(seed), 7919)
    shape, _ = _output_shape()
    out = dict(inp)
    out["loss_target"] = _jax.random.normal(_jax.random.fold_in(key, 0), shape, _jnp.float32)
    for i, name in enumerate(TWIN_WEIGHTS):
        w = inp[name].astype(_jnp.float32)
        if MOMENT_SCALE is None:
            s = _jnp.sqrt(_jnp.mean(_jnp.square(w)) + 1e-30)
        else:
            s = MOMENT_SCALE[name]
        km, kv = _jax.random.split(_jax.random.fold_in(key, i + 1))
        out[name] = w
        out["m_" + name] = s * _jax.random.normal(km, w.shape, _jnp.float32)
        out["v_" + name] = (s * s) * _jax.random.uniform(kv, w.shape, _jnp.float32, 0.5, 1.5)
    if N_MICROBATCH > 1:
        for name, axis in PER_EXAMPLE_BATCH_AXIS.items():
            out[name] = _to_microbatches(out[name], axis)
    return {'x': out['x'], 'c': out['c'], 'ctx': out['ctx'], 'c_ctx': out['c_ctx'], 'w_ada': out['w_ada'], 'b_ada': out['b_ada'], 'ffn1_w_in': out['ffn1_w_in'], 'ffn1_w_out': out['ffn1_w_out'], 'mix_w_in': out['mix_w_in'], 'attn_q_gain': out['attn_q_gain'], 'attn_k_gain': out['attn_k_gain'], 'ret_decay_logit': out['ret_decay_logit'], 'w_proj_attn': out['w_proj_attn'], 'w_proj_ret': out['w_proj_ret'], 'mix_w_out': out['mix_w_out'], 'ffn2_w_in': out['ffn2_w_in'], 'ffn2_w_out': out['ffn2_w_out'], 'final_norm': out['final_norm'], 'loss_target': out['loss_target'], 'm_c_ctx': out['m_c_ctx'], 'm_w_ada': out['m_w_ada'], 'm_b_ada': out['m_b_ada'], 'm_ffn1_w_in': out['m_ffn1_w_in'], 'm_ffn1_w_out': out['m_ffn1_w_out'], 'm_mix_w_in': out['m_mix_w_in'], 'm_attn_q_gain': out['m_attn_q_gain'], 'm_attn_k_gain': out['m_attn_k_gain'], 'm_ret_decay_logit': out['m_ret_decay_logit'], 'm_w_proj_attn': out['m_w_proj_attn'], 'm_w_proj_ret': out['m_w_proj_ret'], 'm_mix_w_out': out['m_mix_w_out'], 'm_ffn2_w_in': out['m_ffn2_w_in'], 'm_ffn2_w_out': out['m_ffn2_w_out'], 'm_final_norm': out['m_final_norm'], 'v_c_ctx': out['v_c_ctx'], 'v_w_ada': out['v_w_ada'], 'v_b_ada': out['v_b_ada'], 'v_ffn1_w_in': out['v_ffn1_w_in'], 'v_ffn1_w_out': out['v_ffn1_w_out'], 'v_mix_w_in': out['v_mix_w_in'], 'v_attn_q_gain': out['v_attn_q_gain'], 'v_attn_k_gain': out['v_attn_k_gain'], 'v_ret_decay_logit': out['v_ret_decay_logit'], 'v_w_proj_attn': out['v_w_proj_attn'], 'v_w_proj_ret': out['v_w_proj_ret'], 'v_mix_w_out': out['v_mix_w_out'], 'v_ffn2_w_in': out['v_ffn2_w_in'], 'v_ffn2_w_out': out['v_ffn2_w_out'], 'v_final_norm': out['v_final_norm']}


def _loss(weights, diff, rest, loss_target):
    with _jax.named_scope("forward"):
        args = {**rest, TWIN_DIFF_INPUT: diff, **{k: w.astype(_WEIGHT_DTYPES[k]) for k, w in weights.items()}}
        y = _forward(args)
    with _jax.named_scope("loss_head"):
        err = _jnp.square(y.astype(_jnp.float32) - loss_target)
        return 0.5 * _jnp.sum(_jnp.mean(err, axis=-1)) if err.ndim else 0.5 * err


def _adamw(w, g, m, v):
    m = ADAM_B1 * m + (1.0 - ADAM_B1) * g
    v = ADAM_B2 * v + (1.0 - ADAM_B2) * _jnp.square(g)
    m_hat = m / (1.0 - ADAM_B1 ** ADAM_STEP)
    v_hat = v / (1.0 - ADAM_B2 ** ADAM_STEP)
    delta = -ADAM_LR * (m_hat / (_jnp.sqrt(v_hat) + ADAM_EPS) + ADAM_WD * w)
    return delta, m, v


def reference(x, c, ctx, c_ctx, w_ada, b_ada, ffn1_w_in, ffn1_w_out, mix_w_in, attn_q_gain, attn_k_gain, ret_decay_logit, w_proj_attn, w_proj_ret, mix_w_out, ffn2_w_in, ffn2_w_out, final_norm, loss_target, m_c_ctx, m_w_ada, m_b_ada, m_ffn1_w_in, m_ffn1_w_out, m_mix_w_in, m_attn_q_gain, m_attn_k_gain, m_ret_decay_logit, m_w_proj_attn, m_w_proj_ret, m_mix_w_out, m_ffn2_w_in, m_ffn2_w_out, m_final_norm, v_c_ctx, v_w_ada, v_b_ada, v_ffn1_w_in, v_ffn1_w_out, v_mix_w_in, v_attn_q_gain, v_attn_k_gain, v_ret_decay_logit, v_w_proj_attn, v_w_proj_ret, v_mix_w_out, v_ffn2_w_in, v_ffn2_w_out, v_final_norm):
    given = dict(x=x, c=c, ctx=ctx, c_ctx=c_ctx, w_ada=w_ada, b_ada=b_ada, ffn1_w_in=ffn1_w_in, ffn1_w_out=ffn1_w_out, mix_w_in=mix_w_in, attn_q_gain=attn_q_gain, attn_k_gain=attn_k_gain, ret_decay_logit=ret_decay_logit, w_proj_attn=w_proj_attn, w_proj_ret=w_proj_ret, mix_w_out=mix_w_out, ffn2_w_in=ffn2_w_in, ffn2_w_out=ffn2_w_out, final_norm=final_norm, loss_target=loss_target, m_c_ctx=m_c_ctx, m_w_ada=m_w_ada, m_b_ada=m_b_ada, m_ffn1_w_in=m_ffn1_w_in, m_ffn1_w_out=m_ffn1_w_out, m_mix_w_in=m_mix_w_in, m_attn_q_gain=m_attn_q_gain, m_attn_k_gain=m_attn_k_gain, m_ret_decay_logit=m_ret_decay_logit, m_w_proj_attn=m_w_proj_attn, m_w_proj_ret=m_w_proj_ret, m_mix_w_out=m_mix_w_out, m_ffn2_w_in=m_ffn2_w_in, m_ffn2_w_out=m_ffn2_w_out, m_final_norm=m_final_norm, v_c_ctx=v_c_ctx, v_w_ada=v_w_ada, v_b_ada=v_b_ada, v_ffn1_w_in=v_ffn1_w_in, v_ffn1_w_out=v_ffn1_w_out, v_mix_w_in=v_mix_w_in, v_attn_q_gain=v_attn_q_gain, v_attn_k_gain=v_attn_k_gain, v_ret_decay_logit=v_ret_decay_logit, v_w_proj_attn=v_w_proj_attn, v_w_proj_ret=v_w_proj_ret, v_mix_w_out=v_mix_w_out, v_ffn2_w_in=v_ffn2_w_in, v_ffn2_w_out=v_ffn2_w_out, v_final_norm=v_final_norm)
    weights = {n: given[n] for n in TWIN_WEIGHTS}
    shared = {n: given[n] for n in SHARED_INPUTS}
    per_example = {n: given[n] for n in ['x', 'c', 'ctx']}
    grad_fn = _jax.value_and_grad(_loss, argnums=(0, 1))

    def one_microbatch(ex, loss_target):
        ex = dict(ex)
        diff = ex.pop(TWIN_DIFF_INPUT)
        return grad_fn(weights, diff, {**shared, **ex}, loss_target)

    if N_MICROBATCH == 1:
        loss, (grad_w, grad_x) = one_microbatch(per_example, given["loss_target"])
    else:
        def body(carry, xs):
            loss_sum, grad_sum = carry
            l_k, (gw_k, gx_k) = one_microbatch(xs[0], xs[1])
            with _jax.named_scope("update"):
                return (loss_sum + l_k, _jax.tree.map(_jnp.add, grad_sum, gw_k)), gx_k

        init = (_jnp.zeros((), _jnp.float32), _jax.tree.map(_jnp.zeros_like, weights))
        (loss, grad_w), grad_x = _jax.lax.scan(body, init, (per_example, given["loss_target"]))
    with _jax.named_scope("update"):
        delta_w, new_m, new_v = {}, {}, {}
        for n in TWIN_WEIGHTS:
            delta_w[n], new_m[n], new_v[n] = _adamw(weights[n], grad_w[n], given["m_" + n], given["v_" + n])
    return (loss, grad_x, *[grad_w[n] for n in TWIN_WEIGHTS], *[delta_w[n] for n in TWIN_WEIGHTS],
            *[new_m[n] for n in TWIN_WEIGHTS], *[new_v[n] for n in TWIN_WEIGHTS])
```

```python
import math

import jax
import jax.numpy as jnp
from jax import lax
from jax.experimental import pallas as pl
from jax.experimental.pallas import tpu as pltpu

F32 = jnp.float32
BF16 = jnp.bfloat16
MESH = pl.DeviceIdType.MESH

N_DEV = 8
HEAD_DIM = 128
GRID_W = 64
ROPE_THETA = 10000.0
NORM_EPS = 1e-6
RET_CHUNK = 128
N_MOD = 9
LANES = 128
SUBLANES = 8
V7X_VMEM_BYTES = 64 * 1024 * 1024
VMEM_LIMIT = V7X_VMEM_BYTES - 8 * 1024 * 1024

ADAM_LR = 0.001
ADAM_B1 = 0.9
ADAM_B2 = 0.999
ADAM_EPS = 1e-08
ADAM_WD = 0.01
ADAM_STEP = 10

NN = (((1,), (0,)), ((), ()))
NT = (((1,), (1,)), ((), ()))
TN = (((0,), (0,)), ((), ()))


def _tile(n, pref, align=LANES):
    best = None
    t = align
    while t <= min(n, pref):
        if n % t == 0:
            best = t
        t += align
    return n if best is None else best


def _cp(sem):
    return pltpu.CompilerParams(dimension_semantics=sem, vmem_limit_bytes=VMEM_LIMIT)


def _sigmoid(v):
    return 1.0 / (1.0 + jnp.exp(-v))


def _dot(a, b, dims):
    return lax.dot_general(a, b, dims, preferred_element_type=F32)


def _mm(name, a, a_spec, b_list, dims, grid, out_shapes, out_specs, acc_shape, epi, extras=()):
    nb, ne, no = len(b_list), len(extras), len(out_shapes)
    nk = grid[2]

    def body(*refs):
        a_ref = refs[0]
        b_refs = refs[1:1 + nb]
        e_refs = refs[1 + nb:1 + nb + ne]
        o_refs = refs[1 + nb + ne:1 + nb + ne + no]
        accs = refs[1 + nb + ne + no:]
        k = pl.program_id(2)

        @pl.when(k == 0)
        def _():
            for acc in accs:
                acc[...] = jnp.zeros(acc.shape, F32)

        av = a_ref[...]
        for b_ref, acc in zip(b_refs, accs):
            acc[...] += _dot(av, b_ref[...], dims)

        @pl.when(k == nk - 1)
        def _():
            vals = epi([acc[...] for acc in accs], e_refs)
            for o_ref, v in zip(o_refs, vals):
                if isinstance(v, tuple):
                    for idx, part in enumerate(v):
                        o_ref[idx] = part.astype(o_ref.dtype)
                else:
                    o_ref[...] = v.astype(o_ref.dtype)

    return pl.pallas_call(
        body, name=name, grid=grid,
        in_specs=[a_spec] + [s for _, s in b_list] + [s for _, s in extras],
        out_specs=out_specs, out_shape=out_shapes,
        scratch_shapes=[pltpu.VMEM(acc_shape, F32)] * nb,
        compiler_params=_cp(("parallel", "parallel", "arbitrary")),
    )(a, *[b for b, _ in b_list], *[e for e, _ in extras])


def _plain(accs, _):
    return (accs[0],)


def _mm_nn(name, a, b, out_dtype, tm_pref=1024, tn_pref=1024, tk_pref=512):
    m, k = a.shape
    n = b.shape[1]
    tm, tn, tk = _tile(m, tm_pref), _tile(n, tn_pref), _tile(k, tk_pref)
    return _mm(name, a, pl.BlockSpec((tm, tk), lambda i, j, kk: (i, kk)),
               [(b, pl.BlockSpec((tk, tn), lambda i, j, kk: (kk, j)))], NN, (m // tm, n // tn, k // tk),
               [jax.ShapeDtypeStruct((m, n), out_dtype)], [pl.BlockSpec((tm, tn), lambda i, j, kk: (i, j))],
               (tm, tn), _plain)[0]


def _mm_nt(name, a, b, out_dtype, tm_pref=1024, tn_pref=1024, tk_pref=512):
    m, k = a.shape
    n = b.shape[0]
    tm, tn, tk = _tile(m, tm_pref), _tile(n, tn_pref), _tile(k, tk_pref)
    return _mm(name, a, pl.BlockSpec((tm, tk), lambda i, j, kk: (i, kk)),
               [(b, pl.BlockSpec((tn, tk), lambda i, j, kk: (j, kk)))], NT, (m // tm, n // tn, k // tk),
               [jax.ShapeDtypeStruct((m, n), out_dtype)], [pl.BlockSpec((tm, tn), lambda i, j, kk: (i, j))],
               (tm, tn), _plain)[0]


def _mm_tn(name, a, b, out_dtype, rows=None, tm_pref=1024, tn_pref=1024, tk_pref=768):
    k = a.shape[0] if rows is None else rows
    m, n = a.shape[1], b.shape[1]
    tm, tn, tk = _tile(m, tm_pref), _tile(n, tn_pref), _tile(k, tk_pref)
    return _mm(name, a, pl.BlockSpec((tk, tm), lambda i, j, kk: (kk, i)),
               [(b, pl.BlockSpec((tk, tn), lambda i, j, kk: (kk, j)))], TN, (m // tm, n // tn, k // tk),
               [jax.ShapeDtypeStruct((m, n), out_dtype)], [pl.BlockSpec((tm, tn), lambda i, j, kk: (i, j))],
               (tm, tn), _plain)[0]


def _ffn_in(name, u, w_in):
    r, d = u.shape
    f = w_in.shape[1] // 2
    tm, tn, tk = _tile(r, 1024), _tile(f, 512), _tile(d, 512)
    nf = f // tn

    def epi(accs, _):
        za, zb = accs
        s = za * _sigmoid(za) * zb
        return (za, zb), s

    return _mm(name, u, pl.BlockSpec((tm, tk), lambda i, j, kk: (i, kk)),
               [(w_in, pl.BlockSpec((tk, tn), lambda i, j, kk: (kk, j))),
                (w_in, pl.BlockSpec((tk, tn), lambda i, j, kk: (kk, j + nf)))],
               NN, (r // tm, nf, d // tk),
               [jax.ShapeDtypeStruct((2, r, f), BF16), jax.ShapeDtypeStruct((r, f), BF16)],
               [pl.BlockSpec((2, tm, tn), lambda i, j, kk: (0, i, j)), pl.BlockSpec((tm, tn), lambda i, j, kk: (i, j))],
               (tm, tn), epi)


def _mm_residual(name, a, w, res, gate, gate_scale, lx):
    r, k = a.shape
    n = w.shape[1]
    tm, tn, tk = _tile(r, 1024), _tile(n, 1024), _tile(k, 512)

    def epi(accs, e_refs):
        res_ref, g_ref = e_refs
        rows = pl.program_id(0) * tm + lax.broadcasted_iota(jnp.int32, (tm, 1), 0)
        g = jnp.where(rows < lx, g_ref[1], g_ref[0])
        return res_ref[...] + gate_scale * g * accs[0], accs[0]

    return _mm(name, a, pl.BlockSpec((tm, tk), lambda i, j, kk: (i, kk)),
               [(w, pl.BlockSpec((tk, tn), lambda i, j, kk: (kk, j)))], NN, (r // tm, n // tn, k // tk),
               [jax.ShapeDtypeStruct((r, n), F32), jax.ShapeDtypeStruct((r, n), BF16)],
               [pl.BlockSpec((tm, tn), lambda i, j, kk: (i, j))] * 2, (tm, tn), epi,
               extras=[(res, pl.BlockSpec((tm, tn), lambda i, j, kk: (i, j))),
                       (gate, pl.BlockSpec((2, 1, tn), lambda i, j, kk: (0, 0, j)))])


def _ffn_out_bwd(name, df, w_out, z):
    r, d = df.shape
    f = w_out.shape[0]
    tm, tn, tk = _tile(r, 1024), _tile(f, 512), _tile(d, 512)

    def epi(accs, e_refs):
        ds = accs[0]
        za = e_refs[0][0].astype(F32)
        zb = e_refs[0][1].astype(F32)
        sg = _sigmoid(za)
        da = ds * zb * sg * (1.0 + za * (1.0 - sg))
        db = ds * za * sg
        return ((da, db),)

    zspec = pl.BlockSpec((2, tm, tn), lambda i, j, kk: (0, i, j))
    return _mm(name, df, pl.BlockSpec((tm, tk), lambda i, j, kk: (i, kk)),
               [(w_out, pl.BlockSpec((tn, tk), lambda i, j, kk: (j, kk)))], NT, (r // tm, f // tn, d // tk),
               [jax.ShapeDtypeStruct((2, r, f), BF16)], [zspec], (tm, tn), epi, extras=[(z, zspec)])[0]


def _ffn_in_bwd_x(name, dz, w_in):
    _, r, f = dz.shape
    d = w_in.shape[0]
    tm, tn, tk = _tile(r, 1024), _tile(d, 1024), _tile(f, 512)
    nkf = f // tk
    return _mm(name, dz, pl.BlockSpec((None, tm, tk), lambda i, j, kk: (kk // nkf, i, kk % nkf)),
               [(w_in, pl.BlockSpec((tn, tk), lambda i, j, kk: (j, kk)))], NT, (r // tm, d // tn, 2 * nkf),
               [jax.ShapeDtypeStruct((r, d), F32)], [pl.BlockSpec((tm, tn), lambda i, j, kk: (i, j))],
               (tm, tn), _plain)[0]


def _ffn_in_bwd_w(name, u, dz):
    r, d = u.shape
    f = dz.shape[2]
    tm, tn, tk = _tile(d, 1024), _tile(f, 512), _tile(r, 768)
    nf = f // tn
    return _mm(name, u, pl.BlockSpec((tk, tm), lambda i, j, kk: (kk, i)),
               [(dz, pl.BlockSpec((None, tk, tn), lambda i, j, kk: (j // nf, kk, j % nf)))], TN,
               (d // tm, 2 * nf, r // tk),
               [jax.ShapeDtypeStruct((d, 2 * f), BF16)], [pl.BlockSpec((tm, tn), lambda i, j, kk: (i, j))],
               (tm, tn), _plain)[0]


def _merge_bwd(name, dout, w_out, pa, pr, p, ga_off, gb_off):
    r, d = dout.shape
    n = w_out.shape[0]
    cw = math.gcd(math.gcd(ga_off, gb_off), n)
    tm, tn, tk = _tile(r, 1024), _tile(cw, 512), _tile(d, 512)

    def epi(accs, e_refs):
        dm = accs[0]
        pa_ref, pr_ref, ga_ref, gb_ref = e_refs
        sa = _sigmoid(ga_ref[...].astype(F32))
        sb = _sigmoid(gb_ref[...].astype(F32))
        pav = pa_ref[...].astype(F32)
        prv = pr_ref[...].astype(F32)
        return dm * sa, dm * sb, dm * pav * sa * (1.0 - sa), dm * prv * sb * (1.0 - sb)

    o_spec = pl.BlockSpec((tm, tn), lambda i, j, kk: (i, j))
    return _mm(name, dout, pl.BlockSpec((tm, tk), lambda i, j, kk: (i, kk)),
               [(w_out, pl.BlockSpec((tn, tk), lambda i, j, kk: (j, kk)))], NT, (r // tm, n // tn, d // tk),
               [jax.ShapeDtypeStruct((r, n), BF16)] * 4, [o_spec] * 4, (tm, tn), epi,
               extras=[(pa, o_spec), (pr, o_spec),
                       (p, pl.BlockSpec((tm, tn), lambda i, j, kk: (i, ga_off // tn + j))),
                       (p, pl.BlockSpec((tm, tn), lambda i, j, kk: (i, gb_off // tn + j)))])


def _row_tile(r, lx, d):
    pref = 256 if d > 1024 else 512
    return _tile(math.gcd(r, lx), pref, SUBLANES)


def _rmsmod(name, h, scale, shift, lx, rows=None):
    r = h.shape[0] if rows is None else rows
    d = h.shape[1]
    tr = _row_tile(r, lx, d)
    nx = lx // tr
    cls = lambda i: (jnp.where(i < nx, 1, 0), 0, 0)

    def body(h_ref, sc_ref, sh_ref, u_ref):
        hv = h_ref[...]
        rinv = lax.rsqrt(jnp.mean(hv * hv, axis=-1, keepdims=True) + NORM_EPS)
        u_ref[...] = (hv * rinv * (1.0 + sc_ref[...]) + sh_ref[...]).astype(u_ref.dtype)

    return pl.pallas_call(
        body, name=name, grid=(r // tr,),
        in_specs=[pl.BlockSpec((tr, d), lambda i: (i, 0)), pl.BlockSpec((None, 1, d), cls), pl.BlockSpec((None, 1, d), cls)],
        out_specs=pl.BlockSpec((tr, d), lambda i: (i, 0)), out_shape=jax.ShapeDtypeStruct((r, d), BF16),
        compiler_params=_cp(("parallel",)))(h, scale, shift)


def _rmsmod_bwd(name, du, h, scale, dh_in, lx, rows_out):
    r, d = du.shape
    rin = dh_in.shape[0]
    tr = _row_tile(math.gcd(r, math.gcd(rin, rows_out)), lx, d)
    nx, nin, nout = lx // tr, rin // tr, rows_out // tr
    cls = lambda i: (jnp.where(i < nx, 1, 0), 0, 0)

    def body(du_ref, h_ref, sc_ref, dhin_ref, dh_ref, dsc_ref, dsh_ref):
        i = pl.program_id(0)
        hv = h_ref[...]
        duv = du_ref[...]
        rinv = lax.rsqrt(jnp.mean(hv * hv, axis=-1, keepdims=True) + NORM_EPS)
        nv = hv * rinv
        dn = duv * (1.0 + sc_ref[...])

        @pl.when(jnp.logical_or(i == 0, i == nx))
        def _():
            dsc_ref[...] = jnp.zeros(dsc_ref.shape, F32)
            dsh_ref[...] = jnp.zeros(dsh_ref.shape, F32)

        dsc_ref[...] += jnp.sum(duv * nv, axis=0, keepdims=True)
        dsh_ref[...] += jnp.sum(duv, axis=0, keepdims=True)

        @pl.when(i < nout)
        def _():
            dh = rinv * (dn - nv * jnp.mean(dn * nv, axis=-1, keepdims=True))
            dh_ref[...] = dh + jnp.where(i < nin, dhin_ref[...], 0.0)

    return pl.pallas_call(
        body, name=name, grid=(r // tr,),
        in_specs=[pl.BlockSpec((tr, d), lambda i: (i, 0)), pl.BlockSpec((tr, d), lambda i: (i, 0)),
                  pl.BlockSpec((None, 1, d), cls), pl.BlockSpec((tr, d), lambda i: (jnp.minimum(i, nin - 1), 0))],
        out_specs=[pl.BlockSpec((tr, d), lambda i: (jnp.minimum(i, nout - 1), 0)),
                   pl.BlockSpec((None, 1, d), cls), pl.BlockSpec((None, 1, d), cls)],
        out_shape=[jax.ShapeDtypeStruct((rows_out, d), F32), jax.ShapeDtypeStruct((2, 1, d), F32),
                   jax.ShapeDtypeStruct((2, 1, d), F32)],
        compiler_params=_cp(("arbitrary",)))(du, h, scale, dh_in)


def _gate_bwd(name, dh, f, gate, gate_scale, lx):
    r, d = dh.shape
    tr = _row_tile(r, lx, d)
    nx = lx // tr
    cls = lambda i: (jnp.where(i < nx, 1, 0), 0, 0)

    def body(dh_ref, f_ref, g_ref, df_ref, dg_ref):
        i = pl.program_id(0)
        dhv = dh_ref[...]

        @pl.when(jnp.logical_or(i == 0, i == nx))
        def _():
            dg_ref[...] = jnp.zeros(dg_ref.shape, F32)

        df_ref[...] = (gate_scale * g_ref[...] * dhv).astype(df_ref.dtype)
        dg_ref[...] += jnp.sum(gate_scale * f_ref[...].astype(F32) * dhv, axis=0, keepdims=True)

    return pl.pallas_call(
        body, name=name, grid=(r // tr,),
        in_specs=[pl.BlockSpec((tr, d), lambda i: (i, 0)), pl.BlockSpec((tr, d), lambda i: (i, 0)),
                  pl.BlockSpec((None, 1, d), cls)],
        out_specs=[pl.BlockSpec((tr, d), lambda i: (i, 0)), pl.BlockSpec((None, 1, d), cls)],
        out_shape=[jax.ShapeDtypeStruct((r, d), BF16), jax.ShapeDtypeStruct((2, 1, d), F32)],
        compiler_params=_cp(("arbitrary",)))(dh, f, gate)


def _final_loss(name, h, final_norm, target):
    r, d = h.shape
    tr = _row_tile(r, r, d)

    def body(h_ref, fn_ref, t_ref, dh_ref, dfn_ref, loss_ref):
        i = pl.program_id(0)
        hv = h_ref[...]
        rinv = lax.rsqrt(jnp.mean(hv * hv, axis=-1, keepdims=True) + NORM_EPS)
        nv = hv * rinv
        fn = fn_ref[...]
        err = nv * fn - t_ref[...]
        dy = err * (1.0 / d)

        @pl.when(i == 0)
        def _():
            dfn_ref[...] = jnp.zeros(dfn_ref.shape, F32)
            loss_ref[...] = jnp.zeros(loss_ref.shape, F32)

        loss_ref[...] += 0.5 * jnp.sum(jnp.mean(err * err, axis=-1, keepdims=True), axis=0, keepdims=True)
        dfn_ref[...] += jnp.sum(dy * nv, axis=0, keepdims=True)
        dn = dy * fn
        dh_ref[...] = rinv * (dn - nv * jnp.mean(dn * nv, axis=-1, keepdims=True))

    return pl.pallas_call(
        body, name=name, grid=(r // tr,),
        in_specs=[pl.BlockSpec((tr, d), lambda i: (i, 0)), pl.BlockSpec((1, d), lambda i: (0, 0)),
                  pl.BlockSpec((tr, d), lambda i: (i, 0))],
        out_specs=[pl.BlockSpec((tr, d), lambda i: (i, 0)), pl.BlockSpec((1, d), lambda i: (0, 0)),
                   pl.BlockSpec((SUBLANES, LANES), lambda i: (0, 0))],
        out_shape=[jax.ShapeDtypeStruct((r, d), F32), jax.ShapeDtypeStruct((1, d), F32),
                   jax.ShapeDtypeStruct((SUBLANES, LANES), F32)],
        compiler_params=_cp(("arbitrary",)))(h, final_norm, target)


def _swap_pairs(t):
    lane = lax.broadcasted_iota(jnp.int32, t.shape, 1)
    nxt = pltpu.roll(t, HEAD_DIM - 1, 1)
    prv = pltpu.roll(t, 1, 1)
    return jnp.where(lane % 2 == 0, nxt, prv)


def _qk_prep(name, p, cosf, sinf, q_gain, k_gain, aw, kw):
    t = p.shape[0]
    tr = _tile(t, 256, SUBLANES)

    def head_fwd(v, gain, cs, sn):
        v = v.astype(F32)
        rinv = lax.rsqrt(jnp.mean(v * v, axis=-1, keepdims=True) + NORM_EPS)
        tt = v * rinv * gain
        return tt * cs + _swap_pairs(tt) * sn

    def body(q_ref, k_ref, cos_ref, sin_ref, qg_ref, kg_ref, qo_ref, ko_ref):
        cs, sn = cos_ref[...], sin_ref[...]
        for hh in range(aw // HEAD_DIM):
            sl = slice(hh * HEAD_DIM, (hh + 1) * HEAD_DIM)
            qo_ref[:, sl] = head_fwd(q_ref[:, sl], qg_ref[...], cs, sn).astype(qo_ref.dtype)
        for hh in range(kw // HEAD_DIM):
            sl = slice(hh * HEAD_DIM, (hh + 1) * HEAD_DIM)
            ko_ref[:, sl] = head_fwd(k_ref[:, sl], kg_ref[...], cs, sn).astype(ko_ref.dtype)

    assert aw % kw == 0
    row = lambda i: (i, 0)
    return pl.pallas_call(
        body, name=name, grid=(t // tr,),
        in_specs=[pl.BlockSpec((tr, aw), row), pl.BlockSpec((tr, kw), lambda i: (i, aw // kw)),
                  pl.BlockSpec((tr, HEAD_DIM), row), pl.BlockSpec((tr, HEAD_DIM), row),
                  pl.BlockSpec((1, HEAD_DIM), lambda i: (0, 0)), pl.BlockSpec((1, HEAD_DIM), lambda i: (0, 0))],
        out_specs=[pl.BlockSpec((tr, aw), row), pl.BlockSpec((tr, kw), row)],
        out_shape=[jax.ShapeDtypeStruct((t, aw), BF16), jax.ShapeDtypeStruct((t, kw), BF16)],
        compiler_params=_cp(("parallel",)))(p, p, cosf, sinf, q_gain, k_gain)


def _qk_prep_bwd(name, dq_rot, dk_rot, p, cosf, sinf, q_gain, k_gain, aw, kw):
    t = p.shape[0]
    lq = dq_rot.shape[0]
    tr = _tile(math.gcd(t, lq), 256, SUBLANES)
    nq = lq // tr

    def head_bwd(dout, v, gain, cs, sn):
        v = v.astype(F32)
        rinv = lax.rsqrt(jnp.mean(v * v, axis=-1, keepdims=True) + NORM_EPS)
        vn = v * rinv
        dt = dout * cs - _swap_pairs(dout) * sn
        dvn = dt * gain
        dv = rinv * (dvn - vn * jnp.mean(dvn * vn, axis=-1, keepdims=True))
        return dv, jnp.sum(dt * vn, axis=0, keepdims=True)

    def body(dq_ref, dk_ref, q_ref, k_ref, cos_ref, sin_ref, qg_ref, kg_ref, dqo_ref, dko_ref, dqg_ref, dkg_ref):
        i = pl.program_id(0)
        cs, sn = cos_ref[...], sin_ref[...]

        @pl.when(i == 0)
        def _():
            dqg_ref[...] = jnp.zeros(dqg_ref.shape, F32)
            dkg_ref[...] = jnp.zeros(dkg_ref.shape, F32)

        has_q = i < nq
        for hh in range(aw // HEAD_DIM):
            sl = slice(hh * HEAD_DIM, (hh + 1) * HEAD_DIM)
            dout = jnp.where(has_q, dq_ref[:, sl], 0.0)
            dv, dg = head_bwd(dout, q_ref[:, sl], qg_ref[...], cs, sn)
            dqo_ref[:, sl] = dv.astype(dqo_ref.dtype)
            dqg_ref[...] += dg
        for hh in range(kw // HEAD_DIM):
            sl = slice(hh * HEAD_DIM, (hh + 1) * HEAD_DIM)
            dv, dg = head_bwd(dk_ref[:, sl], k_ref[:, sl], kg_ref[...], cs, sn)
            dko_ref[:, sl] = dv.astype(dko_ref.dtype)
            dkg_ref[...] += dg

    row = lambda i: (i, 0)
    one = lambda i: (0, 0)
    return pl.pallas_call(
        body, name=name, grid=(t // tr,),
        in_specs=[pl.BlockSpec((tr, aw), lambda i: (jnp.minimum(i, nq - 1), 0)), pl.BlockSpec((tr, kw), row),
                  pl.BlockSpec((tr, aw), row), pl.BlockSpec((tr, kw), lambda i: (i, aw // kw)),
                  pl.BlockSpec((tr, HEAD_DIM), row), pl.BlockSpec((tr, HEAD_DIM), row),
                  pl.BlockSpec((1, HEAD_DIM), one), pl.BlockSpec((1, HEAD_DIM), one)],
        out_specs=[pl.BlockSpec((tr, aw), row), pl.BlockSpec((tr, kw), row),
                   pl.BlockSpec((1, HEAD_DIM), one), pl.BlockSpec((1, HEAD_DIM), one)],
        out_shape=[jax.ShapeDtypeStruct((t, aw), BF16), jax.ShapeDtypeStruct((t, kw), BF16),
                   jax.ShapeDtypeStruct((1, HEAD_DIM), F32), jax.ShapeDtypeStruct((1, HEAD_DIM), F32)],
        compiler_params=_cp(("arbitrary",)))(dq_rot, dk_rot, p, p, cosf, sinf, q_gain, k_gain)


def _attn_tiles(lq, t):
    return _tile(lq, 512), _tile(t, 768)


def _attn_fwd(name, q, k, p, v_off, lq, groups):
    t = k.shape[0]
    hq = q.shape[1] // HEAD_DIM
    tq, tk = _attn_tiles(lq, t)
    nkv = t // tk
    scale = HEAD_DIM ** -0.5
    vb = v_off // HEAD_DIM

    def body(q_ref, k_ref, v_ref, o_ref, lse_ref, m_sc, l_sc, acc_sc):
        j = pl.program_id(2)

        @pl.when(j == 0)
        def _():
            m_sc[...] = jnp.full(m_sc.shape, -jnp.inf, F32)
            l_sc[...] = jnp.zeros(l_sc.shape, F32)
            acc_sc[...] = jnp.zeros(acc_sc.shape, F32)

        s = _dot(q_ref[...], k_ref[...], NT) * scale
        m_prev = m_sc[...]
        m_new = jnp.maximum(m_prev, jnp.max(s, axis=-1, keepdims=True))
        alpha = jnp.exp(m_prev - m_new)
        pexp = jnp.exp(s - m_new)
        l_sc[...] = alpha * l_sc[...] + jnp.sum(pexp, axis=-1, keepdims=True)
        acc_sc[...] = alpha * acc_sc[...] + _dot(pexp.astype(BF16), v_ref[...], NN)
        m_sc[...] = m_new

        @pl.when(j == nkv - 1)
        def _():
            o_ref[...] = (acc_sc[...] * (1.0 / l_sc[...])).astype(o_ref.dtype)
            lse_ref[...] = jnp.broadcast_to(m_sc[...] + jnp.log(l_sc[...]), lse_ref.shape)

    return pl.pallas_call(
        body, name=name, grid=(hq, lq // tq, nkv),
        in_specs=[pl.BlockSpec((tq, HEAD_DIM), lambda h, i, j: (i, h)),
                  pl.BlockSpec((tk, HEAD_DIM), lambda h, i, j: (j, h // groups)),
                  pl.BlockSpec((tk, HEAD_DIM), lambda h, i, j: (j, vb + h // groups))],
        out_specs=[pl.BlockSpec((tq, HEAD_DIM), lambda h, i, j: (i, h)),
                   pl.BlockSpec((None, tq, LANES), lambda h, i, j: (h, i, 0))],
        out_shape=[jax.ShapeDtypeStruct((lq, hq * HEAD_DIM), BF16), jax.ShapeDtypeStruct((hq, lq, LANES), F32)],
        scratch_shapes=[pltpu.VMEM((tq, 1), F32), pltpu.VMEM((tq, 1), F32), pltpu.VMEM((tq, HEAD_DIM), F32)],
        compiler_params=_cp(("parallel", "parallel", "arbitrary")))(q, k, p)


def _attn_bwd_q(name, q, k, p, v_off, o, do, lse, lq, groups):
    t = k.shape[0]
    hq = q.shape[1] // HEAD_DIM
    tq, tk = _attn_tiles(lq, t)
    nkv = t // tk
    scale = HEAD_DIM ** -0.5
    vb = v_off // HEAD_DIM

    def body(q_ref, k_ref, v_ref, o_ref, do_ref, lse_ref, dq_ref, delta_sc, acc_sc):
        j = pl.program_id(2)

        @pl.when(j == 0)
        def _():
            delta_sc[...] = jnp.sum(do_ref[...].astype(F32) * o_ref[...].astype(F32), axis=-1, keepdims=True)
            acc_sc[...] = jnp.zeros(acc_sc.shape, F32)

        kv = k_ref[...]
        s = _dot(q_ref[...], kv, NT) * scale
        pexp = jnp.exp(s - lse_ref[:, 0:1])
        dp = _dot(do_ref[...], v_ref[...], NT)
        ds = pexp * (dp - delta_sc[...]) * scale
        acc_sc[...] += _dot(ds.astype(BF16), kv, NN)

        @pl.when(j == nkv - 1)
        def _():
            dq_ref[...] = acc_sc[...]

    qspec = pl.BlockSpec((tq, HEAD_DIM), lambda h, i, j: (i, h))
    return pl.pallas_call(
        body, name=name, grid=(hq, lq // tq, nkv),
        in_specs=[qspec, pl.BlockSpec((tk, HEAD_DIM), lambda h, i, j: (j, h // groups)),
                  pl.BlockSpec((tk, HEAD_DIM), lambda h, i, j: (j, vb + h // groups)), qspec, qspec,
                  pl.BlockSpec((None, tq, LANES), lambda h, i, j: (h, i, 0))],
        out_specs=qspec, out_shape=jax.ShapeDtypeStruct((lq, hq * HEAD_DIM), F32),
        scratch_shapes=[pltpu.VMEM((tq, 1), F32), pltpu.VMEM((tq, HEAD_DIM), F32)],
        compiler_params=_cp(("parallel", "parallel", "arbitrary")))(q, k, p, o, do, lse)


def _attn_bwd_kv(name, q, k, p, v_off, o, do, lse, lq, groups):
    t = k.shape[0]
    hkv = k.shape[1] // HEAD_DIM
    tq, tk = _attn_tiles(lq, t)
    nq = lq // tq
    scale = HEAD_DIM ** -0.5
    vb = v_off // HEAD_DIM

    def body(q_ref, k_ref, v_ref, o_ref, do_ref, lse_ref, dk_ref, dv_ref, dk_sc, dv_sc):
        g, i = pl.program_id(2), pl.program_id(3)

        @pl.when(jnp.logical_and(g == 0, i == 0))
        def _():
            dk_sc[...] = jnp.zeros(dk_sc.shape, F32)
            dv_sc[...] = jnp.zeros(dv_sc.shape, F32)

        qv, dov = q_ref[...], do_ref[...]
        delta = jnp.sum(dov.astype(F32) * o_ref[...].astype(F32), axis=-1, keepdims=True)
        s = _dot(qv, k_ref[...], NT) * scale
        pexp = jnp.exp(s - lse_ref[:, 0:1])
        dv_sc[...] += _dot(pexp.astype(BF16), dov, TN)
        dp = _dot(dov, v_ref[...], NT)
        ds = pexp * (dp - delta) * scale
        dk_sc[...] += _dot(ds.astype(BF16), qv, TN)

        @pl.when(jnp.logical_and(g == groups - 1, i == nq - 1))
        def _():
            dk_ref[...] = dk_sc[...]
            dv_ref[...] = dv_sc[...].astype(dv_ref.dtype)

    qspec = pl.BlockSpec((tq, HEAD_DIM), lambda kh, j, g, i: (i, kh * groups + g))
    kspec = pl.BlockSpec((tk, HEAD_DIM), lambda kh, j, g, i: (j, kh))
    return pl.pallas_call(
        body, name=name, grid=(hkv, t // tk, groups, nq),
        in_specs=[qspec, kspec, pl.BlockSpec((tk, HEAD_DIM), lambda kh, j, g, i: (j, vb + kh)), qspec, qspec,
                  pl.BlockSpec((None, tq, LANES), lambda kh, j, g, i: (kh * groups + g, i, 0))],
        out_specs=[kspec, kspec],
        out_shape=[jax.ShapeDtypeStruct((t, hkv * HEAD_DIM), F32), jax.ShapeDtypeStruct((t, hkv * HEAD_DIM), BF16)],
        scratch_shapes=[pltpu.VMEM((tk, HEAD_DIM), F32), pltpu.VMEM((tk, HEAD_DIM), F32)],
        compiler_params=_cp(("parallel", "parallel", "arbitrary", "arbitrary")))(q, k, p, o, do, lse)


def _ret_tables(log_gamma, direction):
    c = RET_CHUNK
    idx = jnp.arange(c, dtype=F32)
    diff = idx[:, None] - idx[None, :]
    if direction == 1:
        diff = -diff
    keep = diff >= 0
    lg = log_gamma.astype(F32)
    mask = jnp.where(keep[None], jnp.exp(jnp.where(keep, diff, 0.0)[None] * lg[:, None, None]), 0.0)
    q_exp = idx + 1.0 if direction == 0 else c - idx
    k_exp = c - 1.0 - idx if direction == 0 else idx
    sign = 1.0 if direction == 0 else -1.0
    lane = lambda v: jnp.broadcast_to(v[..., None], v.shape + (LANES,))
    qdec = lane(jnp.exp(q_exp[None, :] * lg[:, None]))
    kdec = lane(jnp.exp(k_exp[None, :] * lg[:, None]))
    cdec = jnp.broadcast_to(jnp.exp(c * lg)[:, None, None], (lg.shape[0], SUBLANES, LANES))
    weights = lane(jnp.stack([sign * idx, q_exp, -sign * idx, k_exp], axis=0))
    return mask, qdec, kdec, cdec, weights


def _ret_chunk_of(direction, step, nx, nc):
    if direction == 0:
        return jnp.where(step < nc, nx + step, step - nc)
    return jnp.where(step < nc, nx + nc - 1 - step, nx + nc - 1 - step)


def _ret_fwd(name, p, offs, tables, direction, lx, prev):
    q_off, k_off, v_off, rw = offs
    t = p.shape[0]
    c = RET_CHUNK
    n_steps, nx = t // c, lx // c
    nc = n_steps - nx
    bw = math.gcd(math.gcd(q_off, k_off), math.gcd(v_off, rw))
    bw = _tile(bw, 512)
    hpb = bw // HEAD_DIM
    heads = rw // HEAD_DIM
    k_scale = HEAD_DIM ** -0.5
    mask, qdec, kdec, cdec, _ = tables
    rc = lambda n: _ret_chunk_of(direction, n, nx, nc)

    def body(*refs):
        if prev is None:
            q_ref, k_ref, v_ref, m_ref, qd_ref, kd_ref, cd_ref, y_ref, st_ref, s_sc = refs
        else:
            q_ref, k_ref, v_ref, m_ref, qd_ref, kd_ref, cd_ref, prev_ref, y_ref, st_ref, s_sc = refs
        n = pl.program_id(1)

        @pl.when(n == 0)
        def _():
            s_sc[...] = jnp.zeros(s_sc.shape, F32)

        for hh in range(hpb):
            sl = slice(hh * HEAD_DIM, (hh + 1) * HEAD_DIM)
            qv = q_ref[:, sl]
            kf = k_ref[:, sl].astype(F32) * k_scale
            kv = kf.astype(BF16)
            vv = v_ref[:, sl]
            state = s_sc[hh]
            st_ref[hh] = state
            a = _dot(qv, kv, NT) * m_ref[hh]
            y = _dot(a.astype(BF16), vv, NN) + _dot(qv, state.astype(BF16), NN) * qd_ref[hh]
            s_sc[hh] = state * cd_ref[hh, 0:1, :] + _dot((kf * kd_ref[hh]).astype(BF16), vv, TN)
            if prev is not None:
                y = y + prev_ref[:, sl]
            y_ref[:, sl] = y

    col = lambda off: (lambda g, n: (rc(n), off // bw + g))
    tab3 = lambda g, n: (g, 0, 0)
    in_specs = [pl.BlockSpec((c, bw), col(q_off)), pl.BlockSpec((c, bw), col(k_off)), pl.BlockSpec((c, bw), col(v_off)),
                pl.BlockSpec((hpb, c, c), tab3), pl.BlockSpec((hpb, c, LANES), tab3), pl.BlockSpec((hpb, c, LANES), tab3),
                pl.BlockSpec((hpb, SUBLANES, LANES), tab3)]
    args = [p, p, p, mask, qdec, kdec, cdec]
    aliases = {}
    yspec = pl.BlockSpec((c, bw), lambda g, n: (rc(n), g))
    if prev is not None:
        in_specs.append(yspec)
        args.append(prev)
        aliases = {len(args) - 1: 0}
    return pl.pallas_call(
        body, name=name, grid=(heads // hpb, n_steps), in_specs=in_specs,
        out_specs=[yspec, pl.BlockSpec((None, hpb, HEAD_DIM, HEAD_DIM), lambda g, n: (n, g, 0, 0))],
        out_shape=[jax.ShapeDtypeStruct((t, rw), F32), jax.ShapeDtypeStruct((n_steps, heads, HEAD_DIM, HEAD_DIM), F32)],
        scratch_shapes=[pltpu.VMEM((hpb, HEAD_DIM, HEAD_DIM), F32)], input_output_aliases=aliases,
        compiler_params=_cp(("parallel", "arbitrary")))(*args)


def _ret_bwd(name, p, offs, tables, states, dy, direction, lx, prev):
    q_off, k_off, v_off, rw = offs
    t = p.shape[0]
    c = RET_CHUNK
    n_steps, nx = t // c, lx // c
    nc = n_steps - nx
    bw = math.gcd(math.gcd(q_off, k_off), math.gcd(v_off, rw))
    bw = _tile(bw, 512)
    hpb = bw // HEAD_DIM
    heads = rw // HEAD_DIM
    k_scale = HEAD_DIM ** -0.5
    mask, qdec, kdec, cdec, weights = tables
    step_of = lambda n: n_steps - 1 - n
    rc = lambda n: _ret_chunk_of(direction, step_of(n), nx, nc)

    def body(*refs):
        if prev is None:
            (q_ref, k_ref, v_ref, dy_ref, st_ref, m_ref, qd_ref, kd_ref, cd_ref, w_ref,
             dq_ref, dk_ref, dv_ref, dl_ref, ds_sc, lam_sc) = refs
        else:
            (q_ref, k_ref, v_ref, dy_ref, st_ref, m_ref, qd_ref, kd_ref, cd_ref, w_ref, pq_ref, pk_ref, pv_ref,
             dq_ref, dk_ref, dv_ref, dl_ref, ds_sc, lam_sc) = refs
        n = pl.program_id(1)

        @pl.when(n == 0)
        def _():
            ds_sc[...] = jnp.zeros(ds_sc.shape, F32)
            lam_sc[...] = jnp.zeros(lam_sc.shape, F32)

        is_x = rc(n) < nx
        for hh in range(hpb):
            sl = slice(hh * HEAD_DIM, (hh + 1) * HEAD_DIM)
            qv = q_ref[:, sl]
            qf = qv.astype(F32)
            kf = k_ref[:, sl].astype(F32) * k_scale
            kv = kf.astype(BF16)
            vv = v_ref[:, sl]
            dyv = jnp.where(is_x, dy_ref[:, sl], 0.0).astype(BF16)
            state = st_ref[hh]
            dstate = ds_sc[hh]
            dstate_b = dstate.astype(BF16)
            msk, qd, kd = m_ref[hh], qd_ref[hh], kd_ref[hh]
            cd = cd_ref[hh, 0:1, :]
            a = _dot(qv, kv, NT) * msk
            da = (_dot(dyv, vv, NT) * msk).astype(BF16)
            dq_intra = _dot(da, kv, NN)
            dk_intra = _dot(da, qv, TN)
            dq_inter = _dot(dyv, state.astype(BF16), NT) * qd
            dk_inter = _dot(vv, dstate_b, NT) * kd
            dv = _dot(a.astype(BF16), dyv, TN) + _dot((kf * kd).astype(BF16), dstate_b, NN)
            lam_sc[hh] += (qf * (w_ref[0] * dq_intra + w_ref[1] * dq_inter)
                           + kf * (w_ref[2] * dk_intra + w_ref[3] * dk_inter)
                           + (c * cd) * state * dstate)
            ds_sc[hh] = _dot((qf * qd).astype(BF16), dyv, TN) + cd * dstate
            dq = dq_intra + dq_inter
            dk = (dk_intra + dk_inter) * k_scale
            if prev is not None:
                dq = dq + pq_ref[:, sl]
                dk = dk + pk_ref[:, sl]
                dv = dv + pv_ref[:, sl]
            dq_ref[:, sl] = dq
            dk_ref[:, sl] = dk
            dv_ref[:, sl] = dv

        @pl.when(n == n_steps - 1)
        def _():
            for hh in range(hpb):
                dl_ref[hh] = jnp.broadcast_to(jnp.sum(lam_sc[hh]), (SUBLANES, LANES))

    col = lambda off: (lambda g, n: (rc(n), off // bw + g))
    tab3 = lambda g, n: (g, 0, 0)
    ospec = pl.BlockSpec((c, bw), lambda g, n: (rc(n), g))
    in_specs = [pl.BlockSpec((c, bw), col(q_off)), pl.BlockSpec((c, bw), col(k_off)), pl.BlockSpec((c, bw), col(v_off)),
                pl.BlockSpec((c, bw), lambda g, n: (jnp.minimum(rc(n), nx - 1), g)),
                pl.BlockSpec((None, hpb, HEAD_DIM, HEAD_DIM), lambda g, n: (step_of(n), g, 0, 0)),
                pl.BlockSpec((hpb, c, c), tab3), pl.BlockSpec((hpb, c, LANES), tab3), pl.BlockSpec((hpb, c, LANES), tab3),
                pl.BlockSpec((hpb, SUBLANES, LANES), tab3), pl.BlockSpec((4, c, LANES), lambda g, n: (0, 0, 0))]
    args = [p, p, p, dy, states, mask, qdec, kdec, cdec, weights]
    aliases = {}
    if prev is not None:
        for k_out, arr in enumerate(prev):
            in_specs.append(ospec)
            args.append(arr)
            aliases[len(args) - 1] = k_out
    big = jax.ShapeDtypeStruct((t, rw), F32)
    return pl.pallas_call(
        body, name=name, grid=(heads // hpb, n_steps), in_specs=in_specs,
        out_specs=[ospec, ospec, ospec, pl.BlockSpec((hpb, SUBLANES, LANES), tab3)],
        out_shape=[big, big, big, jax.ShapeDtypeStruct((heads, SUBLANES, LANES), F32)],
        scratch_shapes=[pltpu.VMEM((hpb, HEAD_DIM, HEAD_DIM), F32), pltpu.VMEM((hpb, HEAD_DIM, HEAD_DIM), F32)],
        input_output_aliases=aliases, compiler_params=_cp(("parallel", "arbitrary")))(*args)


def _ret_out(name, y, p, g_off, lx):
    rw = y.shape[1]
    bw = _tile(math.gcd(g_off, rw), 512)
    tr = _tile(lx, 512, SUBLANES)

    def body(y_ref, g_ref, o_ref):
        for hh in range(bw // HEAD_DIM):
            sl = slice(hh * HEAD_DIM, (hh + 1) * HEAD_DIM)
            yv = y_ref[:, sl]
            gv = g_ref[:, sl].astype(F32)
            rinv = lax.rsqrt(jnp.mean(yv * yv, axis=-1, keepdims=True) + NORM_EPS)
            o_ref[:, sl] = (gv * _sigmoid(gv) * yv * rinv).astype(o_ref.dtype)

    spec = pl.BlockSpec((tr, bw), lambda i, g: (i, g))
    return pl.pallas_call(
        body, name=name, grid=(lx // tr, rw // bw),
        in_specs=[spec, pl.BlockSpec((tr, bw), lambda i, g: (i, g_off // bw + g))],
        out_specs=spec, out_shape=jax.ShapeDtypeStruct((lx, rw), BF16),
        compiler_params=_cp(("parallel", "parallel")))(y, p)


def _ret_out_bwd(name, dyr, y, p, g_off, lx):
    rw = y.shape[1]
    bw = _tile(math.gcd(g_off, rw), 512)
    tr = _tile(lx, 512, SUBLANES)

    def body(d_ref, y_ref, g_ref, dy_ref, dg_ref):
        for hh in range(bw // HEAD_DIM):
            sl = slice(hh * HEAD_DIM, (hh + 1) * HEAD_DIM)
            yv = y_ref[:, sl]
            gv = g_ref[:, sl].astype(F32)
            dv = d_ref[:, sl]
            rinv = lax.rsqrt(jnp.mean(yv * yv, axis=-1, keepdims=True) + NORM_EPS)
            yn = yv * rinv
            sg = _sigmoid(gv)
            dg_ref[:, sl] = (dv * yn * sg * (1.0 + gv * (1.0 - sg))).astype(dg_ref.dtype)
            dyn = dv * gv * sg
            dy_ref[:, sl] = rinv * (dyn - yn * jnp.mean(dyn * yn, axis=-1, keepdims=True))

    spec = pl.BlockSpec((tr, bw), lambda i, g: (i, g))
    return pl.pallas_call(
        body, name=name, grid=(lx // tr, rw // bw),
        in_specs=[spec, spec, pl.BlockSpec((tr, bw), lambda i, g: (i, g_off // bw + g))],
        out_specs=[spec, spec],
        out_shape=[jax.ShapeDtypeStruct((lx, rw), F32), jax.ShapeDtypeStruct((lx, rw), BF16)],
        compiler_params=_cp(("parallel", "parallel")))(dyr, y, p)


def _merge(name, pa, pr, p, ga_off, gb_off):
    r, d = pa.shape
    cw = _tile(math.gcd(math.gcd(ga_off, gb_off), d), 1024)
    tr = _tile(r, 512, SUBLANES)

    def body(pa_ref, pr_ref, ga_ref, gb_ref, o_ref):
        o_ref[...] = (_sigmoid(ga_ref[...].astype(F32)) * pa_ref[...].astype(F32)
                      + _sigmoid(gb_ref[...].astype(F32)) * pr_ref[...].astype(F32)).astype(o_ref.dtype)

    spec = pl.BlockSpec((tr, cw), lambda i, j: (i, j))
    return pl.pallas_call(
        body, name=name, grid=(r // tr, d // cw),
        in_specs=[spec, spec, pl.BlockSpec((tr, cw), lambda i, j: (i, ga_off // cw + j)),
                  pl.BlockSpec((tr, cw), lambda i, j: (i, gb_off // cw + j))],
        out_specs=spec, out_shape=jax.ShapeDtypeStruct((r, d), BF16),
        compiler_params=_cp(("parallel", "parallel")))(pa, pr, p, p)


def _ada_fwd(name, cond, w):
    rows, d = cond.shape
    n = w.shape[1]
    tn = _tile(n, 768)

    def body(c_ref, w_ref, o_ref, s_ref):
        cv = c_ref[...]
        sv = cv * _sigmoid(cv)
        s_ref[...] = sv
        o_ref[...] = _dot(sv.astype(BF16), w_ref[...].astype(BF16), NN)

    return pl.pallas_call(
        body, name=name, grid=(n // tn,),
        in_specs=[pl.BlockSpec((rows, d), lambda j: (0, 0)), pl.BlockSpec((d, tn), lambda j: (0, j))],
        out_specs=[pl.BlockSpec((rows, tn), lambda j: (0, j)), pl.BlockSpec((rows, d), lambda j: (0, 0))],
        out_shape=[jax.ShapeDtypeStruct((rows, n), F32), jax.ShapeDtypeStruct((rows, d), F32)],
        compiler_params=_cp(("arbitrary",)))(cond, w)


def _ada_bwd(name, s_cond, dmod, w):
    rows, d = s_cond.shape
    n = w.shape[1]
    tn = _tile(n, 768)

    def body(s_ref, dm_ref, w_ref, gw_ref, ds_ref):
        j = pl.program_id(0)

        @pl.when(j == 0)
        def _():
            ds_ref[...] = jnp.zeros(ds_ref.shape, F32)

        dmv = dm_ref[...].astype(BF16)
        gw_ref[...] = _dot(s_ref[...].astype(BF16), dmv, TN)
        ds_ref[...] += _dot(dmv, w_ref[...].astype(BF16), NT)

    return pl.pallas_call(
        body, name=name, grid=(n // tn,),
        in_specs=[pl.BlockSpec((rows, d), lambda j: (0, 0)), pl.BlockSpec((rows, tn), lambda j: (0, j)),
                  pl.BlockSpec((d, tn), lambda j: (0, j))],
        out_specs=[pl.BlockSpec((d, tn), lambda j: (0, j)), pl.BlockSpec((rows, d), lambda j: (0, 0))],
        out_shape=[jax.ShapeDtypeStruct((d, n), F32), jax.ShapeDtypeStruct((rows, d), F32)],
        compiler_params=_cp(("arbitrary",)))(s_cond, dmod, w)


def _sum_slots(name, a):
    s, r, c = a.shape

    def body(a_ref, o_ref):
        acc = a_ref[0]
        for k in range(1, s):
            acc = acc + a_ref[k]
        o_ref[...] = acc

    return pl.pallas_call(
        body, name=name, grid=(1,), in_specs=[pl.BlockSpec((s, r, c), lambda i: (0, 0, 0))],
        out_specs=pl.BlockSpec((r, c), lambda i: (0, 0)), out_shape=jax.ShapeDtypeStruct((r, c), F32),
        compiler_params=_cp(("arbitrary",)))(a)


def _cctx_grad(name, parts, c_ctx):
    s, r, d = parts.shape

    def body(p_ref, c_ref, o_ref):
        acc = p_ref[0]
        for k in range(1, s):
            acc = acc + p_ref[k]
        cv = c_ref[...]
        sg = _sigmoid(cv)
        o_ref[...] = acc[0:1, :] * sg * (1.0 + cv * (1.0 - sg))

    return pl.pallas_call(
        body, name=name, grid=(1,),
        in_specs=[pl.BlockSpec((s, r, d), lambda i: (0, 0, 0)), pl.BlockSpec((1, d), lambda i: (0, 0))],
        out_specs=pl.BlockSpec((1, d), lambda i: (0, 0)), out_shape=jax.ShapeDtypeStruct((1, d), F32),
        compiler_params=_cp(("arbitrary",)))(parts, c_ctx)


def _adamw(name, slots, w, m, v):
    s, r, c = slots.shape
    tr = _tile(r, 256 if c > 1024 else 512, SUBLANES)
    c1 = 1.0 - ADAM_B1 ** ADAM_STEP
    c2 = 1.0 - ADAM_B2 ** ADAM_STEP

    def body(s_ref, w_ref, m_ref, v_ref, g_ref, d_ref, mo_ref, vo_ref):
        g = s_ref[0].astype(F32)
        for k in range(1, s):
            g = g + s_ref[k].astype(F32)
        mn = ADAM_B1 * m_ref[...] + (1.0 - ADAM_B1) * g
        vn = ADAM_B2 * v_ref[...] + (1.0 - ADAM_B2) * (g * g)
        m_hat = mn / c1
        v_hat = vn / c2
        g_ref[...] = g
        mo_ref[...] = mn
        vo_ref[...] = vn
        d_ref[...] = -ADAM_LR * (m_hat / (jnp.sqrt(v_hat) + ADAM_EPS) + ADAM_WD * w_ref[...])

    spec = pl.BlockSpec((tr, c), lambda i: (i, 0))
    shp = jax.ShapeDtypeStruct((r, c), F32)
    return pl.pallas_call(
        body, name=name, grid=(r // tr,),
        in_specs=[pl.BlockSpec((s, tr, c), lambda i: (0, i, 0)), spec, spec, spec],
        out_specs=[spec] * 4, out_shape=[shp] * 4, compiler_params=_cp(("parallel",)))(slots, w, m, v)


def _coords():
    return lax.axis_index("x"), lax.axis_index("y"), lax.axis_index("c")


def _flip(coord, bit):
    return 1 - coord if bit else coord


def _all_gather_small(name, blk):
    r, ccols = blk.shape

    def body(x_ref, out_ref, send_sems, recv_sems, local_sem):
        x, y, c = _coords()
        me, sibling = (x, y, c), (x, y, 1 - c)
        chips = [(1 - x, y), (x, 1 - y), (1 - x, 1 - y)]

        def slot(px, py, pc):
            return out_ref.at[4 * px + 2 * py + pc]

        def copy(k, block, to, src=None):
            return pltpu.make_async_remote_copy(
                src_ref=slot(*block) if src is None else src, dst_ref=slot(*block),
                send_sem=send_sems.at[k], recv_sem=recv_sems.at[k], device_id=to, device_id_type=MESH)

        mine = pltpu.make_async_copy(x_ref, slot(*me), local_sem)
        mine.start()
        first = [copy(0, me, sibling, src=x_ref)]
        first += [copy(1 + j, me, (*chip, c), src=x_ref) for j, chip in enumerate(chips)]
        for cp in first:
            cp.start()
        passed = [copy(4 + j, (*chip, c), sibling) for j, chip in enumerate(chips)]
        for j, chip in enumerate(chips):
            copy(1 + j, (*chip, c), me).wait_recv()
            passed[j].start()
        copy(0, sibling, me).wait_recv()
        for j, chip in enumerate(chips):
            copy(4 + j, (*chip, 1 - c), me).wait_recv()
        for cp in first + passed:
            cp.wait_send()
        mine.wait()

    return pl.pallas_call(
        body, name=name, out_shape=jax.ShapeDtypeStruct((N_DEV, r, ccols), blk.dtype),
        in_specs=[pl.BlockSpec(memory_space=pltpu.VMEM)], out_specs=pl.BlockSpec(memory_space=pltpu.VMEM),
        scratch_shapes=[pltpu.SemaphoreType.DMA((7,)), pltpu.SemaphoreType.DMA((7,)), pltpu.SemaphoreType.DMA],
    )(blk)


def _all_gather_big(name, shards):
    n = len(shards)

    def body(*refs):
        ins, outs = refs[:n], refs[n:2 * n]
        send_sems, recv_sems, local_sems = refs[2 * n:]
        x, y, c = _coords()
        me, sibling = (x, y, c), (x, y, 1 - c)
        chips = [(1 - x, y), (x, 1 - y), (1 - x, 1 - y)]

        def slot(a, px, py, pc):
            return outs[a].at[4 * px + 2 * py + pc]

        def copy(a, k, block, to, src=None):
            return pltpu.make_async_remote_copy(
                src_ref=slot(a, *block) if src is None else src, dst_ref=slot(a, *block),
                send_sem=send_sems.at[7 * a + k], recv_sem=recv_sems.at[7 * a + k], device_id=to, device_id_type=MESH)

        local = [pltpu.make_async_copy(ins[a], slot(a, *me), local_sems.at[a]) for a in range(n)]
        for cp in local:
            cp.start()
        started = []
        for a in range(n):
            first = [copy(a, 0, me, sibling, src=ins[a])]
            first += [copy(a, 1 + j, me, (*chip, c), src=ins[a]) for j, chip in enumerate(chips)]
            for cp in first:
                cp.start()
            started += first
        for a in range(n):
            for j, chip in enumerate(chips):
                copy(a, 1 + j, (*chip, c), me).wait_recv()
                fwd = copy(a, 4 + j, (*chip, c), sibling)
                fwd.start()
                started.append(fwd)
        for a in range(n):
            copy(a, 0, sibling, me).wait_recv()
            for j, chip in enumerate(chips):
                copy(a, 4 + j, (*chip, 1 - c), me).wait_recv()
        for cp in started:
            cp.wait_send()
        for cp in local:
            cp.wait()

    any_spec = pl.BlockSpec(memory_space=pl.ANY)
    return pl.pallas_call(
        body, name=name, out_shape=[jax.ShapeDtypeStruct((N_DEV,) + s.shape, s.dtype) for s in shards],
        in_specs=[any_spec] * n, out_specs=[any_spec] * n,
        scratch_shapes=[pltpu.SemaphoreType.DMA((7 * n,)), pltpu.SemaphoreType.DMA((7 * n,)), pltpu.SemaphoreType.DMA((n,))],
    )(*shards)


def _exchange_blocks(name, arrays):
    n = len(arrays)

    def body(*refs):
        ins, outs = refs[:n], refs[n:2 * n]
        send_sems, recv_sems, local_sems = refs[2 * n:]
        x, y, c = _coords()
        my_idx = 4 * x + 2 * y + c
        local = [pltpu.make_async_copy(ins[a].at[my_idx], outs[a].at[my_idx], local_sems.at[a]) for a in range(n)]
        for cp in local:
            cp.start()
        copies = []
        for a in range(n):
            for rel in range(1, N_DEV):
                px, py, pc = _flip(x, rel & 4), _flip(y, rel & 2), _flip(c, rel & 1)
                peer_idx = 4 * px + 2 * py + pc
                send = pltpu.make_async_remote_copy(
                    src_ref=ins[a].at[peer_idx], dst_ref=outs[a].at[my_idx],
                    send_sem=send_sems.at[7 * a + rel - 1], recv_sem=recv_sems.at[7 * a + rel - 1],
                    device_id=(px, py, pc), device_id_type=MESH)
                send.start()
                recv = pltpu.make_async_remote_copy(
                    src_ref=ins[a].at[my_idx], dst_ref=outs[a].at[peer_idx],
                    send_sem=send_sems.at[7 * a + rel - 1], recv_sem=recv_sems.at[7 * a + rel - 1],
                    device_id=(px, py, pc), device_id_type=MESH)
                copies.append((send, recv))
        for send, recv in copies:
            recv.wait_recv()
            send.wait_send()
        for cp in local:
            cp.wait()

    any_spec = pl.BlockSpec(memory_space=pl.ANY)
    return pl.pallas_call(
        body, name=name, out_shape=[jax.ShapeDtypeStruct(a.shape, a.dtype) for a in arrays],
        in_specs=[any_spec] * n, out_specs=[any_spec] * n,
        scratch_shapes=[pltpu.SemaphoreType.DMA((7 * n,)), pltpu.SemaphoreType.DMA((7 * n,)), pltpu.SemaphoreType.DMA((n,))],
    )(*arrays)


def _rope_tables(lx, lc):
    rows = lx // GRID_W
    row = jnp.repeat(jnp.arange(rows, dtype=F32), GRID_W)
    col = jnp.tile(jnp.arange(GRID_W, dtype=F32), rows)
    half = HEAD_DIM // 2
    inv_freq = ROPE_THETA ** (-jnp.arange(0, half, 2, dtype=F32) / half)
    ang = jnp.concatenate([row[:, None] * inv_freq, col[:, None] * inv_freq], axis=-1)
    cos, sin = jnp.cos(ang), jnp.sin(ang)
    cosf = jnp.repeat(cos, 2, axis=-1)
    sinf = jnp.stack([-sin, sin], axis=-1).reshape(lx, HEAD_DIM)
    cosf = jnp.concatenate([cosf, jnp.ones((lc, HEAD_DIM), F32)], axis=0)
    sinf = jnp.concatenate([sinf, jnp.zeros((lc, HEAD_DIM), F32)], axis=0)
    return cosf, sinf


def _cols_full(g):
    return jnp.transpose(g, (1, 0, 2)).reshape(g.shape[1], N_DEV * g.shape[2])


def _cols_split(w):
    k, n = w.shape
    return jnp.transpose(w.reshape(k, N_DEV, n // N_DEV), (1, 0, 2))


def _pad_rows(a, rows):
    return jnp.pad(a, ((0, rows - a.shape[0]), (0, 0)))


def kernel(x, c, ctx, c_ctx, w_ada, b_ada, ffn1_w_in, ffn1_w_out, mix_w_in, attn_q_gain, attn_k_gain, ret_decay_logit, w_proj_attn, w_proj_ret, mix_w_out, ffn2_w_in, ffn2_w_out, final_norm, loss_target, m_c_ctx, m_w_ada, m_b_ada, m_ffn1_w_in, m_ffn1_w_out, m_mix_w_in, m_attn_q_gain, m_attn_k_gain, m_ret_decay_logit, m_w_proj_attn, m_w_proj_ret, m_mix_w_out, m_ffn2_w_in, m_ffn2_w_out, m_final_norm, v_c_ctx, v_w_ada, v_b_ada, v_ffn1_w_in, v_ffn1_w_out, v_mix_w_in, v_attn_q_gain, v_attn_k_gain, v_ret_decay_logit, v_w_proj_attn, v_w_proj_ret, v_mix_w_out, v_ffn2_w_in, v_ffn2_w_out, v_final_norm):
    lx, d = x.shape[1], x.shape[2]
    lc = ctx.shape[1]
    t = lx + lc
    aw, rw = w_proj_attn.shape[1], w_proj_ret.shape[1]
    pw = mix_w_in.shape[2] * N_DEV
    kw = (pw - aw - 4 * rw - 2 * d) // 2
    groups = aw // kw
    heads_r = rw // HEAD_DIM
    ka_off, va_off = aw, aw + kw
    qr_off = aw + 2 * kw
    kr_off, vr_off, gr_off = qr_off + rw, qr_off + 2 * rw, qr_off + 3 * rw
    ga_off, gb_off = qr_off + 4 * rw, qr_off + 4 * rw + d
    xi, yi, ci = _coords()
    me = 4 * xi + 2 * yi + ci

    col_names = ["ffn1_w_in", "mix_w_in", "w_proj_attn", "w_proj_ret", "ffn2_w_in"]
    row_names = ["ffn1_w_out", "mix_w_out", "ffn2_w_out"]
    shard = dict(ffn1_w_in=ffn1_w_in[0], mix_w_in=mix_w_in[0], w_proj_attn=w_proj_attn[0], w_proj_ret=w_proj_ret[0],
                 ffn2_w_in=ffn2_w_in[0], ffn1_w_out=ffn1_w_out[0], mix_w_out=mix_w_out[0], ffn2_w_out=ffn2_w_out[0])
    names = col_names + row_names
    gathered = _all_gather_big("gather_weights", [shard[k].astype(BF16) for k in names])
    full = {}
    for k, g in zip(names, gathered):
        full[k] = _cols_full(g) if k in col_names else g.reshape(N_DEV * g.shape[1], g.shape[2])

    c_all = _all_gather_small("gather_cond", _pad_rows(c, SUBLANES))[:, 0, :]
    cond = _pad_rows(jnp.concatenate([c_all, c_ctx[None, :]], axis=0), 2 * SUBLANES)
    ada_part, s_cond = _ada_fwd("ada_fwd", cond, w_ada[0])
    ada_all = _all_gather_small("gather_ada", ada_part)
    mod_all = jnp.transpose(ada_all, (1, 0, 2)).reshape(2 * SUBLANES, N_MOD * d) + b_ada
    mod_x = lax.dynamic_index_in_dim(mod_all, me, axis=0, keepdims=False).reshape(N_MOD, d)
    mod_c = mod_all[N_DEV].reshape(N_MOD, d)
    mods = jnp.stack([mod_c, mod_x], axis=0)[:, :, None, :]
    sh1, sc1, g1, sh2, sc2, g2, sh3, sc3, g3 = [mods[:, k] for k in range(N_MOD)]

    h0 = jnp.concatenate([x[0], ctx[0]], axis=0)
    u1 = _rmsmod("ffn1_norm", h0, sc1, sh1, lx)
    z1, s1 = _ffn_in("ffn1_in", u1, full["ffn1_w_in"])
    h1, f1 = _mm_residual("ffn1_out", s1, full["ffn1_w_out"], h0, g1, 0.5, lx)

    u2 = _rmsmod("mix_norm", h1, sc2, sh2, lx)
    p = _mm_nn("mix_in", u2, full["mix_w_in"], BF16, tn_pref=512)
    cosf, sinf = _rope_tables(lx, lc)
    q_rot, k_rot = _qk_prep("qk_prep", p, cosf, sinf, attn_q_gain, attn_k_gain, aw, kw)
    ya, lse = _attn_fwd("attn_fwd", q_rot, k_rot, p, va_off, lx, groups)

    decay = ret_decay_logit[0].astype(F32)
    log_gamma = jax.nn.log_sigmoid(decay)
    r_offs = (qr_off, kr_off, vr_off, rw)
    tab = [_ret_tables(log_gamma[k], k) for k in range(2)]
    y_f, st_f = _ret_fwd("ret_fwd_a", p, r_offs, tab[0], 0, lx, None)
    y_r, st_b = _ret_fwd("ret_fwd_b", p, r_offs, tab[1], 1, lx, y_f)
    yr = _ret_out("ret_out", y_r, p, gr_off, lx)

    pa = _mm_nn("proj_attn", ya, full["w_proj_attn"], BF16)
    pr = _mm_nn("proj_ret", yr, full["w_proj_ret"], BF16)
    mg = _merge("merge", pa, pr, p, ga_off, gb_off)
    h2, o2 = _mm_residual("mix_out", mg, full["mix_w_out"], h1, g2, 1.0, lx)

    u3 = _rmsmod("ffn2_norm", h2, sc3, sh3, lx)
    z3, s3 = _ffn_in("ffn2_in", u3, full["ffn2_w_in"])
    h3, f3 = _mm_residual("ffn2_out", s3, full["ffn2_w_out"], h2, g3, 0.5, lx)
    dh3, d_fn, loss_tile = _final_loss("final_loss", h3, final_norm[None, :], loss_target[0])

    df3, dg3 = _gate_bwd("ffn2_gate_bwd", dh3, f3, g3, 0.5, lx)
    dz3 = _ffn_out_bwd("ffn2_out_bwd", df3, full["ffn2_w_out"], z3)
    gw = {}
    gw["ffn2_w_out"] = _mm_tn("ffn2_out_dw", s3, df3, BF16)
    du3 = _ffn_in_bwd_x("ffn2_in_bwd", dz3, full["ffn2_w_in"])
    gw["ffn2_w_in"] = _ffn_in_bwd_w("ffn2_in_dw", u3, dz3)
    dh2, dsc3, dsh3 = _rmsmod_bwd("ffn2_norm_bwd", du3, h2, sc3, dh3, lx, lx)

    do2, dg2 = _gate_bwd("mix_gate_bwd", dh2, o2, g2, 1.0, lx)
    dpa, dpr, dga, dgb = _merge_bwd("merge_bwd", do2, full["mix_w_out"], pa, pr, p, ga_off, gb_off)
    gw["mix_w_out"] = _mm_tn("mix_out_dw", mg, do2, BF16)
    dya = _mm_nt("proj_attn_bwd", dpa, full["w_proj_attn"], BF16)
    dyr = _mm_nt("proj_ret_bwd", dpr, full["w_proj_ret"], F32)
    gw["w_proj_attn"] = _mm_tn("proj_attn_dw", ya, dpa, BF16)
    gw["w_proj_ret"] = _mm_tn("proj_ret_dw", yr, dpr, BF16)

    dy_ret, dgr = _ret_out_bwd("ret_out_bwd", dyr, y_r, p, gr_off, lx)
    dqr, dkr, dvr, dl_f = _ret_bwd("ret_bwd_a", p, r_offs, tab[0], st_f, dy_ret, 0, lx, None)
    dqr, dkr, dvr, dl_b = _ret_bwd("ret_bwd_b", p, r_offs, tab[1], st_b, dy_ret, 1, lx, (dqr, dkr, dvr))
    d_lam = jnp.stack([dl_f[:, 0, 0], dl_b[:, 0, 0]], axis=0)
    d_decay = d_lam * jax.nn.sigmoid(-decay)

    dq_rot = _attn_bwd_q("attn_bwd_q", q_rot, k_rot, p, va_off, ya, dya, lse, lx, groups)
    dk_rot, dva = _attn_bwd_kv("attn_bwd_kv", q_rot, k_rot, p, va_off, ya, dya, lse, lx, groups)
    dqa, dka, d_qg, d_kg = _qk_prep_bwd("qk_prep_bwd", dq_rot, dk_rot, p, cosf, sinf, attn_q_gain, attn_k_gain, aw, kw)

    dp = jnp.concatenate([dqa, dka, dva, dqr.astype(BF16), dkr.astype(BF16), dvr.astype(BF16),
                          _pad_rows(dgr, t), _pad_rows(dga, t), _pad_rows(dgb, t)], axis=1)
    du2 = _mm_nt("mix_in_bwd", dp, full["mix_w_in"], F32)
    gw["mix_w_in"] = _mm_tn("mix_in_dw", u2, dp, BF16, tn_pref=512)
    dh1, dsc2, dsh2 = _rmsmod_bwd("mix_norm_bwd", du2, h1, sc2, dh2, lx, t)

    df1, dg1 = _gate_bwd("ffn1_gate_bwd", dh1, f1, g1, 0.5, lx)
    dz1 = _ffn_out_bwd("ffn1_out_bwd", df1, full["ffn1_w_out"], z1)
    gw["ffn1_w_out"] = _mm_tn("ffn1_out_dw", s1, df1, BF16)
    du1 = _ffn_in_bwd_x("ffn1_in_bwd", dz1, full["ffn1_w_in"])
    gw["ffn1_w_in"] = _ffn_in_bwd_w("ffn1_in_dw", u1, dz1)
    grad_x, dsc1, dsh1 = _rmsmod_bwd("ffn1_norm_bwd", du1, h0, sc1, dh1, lx, lx)

    zero = jnp.zeros((1, d), F32)
    dmod_c = jnp.concatenate([dsh1[0], dsc1[0], dg1[0], dsh2[0], dsc2[0], zero, zero, zero, zero], axis=0)
    dmod_x = jnp.concatenate([dsh1[1], dsc1[1], dg1[1], dsh2[1], dsc2[1], dg2[1], dsh3[1], dsc3[1], dg3[1]], axis=0)
    misc = jnp.concatenate([d_qg[0], d_kg[0], d_decay.reshape(-1), loss_tile[0, 0:1]])
    misc = jnp.pad(misc, (0, d - misc.shape[0]))[None, :]
    n_small = 3 * SUBLANES
    small = _pad_rows(jnp.concatenate([dmod_c, dmod_x, d_fn, misc], axis=0), n_small)
    small_all = _all_gather_small("gather_small", small)
    small_sum = _sum_slots("sum_small", small_all)
    dmod_c_sum, dmod_x_sum = small_sum[0:N_MOD], small_sum[N_MOD:2 * N_MOD]
    g_final_norm = small_sum[2 * N_MOD]
    misc_sum = small_sum[2 * N_MOD + 1]
    g_qg = misc_sum[0:HEAD_DIM][None, :]
    g_kg = misc_sum[HEAD_DIM:2 * HEAD_DIM][None, :]
    g_decay = misc_sum[2 * HEAD_DIM:2 * HEAD_DIM + 2 * heads_r].reshape(1, 2, heads_r)
    loss = misc_sum[2 * HEAD_DIM + 2 * heads_r]
    g_b_ada = (dmod_x_sum + dmod_c_sum).reshape(1, N_MOD * d)

    n_ada = w_ada.shape[2]
    dmod_rows = jnp.concatenate([small_all[:, N_MOD:2 * N_MOD, :].reshape(N_DEV, N_MOD * d),
                                 dmod_c_sum.reshape(1, N_MOD * d)], axis=0)
    dmod_mine = _pad_rows(lax.dynamic_slice_in_dim(dmod_rows, me * n_ada, n_ada, axis=1), 2 * SUBLANES)
    g_w_ada, ds_cond = _ada_bwd("ada_bwd", s_cond, dmod_mine, w_ada[0])
    cctx_parts = _all_gather_small("gather_cctx", ds_cond[N_DEV:N_DEV + SUBLANES])
    g_c_ctx = _cctx_grad("cctx_grad", cctx_parts, c_ctx[None, :])[0]

    send = [_cols_split(gw[k]) if k in col_names else gw[k].reshape(N_DEV, gw[k].shape[0] // N_DEV, gw[k].shape[1])
            for k in names]
    landed = _exchange_blocks("exchange_grads", send)
    mom = dict(ffn1_w_in=(m_ffn1_w_in, v_ffn1_w_in), mix_w_in=(m_mix_w_in, v_mix_w_in),
               w_proj_attn=(m_w_proj_attn, v_w_proj_attn), w_proj_ret=(m_w_proj_ret, v_w_proj_ret),
               ffn2_w_in=(m_ffn2_w_in, v_ffn2_w_in), ffn1_w_out=(m_ffn1_w_out, v_ffn1_w_out),
               mix_w_out=(m_mix_w_out, v_mix_w_out), ffn2_w_out=(m_ffn2_w_out, v_ffn2_w_out))
    res = {}
    for k, land in zip(names, landed):
        res[k] = _adamw("adamw_" + k, land, shard[k], mom[k][0][0], mom[k][1][0])
    res["w_ada"] = _adamw("adamw_w_ada", g_w_ada[None], w_ada[0], m_w_ada[0], v_w_ada[0])

    def pack(cc, ba, qg, kg, dec, fn):
        misc_row = jnp.concatenate([qg.reshape(-1), kg.reshape(-1), dec.reshape(-1)])
        misc_row = jnp.pad(misc_row, (0, d - misc_row.shape[0]))[None, :]
        return _pad_rows(jnp.concatenate([cc.reshape(1, d), ba.reshape(N_MOD, d), fn.reshape(1, d), misc_row], axis=0),
                         2 * SUBLANES)

    sg, sd, sm, sv = _adamw(
        "adamw_small", pack(g_c_ctx, g_b_ada, g_qg, g_kg, g_decay, g_final_norm)[None],
        pack(c_ctx, b_ada, attn_q_gain, attn_k_gain, ret_decay_logit, final_norm),
        pack(m_c_ctx, m_b_ada, m_attn_q_gain, m_attn_k_gain, m_ret_decay_logit, m_final_norm),
        pack(v_c_ctx, v_b_ada, v_attn_q_gain, v_attn_k_gain, v_ret_decay_logit, v_final_norm))

    def unpack(a):
        misc_row = a[N_MOD + 2]
        return dict(c_ctx=a[0], b_ada=a[1:1 + N_MOD].reshape(1, N_MOD * d), final_norm=a[N_MOD + 1],
                    attn_q_gain=misc_row[0:HEAD_DIM][None, :], attn_k_gain=misc_row[HEAD_DIM:2 * HEAD_DIM][None, :],
                    ret_decay_logit=misc_row[2 * HEAD_DIM:2 * HEAD_DIM + 2 * heads_r].reshape(1, 2, heads_r))

    small_out = [unpack(a) for a in (sg, sd, sm, sv)]
    order = ["c_ctx", "w_ada", "b_ada", "ffn1_w_in", "ffn1_w_out", "mix_w_in", "attn_q_gain", "attn_k_gain",
             "ret_decay_logit", "w_proj_attn", "w_proj_ret", "mix_w_out", "ffn2_w_in", "ffn2_w_out", "final_norm"]
    outs = [loss, grad_x[None]]
    for which in range(4):
        for k in order:
            outs.append(res[k][which][None] if k in res else small_out[which][k])
    return tuple(outs)
```

```python
import math

import jax
import jax.numpy as jnp
from jax import lax
from jax.experimental import pallas as pl
from jax.experimental.pallas import tpu as pltpu

F32 = jnp.float32
BF16 = jnp.bfloat16
MESH = pl.DeviceIdType.MESH

N_DEV = 8
HEAD_DIM = 128
GRID_W = 64
ROPE_THETA = 10000.0
NORM_EPS = 1e-6
RET_CHUNK = 128
N_MOD = 9
LANES = 128
SUBLANES = 8
V7X_VMEM_BYTES = 64 * 1024 * 1024
VMEM_LIMIT = V7X_VMEM_BYTES - 8 * 1024 * 1024
K_TILE = 2560
LOG2E = 1.4426950408889634
LN2 = 0.6931471805599453

ADAM_LR = 0.001
ADAM_B1 = 0.9
ADAM_B2 = 0.999
ADAM_EPS = 1e-08
ADAM_WD = 0.01
ADAM_STEP = 10

NN = (((1,), (0,)), ((), ()))
NT = (((1,), (1,)), ((), ()))
TN = (((0,), (0,)), ((), ()))


def _tile(n, pref, align=LANES):
    best = None
    t = align
    while t <= min(n, pref):
        if n % t == 0:
            best = t
        t += align
    return n if best is None else best


def _cp(sem):
    return pltpu.CompilerParams(dimension_semantics=sem, vmem_limit_bytes=VMEM_LIMIT)


def _sigmoid(v):
    return 1.0 / (1.0 + jnp.exp(-v))


def _dot(a, b, dims):
    return lax.dot_general(a, b, dims, preferred_element_type=F32)


def _mm(name, a, a_spec, b_list, dims, grid, out_shapes, out_specs, acc_shape, epi, extras=()):
    nb, ne, no = len(b_list), len(extras), len(out_shapes)
    nk = grid[2]

    def body(*refs):
        a_ref = refs[0]
        b_refs = refs[1:1 + nb]
        e_refs = refs[1 + nb:1 + nb + ne]
        o_refs = refs[1 + nb + ne:1 + nb + ne + no]
        accs = refs[1 + nb + ne + no:]

        def finish(tiles):
            vals = epi(tiles, e_refs)
            for o_ref, v in zip(o_refs, vals):
                if isinstance(v, tuple):
                    for idx, part in enumerate(v):
                        o_ref[idx] = part.astype(o_ref.dtype)
                else:
                    o_ref[...] = v.astype(o_ref.dtype)

        if nk == 1:
            finish([_dot(a_ref[...], b_ref[...], dims) for b_ref in b_refs])
            return
        k = pl.program_id(2)

        @pl.when(k == 0)
        def _():
            for acc in accs:
                acc[...] = jnp.zeros(acc.shape, F32)

        av = a_ref[...]
        for b_ref, acc in zip(b_refs, accs):
            acc[...] += _dot(av, b_ref[...], dims)

        @pl.when(k == nk - 1)
        def _():
            finish([acc[...] for acc in accs])

    return pl.pallas_call(
        body, name=name, grid=grid,
        in_specs=[a_spec] + [s for _, s in b_list] + [s for _, s in extras],
        out_specs=out_specs, out_shape=out_shapes,
        scratch_shapes=[] if nk == 1 else [pltpu.VMEM(acc_shape, F32)] * nb,
        compiler_params=_cp(("parallel", "parallel", "arbitrary")),
    )(a, *[b for b, _ in b_list], *[e for e, _ in extras])


def _plain(accs, _):
    return (accs[0],)


def _mm_nn(name, a, b, out_dtype, tm_pref=1024, tn_pref=1024, tk_pref=K_TILE):
    m, k = a.shape
    n = b.shape[1]
    tm, tn, tk = _tile(m, tm_pref), _tile(n, tn_pref), _tile(k, tk_pref)
    return _mm(name, a, pl.BlockSpec((tm, tk), lambda i, j, kk: (i, kk)),
               [(b, pl.BlockSpec((tk, tn), lambda i, j, kk: (kk, j)))], NN, (m // tm, n // tn, k // tk),
               [jax.ShapeDtypeStruct((m, n), out_dtype)], [pl.BlockSpec((tm, tn), lambda i, j, kk: (i, j))],
               (tm, tn), _plain)[0]


def _mm_nt(name, a, b, out_dtype, tm_pref=1024, tn_pref=1024, tk_pref=K_TILE):
    m, k = a.shape
    n = b.shape[0]
    tm, tn, tk = _tile(m, tm_pref), _tile(n, tn_pref), _tile(k, tk_pref)
    return _mm(name, a, pl.BlockSpec((tm, tk), lambda i, j, kk: (i, kk)),
               [(b, pl.BlockSpec((tn, tk), lambda i, j, kk: (j, kk)))], NT, (m // tm, n // tn, k // tk),
               [jax.ShapeDtypeStruct((m, n), out_dtype)], [pl.BlockSpec((tm, tn), lambda i, j, kk: (i, j))],
               (tm, tn), _plain)[0]


def _mm_tn(name, a, b, out_dtype, rows=None, tm_pref=1024, tn_pref=1024, tk_pref=K_TILE):
    k = a.shape[0] if rows is None else rows
    m, n = a.shape[1], b.shape[1]
    tm, tn, tk = _tile(m, tm_pref), _tile(n, tn_pref), _tile(k, tk_pref)
    return _mm(name, a, pl.BlockSpec((tk, tm), lambda i, j, kk: (kk, i)),
               [(b, pl.BlockSpec((tk, tn), lambda i, j, kk: (kk, j)))], TN, (m // tm, n // tn, k // tk),
               [jax.ShapeDtypeStruct((m, n), out_dtype)], [pl.BlockSpec((tm, tn), lambda i, j, kk: (i, j))],
               (tm, tn), _plain)[0]


def _ffn_in(name, u, w_in):
    r, d = u.shape
    f = w_in.shape[1] // 2
    tm, tn, tk = _tile(r, 1024), _tile(f, 512), _tile(d, K_TILE)
    nf = f // tn

    def epi(accs, _):
        za, zb = accs
        s = za * _sigmoid(za) * zb
        return (za, zb), s

    return _mm(name, u, pl.BlockSpec((tm, tk), lambda i, j, kk: (i, kk)),
               [(w_in, pl.BlockSpec((tk, tn), lambda i, j, kk: (kk, j))),
                (w_in, pl.BlockSpec((tk, tn), lambda i, j, kk: (kk, j + nf)))],
               NN, (r // tm, nf, d // tk),
               [jax.ShapeDtypeStruct((2, r, f), BF16), jax.ShapeDtypeStruct((r, f), BF16)],
               [pl.BlockSpec((2, tm, tn), lambda i, j, kk: (0, i, j)), pl.BlockSpec((tm, tn), lambda i, j, kk: (i, j))],
               (tm, tn), epi)


def _mm_residual(name, a, w, res, gate, gate_scale, lx):
    r, k = a.shape
    n = w.shape[1]
    tm, tn, tk = _tile(r, 1024), _tile(n, 1024), _tile(k, K_TILE)

    def epi(accs, e_refs):
        res_ref, g_ref = e_refs
        rows = pl.program_id(0) * tm + lax.broadcasted_iota(jnp.int32, (tm, 1), 0)
        g = jnp.where(rows < lx, g_ref[1], g_ref[0])
        return res_ref[...] + gate_scale * g * accs[0], accs[0]

    return _mm(name, a, pl.BlockSpec((tm, tk), lambda i, j, kk: (i, kk)),
               [(w, pl.BlockSpec((tk, tn), lambda i, j, kk: (kk, j)))], NN, (r // tm, n // tn, k // tk),
               [jax.ShapeDtypeStruct((r, n), F32), jax.ShapeDtypeStruct((r, n), BF16)],
               [pl.BlockSpec((tm, tn), lambda i, j, kk: (i, j))] * 2, (tm, tn), epi,
               extras=[(res, pl.BlockSpec((tm, tn), lambda i, j, kk: (i, j))),
                       (gate, pl.BlockSpec((2, 1, tn), lambda i, j, kk: (0, 0, j)))])


def _ffn_out_bwd(name, df, w_out, z):
    r, d = df.shape
    f = w_out.shape[0]
    tm, tn, tk = _tile(r, 1024), _tile(f, 512), _tile(d, K_TILE)

    def epi(accs, e_refs):
        ds = accs[0]
        za = e_refs[0][0].astype(F32)
        zb = e_refs[0][1].astype(F32)
        sg = _sigmoid(za)
        da = ds * zb * sg * (1.0 + za * (1.0 - sg))
        db = ds * za * sg
        return ((da, db),)

    zspec = pl.BlockSpec((2, tm, tn), lambda i, j, kk: (0, i, j))
    return _mm(name, df, pl.BlockSpec((tm, tk), lambda i, j, kk: (i, kk)),
               [(w_out, pl.BlockSpec((tn, tk), lambda i, j, kk: (j, kk)))], NT, (r // tm, f // tn, d // tk),
               [jax.ShapeDtypeStruct((2, r, f), BF16)], [zspec], (tm, tn), epi, extras=[(z, zspec)])[0]


def _ffn_in_bwd_x(name, dz, w_in):
    _, r, f = dz.shape
    d = w_in.shape[0]
    tm, tn, tk = _tile(r, 1024), _tile(d, 1024), _tile(f, K_TILE)
    nkf = f // tk
    return _mm(name, dz, pl.BlockSpec((None, tm, tk), lambda i, j, kk: (kk // nkf, i, kk % nkf)),
               [(w_in, pl.BlockSpec((tn, tk), lambda i, j, kk: (j, kk)))], NT, (r // tm, d // tn, 2 * nkf),
               [jax.ShapeDtypeStruct((r, d), F32)], [pl.BlockSpec((tm, tn), lambda i, j, kk: (i, j))],
               (tm, tn), _plain)[0]


def _ffn_in_bwd_w(name, u, dz):
    r, d = u.shape
    f = dz.shape[2]
    tm, tn, tk = _tile(d, 1024), _tile(f, 512), _tile(r, K_TILE)
    nf = f // tn
    return _mm(name, u, pl.BlockSpec((tk, tm), lambda i, j, kk: (kk, i)),
               [(dz, pl.BlockSpec((None, tk, tn), lambda i, j, kk: (j // nf, kk, j % nf)))], TN,
               (d // tm, 2 * nf, r // tk),
               [jax.ShapeDtypeStruct((d, 2 * f), BF16)], [pl.BlockSpec((tm, tn), lambda i, j, kk: (i, j))],
               (tm, tn), _plain)[0]


def _merge_bwd(name, dout, w_out, pa, pr, p, ga_off, gb_off):
    r, d = dout.shape
    n = w_out.shape[0]
    cw = math.gcd(math.gcd(ga_off, gb_off), n)
    tm, tn, tk = _tile(r, 1024), _tile(cw, 512), _tile(d, K_TILE)

    def epi(accs, e_refs):
        dm = accs[0]
        pa_ref, pr_ref, ga_ref, gb_ref = e_refs
        sa = _sigmoid(ga_ref[...].astype(F32))
        sb = _sigmoid(gb_ref[...].astype(F32))
        pav = pa_ref[...].astype(F32)
        prv = pr_ref[...].astype(F32)
        return dm * sa, dm * sb, dm * pav * sa * (1.0 - sa), dm * prv * sb * (1.0 - sb)

    o_spec = pl.BlockSpec((tm, tn), lambda i, j, kk: (i, j))
    return _mm(name, dout, pl.BlockSpec((tm, tk), lambda i, j, kk: (i, kk)),
               [(w_out, pl.BlockSpec((tn, tk), lambda i, j, kk: (j, kk)))], NT, (r // tm, n // tn, d // tk),
               [jax.ShapeDtypeStruct((r, n), BF16)] * 4, [o_spec] * 4, (tm, tn), epi,
               extras=[(pa, o_spec), (pr, o_spec),
                       (p, pl.BlockSpec((tm, tn), lambda i, j, kk: (i, ga_off // tn + j))),
                       (p, pl.BlockSpec((tm, tn), lambda i, j, kk: (i, gb_off // tn + j)))])


def _row_tile(r, lx, d):
    pref = 256 if d > 1024 else 512
    return _tile(math.gcd(r, lx), pref, SUBLANES)


def _rmsmod(name, h, scale, shift, lx, rows=None):
    r = h.shape[0] if rows is None else rows
    d = h.shape[1]
    tr = _row_tile(r, lx, d)
    nx = lx // tr
    cls = lambda i: (jnp.where(i < nx, 1, 0), 0, 0)

    def body(h_ref, sc_ref, sh_ref, u_ref):
        hv = h_ref[...]
        rinv = lax.rsqrt(jnp.mean(hv * hv, axis=-1, keepdims=True) + NORM_EPS)
        u_ref[...] = (hv * rinv * (1.0 + sc_ref[...]) + sh_ref[...]).astype(u_ref.dtype)

    return pl.pallas_call(
        body, name=name, grid=(r // tr,),
        in_specs=[pl.BlockSpec((tr, d), lambda i: (i, 0)), pl.BlockSpec((None, 1, d), cls), pl.BlockSpec((None, 1, d), cls)],
        out_specs=pl.BlockSpec((tr, d), lambda i: (i, 0)), out_shape=jax.ShapeDtypeStruct((r, d), BF16),
        compiler_params=_cp(("parallel",)))(h, scale, shift)


def _rmsmod_bwd(name, du, h, scale, dh_in, lx, rows_out):
    r, d = du.shape
    rin = dh_in.shape[0]
    tr = _row_tile(math.gcd(r, math.gcd(rin, rows_out)), lx, d)
    nx, nin, nout = lx // tr, rin // tr, rows_out // tr
    cls = lambda i: (jnp.where(i < nx, 1, 0), 0, 0)

    def body(du_ref, h_ref, sc_ref, dhin_ref, dh_ref, dsc_ref, dsh_ref):
        i = pl.program_id(0)
        hv = h_ref[...]
        duv = du_ref[...]
        rinv = lax.rsqrt(jnp.mean(hv * hv, axis=-1, keepdims=True) + NORM_EPS)
        nv = hv * rinv
        dn = duv * (1.0 + sc_ref[...])

        @pl.when(jnp.logical_or(i == 0, i == nx))
        def _():
            dsc_ref[...] = jnp.zeros(dsc_ref.shape, F32)
            dsh_ref[...] = jnp.zeros(dsh_ref.shape, F32)

        dsc_ref[...] += jnp.sum(duv * nv, axis=0, keepdims=True)
        dsh_ref[...] += jnp.sum(duv, axis=0, keepdims=True)

        @pl.when(i < nout)
        def _():
            dh = rinv * (dn - nv * jnp.mean(dn * nv, axis=-1, keepdims=True))
            dh_ref[...] = dh + jnp.where(i < nin, dhin_ref[...], 0.0)

    return pl.pallas_call(
        body, name=name, grid=(r // tr,),
        in_specs=[pl.BlockSpec((tr, d), lambda i: (i, 0)), pl.BlockSpec((tr, d), lambda i: (i, 0)),
                  pl.BlockSpec((None, 1, d), cls), pl.BlockSpec((tr, d), lambda i: (jnp.minimum(i, nin - 1), 0))],
        out_specs=[pl.BlockSpec((tr, d), lambda i: (jnp.minimum(i, nout - 1), 0)),
                   pl.BlockSpec((None, 1, d), cls), pl.BlockSpec((None, 1, d), cls)],
        out_shape=[jax.ShapeDtypeStruct((rows_out, d), F32), jax.ShapeDtypeStruct((2, 1, d), F32),
                   jax.ShapeDtypeStruct((2, 1, d), F32)],
        compiler_params=_cp(("arbitrary",)))(du, h, scale, dh_in)


def _gate_bwd(name, dh, f, gate, gate_scale, lx):
    r, d = dh.shape
    tr = _row_tile(r, lx, d)
    nx = lx // tr
    cls = lambda i: (jnp.where(i < nx, 1, 0), 0, 0)

    def body(dh_ref, f_ref, g_ref, df_ref, dg_ref):
        i = pl.program_id(0)
        dhv = dh_ref[...]

        @pl.when(jnp.logical_or(i == 0, i == nx))
        def _():
            dg_ref[...] = jnp.zeros(dg_ref.shape, F32)

        df_ref[...] = (gate_scale * g_ref[...] * dhv).astype(df_ref.dtype)
        dg_ref[...] += jnp.sum(gate_scale * f_ref[...].astype(F32) * dhv, axis=0, keepdims=True)

    return pl.pallas_call(
        body, name=name, grid=(r // tr,),
        in_specs=[pl.BlockSpec((tr, d), lambda i: (i, 0)), pl.BlockSpec((tr, d), lambda i: (i, 0)),
                  pl.BlockSpec((None, 1, d), cls)],
        out_specs=[pl.BlockSpec((tr, d), lambda i: (i, 0)), pl.BlockSpec((None, 1, d), cls)],
        out_shape=[jax.ShapeDtypeStruct((r, d), BF16), jax.ShapeDtypeStruct((2, 1, d), F32)],
        compiler_params=_cp(("arbitrary",)))(dh, f, gate)


def _final_loss(name, h, final_norm, target):
    r, d = h.shape
    tr = _row_tile(r, r, d)

    def body(h_ref, fn_ref, t_ref, dh_ref, dfn_ref, loss_ref):
        i = pl.program_id(0)
        hv = h_ref[...]
        rinv = lax.rsqrt(jnp.mean(hv * hv, axis=-1, keepdims=True) + NORM_EPS)
        nv = hv * rinv
        fn = fn_ref[...]
        err = nv * fn - t_ref[...]
        dy = err * (1.0 / d)

        @pl.when(i == 0)
        def _():
            dfn_ref[...] = jnp.zeros(dfn_ref.shape, F32)
            loss_ref[...] = jnp.zeros(loss_ref.shape, F32)

        loss_ref[...] += 0.5 * jnp.sum(jnp.mean(err * err, axis=-1, keepdims=True), axis=0, keepdims=True)
        dfn_ref[...] += jnp.sum(dy * nv, axis=0, keepdims=True)
        dn = dy * fn
        dh_ref[...] = rinv * (dn - nv * jnp.mean(dn * nv, axis=-1, keepdims=True))

    return pl.pallas_call(
        body, name=name, grid=(r // tr,),
        in_specs=[pl.BlockSpec((tr, d), lambda i: (i, 0)), pl.BlockSpec((1, d), lambda i: (0, 0)),
                  pl.BlockSpec((tr, d), lambda i: (i, 0))],
        out_specs=[pl.BlockSpec((tr, d), lambda i: (i, 0)), pl.BlockSpec((1, d), lambda i: (0, 0)),
                   pl.BlockSpec((SUBLANES, LANES), lambda i: (0, 0))],
        out_shape=[jax.ShapeDtypeStruct((r, d), F32), jax.ShapeDtypeStruct((1, d), F32),
                   jax.ShapeDtypeStruct((SUBLANES, LANES), F32)],
        compiler_params=_cp(("arbitrary",)))(h, final_norm, target)


def _swap_pairs(t):
    lane = lax.broadcasted_iota(jnp.int32, t.shape, 1)
    nxt = pltpu.roll(t, HEAD_DIM - 1, 1)
    prv = pltpu.roll(t, 1, 1)
    return jnp.where(lane % 2 == 0, nxt, prv)


def _qk_prep(name, p, cosf, sinf, q_gain, k_gain, aw, kw):
    t = p.shape[0]
    tr = _tile(t, 256, SUBLANES)
    q_prescale = HEAD_DIM ** -0.5 * LOG2E

    def head_fwd(v, gain, cs, sn):
        v = v.astype(F32)
        rinv = lax.rsqrt(jnp.mean(v * v, axis=-1, keepdims=True) + NORM_EPS)
        tt = v * rinv * gain
        return tt * cs + _swap_pairs(tt) * sn

    def body(q_ref, k_ref, cos_ref, sin_ref, qg_ref, kg_ref, qo_ref, ko_ref):
        cs, sn = cos_ref[...], sin_ref[...]
        for hh in range(aw // HEAD_DIM):
            sl = slice(hh * HEAD_DIM, (hh + 1) * HEAD_DIM)
            qo_ref[:, sl] = (head_fwd(q_ref[:, sl], qg_ref[...], cs, sn) * q_prescale).astype(qo_ref.dtype)
        for hh in range(kw // HEAD_DIM):
            sl = slice(hh * HEAD_DIM, (hh + 1) * HEAD_DIM)
            ko_ref[:, sl] = head_fwd(k_ref[:, sl], kg_ref[...], cs, sn).astype(ko_ref.dtype)

    assert aw % kw == 0
    row = lambda i: (i, 0)
    return pl.pallas_call(
        body, name=name, grid=(t // tr,),
        in_specs=[pl.BlockSpec((tr, aw), row), pl.BlockSpec((tr, kw), lambda i: (i, aw // kw)),
                  pl.BlockSpec((tr, HEAD_DIM), row), pl.BlockSpec((tr, HEAD_DIM), row),
                  pl.BlockSpec((1, HEAD_DIM), lambda i: (0, 0)), pl.BlockSpec((1, HEAD_DIM), lambda i: (0, 0))],
        out_specs=[pl.BlockSpec((tr, aw), row), pl.BlockSpec((tr, kw), row)],
        out_shape=[jax.ShapeDtypeStruct((t, aw), BF16), jax.ShapeDtypeStruct((t, kw), BF16)],
        compiler_params=_cp(("parallel",)))(p, p, cosf, sinf, q_gain, k_gain)


def _qk_prep_bwd(name, dq_rot, dk_rot, p, cosf, sinf, q_gain, k_gain, aw, kw):
    t = p.shape[0]
    lq = dq_rot.shape[0]
    tr = _tile(math.gcd(t, lq), 256, SUBLANES)
    nq = lq // tr

    def head_bwd(dout, v, gain, cs, sn):
        v = v.astype(F32)
        rinv = lax.rsqrt(jnp.mean(v * v, axis=-1, keepdims=True) + NORM_EPS)
        vn = v * rinv
        dt = dout * cs - _swap_pairs(dout) * sn
        dvn = dt * gain
        dv = rinv * (dvn - vn * jnp.mean(dvn * vn, axis=-1, keepdims=True))
        return dv, jnp.sum(dt * vn, axis=0, keepdims=True)

    def body(dq_ref, dk_ref, q_ref, k_ref, cos_ref, sin_ref, qg_ref, kg_ref, dqo_ref, dko_ref, dqg_ref, dkg_ref):
        i = pl.program_id(0)
        cs, sn = cos_ref[...], sin_ref[...]

        @pl.when(i == 0)
        def _():
            dqg_ref[...] = jnp.zeros(dqg_ref.shape, F32)
            dkg_ref[...] = jnp.zeros(dkg_ref.shape, F32)

        has_q = i < nq
        for hh in range(aw // HEAD_DIM):
            sl = slice(hh * HEAD_DIM, (hh + 1) * HEAD_DIM)
            dout = jnp.where(has_q, dq_ref[:, sl], 0.0)
            dv, dg = head_bwd(dout, q_ref[:, sl], qg_ref[...], cs, sn)
            dqo_ref[:, sl] = dv.astype(dqo_ref.dtype)
            dqg_ref[...] += dg
        for hh in range(kw // HEAD_DIM):
            sl = slice(hh * HEAD_DIM, (hh + 1) * HEAD_DIM)
            dv, dg = head_bwd(dk_ref[:, sl], k_ref[:, sl], kg_ref[...], cs, sn)
            dko_ref[:, sl] = dv.astype(dko_ref.dtype)
            dkg_ref[...] += dg

    row = lambda i: (i, 0)
    one = lambda i: (0, 0)
    return pl.pallas_call(
        body, name=name, grid=(t // tr,),
        in_specs=[pl.BlockSpec((tr, aw), lambda i: (jnp.minimum(i, nq - 1), 0)), pl.BlockSpec((tr, kw), row),
                  pl.BlockSpec((tr, aw), row), pl.BlockSpec((tr, kw), lambda i: (i, aw // kw)),
                  pl.BlockSpec((tr, HEAD_DIM), row), pl.BlockSpec((tr, HEAD_DIM), row),
                  pl.BlockSpec((1, HEAD_DIM), one), pl.BlockSpec((1, HEAD_DIM), one)],
        out_specs=[pl.BlockSpec((tr, aw), row), pl.BlockSpec((tr, kw), row),
                   pl.BlockSpec((1, HEAD_DIM), one), pl.BlockSpec((1, HEAD_DIM), one)],
        out_shape=[jax.ShapeDtypeStruct((t, aw), BF16), jax.ShapeDtypeStruct((t, kw), BF16),
                   jax.ShapeDtypeStruct((1, HEAD_DIM), F32), jax.ShapeDtypeStruct((1, HEAD_DIM), F32)],
        compiler_params=_cp(("arbitrary",)))(dq_rot, dk_rot, p, p, cosf, sinf, q_gain, k_gain)


def _attn_tiles(lq, t):
    return _tile(lq, 512), _tile(t, 768)


def _attn_fwd(name, q, k, p, v_off, lq, groups):
    t = k.shape[0]
    hq = q.shape[1] // HEAD_DIM
    hkv = hq // groups
    gw = groups * HEAD_DIM
    tq, tk = _attn_tiles(lq, t)
    nkv = t // tk
    vb = v_off // HEAD_DIM

    def body(q_ref, k_ref, v_ref, o_ref, lse_ref, m_sc, l_sc, acc_sc):
        j = pl.program_id(2)

        @pl.when(j == 0)
        def _():
            m_sc[...] = jnp.full(m_sc.shape, -jnp.inf, F32)
            l_sc[...] = jnp.zeros(l_sc.shape, F32)
            acc_sc[...] = jnp.zeros(acc_sc.shape, F32)

        kv, vv = k_ref[...], v_ref[...]
        for g in range(groups):
            sl = slice(g * HEAD_DIM, (g + 1) * HEAD_DIM)
            s = _dot(q_ref[:, sl], kv, NT)
            m_prev = m_sc[g]
            m_new = jnp.maximum(m_prev, jnp.max(s, axis=-1, keepdims=True))
            alpha = jnp.exp2(m_prev - m_new)
            pexp = jnp.exp2(s - m_new)
            l_sc[g] = alpha * l_sc[g] + jnp.sum(pexp, axis=-1, keepdims=True)
            acc_sc[:, sl] = alpha * acc_sc[:, sl] + _dot(pexp.astype(BF16), vv, NN)
            m_sc[g] = m_new

        @pl.when(j == nkv - 1)
        def _():
            for g in range(groups):
                sl = slice(g * HEAD_DIM, (g + 1) * HEAD_DIM)
                o_ref[:, sl] = (acc_sc[:, sl] * (1.0 / l_sc[g])).astype(o_ref.dtype)
                lse_ref[g] = jnp.broadcast_to(m_sc[g] + jnp.log2(l_sc[g]), (tq, LANES))

    return pl.pallas_call(
        body, name=name, grid=(hkv, lq // tq, nkv),
        in_specs=[pl.BlockSpec((tq, gw), lambda h, i, j: (i, h)),
                  pl.BlockSpec((tk, HEAD_DIM), lambda h, i, j: (j, h)),
                  pl.BlockSpec((tk, HEAD_DIM), lambda h, i, j: (j, vb + h))],
        out_specs=[pl.BlockSpec((tq, gw), lambda h, i, j: (i, h)),
                   pl.BlockSpec((groups, tq, LANES), lambda h, i, j: (h, i, 0))],
        out_shape=[jax.ShapeDtypeStruct((lq, hq * HEAD_DIM), BF16), jax.ShapeDtypeStruct((hq, lq, LANES), F32)],
        scratch_shapes=[pltpu.VMEM((groups, tq, 1), F32), pltpu.VMEM((groups, tq, 1), F32), pltpu.VMEM((tq, gw), F32)],
        compiler_params=_cp(("parallel", "parallel", "arbitrary")))(q, k, p)


def _attn_bwd_q(name, q, k, p, v_off, o, do, lse, lq, groups):
    t = k.shape[0]
    hq = q.shape[1] // HEAD_DIM
    hkv = hq // groups
    gw = groups * HEAD_DIM
    tq, tk = _attn_tiles(lq, t)
    nkv = t // tk
    scale = HEAD_DIM ** -0.5
    vb = v_off // HEAD_DIM

    def body(q_ref, k_ref, v_ref, o_ref, do_ref, lse_ref, dq_ref, delta_sc, acc_sc):
        j = pl.program_id(2)

        @pl.when(j == 0)
        def _():
            for g in range(groups):
                sl = slice(g * HEAD_DIM, (g + 1) * HEAD_DIM)
                delta_sc[g] = jnp.sum(do_ref[:, sl].astype(F32) * o_ref[:, sl].astype(F32), axis=-1, keepdims=True)
            acc_sc[...] = jnp.zeros(acc_sc.shape, F32)

        kv, vv = k_ref[...], v_ref[...]
        for g in range(groups):
            sl = slice(g * HEAD_DIM, (g + 1) * HEAD_DIM)
            s = _dot(q_ref[:, sl], kv, NT)
            pexp = jnp.exp2(s - lse_ref[g, :, 0:1])
            dp = _dot(do_ref[:, sl], vv, NT)
            ds = pexp * (dp - delta_sc[g])
            acc_sc[:, sl] += _dot(ds.astype(BF16), kv, NN)

        @pl.when(j == nkv - 1)
        def _():
            dq_ref[...] = acc_sc[...] * scale

    qspec = pl.BlockSpec((tq, gw), lambda h, i, j: (i, h))
    return pl.pallas_call(
        body, name=name, grid=(hkv, lq // tq, nkv),
        in_specs=[qspec, pl.BlockSpec((tk, HEAD_DIM), lambda h, i, j: (j, h)),
                  pl.BlockSpec((tk, HEAD_DIM), lambda h, i, j: (j, vb + h)), qspec, qspec,
                  pl.BlockSpec((groups, tq, LANES), lambda h, i, j: (h, i, 0))],
        out_specs=qspec, out_shape=jax.ShapeDtypeStruct((lq, hq * HEAD_DIM), F32),
        scratch_shapes=[pltpu.VMEM((groups, tq, 1), F32), pltpu.VMEM((tq, gw), F32)],
        compiler_params=_cp(("parallel", "parallel", "arbitrary")))(q, k, p, o, do, lse)


def _attn_bwd_kv(name, q, k, p, v_off, o, do, lse, lq, groups):
    t = k.shape[0]
    hkv = k.shape[1] // HEAD_DIM
    gw = groups * HEAD_DIM
    tq, tk = _attn_tiles(lq, t)
    nq = lq // tq
    vb = v_off // HEAD_DIM

    def body(q_ref, k_ref, v_ref, o_ref, do_ref, lse_ref, dk_ref, dv_ref, dk_sc, dv_sc):
        i = pl.program_id(2)

        @pl.when(i == 0)
        def _():
            dk_sc[...] = jnp.zeros(dk_sc.shape, F32)
            dv_sc[...] = jnp.zeros(dv_sc.shape, F32)

        kv, vv = k_ref[...], v_ref[...]
        dk_part, dv_part = None, None
        for g in range(groups):
            sl = slice(g * HEAD_DIM, (g + 1) * HEAD_DIM)
            qv, dov = q_ref[:, sl], do_ref[:, sl]
            delta = jnp.sum(dov.astype(F32) * o_ref[:, sl].astype(F32), axis=-1, keepdims=True)
            s = _dot(qv, kv, NT)
            pexp = jnp.exp2(s - lse_ref[g, :, 0:1])
            dv_g = _dot(pexp.astype(BF16), dov, TN)
            dp = _dot(dov, vv, NT)
            ds = pexp * (dp - delta)
            dk_g = _dot(ds.astype(BF16), qv, TN)
            dk_part = dk_g if dk_part is None else dk_part + dk_g
            dv_part = dv_g if dv_part is None else dv_part + dv_g
        dk_sc[...] += dk_part
        dv_sc[...] += dv_part

        @pl.when(i == nq - 1)
        def _():
            dk_ref[...] = dk_sc[...] * LN2
            dv_ref[...] = dv_sc[...].astype(dv_ref.dtype)

    qspec = pl.BlockSpec((tq, gw), lambda kh, j, i: (i, kh))
    kspec = pl.BlockSpec((tk, HEAD_DIM), lambda kh, j, i: (j, kh))
    return pl.pallas_call(
        body, name=name, grid=(hkv, t // tk, nq),
        in_specs=[qspec, kspec, pl.BlockSpec((tk, HEAD_DIM), lambda kh, j, i: (j, vb + kh)), qspec, qspec,
                  pl.BlockSpec((groups, tq, LANES), lambda kh, j, i: (kh, i, 0))],
        out_specs=[kspec, kspec],
        out_shape=[jax.ShapeDtypeStruct((t, hkv * HEAD_DIM), F32), jax.ShapeDtypeStruct((t, hkv * HEAD_DIM), BF16)],
        scratch_shapes=[pltpu.VMEM((tk, HEAD_DIM), F32), pltpu.VMEM((tk, HEAD_DIM), F32)],
        compiler_params=_cp(("parallel", "parallel", "arbitrary")))(q, k, p, o, do, lse)


def _ret_tables(log_gamma, direction):
    c = RET_CHUNK
    idx = jnp.arange(c, dtype=F32)
    diff = idx[:, None] - idx[None, :]
    if direction == 1:
        diff = -diff
    keep = diff >= 0
    lg = log_gamma.astype(F32)
    mask = jnp.where(keep[None], jnp.exp(jnp.where(keep, diff, 0.0)[None] * lg[:, None, None]), 0.0)
    q_exp = idx + 1.0 if direction == 0 else c - idx
    k_exp = c - 1.0 - idx if direction == 0 else idx
    sign = 1.0 if direction == 0 else -1.0
    lane = lambda v: jnp.broadcast_to(v[..., None], v.shape + (LANES,))
    qdec = lane(jnp.exp(q_exp[None, :] * lg[:, None]))
    kdec = lane(jnp.exp(k_exp[None, :] * lg[:, None]))
    cdec = jnp.broadcast_to(jnp.exp(c * lg)[:, None, None], (lg.shape[0], SUBLANES, LANES))
    weights = lane(jnp.stack([sign * idx, q_exp, -sign * idx, k_exp], axis=0))
    return mask, qdec, kdec, cdec, weights


def _ret_chunk_of(direction, step, nx, nc):
    if direction == 0:
        return jnp.where(step < nc, nx + step, step - nc)
    return jnp.where(step < nc, nx + nc - 1 - step, nx + nc - 1 - step)


def _ret_fwd(name, p, offs, tables, direction, lx, prev):
    q_off, k_off, v_off, rw = offs
    t = p.shape[0]
    c = RET_CHUNK
    n_steps, nx = t // c, lx // c
    nc = n_steps - nx
    bw = math.gcd(math.gcd(q_off, k_off), math.gcd(v_off, rw))
    bw = _tile(bw, 512)
    hpb = bw // HEAD_DIM
    heads = rw // HEAD_DIM
    k_scale = HEAD_DIM ** -0.5
    mask, qdec, kdec, cdec, _ = tables
    rc = lambda n: _ret_chunk_of(direction, n, nx, nc)

    def body(*refs):
        if prev is None:
            q_ref, k_ref, v_ref, m_ref, qd_ref, kd_ref, cd_ref, y_ref, st_ref, s_sc = refs
        else:
            q_ref, k_ref, v_ref, m_ref, qd_ref, kd_ref, cd_ref, prev_ref, y_ref, st_ref, s_sc = refs
        n = pl.program_id(1)

        @pl.when(n == 0)
        def _():
            s_sc[...] = jnp.zeros(s_sc.shape, F32)

        for hh in range(hpb):
            sl = slice(hh * HEAD_DIM, (hh + 1) * HEAD_DIM)
            qv = q_ref[:, sl]
            kf = k_ref[:, sl].astype(F32) * k_scale
            kv = kf.astype(BF16)
            vv = v_ref[:, sl]
            state = s_sc[hh]
            st_ref[hh] = state
            a = _dot(qv, kv, NT) * m_ref[hh]
            y = _dot(a.astype(BF16), vv, NN) + _dot(qv, state.astype(BF16), NN) * qd_ref[hh]
            s_sc[hh] = state * cd_ref[hh, 0:1, :] + _dot((kf * kd_ref[hh]).astype(BF16), vv, TN)
            if prev is not None:
                y = y + prev_ref[:, sl]
            y_ref[:, sl] = y

    col = lambda off: (lambda g, n: (rc(n), off // bw + g))
    tab3 = lambda g, n: (g, 0, 0)
    in_specs = [pl.BlockSpec((c, bw), col(q_off)), pl.BlockSpec((c, bw), col(k_off)), pl.BlockSpec((c, bw), col(v_off)),
                pl.BlockSpec((hpb, c, c), tab3), pl.BlockSpec((hpb, c, LANES), tab3), pl.BlockSpec((hpb, c, LANES), tab3),
                pl.BlockSpec((hpb, SUBLANES, LANES), tab3)]
    args = [p, p, p, mask, qdec, kdec, cdec]
    aliases = {}
    yspec = pl.BlockSpec((c, bw), lambda g, n: (rc(n), g))
    if prev is not None:
        in_specs.append(yspec)
        args.append(prev)
        aliases = {len(args) - 1: 0}
    return pl.pallas_call(
        body, name=name, grid=(heads // hpb, n_steps), in_specs=in_specs,
        out_specs=[yspec, pl.BlockSpec((None, hpb, HEAD_DIM, HEAD_DIM), lambda g, n: (n, g, 0, 0))],
        out_shape=[jax.ShapeDtypeStruct((t, rw), F32), jax.ShapeDtypeStruct((n_steps, heads, HEAD_DIM, HEAD_DIM), F32)],
        scratch_shapes=[pltpu.VMEM((hpb, HEAD_DIM, HEAD_DIM), F32)], input_output_aliases=aliases,
        compiler_params=_cp(("parallel", "arbitrary")))(*args)


def _ret_bwd(name, p, offs, tables, states, dy, direction, lx, prev):
    q_off, k_off, v_off, rw = offs
    t = p.shape[0]
    c = RET_CHUNK
    n_steps, nx = t // c, lx // c
    nc = n_steps - nx
    bw = math.gcd(math.gcd(q_off, k_off), math.gcd(v_off, rw))
    bw = _tile(bw, 512)
    hpb = bw // HEAD_DIM
    heads = rw // HEAD_DIM
    k_scale = HEAD_DIM ** -0.5
    mask, qdec, kdec, cdec, weights = tables
    step_of = lambda n: n_steps - 1 - n
    rc = lambda n: _ret_chunk_of(direction, step_of(n), nx, nc)

    def body(*refs):
        if prev is None:
            (q_ref, k_ref, v_ref, dy_ref, st_ref, m_ref, qd_ref, kd_ref, cd_ref, w_ref,
             dq_ref, dk_ref, dv_ref, dl_ref, ds_sc, lam_sc) = refs
        else:
            (q_ref, k_ref, v_ref, dy_ref, st_ref, m_ref, qd_ref, kd_ref, cd_ref, w_ref, pq_ref, pk_ref, pv_ref,
             dq_ref, dk_ref, dv_ref, dl_ref, ds_sc, lam_sc) = refs
        n = pl.program_id(1)

        @pl.when(n == 0)
        def _():
            ds_sc[...] = jnp.zeros(ds_sc.shape, F32)
            lam_sc[...] = jnp.zeros(lam_sc.shape, F32)

        is_x = rc(n) < nx
        for hh in range(hpb):
            sl = slice(hh * HEAD_DIM, (hh + 1) * HEAD_DIM)
            qv = q_ref[:, sl]
            qf = qv.astype(F32)
            kf = k_ref[:, sl].astype(F32) * k_scale
            kv = kf.astype(BF16)
            vv = v_ref[:, sl]
            dyv = jnp.where(is_x, dy_ref[:, sl], 0.0).astype(BF16)
            state = st_ref[hh]
            dstate = ds_sc[hh]
            dstate_b = dstate.astype(BF16)
            msk, qd, kd = m_ref[hh], qd_ref[hh], kd_ref[hh]
            cd = cd_ref[hh, 0:1, :]
            a = _dot(qv, kv, NT) * msk
            da = (_dot(dyv, vv, NT) * msk).astype(BF16)
            dq_intra = _dot(da, kv, NN)
            dk_intra = _dot(da, qv, TN)
            dq_inter = _dot(dyv, state.astype(BF16), NT) * qd
            dk_inter = _dot(vv, dstate_b, NT) * kd
            dv = _dot(a.astype(BF16), dyv, TN) + _dot((kf * kd).astype(BF16), dstate_b, NN)
            lam_sc[hh] += (qf * (w_ref[0] * dq_intra + w_ref[1] * dq_inter)
                           + kf * (w_ref[2] * dk_intra + w_ref[3] * dk_inter)
                           + (c * cd) * state * dstate)
            ds_sc[hh] = _dot((qf * qd).astype(BF16), dyv, TN) + cd * dstate
            dq = dq_intra + dq_inter
            dk = (dk_intra + dk_inter) * k_scale
            if prev is not None:
                dq = dq + pq_ref[:, sl]
                dk = dk + pk_ref[:, sl]
                dv = dv + pv_ref[:, sl]
            dq_ref[:, sl] = dq
            dk_ref[:, sl] = dk
            dv_ref[:, sl] = dv

        @pl.when(n == n_steps - 1)
        def _():
            for hh in range(hpb):
                dl_ref[hh] = jnp.broadcast_to(jnp.sum(lam_sc[hh]), (SUBLANES, LANES))

    col = lambda off: (lambda g, n: (rc(n), off // bw + g))
    tab3 = lambda g, n: (g, 0, 0)
    ospec = pl.BlockSpec((c, bw), lambda g, n: (rc(n), g))
    in_specs = [pl.BlockSpec((c, bw), col(q_off)), pl.BlockSpec((c, bw), col(k_off)), pl.BlockSpec((c, bw), col(v_off)),
                pl.BlockSpec((c, bw), lambda g, n: (jnp.minimum(rc(n), nx - 1), g)),
                pl.BlockSpec((None, hpb, HEAD_DIM, HEAD_DIM), lambda g, n: (step_of(n), g, 0, 0)),
                pl.BlockSpec((hpb, c, c), tab3), pl.BlockSpec((hpb, c, LANES), tab3), pl.BlockSpec((hpb, c, LANES), tab3),
                pl.BlockSpec((hpb, SUBLANES, LANES), tab3), pl.BlockSpec((4, c, LANES), lambda g, n: (0, 0, 0))]
    args = [p, p, p, dy, states, mask, qdec, kdec, cdec, weights]
    aliases = {}
    if prev is not None:
        for k_out, arr in enumerate(prev):
            in_specs.append(ospec)
            args.append(arr)
            aliases[len(args) - 1] = k_out
    big = jax.ShapeDtypeStruct((t, rw), F32)
    return pl.pallas_call(
        body, name=name, grid=(heads // hpb, n_steps), in_specs=in_specs,
        out_specs=[ospec, ospec, ospec, pl.BlockSpec((hpb, SUBLANES, LANES), tab3)],
        out_shape=[big, big, big, jax.ShapeDtypeStruct((heads, SUBLANES, LANES), F32)],
        scratch_shapes=[pltpu.VMEM((hpb, HEAD_DIM, HEAD_DIM), F32), pltpu.VMEM((hpb, HEAD_DIM, HEAD_DIM), F32)],
        input_output_aliases=aliases, compiler_params=_cp(("parallel", "arbitrary")))(*args)


def _ret_out(name, y, p, g_off, lx):
    rw = y.shape[1]
    bw = _tile(math.gcd(g_off, rw), 512)
    tr = _tile(lx, 512, SUBLANES)

    def body(y_ref, g_ref, o_ref):
        for hh in range(bw // HEAD_DIM):
            sl = slice(hh * HEAD_DIM, (hh + 1) * HEAD_DIM)
            yv = y_ref[:, sl]
            gv = g_ref[:, sl].astype(F32)
            rinv = lax.rsqrt(jnp.mean(yv * yv, axis=-1, keepdims=True) + NORM_EPS)
            o_ref[:, sl] = (gv * _sigmoid(gv) * yv * rinv).astype(o_ref.dtype)

    spec = pl.BlockSpec((tr, bw), lambda i, g: (i, g))
    return pl.pallas_call(
        body, name=name, grid=(lx // tr, rw // bw),
        in_specs=[spec, pl.BlockSpec((tr, bw), lambda i, g: (i, g_off // bw + g))],
        out_specs=spec, out_shape=jax.ShapeDtypeStruct((lx, rw), BF16),
        compiler_params=_cp(("parallel", "parallel")))(y, p)


def _ret_out_bwd(name, dyr, y, p, g_off, lx):
    rw = y.shape[1]
    bw = _tile(math.gcd(g_off, rw), 512)
    tr = _tile(lx, 512, SUBLANES)

    def body(d_ref, y_ref, g_ref, dy_ref, dg_ref):
        for hh in range(bw // HEAD_DIM):
            sl = slice(hh * HEAD_DIM, (hh + 1) * HEAD_DIM)
            yv = y_ref[:, sl]
            gv = g_ref[:, sl].astype(F32)
            dv = d_ref[:, sl]
            rinv = lax.rsqrt(jnp.mean(yv * yv, axis=-1, keepdims=True) + NORM_EPS)
            yn = yv * rinv
            sg = _sigmoid(gv)
            dg_ref[:, sl] = (dv * yn * sg * (1.0 + gv * (1.0 - sg))).astype(dg_ref.dtype)
            dyn = dv * gv * sg
            dy_ref[:, sl] = rinv * (dyn - yn * jnp.mean(dyn * yn, axis=-1, keepdims=True))

    spec = pl.BlockSpec((tr, bw), lambda i, g: (i, g))
    return pl.pallas_call(
        body, name=name, grid=(lx // tr, rw // bw),
        in_specs=[spec, spec, pl.BlockSpec((tr, bw), lambda i, g: (i, g_off // bw + g))],
        out_specs=[spec, spec],
        out_shape=[jax.ShapeDtypeStruct((lx, rw), F32), jax.ShapeDtypeStruct((lx, rw), BF16)],
        compiler_params=_cp(("parallel", "parallel")))(dyr, y, p)


def _merge(name, pa, pr, p, ga_off, gb_off):
    r, d = pa.shape
    cw = _tile(math.gcd(math.gcd(ga_off, gb_off), d), 1024)
    tr = _tile(r, 512, SUBLANES)

    def body(pa_ref, pr_ref, ga_ref, gb_ref, o_ref):
        o_ref[...] = (_sigmoid(ga_ref[...].astype(F32)) * pa_ref[...].astype(F32)
                      + _sigmoid(gb_ref[...].astype(F32)) * pr_ref[...].astype(F32)).astype(o_ref.dtype)

    spec = pl.BlockSpec((tr, cw), lambda i, j: (i, j))
    return pl.pallas_call(
        body, name=name, grid=(r // tr, d // cw),
        in_specs=[spec, spec, pl.BlockSpec((tr, cw), lambda i, j: (i, ga_off // cw + j)),
                  pl.BlockSpec((tr, cw), lambda i, j: (i, gb_off // cw + j))],
        out_specs=spec, out_shape=jax.ShapeDtypeStruct((r, d), BF16),
        compiler_params=_cp(("parallel", "parallel")))(pa, pr, p, p)


def _ada_fwd(name, cond, w):
    rows, d = cond.shape
    n = w.shape[1]
    tn = _tile(n, 768)

    def body(c_ref, w_ref, o_ref, s_ref):
        cv = c_ref[...]
        sv = cv * _sigmoid(cv)
        s_ref[...] = sv
        o_ref[...] = _dot(sv.astype(BF16), w_ref[...].astype(BF16), NN)

    return pl.pallas_call(
        body, name=name, grid=(n // tn,),
        in_specs=[pl.BlockSpec((rows, d), lambda j: (0, 0)), pl.BlockSpec((d, tn), lambda j: (0, j))],
        out_specs=[pl.BlockSpec((rows, tn), lambda j: (0, j)), pl.BlockSpec((rows, d), lambda j: (0, 0))],
        out_shape=[jax.ShapeDtypeStruct((rows, n), F32), jax.ShapeDtypeStruct((rows, d), F32)],
        compiler_params=_cp(("arbitrary",)))(cond, w)


def _ada_bwd(name, s_cond, dmod, w):
    rows, d = s_cond.shape
    n = w.shape[1]
    tn = _tile(n, 768)

    def body(s_ref, dm_ref, w_ref, gw_ref, ds_ref):
        j = pl.program_id(0)

        @pl.when(j == 0)
        def _():
            ds_ref[...] = jnp.zeros(ds_ref.shape, F32)

        dmv = dm_ref[...].astype(BF16)
        gw_ref[...] = _dot(s_ref[...].astype(BF16), dmv, TN)
        ds_ref[...] += _dot(dmv, w_ref[...].astype(BF16), NT)

    return pl.pallas_call(
        body, name=name, grid=(n // tn,),
        in_specs=[pl.BlockSpec((rows, d), lambda j: (0, 0)), pl.BlockSpec((rows, tn), lambda j: (0, j)),
                  pl.BlockSpec((d, tn), lambda j: (0, j))],
        out_specs=[pl.BlockSpec((d, tn), lambda j: (0, j)), pl.BlockSpec((rows, d), lambda j: (0, 0))],
        out_shape=[jax.ShapeDtypeStruct((d, n), F32), jax.ShapeDtypeStruct((rows, d), F32)],
        compiler_params=_cp(("arbitrary",)))(s_cond, dmod, w)


def _sum_slots(name, a):
    s, r, c = a.shape

    def body(a_ref, o_ref):
        acc = a_ref[0]
        for k in range(1, s):
            acc = acc + a_ref[k]
        o_ref[...] = acc

    return pl.pallas_call(
        body, name=name, grid=(1,), in_specs=[pl.BlockSpec((s, r, c), lambda i: (0, 0, 0))],
        out_specs=pl.BlockSpec((r, c), lambda i: (0, 0)), out_shape=jax.ShapeDtypeStruct((r, c), F32),
        compiler_params=_cp(("arbitrary",)))(a)


def _cctx_grad(name, parts, c_ctx):
    s, r, d = parts.shape

    def body(p_ref, c_ref, o_ref):
        acc = p_ref[0]
        for k in range(1, s):
            acc = acc + p_ref[k]
        cv = c_ref[...]
        sg = _sigmoid(cv)
        o_ref[...] = acc[0:1, :] * sg * (1.0 + cv * (1.0 - sg))

    return pl.pallas_call(
        body, name=name, grid=(1,),
        in_specs=[pl.BlockSpec((s, r, d), lambda i: (0, 0, 0)), pl.BlockSpec((1, d), lambda i: (0, 0))],
        out_specs=pl.BlockSpec((1, d), lambda i: (0, 0)), out_shape=jax.ShapeDtypeStruct((1, d), F32),
        compiler_params=_cp(("arbitrary",)))(parts, c_ctx)


def _adamw(name, slots, w, m, v):
    s, r, c = slots.shape
    tr = _tile(r, 256 if c > 1024 else 512, SUBLANES)
    c1 = 1.0 - ADAM_B1 ** ADAM_STEP
    c2 = 1.0 - ADAM_B2 ** ADAM_STEP

    def body(s_ref, w_ref, m_ref, v_ref, g_ref, d_ref, mo_ref, vo_ref):
        g = s_ref[0].astype(F32)
        for k in range(1, s):
            g = g + s_ref[k].astype(F32)
        mn = ADAM_B1 * m_ref[...] + (1.0 - ADAM_B1) * g
        vn = ADAM_B2 * v_ref[...] + (1.0 - ADAM_B2) * (g * g)
        m_hat = mn / c1
        v_hat = vn / c2
        g_ref[...] = g
        mo_ref[...] = mn
        vo_ref[...] = vn
        d_ref[...] = -ADAM_LR * (m_hat / (jnp.sqrt(v_hat) + ADAM_EPS) + ADAM_WD * w_ref[...])

    spec = pl.BlockSpec((tr, c), lambda i: (i, 0))
    shp = jax.ShapeDtypeStruct((r, c), F32)
    return pl.pallas_call(
        body, name=name, grid=(r // tr,),
        in_specs=[pl.BlockSpec((s, tr, c), lambda i: (0, i, 0)), spec, spec, spec],
        out_specs=[spec] * 4, out_shape=[shp] * 4, compiler_params=_cp(("parallel",)))(slots, w, m, v)


def _coords():
    return lax.axis_index("x"), lax.axis_index("y"), lax.axis_index("c")


def _flip(coord, bit):
    return 1 - coord if bit else coord


def _all_gather_small(name, blk):
    r, ccols = blk.shape

    def body(x_ref, out_ref, send_sems, recv_sems, local_sem):
        x, y, c = _coords()
        me, sibling = (x, y, c), (x, y, 1 - c)
        chips = [(1 - x, y), (x, 1 - y), (1 - x, 1 - y)]

        def slot(px, py, pc):
            return out_ref.at[4 * px + 2 * py + pc]

        def copy(k, block, to, src=None):
            return pltpu.make_async_remote_copy(
                src_ref=slot(*block) if src is None else src, dst_ref=slot(*block),
                send_sem=send_sems.at[k], recv_sem=recv_sems.at[k], device_id=to, device_id_type=MESH)

        mine = pltpu.make_async_copy(x_ref, slot(*me), local_sem)
        mine.start()
        first = [copy(0, me, sibling, src=x_ref)]
        first += [copy(1 + j, me, (*chip, c), src=x_ref) for j, chip in enumerate(chips)]
        for cp in first:
            cp.start()
        passed = [copy(4 + j, (*chip, c), sibling) for j, chip in enumerate(chips)]
        for j, chip in enumerate(chips):
            copy(1 + j, (*chip, c), me).wait_recv()
            passed[j].start()
        copy(0, sibling, me).wait_recv()
        for j, chip in enumerate(chips):
            copy(4 + j, (*chip, 1 - c), me).wait_recv()
        for cp in first + passed:
            cp.wait_send()
        mine.wait()

    return pl.pallas_call(
        body, name=name, out_shape=jax.ShapeDtypeStruct((N_DEV, r, ccols), blk.dtype),
        in_specs=[pl.BlockSpec(memory_space=pltpu.VMEM)], out_specs=pl.BlockSpec(memory_space=pltpu.VMEM),
        scratch_shapes=[pltpu.SemaphoreType.DMA((7,)), pltpu.SemaphoreType.DMA((7,)), pltpu.SemaphoreType.DMA],
    )(blk)


def _all_gather_big(name, shards):
    n = len(shards)

    def body(*refs):
        ins, outs = refs[:n], refs[n:2 * n]
        send_sems, recv_sems, local_sems = refs[2 * n:]
        x, y, c = _coords()
        me, sibling = (x, y, c), (x, y, 1 - c)
        chips = [(1 - x, y), (x, 1 - y), (1 - x, 1 - y)]

        def slot(a, px, py, pc):
            return outs[a].at[4 * px + 2 * py + pc]

        def copy(a, k, block, to, src=None):
            return pltpu.make_async_remote_copy(
                src_ref=slot(a, *block) if src is None else src, dst_ref=slot(a, *block),
                send_sem=send_sems.at[7 * a + k], recv_sem=recv_sems.at[7 * a + k], device_id=to, device_id_type=MESH)

        local = [pltpu.make_async_copy(ins[a], slot(a, *me), local_sems.at[a]) for a in range(n)]
        for cp in local:
            cp.start()
        started = []
        for a in range(n):
            first = [copy(a, 0, me, sibling, src=ins[a])]
            first += [copy(a, 1 + j, me, (*chip, c), src=ins[a]) for j, chip in enumerate(chips)]
            for cp in first:
                cp.start()
            started += first
        for a in range(n):
            for j, chip in enumerate(chips):
                copy(a, 1 + j, (*chip, c), me).wait_recv()
                fwd = copy(a, 4 + j, (*chip, c), sibling)
                fwd.start()
                started.append(fwd)
        for a in range(n):
            copy(a, 0, sibling, me).wait_recv()
            for j, chip in enumerate(chips):
                copy(a, 4 + j, (*chip, 1 - c), me).wait_recv()
        for cp in started:
            cp.wait_send()
        for cp in local:
            cp.wait()

    any_spec = pl.BlockSpec(memory_space=pl.ANY)
    return pl.pallas_call(
        body, name=name, out_shape=[jax.ShapeDtypeStruct((N_DEV,) + s.shape, s.dtype) for s in shards],
        in_specs=[any_spec] * n, out_specs=[any_spec] * n,
        scratch_shapes=[pltpu.SemaphoreType.DMA((7 * n,)), pltpu.SemaphoreType.DMA((7 * n,)), pltpu.SemaphoreType.DMA((n,))],
    )(*shards)


def _exchange_blocks(name, arrays):
    n = len(arrays)

    def body(*refs):
        ins, outs = refs[:n], refs[n:2 * n]
        send_sems, recv_sems, local_sems = refs[2 * n:]
        x, y, c = _coords()
        my_idx = 4 * x + 2 * y + c
        local = [pltpu.make_async_copy(ins[a].at[my_idx], outs[a].at[my_idx], local_sems.at[a]) for a in range(n)]
        for cp in local:
            cp.start()
        copies = []
        for a in range(n):
            for rel in range(1, N_DEV):
                px, py, pc = _flip(x, rel & 4), _flip(y, rel & 2), _flip(c, rel & 1)
                peer_idx = 4 * px + 2 * py + pc
                send = pltpu.make_async_remote_copy(
                    src_ref=ins[a].at[peer_idx], dst_ref=outs[a].at[my_idx],
                    send_sem=send_sems.at[7 * a + rel - 1], recv_sem=recv_sems.at[7 * a + rel - 1],
                    device_id=(px, py, pc), device_id_type=MESH)
                send.start()
                recv = pltpu.make_async_remote_copy(
                    src_ref=ins[a].at[my_idx], dst_ref=outs[a].at[peer_idx],
                    send_sem=send_sems.at[7 * a + rel - 1], recv_sem=recv_sems.at[7 * a + rel - 1],
                    device_id=(px, py, pc), device_id_type=MESH)
                copies.append((send, recv))
        for send, recv in copies:
            recv.wait_recv()
            send.wait_send()
        for cp in local:
            cp.wait()

    any_spec = pl.BlockSpec(memory_space=pl.ANY)
    return pl.pallas_call(
        body, name=name, out_shape=[jax.ShapeDtypeStruct(a.shape, a.dtype) for a in arrays],
        in_specs=[any_spec] * n, out_specs=[any_spec] * n,
        scratch_shapes=[pltpu.SemaphoreType.DMA((7 * n,)), pltpu.SemaphoreType.DMA((7 * n,)), pltpu.SemaphoreType.DMA((n,))],
    )(*arrays)


def _rope_tables(lx, lc):
    rows = lx // GRID_W
    row = jnp.repeat(jnp.arange(rows, dtype=F32), GRID_W)
    col = jnp.tile(jnp.arange(GRID_W, dtype=F32), rows)
    half = HEAD_DIM // 2
    inv_freq = ROPE_THETA ** (-jnp.arange(0, half, 2, dtype=F32) / half)
    ang = jnp.concatenate([row[:, None] * inv_freq, col[:, None] * inv_freq], axis=-1)
    cos, sin = jnp.cos(ang), jnp.sin(ang)
    cosf = jnp.repeat(cos, 2, axis=-1)
    sinf = jnp.stack([-sin, sin], axis=-1).reshape(lx, HEAD_DIM)
    cosf = jnp.concatenate([cosf, jnp.ones((lc, HEAD_DIM), F32)], axis=0)
    sinf = jnp.concatenate([sinf, jnp.zeros((lc, HEAD_DIM), F32)], axis=0)
    return cosf, sinf


def _cols_full(g):
    return jnp.transpose(g, (1, 0, 2)).reshape(g.shape[1], N_DEV * g.shape[2])


def _cols_split(w):
    k, n = w.shape
    return jnp.transpose(w.reshape(k, N_DEV, n // N_DEV), (1, 0, 2))


def _pad_rows(a, rows):
    return jnp.pad(a, ((0, rows - a.shape[0]), (0, 0)))


def kernel(x, c, ctx, c_ctx, w_ada, b_ada, ffn1_w_in, ffn1_w_out, mix_w_in, attn_q_gain, attn_k_gain, ret_decay_logit, w_proj_attn, w_proj_ret, mix_w_out, ffn2_w_in, ffn2_w_out, final_norm, loss_target, m_c_ctx, m_w_ada, m_b_ada, m_ffn1_w_in, m_ffn1_w_out, m_mix_w_in, m_attn_q_gain, m_attn_k_gain, m_ret_decay_logit, m_w_proj_attn, m_w_proj_ret, m_mix_w_out, m_ffn2_w_in, m_ffn2_w_out, m_final_norm, v_c_ctx, v_w_ada, v_b_ada, v_ffn1_w_in, v_ffn1_w_out, v_mix_w_in, v_attn_q_gain, v_attn_k_gain, v_ret_decay_logit, v_w_proj_attn, v_w_proj_ret, v_mix_w_out, v_ffn2_w_in, v_ffn2_w_out, v_final_norm):
    lx, d = x.shape[1], x.shape[2]
    lc = ctx.shape[1]
    t = lx + lc
    aw, rw = w_proj_attn.shape[1], w_proj_ret.shape[1]
    pw = mix_w_in.shape[2] * N_DEV
    kw = (pw - aw - 4 * rw - 2 * d) // 2
    groups = aw // kw
    heads_r = rw // HEAD_DIM
    ka_off, va_off = aw, aw + kw
    qr_off = aw + 2 * kw
    kr_off, vr_off, gr_off = qr_off + rw, qr_off + 2 * rw, qr_off + 3 * rw
    ga_off, gb_off = qr_off + 4 * rw, qr_off + 4 * rw + d
    xi, yi, ci = _coords()
    me = 4 * xi + 2 * yi + ci

    col_names = ["ffn1_w_in", "mix_w_in", "w_proj_attn", "w_proj_ret", "ffn2_w_in"]
    row_names = ["ffn1_w_out", "mix_w_out", "ffn2_w_out"]
    shard = dict(ffn1_w_in=ffn1_w_in[0], mix_w_in=mix_w_in[0], w_proj_attn=w_proj_attn[0], w_proj_ret=w_proj_ret[0],
                 ffn2_w_in=ffn2_w_in[0], ffn1_w_out=ffn1_w_out[0], mix_w_out=mix_w_out[0], ffn2_w_out=ffn2_w_out[0])
    names = col_names + row_names
    gathered = _all_gather_big("gather_weights", [shard[k].astype(BF16) for k in names])
    full = {}
    for k, g in zip(names, gathered):
        full[k] = _cols_full(g) if k in col_names else g.reshape(N_DEV * g.shape[1], g.shape[2])

    c_all = _all_gather_small("gather_cond", _pad_rows(c, SUBLANES))[:, 0, :]
    cond = _pad_rows(jnp.concatenate([c_all, c_ctx[None, :]], axis=0), 2 * SUBLANES)
    ada_part, s_cond = _ada_fwd("ada_fwd", cond, w_ada[0])
    ada_all = _all_gather_small("gather_ada", ada_part)
    mod_all = jnp.transpose(ada_all, (1, 0, 2)).reshape(2 * SUBLANES, N_MOD * d) + b_ada
    mod_x = lax.dynamic_index_in_dim(mod_all, me, axis=0, keepdims=False).reshape(N_MOD, d)
    mod_c = mod_all[N_DEV].reshape(N_MOD, d)
    mods = jnp.stack([mod_c, mod_x], axis=0)[:, :, None, :]
    sh1, sc1, g1, sh2, sc2, g2, sh3, sc3, g3 = [mods[:, k] for k in range(N_MOD)]

    h0 = jnp.concatenate([x[0], ctx[0]], axis=0)
    u1 = _rmsmod("ffn1_norm", h0, sc1, sh1, lx)
    z1, s1 = _ffn_in("ffn1_in", u1, full["ffn1_w_in"])
    h1, f1 = _mm_residual("ffn1_out", s1, full["ffn1_w_out"], h0, g1, 0.5, lx)

    u2 = _rmsmod("mix_norm", h1, sc2, sh2, lx)
    p = _mm_nn("mix_in", u2, full["mix_w_in"], BF16, tn_pref=512)
    cosf, sinf = _rope_tables(lx, lc)
    q_rot, k_rot = _qk_prep("qk_prep", p, cosf, sinf, attn_q_gain, attn_k_gain, aw, kw)
    ya, lse = _attn_fwd("attn_fwd", q_rot, k_rot, p, va_off, lx, groups)

    decay = ret_decay_logit[0].astype(F32)
    log_gamma = jax.nn.log_sigmoid(decay)
    r_offs = (qr_off, kr_off, vr_off, rw)
    tab = [_ret_tables(log_gamma[k], k) for k in range(2)]
    y_f, st_f = _ret_fwd("ret_fwd_a", p, r_offs, tab[0], 0, lx, None)
    y_r, st_b = _ret_fwd("ret_fwd_b", p, r_offs, tab[1], 1, lx, y_f)
    yr = _ret_out("ret_out", y_r, p, gr_off, lx)

    pa = _mm_nn("proj_attn", ya, full["w_proj_attn"], BF16)
    pr = _mm_nn("proj_ret", yr, full["w_proj_ret"], BF16)
    mg = _merge("merge", pa, pr, p, ga_off, gb_off)
    h2, o2 = _mm_residual("mix_out", mg, full["mix_w_out"], h1, g2, 1.0, lx)

    u3 = _rmsmod("ffn2_norm", h2, sc3, sh3, lx)
    z3, s3 = _ffn_in("ffn2_in", u3, full["ffn2_w_in"])
    h3, f3 = _mm_residual("ffn2_out", s3, full["ffn2_w_out"], h2, g3, 0.5, lx)
    dh3, d_fn, loss_tile = _final_loss("final_loss", h3, final_norm[None, :], loss_target[0])

    df3, dg3 = _gate_bwd("ffn2_gate_bwd", dh3, f3, g3, 0.5, lx)
    dz3 = _ffn_out_bwd("ffn2_out_bwd", df3, full["ffn2_w_out"], z3)
    gw = {}
    gw["ffn2_w_out"] = _mm_tn("ffn2_out_dw", s3, df3, BF16)
    du3 = _ffn_in_bwd_x("ffn2_in_bwd", dz3, full["ffn2_w_in"])
    gw["ffn2_w_in"] = _ffn_in_bwd_w("ffn2_in_dw", u3, dz3)
    dh2, dsc3, dsh3 = _rmsmod_bwd("ffn2_norm_bwd", du3, h2, sc3, dh3, lx, lx)

    do2, dg2 = _gate_bwd("mix_gate_bwd", dh2, o2, g2, 1.0, lx)
    dpa, dpr, dga, dgb = _merge_bwd("merge_bwd", do2, full["mix_w_out"], pa, pr, p, ga_off, gb_off)
    gw["mix_w_out"] = _mm_tn("mix_out_dw", mg, do2, BF16)
    dya = _mm_nt("proj_attn_bwd", dpa, full["w_proj_attn"], BF16)
    dyr = _mm_nt("proj_ret_bwd", dpr, full["w_proj_ret"], F32)
    gw["w_proj_attn"] = _mm_tn("proj_attn_dw", ya, dpa, BF16)
    gw["w_proj_ret"] = _mm_tn("proj_ret_dw", yr, dpr, BF16)

    dy_ret, dgr = _ret_out_bwd("ret_out_bwd", dyr, y_r, p, gr_off, lx)
    dqr, dkr, dvr, dl_f = _ret_bwd("ret_bwd_a", p, r_offs, tab[0], st_f, dy_ret, 0, lx, None)
    dqr, dkr, dvr, dl_b = _ret_bwd("ret_bwd_b", p, r_offs, tab[1], st_b, dy_ret, 1, lx, (dqr, dkr, dvr))
    d_lam = jnp.stack([dl_f[:, 0, 0], dl_b[:, 0, 0]], axis=0)
    d_decay = d_lam * jax.nn.sigmoid(-decay)

    dq_rot = _attn_bwd_q("attn_bwd_q", q_rot, k_rot, p, va_off, ya, dya, lse, lx, groups)
    dk_rot, dva = _attn_bwd_kv("attn_bwd_kv", q_rot, k_rot, p, va_off, ya, dya, lse, lx, groups)
    dqa, dka, d_qg, d_kg = _qk_prep_bwd("qk_prep_bwd", dq_rot, dk_rot, p, cosf, sinf, attn_q_gain, attn_k_gain, aw, kw)

    dp = jnp.concatenate([dqa, dka, dva, dqr.astype(BF16), dkr.astype(BF16), dvr.astype(BF16),
                          _pad_rows(dgr, t), _pad_rows(dga, t), _pad_rows(dgb, t)], axis=1)
    du2 = _mm_nt("mix_in_bwd", dp, full["mix_w_in"], F32)
    gw["mix_w_in"] = _mm_tn("mix_in_dw", u2, dp, BF16, tn_pref=512)
    dh1, dsc2, dsh2 = _rmsmod_bwd("mix_norm_bwd", du2, h1, sc2, dh2, lx, t)

    df1, dg1 = _gate_bwd("ffn1_gate_bwd", dh1, f1, g1, 0.5, lx)
    dz1 = _ffn_out_bwd("ffn1_out_bwd", df1, full["ffn1_w_out"], z1)
    gw["ffn1_w_out"] = _mm_tn("ffn1_out_dw", s1, df1, BF16)
    du1 = _ffn_in_bwd_x("ffn1_in_bwd", dz1, full["ffn1_w_in"])
    gw["ffn1_w_in"] = _ffn_in_bwd_w("ffn1_in_dw", u1, dz1)
    grad_x, dsc1, dsh1 = _rmsmod_bwd("ffn1_norm_bwd", du1, h0, sc1, dh1, lx, lx)

    zero = jnp.zeros((1, d), F32)
    dmod_c = jnp.concatenate([dsh1[0], dsc1[0], dg1[0], dsh2[0], dsc2[0], zero, zero, zero, zero], axis=0)
    dmod_x = jnp.concatenate([dsh1[1], dsc1[1], dg1[1], dsh2[1], dsc2[1], dg2[1], dsh3[1], dsc3[1], dg3[1]], axis=0)
    misc = jnp.concatenate([d_qg[0], d_kg[0], d_decay.reshape(-1), loss_tile[0, 0:1]])
    misc = jnp.pad(misc, (0, d - misc.shape[0]))[None, :]
    n_small = 3 * SUBLANES
    small = _pad_rows(jnp.concatenate([dmod_c, dmod_x, d_fn, misc], axis=0), n_small)
    small_all = _all_gather_small("gather_small", small)
    small_sum = _sum_slots("sum_small", small_all)
    dmod_c_sum, dmod_x_sum = small_sum[0:N_MOD], small_sum[N_MOD:2 * N_MOD]
    g_final_norm = small_sum[2 * N_MOD]
    misc_sum = small_sum[2 * N_MOD + 1]
    g_qg = misc_sum[0:HEAD_DIM][None, :]
    g_kg = misc_sum[HEAD_DIM:2 * HEAD_DIM][None, :]
    g_decay = misc_sum[2 * HEAD_DIM:2 * HEAD_DIM + 2 * heads_r].reshape(1, 2, heads_r)
    loss = misc_sum[2 * HEAD_DIM + 2 * heads_r]
    g_b_ada = (dmod_x_sum + dmod_c_sum).reshape(1, N_MOD * d)

    n_ada = w_ada.shape[2]
    dmod_rows = jnp.concatenate([small_all[:, N_MOD:2 * N_MOD, :].reshape(N_DEV, N_MOD * d),
                                 dmod_c_sum.reshape(1, N_MOD * d)], axis=0)
    dmod_mine = _pad_rows(lax.dynamic_slice_in_dim(dmod_rows, me * n_ada, n_ada, axis=1), 2 * SUBLANES)
    g_w_ada, ds_cond = _ada_bwd("ada_bwd", s_cond, dmod_mine, w_ada[0])
    cctx_parts = _all_gather_small("gather_cctx", ds_cond[N_DEV:N_DEV + SUBLANES])
    g_c_ctx = _cctx_grad("cctx_grad", cctx_parts, c_ctx[None, :])[0]

    send = [_cols_split(gw[k]) if k in col_names else gw[k].reshape(N_DEV, gw[k].shape[0] // N_DEV, gw[k].shape[1])
            for k in names]
    landed = _exchange_blocks("exchange_grads", send)
    mom = dict(ffn1_w_in=(m_ffn1_w_in, v_ffn1_w_in), mix_w_in=(m_mix_w_in, v_mix_w_in),
               w_proj_attn=(m_w_proj_attn, v_w_proj_attn), w_proj_ret=(m_w_proj_ret, v_w_proj_ret),
               ffn2_w_in=(m_ffn2_w_in, v_ffn2_w_in), ffn1_w_out=(m_ffn1_w_out, v_ffn1_w_out),
               mix_w_out=(m_mix_w_out, v_mix_w_out), ffn2_w_out=(m_ffn2_w_out, v_ffn2_w_out))
    res = {}
    for k, land in zip(names, landed):
        res[k] = _adamw("adamw_" + k, land, shard[k], mom[k][0][0], mom[k][1][0])
    res["w_ada"] = _adamw("adamw_w_ada", g_w_ada[None], w_ada[0], m_w_ada[0], v_w_ada[0])

    def pack(cc, ba, qg, kg, dec, fn):
        misc_row = jnp.concatenate([qg.reshape(-1), kg.reshape(-1), dec.reshape(-1)])
        misc_row = jnp.pad(misc_row, (0, d - misc_row.shape[0]))[None, :]
        return _pad_rows(jnp.concatenate([cc.reshape(1, d), ba.reshape(N_MOD, d), fn.reshape(1, d), misc_row], axis=0),
                         2 * SUBLANES)

    sg, sd, sm, sv = _adamw(
        "adamw_small", pack(g_c_ctx, g_b_ada, g_qg, g_kg, g_decay, g_final_norm)[None],
        pack(c_ctx, b_ada, attn_q_gain, attn_k_gain, ret_decay_logit, final_norm),
        pack(m_c_ctx, m_b_ada, m_attn_q_gain, m_attn_k_gain, m_ret_decay_logit, m_final_norm),
        pack(v_c_ctx, v_b_ada, v_attn_q_gain, v_attn_k_gain, v_ret_decay_logit, v_final_norm))

    def unpack(a):
        misc_row = a[N_MOD + 2]
        return dict(c_ctx=a[0], b_ada=a[1:1 + N_MOD].reshape(1, N_MOD * d), final_norm=a[N_MOD + 1],
                    attn_q_gain=misc_row[0:HEAD_DIM][None, :], attn_k_gain=misc_row[HEAD_DIM:2 * HEAD_DIM][None, :],
                    ret_decay_logit=misc_row[2 * HEAD_DIM:2 * HEAD_DIM + 2 * heads_r].reshape(1, 2, heads_r))

    small_out = [unpack(a) for a in (sg, sd, sm, sv)]
    order = ["c_ctx", "w_ada", "b_ada", "ffn1_w_in", "ffn1_w_out", "mix_w_in", "attn_q_gain", "attn_k_gain",
             "ret_decay_logit", "w_proj_attn", "w_proj_ret", "mix_w_out", "ffn2_w_in", "ffn2_w_out", "final_norm"]
    outs = [loss, grad_x[None]]
    for which in range(4):
        for k in order:
            outs.append(res[k][which][None] if k in res else small_out[which][k])
    return tuple(outs)
```

```python
import math

import jax
import jax.numpy as jnp
from jax import lax
from jax.experimental import pallas as pl
from jax.experimental.pallas import tpu as pltpu

F32 = jnp.float32
BF16 = jnp.bfloat16
MESH = pl.DeviceIdType.MESH

N_DEV = 8
HEAD_DIM = 128
GRID_W = 64
ROPE_THETA = 10000.0
NORM_EPS = 1e-6
RET_CHUNK = 128
N_MOD = 9
LANES = 128
SUBLANES = 8
V7X_VMEM_BYTES = 64 * 1024 * 1024
VMEM_LIMIT = V7X_VMEM_BYTES - 8 * 1024 * 1024
K_TILE = 2560
LOG2E = 1.4426950408889634
LN2 = 0.6931471805599453

ADAM_LR = 0.001
ADAM_B1 = 0.9
ADAM_B2 = 0.999
ADAM_EPS = 1e-08
ADAM_WD = 0.01
ADAM_STEP = 10

NN = (((1,), (0,)), ((), ()))
NT = (((1,), (1,)), ((), ()))
TN = (((0,), (0,)), ((), ()))


def _tile(n, pref, align=LANES):
    best = None
    t = align
    while t <= min(n, pref):
        if n % t == 0:
            best = t
        t += align
    return n if best is None else best


def _cp(sem):
    return pltpu.CompilerParams(dimension_semantics=sem, vmem_limit_bytes=VMEM_LIMIT)


def _sigmoid(v):
    return 1.0 / (1.0 + jnp.exp(-v))


def _dot(a, b, dims):
    return lax.dot_general(a, b, dims, preferred_element_type=F32)


def _mm(name, a, a_spec, b_list, dims, grid, out_shapes, out_specs, acc_shape, epi, extras=(), comm=None):
    nb, ne, no = len(b_list), len(extras), len(out_shapes)
    nk = grid[2]
    kind, c_arrays = comm if comm is not None else (None, [])
    ncm = len(c_arrays)

    def body(*refs):
        a_ref = refs[0]
        b_refs = refs[1:1 + nb]
        e_refs = refs[1 + nb:1 + nb + ne]
        pos = 1 + nb + ne
        c_ins = refs[pos:pos + ncm]
        o_refs = refs[pos + ncm:pos + ncm + no]
        c_outs = refs[pos + ncm + no:pos + 2 * ncm + no]
        scratch = refs[pos + 2 * ncm + no:]
        accs = scratch[:0 if nk == 1 else nb]
        ids = [pl.program_id(axis) for axis in range(3)]
        if ncm:
            start_comm, finish_comm = _comm_plan(kind, c_ins, c_outs, *scratch[len(accs):])
            pl.when(jnp.logical_and(jnp.logical_and(ids[0] == 0, ids[1] == 0), ids[2] == 0))(start_comm)

        def finish(tiles):
            vals = epi(tiles, e_refs)
            for o_ref, v in zip(o_refs, vals):
                if isinstance(v, tuple):
                    for idx, part in enumerate(v):
                        o_ref[idx] = part.astype(o_ref.dtype)
                else:
                    o_ref[...] = v.astype(o_ref.dtype)

        if nk == 1:
            finish([_dot(a_ref[...], b_ref[...], dims) for b_ref in b_refs])
        else:
            @pl.when(ids[2] == 0)
            def _():
                for acc in accs:
                    acc[...] = jnp.zeros(acc.shape, F32)

            av = a_ref[...]
            for b_ref, acc in zip(b_refs, accs):
                acc[...] += _dot(av, b_ref[...], dims)

            @pl.when(ids[2] == nk - 1)
            def _():
                finish([acc[...] for acc in accs])

        if ncm:
            pl.when(jnp.logical_and(jnp.logical_and(ids[0] == grid[0] - 1, ids[1] == grid[1] - 1),
                                    ids[2] == nk - 1))(finish_comm)

    any_spec = pl.BlockSpec(memory_space=pl.ANY)
    c_shapes = [jax.ShapeDtypeStruct(((N_DEV,) if kind == "gather" else ()) + s.shape, s.dtype) for s in c_arrays]
    scratch_shapes = [] if nk == 1 else [pltpu.VMEM(acc_shape, F32)] * nb
    if ncm:
        scratch_shapes = scratch_shapes + _comm_semaphores(ncm)
    semantics = ("arbitrary",) * 3 if ncm else ("parallel", "parallel", "arbitrary")
    outs = pl.pallas_call(
        body, name=name, grid=grid,
        in_specs=[a_spec] + [s for _, s in b_list] + [s for _, s in extras] + [any_spec] * ncm,
        out_specs=list(out_specs) + [any_spec] * ncm, out_shape=list(out_shapes) + c_shapes,
        scratch_shapes=scratch_shapes, compiler_params=_cp(semantics),
    )(a, *[b for b, _ in b_list], *[e for e, _ in extras], *c_arrays)
    main = outs[0] if no == 1 else tuple(outs[:no])
    return main if comm is None else (main, list(outs[no:]))


def _plain(accs, _):
    return (accs[0],)


def _mm_nn(name, a, b, out_dtype, tm_pref=1024, tn_pref=1024, tk_pref=K_TILE, comm=None):
    m, k = a.shape
    n = b.shape[1]
    tm, tn, tk = _tile(m, tm_pref), _tile(n, tn_pref), _tile(k, tk_pref)
    return _mm(name, a, pl.BlockSpec((tm, tk), lambda i, j, kk: (i, kk)),
               [(b, pl.BlockSpec((tk, tn), lambda i, j, kk: (kk, j)))], NN, (m // tm, n // tn, k // tk),
               [jax.ShapeDtypeStruct((m, n), out_dtype)], [pl.BlockSpec((tm, tn), lambda i, j, kk: (i, j))],
               (tm, tn), _plain, comm=comm)


def _mm_nt(name, a, b, out_dtype, tm_pref=1024, tn_pref=1024, tk_pref=K_TILE, comm=None):
    m, k = a.shape
    n = b.shape[0]
    tm, tn, tk = _tile(m, tm_pref), _tile(n, tn_pref), _tile(k, tk_pref)
    return _mm(name, a, pl.BlockSpec((tm, tk), lambda i, j, kk: (i, kk)),
               [(b, pl.BlockSpec((tn, tk), lambda i, j, kk: (j, kk)))], NT, (m // tm, n // tn, k // tk),
               [jax.ShapeDtypeStruct((m, n), out_dtype)], [pl.BlockSpec((tm, tn), lambda i, j, kk: (i, j))],
               (tm, tn), _plain, comm=comm)


def _mm_tn(name, a, b, out_dtype, rows=None, tm_pref=1024, tn_pref=1024, tk_pref=K_TILE, comm=None):
    k = a.shape[0] if rows is None else rows
    m, n = a.shape[1], b.shape[1]
    tm, tn, tk = _tile(m, tm_pref), _tile(n, tn_pref), _tile(k, tk_pref)
    return _mm(name, a, pl.BlockSpec((tk, tm), lambda i, j, kk: (kk, i)),
               [(b, pl.BlockSpec((tk, tn), lambda i, j, kk: (kk, j)))], TN, (m // tm, n // tn, k // tk),
               [jax.ShapeDtypeStruct((m, n), out_dtype)], [pl.BlockSpec((tm, tn), lambda i, j, kk: (i, j))],
               (tm, tn), _plain, comm=comm)


def _ffn_in(name, u, w_in, comm=None):
    r, d = u.shape
    f = w_in.shape[1] // 2
    tm, tn, tk = _tile(r, 1024), _tile(f, 512), _tile(d, K_TILE)
    nf = f // tn

    def epi(accs, _):
        za, zb = accs
        s = za * _sigmoid(za) * zb
        return (za, zb), s

    return _mm(name, u, pl.BlockSpec((tm, tk), lambda i, j, kk: (i, kk)),
               [(w_in, pl.BlockSpec((tk, tn), lambda i, j, kk: (kk, j))),
                (w_in, pl.BlockSpec((tk, tn), lambda i, j, kk: (kk, j + nf)))],
               NN, (r // tm, nf, d // tk),
               [jax.ShapeDtypeStruct((2, r, f), BF16), jax.ShapeDtypeStruct((r, f), BF16)],
               [pl.BlockSpec((2, tm, tn), lambda i, j, kk: (0, i, j)), pl.BlockSpec((tm, tn), lambda i, j, kk: (i, j))],
               (tm, tn), epi, comm=comm)


def _mm_residual(name, a, w, res, gate, gate_scale, lx, comm=None):
    r, k = a.shape
    n = w.shape[1]
    tm, tn, tk = _tile(r, 1024), _tile(n, 1024), _tile(k, K_TILE)

    def epi(accs, e_refs):
        res_ref, g_ref = e_refs
        rows = pl.program_id(0) * tm + lax.broadcasted_iota(jnp.int32, (tm, 1), 0)
        g = jnp.where(rows < lx, g_ref[1], g_ref[0])
        return res_ref[...] + gate_scale * g * accs[0], accs[0]

    return _mm(name, a, pl.BlockSpec((tm, tk), lambda i, j, kk: (i, kk)),
               [(w, pl.BlockSpec((tk, tn), lambda i, j, kk: (kk, j)))], NN, (r // tm, n // tn, k // tk),
               [jax.ShapeDtypeStruct((r, n), F32), jax.ShapeDtypeStruct((r, n), BF16)],
               [pl.BlockSpec((tm, tn), lambda i, j, kk: (i, j))] * 2, (tm, tn), epi,
               extras=[(res, pl.BlockSpec((tm, tn), lambda i, j, kk: (i, j))),
                       (gate, pl.BlockSpec((2, 1, tn), lambda i, j, kk: (0, 0, j)))], comm=comm)


def _ffn_out_bwd(name, df, w_out, z, comm=None):
    r, d = df.shape
    f = w_out.shape[0]
    tm, tn, tk = _tile(r, 1024), _tile(f, 512), _tile(d, K_TILE)

    def epi(accs, e_refs):
        ds = accs[0]
        za = e_refs[0][0].astype(F32)
        zb = e_refs[0][1].astype(F32)
        sg = _sigmoid(za)
        da = ds * zb * sg * (1.0 + za * (1.0 - sg))
        db = ds * za * sg
        return ((da, db),)

    zspec = pl.BlockSpec((2, tm, tn), lambda i, j, kk: (0, i, j))
    return _mm(name, df, pl.BlockSpec((tm, tk), lambda i, j, kk: (i, kk)),
               [(w_out, pl.BlockSpec((tn, tk), lambda i, j, kk: (j, kk)))], NT, (r // tm, f // tn, d // tk),
               [jax.ShapeDtypeStruct((2, r, f), BF16)], [zspec], (tm, tn), epi, extras=[(z, zspec)], comm=comm)


def _ffn_in_bwd_x(name, dz, w_in, comm=None):
    _, r, f = dz.shape
    d = w_in.shape[0]
    tm, tn, tk = _tile(r, 1024), _tile(d, 1024), _tile(f, K_TILE)
    nkf = f // tk
    return _mm(name, dz, pl.BlockSpec((None, tm, tk), lambda i, j, kk: (kk // nkf, i, kk % nkf)),
               [(w_in, pl.BlockSpec((tn, tk), lambda i, j, kk: (j, kk)))], NT, (r // tm, d // tn, 2 * nkf),
               [jax.ShapeDtypeStruct((r, d), F32)], [pl.BlockSpec((tm, tn), lambda i, j, kk: (i, j))],
               (tm, tn), _plain, comm=comm)


def _ffn_in_bwd_w(name, u, dz, comm=None):
    r, d = u.shape
    f = dz.shape[2]
    tm, tn, tk = _tile(d, 1024), _tile(f, 512), _tile(r, K_TILE)
    nf = f // tn
    return _mm(name, u, pl.BlockSpec((tk, tm), lambda i, j, kk: (kk, i)),
               [(dz, pl.BlockSpec((None, tk, tn), lambda i, j, kk: (j // nf, kk, j % nf)))], TN,
               (d // tm, 2 * nf, r // tk),
               [jax.ShapeDtypeStruct((d, 2 * f), BF16)], [pl.BlockSpec((tm, tn), lambda i, j, kk: (i, j))],
               (tm, tn), _plain, comm=comm)


def _merge_bwd(name, dout, w_out, pa, pr, p, ga_off, gb_off):
    r, d = dout.shape
    n = w_out.shape[0]
    cw = math.gcd(math.gcd(ga_off, gb_off), n)
    tm, tn, tk = _tile(r, 1024), _tile(cw, 512), _tile(d, K_TILE)

    def epi(accs, e_refs):
        dm = accs[0]
        pa_ref, pr_ref, ga_ref, gb_ref = e_refs
        sa = _sigmoid(ga_ref[...].astype(F32))
        sb = _sigmoid(gb_ref[...].astype(F32))
        pav = pa_ref[...].astype(F32)
        prv = pr_ref[...].astype(F32)
        return dm * sa, dm * sb, dm * pav * sa * (1.0 - sa), dm * prv * sb * (1.0 - sb)

    o_spec = pl.BlockSpec((tm, tn), lambda i, j, kk: (i, j))
    return _mm(name, dout, pl.BlockSpec((tm, tk), lambda i, j, kk: (i, kk)),
               [(w_out, pl.BlockSpec((tn, tk), lambda i, j, kk: (j, kk)))], NT, (r // tm, n // tn, d // tk),
               [jax.ShapeDtypeStruct((r, n), BF16)] * 4, [o_spec] * 4, (tm, tn), epi,
               extras=[(pa, o_spec), (pr, o_spec),
                       (p, pl.BlockSpec((tm, tn), lambda i, j, kk: (i, ga_off // tn + j))),
                       (p, pl.BlockSpec((tm, tn), lambda i, j, kk: (i, gb_off // tn + j)))])


def _row_tile(r, lx, d):
    pref = 256 if d > 1024 else 512
    return _tile(math.gcd(r, lx), pref, SUBLANES)


def _rmsmod(name, h, scale, shift, lx, rows=None):
    r = h.shape[0] if rows is None else rows
    d = h.shape[1]
    tr = _row_tile(r, lx, d)
    nx = lx // tr
    cls = lambda i: (jnp.where(i < nx, 1, 0), 0, 0)

    def body(h_ref, sc_ref, sh_ref, u_ref):
        hv = h_ref[...]
        rinv = lax.rsqrt(jnp.mean(hv * hv, axis=-1, keepdims=True) + NORM_EPS)
        u_ref[...] = (hv * rinv * (1.0 + sc_ref[...]) + sh_ref[...]).astype(u_ref.dtype)

    return pl.pallas_call(
        body, name=name, grid=(r // tr,),
        in_specs=[pl.BlockSpec((tr, d), lambda i: (i, 0)), pl.BlockSpec((None, 1, d), cls), pl.BlockSpec((None, 1, d), cls)],
        out_specs=pl.BlockSpec((tr, d), lambda i: (i, 0)), out_shape=jax.ShapeDtypeStruct((r, d), BF16),
        compiler_params=_cp(("parallel",)))(h, scale, shift)


def _rmsmod_bwd(name, du, h, scale, dh_in, lx, rows_out):
    r, d = du.shape
    rin = dh_in.shape[0]
    tr = _row_tile(math.gcd(r, math.gcd(rin, rows_out)), lx, d)
    nx, nin, nout = lx // tr, rin // tr, rows_out // tr
    cls = lambda i: (jnp.where(i < nx, 1, 0), 0, 0)

    def body(du_ref, h_ref, sc_ref, dhin_ref, dh_ref, dsc_ref, dsh_ref):
        i = pl.program_id(0)
        hv = h_ref[...]
        duv = du_ref[...]
        rinv = lax.rsqrt(jnp.mean(hv * hv, axis=-1, keepdims=True) + NORM_EPS)
        nv = hv * rinv
        dn = duv * (1.0 + sc_ref[...])

        @pl.when(jnp.logical_or(i == 0, i == nx))
        def _():
            dsc_ref[...] = jnp.zeros(dsc_ref.shape, F32)
            dsh_ref[...] = jnp.zeros(dsh_ref.shape, F32)

        dsc_ref[...] += jnp.sum(duv * nv, axis=0, keepdims=True)
        dsh_ref[...] += jnp.sum(duv, axis=0, keepdims=True)

        @pl.when(i < nout)
        def _():
            dh = rinv * (dn - nv * jnp.mean(dn * nv, axis=-1, keepdims=True))
            dh_ref[...] = dh + jnp.where(i < nin, dhin_ref[...], 0.0)

    return pl.pallas_call(
        body, name=name, grid=(r // tr,),
        in_specs=[pl.BlockSpec((tr, d), lambda i: (i, 0)), pl.BlockSpec((tr, d), lambda i: (i, 0)),
                  pl.BlockSpec((None, 1, d), cls), pl.BlockSpec((tr, d), lambda i: (jnp.minimum(i, nin - 1), 0))],
        out_specs=[pl.BlockSpec((tr, d), lambda i: (jnp.minimum(i, nout - 1), 0)),
                   pl.BlockSpec((None, 1, d), cls), pl.BlockSpec((None, 1, d), cls)],
        out_shape=[jax.ShapeDtypeStruct((rows_out, d), F32), jax.ShapeDtypeStruct((2, 1, d), F32),
                   jax.ShapeDtypeStruct((2, 1, d), F32)],
        compiler_params=_cp(("arbitrary",)))(du, h, scale, dh_in)


def _gate_bwd(name, dh, f, gate, gate_scale, lx):
    r, d = dh.shape
    tr = _row_tile(r, lx, d)
    nx = lx // tr
    cls = lambda i: (jnp.where(i < nx, 1, 0), 0, 0)

    def body(dh_ref, f_ref, g_ref, df_ref, dg_ref):
        i = pl.program_id(0)
        dhv = dh_ref[...]

        @pl.when(jnp.logical_or(i == 0, i == nx))
        def _():
            dg_ref[...] = jnp.zeros(dg_ref.shape, F32)

        df_ref[...] = (gate_scale * g_ref[...] * dhv).astype(df_ref.dtype)
        dg_ref[...] += jnp.sum(gate_scale * f_ref[...].astype(F32) * dhv, axis=0, keepdims=True)

    return pl.pallas_call(
        body, name=name, grid=(r // tr,),
        in_specs=[pl.BlockSpec((tr, d), lambda i: (i, 0)), pl.BlockSpec((tr, d), lambda i: (i, 0)),
                  pl.BlockSpec((None, 1, d), cls)],
        out_specs=[pl.BlockSpec((tr, d), lambda i: (i, 0)), pl.BlockSpec((None, 1, d), cls)],
        out_shape=[jax.ShapeDtypeStruct((r, d), BF16), jax.ShapeDtypeStruct((2, 1, d), F32)],
        compiler_params=_cp(("arbitrary",)))(dh, f, gate)


def _final_loss(name, h, final_norm, target):
    r, d = h.shape
    tr = _row_tile(r, r, d)

    def body(h_ref, fn_ref, t_ref, dh_ref, dfn_ref, loss_ref):
        i = pl.program_id(0)
        hv = h_ref[...]
        rinv = lax.rsqrt(jnp.mean(hv * hv, axis=-1, keepdims=True) + NORM_EPS)
        nv = hv * rinv
        fn = fn_ref[...]
        err = nv * fn - t_ref[...]
        dy = err * (1.0 / d)

        @pl.when(i == 0)
        def _():
            dfn_ref[...] = jnp.zeros(dfn_ref.shape, F32)
            loss_ref[...] = jnp.zeros(loss_ref.shape, F32)

        loss_ref[...] += 0.5 * jnp.sum(jnp.mean(err * err, axis=-1, keepdims=True), axis=0, keepdims=True)
        dfn_ref[...] += jnp.sum(dy * nv, axis=0, keepdims=True)
        dn = dy * fn
        dh_ref[...] = rinv * (dn - nv * jnp.mean(dn * nv, axis=-1, keepdims=True))

    return pl.pallas_call(
        body, name=name, grid=(r // tr,),
        in_specs=[pl.BlockSpec((tr, d), lambda i: (i, 0)), pl.BlockSpec((1, d), lambda i: (0, 0)),
                  pl.BlockSpec((tr, d), lambda i: (i, 0))],
        out_specs=[pl.BlockSpec((tr, d), lambda i: (i, 0)), pl.BlockSpec((1, d), lambda i: (0, 0)),
                   pl.BlockSpec((SUBLANES, LANES), lambda i: (0, 0))],
        out_shape=[jax.ShapeDtypeStruct((r, d), F32), jax.ShapeDtypeStruct((1, d), F32),
                   jax.ShapeDtypeStruct((SUBLANES, LANES), F32)],
        compiler_params=_cp(("arbitrary",)))(h, final_norm, target)


def _swap_pairs(t):
    lane = lax.broadcasted_iota(jnp.int32, t.shape, 1)
    nxt = pltpu.roll(t, HEAD_DIM - 1, 1)
    prv = pltpu.roll(t, 1, 1)
    return jnp.where(lane % 2 == 0, nxt, prv)


def _qk_prep(name, p, cosf, sinf, q_gain, k_gain, aw, kw):
    t = p.shape[0]
    tr = _tile(t, 256, SUBLANES)
    q_prescale = HEAD_DIM ** -0.5 * LOG2E

    def head_fwd(v, gain, cs, sn):
        v = v.astype(F32)
        rinv = lax.rsqrt(jnp.mean(v * v, axis=-1, keepdims=True) + NORM_EPS)
        tt = v * rinv * gain
        return tt * cs + _swap_pairs(tt) * sn

    def body(q_ref, k_ref, cos_ref, sin_ref, qg_ref, kg_ref, qo_ref, ko_ref):
        cs, sn = cos_ref[...], sin_ref[...]
        for hh in range(aw // HEAD_DIM):
            sl = slice(hh * HEAD_DIM, (hh + 1) * HEAD_DIM)
            qo_ref[:, sl] = (head_fwd(q_ref[:, sl], qg_ref[...], cs, sn) * q_prescale).astype(qo_ref.dtype)
        for hh in range(kw // HEAD_DIM):
            sl = slice(hh * HEAD_DIM, (hh + 1) * HEAD_DIM)
            ko_ref[:, sl] = head_fwd(k_ref[:, sl], kg_ref[...], cs, sn).astype(ko_ref.dtype)

    assert aw % kw == 0
    row = lambda i: (i, 0)
    return pl.pallas_call(
        body, name=name, grid=(t // tr,),
        in_specs=[pl.BlockSpec((tr, aw), row), pl.BlockSpec((tr, kw), lambda i: (i, aw // kw)),
                  pl.BlockSpec((tr, HEAD_DIM), row), pl.BlockSpec((tr, HEAD_DIM), row),
                  pl.BlockSpec((1, HEAD_DIM), lambda i: (0, 0)), pl.BlockSpec((1, HEAD_DIM), lambda i: (0, 0))],
        out_specs=[pl.BlockSpec((tr, aw), row), pl.BlockSpec((tr, kw), row)],
        out_shape=[jax.ShapeDtypeStruct((t, aw), BF16), jax.ShapeDtypeStruct((t, kw), BF16)],
        compiler_params=_cp(("parallel",)))(p, p, cosf, sinf, q_gain, k_gain)


def _qk_prep_bwd(name, dq_rot, dk_rot, p, cosf, sinf, q_gain, k_gain, aw, kw):
    t = p.shape[0]
    lq = dq_rot.shape[0]
    tr = _tile(math.gcd(t, lq), 256, SUBLANES)
    nq = lq // tr

    def head_bwd(dout, v, gain, cs, sn):
        v = v.astype(F32)
        rinv = lax.rsqrt(jnp.mean(v * v, axis=-1, keepdims=True) + NORM_EPS)
        vn = v * rinv
        dt = dout * cs - _swap_pairs(dout) * sn
        dvn = dt * gain
        dv = rinv * (dvn - vn * jnp.mean(dvn * vn, axis=-1, keepdims=True))
        return dv, jnp.sum(dt * vn, axis=0, keepdims=True)

    def body(dq_ref, dk_ref, q_ref, k_ref, cos_ref, sin_ref, qg_ref, kg_ref, dqo_ref, dko_ref, dqg_ref, dkg_ref):
        i = pl.program_id(0)
        cs, sn = cos_ref[...], sin_ref[...]

        @pl.when(i == 0)
        def _():
            dqg_ref[...] = jnp.zeros(dqg_ref.shape, F32)
            dkg_ref[...] = jnp.zeros(dkg_ref.shape, F32)

        has_q = i < nq
        for hh in range(aw // HEAD_DIM):
            sl = slice(hh * HEAD_DIM, (hh + 1) * HEAD_DIM)
            dout = jnp.where(has_q, dq_ref[:, sl], 0.0)
            dv, dg = head_bwd(dout, q_ref[:, sl], qg_ref[...], cs, sn)
            dqo_ref[:, sl] = dv.astype(dqo_ref.dtype)
            dqg_ref[...] += dg
        for hh in range(kw // HEAD_DIM):
            sl = slice(hh * HEAD_DIM, (hh + 1) * HEAD_DIM)
            dv, dg = head_bwd(dk_ref[:, sl], k_ref[:, sl], kg_ref[...], cs, sn)
            dko_ref[:, sl] = dv.astype(dko_ref.dtype)
            dkg_ref[...] += dg

    row = lambda i: (i, 0)
    one = lambda i: (0, 0)
    return pl.pallas_call(
        body, name=name, grid=(t // tr,),
        in_specs=[pl.BlockSpec((tr, aw), lambda i: (jnp.minimum(i, nq - 1), 0)), pl.BlockSpec((tr, kw), row),
                  pl.BlockSpec((tr, aw), row), pl.BlockSpec((tr, kw), lambda i: (i, aw // kw)),
                  pl.BlockSpec((tr, HEAD_DIM), row), pl.BlockSpec((tr, HEAD_DIM), row),
                  pl.BlockSpec((1, HEAD_DIM), one), pl.BlockSpec((1, HEAD_DIM), one)],
        out_specs=[pl.BlockSpec((tr, aw), row), pl.BlockSpec((tr, kw), row),
                   pl.BlockSpec((1, HEAD_DIM), one), pl.BlockSpec((1, HEAD_DIM), one)],
        out_shape=[jax.ShapeDtypeStruct((t, aw), BF16), jax.ShapeDtypeStruct((t, kw), BF16),
                   jax.ShapeDtypeStruct((1, HEAD_DIM), F32), jax.ShapeDtypeStruct((1, HEAD_DIM), F32)],
        compiler_params=_cp(("arbitrary",)))(dq_rot, dk_rot, p, p, cosf, sinf, q_gain, k_gain)


def _attn_tiles(lq, t):
    return _tile(lq, 512), _tile(t, 768)


def _attn_fwd(name, q, k, p, v_off, lq, groups):
    t = k.shape[0]
    hq = q.shape[1] // HEAD_DIM
    hkv = hq // groups
    gw = groups * HEAD_DIM
    tq, tk = _attn_tiles(lq, t)
    nkv = t // tk
    vb = v_off // HEAD_DIM

    def body(q_ref, k_ref, v_ref, o_ref, lse_ref, m_sc, l_sc, acc_sc):
        j = pl.program_id(2)

        @pl.when(j == 0)
        def _():
            m_sc[...] = jnp.full(m_sc.shape, -jnp.inf, F32)
            l_sc[...] = jnp.zeros(l_sc.shape, F32)
            acc_sc[...] = jnp.zeros(acc_sc.shape, F32)

        kv, vv = k_ref[...], v_ref[...]
        for g in range(groups):
            sl = slice(g * HEAD_DIM, (g + 1) * HEAD_DIM)
            s = _dot(q_ref[:, sl], kv, NT)
            m_prev = m_sc[g]
            m_new = jnp.maximum(m_prev, jnp.max(s, axis=-1, keepdims=True))
            alpha = jnp.exp2(m_prev - m_new)
            pexp = jnp.exp2(s - m_new)
            l_sc[g] = alpha * l_sc[g] + jnp.sum(pexp, axis=-1, keepdims=True)
            acc_sc[:, sl] = alpha * acc_sc[:, sl] + _dot(pexp.astype(BF16), vv, NN)
            m_sc[g] = m_new

        @pl.when(j == nkv - 1)
        def _():
            for g in range(groups):
                sl = slice(g * HEAD_DIM, (g + 1) * HEAD_DIM)
                o_ref[:, sl] = (acc_sc[:, sl] * (1.0 / l_sc[g])).astype(o_ref.dtype)
                lse_ref[g] = jnp.broadcast_to(m_sc[g] + jnp.log2(l_sc[g]), (tq, LANES))

    return pl.pallas_call(
        body, name=name, grid=(hkv, lq // tq, nkv),
        in_specs=[pl.BlockSpec((tq, gw), lambda h, i, j: (i, h)),
                  pl.BlockSpec((tk, HEAD_DIM), lambda h, i, j: (j, h)),
                  pl.BlockSpec((tk, HEAD_DIM), lambda h, i, j: (j, vb + h))],
        out_specs=[pl.BlockSpec((tq, gw), lambda h, i, j: (i, h)),
                   pl.BlockSpec((groups, tq, LANES), lambda h, i, j: (h, i, 0))],
        out_shape=[jax.ShapeDtypeStruct((lq, hq * HEAD_DIM), BF16), jax.ShapeDtypeStruct((hq, lq, LANES), F32)],
        scratch_shapes=[pltpu.VMEM((groups, tq, 1), F32), pltpu.VMEM((groups, tq, 1), F32), pltpu.VMEM((tq, gw), F32)],
        compiler_params=_cp(("parallel", "parallel", "arbitrary")))(q, k, p)


def _attn_bwd_q(name, q, k, p, v_off, o, do, lse, lq, groups):
    t = k.shape[0]
    hq = q.shape[1] // HEAD_DIM
    hkv = hq // groups
    gw = groups * HEAD_DIM
    tq, tk = _attn_tiles(lq, t)
    nkv = t // tk
    scale = HEAD_DIM ** -0.5
    vb = v_off // HEAD_DIM

    def body(q_ref, k_ref, v_ref, o_ref, do_ref, lse_ref, dq_ref, delta_sc, acc_sc):
        j = pl.program_id(2)

        @pl.when(j == 0)
        def _():
            for g in range(groups):
                sl = slice(g * HEAD_DIM, (g + 1) * HEAD_DIM)
                delta_sc[g] = jnp.sum(do_ref[:, sl].astype(F32) * o_ref[:, sl].astype(F32), axis=-1, keepdims=True)
            acc_sc[...] = jnp.zeros(acc_sc.shape, F32)

        kv, vv = k_ref[...], v_ref[...]
        for g in range(groups):
            sl = slice(g * HEAD_DIM, (g + 1) * HEAD_DIM)
            s = _dot(q_ref[:, sl], kv, NT)
            pexp = jnp.exp2(s - lse_ref[g, :, 0:1])
            dp = _dot(do_ref[:, sl], vv, NT)
            ds = pexp * (dp - delta_sc[g])
            acc_sc[:, sl] += _dot(ds.astype(BF16), kv, NN)

        @pl.when(j == nkv - 1)
        def _():
            dq_ref[...] = acc_sc[...] * scale

    qspec = pl.BlockSpec((tq, gw), lambda h, i, j: (i, h))
    return pl.pallas_call(
        body, name=name, grid=(hkv, lq // tq, nkv),
        in_specs=[qspec, pl.BlockSpec((tk, HEAD_DIM), lambda h, i, j: (j, h)),
                  pl.BlockSpec((tk, HEAD_DIM), lambda h, i, j: (j, vb + h)), qspec, qspec,
                  pl.BlockSpec((groups, tq, LANES), lambda h, i, j: (h, i, 0))],
        out_specs=qspec, out_shape=jax.ShapeDtypeStruct((lq, hq * HEAD_DIM), F32),
        scratch_shapes=[pltpu.VMEM((groups, tq, 1), F32), pltpu.VMEM((tq, gw), F32)],
        compiler_params=_cp(("parallel", "parallel", "arbitrary")))(q, k, p, o, do, lse)


def _attn_bwd_kv(name, q, k, p, v_off, o, do, lse, lq, groups):
    t = k.shape[0]
    hkv = k.shape[1] // HEAD_DIM
    gw = groups * HEAD_DIM
    tq, tk = _attn_tiles(lq, t)
    nq = lq // tq
    vb = v_off // HEAD_DIM

    def body(q_ref, k_ref, v_ref, o_ref, do_ref, lse_ref, dk_ref, dv_ref, dk_sc, dv_sc):
        i = pl.program_id(2)

        @pl.when(i == 0)
        def _():
            dk_sc[...] = jnp.zeros(dk_sc.shape, F32)
            dv_sc[...] = jnp.zeros(dv_sc.shape, F32)

        kv, vv = k_ref[...], v_ref[...]
        dk_part, dv_part = None, None
        for g in range(groups):
            sl = slice(g * HEAD_DIM, (g + 1) * HEAD_DIM)
            qv, dov = q_ref[:, sl], do_ref[:, sl]
            delta = jnp.sum(dov.astype(F32) * o_ref[:, sl].astype(F32), axis=-1, keepdims=True)
            s = _dot(qv, kv, NT)
            pexp = jnp.exp2(s - lse_ref[g, :, 0:1])
            dv_g = _dot(pexp.astype(BF16), dov, TN)
            dp = _dot(dov, vv, NT)
            ds = pexp * (dp - delta)
            dk_g = _dot(ds.astype(BF16), qv, TN)
            dk_part = dk_g if dk_part is None else dk_part + dk_g
            dv_part = dv_g if dv_part is None else dv_part + dv_g
        dk_sc[...] += dk_part
        dv_sc[...] += dv_part

        @pl.when(i == nq - 1)
        def _():
            dk_ref[...] = dk_sc[...] * LN2
            dv_ref[...] = dv_sc[...].astype(dv_ref.dtype)

    qspec = pl.BlockSpec((tq, gw), lambda kh, j, i: (i, kh))
    kspec = pl.BlockSpec((tk, HEAD_DIM), lambda kh, j, i: (j, kh))
    return pl.pallas_call(
        body, name=name, grid=(hkv, t // tk, nq),
        in_specs=[qspec, kspec, pl.BlockSpec((tk, HEAD_DIM), lambda kh, j, i: (j, vb + kh)), qspec, qspec,
                  pl.BlockSpec((groups, tq, LANES), lambda kh, j, i: (kh, i, 0))],
        out_specs=[kspec, kspec],
        out_shape=[jax.ShapeDtypeStruct((t, hkv * HEAD_DIM), F32), jax.ShapeDtypeStruct((t, hkv * HEAD_DIM), BF16)],
        scratch_shapes=[pltpu.VMEM((tk, HEAD_DIM), F32), pltpu.VMEM((tk, HEAD_DIM), F32)],
        compiler_params=_cp(("parallel", "parallel", "arbitrary")))(q, k, p, o, do, lse)


def _ret_tables(log_gamma, direction):
    c = RET_CHUNK
    idx = jnp.arange(c, dtype=F32)
    diff = idx[:, None] - idx[None, :]
    if direction == 1:
        diff = -diff
    keep = diff >= 0
    lg = log_gamma.astype(F32)
    mask = jnp.where(keep[None], jnp.exp(jnp.where(keep, diff, 0.0)[None] * lg[:, None, None]), 0.0)
    q_exp = idx + 1.0 if direction == 0 else c - idx
    k_exp = c - 1.0 - idx if direction == 0 else idx
    sign = 1.0 if direction == 0 else -1.0
    lane = lambda v: jnp.broadcast_to(v[..., None], v.shape + (LANES,))
    qdec = lane(jnp.exp(q_exp[None, :] * lg[:, None]))
    kdec = lane(jnp.exp(k_exp[None, :] * lg[:, None]))
    cdec = jnp.broadcast_to(jnp.exp(c * lg)[:, None, None], (lg.shape[0], SUBLANES, LANES))
    weights = lane(jnp.stack([sign * idx, q_exp, -sign * idx, k_exp], axis=0))
    return mask, qdec, kdec, cdec, weights


def _ret_chunk_of(direction, step, nx, nc):
    if direction == 0:
        return jnp.where(step < nc, nx + step, step - nc)
    return jnp.where(step < nc, nx + nc - 1 - step, nx + nc - 1 - step)


def _ret_fwd(name, p, offs, tables, direction, lx, prev):
    q_off, k_off, v_off, rw = offs
    t = p.shape[0]
    c = RET_CHUNK
    n_steps, nx = t // c, lx // c
    nc = n_steps - nx
    bw = math.gcd(math.gcd(q_off, k_off), math.gcd(v_off, rw))
    bw = _tile(bw, 512)
    hpb = bw // HEAD_DIM
    heads = rw // HEAD_DIM
    k_scale = HEAD_DIM ** -0.5
    mask, qdec, kdec, cdec, _ = tables
    rc = lambda n: _ret_chunk_of(direction, n, nx, nc)

    def body(*refs):
        if prev is None:
            q_ref, k_ref, v_ref, m_ref, qd_ref, kd_ref, cd_ref, y_ref, st_ref, s_sc = refs
        else:
            q_ref, k_ref, v_ref, m_ref, qd_ref, kd_ref, cd_ref, prev_ref, y_ref, st_ref, s_sc = refs
        n = pl.program_id(1)

        @pl.when(n == 0)
        def _():
            s_sc[...] = jnp.zeros(s_sc.shape, F32)

        for hh in range(hpb):
            sl = slice(hh * HEAD_DIM, (hh + 1) * HEAD_DIM)
            qv = q_ref[:, sl]
            kf = k_ref[:, sl].astype(F32) * k_scale
            kv = kf.astype(BF16)
            vv = v_ref[:, sl]
            state = s_sc[hh]
            st_ref[hh] = state
            a = _dot(qv, kv, NT) * m_ref[hh]
            y = _dot(a.astype(BF16), vv, NN) + _dot(qv, state.astype(BF16), NN) * qd_ref[hh]
            s_sc[hh] = state * cd_ref[hh, 0:1, :] + _dot((kf * kd_ref[hh]).astype(BF16), vv, TN)
            if prev is not None:
                y = y + prev_ref[:, sl]
            y_ref[:, sl] = y

    col = lambda off: (lambda g, n: (rc(n), off // bw + g))
    tab3 = lambda g, n: (g, 0, 0)
    in_specs = [pl.BlockSpec((c, bw), col(q_off)), pl.BlockSpec((c, bw), col(k_off)), pl.BlockSpec((c, bw), col(v_off)),
                pl.BlockSpec((hpb, c, c), tab3), pl.BlockSpec((hpb, c, LANES), tab3), pl.BlockSpec((hpb, c, LANES), tab3),
                pl.BlockSpec((hpb, SUBLANES, LANES), tab3)]
    args = [p, p, p, mask, qdec, kdec, cdec]
    aliases = {}
    yspec = pl.BlockSpec((c, bw), lambda g, n: (rc(n), g))
    if prev is not None:
        in_specs.append(yspec)
        args.append(prev)
        aliases = {len(args) - 1: 0}
    return pl.pallas_call(
        body, name=name, grid=(heads // hpb, n_steps), in_specs=in_specs,
        out_specs=[yspec, pl.BlockSpec((None, hpb, HEAD_DIM, HEAD_DIM), lambda g, n: (n, g, 0, 0))],
        out_shape=[jax.ShapeDtypeStruct((t, rw), F32), jax.ShapeDtypeStruct((n_steps, heads, HEAD_DIM, HEAD_DIM), F32)],
        scratch_shapes=[pltpu.VMEM((hpb, HEAD_DIM, HEAD_DIM), F32)], input_output_aliases=aliases,
        compiler_params=_cp(("parallel", "arbitrary")))(*args)


def _ret_bwd(name, p, offs, tables, states, dy, direction, lx, prev):
    q_off, k_off, v_off, rw = offs
    t = p.shape[0]
    c = RET_CHUNK
    n_steps, nx = t // c, lx // c
    nc = n_steps - nx
    bw = math.gcd(math.gcd(q_off, k_off), math.gcd(v_off, rw))
    bw = _tile(bw, 512)
    hpb = bw // HEAD_DIM
    heads = rw // HEAD_DIM
    k_scale = HEAD_DIM ** -0.5
    mask, qdec, kdec, cdec, weights = tables
    step_of = lambda n: n_steps - 1 - n
    rc = lambda n: _ret_chunk_of(direction, step_of(n), nx, nc)

    def body(*refs):
        if prev is None:
            (q_ref, k_ref, v_ref, dy_ref, st_ref, m_ref, qd_ref, kd_ref, cd_ref, w_ref,
             dq_ref, dk_ref, dv_ref, dl_ref, ds_sc, lam_sc) = refs
        else:
            (q_ref, k_ref, v_ref, dy_ref, st_ref, m_ref, qd_ref, kd_ref, cd_ref, w_ref, pq_ref, pk_ref, pv_ref,
             dq_ref, dk_ref, dv_ref, dl_ref, ds_sc, lam_sc) = refs
        n = pl.program_id(1)

        @pl.when(n == 0)
        def _():
            ds_sc[...] = jnp.zeros(ds_sc.shape, F32)
            lam_sc[...] = jnp.zeros(lam_sc.shape, F32)

        is_x = rc(n) < nx
        for hh in range(hpb):
            sl = slice(hh * HEAD_DIM, (hh + 1) * HEAD_DIM)
            qv = q_ref[:, sl]
            qf = qv.astype(F32)
            kf = k_ref[:, sl].astype(F32) * k_scale
            kv = kf.astype(BF16)
            vv = v_ref[:, sl]
            dyv = jnp.where(is_x, dy_ref[:, sl], 0.0).astype(BF16)
            state = st_ref[hh]
            dstate = ds_sc[hh]
            dstate_b = dstate.astype(BF16)
            msk, qd, kd = m_ref[hh], qd_ref[hh], kd_ref[hh]
            cd = cd_ref[hh, 0:1, :]
            a = _dot(qv, kv, NT) * msk
            da = (_dot(dyv, vv, NT) * msk).astype(BF16)
            dq_intra = _dot(da, kv, NN)
            dk_intra = _dot(da, qv, TN)
            dq_inter = _dot(dyv, state.astype(BF16), NT) * qd
            dk_inter = _dot(vv, dstate_b, NT) * kd
            dv = _dot(a.astype(BF16), dyv, TN) + _dot((kf * kd).astype(BF16), dstate_b, NN)
            lam_sc[hh] += (qf * (w_ref[0] * dq_intra + w_ref[1] * dq_inter)
                           + kf * (w_ref[2] * dk_intra + w_ref[3] * dk_inter)
                           + (c * cd) * state * dstate)
            ds_sc[hh] = _dot((qf * qd).astype(BF16), dyv, TN) + cd * dstate
            dq = dq_intra + dq_inter
            dk = (dk_intra + dk_inter) * k_scale
            if prev is not None:
                dq = dq + pq_ref[:, sl]
                dk = dk + pk_ref[:, sl]
                dv = dv + pv_ref[:, sl]
            dq_ref[:, sl] = dq
            dk_ref[:, sl] = dk
            dv_ref[:, sl] = dv

        @pl.when(n == n_steps - 1)
        def _():
            for hh in range(hpb):
                dl_ref[hh] = jnp.broadcast_to(jnp.sum(lam_sc[hh]), (SUBLANES, LANES))

    col = lambda off: (lambda g, n: (rc(n), off // bw + g))
    tab3 = lambda g, n: (g, 0, 0)
    ospec = pl.BlockSpec((c, bw), lambda g, n: (rc(n), g))
    in_specs = [pl.BlockSpec((c, bw), col(q_off)), pl.BlockSpec((c, bw), col(k_off)), pl.BlockSpec((c, bw), col(v_off)),
                pl.BlockSpec((c, bw), lambda g, n: (jnp.minimum(rc(n), nx - 1), g)),
                pl.BlockSpec((None, hpb, HEAD_DIM, HEAD_DIM), lambda g, n: (step_of(n), g, 0, 0)),
                pl.BlockSpec((hpb, c, c), tab3), pl.BlockSpec((hpb, c, LANES), tab3), pl.BlockSpec((hpb, c, LANES), tab3),
                pl.BlockSpec((hpb, SUBLANES, LANES), tab3), pl.BlockSpec((4, c, LANES), lambda g, n: (0, 0, 0))]
    args = [p, p, p, dy, states, mask, qdec, kdec, cdec, weights]
    aliases = {}
    if prev is not None:
        for k_out, arr in enumerate(prev):
            in_specs.append(ospec)
            args.append(arr)
            aliases[len(args) - 1] = k_out
    big = jax.ShapeDtypeStruct((t, rw), F32)
    return pl.pallas_call(
        body, name=name, grid=(heads // hpb, n_steps), in_specs=in_specs,
        out_specs=[ospec, ospec, ospec, pl.BlockSpec((hpb, SUBLANES, LANES), tab3)],
        out_shape=[big, big, big, jax.ShapeDtypeStruct((heads, SUBLANES, LANES), F32)],
        scratch_shapes=[pltpu.VMEM((hpb, HEAD_DIM, HEAD_DIM), F32), pltpu.VMEM((hpb, HEAD_DIM, HEAD_DIM), F32)],
        input_output_aliases=aliases, compiler_params=_cp(("parallel", "arbitrary")))(*args)


def _ret_out(name, y, p, g_off, lx):
    rw = y.shape[1]
    bw = _tile(math.gcd(g_off, rw), 512)
    tr = _tile(lx, 512, SUBLANES)

    def body(y_ref, g_ref, o_ref):
        for hh in range(bw // HEAD_DIM):
            sl = slice(hh * HEAD_DIM, (hh + 1) * HEAD_DIM)
            yv = y_ref[:, sl]
            gv = g_ref[:, sl].astype(F32)
            rinv = lax.rsqrt(jnp.mean(yv * yv, axis=-1, keepdims=True) + NORM_EPS)
            o_ref[:, sl] = (gv * _sigmoid(gv) * yv * rinv).astype(o_ref.dtype)

    spec = pl.BlockSpec((tr, bw), lambda i, g: (i, g))
    return pl.pallas_call(
        body, name=name, grid=(lx // tr, rw // bw),
        in_specs=[spec, pl.BlockSpec((tr, bw), lambda i, g: (i, g_off // bw + g))],
        out_specs=spec, out_shape=jax.ShapeDtypeStruct((lx, rw), BF16),
        compiler_params=_cp(("parallel", "parallel")))(y, p)


def _ret_out_bwd(name, dyr, y, p, g_off, lx):
    rw = y.shape[1]
    bw = _tile(math.gcd(g_off, rw), 512)
    tr = _tile(lx, 512, SUBLANES)

    def body(d_ref, y_ref, g_ref, dy_ref, dg_ref):
        for hh in range(bw // HEAD_DIM):
            sl = slice(hh * HEAD_DIM, (hh + 1) * HEAD_DIM)
            yv = y_ref[:, sl]
            gv = g_ref[:, sl].astype(F32)
            dv = d_ref[:, sl]
            rinv = lax.rsqrt(jnp.mean(yv * yv, axis=-1, keepdims=True) + NORM_EPS)
            yn = yv * rinv
            sg = _sigmoid(gv)
            dg_ref[:, sl] = (dv * yn * sg * (1.0 + gv * (1.0 - sg))).astype(dg_ref.dtype)
            dyn = dv * gv * sg
            dy_ref[:, sl] = rinv * (dyn - yn * jnp.mean(dyn * yn, axis=-1, keepdims=True))

    spec = pl.BlockSpec((tr, bw), lambda i, g: (i, g))
    return pl.pallas_call(
        body, name=name, grid=(lx // tr, rw // bw),
        in_specs=[spec, spec, pl.BlockSpec((tr, bw), lambda i, g: (i, g_off // bw + g))],
        out_specs=[spec, spec],
        out_shape=[jax.ShapeDtypeStruct((lx, rw), F32), jax.ShapeDtypeStruct((lx, rw), BF16)],
        compiler_params=_cp(("parallel", "parallel")))(dyr, y, p)


def _merge(name, pa, pr, p, ga_off, gb_off):
    r, d = pa.shape
    cw = _tile(math.gcd(math.gcd(ga_off, gb_off), d), 1024)
    tr = _tile(r, 512, SUBLANES)

    def body(pa_ref, pr_ref, ga_ref, gb_ref, o_ref):
        o_ref[...] = (_sigmoid(ga_ref[...].astype(F32)) * pa_ref[...].astype(F32)
                      + _sigmoid(gb_ref[...].astype(F32)) * pr_ref[...].astype(F32)).astype(o_ref.dtype)

    spec = pl.BlockSpec((tr, cw), lambda i, j: (i, j))
    return pl.pallas_call(
        body, name=name, grid=(r // tr, d // cw),
        in_specs=[spec, spec, pl.BlockSpec((tr, cw), lambda i, j: (i, ga_off // cw + j)),
                  pl.BlockSpec((tr, cw), lambda i, j: (i, gb_off // cw + j))],
        out_specs=spec, out_shape=jax.ShapeDtypeStruct((r, d), BF16),
        compiler_params=_cp(("parallel", "parallel")))(pa, pr, p, p)


def _ada_fwd(name, cond, w):
    rows, d = cond.shape
    n = w.shape[1]
    tn = _tile(n, 768)

    def body(c_ref, w_ref, o_ref, s_ref):
        cv = c_ref[...]
        sv = cv * _sigmoid(cv)
        s_ref[...] = sv
        o_ref[...] = _dot(sv.astype(BF16), w_ref[...].astype(BF16), NN)

    return pl.pallas_call(
        body, name=name, grid=(n // tn,),
        in_specs=[pl.BlockSpec((rows, d), lambda j: (0, 0)), pl.BlockSpec((d, tn), lambda j: (0, j))],
        out_specs=[pl.BlockSpec((rows, tn), lambda j: (0, j)), pl.BlockSpec((rows, d), lambda j: (0, 0))],
        out_shape=[jax.ShapeDtypeStruct((rows, n), F32), jax.ShapeDtypeStruct((rows, d), F32)],
        compiler_params=_cp(("arbitrary",)))(cond, w)


def _ada_bwd(name, s_cond, dmod, w):
    rows, d = s_cond.shape
    n = w.shape[1]
    tn = _tile(n, 768)

    def body(s_ref, dm_ref, w_ref, gw_ref, ds_ref):
        j = pl.program_id(0)

        @pl.when(j == 0)
        def _():
            ds_ref[...] = jnp.zeros(ds_ref.shape, F32)

        dmv = dm_ref[...].astype(BF16)
        gw_ref[...] = _dot(s_ref[...].astype(BF16), dmv, TN)
        ds_ref[...] += _dot(dmv, w_ref[...].astype(BF16), NT)

    return pl.pallas_call(
        body, name=name, grid=(n // tn,),
        in_specs=[pl.BlockSpec((rows, d), lambda j: (0, 0)), pl.BlockSpec((rows, tn), lambda j: (0, j)),
                  pl.BlockSpec((d, tn), lambda j: (0, j))],
        out_specs=[pl.BlockSpec((d, tn), lambda j: (0, j)), pl.BlockSpec((rows, d), lambda j: (0, 0))],
        out_shape=[jax.ShapeDtypeStruct((d, n), F32), jax.ShapeDtypeStruct((rows, d), F32)],
        compiler_params=_cp(("arbitrary",)))(s_cond, dmod, w)


def _sum_slots(name, a):
    s, r, c = a.shape

    def body(a_ref, o_ref):
        acc = a_ref[0]
        for k in range(1, s):
            acc = acc + a_ref[k]
        o_ref[...] = acc

    return pl.pallas_call(
        body, name=name, grid=(1,), in_specs=[pl.BlockSpec((s, r, c), lambda i: (0, 0, 0))],
        out_specs=pl.BlockSpec((r, c), lambda i: (0, 0)), out_shape=jax.ShapeDtypeStruct((r, c), F32),
        compiler_params=_cp(("arbitrary",)))(a)


def _cctx_grad(name, parts, c_ctx):
    s, r, d = parts.shape

    def body(p_ref, c_ref, o_ref):
        acc = p_ref[0]
        for k in range(1, s):
            acc = acc + p_ref[k]
        cv = c_ref[...]
        sg = _sigmoid(cv)
        o_ref[...] = acc[0:1, :] * sg * (1.0 + cv * (1.0 - sg))

    return pl.pallas_call(
        body, name=name, grid=(1,),
        in_specs=[pl.BlockSpec((s, r, d), lambda i: (0, 0, 0)), pl.BlockSpec((1, d), lambda i: (0, 0))],
        out_specs=pl.BlockSpec((1, d), lambda i: (0, 0)), out_shape=jax.ShapeDtypeStruct((1, d), F32),
        compiler_params=_cp(("arbitrary",)))(parts, c_ctx)


def _adamw(name, slots, w, m, v):
    s, r, c = slots.shape
    tr = _tile(r, 256 if c > 1024 else 512, SUBLANES)
    c1 = 1.0 - ADAM_B1 ** ADAM_STEP
    c2 = 1.0 - ADAM_B2 ** ADAM_STEP

    def body(s_ref, w_ref, m_ref, v_ref, g_ref, d_ref, mo_ref, vo_ref):
        g = s_ref[0].astype(F32)
        for k in range(1, s):
            g = g + s_ref[k].astype(F32)
        mn = ADAM_B1 * m_ref[...] + (1.0 - ADAM_B1) * g
        vn = ADAM_B2 * v_ref[...] + (1.0 - ADAM_B2) * (g * g)
        m_hat = mn / c1
        v_hat = vn / c2
        g_ref[...] = g
        mo_ref[...] = mn
        vo_ref[...] = vn
        d_ref[...] = -ADAM_LR * (m_hat / (jnp.sqrt(v_hat) + ADAM_EPS) + ADAM_WD * w_ref[...])

    spec = pl.BlockSpec((tr, c), lambda i: (i, 0))
    shp = jax.ShapeDtypeStruct((r, c), F32)
    return pl.pallas_call(
        body, name=name, grid=(r // tr,),
        in_specs=[pl.BlockSpec((s, tr, c), lambda i: (0, i, 0)), spec, spec, spec],
        out_specs=[spec] * 4, out_shape=[shp] * 4, compiler_params=_cp(("parallel",)))(slots, w, m, v)


def _coords():
    return lax.axis_index("x"), lax.axis_index("y"), lax.axis_index("c")


def _flip(coord, bit):
    return 1 - coord if bit else coord


def _all_gather_small(name, blk):
    r, ccols = blk.shape

    def body(x_ref, out_ref, send_sems, recv_sems, local_sem):
        x, y, c = _coords()
        me, sibling = (x, y, c), (x, y, 1 - c)
        chips = [(1 - x, y), (x, 1 - y), (1 - x, 1 - y)]

        def slot(px, py, pc):
            return out_ref.at[4 * px + 2 * py + pc]

        def copy(k, block, to, src=None):
            return pltpu.make_async_remote_copy(
                src_ref=slot(*block) if src is None else src, dst_ref=slot(*block),
                send_sem=send_sems.at[k], recv_sem=recv_sems.at[k], device_id=to, device_id_type=MESH)

        mine = pltpu.make_async_copy(x_ref, slot(*me), local_sem)
        mine.start()
        first = [copy(0, me, sibling, src=x_ref)]
        first += [copy(1 + j, me, (*chip, c), src=x_ref) for j, chip in enumerate(chips)]
        for cp in first:
            cp.start()
        passed = [copy(4 + j, (*chip, c), sibling) for j, chip in enumerate(chips)]
        for j, chip in enumerate(chips):
            copy(1 + j, (*chip, c), me).wait_recv()
            passed[j].start()
        copy(0, sibling, me).wait_recv()
        for j, chip in enumerate(chips):
            copy(4 + j, (*chip, 1 - c), me).wait_recv()
        for cp in first + passed:
            cp.wait_send()
        mine.wait()

    return pl.pallas_call(
        body, name=name, out_shape=jax.ShapeDtypeStruct((N_DEV, r, ccols), blk.dtype),
        in_specs=[pl.BlockSpec(memory_space=pltpu.VMEM)], out_specs=pl.BlockSpec(memory_space=pltpu.VMEM),
        scratch_shapes=[pltpu.SemaphoreType.DMA((7,)), pltpu.SemaphoreType.DMA((7,)), pltpu.SemaphoreType.DMA],
    )(blk)


def _comm_semaphores(n):
    return [pltpu.SemaphoreType.DMA((7 * n,)), pltpu.SemaphoreType.DMA((7 * n,)), pltpu.SemaphoreType.DMA((n,))]


def _gather_plan(ins, outs, send_sems, recv_sems, local_sems):
    n = len(ins)
    x, y, c = _coords()
    me, sibling = (x, y, c), (x, y, 1 - c)
    chips = [(1 - x, y), (x, 1 - y), (1 - x, 1 - y)]

    def slot(a, px, py, pc):
        return outs[a].at[4 * px + 2 * py + pc]

    def copy(a, k, block, to, src=None):
        return pltpu.make_async_remote_copy(
            src_ref=slot(a, *block) if src is None else src, dst_ref=slot(a, *block),
            send_sem=send_sems.at[7 * a + k], recv_sem=recv_sems.at[7 * a + k], device_id=to, device_id_type=MESH)

    def local(a):
        return pltpu.make_async_copy(ins[a], slot(a, *me), local_sems.at[a])

    def first(a):
        return [copy(a, 0, me, sibling, src=ins[a])] + [copy(a, 1 + j, me, (*chip, c), src=ins[a])
                                                        for j, chip in enumerate(chips)]

    def passed(a, j):
        return copy(a, 4 + j, (*chips[j], c), sibling)

    def start():
        for a in range(n):
            local(a).start()
            for cp in first(a):
                cp.start()

    def finish():
        for a in range(n):
            for j, chip in enumerate(chips):
                copy(a, 1 + j, (*chip, c), me).wait_recv()
                passed(a, j).start()
        for a in range(n):
            copy(a, 0, sibling, me).wait_recv()
            for j, chip in enumerate(chips):
                copy(a, 4 + j, (*chip, 1 - c), me).wait_recv()
        for a in range(n):
            for cp in first(a) + [passed(a, j) for j in range(3)]:
                cp.wait_send()
            local(a).wait()

    return start, finish


def _exchange_plan(ins, outs, send_sems, recv_sems, local_sems):
    n = len(ins)
    x, y, c = _coords()
    my_idx = 4 * x + 2 * y + c

    def local(a):
        return pltpu.make_async_copy(ins[a].at[my_idx], outs[a].at[my_idx], local_sems.at[a])

    def pair(a, rel):
        px, py, pc = _flip(x, rel & 4), _flip(y, rel & 2), _flip(c, rel & 1)
        peer_idx = 4 * px + 2 * py + pc
        sems = dict(send_sem=send_sems.at[7 * a + rel - 1], recv_sem=recv_sems.at[7 * a + rel - 1],
                    device_id=(px, py, pc), device_id_type=MESH)
        send = pltpu.make_async_remote_copy(src_ref=ins[a].at[peer_idx], dst_ref=outs[a].at[my_idx], **sems)
        recv = pltpu.make_async_remote_copy(src_ref=ins[a].at[my_idx], dst_ref=outs[a].at[peer_idx], **sems)
        return send, recv

    def start():
        for a in range(n):
            local(a).start()
            for rel in range(1, N_DEV):
                pair(a, rel)[0].start()

    def finish():
        for a in range(n):
            for rel in range(1, N_DEV):
                send, recv = pair(a, rel)
                recv.wait_recv()
                send.wait_send()
            local(a).wait()

    return start, finish


def _comm_plan(kind, ins, outs, send_sems, recv_sems, local_sems):
    plan = {"gather": _gather_plan, "exchange": _exchange_plan}[kind]
    return plan(ins, outs, send_sems, recv_sems, local_sems)


def _all_gather_big(name, shards):
    n = len(shards)

    def body(*refs):
        start, finish = _gather_plan(refs[:n], refs[n:2 * n], *refs[2 * n:])
        start()
        finish()

    any_spec = pl.BlockSpec(memory_space=pl.ANY)
    return pl.pallas_call(
        body, name=name, out_shape=[jax.ShapeDtypeStruct((N_DEV,) + s.shape, s.dtype) for s in shards],
        in_specs=[any_spec] * n, out_specs=[any_spec] * n, scratch_shapes=_comm_semaphores(n))(*shards)


def _rope_tables(lx, lc):
    rows = lx // GRID_W
    row = jnp.repeat(jnp.arange(rows, dtype=F32), GRID_W)
    col = jnp.tile(jnp.arange(GRID_W, dtype=F32), rows)
    half = HEAD_DIM // 2
    inv_freq = ROPE_THETA ** (-jnp.arange(0, half, 2, dtype=F32) / half)
    ang = jnp.concatenate([row[:, None] * inv_freq, col[:, None] * inv_freq], axis=-1)
    cos, sin = jnp.cos(ang), jnp.sin(ang)
    cosf = jnp.repeat(cos, 2, axis=-1)
    sinf = jnp.stack([-sin, sin], axis=-1).reshape(lx, HEAD_DIM)
    cosf = jnp.concatenate([cosf, jnp.ones((lc, HEAD_DIM), F32)], axis=0)
    sinf = jnp.concatenate([sinf, jnp.zeros((lc, HEAD_DIM), F32)], axis=0)
    return cosf, sinf


def _cols_full(g):
    return jnp.transpose(g, (1, 0, 2)).reshape(g.shape[1], N_DEV * g.shape[2])


def _cols_split(w):
    k, n = w.shape
    return jnp.transpose(w.reshape(k, N_DEV, n // N_DEV), (1, 0, 2))


def _pad_rows(a, rows):
    return jnp.pad(a, ((0, rows - a.shape[0]), (0, 0)))


def kernel(x, c, ctx, c_ctx, w_ada, b_ada, ffn1_w_in, ffn1_w_out, mix_w_in, attn_q_gain, attn_k_gain, ret_decay_logit, w_proj_attn, w_proj_ret, mix_w_out, ffn2_w_in, ffn2_w_out, final_norm, loss_target, m_c_ctx, m_w_ada, m_b_ada, m_ffn1_w_in, m_ffn1_w_out, m_mix_w_in, m_attn_q_gain, m_attn_k_gain, m_ret_decay_logit, m_w_proj_attn, m_w_proj_ret, m_mix_w_out, m_ffn2_w_in, m_ffn2_w_out, m_final_norm, v_c_ctx, v_w_ada, v_b_ada, v_ffn1_w_in, v_ffn1_w_out, v_mix_w_in, v_attn_q_gain, v_attn_k_gain, v_ret_decay_logit, v_w_proj_attn, v_w_proj_ret, v_mix_w_out, v_ffn2_w_in, v_ffn2_w_out, v_final_norm):
    lx, d = x.shape[1], x.shape[2]
    lc = ctx.shape[1]
    t = lx + lc
    aw, rw = w_proj_attn.shape[1], w_proj_ret.shape[1]
    pw = mix_w_in.shape[2] * N_DEV
    kw = (pw - aw - 4 * rw - 2 * d) // 2
    groups = aw // kw
    heads_r = rw // HEAD_DIM
    ka_off, va_off = aw, aw + kw
    qr_off = aw + 2 * kw
    kr_off, vr_off, gr_off = qr_off + rw, qr_off + 2 * rw, qr_off + 3 * rw
    ga_off, gb_off = qr_off + 4 * rw, qr_off + 4 * rw + d
    xi, yi, ci = _coords()
    me = 4 * xi + 2 * yi + ci

    col_names = ["ffn1_w_in", "mix_w_in", "w_proj_attn", "w_proj_ret", "ffn2_w_in"]
    row_names = ["ffn1_w_out", "mix_w_out", "ffn2_w_out"]
    shard = dict(ffn1_w_in=ffn1_w_in[0], mix_w_in=mix_w_in[0], w_proj_attn=w_proj_attn[0], w_proj_ret=w_proj_ret[0],
                 ffn2_w_in=ffn2_w_in[0], ffn1_w_out=ffn1_w_out[0], mix_w_out=mix_w_out[0], ffn2_w_out=ffn2_w_out[0])
    names = col_names + row_names
    bf_shard = {k: shard[k].astype(BF16) for k in names}
    full, landed = {}, {}

    def gather_of(keys):
        return "gather", [bf_shard[k] for k in keys]

    def keep_full(keys, gathered):
        for k, g in zip(keys, gathered):
            full[k] = _cols_full(g) if k in col_names else g.reshape(N_DEV * g.shape[1], g.shape[2])

    def exchange_of(grads):
        return "exchange", [_cols_split(g) if k in col_names else g.reshape(N_DEV, g.shape[0] // N_DEV, g.shape[1])
                            for k, g in grads.items()]

    def keep_landed(grads, got):
        landed.update(zip(grads.keys(), got))

    keep_full(["ffn1_w_in"], _all_gather_big("gather_first", [bf_shard["ffn1_w_in"]]))

    c_all = _all_gather_small("gather_cond", _pad_rows(c, SUBLANES))[:, 0, :]
    cond = _pad_rows(jnp.concatenate([c_all, c_ctx[None, :]], axis=0), 2 * SUBLANES)
    ada_part, s_cond = _ada_fwd("ada_fwd", cond, w_ada[0])
    ada_all = _all_gather_small("gather_ada", ada_part)
    mod_all = jnp.transpose(ada_all, (1, 0, 2)).reshape(2 * SUBLANES, N_MOD * d) + b_ada
    mod_x = lax.dynamic_index_in_dim(mod_all, me, axis=0, keepdims=False).reshape(N_MOD, d)
    mod_c = mod_all[N_DEV].reshape(N_MOD, d)
    mods = jnp.stack([mod_c, mod_x], axis=0)[:, :, None, :]
    sh1, sc1, g1, sh2, sc2, g2, sh3, sc3, g3 = [mods[:, k] for k in range(N_MOD)]

    h0 = jnp.concatenate([x[0], ctx[0]], axis=0)
    u1 = _rmsmod("ffn1_norm", h0, sc1, sh1, lx)
    keys = ["ffn1_w_out", "mix_w_in"]
    (z1, s1), got = _ffn_in("ffn1_in", u1, full["ffn1_w_in"], comm=gather_of(keys))
    keep_full(keys, got)
    keys = ["w_proj_attn", "w_proj_ret", "mix_w_out", "ffn2_w_out"]
    (h1, f1), got = _mm_residual("ffn1_out", s1, full["ffn1_w_out"], h0, g1, 0.5, lx, comm=gather_of(keys))
    keep_full(keys, got)

    u2 = _rmsmod("mix_norm", h1, sc2, sh2, lx)
    keys = ["ffn2_w_in"]
    p, got = _mm_nn("mix_in", u2, full["mix_w_in"], BF16, tn_pref=512, comm=gather_of(keys))
    keep_full(keys, got)
    cosf, sinf = _rope_tables(lx, lc)
    q_rot, k_rot = _qk_prep("qk_prep", p, cosf, sinf, attn_q_gain, attn_k_gain, aw, kw)
    ya, lse = _attn_fwd("attn_fwd", q_rot, k_rot, p, va_off, lx, groups)

    decay = ret_decay_logit[0].astype(F32)
    log_gamma = jax.nn.log_sigmoid(decay)
    r_offs = (qr_off, kr_off, vr_off, rw)
    tab = [_ret_tables(log_gamma[k], k) for k in range(2)]
    y_f, st_f = _ret_fwd("ret_fwd_a", p, r_offs, tab[0], 0, lx, None)
    y_r, st_b = _ret_fwd("ret_fwd_b", p, r_offs, tab[1], 1, lx, y_f)
    yr = _ret_out("ret_out", y_r, p, gr_off, lx)

    pa = _mm_nn("proj_attn", ya, full["w_proj_attn"], BF16)
    pr = _mm_nn("proj_ret", yr, full["w_proj_ret"], BF16)
    mg = _merge("merge", pa, pr, p, ga_off, gb_off)
    h2, o2 = _mm_residual("mix_out", mg, full["mix_w_out"], h1, g2, 1.0, lx)

    u3 = _rmsmod("ffn2_norm", h2, sc3, sh3, lx)
    z3, s3 = _ffn_in("ffn2_in", u3, full["ffn2_w_in"])
    h3, f3 = _mm_residual("ffn2_out", s3, full["ffn2_w_out"], h2, g3, 0.5, lx)
    dh3, d_fn, loss_tile = _final_loss("final_loss", h3, final_norm[None, :], loss_target[0])

    df3, dg3 = _gate_bwd("ffn2_gate_bwd", dh3, f3, g3, 0.5, lx)
    dz3 = _ffn_out_bwd("ffn2_out_bwd", df3, full["ffn2_w_out"], z3)
    grads = {"ffn2_w_out": _mm_tn("ffn2_out_dw", s3, df3, BF16)}
    g_ffn2_w_in, got = _ffn_in_bwd_w("ffn2_in_dw", u3, dz3, comm=exchange_of(grads))
    keep_landed(grads, got)
    grads = {"ffn2_w_in": g_ffn2_w_in}
    du3, got = _ffn_in_bwd_x("ffn2_in_bwd", dz3, full["ffn2_w_in"], comm=exchange_of(grads))
    keep_landed(grads, got)
    dh2, dsc3, dsh3 = _rmsmod_bwd("ffn2_norm_bwd", du3, h2, sc3, dh3, lx, lx)

    do2, dg2 = _gate_bwd("mix_gate_bwd", dh2, o2, g2, 1.0, lx)
    dpa, dpr, dga, dgb = _merge_bwd("merge_bwd", do2, full["mix_w_out"], pa, pr, p, ga_off, gb_off)
    dya = _mm_nt("proj_attn_bwd", dpa, full["w_proj_attn"], BF16)
    dyr = _mm_nt("proj_ret_bwd", dpr, full["w_proj_ret"], F32)
    mix_grads = {"mix_w_out": _mm_tn("mix_out_dw", mg, do2, BF16),
                 "w_proj_attn": _mm_tn("proj_attn_dw", ya, dpa, BF16),
                 "w_proj_ret": _mm_tn("proj_ret_dw", yr, dpr, BF16)}

    dy_ret, dgr = _ret_out_bwd("ret_out_bwd", dyr, y_r, p, gr_off, lx)
    dqr, dkr, dvr, dl_f = _ret_bwd("ret_bwd_a", p, r_offs, tab[0], st_f, dy_ret, 0, lx, None)
    dqr, dkr, dvr, dl_b = _ret_bwd("ret_bwd_b", p, r_offs, tab[1], st_b, dy_ret, 1, lx, (dqr, dkr, dvr))
    d_lam = jnp.stack([dl_f[:, 0, 0], dl_b[:, 0, 0]], axis=0)
    d_decay = d_lam * jax.nn.sigmoid(-decay)

    dq_rot = _attn_bwd_q("attn_bwd_q", q_rot, k_rot, p, va_off, ya, dya, lse, lx, groups)
    dk_rot, dva = _attn_bwd_kv("attn_bwd_kv", q_rot, k_rot, p, va_off, ya, dya, lse, lx, groups)
    dqa, dka, d_qg, d_kg = _qk_prep_bwd("qk_prep_bwd", dq_rot, dk_rot, p, cosf, sinf, attn_q_gain, attn_k_gain, aw, kw)

    dp = jnp.concatenate([dqa, dka, dva, dqr.astype(BF16), dkr.astype(BF16), dvr.astype(BF16),
                          _pad_rows(dgr, t), _pad_rows(dga, t), _pad_rows(dgb, t)], axis=1)
    du2, got = _mm_nt("mix_in_bwd", dp, full["mix_w_in"], F32, comm=exchange_of(mix_grads))
    keep_landed(mix_grads, got)
    grads = {"mix_w_in": _mm_tn("mix_in_dw", u2, dp, BF16, tn_pref=512)}
    dh1, dsc2, dsh2 = _rmsmod_bwd("mix_norm_bwd", du2, h1, sc2, dh2, lx, t)

    df1, dg1 = _gate_bwd("ffn1_gate_bwd", dh1, f1, g1, 0.5, lx)
    dz1, got = _ffn_out_bwd("ffn1_out_bwd", df1, full["ffn1_w_out"], z1, comm=exchange_of(grads))
    keep_landed(grads, got)
    grads = {"ffn1_w_out": _mm_tn("ffn1_out_dw", s1, df1, BF16)}
    g_ffn1_w_in, got = _ffn_in_bwd_w("ffn1_in_dw", u1, dz1, comm=exchange_of(grads))
    keep_landed(grads, got)
    grads = {"ffn1_w_in": g_ffn1_w_in}
    du1, got = _ffn_in_bwd_x("ffn1_in_bwd", dz1, full["ffn1_w_in"], comm=exchange_of(grads))
    keep_landed(grads, got)
    grad_x, dsc1, dsh1 = _rmsmod_bwd("ffn1_norm_bwd", du1, h0, sc1, dh1, lx, lx)

    zero = jnp.zeros((1, d), F32)
    dmod_c = jnp.concatenate([dsh1[0], dsc1[0], dg1[0], dsh2[0], dsc2[0], zero, zero, zero, zero], axis=0)
    dmod_x = jnp.concatenate([dsh1[1], dsc1[1], dg1[1], dsh2[1], dsc2[1], dg2[1], dsh3[1], dsc3[1], dg3[1]], axis=0)
    misc = jnp.concatenate([d_qg[0], d_kg[0], d_decay.reshape(-1), loss_tile[0, 0:1]])
    misc = jnp.pad(misc, (0, d - misc.shape[0]))[None, :]
    n_small = 3 * SUBLANES
    small = _pad_rows(jnp.concatenate([dmod_c, dmod_x, d_fn, misc], axis=0), n_small)
    small_all = _all_gather_small("gather_small", small)
    small_sum = _sum_slots("sum_small", small_all)
    dmod_c_sum, dmod_x_sum = small_sum[0:N_MOD], small_sum[N_MOD:2 * N_MOD]
    g_final_norm = small_sum[2 * N_MOD]
    misc_sum = small_sum[2 * N_MOD + 1]
    g_qg = misc_sum[0:HEAD_DIM][None, :]
    g_kg = misc_sum[HEAD_DIM:2 * HEAD_DIM][None, :]
    g_decay = misc_sum[2 * HEAD_DIM:2 * HEAD_DIM + 2 * heads_r].reshape(1, 2, heads_r)
    loss = misc_sum[2 * HEAD_DIM + 2 * heads_r]
    g_b_ada = (dmod_x_sum + dmod_c_sum).reshape(1, N_MOD * d)

    n_ada = w_ada.shape[2]
    dmod_rows = jnp.concatenate([small_all[:, N_MOD:2 * N_MOD, :].reshape(N_DEV, N_MOD * d),
                                 dmod_c_sum.reshape(1, N_MOD * d)], axis=0)
    dmod_mine = _pad_rows(lax.dynamic_slice_in_dim(dmod_rows, me * n_ada, n_ada, axis=1), 2 * SUBLANES)
    g_w_ada, ds_cond = _ada_bwd("ada_bwd", s_cond, dmod_mine, w_ada[0])
    cctx_parts = _all_gather_small("gather_cctx", ds_cond[N_DEV:N_DEV + SUBLANES])
    g_c_ctx = _cctx_grad("cctx_grad", cctx_parts, c_ctx[None, :])[0]

    mom = dict(ffn1_w_in=(m_ffn1_w_in, v_ffn1_w_in), mix_w_in=(m_mix_w_in, v_mix_w_in),
               w_proj_attn=(m_w_proj_attn, v_w_proj_attn), w_proj_ret=(m_w_proj_ret, v_w_proj_ret),
               ffn2_w_in=(m_ffn2_w_in, v_ffn2_w_in), ffn1_w_out=(m_ffn1_w_out, v_ffn1_w_out),
               mix_w_out=(m_mix_w_out, v_mix_w_out), ffn2_w_out=(m_ffn2_w_out, v_ffn2_w_out))
    res = {}
    for k in names:
        res[k] = _adamw("adamw_" + k, landed[k], shard[k], mom[k][0][0], mom[k][1][0])
    res["w_ada"] = _adamw("adamw_w_ada", g_w_ada[None], w_ada[0], m_w_ada[0], v_w_ada[0])

    def pack(cc, ba, qg, kg, dec, fn):
        misc_row = jnp.concatenate([qg.reshape(-1), kg.reshape(-1), dec.reshape(-1)])
        misc_row = jnp.pad(misc_row, (0, d - misc_row.shape[0]))[None, :]
        return _pad_rows(jnp.concatenate([cc.reshape(1, d), ba.reshape(N_MOD, d), fn.reshape(1, d), misc_row], axis=0),
                         2 * SUBLANES)

    sg, sd, sm, sv = _adamw(
        "adamw_small", pack(g_c_ctx, g_b_ada, g_qg, g_kg, g_decay, g_final_norm)[None],
        pack(c_ctx, b_ada, attn_q_gain, attn_k_gain, ret_decay_logit, final_norm),
        pack(m_c_ctx, m_b_ada, m_attn_q_gain, m_attn_k_gain, m_ret_decay_logit, m_final_norm),
        pack(v_c_ctx, v_b_ada, v_attn_q_gain, v_attn_k_gain, v_ret_decay_logit, v_final_norm))

    def unpack(a):
        misc_row = a[N_MOD + 2]
        return dict(c_ctx=a[0], b_ada=a[1:1 + N_MOD].reshape(1, N_MOD * d), final_norm=a[N_MOD + 1],
                    attn_q_gain=misc_row[0:HEAD_DIM][None, :], attn_k_gain=misc_row[HEAD_DIM:2 * HEAD_DIM][None, :],
                    ret_decay_logit=misc_row[2 * HEAD_DIM:2 * HEAD_DIM + 2 * heads_r].reshape(1, 2, heads_r))

    small_out = [unpack(a) for a in (sg, sd, sm, sv)]
    order = ["c_ctx", "w_ada", "b_ada", "ffn1_w_in", "ffn1_w_out", "mix_w_in", "attn_q_gain", "attn_k_gain",
             "ret_decay_logit", "w_proj_attn", "w_proj_ret", "mix_w_out", "ffn2_w_in", "ffn2_w_out", "final_norm"]
    outs = [loss, grad_x[None]]
    for which in range(4):
        for k in order:
            outs.append(res[k][which][None] if k in res else small_out[which][k])
    return tuple(outs)
```

```python
import math

import jax
import jax.numpy as jnp
from jax import lax
from jax.experimental import pallas as pl
from jax.experimental.pallas import tpu as pltpu

F32 = jnp.float32
BF16 = jnp.bfloat16
MESH = pl.DeviceIdType.MESH

N_DEV = 8
HEAD_DIM = 128
GRID_W = 64
ROPE_THETA = 10000.0
NORM_EPS = 1e-6
RET_CHUNK = 128
N_MOD = 9
LANES = 128
SUBLANES = 8
V7X_VMEM_BYTES = 64 * 1024 * 1024
VMEM_LIMIT = V7X_VMEM_BYTES - 8 * 1024 * 1024
K_TILE = 2560
LOG2E = 1.4426950408889634
LN2 = 0.6931471805599453

ADAM_LR = 0.001
ADAM_B1 = 0.9
ADAM_B2 = 0.999
ADAM_EPS = 1e-08
ADAM_WD = 0.01
ADAM_STEP = 10

NN = (((1,), (0,)), ((), ()))
NT = (((1,), (1,)), ((), ()))
TN = (((0,), (0,)), ((), ()))


def _tile(n, pref, align=LANES):
    best = None
    t = align
    while t <= min(n, pref):
        if n % t == 0:
            best = t
        t += align
    return n if best is None else best


def _cp(sem):
    return pltpu.CompilerParams(dimension_semantics=sem, vmem_limit_bytes=VMEM_LIMIT)


def _sigmoid(v):
    return 1.0 / (1.0 + jnp.exp(-v))


def _dot(a, b, dims):
    return lax.dot_general(a, b, dims, preferred_element_type=F32)


def _mm(name, a, a_spec, b_list, dims, grid, out_shapes, out_specs, acc_shape, epi, extras=(), comm=None):
    nb, ne, no = len(b_list), len(extras), len(out_shapes)
    nk = grid[2]
    kind, c_arrays = comm if comm is not None else (None, [])
    ncm = len(c_arrays)

    def body(*refs):
        a_ref = refs[0]
        b_refs = refs[1:1 + nb]
        e_refs = refs[1 + nb:1 + nb + ne]
        pos = 1 + nb + ne
        c_ins = refs[pos:pos + ncm]
        o_refs = refs[pos + ncm:pos + ncm + no]
        c_outs = refs[pos + ncm + no:pos + 2 * ncm + no]
        scratch = refs[pos + 2 * ncm + no:]
        accs = scratch[:0 if nk == 1 else nb]
        ids = [pl.program_id(axis) for axis in range(3)]
        if ncm:
            start_comm, finish_comm = _comm_plan(kind, c_ins, c_outs, *scratch[len(accs):])
            pl.when(jnp.logical_and(jnp.logical_and(ids[0] == 0, ids[1] == 0), ids[2] == 0))(start_comm)

        def finish(tiles):
            vals = epi(tiles, e_refs)
            for o_ref, v in zip(o_refs, vals):
                if isinstance(v, tuple):
                    for idx, part in enumerate(v):
                        o_ref[idx] = part.astype(o_ref.dtype)
                else:
                    o_ref[...] = v.astype(o_ref.dtype)

        if nk == 1:
            finish([_dot(a_ref[...], b_ref[...], dims) for b_ref in b_refs])
        else:
            @pl.when(ids[2] == 0)
            def _():
                for acc in accs:
                    acc[...] = jnp.zeros(acc.shape, F32)

            av = a_ref[...]
            for b_ref, acc in zip(b_refs, accs):
                acc[...] += _dot(av, b_ref[...], dims)

            @pl.when(ids[2] == nk - 1)
            def _():
                finish([acc[...] for acc in accs])

        if ncm:
            pl.when(jnp.logical_and(jnp.logical_and(ids[0] == grid[0] - 1, ids[1] == grid[1] - 1),
                                    ids[2] == nk - 1))(finish_comm)

    any_spec = pl.BlockSpec(memory_space=pl.ANY)
    c_shapes = [jax.ShapeDtypeStruct(((N_DEV,) if kind == "gather" else ()) + s.shape, s.dtype) for s in c_arrays]
    scratch_shapes = [] if nk == 1 else [pltpu.VMEM(acc_shape, F32)] * nb
    if ncm:
        scratch_shapes = scratch_shapes + _comm_semaphores(ncm)
    semantics = ("arbitrary",) * 3 if ncm else ("parallel", "parallel", "arbitrary")
    outs = pl.pallas_call(
        body, name=name, grid=grid,
        in_specs=[a_spec] + [s for _, s in b_list] + [s for _, s in extras] + [any_spec] * ncm,
        out_specs=list(out_specs) + [any_spec] * ncm, out_shape=list(out_shapes) + c_shapes,
        scratch_shapes=scratch_shapes, compiler_params=_cp(semantics),
    )(a, *[b for b, _ in b_list], *[e for e, _ in extras], *c_arrays)
    main = outs[0] if no == 1 else tuple(outs[:no])
    return main if comm is None else (main, list(outs[no:]))


def _plain(accs, _):
    return (accs[0],)


def _mm_nn(name, a, b, out_dtype, tm_pref=1024, tn_pref=1024, tk_pref=K_TILE, comm=None):
    m, k = a.shape
    n = b.shape[1]
    tm, tn, tk = _tile(m, tm_pref), _tile(n, tn_pref), _tile(k, tk_pref)
    return _mm(name, a, pl.BlockSpec((tm, tk), lambda i, j, kk: (i, kk)),
               [(b, pl.BlockSpec((tk, tn), lambda i, j, kk: (kk, j)))], NN, (m // tm, n // tn, k // tk),
               [jax.ShapeDtypeStruct((m, n), out_dtype)], [pl.BlockSpec((tm, tn), lambda i, j, kk: (i, j))],
               (tm, tn), _plain, comm=comm)


def _mm_nt(name, a, b, out_dtype, tm_pref=1024, tn_pref=1024, tk_pref=K_TILE, comm=None):
    m, k = a.shape
    n = b.shape[0]
    tm, tn, tk = _tile(m, tm_pref), _tile(n, tn_pref), _tile(k, tk_pref)
    return _mm(name, a, pl.BlockSpec((tm, tk), lambda i, j, kk: (i, kk)),
               [(b, pl.BlockSpec((tn, tk), lambda i, j, kk: (j, kk)))], NT, (m // tm, n // tn, k // tk),
               [jax.ShapeDtypeStruct((m, n), out_dtype)], [pl.BlockSpec((tm, tn), lambda i, j, kk: (i, j))],
               (tm, tn), _plain, comm=comm)


def _mm_tn(name, a, b, out_dtype, rows=None, tm_pref=1024, tn_pref=1024, tk_pref=K_TILE, comm=None):
    k = a.shape[0] if rows is None else rows
    m, n = a.shape[1], b.shape[1]
    tm, tn, tk = _tile(m, tm_pref), _tile(n, tn_pref), _tile(k, tk_pref)
    return _mm(name, a, pl.BlockSpec((tk, tm), lambda i, j, kk: (kk, i)),
               [(b, pl.BlockSpec((tk, tn), lambda i, j, kk: (kk, j)))], TN, (m // tm, n // tn, k // tk),
               [jax.ShapeDtypeStruct((m, n), out_dtype)], [pl.BlockSpec((tm, tn), lambda i, j, kk: (i, j))],
               (tm, tn), _plain, comm=comm)


def _ffn_in(name, u, w_in, comm=None):
    r, d = u.shape
    f = w_in.shape[1] // 2
    tm, tn, tk = _tile(r, 1024), _tile(f, 512), _tile(d, K_TILE)
    nf = f // tn

    def epi(accs, _):
        za, zb = accs
        s = za * _sigmoid(za) * zb
        return (za, zb), s

    return _mm(name, u, pl.BlockSpec((tm, tk), lambda i, j, kk: (i, kk)),
               [(w_in, pl.BlockSpec((tk, tn), lambda i, j, kk: (kk, j))),
                (w_in, pl.BlockSpec((tk, tn), lambda i, j, kk: (kk, j + nf)))],
               NN, (r // tm, nf, d // tk),
               [jax.ShapeDtypeStruct((2, r, f), BF16), jax.ShapeDtypeStruct((r, f), BF16)],
               [pl.BlockSpec((2, tm, tn), lambda i, j, kk: (0, i, j)), pl.BlockSpec((tm, tn), lambda i, j, kk: (i, j))],
               (tm, tn), epi, comm=comm)


def _mm_residual(name, a, w, res, gate, gate_scale, lx, comm=None):
    r, k = a.shape
    n = w.shape[1]
    tm, tn, tk = _tile(r, 1024), _tile(n, 1024), _tile(k, K_TILE)

    def epi(accs, e_refs):
        res_ref, g_ref = e_refs
        rows = pl.program_id(0) * tm + lax.broadcasted_iota(jnp.int32, (tm, 1), 0)
        g = jnp.where(rows < lx, g_ref[1], g_ref[0])
        return res_ref[...] + gate_scale * g * accs[0], accs[0]

    return _mm(name, a, pl.BlockSpec((tm, tk), lambda i, j, kk: (i, kk)),
               [(w, pl.BlockSpec((tk, tn), lambda i, j, kk: (kk, j)))], NN, (r // tm, n // tn, k // tk),
               [jax.ShapeDtypeStruct((r, n), F32), jax.ShapeDtypeStruct((r, n), BF16)],
               [pl.BlockSpec((tm, tn), lambda i, j, kk: (i, j))] * 2, (tm, tn), epi,
               extras=[(res, pl.BlockSpec((tm, tn), lambda i, j, kk: (i, j))),
                       (gate, pl.BlockSpec((2, 1, tn), lambda i, j, kk: (0, 0, j)))], comm=comm)


def _ffn_out_bwd(name, df, w_out, z, comm=None):
    r, d = df.shape
    f = w_out.shape[0]
    tm, tn, tk = _tile(r, 1024), _tile(f, 512), _tile(d, K_TILE)

    def epi(accs, e_refs):
        ds = accs[0]
        za = e_refs[0][0].astype(F32)
        zb = e_refs[0][1].astype(F32)
        sg = _sigmoid(za)
        da = ds * zb * sg * (1.0 + za * (1.0 - sg))
        db = ds * za * sg
        return ((da, db),)

    zspec = pl.BlockSpec((2, tm, tn), lambda i, j, kk: (0, i, j))
    return _mm(name, df, pl.BlockSpec((tm, tk), lambda i, j, kk: (i, kk)),
               [(w_out, pl.BlockSpec((tn, tk), lambda i, j, kk: (j, kk)))], NT, (r // tm, f // tn, d // tk),
               [jax.ShapeDtypeStruct((2, r, f), BF16)], [zspec], (tm, tn), epi, extras=[(z, zspec)], comm=comm)


def _ffn_in_bwd_x(name, dz, w_in, comm=None):
    _, r, f = dz.shape
    d = w_in.shape[0]
    tm, tn, tk = _tile(r, 1024), _tile(d, 1024), _tile(f, K_TILE)
    nkf = f // tk
    return _mm(name, dz, pl.BlockSpec((None, tm, tk), lambda i, j, kk: (kk // nkf, i, kk % nkf)),
               [(w_in, pl.BlockSpec((tn, tk), lambda i, j, kk: (j, kk)))], NT, (r // tm, d // tn, 2 * nkf),
               [jax.ShapeDtypeStruct((r, d), F32)], [pl.BlockSpec((tm, tn), lambda i, j, kk: (i, j))],
               (tm, tn), _plain, comm=comm)


def _ffn_in_bwd_w(name, u, dz, comm=None):
    r, d = u.shape
    f = dz.shape[2]
    tm, tn, tk = _tile(d, 1024), _tile(f, 512), _tile(r, K_TILE)
    nf = f // tn
    return _mm(name, u, pl.BlockSpec((tk, tm), lambda i, j, kk: (kk, i)),
               [(dz, pl.BlockSpec((None, tk, tn), lambda i, j, kk: (j // nf, kk, j % nf)))], TN,
               (d // tm, 2 * nf, r // tk),
               [jax.ShapeDtypeStruct((d, 2 * f), BF16)], [pl.BlockSpec((tm, tn), lambda i, j, kk: (i, j))],
               (tm, tn), _plain, comm=comm)


def _merge_bwd(name, dout, w_out, pa, pr, p, ga_off, gb_off):
    r, d = dout.shape
    n = w_out.shape[0]
    cw = math.gcd(math.gcd(ga_off, gb_off), n)
    tm, tn, tk = _tile(r, 1024), _tile(cw, 512), _tile(d, K_TILE)

    def epi(accs, e_refs):
        dm = accs[0]
        pa_ref, pr_ref, ga_ref, gb_ref = e_refs
        sa = _sigmoid(ga_ref[...].astype(F32))
        sb = _sigmoid(gb_ref[...].astype(F32))
        pav = pa_ref[...].astype(F32)
        prv = pr_ref[...].astype(F32)
        return dm * sa, dm * sb, dm * pav * sa * (1.0 - sa), dm * prv * sb * (1.0 - sb)

    o_spec = pl.BlockSpec((tm, tn), lambda i, j, kk: (i, j))
    return _mm(name, dout, pl.BlockSpec((tm, tk), lambda i, j, kk: (i, kk)),
               [(w_out, pl.BlockSpec((tn, tk), lambda i, j, kk: (j, kk)))], NT, (r // tm, n // tn, d // tk),
               [jax.ShapeDtypeStruct((r, n), BF16)] * 4, [o_spec] * 4, (tm, tn), epi,
               extras=[(pa, o_spec), (pr, o_spec),
                       (p, pl.BlockSpec((tm, tn), lambda i, j, kk: (i, ga_off // tn + j))),
                       (p, pl.BlockSpec((tm, tn), lambda i, j, kk: (i, gb_off // tn + j)))])


def _row_tile(r, lx, d):
    pref = 256 if d > 1024 else 512
    return _tile(math.gcd(r, lx), pref, SUBLANES)


def _rmsmod(name, h, scale, shift, lx, rows=None):
    r = h.shape[0] if rows is None else rows
    d = h.shape[1]
    tr = _row_tile(r, lx, d)
    nx = lx // tr
    cls = lambda i: (jnp.where(i < nx, 1, 0), 0, 0)

    def body(h_ref, sc_ref, sh_ref, u_ref):
        hv = h_ref[...]
        rinv = lax.rsqrt(jnp.mean(hv * hv, axis=-1, keepdims=True) + NORM_EPS)
        u_ref[...] = (hv * rinv * (1.0 + sc_ref[...]) + sh_ref[...]).astype(u_ref.dtype)

    return pl.pallas_call(
        body, name=name, grid=(r // tr,),
        in_specs=[pl.BlockSpec((tr, d), lambda i: (i, 0)), pl.BlockSpec((None, 1, d), cls), pl.BlockSpec((None, 1, d), cls)],
        out_specs=pl.BlockSpec((tr, d), lambda i: (i, 0)), out_shape=jax.ShapeDtypeStruct((r, d), BF16),
        compiler_params=_cp(("parallel",)))(h, scale, shift)


def _rmsmod_bwd(name, du, h, scale, dh_in, lx, rows_out):
    r, d = du.shape
    rin = dh_in.shape[0]
    tr = _row_tile(math.gcd(r, math.gcd(rin, rows_out)), lx, d)
    nx, nin, nout = lx // tr, rin // tr, rows_out // tr
    cls = lambda i: (jnp.where(i < nx, 1, 0), 0, 0)

    def body(du_ref, h_ref, sc_ref, dhin_ref, dh_ref, dsc_ref, dsh_ref):
        i = pl.program_id(0)
        hv = h_ref[...]
        duv = du_ref[...]
        rinv = lax.rsqrt(jnp.mean(hv * hv, axis=-1, keepdims=True) + NORM_EPS)
        nv = hv * rinv
        dn = duv * (1.0 + sc_ref[...])

        @pl.when(jnp.logical_or(i == 0, i == nx))
        def _():
            dsc_ref[...] = jnp.zeros(dsc_ref.shape, F32)
            dsh_ref[...] = jnp.zeros(dsh_ref.shape, F32)

        dsc_ref[...] += jnp.sum(duv * nv, axis=0, keepdims=True)
        dsh_ref[...] += jnp.sum(duv, axis=0, keepdims=True)

        @pl.when(i < nout)
        def _():
            dh = rinv * (dn - nv * jnp.mean(dn * nv, axis=-1, keepdims=True))
            dh_ref[...] = dh + jnp.where(i < nin, dhin_ref[...], 0.0)

    return pl.pallas_call(
        body, name=name, grid=(r // tr,),
        in_specs=[pl.BlockSpec((tr, d), lambda i: (i, 0)), pl.BlockSpec((tr, d), lambda i: (i, 0)),
                  pl.BlockSpec((None, 1, d), cls), pl.BlockSpec((tr, d), lambda i: (jnp.minimum(i, nin - 1), 0))],
        out_specs=[pl.BlockSpec((tr, d), lambda i: (jnp.minimum(i, nout - 1), 0)),
                   pl.BlockSpec((None, 1, d), cls), pl.BlockSpec((None, 1, d), cls)],
        out_shape=[jax.ShapeDtypeStruct((rows_out, d), F32), jax.ShapeDtypeStruct((2, 1, d), F32),
                   jax.ShapeDtypeStruct((2, 1, d), F32)],
        compiler_params=_cp(("arbitrary",)))(du, h, scale, dh_in)


def _gate_bwd(name, dh, f, gate, gate_scale, lx):
    r, d = dh.shape
    tr = _row_tile(r, lx, d)
    nx = lx // tr
    cls = lambda i: (jnp.where(i < nx, 1, 0), 0, 0)

    def body(dh_ref, f_ref, g_ref, df_ref, dg_ref):
        i = pl.program_id(0)
        dhv = dh_ref[...]

        @pl.when(jnp.logical_or(i == 0, i == nx))
        def _():
            dg_ref[...] = jnp.zeros(dg_ref.shape, F32)

        df_ref[...] = (gate_scale * g_ref[...] * dhv).astype(df_ref.dtype)
        dg_ref[...] += jnp.sum(gate_scale * f_ref[...].astype(F32) * dhv, axis=0, keepdims=True)

    return pl.pallas_call(
        body, name=name, grid=(r // tr,),
        in_specs=[pl.BlockSpec((tr, d), lambda i: (i, 0)), pl.BlockSpec((tr, d), lambda i: (i, 0)),
                  pl.BlockSpec((None, 1, d), cls)],
        out_specs=[pl.BlockSpec((tr, d), lambda i: (i, 0)), pl.BlockSpec((None, 1, d), cls)],
        out_shape=[jax.ShapeDtypeStruct((r, d), BF16), jax.ShapeDtypeStruct((2, 1, d), F32)],
        compiler_params=_cp(("arbitrary",)))(dh, f, gate)


def _final_loss(name, h, final_norm, target):
    r, d = h.shape
    tr = _row_tile(r, r, d)

    def body(h_ref, fn_ref, t_ref, dh_ref, dfn_ref, loss_ref):
        i = pl.program_id(0)
        hv = h_ref[...]
        rinv = lax.rsqrt(jnp.mean(hv * hv, axis=-1, keepdims=True) + NORM_EPS)
        nv = hv * rinv
        fn = fn_ref[...]
        err = nv * fn - t_ref[...]
        dy = err * (1.0 / d)

        @pl.when(i == 0)
        def _():
            dfn_ref[...] = jnp.zeros(dfn_ref.shape, F32)
            loss_ref[...] = jnp.zeros(loss_ref.shape, F32)

        loss_ref[...] += 0.5 * jnp.sum(jnp.mean(err * err, axis=-1, keepdims=True), axis=0, keepdims=True)
        dfn_ref[...] += jnp.sum(dy * nv, axis=0, keepdims=True)
        dn = dy * fn
        dh_ref[...] = rinv * (dn - nv * jnp.mean(dn * nv, axis=-1, keepdims=True))

    return pl.pallas_call(
        body, name=name, grid=(r // tr,),
        in_specs=[pl.BlockSpec((tr, d), lambda i: (i, 0)), pl.BlockSpec((1, d), lambda i: (0, 0)),
                  pl.BlockSpec((tr, d), lambda i: (i, 0))],
        out_specs=[pl.BlockSpec((tr, d), lambda i: (i, 0)), pl.BlockSpec((1, d), lambda i: (0, 0)),
                   pl.BlockSpec((SUBLANES, LANES), lambda i: (0, 0))],
        out_shape=[jax.ShapeDtypeStruct((r, d), F32), jax.ShapeDtypeStruct((1, d), F32),
                   jax.ShapeDtypeStruct((SUBLANES, LANES), F32)],
        compiler_params=_cp(("arbitrary",)))(h, final_norm, target)


def _swap_pairs(t):
    lane = lax.broadcasted_iota(jnp.int32, t.shape, 1)
    nxt = pltpu.roll(t, HEAD_DIM - 1, 1)
    prv = pltpu.roll(t, 1, 1)
    return jnp.where(lane % 2 == 0, nxt, prv)


def _qk_prep(name, p, cosf, sinf, q_gain, k_gain, aw, kw):
    t = p.shape[0]
    tr = _tile(t, 256, SUBLANES)
    q_prescale = HEAD_DIM ** -0.5 * LOG2E

    def head_fwd(v, gain, cs, sn):
        v = v.astype(F32)
        rinv = lax.rsqrt(jnp.mean(v * v, axis=-1, keepdims=True) + NORM_EPS)
        tt = v * rinv * gain
        return tt * cs + _swap_pairs(tt) * sn

    def body(q_ref, k_ref, cos_ref, sin_ref, qg_ref, kg_ref, qo_ref, ko_ref):
        cs, sn = cos_ref[...], sin_ref[...]
        for hh in range(aw // HEAD_DIM):
            sl = slice(hh * HEAD_DIM, (hh + 1) * HEAD_DIM)
            qo_ref[:, sl] = (head_fwd(q_ref[:, sl], qg_ref[...], cs, sn) * q_prescale).astype(qo_ref.dtype)
        for hh in range(kw // HEAD_DIM):
            sl = slice(hh * HEAD_DIM, (hh + 1) * HEAD_DIM)
            ko_ref[:, sl] = head_fwd(k_ref[:, sl], kg_ref[...], cs, sn).astype(ko_ref.dtype)

    assert aw % kw == 0
    row = lambda i: (i, 0)
    return pl.pallas_call(
        body, name=name, grid=(t // tr,),
        in_specs=[pl.BlockSpec((tr, aw), row), pl.BlockSpec((tr, kw), lambda i: (i, aw // kw)),
                  pl.BlockSpec((tr, HEAD_DIM), row), pl.BlockSpec((tr, HEAD_DIM), row),
                  pl.BlockSpec((1, HEAD_DIM), lambda i: (0, 0)), pl.BlockSpec((1, HEAD_DIM), lambda i: (0, 0))],
        out_specs=[pl.BlockSpec((tr, aw), row), pl.BlockSpec((tr, kw), row)],
        out_shape=[jax.ShapeDtypeStruct((t, aw), BF16), jax.ShapeDtypeStruct((t, kw), BF16)],
        compiler_params=_cp(("parallel",)))(p, p, cosf, sinf, q_gain, k_gain)


def _qk_prep_bwd(name, dq_rot, dk_rot, p, cosf, sinf, q_gain, k_gain, aw, kw):
    t = p.shape[0]
    lq = dq_rot.shape[0]
    tr = _tile(math.gcd(t, lq), 256, SUBLANES)
    nq = lq // tr

    def head_bwd(dout, v, gain, cs, sn):
        v = v.astype(F32)
        rinv = lax.rsqrt(jnp.mean(v * v, axis=-1, keepdims=True) + NORM_EPS)
        vn = v * rinv
        dt = dout * cs - _swap_pairs(dout) * sn
        dvn = dt * gain
        dv = rinv * (dvn - vn * jnp.mean(dvn * vn, axis=-1, keepdims=True))
        return dv, jnp.sum(dt * vn, axis=0, keepdims=True)

    def body(dq_ref, dk_ref, q_ref, k_ref, cos_ref, sin_ref, qg_ref, kg_ref, dqo_ref, dko_ref, dqg_ref, dkg_ref):
        i = pl.program_id(0)
        cs, sn = cos_ref[...], sin_ref[...]

        @pl.when(i == 0)
        def _():
            dqg_ref[...] = jnp.zeros(dqg_ref.shape, F32)
            dkg_ref[...] = jnp.zeros(dkg_ref.shape, F32)

        has_q = i < nq
        for hh in range(aw // HEAD_DIM):
            sl = slice(hh * HEAD_DIM, (hh + 1) * HEAD_DIM)
            dout = jnp.where(has_q, dq_ref[:, sl], 0.0)
            dv, dg = head_bwd(dout, q_ref[:, sl], qg_ref[...], cs, sn)
            dqo_ref[:, sl] = dv.astype(dqo_ref.dtype)
            dqg_ref[...] += dg
        for hh in range(kw // HEAD_DIM):
            sl = slice(hh * HEAD_DIM, (hh + 1) * HEAD_DIM)
            dv, dg = head_bwd(dk_ref[:, sl], k_ref[:, sl], kg_ref[...], cs, sn)
            dko_ref[:, sl] = dv.astype(dko_ref.dtype)
            dkg_ref[...] += dg

    row = lambda i: (i, 0)
    one = lambda i: (0, 0)
    return pl.pallas_call(
        body, name=name, grid=(t // tr,),
        in_specs=[pl.BlockSpec((tr, aw), lambda i: (jnp.minimum(i, nq - 1), 0)), pl.BlockSpec((tr, kw), row),
                  pl.BlockSpec((tr, aw), row), pl.BlockSpec((tr, kw), lambda i: (i, aw // kw)),
                  pl.BlockSpec((tr, HEAD_DIM), row), pl.BlockSpec((tr, HEAD_DIM), row),
                  pl.BlockSpec((1, HEAD_DIM), one), pl.BlockSpec((1, HEAD_DIM), one)],
        out_specs=[pl.BlockSpec((tr, aw), row), pl.BlockSpec((tr, kw), row),
                   pl.BlockSpec((1, HEAD_DIM), one), pl.BlockSpec((1, HEAD_DIM), one)],
        out_shape=[jax.ShapeDtypeStruct((t, aw), BF16), jax.ShapeDtypeStruct((t, kw), BF16),
                   jax.ShapeDtypeStruct((1, HEAD_DIM), F32), jax.ShapeDtypeStruct((1, HEAD_DIM), F32)],
        compiler_params=_cp(("arbitrary",)))(dq_rot, dk_rot, p, p, cosf, sinf, q_gain, k_gain)


def _attn_tiles(lq, t):
    return _tile(lq, 512), _tile(t, 768)


def _attn_fwd(name, q, k, p, v_off, lq, groups):
    t = k.shape[0]
    hq = q.shape[1] // HEAD_DIM
    hkv = hq // groups
    gw = groups * HEAD_DIM
    tq, tk = _attn_tiles(lq, t)
    nkv = t // tk
    vb = v_off // HEAD_DIM

    def body(q_ref, k_ref, v_ref, o_ref, lse_ref, m_sc, l_sc, acc_sc):
        j = pl.program_id(2)

        @pl.when(j == 0)
        def _():
            m_sc[...] = jnp.full(m_sc.shape, -jnp.inf, F32)
            l_sc[...] = jnp.zeros(l_sc.shape, F32)
            acc_sc[...] = jnp.zeros(acc_sc.shape, F32)

        kv, vv = k_ref[...], v_ref[...]
        for g in range(groups):
            sl = slice(g * HEAD_DIM, (g + 1) * HEAD_DIM)
            s = _dot(q_ref[:, sl], kv, NT)
            m_prev = m_sc[g]
            m_new = jnp.maximum(m_prev, jnp.max(s, axis=-1, keepdims=True))
            alpha = jnp.exp2(m_prev - m_new)
            pexp = jnp.exp2(s - m_new)
            part = pexp[:, 0:LANES]
            for cb in range(1, tk // LANES):
                part = part + pexp[:, cb * LANES:(cb + 1) * LANES]
            l_sc[g] = alpha * l_sc[g] + part
            acc_sc[:, sl] = alpha * acc_sc[:, sl] + _dot(pexp.astype(BF16), vv, NN)
            m_sc[g] = m_new

        @pl.when(j == nkv - 1)
        def _():
            for g in range(groups):
                sl = slice(g * HEAD_DIM, (g + 1) * HEAD_DIM)
                l_row = jnp.sum(l_sc[g], axis=-1, keepdims=True)
                o_ref[:, sl] = (acc_sc[:, sl] * (1.0 / l_row)).astype(o_ref.dtype)
                lse_ref[g] = jnp.broadcast_to(m_sc[g] + jnp.log2(l_row), (tq, LANES))

    return pl.pallas_call(
        body, name=name, grid=(hkv, lq // tq, nkv),
        in_specs=[pl.BlockSpec((tq, gw), lambda h, i, j: (i, h)),
                  pl.BlockSpec((tk, HEAD_DIM), lambda h, i, j: (j, h)),
                  pl.BlockSpec((tk, HEAD_DIM), lambda h, i, j: (j, vb + h))],
        out_specs=[pl.BlockSpec((tq, gw), lambda h, i, j: (i, h)),
                   pl.BlockSpec((groups, tq, LANES), lambda h, i, j: (h, i, 0))],
        out_shape=[jax.ShapeDtypeStruct((lq, hq * HEAD_DIM), BF16), jax.ShapeDtypeStruct((hq, lq, LANES), F32)],
        scratch_shapes=[pltpu.VMEM((groups, tq, 1), F32), pltpu.VMEM((groups, tq, LANES), F32), pltpu.VMEM((tq, gw), F32)],
        compiler_params=_cp(("parallel", "parallel", "arbitrary")))(q, k, p)


def _attn_delta(name, o, do):
    lq, aw = o.shape
    hq = aw // HEAD_DIM
    tr = _tile(lq, 512, SUBLANES)

    def body(o_ref, do_ref, d_ref):
        for h in range(hq):
            sl = slice(h * HEAD_DIM, (h + 1) * HEAD_DIM)
            dsum = jnp.sum(do_ref[:, sl].astype(F32) * o_ref[:, sl].astype(F32), axis=-1, keepdims=True)
            d_ref[h] = jnp.broadcast_to(dsum, (tr, LANES))

    spec = pl.BlockSpec((tr, aw), lambda i: (i, 0))
    return pl.pallas_call(
        body, name=name, grid=(lq // tr,), in_specs=[spec, spec],
        out_specs=pl.BlockSpec((hq, tr, LANES), lambda i: (0, i, 0)),
        out_shape=jax.ShapeDtypeStruct((hq, lq, LANES), F32), compiler_params=_cp(("parallel",)))(o, do)


def _attn_bwd(name, q, k, p, v_off, do, lse, delta, lq, groups):
    t = k.shape[0]
    hkv = k.shape[1] // HEAD_DIM
    gw = groups * HEAD_DIM
    tq, tk = _attn_tiles(lq, t)
    nq, nkv = lq // tq, t // tk
    scale = HEAD_DIM ** -0.5
    vb = v_off // HEAD_DIM

    def body(q_ref, k_ref, v_ref, do_ref, lse_ref, delta_ref, dq_ref, dk_ref, dv_ref, dq_sc, dk_sc, dv_sc):
        j, i = pl.program_id(1), pl.program_id(2)

        @pl.when(i == 0)
        def _():
            dk_sc[...] = jnp.zeros(dk_sc.shape, F32)
            dv_sc[...] = jnp.zeros(dv_sc.shape, F32)

        @pl.when(j == 0)
        def _():
            dq_sc[i] = jnp.zeros((tq, gw), F32)

        kv, vv = k_ref[...], v_ref[...]
        dk_part, dv_part = None, None
        for g in range(groups):
            sl = slice(g * HEAD_DIM, (g + 1) * HEAD_DIM)
            qv, dov = q_ref[:, sl], do_ref[:, sl]
            s = _dot(qv, kv, NT)
            pexp = jnp.exp2(s - lse_ref[g, :, 0:1])
            dv_g = _dot(pexp.astype(BF16), dov, TN)
            dp = _dot(dov, vv, NT)
            ds = (pexp * (dp - delta_ref[g, :, 0:1])).astype(BF16)
            dk_g = _dot(ds, qv, TN)
            dq_sc[i, :, sl] += _dot(ds, kv, NN)
            dk_part = dk_g if dk_part is None else dk_part + dk_g
            dv_part = dv_g if dv_part is None else dv_part + dv_g
        dk_sc[...] += dk_part
        dv_sc[...] += dv_part

        @pl.when(i == nq - 1)
        def _():
            dk_ref[...] = dk_sc[...] * LN2
            dv_ref[...] = dv_sc[...].astype(dv_ref.dtype)

        @pl.when(j == nkv - 1)
        def _():
            dq_ref[...] = dq_sc[i] * scale

    qspec = pl.BlockSpec((tq, gw), lambda kh, j, i: (i, kh))
    kspec = pl.BlockSpec((tk, HEAD_DIM), lambda kh, j, i: (j, kh))
    rowspec = pl.BlockSpec((groups, tq, LANES), lambda kh, j, i: (kh, i, 0))
    dqspec = pl.BlockSpec((tq, gw), lambda kh, j, i: (jnp.where(j == nkv - 1, i, 0), kh))
    return pl.pallas_call(
        body, name=name, grid=(hkv, nkv, nq),
        in_specs=[qspec, kspec, pl.BlockSpec((tk, HEAD_DIM), lambda kh, j, i: (j, vb + kh)), qspec, rowspec, rowspec],
        out_specs=[dqspec, kspec, kspec],
        out_shape=[jax.ShapeDtypeStruct((lq, hkv * gw), F32), jax.ShapeDtypeStruct((t, hkv * HEAD_DIM), F32),
                   jax.ShapeDtypeStruct((t, hkv * HEAD_DIM), BF16)],
        scratch_shapes=[pltpu.VMEM((nq, tq, gw), F32), pltpu.VMEM((tk, HEAD_DIM), F32), pltpu.VMEM((tk, HEAD_DIM), F32)],
        compiler_params=_cp(("parallel", "arbitrary", "arbitrary")))(q, k, p, do, lse, delta)


def _ret_tables(log_gamma, direction):
    c = RET_CHUNK
    idx = jnp.arange(c, dtype=F32)
    diff = idx[:, None] - idx[None, :]
    if direction == 1:
        diff = -diff
    keep = diff >= 0
    lg = log_gamma.astype(F32)
    mask = jnp.where(keep[None], jnp.exp(jnp.where(keep, diff, 0.0)[None] * lg[:, None, None]), 0.0)
    q_exp = idx + 1.0 if direction == 0 else c - idx
    k_exp = c - 1.0 - idx if direction == 0 else idx
    sign = 1.0 if direction == 0 else -1.0
    lane = lambda v: jnp.broadcast_to(v[..., None], v.shape + (LANES,))
    qdec = lane(jnp.exp(q_exp[None, :] * lg[:, None]))
    kdec = lane(jnp.exp(k_exp[None, :] * lg[:, None]))
    cdec = jnp.broadcast_to(jnp.exp(c * lg)[:, None, None], (lg.shape[0], SUBLANES, LANES))
    weights = lane(jnp.stack([sign * idx, q_exp, -sign * idx, k_exp], axis=0))
    return mask, qdec, kdec, cdec, weights


def _ret_chunk_of(direction, step, nx, nc):
    if direction == 0:
        return jnp.where(step < nc, nx + step, step - nc)
    return jnp.where(step < nc, nx + nc - 1 - step, nx + nc - 1 - step)


def _ret_fwd(name, p, offs, tables, direction, lx, prev):
    q_off, k_off, v_off, rw = offs
    t = p.shape[0]
    c = RET_CHUNK
    n_steps, nx = t // c, lx // c
    nc = n_steps - nx
    bw = math.gcd(math.gcd(q_off, k_off), math.gcd(v_off, rw))
    bw = _tile(bw, 512)
    hpb = bw // HEAD_DIM
    heads = rw // HEAD_DIM
    k_scale = HEAD_DIM ** -0.5
    mask, qdec, kdec, cdec, _ = tables
    rc = lambda n: _ret_chunk_of(direction, n, nx, nc)

    def body(*refs):
        if prev is None:
            q_ref, k_ref, v_ref, m_ref, qd_ref, kd_ref, cd_ref, y_ref, st_ref, s_sc = refs
        else:
            q_ref, k_ref, v_ref, m_ref, qd_ref, kd_ref, cd_ref, prev_ref, y_ref, st_ref, s_sc = refs
        n = pl.program_id(1)

        @pl.when(n == 0)
        def _():
            s_sc[...] = jnp.zeros(s_sc.shape, F32)

        for hh in range(hpb):
            sl = slice(hh * HEAD_DIM, (hh + 1) * HEAD_DIM)
            qv = q_ref[:, sl]
            kf = k_ref[:, sl].astype(F32) * k_scale
            kv = kf.astype(BF16)
            vv = v_ref[:, sl]
            state = s_sc[hh]
            st_ref[hh] = state
            a = _dot(qv, kv, NT) * m_ref[hh]
            y = _dot(a.astype(BF16), vv, NN) + _dot(qv, state.astype(BF16), NN) * qd_ref[hh]
            s_sc[hh] = state * cd_ref[hh, 0:1, :] + _dot((kf * kd_ref[hh]).astype(BF16), vv, TN)
            if prev is not None:
                y = y + prev_ref[:, sl]
            y_ref[:, sl] = y

    col = lambda off: (lambda g, n: (rc(n), off // bw + g))
    tab3 = lambda g, n: (g, 0, 0)
    in_specs = [pl.BlockSpec((c, bw), col(q_off)), pl.BlockSpec((c, bw), col(k_off)), pl.BlockSpec((c, bw), col(v_off)),
                pl.BlockSpec((hpb, c, c), tab3), pl.BlockSpec((hpb, c, LANES), tab3), pl.BlockSpec((hpb, c, LANES), tab3),
                pl.BlockSpec((hpb, SUBLANES, LANES), tab3)]
    args = [p, p, p, mask, qdec, kdec, cdec]
    aliases = {}
    yspec = pl.BlockSpec((c, bw), lambda g, n: (rc(n), g))
    if prev is not None:
        in_specs.append(yspec)
        args.append(prev)
        aliases = {len(args) - 1: 0}
    return pl.pallas_call(
        body, name=name, grid=(heads // hpb, n_steps), in_specs=in_specs,
        out_specs=[yspec, pl.BlockSpec((None, hpb, HEAD_DIM, HEAD_DIM), lambda g, n: (n, g, 0, 0))],
        out_shape=[jax.ShapeDtypeStruct((t, rw), F32), jax.ShapeDtypeStruct((n_steps, heads, HEAD_DIM, HEAD_DIM), F32)],
        scratch_shapes=[pltpu.VMEM((hpb, HEAD_DIM, HEAD_DIM), F32)], input_output_aliases=aliases,
        compiler_params=_cp(("parallel", "arbitrary")))(*args)


def _ret_bwd(name, p, offs, tables, states, dy, direction, lx, prev):
    q_off, k_off, v_off, rw = offs
    t = p.shape[0]
    c = RET_CHUNK
    n_steps, nx = t // c, lx // c
    nc = n_steps - nx
    bw = math.gcd(math.gcd(q_off, k_off), math.gcd(v_off, rw))
    bw = _tile(bw, 512)
    hpb = bw // HEAD_DIM
    heads = rw // HEAD_DIM
    k_scale = HEAD_DIM ** -0.5
    mask, qdec, kdec, cdec, weights = tables
    step_of = lambda n: n_steps - 1 - n
    rc = lambda n: _ret_chunk_of(direction, step_of(n), nx, nc)

    def body(*refs):
        if prev is None:
            (q_ref, k_ref, v_ref, dy_ref, st_ref, m_ref, qd_ref, kd_ref, cd_ref, w_ref,
             dq_ref, dk_ref, dv_ref, dl_ref, ds_sc, lam_sc) = refs
        else:
            (q_ref, k_ref, v_ref, dy_ref, st_ref, m_ref, qd_ref, kd_ref, cd_ref, w_ref, pq_ref, pk_ref, pv_ref,
             dq_ref, dk_ref, dv_ref, dl_ref, ds_sc, lam_sc) = refs
        n = pl.program_id(1)

        @pl.when(n == 0)
        def _():
            ds_sc[...] = jnp.zeros(ds_sc.shape, F32)
            lam_sc[...] = jnp.zeros(lam_sc.shape, F32)

        is_x = rc(n) < nx
        for hh in range(hpb):
            sl = slice(hh * HEAD_DIM, (hh + 1) * HEAD_DIM)
            qv = q_ref[:, sl]
            qf = qv.astype(F32)
            kf = k_ref[:, sl].astype(F32) * k_scale
            kv = kf.astype(BF16)
            vv = v_ref[:, sl]
            dyv = jnp.where(is_x, dy_ref[:, sl], 0.0).astype(BF16)
            state = st_ref[hh]
            dstate = ds_sc[hh]
            dstate_b = dstate.astype(BF16)
            msk, qd, kd = m_ref[hh], qd_ref[hh], kd_ref[hh]
            cd = cd_ref[hh, 0:1, :]
            a = _dot(qv, kv, NT) * msk
            da = (_dot(dyv, vv, NT) * msk).astype(BF16)
            dq_intra = _dot(da, kv, NN)
            dk_intra = _dot(da, qv, TN)
            dq_inter = _dot(dyv, state.astype(BF16), NT) * qd
            dk_inter = _dot(vv, dstate_b, NT) * kd
            dv = _dot(a.astype(BF16), dyv, TN) + _dot((kf * kd).astype(BF16), dstate_b, NN)
            lam_sc[hh] += (qf * (w_ref[0] * dq_intra + w_ref[1] * dq_inter)
                           + kf * (w_ref[2] * dk_intra + w_ref[3] * dk_inter)
                           + (c * cd) * state * dstate)
            ds_sc[hh] = _dot((qf * qd).astype(BF16), dyv, TN) + cd * dstate
            dq = dq_intra + dq_inter
            dk = (dk_intra + dk_inter) * k_scale
            if prev is not None:
                dq = dq + pq_ref[:, sl]
                dk = dk + pk_ref[:, sl]
                dv = dv + pv_ref[:, sl]
            dq_ref[:, sl] = dq
            dk_ref[:, sl] = dk
            dv_ref[:, sl] = dv

        @pl.when(n == n_steps - 1)
        def _():
            for hh in range(hpb):
                dl_ref[hh] = jnp.broadcast_to(jnp.sum(lam_sc[hh]), (SUBLANES, LANES))

    col = lambda off: (lambda g, n: (rc(n), off // bw + g))
    tab3 = lambda g, n: (g, 0, 0)
    ospec = pl.BlockSpec((c, bw), lambda g, n: (rc(n), g))
    in_specs = [pl.BlockSpec((c, bw), col(q_off)), pl.BlockSpec((c, bw), col(k_off)), pl.BlockSpec((c, bw), col(v_off)),
                pl.BlockSpec((c, bw), lambda g, n: (jnp.minimum(rc(n), nx - 1), g)),
                pl.BlockSpec((None, hpb, HEAD_DIM, HEAD_DIM), lambda g, n: (step_of(n), g, 0, 0)),
                pl.BlockSpec((hpb, c, c), tab3), pl.BlockSpec((hpb, c, LANES), tab3), pl.BlockSpec((hpb, c, LANES), tab3),
                pl.BlockSpec((hpb, SUBLANES, LANES), tab3), pl.BlockSpec((4, c, LANES), lambda g, n: (0, 0, 0))]
    args = [p, p, p, dy, states, mask, qdec, kdec, cdec, weights]
    aliases = {}
    if prev is not None:
        for k_out, arr in enumerate(prev):
            in_specs.append(ospec)
            args.append(arr)
            aliases[len(args) - 1] = k_out
    big = jax.ShapeDtypeStruct((t, rw), F32)
    return pl.pallas_call(
        body, name=name, grid=(heads // hpb, n_steps), in_specs=in_specs,
        out_specs=[ospec, ospec, ospec, pl.BlockSpec((hpb, SUBLANES, LANES), tab3)],
        out_shape=[big, big, big, jax.ShapeDtypeStruct((heads, SUBLANES, LANES), F32)],
        scratch_shapes=[pltpu.VMEM((hpb, HEAD_DIM, HEAD_DIM), F32), pltpu.VMEM((hpb, HEAD_DIM, HEAD_DIM), F32)],
        input_output_aliases=aliases, compiler_params=_cp(("parallel", "arbitrary")))(*args)


def _ret_out(name, y, p, g_off, lx):
    rw = y.shape[1]
    bw = _tile(math.gcd(g_off, rw), 512)
    tr = _tile(lx, 512, SUBLANES)

    def body(y_ref, g_ref, o_ref):
        for hh in range(bw // HEAD_DIM):
            sl = slice(hh * HEAD_DIM, (hh + 1) * HEAD_DIM)
            yv = y_ref[:, sl]
            gv = g_ref[:, sl].astype(F32)
            rinv = lax.rsqrt(jnp.mean(yv * yv, axis=-1, keepdims=True) + NORM_EPS)
            o_ref[:, sl] = (gv * _sigmoid(gv) * yv * rinv).astype(o_ref.dtype)

    spec = pl.BlockSpec((tr, bw), lambda i, g: (i, g))
    return pl.pallas_call(
        body, name=name, grid=(lx // tr, rw // bw),
        in_specs=[spec, pl.BlockSpec((tr, bw), lambda i, g: (i, g_off // bw + g))],
        out_specs=spec, out_shape=jax.ShapeDtypeStruct((lx, rw), BF16),
        compiler_params=_cp(("parallel", "parallel")))(y, p)


def _ret_out_bwd(name, dyr, y, p, g_off, lx):
    rw = y.shape[1]
    bw = _tile(math.gcd(g_off, rw), 512)
    tr = _tile(lx, 512, SUBLANES)

    def body(d_ref, y_ref, g_ref, dy_ref, dg_ref):
        for hh in range(bw // HEAD_DIM):
            sl = slice(hh * HEAD_DIM, (hh + 1) * HEAD_DIM)
            yv = y_ref[:, sl]
            gv = g_ref[:, sl].astype(F32)
            dv = d_ref[:, sl]
            rinv = lax.rsqrt(jnp.mean(yv * yv, axis=-1, keepdims=True) + NORM_EPS)
            yn = yv * rinv
            sg = _sigmoid(gv)
            dg_ref[:, sl] = (dv * yn * sg * (1.0 + gv * (1.0 - sg))).astype(dg_ref.dtype)
            dyn = dv * gv * sg
            dy_ref[:, sl] = rinv * (dyn - yn * jnp.mean(dyn * yn, axis=-1, keepdims=True))

    spec = pl.BlockSpec((tr, bw), lambda i, g: (i, g))
    return pl.pallas_call(
        body, name=name, grid=(lx // tr, rw // bw),
        in_specs=[spec, spec, pl.BlockSpec((tr, bw), lambda i, g: (i, g_off // bw + g))],
        out_specs=[spec, spec],
        out_shape=[jax.ShapeDtypeStruct((lx, rw), F32), jax.ShapeDtypeStruct((lx, rw), BF16)],
        compiler_params=_cp(("parallel", "parallel")))(dyr, y, p)


def _merge(name, pa, pr, p, ga_off, gb_off):
    r, d = pa.shape
    cw = _tile(math.gcd(math.gcd(ga_off, gb_off), d), 1024)
    tr = _tile(r, 512, SUBLANES)

    def body(pa_ref, pr_ref, ga_ref, gb_ref, o_ref):
        o_ref[...] = (_sigmoid(ga_ref[...].astype(F32)) * pa_ref[...].astype(F32)
                      + _sigmoid(gb_ref[...].astype(F32)) * pr_ref[...].astype(F32)).astype(o_ref.dtype)

    spec = pl.BlockSpec((tr, cw), lambda i, j: (i, j))
    return pl.pallas_call(
        body, name=name, grid=(r // tr, d // cw),
        in_specs=[spec, spec, pl.BlockSpec((tr, cw), lambda i, j: (i, ga_off // cw + j)),
                  pl.BlockSpec((tr, cw), lambda i, j: (i, gb_off // cw + j))],
        out_specs=spec, out_shape=jax.ShapeDtypeStruct((r, d), BF16),
        compiler_params=_cp(("parallel", "parallel")))(pa, pr, p, p)


def _ada_fwd(name, cond, w):
    rows, d = cond.shape
    n = w.shape[1]
    tn = _tile(n, 768)

    def body(c_ref, w_ref, o_ref, s_ref):
        cv = c_ref[...]
        sv = cv * _sigmoid(cv)
        s_ref[...] = sv
        o_ref[...] = _dot(sv.astype(BF16), w_ref[...].astype(BF16), NN)

    return pl.pallas_call(
        body, name=name, grid=(n // tn,),
        in_specs=[pl.BlockSpec((rows, d), lambda j: (0, 0)), pl.BlockSpec((d, tn), lambda j: (0, j))],
        out_specs=[pl.BlockSpec((rows, tn), lambda j: (0, j)), pl.BlockSpec((rows, d), lambda j: (0, 0))],
        out_shape=[jax.ShapeDtypeStruct((rows, n), F32), jax.ShapeDtypeStruct((rows, d), F32)],
        compiler_params=_cp(("arbitrary",)))(cond, w)


def _ada_bwd(name, s_cond, dmod, w):
    rows, d = s_cond.shape
    n = w.shape[1]
    tn = _tile(n, 768)

    def body(s_ref, dm_ref, w_ref, gw_ref, ds_ref):
        j = pl.program_id(0)

        @pl.when(j == 0)
        def _():
            ds_ref[...] = jnp.zeros(ds_ref.shape, F32)

        dmv = dm_ref[...].astype(BF16)
        gw_ref[...] = _dot(s_ref[...].astype(BF16), dmv, TN)
        ds_ref[...] += _dot(dmv, w_ref[...].astype(BF16), NT)

    return pl.pallas_call(
        body, name=name, grid=(n // tn,),
        in_specs=[pl.BlockSpec((rows, d), lambda j: (0, 0)), pl.BlockSpec((rows, tn), lambda j: (0, j)),
                  pl.BlockSpec((d, tn), lambda j: (0, j))],
        out_specs=[pl.BlockSpec((d, tn), lambda j: (0, j)), pl.BlockSpec((rows, d), lambda j: (0, 0))],
        out_shape=[jax.ShapeDtypeStruct((d, n), F32), jax.ShapeDtypeStruct((rows, d), F32)],
        compiler_params=_cp(("arbitrary",)))(s_cond, dmod, w)


def _sum_slots(name, a):
    s, r, c = a.shape

    def body(a_ref, o_ref):
        acc = a_ref[0]
        for k in range(1, s):
            acc = acc + a_ref[k]
        o_ref[...] = acc

    return pl.pallas_call(
        body, name=name, grid=(1,), in_specs=[pl.BlockSpec((s, r, c), lambda i: (0, 0, 0))],
        out_specs=pl.BlockSpec((r, c), lambda i: (0, 0)), out_shape=jax.ShapeDtypeStruct((r, c), F32),
        compiler_params=_cp(("arbitrary",)))(a)


def _cctx_grad(name, parts, c_ctx):
    s, r, d = parts.shape

    def body(p_ref, c_ref, o_ref):
        acc = p_ref[0]
        for k in range(1, s):
            acc = acc + p_ref[k]
        cv = c_ref[...]
        sg = _sigmoid(cv)
        o_ref[...] = acc[0:1, :] * sg * (1.0 + cv * (1.0 - sg))

    return pl.pallas_call(
        body, name=name, grid=(1,),
        in_specs=[pl.BlockSpec((s, r, d), lambda i: (0, 0, 0)), pl.BlockSpec((1, d), lambda i: (0, 0))],
        out_specs=pl.BlockSpec((1, d), lambda i: (0, 0)), out_shape=jax.ShapeDtypeStruct((1, d), F32),
        compiler_params=_cp(("arbitrary",)))(parts, c_ctx)


def _adamw(name, slots, w, m, v):
    s, r, c = slots.shape
    tr = _tile(r, 256 if c > 1024 else 512, SUBLANES)
    c1 = 1.0 - ADAM_B1 ** ADAM_STEP
    c2 = 1.0 - ADAM_B2 ** ADAM_STEP

    def body(s_ref, w_ref, m_ref, v_ref, g_ref, d_ref, mo_ref, vo_ref):
        g = s_ref[0].astype(F32)
        for k in range(1, s):
            g = g + s_ref[k].astype(F32)
        mn = ADAM_B1 * m_ref[...] + (1.0 - ADAM_B1) * g
        vn = ADAM_B2 * v_ref[...] + (1.0 - ADAM_B2) * (g * g)
        m_hat = mn / c1
        v_hat = vn / c2
        g_ref[...] = g
        mo_ref[...] = mn
        vo_ref[...] = vn
        d_ref[...] = -ADAM_LR * (m_hat / (jnp.sqrt(v_hat) + ADAM_EPS) + ADAM_WD * w_ref[...])

    spec = pl.BlockSpec((tr, c), lambda i: (i, 0))
    shp = jax.ShapeDtypeStruct((r, c), F32)
    return pl.pallas_call(
        body, name=name, grid=(r // tr,),
        in_specs=[pl.BlockSpec((s, tr, c), lambda i: (0, i, 0)), spec, spec, spec],
        out_specs=[spec] * 4, out_shape=[shp] * 4, compiler_params=_cp(("parallel",)))(slots, w, m, v)


def _coords():
    return lax.axis_index("x"), lax.axis_index("y"), lax.axis_index("c")


def _flip(coord, bit):
    return 1 - coord if bit else coord


def _all_gather_small(name, blk):
    r, ccols = blk.shape

    def body(x_ref, out_ref, send_sems, recv_sems, local_sem):
        x, y, c = _coords()
        me, sibling = (x, y, c), (x, y, 1 - c)
        chips = [(1 - x, y), (x, 1 - y), (1 - x, 1 - y)]

        def slot(px, py, pc):
            return out_ref.at[4 * px + 2 * py + pc]

        def copy(k, block, to, src=None):
            return pltpu.make_async_remote_copy(
                src_ref=slot(*block) if src is None else src, dst_ref=slot(*block),
                send_sem=send_sems.at[k], recv_sem=recv_sems.at[k], device_id=to, device_id_type=MESH)

        mine = pltpu.make_async_copy(x_ref, slot(*me), local_sem)
        mine.start()
        first = [copy(0, me, sibling, src=x_ref)]
        first += [copy(1 + j, me, (*chip, c), src=x_ref) for j, chip in enumerate(chips)]
        for cp in first:
            cp.start()
        passed = [copy(4 + j, (*chip, c), sibling) for j, chip in enumerate(chips)]
        for j, chip in enumerate(chips):
            copy(1 + j, (*chip, c), me).wait_recv()
            passed[j].start()
        copy(0, sibling, me).wait_recv()
        for j, chip in enumerate(chips):
            copy(4 + j, (*chip, 1 - c), me).wait_recv()
        for cp in first + passed:
            cp.wait_send()
        mine.wait()

    return pl.pallas_call(
        body, name=name, out_shape=jax.ShapeDtypeStruct((N_DEV, r, ccols), blk.dtype),
        in_specs=[pl.BlockSpec(memory_space=pltpu.VMEM)], out_specs=pl.BlockSpec(memory_space=pltpu.VMEM),
        scratch_shapes=[pltpu.SemaphoreType.DMA((7,)), pltpu.SemaphoreType.DMA((7,)), pltpu.SemaphoreType.DMA],
    )(blk)


def _comm_semaphores(n):
    return [pltpu.SemaphoreType.DMA((7 * n,)), pltpu.SemaphoreType.DMA((7 * n,)), pltpu.SemaphoreType.DMA((n,))]


def _gather_plan(ins, outs, send_sems, recv_sems, local_sems):
    n = len(ins)
    x, y, c = _coords()
    me, sibling = (x, y, c), (x, y, 1 - c)
    chips = [(1 - x, y), (x, 1 - y), (1 - x, 1 - y)]

    def slot(a, px, py, pc):
        return outs[a].at[4 * px + 2 * py + pc]

    def copy(a, k, block, to, src=None):
        return pltpu.make_async_remote_copy(
            src_ref=slot(a, *block) if src is None else src, dst_ref=slot(a, *block),
            send_sem=send_sems.at[7 * a + k], recv_sem=recv_sems.at[7 * a + k], device_id=to, device_id_type=MESH)

    def local(a):
        return pltpu.make_async_copy(ins[a], slot(a, *me), local_sems.at[a])

    def first(a):
        return [copy(a, 0, me, sibling, src=ins[a])] + [copy(a, 1 + j, me, (*chip, c), src=ins[a])
                                                        for j, chip in enumerate(chips)]

    def passed(a, j):
        return copy(a, 4 + j, (*chips[j], c), sibling)

    def start():
        for a in range(n):
            local(a).start()
            for cp in first(a):
                cp.start()

    def finish():
        for a in range(n):
            for j, chip in enumerate(chips):
                copy(a, 1 + j, (*chip, c), me).wait_recv()
                passed(a, j).start()
        for a in range(n):
            copy(a, 0, sibling, me).wait_recv()
            for j, chip in enumerate(chips):
                copy(a, 4 + j, (*chip, 1 - c), me).wait_recv()
        for a in range(n):
            for cp in first(a) + [passed(a, j) for j in range(3)]:
                cp.wait_send()
            local(a).wait()

    return start, finish


def _exchange_plan(ins, outs, send_sems, recv_sems, local_sems):
    n = len(ins)
    x, y, c = _coords()
    my_idx = 4 * x + 2 * y + c

    def local(a):
        return pltpu.make_async_copy(ins[a].at[my_idx], outs[a].at[my_idx], local_sems.at[a])

    def pair(a, rel):
        px, py, pc = _flip(x, rel & 4), _flip(y, rel & 2), _flip(c, rel & 1)
        peer_idx = 4 * px + 2 * py + pc
        sems = dict(send_sem=send_sems.at[7 * a + rel - 1], recv_sem=recv_sems.at[7 * a + rel - 1],
                    device_id=(px, py, pc), device_id_type=MESH)
        send = pltpu.make_async_remote_copy(src_ref=ins[a].at[peer_idx], dst_ref=outs[a].at[my_idx], **sems)
        recv = pltpu.make_async_remote_copy(src_ref=ins[a].at[my_idx], dst_ref=outs[a].at[peer_idx], **sems)
        return send, recv

    def start():
        for a in range(n):
            local(a).start()
            for rel in range(1, N_DEV):
                pair(a, rel)[0].start()

    def finish():
        for a in range(n):
            for rel in range(1, N_DEV):
                send, recv = pair(a, rel)
                recv.wait_recv()
                send.wait_send()
            local(a).wait()

    return start, finish


def _comm_plan(kind, ins, outs, send_sems, recv_sems, local_sems):
    plan = {"gather": _gather_plan, "exchange": _exchange_plan}[kind]
    return plan(ins, outs, send_sems, recv_sems, local_sems)


def _all_gather_big(name, shards):
    n = len(shards)

    def body(*refs):
        start, finish = _gather_plan(refs[:n], refs[n:2 * n], *refs[2 * n:])
        start()
        finish()

    any_spec = pl.BlockSpec(memory_space=pl.ANY)
    return pl.pallas_call(
        body, name=name, out_shape=[jax.ShapeDtypeStruct((N_DEV,) + s.shape, s.dtype) for s in shards],
        in_specs=[any_spec] * n, out_specs=[any_spec] * n, scratch_shapes=_comm_semaphores(n))(*shards)


def _rope_tables(lx, lc):
    rows = lx // GRID_W
    row = jnp.repeat(jnp.arange(rows, dtype=F32), GRID_W)
    col = jnp.tile(jnp.arange(GRID_W, dtype=F32), rows)
    half = HEAD_DIM // 2
    inv_freq = ROPE_THETA ** (-jnp.arange(0, half, 2, dtype=F32) / half)
    ang = jnp.concatenate([row[:, None] * inv_freq, col[:, None] * inv_freq], axis=-1)
    cos, sin = jnp.cos(ang), jnp.sin(ang)
    cosf = jnp.repeat(cos, 2, axis=-1)
    sinf = jnp.stack([-sin, sin], axis=-1).reshape(lx, HEAD_DIM)
    cosf = jnp.concatenate([cosf, jnp.ones((lc, HEAD_DIM), F32)], axis=0)
    sinf = jnp.concatenate([sinf, jnp.zeros((lc, HEAD_DIM), F32)], axis=0)
    return cosf, sinf


def _cols_full(g):
    return jnp.transpose(g, (1, 0, 2)).reshape(g.shape[1], N_DEV * g.shape[2])


def _cols_split(w):
    k, n = w.shape
    return jnp.transpose(w.reshape(k, N_DEV, n // N_DEV), (1, 0, 2))


def _pad_rows(a, rows):
    return jnp.pad(a, ((0, rows - a.shape[0]), (0, 0)))


def kernel(x, c, ctx, c_ctx, w_ada, b_ada, ffn1_w_in, ffn1_w_out, mix_w_in, attn_q_gain, attn_k_gain, ret_decay_logit, w_proj_attn, w_proj_ret, mix_w_out, ffn2_w_in, ffn2_w_out, final_norm, loss_target, m_c_ctx, m_w_ada, m_b_ada, m_ffn1_w_in, m_ffn1_w_out, m_mix_w_in, m_attn_q_gain, m_attn_k_gain, m_ret_decay_logit, m_w_proj_attn, m_w_proj_ret, m_mix_w_out, m_ffn2_w_in, m_ffn2_w_out, m_final_norm, v_c_ctx, v_w_ada, v_b_ada, v_ffn1_w_in, v_ffn1_w_out, v_mix_w_in, v_attn_q_gain, v_attn_k_gain, v_ret_decay_logit, v_w_proj_attn, v_w_proj_ret, v_mix_w_out, v_ffn2_w_in, v_ffn2_w_out, v_final_norm):
    lx, d = x.shape[1], x.shape[2]
    lc = ctx.shape[1]
    t = lx + lc
    aw, rw = w_proj_attn.shape[1], w_proj_ret.shape[1]
    pw = mix_w_in.shape[2] * N_DEV
    kw = (pw - aw - 4 * rw - 2 * d) // 2
    groups = aw // kw
    heads_r = rw // HEAD_DIM
    ka_off, va_off = aw, aw + kw
    qr_off = aw + 2 * kw
    kr_off, vr_off, gr_off = qr_off + rw, qr_off + 2 * rw, qr_off + 3 * rw
    ga_off, gb_off = qr_off + 4 * rw, qr_off + 4 * rw + d
    xi, yi, ci = _coords()
    me = 4 * xi + 2 * yi + ci

    col_names = ["ffn1_w_in", "mix_w_in", "w_proj_attn", "w_proj_ret", "ffn2_w_in"]
    row_names = ["ffn1_w_out", "mix_w_out", "ffn2_w_out"]
    shard = dict(ffn1_w_in=ffn1_w_in[0], mix_w_in=mix_w_in[0], w_proj_attn=w_proj_attn[0], w_proj_ret=w_proj_ret[0],
                 ffn2_w_in=ffn2_w_in[0], ffn1_w_out=ffn1_w_out[0], mix_w_out=mix_w_out[0], ffn2_w_out=ffn2_w_out[0])
    names = col_names + row_names
    bf_shard = {k: shard[k].astype(BF16) for k in names}
    full, landed = {}, {}

    def gather_of(keys):
        return "gather", [bf_shard[k] for k in keys]

    def keep_full(keys, gathered):
        for k, g in zip(keys, gathered):
            full[k] = _cols_full(g) if k in col_names else g.reshape(N_DEV * g.shape[1], g.shape[2])

    def exchange_of(grads):
        return "exchange", [_cols_split(g) if k in col_names else g.reshape(N_DEV, g.shape[0] // N_DEV, g.shape[1])
                            for k, g in grads.items()]

    def keep_landed(grads, got):
        landed.update(zip(grads.keys(), got))

    keep_full(["ffn1_w_in"], _all_gather_big("gather_first", [bf_shard["ffn1_w_in"]]))

    c_all = _all_gather_small("gather_cond", _pad_rows(c, SUBLANES))[:, 0, :]
    cond = _pad_rows(jnp.concatenate([c_all, c_ctx[None, :]], axis=0), 2 * SUBLANES)
    ada_part, s_cond = _ada_fwd("ada_fwd", cond, w_ada[0])
    ada_all = _all_gather_small("gather_ada", ada_part)
    mod_all = jnp.transpose(ada_all, (1, 0, 2)).reshape(2 * SUBLANES, N_MOD * d) + b_ada
    mod_x = lax.dynamic_index_in_dim(mod_all, me, axis=0, keepdims=False).reshape(N_MOD, d)
    mod_c = mod_all[N_DEV].reshape(N_MOD, d)
    mods = jnp.stack([mod_c, mod_x], axis=0)[:, :, None, :]
    sh1, sc1, g1, sh2, sc2, g2, sh3, sc3, g3 = [mods[:, k] for k in range(N_MOD)]

    h0 = jnp.concatenate([x[0], ctx[0]], axis=0)
    u1 = _rmsmod("ffn1_norm", h0, sc1, sh1, lx)
    keys = ["ffn1_w_out", "mix_w_in"]
    (z1, s1), got = _ffn_in("ffn1_in", u1, full["ffn1_w_in"], comm=gather_of(keys))
    keep_full(keys, got)
    keys = ["w_proj_attn", "w_proj_ret", "mix_w_out", "ffn2_w_out"]
    (h1, f1), got = _mm_residual("ffn1_out", s1, full["ffn1_w_out"], h0, g1, 0.5, lx, comm=gather_of(keys))
    keep_full(keys, got)

    u2 = _rmsmod("mix_norm", h1, sc2, sh2, lx)
    keys = ["ffn2_w_in"]
    p, got = _mm_nn("mix_in", u2, full["mix_w_in"], BF16, tn_pref=512, comm=gather_of(keys))
    keep_full(keys, got)
    cosf, sinf = _rope_tables(lx, lc)
    q_rot, k_rot = _qk_prep("qk_prep", p, cosf, sinf, attn_q_gain, attn_k_gain, aw, kw)
    ya, lse = _attn_fwd("attn_fwd", q_rot, k_rot, p, va_off, lx, groups)

    decay = ret_decay_logit[0].astype(F32)
    log_gamma = jax.nn.log_sigmoid(decay)
    r_offs = (qr_off, kr_off, vr_off, rw)
    tab = [_ret_tables(log_gamma[k], k) for k in range(2)]
    y_f, st_f = _ret_fwd("ret_fwd_a", p, r_offs, tab[0], 0, lx, None)
    y_r, st_b = _ret_fwd("ret_fwd_b", p, r_offs, tab[1], 1, lx, y_f)
    yr = _ret_out("ret_out", y_r, p, gr_off, lx)

    pa = _mm_nn("proj_attn", ya, full["w_proj_attn"], BF16)
    pr = _mm_nn("proj_ret", yr, full["w_proj_ret"], BF16)
    mg = _merge("merge", pa, pr, p, ga_off, gb_off)
    h2, o2 = _mm_residual("mix_out", mg, full["mix_w_out"], h1, g2, 1.0, lx)

    u3 = _rmsmod("ffn2_norm", h2, sc3, sh3, lx)
    z3, s3 = _ffn_in("ffn2_in", u3, full["ffn2_w_in"])
    h3, f3 = _mm_residual("ffn2_out", s3, full["ffn2_w_out"], h2, g3, 0.5, lx)
    dh3, d_fn, loss_tile = _final_loss("final_loss", h3, final_norm[None, :], loss_target[0])

    df3, dg3 = _gate_bwd("ffn2_gate_bwd", dh3, f3, g3, 0.5, lx)
    dz3 = _ffn_out_bwd("ffn2_out_bwd", df3, full["ffn2_w_out"], z3)
    grads = {"ffn2_w_out": _mm_tn("ffn2_out_dw", s3, df3, BF16)}
    g_ffn2_w_in, got = _ffn_in_bwd_w("ffn2_in_dw", u3, dz3, comm=exchange_of(grads))
    keep_landed(grads, got)
    grads = {"ffn2_w_in": g_ffn2_w_in}
    du3, got = _ffn_in_bwd_x("ffn2_in_bwd", dz3, full["ffn2_w_in"], comm=exchange_of(grads))
    keep_landed(grads, got)
    dh2, dsc3, dsh3 = _rmsmod_bwd("ffn2_norm_bwd", du3, h2, sc3, dh3, lx, lx)

    do2, dg2 = _gate_bwd("mix_gate_bwd", dh2, o2, g2, 1.0, lx)
    dpa, dpr, dga, dgb = _merge_bwd("merge_bwd", do2, full["mix_w_out"], pa, pr, p, ga_off, gb_off)
    dya = _mm_nt("proj_attn_bwd", dpa, full["w_proj_attn"], BF16)
    dyr = _mm_nt("proj_ret_bwd", dpr, full["w_proj_ret"], F32)
    mix_grads = {"mix_w_out": _mm_tn("mix_out_dw", mg, do2, BF16),
                 "w_proj_attn": _mm_tn("proj_attn_dw", ya, dpa, BF16),
                 "w_proj_ret": _mm_tn("proj_ret_dw", yr, dpr, BF16)}

    dy_ret, dgr = _ret_out_bwd("ret_out_bwd", dyr, y_r, p, gr_off, lx)
    dqr, dkr, dvr, dl_f = _ret_bwd("ret_bwd_a", p, r_offs, tab[0], st_f, dy_ret, 0, lx, None)
    dqr, dkr, dvr, dl_b = _ret_bwd("ret_bwd_b", p, r_offs, tab[1], st_b, dy_ret, 1, lx, (dqr, dkr, dvr))
    d_lam = jnp.stack([dl_f[:, 0, 0], dl_b[:, 0, 0]], axis=0)
    d_decay = d_lam * jax.nn.sigmoid(-decay)

    delta = _attn_delta("attn_delta", ya, dya)
    dq_rot, dk_rot, dva = _attn_bwd("attn_bwd", q_rot, k_rot, p, va_off, dya, lse, delta, lx, groups)
    dqa, dka, d_qg, d_kg = _qk_prep_bwd("qk_prep_bwd", dq_rot, dk_rot, p, cosf, sinf, attn_q_gain, attn_k_gain, aw, kw)

    dp = jnp.concatenate([dqa, dka, dva, dqr.astype(BF16), dkr.astype(BF16), dvr.astype(BF16),
                          _pad_rows(dgr, t), _pad_rows(dga, t), _pad_rows(dgb, t)], axis=1)
    grads = {"mix_w_in": _mm_tn("mix_in_dw", u2, dp, BF16, tn_pref=512)}
    du2, got = _mm_nt("mix_in_bwd", dp, full["mix_w_in"], F32, comm=exchange_of(grads))
    keep_landed(grads, got)
    dh1, dsc2, dsh2 = _rmsmod_bwd("mix_norm_bwd", du2, h1, sc2, dh2, lx, t)

    df1, dg1 = _gate_bwd("ffn1_gate_bwd", dh1, f1, g1, 0.5, lx)
    dz1, got = _ffn_out_bwd("ffn1_out_bwd", df1, full["ffn1_w_out"], z1, comm=exchange_of(mix_grads))
    keep_landed(mix_grads, got)
    grads = {"ffn1_w_out": _mm_tn("ffn1_out_dw", s1, df1, BF16)}
    g_ffn1_w_in, got = _ffn_in_bwd_w("ffn1_in_dw", u1, dz1, comm=exchange_of(grads))
    keep_landed(grads, got)
    grads = {"ffn1_w_in": g_ffn1_w_in}
    du1, got = _ffn_in_bwd_x("ffn1_in_bwd", dz1, full["ffn1_w_in"], comm=exchange_of(grads))
    keep_landed(grads, got)
    grad_x, dsc1, dsh1 = _rmsmod_bwd("ffn1_norm_bwd", du1, h0, sc1, dh1, lx, lx)

    zero = jnp.zeros((1, d), F32)
    dmod_c = jnp.concatenate([dsh1[0], dsc1[0], dg1[0], dsh2[0], dsc2[0], zero, zero, zero, zero], axis=0)
    dmod_x = jnp.concatenate([dsh1[1], dsc1[1], dg1[1], dsh2[1], dsc2[1], dg2[1], dsh3[1], dsc3[1], dg3[1]], axis=0)
    misc = jnp.concatenate([d_qg[0], d_kg[0], d_decay.reshape(-1), loss_tile[0, 0:1]])
    misc = jnp.pad(misc, (0, d - misc.shape[0]))[None, :]
    n_small = 3 * SUBLANES
    small = _pad_rows(jnp.concatenate([dmod_c, dmod_x, d_fn, misc], axis=0), n_small)
    small_all = _all_gather_small("gather_small", small)
    small_sum = _sum_slots("sum_small", small_all)
    dmod_c_sum, dmod_x_sum = small_sum[0:N_MOD], small_sum[N_MOD:2 * N_MOD]
    g_final_norm = small_sum[2 * N_MOD]
    misc_sum = small_sum[2 * N_MOD + 1]
    g_qg = misc_sum[0:HEAD_DIM][None, :]
    g_kg = misc_sum[HEAD_DIM:2 * HEAD_DIM][None, :]
    g_decay = misc_sum[2 * HEAD_DIM:2 * HEAD_DIM + 2 * heads_r].reshape(1, 2, heads_r)
    loss = misc_sum[2 * HEAD_DIM + 2 * heads_r]
    g_b_ada = (dmod_x_sum + dmod_c_sum).reshape(1, N_MOD * d)

    n_ada = w_ada.shape[2]
    dmod_rows = jnp.concatenate([small_all[:, N_MOD:2 * N_MOD, :].reshape(N_DEV, N_MOD * d),
                                 dmod_c_sum.reshape(1, N_MOD * d)], axis=0)
    dmod_mine = _pad_rows(lax.dynamic_slice_in_dim(dmod_rows, me * n_ada, n_ada, axis=1), 2 * SUBLANES)
    g_w_ada, ds_cond = _ada_bwd("ada_bwd", s_cond, dmod_mine, w_ada[0])
    cctx_parts = _all_gather_small("gather_cctx", ds_cond[N_DEV:N_DEV + SUBLANES])
    g_c_ctx = _cctx_grad("cctx_grad", cctx_parts, c_ctx[None, :])[0]

    mom = dict(ffn1_w_in=(m_ffn1_w_in, v_ffn1_w_in), mix_w_in=(m_mix_w_in, v_mix_w_in),
               w_proj_attn=(m_w_proj_attn, v_w_proj_attn), w_proj_ret=(m_w_proj_ret, v_w_proj_ret),
               ffn2_w_in=(m_ffn2_w_in, v_ffn2_w_in), ffn1_w_out=(m_ffn1_w_out, v_ffn1_w_out),
               mix_w_out=(m_mix_w_out, v_mix_w_out), ffn2_w_out=(m_ffn2_w_out, v_ffn2_w_out))
    res = {}
    for k in names:
        res[k] = _adamw("adamw_" + k, landed[k], shard[k], mom[k][0][0], mom[k][1][0])
    res["w_ada"] = _adamw("adamw_w_ada", g_w_ada[None], w_ada[0], m_w_ada[0], v_w_ada[0])

    def pack(cc, ba, qg, kg, dec, fn):
        misc_row = jnp.concatenate([qg.reshape(-1), kg.reshape(-1), dec.reshape(-1)])
        misc_row = jnp.pad(misc_row, (0, d - misc_row.shape[0]))[None, :]
        return _pad_rows(jnp.concatenate([cc.reshape(1, d), ba.reshape(N_MOD, d), fn.reshape(1, d), misc_row], axis=0),
                         2 * SUBLANES)

    sg, sd, sm, sv = _adamw(
        "adamw_small", pack(g_c_ctx, g_b_ada, g_qg, g_kg, g_decay, g_final_norm)[None],
        pack(c_ctx, b_ada, attn_q_gain, attn_k_gain, ret_decay_logit, final_norm),
        pack(m_c_ctx, m_b_ada, m_attn_q_gain, m_attn_k_gain, m_ret_decay_logit, m_final_norm),
        pack(v_c_ctx, v_b_ada, v_attn_q_gain, v_attn_k_gain, v_ret_decay_logit, v_final_norm))

    def unpack(a):
        misc_row = a[N_MOD + 2]
        return dict(c_ctx=a[0], b_ada=a[1:1 + N_MOD].reshape(1, N_MOD * d), final_norm=a[N_MOD + 1],
                    attn_q_gain=misc_row[0:HEAD_DIM][None, :], attn_k_gain=misc_row[HEAD_DIM:2 * HEAD_DIM][None, :],
                    ret_decay_logit=misc_row[2 * HEAD_DIM:2 * HEAD_DIM + 2 * heads_r].reshape(1, 2, heads_r))

    small_out = [unpack(a) for a in (sg, sd, sm, sv)]
    order = ["c_ctx", "w_ada", "b_ada", "ffn1_w_in", "ffn1_w_out", "mix_w_in", "attn_q_gain", "attn_k_gain",
             "ret_decay_logit", "w_proj_attn", "w_proj_ret", "mix_w_out", "ffn2_w_in", "ffn2_w_out", "final_norm"]
    outs = [loss, grad_x[None]]
    for which in range(4):
        for k in order:
            outs.append(res[k][which][None] if k in res else small_out[which][k])
    return tuple(outs)
```

```python
import math

import jax
import jax.numpy as jnp
from jax import lax
from jax.experimental import pallas as pl
from jax.experimental.pallas import tpu as pltpu

F32 = jnp.float32
BF16 = jnp.bfloat16
MESH = pl.DeviceIdType.MESH

N_DEV = 8
HEAD_DIM = 128
GRID_W = 64
ROPE_THETA = 10000.0
NORM_EPS = 1e-6
RET_CHUNK = 128
N_MOD = 9
LANES = 128
SUBLANES = 8
V7X_VMEM_BYTES = 64 * 1024 * 1024
VMEM_LIMIT = V7X_VMEM_BYTES - 8 * 1024 * 1024
K_TILE = 2560
M_TILE_BIG = 1408
M_TILE_DW = 2048
LOG2E = 1.4426950408889634
LN2 = 0.6931471805599453

ADAM_LR = 0.001
ADAM_B1 = 0.9
ADAM_B2 = 0.999
ADAM_EPS = 1e-08
ADAM_WD = 0.01
ADAM_STEP = 10

NN = (((1,), (0,)), ((), ()))
NT = (((1,), (1,)), ((), ()))
TN = (((0,), (0,)), ((), ()))


def _tile(n, pref, align=LANES):
    best = None
    t = align
    while t <= min(n, pref):
        if n % t == 0:
            best = t
        t += align
    return n if best is None else best


def _cp(sem):
    return pltpu.CompilerParams(dimension_semantics=sem, vmem_limit_bytes=VMEM_LIMIT)


def _sigmoid(v):
    return 1.0 / (1.0 + jnp.exp(-v))


def _dot(a, b, dims):
    return lax.dot_general(a, b, dims, preferred_element_type=F32)


def _mm(name, a, a_spec, b_list, dims, grid, out_shapes, out_specs, acc_shape, epi, extras=(), comm=None):
    nb, ne, no = len(b_list), len(extras), len(out_shapes)
    nk = grid[2]
    kind, c_arrays = comm if comm is not None else (None, [])
    ncm = len(c_arrays)

    def body(*refs):
        a_ref = refs[0]
        b_refs = refs[1:1 + nb]
        e_refs = refs[1 + nb:1 + nb + ne]
        pos = 1 + nb + ne
        c_ins = refs[pos:pos + ncm]
        o_refs = refs[pos + ncm:pos + ncm + no]
        c_outs = refs[pos + ncm + no:pos + 2 * ncm + no]
        scratch = refs[pos + 2 * ncm + no:]
        accs = scratch[:0 if nk == 1 else nb]
        ids = [pl.program_id(axis) for axis in range(3)]
        if ncm:
            start_comm, finish_comm = _comm_plan(kind, c_ins, c_outs, *scratch[len(accs):])
            pl.when(jnp.logical_and(jnp.logical_and(ids[0] == 0, ids[1] == 0), ids[2] == 0))(start_comm)

        def finish(tiles):
            vals = epi(tiles, e_refs)
            for o_ref, v in zip(o_refs, vals):
                if isinstance(v, tuple):
                    for idx, part in enumerate(v):
                        o_ref[idx] = part.astype(o_ref.dtype)
                else:
                    o_ref[...] = v.astype(o_ref.dtype)

        if nk == 1:
            finish([_dot(a_ref[...], b_ref[...], dims) for b_ref in b_refs])
        else:
            @pl.when(ids[2] == 0)
            def _():
                for acc in accs:
                    acc[...] = jnp.zeros(acc.shape, F32)

            av = a_ref[...]
            for b_ref, acc in zip(b_refs, accs):
                acc[...] += _dot(av, b_ref[...], dims)

            @pl.when(ids[2] == nk - 1)
            def _():
                finish([acc[...] for acc in accs])

        if ncm:
            pl.when(jnp.logical_and(jnp.logical_and(ids[0] == grid[0] - 1, ids[1] == grid[1] - 1),
                                    ids[2] == nk - 1))(finish_comm)

    any_spec = pl.BlockSpec(memory_space=pl.ANY)
    c_shapes = [jax.ShapeDtypeStruct(((N_DEV,) if kind == "gather" else ()) + s.shape, s.dtype) for s in c_arrays]
    scratch_shapes = [] if nk == 1 else [pltpu.VMEM(acc_shape, F32)] * nb
    if ncm:
        scratch_shapes = scratch_shapes + _comm_semaphores(ncm)
    semantics = ("arbitrary",) * 3 if ncm else ("parallel", "parallel", "arbitrary")
    outs = pl.pallas_call(
        body, name=name, grid=grid,
        in_specs=[a_spec] + [s for _, s in b_list] + [s for _, s in extras] + [any_spec] * ncm,
        out_specs=list(out_specs) + [any_spec] * ncm, out_shape=list(out_shapes) + c_shapes,
        scratch_shapes=scratch_shapes, compiler_params=_cp(semantics),
    )(a, *[b for b, _ in b_list], *[e for e, _ in extras], *c_arrays)
    main = outs[0] if no == 1 else tuple(outs[:no])
    return main if comm is None else (main, list(outs[no:]))


def _plain(accs, _):
    return (accs[0],)


def _mm_nn(name, a, b, out_dtype, tm_pref=1024, tn_pref=1024, tk_pref=K_TILE, comm=None):
    m, k = a.shape
    n = b.shape[1]
    tm, tn, tk = _tile(m, tm_pref), _tile(n, tn_pref), _tile(k, tk_pref)
    return _mm(name, a, pl.BlockSpec((tm, tk), lambda i, j, kk: (i, kk)),
               [(b, pl.BlockSpec((tk, tn), lambda i, j, kk: (kk, j)))], NN, (m // tm, n // tn, k // tk),
               [jax.ShapeDtypeStruct((m, n), out_dtype)], [pl.BlockSpec((tm, tn), lambda i, j, kk: (i, j))],
               (tm, tn), _plain, comm=comm)


def _mm_nt(name, a, b, out_dtype, tm_pref=1024, tn_pref=1024, tk_pref=K_TILE, comm=None):
    m, k = a.shape
    n = b.shape[0]
    tm, tn, tk = _tile(m, tm_pref), _tile(n, tn_pref), _tile(k, tk_pref)
    return _mm(name, a, pl.BlockSpec((tm, tk), lambda i, j, kk: (i, kk)),
               [(b, pl.BlockSpec((tn, tk), lambda i, j, kk: (j, kk)))], NT, (m // tm, n // tn, k // tk),
               [jax.ShapeDtypeStruct((m, n), out_dtype)], [pl.BlockSpec((tm, tn), lambda i, j, kk: (i, j))],
               (tm, tn), _plain, comm=comm)


def _mm_tn(name, a, b, out_dtype, rows=None, tm_pref=1024, tn_pref=1024, tk_pref=K_TILE, comm=None):
    k = a.shape[0] if rows is None else rows
    m, n = a.shape[1], b.shape[1]
    tm, tn, tk = _tile(m, tm_pref), _tile(n, tn_pref), _tile(k, tk_pref)
    return _mm(name, a, pl.BlockSpec((tk, tm), lambda i, j, kk: (kk, i)),
               [(b, pl.BlockSpec((tk, tn), lambda i, j, kk: (kk, j)))], TN, (m // tm, n // tn, k // tk),
               [jax.ShapeDtypeStruct((m, n), out_dtype)], [pl.BlockSpec((tm, tn), lambda i, j, kk: (i, j))],
               (tm, tn), _plain, comm=comm)


def _ffn_in(name, u, w_in, comm=None):
    r, d = u.shape
    f = w_in.shape[1] // 2
    tm, tn, tk = _tile(r, 1024), _tile(f, 512), _tile(d, K_TILE)
    nf = f // tn

    def epi(accs, _):
        za, zb = accs
        s = za * _sigmoid(za) * zb
        return (za, zb), s

    return _mm(name, u, pl.BlockSpec((tm, tk), lambda i, j, kk: (i, kk)),
               [(w_in, pl.BlockSpec((tk, tn), lambda i, j, kk: (kk, j))),
                (w_in, pl.BlockSpec((tk, tn), lambda i, j, kk: (kk, j + nf)))],
               NN, (r // tm, nf, d // tk),
               [jax.ShapeDtypeStruct((2, r, f), BF16), jax.ShapeDtypeStruct((r, f), BF16)],
               [pl.BlockSpec((2, tm, tn), lambda i, j, kk: (0, i, j)), pl.BlockSpec((tm, tn), lambda i, j, kk: (i, j))],
               (tm, tn), epi, comm=comm)


def _mm_residual(name, a, w, res, gate, gate_scale, lx, comm=None):
    r, k = a.shape
    n = w.shape[1]
    tm, tn, tk = _tile(r, 1024), _tile(n, 1024), _tile(k, K_TILE)

    def epi(accs, e_refs):
        res_ref, g_ref = e_refs
        rows = pl.program_id(0) * tm + lax.broadcasted_iota(jnp.int32, (tm, 1), 0)
        g = jnp.where(rows < lx, g_ref[1], g_ref[0])
        return res_ref[...] + gate_scale * g * accs[0], accs[0]

    return _mm(name, a, pl.BlockSpec((tm, tk), lambda i, j, kk: (i, kk)),
               [(w, pl.BlockSpec((tk, tn), lambda i, j, kk: (kk, j)))], NN, (r // tm, n // tn, k // tk),
               [jax.ShapeDtypeStruct((r, n), F32), jax.ShapeDtypeStruct((r, n), BF16)],
               [pl.BlockSpec((tm, tn), lambda i, j, kk: (i, j))] * 2, (tm, tn), epi,
               extras=[(res, pl.BlockSpec((tm, tn), lambda i, j, kk: (i, j))),
                       (gate, pl.BlockSpec((2, 1, tn), lambda i, j, kk: (0, 0, j)))], comm=comm)


def _ffn_out_bwd(name, df, w_out, z, comm=None):
    r, d = df.shape
    f = w_out.shape[0]
    tm, tn, tk = _tile(r, M_TILE_BIG), _tile(f, 512), _tile(d, K_TILE)

    def epi(accs, e_refs):
        ds = accs[0]
        za = e_refs[0][0].astype(F32)
        zb = e_refs[0][1].astype(F32)
        sg = _sigmoid(za)
        da = ds * zb * sg * (1.0 + za * (1.0 - sg))
        db = ds * za * sg
        return ((da, db),)

    zspec = pl.BlockSpec((2, tm, tn), lambda i, j, kk: (0, i, j))
    return _mm(name, df, pl.BlockSpec((tm, tk), lambda i, j, kk: (i, kk)),
               [(w_out, pl.BlockSpec((tn, tk), lambda i, j, kk: (j, kk)))], NT, (r // tm, f // tn, d // tk),
               [jax.ShapeDtypeStruct((2, r, f), BF16)], [zspec], (tm, tn), epi, extras=[(z, zspec)], comm=comm)


def _ffn_in_bwd_x(name, dz, w_in, comm=None):
    _, r, f = dz.shape
    d = w_in.shape[0]
    tm, tn, tk = _tile(r, 1024), _tile(d, 2048), _tile(f, K_TILE)
    nkf = f // tk
    return _mm(name, dz, pl.BlockSpec((None, tm, tk), lambda i, j, kk: (kk // nkf, i, kk % nkf)),
               [(w_in, pl.BlockSpec((tn, tk), lambda i, j, kk: (j, kk)))], NT, (r // tm, d // tn, 2 * nkf),
               [jax.ShapeDtypeStruct((r, d), F32)], [pl.BlockSpec((tm, tn), lambda i, j, kk: (i, j))],
               (tm, tn), _plain, comm=comm)


def _ffn_in_bwd_w(name, u, dz, comm=None):
    r, d = u.shape
    f = dz.shape[2]
    tm, tn, tk = _tile(d, M_TILE_DW), _tile(f, 512), _tile(r, K_TILE)
    nf = f // tn
    return _mm(name, u, pl.BlockSpec((tk, tm), lambda i, j, kk: (kk, i)),
               [(dz, pl.BlockSpec((None, tk, tn), lambda i, j, kk: (j // nf, kk, j % nf)))], TN,
               (d // tm, 2 * nf, r // tk),
               [jax.ShapeDtypeStruct((d, 2 * f), BF16)], [pl.BlockSpec((tm, tn), lambda i, j, kk: (i, j))],
               (tm, tn), _plain, comm=comm)


def _merge_bwd(name, dout, w_out, pa, pr, p, ga_off, gb_off):
    r, d = dout.shape
    n = w_out.shape[0]
    cw = math.gcd(math.gcd(ga_off, gb_off), n)
    tm, tn, tk = _tile(r, 1024), _tile(cw, 512), _tile(d, K_TILE)

    def epi(accs, e_refs):
        dm = accs[0]
        pa_ref, pr_ref, ga_ref, gb_ref = e_refs
        sa = _sigmoid(ga_ref[...].astype(F32))
        sb = _sigmoid(gb_ref[...].astype(F32))
        pav = pa_ref[...].astype(F32)
        prv = pr_ref[...].astype(F32)
        return dm * sa, dm * sb, dm * pav * sa * (1.0 - sa), dm * prv * sb * (1.0 - sb)

    o_spec = pl.BlockSpec((tm, tn), lambda i, j, kk: (i, j))
    return _mm(name, dout, pl.BlockSpec((tm, tk), lambda i, j, kk: (i, kk)),
               [(w_out, pl.BlockSpec((tn, tk), lambda i, j, kk: (j, kk)))], NT, (r // tm, n // tn, d // tk),
               [jax.ShapeDtypeStruct((r, n), BF16)] * 4, [o_spec] * 4, (tm, tn), epi,
               extras=[(pa, o_spec), (pr, o_spec),
                       (p, pl.BlockSpec((tm, tn), lambda i, j, kk: (i, ga_off // tn + j))),
                       (p, pl.BlockSpec((tm, tn), lambda i, j, kk: (i, gb_off // tn + j)))])


def _row_tile(r, lx, d):
    pref = 256 if d > 1024 else 512
    return _tile(math.gcd(r, lx), pref, SUBLANES)


def _rmsmod(name, h, scale, shift, lx, rows=None):
    r = h.shape[0] if rows is None else rows
    d = h.shape[1]
    tr = _row_tile(r, lx, d)
    nx = lx // tr
    cls = lambda i: (jnp.where(i < nx, 1, 0), 0, 0)

    def body(h_ref, sc_ref, sh_ref, u_ref):
        hv = h_ref[...]
        rinv = lax.rsqrt(jnp.mean(hv * hv, axis=-1, keepdims=True) + NORM_EPS)
        u_ref[...] = (hv * rinv * (1.0 + sc_ref[...]) + sh_ref[...]).astype(u_ref.dtype)

    return pl.pallas_call(
        body, name=name, grid=(r // tr,),
        in_specs=[pl.BlockSpec((tr, d), lambda i: (i, 0)), pl.BlockSpec((None, 1, d), cls), pl.BlockSpec((None, 1, d), cls)],
        out_specs=pl.BlockSpec((tr, d), lambda i: (i, 0)), out_shape=jax.ShapeDtypeStruct((r, d), BF16),
        compiler_params=_cp(("parallel",)))(h, scale, shift)


def _rmsmod_bwd(name, du, h, scale, dh_in, lx, rows_out):
    r, d = du.shape
    rin = dh_in.shape[0]
    tr = _row_tile(math.gcd(r, math.gcd(rin, rows_out)), lx, d)
    nx, nin, nout = lx // tr, rin // tr, rows_out // tr
    cls = lambda i: (jnp.where(i < nx, 1, 0), 0, 0)

    def body(du_ref, h_ref, sc_ref, dhin_ref, dh_ref, dsc_ref, dsh_ref):
        i = pl.program_id(0)
        hv = h_ref[...]
        duv = du_ref[...]
        rinv = lax.rsqrt(jnp.mean(hv * hv, axis=-1, keepdims=True) + NORM_EPS)
        nv = hv * rinv
        dn = duv * (1.0 + sc_ref[...])

        @pl.when(jnp.logical_or(i == 0, i == nx))
        def _():
            dsc_ref[...] = jnp.zeros(dsc_ref.shape, F32)
            dsh_ref[...] = jnp.zeros(dsh_ref.shape, F32)

        dsc_ref[...] += jnp.sum(duv * nv, axis=0, keepdims=True)
        dsh_ref[...] += jnp.sum(duv, axis=0, keepdims=True)

        @pl.when(i < nout)
        def _():
            dh = rinv * (dn - nv * jnp.mean(dn * nv, axis=-1, keepdims=True))
            dh_ref[...] = dh + jnp.where(i < nin, dhin_ref[...], 0.0)

    return pl.pallas_call(
        body, name=name, grid=(r // tr,),
        in_specs=[pl.BlockSpec((tr, d), lambda i: (i, 0)), pl.BlockSpec((tr, d), lambda i: (i, 0)),
                  pl.BlockSpec((None, 1, d), cls), pl.BlockSpec((tr, d), lambda i: (jnp.minimum(i, nin - 1), 0))],
        out_specs=[pl.BlockSpec((tr, d), lambda i: (jnp.minimum(i, nout - 1), 0)),
                   pl.BlockSpec((None, 1, d), cls), pl.BlockSpec((None, 1, d), cls)],
        out_shape=[jax.ShapeDtypeStruct((rows_out, d), F32), jax.ShapeDtypeStruct((2, 1, d), F32),
                   jax.ShapeDtypeStruct((2, 1, d), F32)],
        compiler_params=_cp(("arbitrary",)))(du, h, scale, dh_in)


def _gate_bwd(name, dh, f, gate, gate_scale, lx):
    r, d = dh.shape
    tr = _row_tile(r, lx, d)
    nx = lx // tr
    cls = lambda i: (jnp.where(i < nx, 1, 0), 0, 0)

    def body(dh_ref, f_ref, g_ref, df_ref, dg_ref):
        i = pl.program_id(0)
        dhv = dh_ref[...]

        @pl.when(jnp.logical_or(i == 0, i == nx))
        def _():
            dg_ref[...] = jnp.zeros(dg_ref.shape, F32)

        df_ref[...] = (gate_scale * g_ref[...] * dhv).astype(df_ref.dtype)
        dg_ref[...] += jnp.sum(gate_scale * f_ref[...].astype(F32) * dhv, axis=0, keepdims=True)

    return pl.pallas_call(
        body, name=name, grid=(r // tr,),
        in_specs=[pl.BlockSpec((tr, d), lambda i: (i, 0)), pl.BlockSpec((tr, d), lambda i: (i, 0)),
                  pl.BlockSpec((None, 1, d), cls)],
        out_specs=[pl.BlockSpec((tr, d), lambda i: (i, 0)), pl.BlockSpec((None, 1, d), cls)],
        out_shape=[jax.ShapeDtypeStruct((r, d), BF16), jax.ShapeDtypeStruct((2, 1, d), F32)],
        compiler_params=_cp(("arbitrary",)))(dh, f, gate)


def _final_loss(name, h, final_norm, target):
    r, d = h.shape
    tr = _row_tile(r, r, d)

    def body(h_ref, fn_ref, t_ref, dh_ref, dfn_ref, loss_ref):
        i = pl.program_id(0)
        hv = h_ref[...]
        rinv = lax.rsqrt(jnp.mean(hv * hv, axis=-1, keepdims=True) + NORM_EPS)
        nv = hv * rinv
        fn = fn_ref[...]
        err = nv * fn - t_ref[...]
        dy = err * (1.0 / d)

        @pl.when(i == 0)
        def _():
            dfn_ref[...] = jnp.zeros(dfn_ref.shape, F32)
            loss_ref[...] = jnp.zeros(loss_ref.shape, F32)

        loss_ref[...] += 0.5 * jnp.sum(jnp.mean(err * err, axis=-1, keepdims=True), axis=0, keepdims=True)
        dfn_ref[...] += jnp.sum(dy * nv, axis=0, keepdims=True)
        dn = dy * fn
        dh_ref[...] = rinv * (dn - nv * jnp.mean(dn * nv, axis=-1, keepdims=True))

    return pl.pallas_call(
        body, name=name, grid=(r // tr,),
        in_specs=[pl.BlockSpec((tr, d), lambda i: (i, 0)), pl.BlockSpec((1, d), lambda i: (0, 0)),
                  pl.BlockSpec((tr, d), lambda i: (i, 0))],
        out_specs=[pl.BlockSpec((tr, d), lambda i: (i, 0)), pl.BlockSpec((1, d), lambda i: (0, 0)),
                   pl.BlockSpec((SUBLANES, LANES), lambda i: (0, 0))],
        out_shape=[jax.ShapeDtypeStruct((r, d), F32), jax.ShapeDtypeStruct((1, d), F32),
                   jax.ShapeDtypeStruct((SUBLANES, LANES), F32)],
        compiler_params=_cp(("arbitrary",)))(h, final_norm, target)


def _swap_pairs(t):
    lane = lax.broadcasted_iota(jnp.int32, t.shape, 1)
    nxt = pltpu.roll(t, HEAD_DIM - 1, 1)
    prv = pltpu.roll(t, 1, 1)
    return jnp.where(lane % 2 == 0, nxt, prv)


def _qk_prep(name, p, cosf, sinf, q_gain, k_gain, aw, kw):
    t = p.shape[0]
    tr = _tile(t, 256, SUBLANES)
    q_prescale = HEAD_DIM ** -0.5 * LOG2E

    def head_fwd(v, gain, cs, sn):
        v = v.astype(F32)
        rinv = lax.rsqrt(jnp.mean(v * v, axis=-1, keepdims=True) + NORM_EPS)
        tt = v * rinv * gain
        return tt * cs + _swap_pairs(tt) * sn

    def body(q_ref, k_ref, cos_ref, sin_ref, qg_ref, kg_ref, qo_ref, ko_ref):
        cs, sn = cos_ref[...], sin_ref[...]
        for hh in range(aw // HEAD_DIM):
            sl = slice(hh * HEAD_DIM, (hh + 1) * HEAD_DIM)
            qo_ref[:, sl] = (head_fwd(q_ref[:, sl], qg_ref[...], cs, sn) * q_prescale).astype(qo_ref.dtype)
        for hh in range(kw // HEAD_DIM):
            sl = slice(hh * HEAD_DIM, (hh + 1) * HEAD_DIM)
            ko_ref[:, sl] = head_fwd(k_ref[:, sl], kg_ref[...], cs, sn).astype(ko_ref.dtype)

    assert aw % kw == 0
    row = lambda i: (i, 0)
    return pl.pallas_call(
        body, name=name, grid=(t // tr,),
        in_specs=[pl.BlockSpec((tr, aw), row), pl.BlockSpec((tr, kw), lambda i: (i, aw // kw)),
                  pl.BlockSpec((tr, HEAD_DIM), row), pl.BlockSpec((tr, HEAD_DIM), row),
                  pl.BlockSpec((1, HEAD_DIM), lambda i: (0, 0)), pl.BlockSpec((1, HEAD_DIM), lambda i: (0, 0))],
        out_specs=[pl.BlockSpec((tr, aw), row), pl.BlockSpec((tr, kw), row)],
        out_shape=[jax.ShapeDtypeStruct((t, aw), BF16), jax.ShapeDtypeStruct((t, kw), BF16)],
        compiler_params=_cp(("parallel",)))(p, p, cosf, sinf, q_gain, k_gain)


def _qk_prep_bwd(name, dq_rot, dk_rot, p, cosf, sinf, q_gain, k_gain, aw, kw):
    t = p.shape[0]
    lq = dq_rot.shape[0]
    tr = _tile(math.gcd(t, lq), 256, SUBLANES)
    nq = lq // tr

    def head_bwd(dout, v, gain, cs, sn):
        v = v.astype(F32)
        rinv = lax.rsqrt(jnp.mean(v * v, axis=-1, keepdims=True) + NORM_EPS)
        vn = v * rinv
        dt = dout * cs - _swap_pairs(dout) * sn
        dvn = dt * gain
        dv = rinv * (dvn - vn * jnp.mean(dvn * vn, axis=-1, keepdims=True))
        return dv, jnp.sum(dt * vn, axis=0, keepdims=True)

    def body(dq_ref, dk_ref, q_ref, k_ref, cos_ref, sin_ref, qg_ref, kg_ref, dqo_ref, dko_ref, dqg_ref, dkg_ref):
        i = pl.program_id(0)
        cs, sn = cos_ref[...], sin_ref[...]

        @pl.when(i == 0)
        def _():
            dqg_ref[...] = jnp.zeros(dqg_ref.shape, F32)
            dkg_ref[...] = jnp.zeros(dkg_ref.shape, F32)

        has_q = i < nq
        for hh in range(aw // HEAD_DIM):
            sl = slice(hh * HEAD_DIM, (hh + 1) * HEAD_DIM)
            dout = jnp.where(has_q, dq_ref[:, sl], 0.0)
            dv, dg = head_bwd(dout, q_ref[:, sl], qg_ref[...], cs, sn)
            dqo_ref[:, sl] = dv.astype(dqo_ref.dtype)
            dqg_ref[...] += dg
        for hh in range(kw // HEAD_DIM):
            sl = slice(hh * HEAD_DIM, (hh + 1) * HEAD_DIM)
            dv, dg = head_bwd(dk_ref[:, sl], k_ref[:, sl], kg_ref[...], cs, sn)
            dko_ref[:, sl] = dv.astype(dko_ref.dtype)
            dkg_ref[...] += dg

    row = lambda i: (i, 0)
    one = lambda i: (0, 0)
    return pl.pallas_call(
        body, name=name, grid=(t // tr,),
        in_specs=[pl.BlockSpec((tr, aw), lambda i: (jnp.minimum(i, nq - 1), 0)), pl.BlockSpec((tr, kw), row),
                  pl.BlockSpec((tr, aw), row), pl.BlockSpec((tr, kw), lambda i: (i, aw // kw)),
                  pl.BlockSpec((tr, HEAD_DIM), row), pl.BlockSpec((tr, HEAD_DIM), row),
                  pl.BlockSpec((1, HEAD_DIM), one), pl.BlockSpec((1, HEAD_DIM), one)],
        out_specs=[pl.BlockSpec((tr, aw), row), pl.BlockSpec((tr, kw), row),
                   pl.BlockSpec((1, HEAD_DIM), one), pl.BlockSpec((1, HEAD_DIM), one)],
        out_shape=[jax.ShapeDtypeStruct((t, aw), BF16), jax.ShapeDtypeStruct((t, kw), BF16),
                   jax.ShapeDtypeStruct((1, HEAD_DIM), F32), jax.ShapeDtypeStruct((1, HEAD_DIM), F32)],
        compiler_params=_cp(("arbitrary",)))(dq_rot, dk_rot, p, p, cosf, sinf, q_gain, k_gain)


def _attn_tiles(lq, t):
    return _tile(lq, 512), _tile(t, 768)


def _attn_fwd(name, q, k, p, v_off, lq, groups):
    t = k.shape[0]
    hq = q.shape[1] // HEAD_DIM
    hkv = hq // groups
    gw = groups * HEAD_DIM
    tq, tk = _attn_tiles(lq, t)
    nkv = t // tk
    vb = v_off // HEAD_DIM

    def body(q_ref, k_ref, v_ref, o_ref, lse_ref, m_sc, l_sc, acc_sc, s_sc):
        j = pl.program_id(2)

        @pl.when(j == 0)
        def _():
            m_sc[...] = jnp.full(m_sc.shape, -jnp.inf, F32)
            l_sc[...] = jnp.zeros(l_sc.shape, F32)
            acc_sc[...] = jnp.zeros(acc_sc.shape, F32)
            s_sc[...] = jnp.zeros(s_sc.shape, F32)

        has_prev = j > 0
        kv, vv = k_ref[...], v_ref[...]
        for g in range(groups):
            sl = slice(g * HEAD_DIM, (g + 1) * HEAD_DIM)
            s = s_sc[g]
            s_sc[g] = _dot(q_ref[:, sl], kv, NT)
            m_prev = m_sc[g]
            m_new = jnp.where(has_prev, jnp.maximum(m_prev, jnp.max(s, axis=-1, keepdims=True)), jnp.inf)
            alpha = jnp.exp2(m_prev - m_new)
            pexp = jnp.exp2(s - m_new)
            part = pexp[:, 0:LANES]
            for cb in range(1, tk // LANES):
                part = part + pexp[:, cb * LANES:(cb + 1) * LANES]
            l_sc[g] = alpha * l_sc[g] + part
            acc_sc[:, sl] = alpha * acc_sc[:, sl] + _dot(pexp.astype(BF16), vv, NN)
            m_sc[g] = jnp.where(has_prev, m_new, -jnp.inf)

        @pl.when(j == nkv)
        def _():
            for g in range(groups):
                sl = slice(g * HEAD_DIM, (g + 1) * HEAD_DIM)
                l_row = jnp.sum(l_sc[g], axis=-1, keepdims=True)
                o_ref[:, sl] = (acc_sc[:, sl] * (1.0 / l_row)).astype(o_ref.dtype)
                lse_ref[g] = jnp.broadcast_to(m_sc[g] + jnp.log2(l_row), (tq, LANES))

    return pl.pallas_call(
        body, name=name, grid=(hkv, lq // tq, nkv + 1),
        in_specs=[pl.BlockSpec((tq, gw), lambda h, i, j: (i, h)),
                  pl.BlockSpec((tk, HEAD_DIM), lambda h, i, j: (jnp.minimum(j, nkv - 1), h)),
                  pl.BlockSpec((tk, HEAD_DIM), lambda h, i, j: (jnp.maximum(j - 1, 0), vb + h))],
        out_specs=[pl.BlockSpec((tq, gw), lambda h, i, j: (i, h)),
                   pl.BlockSpec((groups, tq, LANES), lambda h, i, j: (h, i, 0))],
        out_shape=[jax.ShapeDtypeStruct((lq, hq * HEAD_DIM), BF16), jax.ShapeDtypeStruct((hq, lq, LANES), F32)],
        scratch_shapes=[pltpu.VMEM((groups, tq, 1), F32), pltpu.VMEM((groups, tq, LANES), F32), pltpu.VMEM((tq, gw), F32),
                        pltpu.VMEM((groups, tq, tk), F32)],
        compiler_params=_cp(("parallel", "parallel", "arbitrary")))(q, k, p)


def _attn_delta(name, o, do):
    lq, aw = o.shape
    hq = aw // HEAD_DIM
    tr = _tile(lq, 512, SUBLANES)

    def body(o_ref, do_ref, d_ref):
        for h in range(hq):
            sl = slice(h * HEAD_DIM, (h + 1) * HEAD_DIM)
            dsum = jnp.sum(do_ref[:, sl].astype(F32) * o_ref[:, sl].astype(F32), axis=-1, keepdims=True)
            d_ref[h] = jnp.broadcast_to(dsum, (tr, LANES))

    spec = pl.BlockSpec((tr, aw), lambda i: (i, 0))
    return pl.pallas_call(
        body, name=name, grid=(lq // tr,), in_specs=[spec, spec],
        out_specs=pl.BlockSpec((hq, tr, LANES), lambda i: (0, i, 0)),
        out_shape=jax.ShapeDtypeStruct((hq, lq, LANES), F32), compiler_params=_cp(("parallel",)))(o, do)


def _attn_bwd(name, q, k, p, v_off, do, lse, delta, lq, groups):
    t = k.shape[0]
    hkv = k.shape[1] // HEAD_DIM
    gw = groups * HEAD_DIM
    tq, tk = _attn_tiles(lq, t)
    nq, nkv = lq // tq, t // tk
    scale = HEAD_DIM ** -0.5
    vb = v_off // HEAD_DIM

    def body(q_ref, k_ref, v_ref, do_ref, lse_ref, delta_ref, dq_ref, dk_ref, dv_ref, dq_sc, dk_sc, dv_sc):
        j, i = pl.program_id(1), pl.program_id(2)

        @pl.when(i == 0)
        def _():
            dk_sc[...] = jnp.zeros(dk_sc.shape, F32)
            dv_sc[...] = jnp.zeros(dv_sc.shape, F32)

        @pl.when(j == 0)
        def _():
            dq_sc[i] = jnp.zeros((tq, gw), F32)

        kv, vv = k_ref[...], v_ref[...]
        dk_part, dv_part = None, None
        for g in range(groups):
            sl = slice(g * HEAD_DIM, (g + 1) * HEAD_DIM)
            qv, dov = q_ref[:, sl], do_ref[:, sl]
            s = _dot(qv, kv, NT)
            pexp = jnp.exp2(s - lse_ref[g, :, 0:1])
            dv_g = _dot(pexp.astype(BF16), dov, TN)
            dp = _dot(dov, vv, NT)
            ds = (pexp * (dp - delta_ref[g, :, 0:1])).astype(BF16)
            dk_g = _dot(ds, qv, TN)
            dq_sc[i, :, sl] += _dot(ds, kv, NN)
            dk_part = dk_g if dk_part is None else dk_part + dk_g
            dv_part = dv_g if dv_part is None else dv_part + dv_g
        dk_sc[...] += dk_part
        dv_sc[...] += dv_part

        @pl.when(i == nq - 1)
        def _():
            dk_ref[...] = dk_sc[...] * LN2
            dv_ref[...] = dv_sc[...].astype(dv_ref.dtype)

        @pl.when(j == nkv - 1)
        def _():
            dq_ref[...] = dq_sc[i] * scale

    qspec = pl.BlockSpec((tq, gw), lambda kh, j, i: (i, kh))
    kspec = pl.BlockSpec((tk, HEAD_DIM), lambda kh, j, i: (j, kh))
    rowspec = pl.BlockSpec((groups, tq, LANES), lambda kh, j, i: (kh, i, 0))
    dqspec = pl.BlockSpec((tq, gw), lambda kh, j, i: (jnp.where(j == nkv - 1, i, 0), kh))
    return pl.pallas_call(
        body, name=name, grid=(hkv, nkv, nq),
        in_specs=[qspec, kspec, pl.BlockSpec((tk, HEAD_DIM), lambda kh, j, i: (j, vb + kh)), qspec, rowspec, rowspec],
        out_specs=[dqspec, kspec, kspec],
        out_shape=[jax.ShapeDtypeStruct((lq, hkv * gw), F32), jax.ShapeDtypeStruct((t, hkv * HEAD_DIM), F32),
                   jax.ShapeDtypeStruct((t, hkv * HEAD_DIM), BF16)],
        scratch_shapes=[pltpu.VMEM((nq, tq, gw), F32), pltpu.VMEM((tk, HEAD_DIM), F32), pltpu.VMEM((tk, HEAD_DIM), F32)],
        compiler_params=_cp(("parallel", "arbitrary", "arbitrary")))(q, k, p, do, lse, delta)


def _ret_tables(log_gamma, direction):
    c = RET_CHUNK
    idx = jnp.arange(c, dtype=F32)
    diff = idx[:, None] - idx[None, :]
    if direction == 1:
        diff = -diff
    keep = diff >= 0
    lg = log_gamma.astype(F32)
    mask = jnp.where(keep[None], jnp.exp(jnp.where(keep, diff, 0.0)[None] * lg[:, None, None]), 0.0)
    q_exp = idx + 1.0 if direction == 0 else c - idx
    k_exp = c - 1.0 - idx if direction == 0 else idx
    sign = 1.0 if direction == 0 else -1.0
    lane = lambda v: jnp.broadcast_to(v[..., None], v.shape + (LANES,))
    qdec = lane(jnp.exp(q_exp[None, :] * lg[:, None]))
    kdec = lane(jnp.exp(k_exp[None, :] * lg[:, None]))
    cdec = jnp.broadcast_to(jnp.exp(c * lg)[:, None, None], (lg.shape[0], SUBLANES, LANES))
    weights = lane(jnp.stack([sign * idx, q_exp, -sign * idx, k_exp], axis=0))
    return mask, qdec, kdec, cdec, weights


def _ret_chunk_of(direction, step, nx, nc):
    if direction == 0:
        return jnp.where(step < nc, nx + step, step - nc)
    return jnp.where(step < nc, nx + nc - 1 - step, nx + nc - 1 - step)


def _ret_fwd(name, p, offs, tables, direction, lx, prev):
    q_off, k_off, v_off, rw = offs
    t = p.shape[0]
    c = RET_CHUNK
    n_steps, nx = t // c, lx // c
    nc = n_steps - nx
    bw = math.gcd(math.gcd(q_off, k_off), math.gcd(v_off, rw))
    bw = _tile(bw, 512)
    hpb = bw // HEAD_DIM
    heads = rw // HEAD_DIM
    k_scale = HEAD_DIM ** -0.5
    mask, qdec, kdec, cdec, _ = tables
    rc = lambda n: _ret_chunk_of(direction, n, nx, nc)

    def body(*refs):
        if prev is None:
            q_ref, k_ref, v_ref, m_ref, qd_ref, kd_ref, cd_ref, y_ref, st_ref, s_sc = refs
        else:
            q_ref, k_ref, v_ref, m_ref, qd_ref, kd_ref, cd_ref, prev_ref, y_ref, st_ref, s_sc = refs
        n = pl.program_id(1)

        @pl.when(n == 0)
        def _():
            s_sc[...] = jnp.zeros(s_sc.shape, F32)

        for hh in range(hpb):
            sl = slice(hh * HEAD_DIM, (hh + 1) * HEAD_DIM)
            qv = q_ref[:, sl]
            kf = k_ref[:, sl].astype(F32) * k_scale
            kv = kf.astype(BF16)
            vv = v_ref[:, sl]
            state = s_sc[hh]
            st_ref[hh] = state
            a = _dot(qv, kv, NT) * m_ref[hh]
            y = _dot(a.astype(BF16), vv, NN) + _dot(qv, state.astype(BF16), NN) * qd_ref[hh]
            s_sc[hh] = state * cd_ref[hh, 0:1, :] + _dot((kf * kd_ref[hh]).astype(BF16), vv, TN)
            if prev is not None:
                y = y + prev_ref[:, sl]
            y_ref[:, sl] = y

    col = lambda off: (lambda g, n: (rc(n), off // bw + g))
    tab3 = lambda g, n: (g, 0, 0)
    in_specs = [pl.BlockSpec((c, bw), col(q_off)), pl.BlockSpec((c, bw), col(k_off)), pl.BlockSpec((c, bw), col(v_off)),
                pl.BlockSpec((hpb, c, c), tab3), pl.BlockSpec((hpb, c, LANES), tab3), pl.BlockSpec((hpb, c, LANES), tab3),
                pl.BlockSpec((hpb, SUBLANES, LANES), tab3)]
    args = [p, p, p, mask, qdec, kdec, cdec]
    aliases = {}
    yspec = pl.BlockSpec((c, bw), lambda g, n: (rc(n), g))
    if prev is not None:
        in_specs.append(yspec)
        args.append(prev)
        aliases = {len(args) - 1: 0}
    return pl.pallas_call(
        body, name=name, grid=(heads // hpb, n_steps), in_specs=in_specs,
        out_specs=[yspec, pl.BlockSpec((None, hpb, HEAD_DIM, HEAD_DIM), lambda g, n: (n, g, 0, 0))],
        out_shape=[jax.ShapeDtypeStruct((t, rw), F32), jax.ShapeDtypeStruct((n_steps, heads, HEAD_DIM, HEAD_DIM), F32)],
        scratch_shapes=[pltpu.VMEM((hpb, HEAD_DIM, HEAD_DIM), F32)], input_output_aliases=aliases,
        compiler_params=_cp(("parallel", "arbitrary")))(*args)


def _ret_bwd(name, p, offs, tables, states, dy, direction, lx, prev):
    q_off, k_off, v_off, rw = offs
    t = p.shape[0]
    c = RET_CHUNK
    n_steps, nx = t // c, lx // c
    nc = n_steps - nx
    bw = math.gcd(math.gcd(q_off, k_off), math.gcd(v_off, rw))
    bw = _tile(bw, 512)
    hpb = bw // HEAD_DIM
    heads = rw // HEAD_DIM
    k_scale = HEAD_DIM ** -0.5
    mask, qdec, kdec, cdec, weights = tables
    step_of = lambda n: n_steps - 1 - n
    rc = lambda n: _ret_chunk_of(direction, step_of(n), nx, nc)

    def body(*refs):
        if prev is None:
            (q_ref, k_ref, v_ref, dy_ref, st_ref, m_ref, qd_ref, kd_ref, cd_ref, w_ref,
             dq_ref, dk_ref, dv_ref, dl_ref, ds_sc, lam_sc) = refs
        else:
            (q_ref, k_ref, v_ref, dy_ref, st_ref, m_ref, qd_ref, kd_ref, cd_ref, w_ref, pq_ref, pk_ref, pv_ref,
             dq_ref, dk_ref, dv_ref, dl_ref, ds_sc, lam_sc) = refs
        n = pl.program_id(1)

        @pl.when(n == 0)
        def _():
            ds_sc[...] = jnp.zeros(ds_sc.shape, F32)
            lam_sc[...] = jnp.zeros(lam_sc.shape, F32)

        is_x = rc(n) < nx
        for hh in range(hpb):
            sl = slice(hh * HEAD_DIM, (hh + 1) * HEAD_DIM)
            qv = q_ref[:, sl]
            qf = qv.astype(F32)
            kf = k_ref[:, sl].astype(F32) * k_scale
            kv = kf.astype(BF16)
            vv = v_ref[:, sl]
            dyv = jnp.where(is_x, dy_ref[:, sl], 0.0).astype(BF16)
            state = st_ref[hh]
            dstate = ds_sc[hh]
            dstate_b = dstate.astype(BF16)
            msk, qd, kd = m_ref[hh], qd_ref[hh], kd_ref[hh]
            cd = cd_ref[hh, 0:1, :]
            a = _dot(qv, kv, NT) * msk
            da = (_dot(dyv, vv, NT) * msk).astype(BF16)
            dq_intra = _dot(da, kv, NN)
            dk_intra = _dot(da, qv, TN)
            dq_inter = _dot(dyv, state.astype(BF16), NT) * qd
            dk_inter = _dot(vv, dstate_b, NT) * kd
            dv = _dot(a.astype(BF16), dyv, TN) + _dot((kf * kd).astype(BF16), dstate_b, NN)
            lam_sc[hh] += (qf * (w_ref[0] * dq_intra + w_ref[1] * dq_inter)
                           + kf * (w_ref[2] * dk_intra + w_ref[3] * dk_inter)
                           + (c * cd) * state * dstate)
            ds_sc[hh] = _dot((qf * qd).astype(BF16), dyv, TN) + cd * dstate
            dq = dq_intra + dq_inter
            dk = (dk_intra + dk_inter) * k_scale
            if prev is not None:
                dq = dq + pq_ref[:, sl]
                dk = dk + pk_ref[:, sl]
                dv = dv + pv_ref[:, sl]
            dq_ref[:, sl] = dq
            dk_ref[:, sl] = dk
            dv_ref[:, sl] = dv

        @pl.when(n == n_steps - 1)
        def _():
            for hh in range(hpb):
                dl_ref[hh] = jnp.broadcast_to(jnp.sum(lam_sc[hh]), (SUBLANES, LANES))

    col = lambda off: (lambda g, n: (rc(n), off // bw + g))
    tab3 = lambda g, n: (g, 0, 0)
    ospec = pl.BlockSpec((c, bw), lambda g, n: (rc(n), g))
    in_specs = [pl.BlockSpec((c, bw), col(q_off)), pl.BlockSpec((c, bw), col(k_off)), pl.BlockSpec((c, bw), col(v_off)),
                pl.BlockSpec((c, bw), lambda g, n: (jnp.minimum(rc(n), nx - 1), g)),
                pl.BlockSpec((None, hpb, HEAD_DIM, HEAD_DIM), lambda g, n: (step_of(n), g, 0, 0)),
                pl.BlockSpec((hpb, c, c), tab3), pl.BlockSpec((hpb, c, LANES), tab3), pl.BlockSpec((hpb, c, LANES), tab3),
                pl.BlockSpec((hpb, SUBLANES, LANES), tab3), pl.BlockSpec((4, c, LANES), lambda g, n: (0, 0, 0))]
    args = [p, p, p, dy, states, mask, qdec, kdec, cdec, weights]
    aliases = {}
    if prev is not None:
        for k_out, arr in enumerate(prev):
            in_specs.append(ospec)
            args.append(arr)
            aliases[len(args) - 1] = k_out
    big = jax.ShapeDtypeStruct((t, rw), F32)
    return pl.pallas_call(
        body, name=name, grid=(heads // hpb, n_steps), in_specs=in_specs,
        out_specs=[ospec, ospec, ospec, pl.BlockSpec((hpb, SUBLANES, LANES), tab3)],
        out_shape=[big, big, big, jax.ShapeDtypeStruct((heads, SUBLANES, LANES), F32)],
        scratch_shapes=[pltpu.VMEM((hpb, HEAD_DIM, HEAD_DIM), F32), pltpu.VMEM((hpb, HEAD_DIM, HEAD_DIM), F32)],
        input_output_aliases=aliases, compiler_params=_cp(("parallel", "arbitrary")))(*args)


def _ret_out(name, y, p, g_off, lx):
    rw = y.shape[1]
    bw = _tile(math.gcd(g_off, rw), 512)
    tr = _tile(lx, 512, SUBLANES)

    def body(y_ref, g_ref, o_ref):
        for hh in range(bw // HEAD_DIM):
            sl = slice(hh * HEAD_DIM, (hh + 1) * HEAD_DIM)
            yv = y_ref[:, sl]
            gv = g_ref[:, sl].astype(F32)
            rinv = lax.rsqrt(jnp.mean(yv * yv, axis=-1, keepdims=True) + NORM_EPS)
            o_ref[:, sl] = (gv * _sigmoid(gv) * yv * rinv).astype(o_ref.dtype)

    spec = pl.BlockSpec((tr, bw), lambda i, g: (i, g))
    return pl.pallas_call(
        body, name=name, grid=(lx // tr, rw // bw),
        in_specs=[spec, pl.BlockSpec((tr, bw), lambda i, g: (i, g_off // bw + g))],
        out_specs=spec, out_shape=jax.ShapeDtypeStruct((lx, rw), BF16),
        compiler_params=_cp(("parallel", "parallel")))(y, p)


def _ret_out_bwd(name, dyr, y, p, g_off, lx):
    rw = y.shape[1]
    bw = _tile(math.gcd(g_off, rw), 512)
    tr = _tile(lx, 512, SUBLANES)

    def body(d_ref, y_ref, g_ref, dy_ref, dg_ref):
        for hh in range(bw // HEAD_DIM):
            sl = slice(hh * HEAD_DIM, (hh + 1) * HEAD_DIM)
            yv = y_ref[:, sl]
            gv = g_ref[:, sl].astype(F32)
            dv = d_ref[:, sl]
            rinv = lax.rsqrt(jnp.mean(yv * yv, axis=-1, keepdims=True) + NORM_EPS)
            yn = yv * rinv
            sg = _sigmoid(gv)
            dg_ref[:, sl] = (dv * yn * sg * (1.0 + gv * (1.0 - sg))).astype(dg_ref.dtype)
            dyn = dv * gv * sg
            dy_ref[:, sl] = rinv * (dyn - yn * jnp.mean(dyn * yn, axis=-1, keepdims=True))

    spec = pl.BlockSpec((tr, bw), lambda i, g: (i, g))
    return pl.pallas_call(
        body, name=name, grid=(lx // tr, rw // bw),
        in_specs=[spec, spec, pl.BlockSpec((tr, bw), lambda i, g: (i, g_off // bw + g))],
        out_specs=[spec, spec],
        out_shape=[jax.ShapeDtypeStruct((lx, rw), F32), jax.ShapeDtypeStruct((lx, rw), BF16)],
        compiler_params=_cp(("parallel", "parallel")))(dyr, y, p)


def _merge(name, pa, pr, p, ga_off, gb_off):
    r, d = pa.shape
    cw = _tile(math.gcd(math.gcd(ga_off, gb_off), d), 1024)
    tr = _tile(r, 512, SUBLANES)

    def body(pa_ref, pr_ref, ga_ref, gb_ref, o_ref):
        o_ref[...] = (_sigmoid(ga_ref[...].astype(F32)) * pa_ref[...].astype(F32)
                      + _sigmoid(gb_ref[...].astype(F32)) * pr_ref[...].astype(F32)).astype(o_ref.dtype)

    spec = pl.BlockSpec((tr, cw), lambda i, j: (i, j))
    return pl.pallas_call(
        body, name=name, grid=(r // tr, d // cw),
        in_specs=[spec, spec, pl.BlockSpec((tr, cw), lambda i, j: (i, ga_off // cw + j)),
                  pl.BlockSpec((tr, cw), lambda i, j: (i, gb_off // cw + j))],
        out_specs=spec, out_shape=jax.ShapeDtypeStruct((r, d), BF16),
        compiler_params=_cp(("parallel", "parallel")))(pa, pr, p, p)


def _ada_fwd(name, cond, w):
    rows, d = cond.shape
    n = w.shape[1]
    tn = _tile(n, 768)

    def body(c_ref, w_ref, o_ref, s_ref):
        cv = c_ref[...]
        sv = cv * _sigmoid(cv)
        s_ref[...] = sv
        o_ref[...] = _dot(sv.astype(BF16), w_ref[...].astype(BF16), NN)

    return pl.pallas_call(
        body, name=name, grid=(n // tn,),
        in_specs=[pl.BlockSpec((rows, d), lambda j: (0, 0)), pl.BlockSpec((d, tn), lambda j: (0, j))],
        out_specs=[pl.BlockSpec((rows, tn), lambda j: (0, j)), pl.BlockSpec((rows, d), lambda j: (0, 0))],
        out_shape=[jax.ShapeDtypeStruct((rows, n), F32), jax.ShapeDtypeStruct((rows, d), F32)],
        compiler_params=_cp(("arbitrary",)))(cond, w)


def _ada_bwd(name, s_cond, dmod, w):
    rows, d = s_cond.shape
    n = w.shape[1]
    tn = _tile(n, 768)

    def body(s_ref, dm_ref, w_ref, gw_ref, ds_ref):
        j = pl.program_id(0)

        @pl.when(j == 0)
        def _():
            ds_ref[...] = jnp.zeros(ds_ref.shape, F32)

        dmv = dm_ref[...].astype(BF16)
        gw_ref[...] = _dot(s_ref[...].astype(BF16), dmv, TN)
        ds_ref[...] += _dot(dmv, w_ref[...].astype(BF16), NT)

    return pl.pallas_call(
        body, name=name, grid=(n // tn,),
        in_specs=[pl.BlockSpec((rows, d), lambda j: (0, 0)), pl.BlockSpec((rows, tn), lambda j: (0, j)),
                  pl.BlockSpec((d, tn), lambda j: (0, j))],
        out_specs=[pl.BlockSpec((d, tn), lambda j: (0, j)), pl.BlockSpec((rows, d), lambda j: (0, 0))],
        out_shape=[jax.ShapeDtypeStruct((d, n), F32), jax.ShapeDtypeStruct((rows, d), F32)],
        compiler_params=_cp(("arbitrary",)))(s_cond, dmod, w)


def _sum_slots(name, a):
    s, r, c = a.shape

    def body(a_ref, o_ref):
        acc = a_ref[0]
        for k in range(1, s):
            acc = acc + a_ref[k]
        o_ref[...] = acc

    return pl.pallas_call(
        body, name=name, grid=(1,), in_specs=[pl.BlockSpec((s, r, c), lambda i: (0, 0, 0))],
        out_specs=pl.BlockSpec((r, c), lambda i: (0, 0)), out_shape=jax.ShapeDtypeStruct((r, c), F32),
        compiler_params=_cp(("arbitrary",)))(a)


def _cctx_grad(name, parts, c_ctx):
    s, r, d = parts.shape

    def body(p_ref, c_ref, o_ref):
        acc = p_ref[0]
        for k in range(1, s):
            acc = acc + p_ref[k]
        cv = c_ref[...]
        sg = _sigmoid(cv)
        o_ref[...] = acc[0:1, :] * sg * (1.0 + cv * (1.0 - sg))

    return pl.pallas_call(
        body, name=name, grid=(1,),
        in_specs=[pl.BlockSpec((s, r, d), lambda i: (0, 0, 0)), pl.BlockSpec((1, d), lambda i: (0, 0))],
        out_specs=pl.BlockSpec((1, d), lambda i: (0, 0)), out_shape=jax.ShapeDtypeStruct((1, d), F32),
        compiler_params=_cp(("arbitrary",)))(parts, c_ctx)


def _adamw(name, slots, w, m, v):
    s, r, c = slots.shape
    tr = _tile(r, 256 if c > 1024 else 512, SUBLANES)
    c1 = 1.0 - ADAM_B1 ** ADAM_STEP
    c2 = 1.0 - ADAM_B2 ** ADAM_STEP

    def body(s_ref, w_ref, m_ref, v_ref, g_ref, d_ref, mo_ref, vo_ref):
        g = s_ref[0].astype(F32)
        for k in range(1, s):
            g = g + s_ref[k].astype(F32)
        mn = ADAM_B1 * m_ref[...] + (1.0 - ADAM_B1) * g
        vn = ADAM_B2 * v_ref[...] + (1.0 - ADAM_B2) * (g * g)
        m_hat = mn / c1
        v_hat = vn / c2
        g_ref[...] = g
        mo_ref[...] = mn
        vo_ref[...] = vn
        d_ref[...] = -ADAM_LR * (m_hat / (jnp.sqrt(v_hat) + ADAM_EPS) + ADAM_WD * w_ref[...])

    spec = pl.BlockSpec((tr, c), lambda i: (i, 0))
    shp = jax.ShapeDtypeStruct((r, c), F32)
    return pl.pallas_call(
        body, name=name, grid=(r // tr,),
        in_specs=[pl.BlockSpec((s, tr, c), lambda i: (0, i, 0)), spec, spec, spec],
        out_specs=[spec] * 4, out_shape=[shp] * 4, compiler_params=_cp(("parallel",)))(slots, w, m, v)


def _coords():
    return lax.axis_index("x"), lax.axis_index("y"), lax.axis_index("c")


def _flip(coord, bit):
    return 1 - coord if bit else coord


def _all_gather_small(name, blk):
    r, ccols = blk.shape

    def body(x_ref, out_ref, send_sems, recv_sems, local_sem):
        x, y, c = _coords()
        me, sibling = (x, y, c), (x, y, 1 - c)
        chips = [(1 - x, y), (x, 1 - y), (1 - x, 1 - y)]

        def slot(px, py, pc):
            return out_ref.at[4 * px + 2 * py + pc]

        def copy(k, block, to, src=None):
            return pltpu.make_async_remote_copy(
                src_ref=slot(*block) if src is None else src, dst_ref=slot(*block),
                send_sem=send_sems.at[k], recv_sem=recv_sems.at[k], device_id=to, device_id_type=MESH)

        mine = pltpu.make_async_copy(x_ref, slot(*me), local_sem)
        mine.start()
        first = [copy(0, me, sibling, src=x_ref)]
        first += [copy(1 + j, me, (*chip, c), src=x_ref) for j, chip in enumerate(chips)]
        for cp in first:
            cp.start()
        passed = [copy(4 + j, (*chip, c), sibling) for j, chip in enumerate(chips)]
        for j, chip in enumerate(chips):
            copy(1 + j, (*chip, c), me).wait_recv()
            passed[j].start()
        copy(0, sibling, me).wait_recv()
        for j, chip in enumerate(chips):
            copy(4 + j, (*chip, 1 - c), me).wait_recv()
        for cp in first + passed:
            cp.wait_send()
        mine.wait()

    return pl.pallas_call(
        body, name=name, out_shape=jax.ShapeDtypeStruct((N_DEV, r, ccols), blk.dtype),
        in_specs=[pl.BlockSpec(memory_space=pltpu.VMEM)], out_specs=pl.BlockSpec(memory_space=pltpu.VMEM),
        scratch_shapes=[pltpu.SemaphoreType.DMA((7,)), pltpu.SemaphoreType.DMA((7,)), pltpu.SemaphoreType.DMA],
    )(blk)


def _comm_semaphores(n):
    return [pltpu.SemaphoreType.DMA((7 * n,)), pltpu.SemaphoreType.DMA((7 * n,)), pltpu.SemaphoreType.DMA((n,))]


def _gather_plan(ins, outs, send_sems, recv_sems, local_sems):
    n = len(ins)
    x, y, c = _coords()
    me, sibling = (x, y, c), (x, y, 1 - c)
    chips = [(1 - x, y), (x, 1 - y), (1 - x, 1 - y)]

    def slot(a, px, py, pc):
        return outs[a].at[4 * px + 2 * py + pc]

    def copy(a, k, block, to, src=None):
        return pltpu.make_async_remote_copy(
            src_ref=slot(a, *block) if src is None else src, dst_ref=slot(a, *block),
            send_sem=send_sems.at[7 * a + k], recv_sem=recv_sems.at[7 * a + k], device_id=to, device_id_type=MESH)

    def local(a):
        return pltpu.make_async_copy(ins[a], slot(a, *me), local_sems.at[a])

    def first(a):
        return [copy(a, 0, me, sibling, src=ins[a])] + [copy(a, 1 + j, me, (*chip, c), src=ins[a])
                                                        for j, chip in enumerate(chips)]

    def passed(a, j):
        return copy(a, 4 + j, (*chips[j], c), sibling)

    def start():
        for a in range(n):
            local(a).start()
            for cp in first(a):
                cp.start()

    def finish():
        for a in range(n):
            for j, chip in enumerate(chips):
                copy(a, 1 + j, (*chip, c), me).wait_recv()
                passed(a, j).start()
        for a in range(n):
            copy(a, 0, sibling, me).wait_recv()
            for j, chip in enumerate(chips):
                copy(a, 4 + j, (*chip, 1 - c), me).wait_recv()
        for a in range(n):
            for cp in first(a) + [passed(a, j) for j in range(3)]:
                cp.wait_send()
            local(a).wait()

    return start, finish


def _exchange_plan(ins, outs, send_sems, recv_sems, local_sems):
    n = len(ins)
    x, y, c = _coords()
    my_idx = 4 * x + 2 * y + c

    def local(a):
        return pltpu.make_async_copy(ins[a].at[my_idx], outs[a].at[my_idx], local_sems.at[a])

    def pair(a, rel):
        px, py, pc = _flip(x, rel & 4), _flip(y, rel & 2), _flip(c, rel & 1)
        peer_idx = 4 * px + 2 * py + pc
        sems = dict(send_sem=send_sems.at[7 * a + rel - 1], recv_sem=recv_sems.at[7 * a + rel - 1],
                    device_id=(px, py, pc), device_id_type=MESH)
        send = pltpu.make_async_remote_copy(src_ref=ins[a].at[peer_idx], dst_ref=outs[a].at[my_idx], **sems)
        recv = pltpu.make_async_remote_copy(src_ref=ins[a].at[my_idx], dst_ref=outs[a].at[peer_idx], **sems)
        return send, recv

    def start():
        for a in range(n):
            local(a).start()
            for rel in range(1, N_DEV):
                pair(a, rel)[0].start()

    def finish():
        for a in range(n):
            for rel in range(1, N_DEV):
                send, recv = pair(a, rel)
                recv.wait_recv()
                send.wait_send()
            local(a).wait()

    return start, finish


def _comm_plan(kind, ins, outs, send_sems, recv_sems, local_sems):
    plan = {"gather": _gather_plan, "exchange": _exchange_plan}[kind]
    return plan(ins, outs, send_sems, recv_sems, local_sems)


def _all_gather_big(name, shards):
    n = len(shards)

    def body(*refs):
        start, finish = _gather_plan(refs[:n], refs[n:2 * n], *refs[2 * n:])
        start()
        finish()

    any_spec = pl.BlockSpec(memory_space=pl.ANY)
    return pl.pallas_call(
        body, name=name, out_shape=[jax.ShapeDtypeStruct((N_DEV,) + s.shape, s.dtype) for s in shards],
        in_specs=[any_spec] * n, out_specs=[any_spec] * n, scratch_shapes=_comm_semaphores(n))(*shards)


def _rope_tables(lx, lc):
    rows = lx // GRID_W
    row = jnp.repeat(jnp.arange(rows, dtype=F32), GRID_W)
    col = jnp.tile(jnp.arange(GRID_W, dtype=F32), rows)
    half = HEAD_DIM // 2
    inv_freq = ROPE_THETA ** (-jnp.arange(0, half, 2, dtype=F32) / half)
    ang = jnp.concatenate([row[:, None] * inv_freq, col[:, None] * inv_freq], axis=-1)
    cos, sin = jnp.cos(ang), jnp.sin(ang)
    cosf = jnp.repeat(cos, 2, axis=-1)
    sinf = jnp.stack([-sin, sin], axis=-1).reshape(lx, HEAD_DIM)
    cosf = jnp.concatenate([cosf, jnp.ones((lc, HEAD_DIM), F32)], axis=0)
    sinf = jnp.concatenate([sinf, jnp.zeros((lc, HEAD_DIM), F32)], axis=0)
    return cosf, sinf


def _cols_full(g):
    return jnp.transpose(g, (1, 0, 2)).reshape(g.shape[1], N_DEV * g.shape[2])


def _cols_split(w):
    k, n = w.shape
    return jnp.transpose(w.reshape(k, N_DEV, n // N_DEV), (1, 0, 2))


def _pad_rows(a, rows):
    return jnp.pad(a, ((0, rows - a.shape[0]), (0, 0)))


def kernel(x, c, ctx, c_ctx, w_ada, b_ada, ffn1_w_in, ffn1_w_out, mix_w_in, attn_q_gain, attn_k_gain, ret_decay_logit, w_proj_attn, w_proj_ret, mix_w_out, ffn2_w_in, ffn2_w_out, final_norm, loss_target, m_c_ctx, m_w_ada, m_b_ada, m_ffn1_w_in, m_ffn1_w_out, m_mix_w_in, m_attn_q_gain, m_attn_k_gain, m_ret_decay_logit, m_w_proj_attn, m_w_proj_ret, m_mix_w_out, m_ffn2_w_in, m_ffn2_w_out, m_final_norm, v_c_ctx, v_w_ada, v_b_ada, v_ffn1_w_in, v_ffn1_w_out, v_mix_w_in, v_attn_q_gain, v_attn_k_gain, v_ret_decay_logit, v_w_proj_attn, v_w_proj_ret, v_mix_w_out, v_ffn2_w_in, v_ffn2_w_out, v_final_norm):
    lx, d = x.shape[1], x.shape[2]
    lc = ctx.shape[1]
    t = lx + lc
    aw, rw = w_proj_attn.shape[1], w_proj_ret.shape[1]
    pw = mix_w_in.shape[2] * N_DEV
    kw = (pw - aw - 4 * rw - 2 * d) // 2
    groups = aw // kw
    heads_r = rw // HEAD_DIM
    ka_off, va_off = aw, aw + kw
    qr_off = aw + 2 * kw
    kr_off, vr_off, gr_off = qr_off + rw, qr_off + 2 * rw, qr_off + 3 * rw
    ga_off, gb_off = qr_off + 4 * rw, qr_off + 4 * rw + d
    xi, yi, ci = _coords()
    me = 4 * xi + 2 * yi + ci

    col_names = ["ffn1_w_in", "mix_w_in", "w_proj_attn", "w_proj_ret", "ffn2_w_in"]
    row_names = ["ffn1_w_out", "mix_w_out", "ffn2_w_out"]
    shard = dict(ffn1_w_in=ffn1_w_in[0], mix_w_in=mix_w_in[0], w_proj_attn=w_proj_attn[0], w_proj_ret=w_proj_ret[0],
                 ffn2_w_in=ffn2_w_in[0], ffn1_w_out=ffn1_w_out[0], mix_w_out=mix_w_out[0], ffn2_w_out=ffn2_w_out[0])
    names = col_names + row_names
    bf_shard = {k: shard[k].astype(BF16) for k in names}
    full, landed = {}, {}

    def gather_of(keys):
        return "gather", [bf_shard[k] for k in keys]

    def keep_full(keys, gathered):
        for k, g in zip(keys, gathered):
            full[k] = _cols_full(g) if k in col_names else g.reshape(N_DEV * g.shape[1], g.shape[2])

    def exchange_of(grads):
        return "exchange", [_cols_split(g) if k in col_names else g.reshape(N_DEV, g.shape[0] // N_DEV, g.shape[1])
                            for k, g in grads.items()]

    def keep_landed(grads, got):
        landed.update(zip(grads.keys(), got))

    keep_full(["ffn1_w_in"], _all_gather_big("gather_first", [bf_shard["ffn1_w_in"]]))

    c_all = _all_gather_small("gather_cond", _pad_rows(c, SUBLANES))[:, 0, :]
    cond = _pad_rows(jnp.concatenate([c_all, c_ctx[None, :]], axis=0), 2 * SUBLANES)
    ada_part, s_cond = _ada_fwd("ada_fwd", cond, w_ada[0])
    ada_all = _all_gather_small("gather_ada", ada_part)
    mod_all = jnp.transpose(ada_all, (1, 0, 2)).reshape(2 * SUBLANES, N_MOD * d) + b_ada
    mod_x = lax.dynamic_index_in_dim(mod_all, me, axis=0, keepdims=False).reshape(N_MOD, d)
    mod_c = mod_all[N_DEV].reshape(N_MOD, d)
    mods = jnp.stack([mod_c, mod_x], axis=0)[:, :, None, :]
    sh1, sc1, g1, sh2, sc2, g2, sh3, sc3, g3 = [mods[:, k] for k in range(N_MOD)]

    h0 = jnp.concatenate([x[0], ctx[0]], axis=0)
    u1 = _rmsmod("ffn1_norm", h0, sc1, sh1, lx)
    keys = ["ffn1_w_out", "mix_w_in"]
    (z1, s1), got = _ffn_in("ffn1_in", u1, full["ffn1_w_in"], comm=gather_of(keys))
    keep_full(keys, got)
    keys = ["w_proj_attn", "w_proj_ret", "mix_w_out", "ffn2_w_out"]
    (h1, f1), got = _mm_residual("ffn1_out", s1, full["ffn1_w_out"], h0, g1, 0.5, lx, comm=gather_of(keys))
    keep_full(keys, got)

    u2 = _rmsmod("mix_norm", h1, sc2, sh2, lx)
    keys = ["ffn2_w_in"]
    p, got = _mm_nn("mix_in", u2, full["mix_w_in"], BF16, tm_pref=M_TILE_BIG, tn_pref=512, comm=gather_of(keys))
    keep_full(keys, got)
    cosf, sinf = _rope_tables(lx, lc)
    q_rot, k_rot = _qk_prep("qk_prep", p, cosf, sinf, attn_q_gain, attn_k_gain, aw, kw)
    ya, lse = _attn_fwd("attn_fwd", q_rot, k_rot, p, va_off, lx, groups)

    decay = ret_decay_logit[0].astype(F32)
    log_gamma = jax.nn.log_sigmoid(decay)
    r_offs = (qr_off, kr_off, vr_off, rw)
    tab = [_ret_tables(log_gamma[k], k) for k in range(2)]
    y_f, st_f = _ret_fwd("ret_fwd_a", p, r_offs, tab[0], 0, lx, None)
    y_r, st_b = _ret_fwd("ret_fwd_b", p, r_offs, tab[1], 1, lx, y_f)
    yr = _ret_out("ret_out", y_r, p, gr_off, lx)

    pa = _mm_nn("proj_attn", ya, full["w_proj_attn"], BF16)
    pr = _mm_nn("proj_ret", yr, full["w_proj_ret"], BF16)
    mg = _merge("merge", pa, pr, p, ga_off, gb_off)
    h2, o2 = _mm_residual("mix_out", mg, full["mix_w_out"], h1, g2, 1.0, lx)

    u3 = _rmsmod("ffn2_norm", h2, sc3, sh3, lx)
    z3, s3 = _ffn_in("ffn2_in", u3, full["ffn2_w_in"])
    h3, f3 = _mm_residual("ffn2_out", s3, full["ffn2_w_out"], h2, g3, 0.5, lx)
    dh3, d_fn, loss_tile = _final_loss("final_loss", h3, final_norm[None, :], loss_target[0])

    df3, dg3 = _gate_bwd("ffn2_gate_bwd", dh3, f3, g3, 0.5, lx)
    dz3 = _ffn_out_bwd("ffn2_out_bwd", df3, full["ffn2_w_out"], z3)
    grads = {"ffn2_w_out": _mm_tn("ffn2_out_dw", s3, df3, BF16, tm_pref=M_TILE_BIG)}
    g_ffn2_w_in, got = _ffn_in_bwd_w("ffn2_in_dw", u3, dz3, comm=exchange_of(grads))
    keep_landed(grads, got)
    grads = {"ffn2_w_in": g_ffn2_w_in}
    du3, got = _ffn_in_bwd_x("ffn2_in_bwd", dz3, full["ffn2_w_in"], comm=exchange_of(grads))
    keep_landed(grads, got)
    dh2, dsc3, dsh3 = _rmsmod_bwd("ffn2_norm_bwd", du3, h2, sc3, dh3, lx, lx)

    do2, dg2 = _gate_bwd("mix_gate_bwd", dh2, o2, g2, 1.0, lx)
    dpa, dpr, dga, dgb = _merge_bwd("merge_bwd", do2, full["mix_w_out"], pa, pr, p, ga_off, gb_off)
    dya = _mm_nt("proj_attn_bwd", dpa, full["w_proj_attn"], BF16)
    dyr = _mm_nt("proj_ret_bwd", dpr, full["w_proj_ret"], F32)
    mix_grads = {"mix_w_out": _mm_tn("mix_out_dw", mg, do2, BF16),
                 "w_proj_attn": _mm_tn("proj_attn_dw", ya, dpa, BF16),
                 "w_proj_ret": _mm_tn("proj_ret_dw", yr, dpr, BF16)}

    dy_ret, dgr = _ret_out_bwd("ret_out_bwd", dyr, y_r, p, gr_off, lx)
    dqr, dkr, dvr, dl_f = _ret_bwd("ret_bwd_a", p, r_offs, tab[0], st_f, dy_ret, 0, lx, None)
    dqr, dkr, dvr, dl_b = _ret_bwd("ret_bwd_b", p, r_offs, tab[1], st_b, dy_ret, 1, lx, (dqr, dkr, dvr))
    d_lam = jnp.stack([dl_f[:, 0, 0], dl_b[:, 0, 0]], axis=0)
    d_decay = d_lam * jax.nn.sigmoid(-decay)

    delta = _attn_delta("attn_delta", ya, dya)
    dq_rot, dk_rot, dva = _attn_bwd("attn_bwd", q_rot, k_rot, p, va_off, dya, lse, delta, lx, groups)
    dqa, dka, d_qg, d_kg = _qk_prep_bwd("qk_prep_bwd", dq_rot, dk_rot, p, cosf, sinf, attn_q_gain, attn_k_gain, aw, kw)

    dp = jnp.concatenate([dqa, dka, dva, dqr.astype(BF16), dkr.astype(BF16), dvr.astype(BF16),
                          _pad_rows(dgr, t), _pad_rows(dga, t), _pad_rows(dgb, t)], axis=1)
    grads = {"mix_w_in": _mm_tn("mix_in_dw", u2, dp, BF16, tm_pref=M_TILE_DW, tn_pref=512)}
    du2, got = _mm_nt("mix_in_bwd", dp, full["mix_w_in"], F32, comm=exchange_of(grads))
    keep_landed(grads, got)
    dh1, dsc2, dsh2 = _rmsmod_bwd("mix_norm_bwd", du2, h1, sc2, dh2, lx, t)

    df1, dg1 = _gate_bwd("ffn1_gate_bwd", dh1, f1, g1, 0.5, lx)
    dz1, got = _ffn_out_bwd("ffn1_out_bwd", df1, full["ffn1_w_out"], z1, comm=exchange_of(mix_grads))
    keep_landed(mix_grads, got)
    grads = {"ffn1_w_out": _mm_tn("ffn1_out_dw", s1, df1, BF16, tm_pref=M_TILE_BIG)}
    g_ffn1_w_in, got = _ffn_in_bwd_w("ffn1_in_dw", u1, dz1, comm=exchange_of(grads))
    keep_landed(grads, got)
    grads = {"ffn1_w_in": g_ffn1_w_in}
    du1, got = _ffn_in_bwd_x("ffn1_in_bwd", dz1, full["ffn1_w_in"], comm=exchange_of(grads))
    keep_landed(grads, got)
    grad_x, dsc1, dsh1 = _rmsmod_bwd("ffn1_norm_bwd", du1, h0, sc1, dh1, lx, lx)

    zero = jnp.zeros((1, d), F32)
    dmod_c = jnp.concatenate([dsh1[0], dsc1[0], dg1[0], dsh2[0], dsc2[0], zero, zero, zero, zero], axis=0)
    dmod_x = jnp.concatenate([dsh1[1], dsc1[1], dg1[1], dsh2[1], dsc2[1], dg2[1], dsh3[1], dsc3[1], dg3[1]], axis=0)
    misc = jnp.concatenate([d_qg[0], d_kg[0], d_decay.reshape(-1), loss_tile[0, 0:1]])
    misc = jnp.pad(misc, (0, d - misc.shape[0]))[None, :]
    n_small = 3 * SUBLANES
    small = _pad_rows(jnp.concatenate([dmod_c, dmod_x, d_fn, misc], axis=0), n_small)
    small_all = _all_gather_small("gather_small", small)
    small_sum = _sum_slots("sum_small", small_all)
    dmod_c_sum, dmod_x_sum = small_sum[0:N_MOD], small_sum[N_MOD:2 * N_MOD]
    g_final_norm = small_sum[2 * N_MOD]
    misc_sum = small_sum[2 * N_MOD + 1]
    g_qg = misc_sum[0:HEAD_DIM][None, :]
    g_kg = misc_sum[HEAD_DIM:2 * HEAD_DIM][None, :]
    g_decay = misc_sum[2 * HEAD_DIM:2 * HEAD_DIM + 2 * heads_r].reshape(1, 2, heads_r)
    loss = misc_sum[2 * HEAD_DIM + 2 * heads_r]
    g_b_ada = (dmod_x_sum + dmod_c_sum).reshape(1, N_MOD * d)

    n_ada = w_ada.shape[2]
    dmod_rows = jnp.concatenate([small_all[:, N_MOD:2 * N_MOD, :].reshape(N_DEV, N_MOD * d),
                                 dmod_c_sum.reshape(1, N_MOD * d)], axis=0)
    dmod_mine = _pad_rows(lax.dynamic_slice_in_dim(dmod_rows, me * n_ada, n_ada, axis=1), 2 * SUBLANES)
    g_w_ada, ds_cond = _ada_bwd("ada_bwd", s_cond, dmod_mine, w_ada[0])
    cctx_parts = _all_gather_small("gather_cctx", ds_cond[N_DEV:N_DEV + SUBLANES])
    g_c_ctx = _cctx_grad("cctx_grad", cctx_parts, c_ctx[None, :])[0]

    mom = dict(ffn1_w_in=(m_ffn1_w_in, v_ffn1_w_in), mix_w_in=(m_mix_w_in, v_mix_w_in),
               w_proj_attn=(m_w_proj_attn, v_w_proj_attn), w_proj_ret=(m_w_proj_ret, v_w_proj_ret),
               ffn2_w_in=(m_ffn2_w_in, v_ffn2_w_in), ffn1_w_out=(m_ffn1_w_out, v_ffn1_w_out),
               mix_w_out=(m_mix_w_out, v_mix_w_out), ffn2_w_out=(m_ffn2_w_out, v_ffn2_w_out))
    res = {}
    for k in names:
        res[k] = _adamw("adamw_" + k, landed[k], shard[k], mom[k][0][0], mom[k][1][0])
    res["w_ada"] = _adamw("adamw_w_ada", g_w_ada[None], w_ada[0], m_w_ada[0], v_w_ada[0])

    def pack(cc, ba, qg, kg, dec, fn):
        misc_row = jnp.concatenate([qg.reshape(-1), kg.reshape(-1), dec.reshape(-1)])
        misc_row = jnp.pad(misc_row, (0, d - misc_row.shape[0]))[None, :]
        return _pad_rows(jnp.concatenate([cc.reshape(1, d), ba.reshape(N_MOD, d), fn.reshape(1, d), misc_row], axis=0),
                         2 * SUBLANES)

    sg, sd, sm, sv = _adamw(
        "adamw_small", pack(g_c_ctx, g_b_ada, g_qg, g_kg, g_decay, g_final_norm)[None],
        pack(c_ctx, b_ada, attn_q_gain, attn_k_gain, ret_decay_logit, final_norm),
        pack(m_c_ctx, m_b_ada, m_attn_q_gain, m_attn_k_gain, m_ret_decay_logit, m_final_norm),
        pack(v_c_ctx, v_b_ada, v_attn_q_gain, v_attn_k_gain, v_ret_decay_logit, v_final_norm))

    def unpack(a):
        misc_row = a[N_MOD + 2]
        return dict(c_ctx=a[0], b_ada=a[1:1 + N_MOD].reshape(1, N_MOD * d), final_norm=a[N_MOD + 1],
                    attn_q_gain=misc_row[0:HEAD_DIM][None, :], attn_k_gain=misc_row[HEAD_DIM:2 * HEAD_DIM][None, :],
                    ret_decay_logit=misc_row[2 * HEAD_DIM:2 * HEAD_DIM + 2 * heads_r].reshape(1, 2, heads_r))

    small_out = [unpack(a) for a in (sg, sd, sm, sv)]
    order = ["c_ctx", "w_ada", "b_ada", "ffn1_w_in", "ffn1_w_out", "mix_w_in", "attn_q_gain", "attn_k_gain",
             "ret_decay_logit", "w_proj_attn", "w_proj_ret", "mix_w_out", "ffn2_w_in", "ffn2_w_out", "final_norm"]
    outs = [loss, grad_x[None]]
    for which in range(4):
        for k in order:
            outs.append(res[k][which][None] if k in res else small_out[which][k])
    return tuple(outs)
```

```python
import math

import jax
import jax.numpy as jnp
from jax import lax
from jax.experimental import pallas as pl
from jax.experimental.pallas import tpu as pltpu

F32 = jnp.float32
BF16 = jnp.bfloat16
MESH = pl.DeviceIdType.MESH

N_DEV = 8
HEAD_DIM = 128
GRID_W = 64
ROPE_THETA = 10000.0
NORM_EPS = 1e-6
RET_CHUNK = 128
N_MOD = 9
LANES = 128
SUBLANES = 8
V7X_VMEM_BYTES = 64 * 1024 * 1024
VMEM_LIMIT = V7X_VMEM_BYTES - 8 * 1024 * 1024
K_TILE = 2560
M_TILE_BIG = 1408
M_TILE_DW = 2048
LOG2E = 1.4426950408889634
LN2 = 0.6931471805599453

ADAM_LR = 0.001
ADAM_B1 = 0.9
ADAM_B2 = 0.999
ADAM_EPS = 1e-08
ADAM_WD = 0.01
ADAM_STEP = 10

NN = (((1,), (0,)), ((), ()))
NT = (((1,), (1,)), ((), ()))
TN = (((0,), (0,)), ((), ()))


def _tile(n, pref, align=LANES):
    best = None
    t = align
    while t <= min(n, pref):
        if n % t == 0:
            best = t
        t += align
    return n if best is None else best


def _cp(sem):
    return pltpu.CompilerParams(dimension_semantics=sem, vmem_limit_bytes=VMEM_LIMIT)


def _sigmoid(v):
    return 0.5 * jnp.tanh(0.5 * v) + 0.5


def _dot(a, b, dims):
    return lax.dot_general(a, b, dims, preferred_element_type=F32)


def _mm(name, a, a_spec, b_list, dims, grid, out_shapes, out_specs, acc_shape, epi, extras=(), comm=None):
    nb, ne, no = len(b_list), len(extras), len(out_shapes)
    nk = grid[2]
    kind, c_arrays = comm if comm is not None else (None, [])
    ncm = len(c_arrays)

    def body(*refs):
        a_ref = refs[0]
        b_refs = refs[1:1 + nb]
        e_refs = refs[1 + nb:1 + nb + ne]
        pos = 1 + nb + ne
        c_ins = refs[pos:pos + ncm]
        o_refs = refs[pos + ncm:pos + ncm + no]
        c_outs = refs[pos + ncm + no:pos + 2 * ncm + no]
        scratch = refs[pos + 2 * ncm + no:]
        accs = scratch[:0 if nk == 1 else nb]
        ids = [pl.program_id(axis) for axis in range(3)]
        if ncm:
            start_comm, finish_comm = _comm_plan(kind, c_ins, c_outs, *scratch[len(accs):])
            pl.when(jnp.logical_and(jnp.logical_and(ids[0] == 0, ids[1] == 0), ids[2] == 0))(start_comm)

        def finish(tiles):
            vals = epi(tiles, e_refs)
            for o_ref, v in zip(o_refs, vals):
                if isinstance(v, tuple):
                    for idx, part in enumerate(v):
                        o_ref[idx] = part.astype(o_ref.dtype)
                else:
                    o_ref[...] = v.astype(o_ref.dtype)

        if nk == 1:
            finish([_dot(a_ref[...], b_ref[...], dims) for b_ref in b_refs])
        else:
            @pl.when(ids[2] == 0)
            def _():
                for acc in accs:
                    acc[...] = jnp.zeros(acc.shape, F32)

            av = a_ref[...]
            for b_ref, acc in zip(b_refs, accs):
                acc[...] += _dot(av, b_ref[...], dims)

            @pl.when(ids[2] == nk - 1)
            def _():
                finish([acc[...] for acc in accs])

        if ncm:
            pl.when(jnp.logical_and(jnp.logical_and(ids[0] == grid[0] - 1, ids[1] == grid[1] - 1),
                                    ids[2] == nk - 1))(finish_comm)

    any_spec = pl.BlockSpec(memory_space=pl.ANY)
    c_shapes = [jax.ShapeDtypeStruct(((N_DEV,) if kind == "gather" else ()) + s.shape, s.dtype) for s in c_arrays]
    scratch_shapes = [] if nk == 1 else [pltpu.VMEM(acc_shape, F32)] * nb
    if ncm:
        scratch_shapes = scratch_shapes + _comm_semaphores(ncm)
    semantics = ("arbitrary",) * 3 if ncm else ("parallel", "parallel", "arbitrary")
    outs = pl.pallas_call(
        body, name=name, grid=grid,
        in_specs=[a_spec] + [s for _, s in b_list] + [s for _, s in extras] + [any_spec] * ncm,
        out_specs=list(out_specs) + [any_spec] * ncm, out_shape=list(out_shapes) + c_shapes,
        scratch_shapes=scratch_shapes, compiler_params=_cp(semantics),
    )(a, *[b for b, _ in b_list], *[e for e, _ in extras], *c_arrays)
    main = outs[0] if no == 1 else tuple(outs[:no])
    return main if comm is None else (main, list(outs[no:]))


def _plain(accs, _):
    return (accs[0],)


def _mm_nn(name, a, b, out_dtype, tm_pref=1024, tn_pref=1024, tk_pref=K_TILE, comm=None):
    m, k = a.shape
    n = b.shape[1]
    tm, tn, tk = _tile(m, tm_pref), _tile(n, tn_pref), _tile(k, tk_pref)
    return _mm(name, a, pl.BlockSpec((tm, tk), lambda i, j, kk: (i, kk)),
               [(b, pl.BlockSpec((tk, tn), lambda i, j, kk: (kk, j)))], NN, (m // tm, n // tn, k // tk),
               [jax.ShapeDtypeStruct((m, n), out_dtype)], [pl.BlockSpec((tm, tn), lambda i, j, kk: (i, j))],
               (tm, tn), _plain, comm=comm)


def _mm_nt(name, a, b, out_dtype, tm_pref=1024, tn_pref=1024, tk_pref=K_TILE, comm=None):
    m, k = a.shape
    n = b.shape[0]
    tm, tn, tk = _tile(m, tm_pref), _tile(n, tn_pref), _tile(k, tk_pref)
    return _mm(name, a, pl.BlockSpec((tm, tk), lambda i, j, kk: (i, kk)),
               [(b, pl.BlockSpec((tn, tk), lambda i, j, kk: (j, kk)))], NT, (m // tm, n // tn, k // tk),
               [jax.ShapeDtypeStruct((m, n), out_dtype)], [pl.BlockSpec((tm, tn), lambda i, j, kk: (i, j))],
               (tm, tn), _plain, comm=comm)


def _mm_tn(name, a, b, out_dtype, rows=None, tm_pref=1024, tn_pref=1024, tk_pref=K_TILE, comm=None):
    k = a.shape[0] if rows is None else rows
    m, n = a.shape[1], b.shape[1]
    tm, tn, tk = _tile(m, tm_pref), _tile(n, tn_pref), _tile(k, tk_pref)
    return _mm(name, a, pl.BlockSpec((tk, tm), lambda i, j, kk: (kk, i)),
               [(b, pl.BlockSpec((tk, tn), lambda i, j, kk: (kk, j)))], TN, (m // tm, n // tn, k // tk),
               [jax.ShapeDtypeStruct((m, n), out_dtype)], [pl.BlockSpec((tm, tn), lambda i, j, kk: (i, j))],
               (tm, tn), _plain, comm=comm)


def _ffn_in(name, u, w_in, comm=None):
    r, d = u.shape
    f = w_in.shape[1] // 2
    tm, tn, tk = _tile(r, 1024), _tile(f, 512), _tile(d, K_TILE)
    nf = f // tn

    def epi(accs, _):
        za, zb = accs
        s = za * _sigmoid(za) * zb
        return (za, zb), s

    return _mm(name, u, pl.BlockSpec((tm, tk), lambda i, j, kk: (i, kk)),
               [(w_in, pl.BlockSpec((tk, tn), lambda i, j, kk: (kk, j))),
                (w_in, pl.BlockSpec((tk, tn), lambda i, j, kk: (kk, j + nf)))],
               NN, (r // tm, nf, d // tk),
               [jax.ShapeDtypeStruct((2, r, f), BF16), jax.ShapeDtypeStruct((r, f), BF16)],
               [pl.BlockSpec((2, tm, tn), lambda i, j, kk: (0, i, j)), pl.BlockSpec((tm, tn), lambda i, j, kk: (i, j))],
               (tm, tn), epi, comm=comm)


def _mm_residual(name, a, w, res, gate, gate_scale, lx, comm=None):
    r, k = a.shape
    n = w.shape[1]
    tm, tn, tk = _tile(r, 1024), _tile(n, 1024), _tile(k, K_TILE)

    def epi(accs, e_refs):
        res_ref, g_ref = e_refs
        rows = pl.program_id(0) * tm + lax.broadcasted_iota(jnp.int32, (tm, 1), 0)
        g = jnp.where(rows < lx, g_ref[1], g_ref[0])
        return res_ref[...] + gate_scale * g * accs[0], accs[0]

    return _mm(name, a, pl.BlockSpec((tm, tk), lambda i, j, kk: (i, kk)),
               [(w, pl.BlockSpec((tk, tn), lambda i, j, kk: (kk, j)))], NN, (r // tm, n // tn, k // tk),
               [jax.ShapeDtypeStruct((r, n), F32), jax.ShapeDtypeStruct((r, n), BF16)],
               [pl.BlockSpec((tm, tn), lambda i, j, kk: (i, j))] * 2, (tm, tn), epi,
               extras=[(res, pl.BlockSpec((tm, tn), lambda i, j, kk: (i, j))),
                       (gate, pl.BlockSpec((2, 1, tn), lambda i, j, kk: (0, 0, j)))], comm=comm)


def _ffn_out_bwd(name, df, w_out, z, comm=None):
    r, d = df.shape
    f = w_out.shape[0]
    tm, tn, tk = _tile(r, M_TILE_BIG), _tile(f, 512), _tile(d, K_TILE)

    def epi(accs, e_refs):
        ds = accs[0]
        za = e_refs[0][0].astype(F32)
        zb = e_refs[0][1].astype(F32)
        sg = _sigmoid(za)
        da = ds * zb * sg * (1.0 + za * (1.0 - sg))
        db = ds * za * sg
        return ((da, db),)

    zspec = pl.BlockSpec((2, tm, tn), lambda i, j, kk: (0, i, j))
    return _mm(name, df, pl.BlockSpec((tm, tk), lambda i, j, kk: (i, kk)),
               [(w_out, pl.BlockSpec((tn, tk), lambda i, j, kk: (j, kk)))], NT, (r // tm, f // tn, d // tk),
               [jax.ShapeDtypeStruct((2, r, f), BF16)], [zspec], (tm, tn), epi, extras=[(z, zspec)], comm=comm)


def _ffn_in_bwd_x(name, dz, w_in, comm=None):
    _, r, f = dz.shape
    d = w_in.shape[0]
    tm, tn, tk = _tile(r, 1024), _tile(d, 2048), _tile(f, K_TILE)
    nkf = f // tk
    return _mm(name, dz, pl.BlockSpec((None, tm, tk), lambda i, j, kk: (kk // nkf, i, kk % nkf)),
               [(w_in, pl.BlockSpec((tn, tk), lambda i, j, kk: (j, kk)))], NT, (r // tm, d // tn, 2 * nkf),
               [jax.ShapeDtypeStruct((r, d), F32)], [pl.BlockSpec((tm, tn), lambda i, j, kk: (i, j))],
               (tm, tn), _plain, comm=comm)


def _ffn_in_bwd_w(name, u, dz, comm=None):
    r, d = u.shape
    f = dz.shape[2]
    tm, tn, tk = _tile(d, M_TILE_DW), _tile(f, 512), _tile(r, K_TILE)
    nf = f // tn
    return _mm(name, u, pl.BlockSpec((tk, tm), lambda i, j, kk: (kk, i)),
               [(dz, pl.BlockSpec((None, tk, tn), lambda i, j, kk: (j // nf, kk, j % nf)))], TN,
               (d // tm, 2 * nf, r // tk),
               [jax.ShapeDtypeStruct((d, 2 * f), BF16)], [pl.BlockSpec((tm, tn), lambda i, j, kk: (i, j))],
               (tm, tn), _plain, comm=comm)


def _merge_bwd(name, dout, w_out, pa, pr, p, ga_off, gb_off):
    r, d = dout.shape
    n = w_out.shape[0]
    cw = math.gcd(math.gcd(ga_off, gb_off), n)
    tm, tn, tk = _tile(r, 1024), _tile(cw, 512), _tile(d, K_TILE)

    def epi(accs, e_refs):
        dm = accs[0]
        pa_ref, pr_ref, ga_ref, gb_ref = e_refs
        sa = _sigmoid(ga_ref[...].astype(F32))
        sb = _sigmoid(gb_ref[...].astype(F32))
        pav = pa_ref[...].astype(F32)
        prv = pr_ref[...].astype(F32)
        return dm * sa, dm * sb, dm * pav * sa * (1.0 - sa), dm * prv * sb * (1.0 - sb)

    o_spec = pl.BlockSpec((tm, tn), lambda i, j, kk: (i, j))
    return _mm(name, dout, pl.BlockSpec((tm, tk), lambda i, j, kk: (i, kk)),
               [(w_out, pl.BlockSpec((tn, tk), lambda i, j, kk: (j, kk)))], NT, (r // tm, n // tn, d // tk),
               [jax.ShapeDtypeStruct((r, n), BF16)] * 4, [o_spec] * 4, (tm, tn), epi,
               extras=[(pa, o_spec), (pr, o_spec),
                       (p, pl.BlockSpec((tm, tn), lambda i, j, kk: (i, ga_off // tn + j))),
                       (p, pl.BlockSpec((tm, tn), lambda i, j, kk: (i, gb_off // tn + j)))])


def _row_tile(r, lx, d):
    pref = 256 if d > 1024 else 512
    return _tile(math.gcd(r, lx), pref, SUBLANES)


def _rmsmod(name, h, scale, shift, lx, rows=None):
    r = h.shape[0] if rows is None else rows
    d = h.shape[1]
    tr = _row_tile(r, lx, d)
    nx = lx // tr
    cls = lambda i: (jnp.where(i < nx, 1, 0), 0, 0)

    def body(h_ref, sc_ref, sh_ref, u_ref):
        hv = h_ref[...]
        rinv = lax.rsqrt(jnp.mean(hv * hv, axis=-1, keepdims=True) + NORM_EPS)
        u_ref[...] = (hv * rinv * (1.0 + sc_ref[...]) + sh_ref[...]).astype(u_ref.dtype)

    return pl.pallas_call(
        body, name=name, grid=(r // tr,),
        in_specs=[pl.BlockSpec((tr, d), lambda i: (i, 0)), pl.BlockSpec((None, 1, d), cls), pl.BlockSpec((None, 1, d), cls)],
        out_specs=pl.BlockSpec((tr, d), lambda i: (i, 0)), out_shape=jax.ShapeDtypeStruct((r, d), BF16),
        compiler_params=_cp(("parallel",)))(h, scale, shift)


def _rmsmod_bwd(name, du, h, scale, dh_in, lx, rows_out):
    r, d = du.shape
    rin = dh_in.shape[0]
    tr = _row_tile(math.gcd(r, math.gcd(rin, rows_out)), lx, d)
    nx, nin, nout = lx // tr, rin // tr, rows_out // tr
    cls = lambda i: (jnp.where(i < nx, 1, 0), 0, 0)

    def body(du_ref, h_ref, sc_ref, dhin_ref, dh_ref, dsc_ref, dsh_ref):
        i = pl.program_id(0)
        hv = h_ref[...]
        duv = du_ref[...]
        rinv = lax.rsqrt(jnp.mean(hv * hv, axis=-1, keepdims=True) + NORM_EPS)
        nv = hv * rinv
        dn = duv * (1.0 + sc_ref[...])

        @pl.when(jnp.logical_or(i == 0, i == nx))
        def _():
            dsc_ref[...] = jnp.zeros(dsc_ref.shape, F32)
            dsh_ref[...] = jnp.zeros(dsh_ref.shape, F32)

        dsc_ref[...] += jnp.sum(duv * nv, axis=0, keepdims=True)
        dsh_ref[...] += jnp.sum(duv, axis=0, keepdims=True)

        @pl.when(i < nout)
        def _():
            dh = rinv * (dn - nv * jnp.mean(dn * nv, axis=-1, keepdims=True))
            dh_ref[...] = dh + jnp.where(i < nin, dhin_ref[...], 0.0)

    return pl.pallas_call(
        body, name=name, grid=(r // tr,),
        in_specs=[pl.BlockSpec((tr, d), lambda i: (i, 0)), pl.BlockSpec((tr, d), lambda i: (i, 0)),
                  pl.BlockSpec((None, 1, d), cls), pl.BlockSpec((tr, d), lambda i: (jnp.minimum(i, nin - 1), 0))],
        out_specs=[pl.BlockSpec((tr, d), lambda i: (jnp.minimum(i, nout - 1), 0)),
                   pl.BlockSpec((None, 1, d), cls), pl.BlockSpec((None, 1, d), cls)],
        out_shape=[jax.ShapeDtypeStruct((rows_out, d), F32), jax.ShapeDtypeStruct((2, 1, d), F32),
                   jax.ShapeDtypeStruct((2, 1, d), F32)],
        compiler_params=_cp(("arbitrary",)))(du, h, scale, dh_in)


def _gate_bwd(name, dh, f, gate, gate_scale, lx):
    r, d = dh.shape
    tr = _row_tile(r, lx, d)
    nx = lx // tr
    cls = lambda i: (jnp.where(i < nx, 1, 0), 0, 0)

    def body(dh_ref, f_ref, g_ref, df_ref, dg_ref):
        i = pl.program_id(0)
        dhv = dh_ref[...]

        @pl.when(jnp.logical_or(i == 0, i == nx))
        def _():
            dg_ref[...] = jnp.zeros(dg_ref.shape, F32)

        df_ref[...] = (gate_scale * g_ref[...] * dhv).astype(df_ref.dtype)
        dg_ref[...] += jnp.sum(gate_scale * f_ref[...].astype(F32) * dhv, axis=0, keepdims=True)

    return pl.pallas_call(
        body, name=name, grid=(r // tr,),
        in_specs=[pl.BlockSpec((tr, d), lambda i: (i, 0)), pl.BlockSpec((tr, d), lambda i: (i, 0)),
                  pl.BlockSpec((None, 1, d), cls)],
        out_specs=[pl.BlockSpec((tr, d), lambda i: (i, 0)), pl.BlockSpec((None, 1, d), cls)],
        out_shape=[jax.ShapeDtypeStruct((r, d), BF16), jax.ShapeDtypeStruct((2, 1, d), F32)],
        compiler_params=_cp(("arbitrary",)))(dh, f, gate)


def _final_loss(name, h, final_norm, target):
    r, d = h.shape
    tr = _row_tile(r, r, d)

    def body(h_ref, fn_ref, t_ref, dh_ref, dfn_ref, loss_ref):
        i = pl.program_id(0)
        hv = h_ref[...]
        rinv = lax.rsqrt(jnp.mean(hv * hv, axis=-1, keepdims=True) + NORM_EPS)
        nv = hv * rinv
        fn = fn_ref[...]
        err = nv * fn - t_ref[...]
        dy = err * (1.0 / d)

        @pl.when(i == 0)
        def _():
            dfn_ref[...] = jnp.zeros(dfn_ref.shape, F32)
            loss_ref[...] = jnp.zeros(loss_ref.shape, F32)

        loss_ref[...] += 0.5 * jnp.sum(jnp.mean(err * err, axis=-1, keepdims=True), axis=0, keepdims=True)
        dfn_ref[...] += jnp.sum(dy * nv, axis=0, keepdims=True)
        dn = dy * fn
        dh_ref[...] = rinv * (dn - nv * jnp.mean(dn * nv, axis=-1, keepdims=True))

    return pl.pallas_call(
        body, name=name, grid=(r // tr,),
        in_specs=[pl.BlockSpec((tr, d), lambda i: (i, 0)), pl.BlockSpec((1, d), lambda i: (0, 0)),
                  pl.BlockSpec((tr, d), lambda i: (i, 0))],
        out_specs=[pl.BlockSpec((tr, d), lambda i: (i, 0)), pl.BlockSpec((1, d), lambda i: (0, 0)),
                   pl.BlockSpec((SUBLANES, LANES), lambda i: (0, 0))],
        out_shape=[jax.ShapeDtypeStruct((r, d), F32), jax.ShapeDtypeStruct((1, d), F32),
                   jax.ShapeDtypeStruct((SUBLANES, LANES), F32)],
        compiler_params=_cp(("arbitrary",)))(h, final_norm, target)


def _swap_pairs(t):
    lane = lax.broadcasted_iota(jnp.int32, t.shape, 1)
    nxt = pltpu.roll(t, HEAD_DIM - 1, 1)
    prv = pltpu.roll(t, 1, 1)
    return jnp.where(lane % 2 == 0, nxt, prv)


def _qk_prep(name, p, cosf, sinf, q_gain, k_gain, aw, kw):
    t = p.shape[0]
    tr = _tile(t, 256, SUBLANES)
    q_prescale = HEAD_DIM ** -0.5 * LOG2E

    def head_fwd(v, gain, cs, sn):
        v = v.astype(F32)
        rinv = lax.rsqrt(jnp.mean(v * v, axis=-1, keepdims=True) + NORM_EPS)
        tt = v * rinv * gain
        return tt * cs + _swap_pairs(tt) * sn

    def body(q_ref, k_ref, cos_ref, sin_ref, qg_ref, kg_ref, qo_ref, ko_ref):
        cs, sn = cos_ref[...], sin_ref[...]
        for hh in range(aw // HEAD_DIM):
            sl = slice(hh * HEAD_DIM, (hh + 1) * HEAD_DIM)
            qo_ref[:, sl] = (head_fwd(q_ref[:, sl], qg_ref[...], cs, sn) * q_prescale).astype(qo_ref.dtype)
        for hh in range(kw // HEAD_DIM):
            sl = slice(hh * HEAD_DIM, (hh + 1) * HEAD_DIM)
            ko_ref[:, sl] = head_fwd(k_ref[:, sl], kg_ref[...], cs, sn).astype(ko_ref.dtype)

    assert aw % kw == 0
    row = lambda i: (i, 0)
    return pl.pallas_call(
        body, name=name, grid=(t // tr,),
        in_specs=[pl.BlockSpec((tr, aw), row), pl.BlockSpec((tr, kw), lambda i: (i, aw // kw)),
                  pl.BlockSpec((tr, HEAD_DIM), row), pl.BlockSpec((tr, HEAD_DIM), row),
                  pl.BlockSpec((1, HEAD_DIM), lambda i: (0, 0)), pl.BlockSpec((1, HEAD_DIM), lambda i: (0, 0))],
        out_specs=[pl.BlockSpec((tr, aw), row), pl.BlockSpec((tr, kw), row)],
        out_shape=[jax.ShapeDtypeStruct((t, aw), BF16), jax.ShapeDtypeStruct((t, kw), BF16)],
        compiler_params=_cp(("parallel",)))(p, p, cosf, sinf, q_gain, k_gain)


def _qk_prep_bwd(name, dq_rot, dk_rot, p, cosf, sinf, q_gain, k_gain, aw, kw):
    t = p.shape[0]
    lq = dq_rot.shape[0]
    tr = _tile(math.gcd(t, lq), 256, SUBLANES)
    nq = lq // tr

    def head_bwd(dout, v, gain, cs, sn):
        v = v.astype(F32)
        rinv = lax.rsqrt(jnp.mean(v * v, axis=-1, keepdims=True) + NORM_EPS)
        vn = v * rinv
        dt = dout * cs - _swap_pairs(dout) * sn
        dvn = dt * gain
        dv = rinv * (dvn - vn * jnp.mean(dvn * vn, axis=-1, keepdims=True))
        return dv, jnp.sum(dt * vn, axis=0, keepdims=True)

    def body(dq_ref, dk_ref, q_ref, k_ref, cos_ref, sin_ref, qg_ref, kg_ref, dqo_ref, dko_ref, dqg_ref, dkg_ref):
        i = pl.program_id(0)
        cs, sn = cos_ref[...], sin_ref[...]

        @pl.when(i == 0)
        def _():
            dqg_ref[...] = jnp.zeros(dqg_ref.shape, F32)
            dkg_ref[...] = jnp.zeros(dkg_ref.shape, F32)

        has_q = i < nq
        for hh in range(aw // HEAD_DIM):
            sl = slice(hh * HEAD_DIM, (hh + 1) * HEAD_DIM)
            dout = jnp.where(has_q, dq_ref[:, sl], 0.0)
            dv, dg = head_bwd(dout, q_ref[:, sl], qg_ref[...], cs, sn)
            dqo_ref[:, sl] = dv.astype(dqo_ref.dtype)
            dqg_ref[...] += dg
        for hh in range(kw // HEAD_DIM):
            sl = slice(hh * HEAD_DIM, (hh + 1) * HEAD_DIM)
            dv, dg = head_bwd(dk_ref[:, sl], k_ref[:, sl], kg_ref[...], cs, sn)
            dko_ref[:, sl] = dv.astype(dko_ref.dtype)
            dkg_ref[...] += dg

    row = lambda i: (i, 0)
    one = lambda i: (0, 0)
    return pl.pallas_call(
        body, name=name, grid=(t // tr,),
        in_specs=[pl.BlockSpec((tr, aw), lambda i: (jnp.minimum(i, nq - 1), 0)), pl.BlockSpec((tr, kw), row),
                  pl.BlockSpec((tr, aw), row), pl.BlockSpec((tr, kw), lambda i: (i, aw // kw)),
                  pl.BlockSpec((tr, HEAD_DIM), row), pl.BlockSpec((tr, HEAD_DIM), row),
                  pl.BlockSpec((1, HEAD_DIM), one), pl.BlockSpec((1, HEAD_DIM), one)],
        out_specs=[pl.BlockSpec((tr, aw), row), pl.BlockSpec((tr, kw), row),
                   pl.BlockSpec((1, HEAD_DIM), one), pl.BlockSpec((1, HEAD_DIM), one)],
        out_shape=[jax.ShapeDtypeStruct((t, aw), BF16), jax.ShapeDtypeStruct((t, kw), BF16),
                   jax.ShapeDtypeStruct((1, HEAD_DIM), F32), jax.ShapeDtypeStruct((1, HEAD_DIM), F32)],
        compiler_params=_cp(("arbitrary",)))(dq_rot, dk_rot, p, p, cosf, sinf, q_gain, k_gain)


def _attn_tiles(lq, t):
    return _tile(lq, 1024), _tile(t, 768)


def _attn_fwd(name, q, k, p, v_off, lq, groups):
    t = k.shape[0]
    hq = q.shape[1] // HEAD_DIM
    hkv = hq // groups
    gw = groups * HEAD_DIM
    tq, tk = _attn_tiles(lq, t)
    nkv = t // tk
    vb = v_off // HEAD_DIM

    def body(q_ref, k_ref, v_ref, o_ref, lse_ref, m_sc, l_sc, acc_sc, s_sc):
        j = pl.program_id(2)

        @pl.when(j == 0)
        def _():
            m_sc[...] = jnp.full(m_sc.shape, -jnp.inf, F32)
            l_sc[...] = jnp.zeros(l_sc.shape, F32)
            acc_sc[...] = jnp.zeros(acc_sc.shape, F32)
            s_sc[...] = jnp.zeros(s_sc.shape, F32)

        has_prev = j > 0
        kv, vv = k_ref[...], v_ref[...]
        for g in range(groups):
            sl = slice(g * HEAD_DIM, (g + 1) * HEAD_DIM)
            s = s_sc[g]
            s_sc[g] = _dot(q_ref[:, sl], kv, NT)
            m_prev = m_sc[g]
            m_new = jnp.where(has_prev, jnp.maximum(m_prev, jnp.max(s, axis=-1, keepdims=True)), jnp.inf)
            alpha = jnp.exp2(m_prev - m_new)
            pexp = jnp.exp2(s - m_new)
            part = pexp[:, 0:LANES]
            for cb in range(1, tk // LANES):
                part = part + pexp[:, cb * LANES:(cb + 1) * LANES]
            l_sc[g] = alpha * l_sc[g] + part
            acc_sc[:, sl] = alpha * acc_sc[:, sl] + _dot(pexp.astype(BF16), vv, NN)
            m_sc[g] = jnp.where(has_prev, m_new, -jnp.inf)

        @pl.when(j == nkv)
        def _():
            for g in range(groups):
                sl = slice(g * HEAD_DIM, (g + 1) * HEAD_DIM)
                l_row = jnp.sum(l_sc[g], axis=-1, keepdims=True)
                o_ref[:, sl] = (acc_sc[:, sl] * (1.0 / l_row)).astype(o_ref.dtype)
                lse_ref[g] = jnp.broadcast_to(m_sc[g] + jnp.log2(l_row), (tq, LANES))

    return pl.pallas_call(
        body, name=name, grid=(hkv, lq // tq, nkv + 1),
        in_specs=[pl.BlockSpec((tq, gw), lambda h, i, j: (i, h)),
                  pl.BlockSpec((tk, HEAD_DIM), lambda h, i, j: (jnp.minimum(j, nkv - 1), h)),
                  pl.BlockSpec((tk, HEAD_DIM), lambda h, i, j: (jnp.maximum(j - 1, 0), vb + h))],
        out_specs=[pl.BlockSpec((tq, gw), lambda h, i, j: (i, h)),
                   pl.BlockSpec((groups, tq, LANES), lambda h, i, j: (h, i, 0))],
        out_shape=[jax.ShapeDtypeStruct((lq, hq * HEAD_DIM), BF16), jax.ShapeDtypeStruct((hq, lq, LANES), F32)],
        scratch_shapes=[pltpu.VMEM((groups, tq, 1), F32), pltpu.VMEM((groups, tq, LANES), F32), pltpu.VMEM((tq, gw), F32),
                        pltpu.VMEM((groups, tq, tk), F32)],
        compiler_params=_cp(("parallel", "parallel", "arbitrary")))(q, k, p)


def _attn_delta(name, o, do):
    lq, aw = o.shape
    hq = aw // HEAD_DIM
    tr = _tile(lq, 512, SUBLANES)

    def body(o_ref, do_ref, d_ref):
        for h in range(hq):
            sl = slice(h * HEAD_DIM, (h + 1) * HEAD_DIM)
            dsum = jnp.sum(do_ref[:, sl].astype(F32) * o_ref[:, sl].astype(F32), axis=-1, keepdims=True)
            d_ref[h] = jnp.broadcast_to(dsum, (tr, LANES))

    spec = pl.BlockSpec((tr, aw), lambda i: (i, 0))
    return pl.pallas_call(
        body, name=name, grid=(lq // tr,), in_specs=[spec, spec],
        out_specs=pl.BlockSpec((hq, tr, LANES), lambda i: (0, i, 0)),
        out_shape=jax.ShapeDtypeStruct((hq, lq, LANES), F32), compiler_params=_cp(("parallel",)))(o, do)


def _attn_bwd(name, q, k, p, v_off, do, lse, delta, lq, groups):
    t = k.shape[0]
    hkv = k.shape[1] // HEAD_DIM
    gw = groups * HEAD_DIM
    tq, tk = _attn_tiles(lq, t)
    nq, nkv = lq // tq, t // tk
    scale = HEAD_DIM ** -0.5
    vb = v_off // HEAD_DIM

    def body(q_ref, k_ref, v_ref, do_ref, lse_ref, delta_ref, dq_ref, dk_ref, dv_ref, dq_sc, dk_sc, dv_sc):
        j, i = pl.program_id(1), pl.program_id(2)

        @pl.when(i == 0)
        def _():
            dk_sc[...] = jnp.zeros(dk_sc.shape, F32)
            dv_sc[...] = jnp.zeros(dv_sc.shape, F32)

        @pl.when(j == 0)
        def _():
            dq_sc[i] = jnp.zeros((tq, gw), F32)

        kv, vv = k_ref[...], v_ref[...]
        dk_part, dv_part = None, None
        for g in range(groups):
            sl = slice(g * HEAD_DIM, (g + 1) * HEAD_DIM)
            qv, dov = q_ref[:, sl], do_ref[:, sl]
            s = _dot(qv, kv, NT)
            pexp = jnp.exp2(s - lse_ref[g, :, 0:1])
            dv_g = _dot(pexp.astype(BF16), dov, TN)
            dp = _dot(dov, vv, NT)
            ds = (pexp * (dp - delta_ref[g, :, 0:1])).astype(BF16)
            dk_g = _dot(ds, qv, TN)
            dq_sc[i, :, sl] += _dot(ds, kv, NN)
            dk_part = dk_g if dk_part is None else dk_part + dk_g
            dv_part = dv_g if dv_part is None else dv_part + dv_g
        dk_sc[...] += dk_part
        dv_sc[...] += dv_part

        @pl.when(i == nq - 1)
        def _():
            dk_ref[...] = dk_sc[...] * LN2
            dv_ref[...] = dv_sc[...].astype(dv_ref.dtype)

        @pl.when(j == nkv - 1)
        def _():
            dq_ref[...] = dq_sc[i] * scale

    qspec = pl.BlockSpec((tq, gw), lambda kh, j, i: (i, kh))
    kspec = pl.BlockSpec((tk, HEAD_DIM), lambda kh, j, i: (j, kh))
    rowspec = pl.BlockSpec((groups, tq, LANES), lambda kh, j, i: (kh, i, 0))
    dqspec = pl.BlockSpec((tq, gw), lambda kh, j, i: (jnp.where(j == nkv - 1, i, 0), kh))
    return pl.pallas_call(
        body, name=name, grid=(hkv, nkv, nq),
        in_specs=[qspec, kspec, pl.BlockSpec((tk, HEAD_DIM), lambda kh, j, i: (j, vb + kh)), qspec, rowspec, rowspec],
        out_specs=[dqspec, kspec, kspec],
        out_shape=[jax.ShapeDtypeStruct((lq, hkv * gw), F32), jax.ShapeDtypeStruct((t, hkv * HEAD_DIM), F32),
                   jax.ShapeDtypeStruct((t, hkv * HEAD_DIM), BF16)],
        scratch_shapes=[pltpu.VMEM((nq, tq, gw), F32), pltpu.VMEM((tk, HEAD_DIM), F32), pltpu.VMEM((tk, HEAD_DIM), F32)],
        compiler_params=_cp(("parallel", "arbitrary", "arbitrary")))(q, k, p, do, lse, delta)


def _ret_tables(log_gamma, direction):
    c = RET_CHUNK
    idx = jnp.arange(c, dtype=F32)
    diff = idx[:, None] - idx[None, :]
    if direction == 1:
        diff = -diff
    keep = diff >= 0
    lg = log_gamma.astype(F32)
    mask = jnp.where(keep[None], jnp.exp(jnp.where(keep, diff, 0.0)[None] * lg[:, None, None]), 0.0)
    q_exp = idx + 1.0 if direction == 0 else c - idx
    k_exp = c - 1.0 - idx if direction == 0 else idx
    sign = 1.0 if direction == 0 else -1.0
    lane = lambda v: jnp.broadcast_to(v[..., None], v.shape + (LANES,))
    qdec = lane(jnp.exp(q_exp[None, :] * lg[:, None]))
    kdec = lane(jnp.exp(k_exp[None, :] * lg[:, None]))
    cdec = jnp.broadcast_to(jnp.exp(c * lg)[:, None, None], (lg.shape[0], SUBLANES, LANES))
    weights = lane(jnp.stack([sign * idx, q_exp, -sign * idx, k_exp], axis=0))
    return mask, qdec, kdec, cdec, weights


def _ret_chunk_of(direction, step, nx, nc):
    if direction == 0:
        return jnp.where(step < nc, nx + step, step - nc)
    return jnp.where(step < nc, nx + nc - 1 - step, nx + nc - 1 - step)


def _ret_fwd(name, p, offs, tables, direction, lx, prev):
    q_off, k_off, v_off, rw = offs
    t = p.shape[0]
    c = RET_CHUNK
    n_steps, nx = t // c, lx // c
    nc = n_steps - nx
    bw = math.gcd(math.gcd(q_off, k_off), math.gcd(v_off, rw))
    bw = _tile(bw, 512)
    hpb = bw // HEAD_DIM
    heads = rw // HEAD_DIM
    k_scale = HEAD_DIM ** -0.5
    mask, qdec, kdec, cdec, _ = tables
    rc = lambda n: _ret_chunk_of(direction, n, nx, nc)

    ng = heads // hpb

    def body(*refs):
        q_refs, k_refs, v_refs = refs[0:ng], refs[ng:2 * ng], refs[2 * ng:3 * ng]
        if prev is None:
            m_ref, qd_ref, kd_ref, cd_ref, y_ref, st_ref, s_sc = refs[3 * ng:]
        else:
            m_ref, qd_ref, kd_ref, cd_ref, prev_ref, y_ref, st_ref, s_sc = refs[3 * ng:]
        n = pl.program_id(0)

        @pl.when(n == 0)
        def _():
            s_sc[...] = jnp.zeros(s_sc.shape, F32)

        for hd in range(heads):
            gg, hh = divmod(hd, hpb)
            sl = slice(hh * HEAD_DIM, (hh + 1) * HEAD_DIM)
            osl = slice(hd * HEAD_DIM, (hd + 1) * HEAD_DIM)
            qv = q_refs[gg][:, sl]
            kf = k_refs[gg][:, sl].astype(F32) * k_scale
            kv = kf.astype(BF16)
            vv = v_refs[gg][:, sl]
            state = s_sc[hd]
            st_ref[hd] = state
            a = _dot(qv, kv, NT) * m_ref[hd]
            y = _dot(a.astype(BF16), vv, NN) + _dot(qv, state.astype(BF16), NN) * qd_ref[hd]
            s_sc[hd] = state * cd_ref[hd, 0:1, :] + _dot((kf * kd_ref[hd]).astype(BF16), vv, TN)
            if prev is not None:
                y = y + prev_ref[:, osl]
            y_ref[:, osl] = y

    col = lambda off, gg: (lambda n: (rc(n), off // bw + gg))
    tab3 = lambda n: (0, 0, 0)
    in_specs = [pl.BlockSpec((c, bw), col(off, gg)) for off in (q_off, k_off, v_off) for gg in range(ng)]
    in_specs += [pl.BlockSpec((heads, c, c), tab3), pl.BlockSpec((heads, c, LANES), tab3),
                 pl.BlockSpec((heads, c, LANES), tab3), pl.BlockSpec((heads, SUBLANES, LANES), tab3)]
    args = [p] * (3 * ng) + [mask, qdec, kdec, cdec]
    aliases = {}
    yspec = pl.BlockSpec((c, rw), lambda n: (rc(n), 0))
    if prev is not None:
        in_specs.append(yspec)
        args.append(prev)
        aliases = {len(args) - 1: 0}
    return pl.pallas_call(
        body, name=name, grid=(n_steps,), in_specs=in_specs,
        out_specs=[yspec, pl.BlockSpec((None, heads, HEAD_DIM, HEAD_DIM), lambda n: (n, 0, 0, 0))],
        out_shape=[jax.ShapeDtypeStruct((t, rw), F32), jax.ShapeDtypeStruct((n_steps, heads, HEAD_DIM, HEAD_DIM), F32)],
        scratch_shapes=[pltpu.VMEM((heads, HEAD_DIM, HEAD_DIM), F32)], input_output_aliases=aliases,
        compiler_params=_cp(("arbitrary",)))(*args)


def _ret_bwd(name, p, offs, tables, states, dy, direction, lx, prev):
    q_off, k_off, v_off, rw = offs
    t = p.shape[0]
    c = RET_CHUNK
    n_steps, nx = t // c, lx // c
    nc = n_steps - nx
    bw = math.gcd(math.gcd(q_off, k_off), math.gcd(v_off, rw))
    bw = _tile(bw, 512)
    hpb = bw // HEAD_DIM
    heads = rw // HEAD_DIM
    k_scale = HEAD_DIM ** -0.5
    mask, qdec, kdec, cdec, weights = tables
    step_of = lambda n: n_steps - 1 - n
    rc = lambda n: _ret_chunk_of(direction, step_of(n), nx, nc)

    ng = heads // hpb

    def body(*refs):
        q_refs, k_refs, v_refs = refs[0:ng], refs[ng:2 * ng], refs[2 * ng:3 * ng]
        if prev is None:
            (dy_ref, st_ref, m_ref, qd_ref, kd_ref, cd_ref, w_ref,
             dq_ref, dk_ref, dv_ref, dl_ref, ds_sc, lam_sc) = refs[3 * ng:]
        else:
            (dy_ref, st_ref, m_ref, qd_ref, kd_ref, cd_ref, w_ref, pq_ref, pk_ref, pv_ref,
             dq_ref, dk_ref, dv_ref, dl_ref, ds_sc, lam_sc) = refs[3 * ng:]
        n = pl.program_id(0)

        @pl.when(n == 0)
        def _():
            ds_sc[...] = jnp.zeros(ds_sc.shape, F32)
            lam_sc[...] = jnp.zeros(lam_sc.shape, F32)

        is_x = rc(n) < nx
        for hd in range(heads):
            gg, hh = divmod(hd, hpb)
            sl = slice(hh * HEAD_DIM, (hh + 1) * HEAD_DIM)
            osl = slice(hd * HEAD_DIM, (hd + 1) * HEAD_DIM)
            qv = q_refs[gg][:, sl]
            qf = qv.astype(F32)
            kf = k_refs[gg][:, sl].astype(F32) * k_scale
            kv = kf.astype(BF16)
            vv = v_refs[gg][:, sl]
            dyv = jnp.where(is_x, dy_ref[:, osl], 0.0).astype(BF16)
            state = st_ref[hd]
            dstate = ds_sc[hd]
            dstate_b = dstate.astype(BF16)
            msk, qd, kd = m_ref[hd], qd_ref[hd], kd_ref[hd]
            cd = cd_ref[hd, 0:1, :]
            a = _dot(qv, kv, NT) * msk
            da = (_dot(dyv, vv, NT) * msk).astype(BF16)
            dq_intra = _dot(da, kv, NN)
            dk_intra = _dot(da, qv, TN)
            dq_inter = _dot(dyv, state.astype(BF16), NT) * qd
            dk_inter = _dot(vv, dstate_b, NT) * kd
            dv = _dot(a.astype(BF16), dyv, TN) + _dot((kf * kd).astype(BF16), dstate_b, NN)
            lam_sc[hd] += (qf * (w_ref[0] * dq_intra + w_ref[1] * dq_inter)
                           + kf * (w_ref[2] * dk_intra + w_ref[3] * dk_inter)
                           + (c * cd) * state * dstate)
            ds_sc[hd] = _dot((qf * qd).astype(BF16), dyv, TN) + cd * dstate
            dq = dq_intra + dq_inter
            dk = (dk_intra + dk_inter) * k_scale
            if prev is not None:
                dq = dq + pq_ref[:, osl]
                dk = dk + pk_ref[:, osl]
                dv = dv + pv_ref[:, osl]
            dq_ref[:, osl] = dq
            dk_ref[:, osl] = dk
            dv_ref[:, osl] = dv

        @pl.when(n == n_steps - 1)
        def _():
            for hd in range(heads):
                dl_ref[hd] = jnp.broadcast_to(jnp.sum(lam_sc[hd]), (SUBLANES, LANES))

    col = lambda off, gg: (lambda n: (rc(n), off // bw + gg))
    tab3 = lambda n: (0, 0, 0)
    ospec = pl.BlockSpec((c, rw), lambda n: (rc(n), 0))
    in_specs = [pl.BlockSpec((c, bw), col(off, gg)) for off in (q_off, k_off, v_off) for gg in range(ng)]
    in_specs += [pl.BlockSpec((c, rw), lambda n: (jnp.minimum(rc(n), nx - 1), 0)),
                 pl.BlockSpec((None, heads, HEAD_DIM, HEAD_DIM), lambda n: (step_of(n), 0, 0, 0)),
                 pl.BlockSpec((heads, c, c), tab3), pl.BlockSpec((heads, c, LANES), tab3), pl.BlockSpec((heads, c, LANES), tab3),
                 pl.BlockSpec((heads, SUBLANES, LANES), tab3), pl.BlockSpec((4, c, LANES), tab3)]
    args = [p] * (3 * ng) + [dy, states, mask, qdec, kdec, cdec, weights]
    aliases = {}
    if prev is not None:
        for k_out, arr in enumerate(prev):
            in_specs.append(ospec)
            args.append(arr)
            aliases[len(args) - 1] = k_out
    big = jax.ShapeDtypeStruct((t, rw), F32)
    return pl.pallas_call(
        body, name=name, grid=(n_steps,), in_specs=in_specs,
        out_specs=[ospec, ospec, ospec, pl.BlockSpec((heads, SUBLANES, LANES), tab3)],
        out_shape=[big, big, big, jax.ShapeDtypeStruct((heads, SUBLANES, LANES), F32)],
        scratch_shapes=[pltpu.VMEM((heads, HEAD_DIM, HEAD_DIM), F32), pltpu.VMEM((heads, HEAD_DIM, HEAD_DIM), F32)],
        input_output_aliases=aliases, compiler_params=_cp(("arbitrary",)))(*args)


def _ret_out(name, y, p, g_off, lx):
    rw = y.shape[1]
    bw = _tile(math.gcd(g_off, rw), 512)
    tr = _tile(lx, 512, SUBLANES)

    def body(y_ref, g_ref, o_ref):
        for hh in range(bw // HEAD_DIM):
            sl = slice(hh * HEAD_DIM, (hh + 1) * HEAD_DIM)
            yv = y_ref[:, sl]
            gv = g_ref[:, sl].astype(F32)
            rinv = lax.rsqrt(jnp.mean(yv * yv, axis=-1, keepdims=True) + NORM_EPS)
            o_ref[:, sl] = (gv * _sigmoid(gv) * yv * rinv).astype(o_ref.dtype)

    spec = pl.BlockSpec((tr, bw), lambda i, g: (i, g))
    return pl.pallas_call(
        body, name=name, grid=(lx // tr, rw // bw),
        in_specs=[spec, pl.BlockSpec((tr, bw), lambda i, g: (i, g_off // bw + g))],
        out_specs=spec, out_shape=jax.ShapeDtypeStruct((lx, rw), BF16),
        compiler_params=_cp(("parallel", "parallel")))(y, p)


def _ret_out_bwd(name, dyr, y, p, g_off, lx):
    rw = y.shape[1]
    bw = _tile(math.gcd(g_off, rw), 512)
    tr = _tile(lx, 512, SUBLANES)

    def body(d_ref, y_ref, g_ref, dy_ref, dg_ref):
        for hh in range(bw // HEAD_DIM):
            sl = slice(hh * HEAD_DIM, (hh + 1) * HEAD_DIM)
            yv = y_ref[:, sl]
            gv = g_ref[:, sl].astype(F32)
            dv = d_ref[:, sl]
            rinv = lax.rsqrt(jnp.mean(yv * yv, axis=-1, keepdims=True) + NORM_EPS)
            yn = yv * rinv
            sg = _sigmoid(gv)
            dg_ref[:, sl] = (dv * yn * sg * (1.0 + gv * (1.0 - sg))).astype(dg_ref.dtype)
            dyn = dv * gv * sg
            dy_ref[:, sl] = rinv * (dyn - yn * jnp.mean(dyn * yn, axis=-1, keepdims=True))

    spec = pl.BlockSpec((tr, bw), lambda i, g: (i, g))
    return pl.pallas_call(
        body, name=name, grid=(lx // tr, rw // bw),
        in_specs=[spec, spec, pl.BlockSpec((tr, bw), lambda i, g: (i, g_off // bw + g))],
        out_specs=[spec, spec],
        out_shape=[jax.ShapeDtypeStruct((lx, rw), F32), jax.ShapeDtypeStruct((lx, rw), BF16)],
        compiler_params=_cp(("parallel", "parallel")))(dyr, y, p)


def _merge(name, pa, pr, p, ga_off, gb_off):
    r, d = pa.shape
    cw = _tile(math.gcd(math.gcd(ga_off, gb_off), d), 1024)
    tr = _tile(r, 512, SUBLANES)

    def body(pa_ref, pr_ref, ga_ref, gb_ref, o_ref):
        o_ref[...] = (_sigmoid(ga_ref[...].astype(F32)) * pa_ref[...].astype(F32)
                      + _sigmoid(gb_ref[...].astype(F32)) * pr_ref[...].astype(F32)).astype(o_ref.dtype)

    spec = pl.BlockSpec((tr, cw), lambda i, j: (i, j))
    return pl.pallas_call(
        body, name=name, grid=(r // tr, d // cw),
        in_specs=[spec, spec, pl.BlockSpec((tr, cw), lambda i, j: (i, ga_off // cw + j)),
                  pl.BlockSpec((tr, cw), lambda i, j: (i, gb_off // cw + j))],
        out_specs=spec, out_shape=jax.ShapeDtypeStruct((r, d), BF16),
        compiler_params=_cp(("parallel", "parallel")))(pa, pr, p, p)


def _ada_fwd(name, cond, w):
    rows, d = cond.shape
    n = w.shape[1]
    tn = _tile(n, 768)

    def body(c_ref, w_ref, o_ref, s_ref):
        cv = c_ref[...]
        sv = cv * _sigmoid(cv)
        s_ref[...] = sv
        o_ref[...] = _dot(sv.astype(BF16), w_ref[...].astype(BF16), NN)

    return pl.pallas_call(
        body, name=name, grid=(n // tn,),
        in_specs=[pl.BlockSpec((rows, d), lambda j: (0, 0)), pl.BlockSpec((d, tn), lambda j: (0, j))],
        out_specs=[pl.BlockSpec((rows, tn), lambda j: (0, j)), pl.BlockSpec((rows, d), lambda j: (0, 0))],
        out_shape=[jax.ShapeDtypeStruct((rows, n), F32), jax.ShapeDtypeStruct((rows, d), F32)],
        compiler_params=_cp(("arbitrary",)))(cond, w)


def _ada_bwd(name, s_cond, dmod, w):
    rows, d = s_cond.shape
    n = w.shape[1]
    tn = _tile(n, 768)

    def body(s_ref, dm_ref, w_ref, gw_ref, ds_ref):
        j = pl.program_id(0)

        @pl.when(j == 0)
        def _():
            ds_ref[...] = jnp.zeros(ds_ref.shape, F32)

        dmv = dm_ref[...].astype(BF16)
        gw_ref[...] = _dot(s_ref[...].astype(BF16), dmv, TN)
        ds_ref[...] += _dot(dmv, w_ref[...].astype(BF16), NT)

    return pl.pallas_call(
        body, name=name, grid=(n // tn,),
        in_specs=[pl.BlockSpec((rows, d), lambda j: (0, 0)), pl.BlockSpec((rows, tn), lambda j: (0, j)),
                  pl.BlockSpec((d, tn), lambda j: (0, j))],
        out_specs=[pl.BlockSpec((d, tn), lambda j: (0, j)), pl.BlockSpec((rows, d), lambda j: (0, 0))],
        out_shape=[jax.ShapeDtypeStruct((d, n), F32), jax.ShapeDtypeStruct((rows, d), F32)],
        compiler_params=_cp(("arbitrary",)))(s_cond, dmod, w)


def _sum_slots(name, a):
    s, r, c = a.shape

    def body(a_ref, o_ref):
        acc = a_ref[0]
        for k in range(1, s):
            acc = acc + a_ref[k]
        o_ref[...] = acc

    return pl.pallas_call(
        body, name=name, grid=(1,), in_specs=[pl.BlockSpec((s, r, c), lambda i: (0, 0, 0))],
        out_specs=pl.BlockSpec((r, c), lambda i: (0, 0)), out_shape=jax.ShapeDtypeStruct((r, c), F32),
        compiler_params=_cp(("arbitrary",)))(a)


def _cctx_grad(name, parts, c_ctx):
    s, r, d = parts.shape

    def body(p_ref, c_ref, o_ref):
        acc = p_ref[0]
        for k in range(1, s):
            acc = acc + p_ref[k]
        cv = c_ref[...]
        sg = _sigmoid(cv)
        o_ref[...] = acc[0:1, :] * sg * (1.0 + cv * (1.0 - sg))

    return pl.pallas_call(
        body, name=name, grid=(1,),
        in_specs=[pl.BlockSpec((s, r, d), lambda i: (0, 0, 0)), pl.BlockSpec((1, d), lambda i: (0, 0))],
        out_specs=pl.BlockSpec((1, d), lambda i: (0, 0)), out_shape=jax.ShapeDtypeStruct((1, d), F32),
        compiler_params=_cp(("arbitrary",)))(parts, c_ctx)


def _adamw(name, slots, w, m, v):
    s, r, c = slots.shape
    tr = _tile(r, 256 if c > 1024 else 512, SUBLANES)
    c1 = 1.0 - ADAM_B1 ** ADAM_STEP
    c2 = 1.0 - ADAM_B2 ** ADAM_STEP

    def body(s_ref, w_ref, m_ref, v_ref, g_ref, d_ref, mo_ref, vo_ref):
        g = s_ref[0].astype(F32)
        for k in range(1, s):
            g = g + s_ref[k].astype(F32)
        mn = ADAM_B1 * m_ref[...] + (1.0 - ADAM_B1) * g
        vn = ADAM_B2 * v_ref[...] + (1.0 - ADAM_B2) * (g * g)
        m_hat = mn / c1
        v_hat = vn / c2
        g_ref[...] = g
        mo_ref[...] = mn
        vo_ref[...] = vn
        d_ref[...] = -ADAM_LR * (m_hat / (jnp.sqrt(v_hat) + ADAM_EPS) + ADAM_WD * w_ref[...])

    spec = pl.BlockSpec((tr, c), lambda i: (i, 0))
    shp = jax.ShapeDtypeStruct((r, c), F32)
    return pl.pallas_call(
        body, name=name, grid=(r // tr,),
        in_specs=[pl.BlockSpec((s, tr, c), lambda i: (0, i, 0)), spec, spec, spec],
        out_specs=[spec] * 4, out_shape=[shp] * 4, compiler_params=_cp(("parallel",)))(slots, w, m, v)


def _coords():
    return lax.axis_index("x"), lax.axis_index("y"), lax.axis_index("c")


def _flip(coord, bit):
    return 1 - coord if bit else coord


def _all_gather_small(name, blk):
    r, ccols = blk.shape

    def body(x_ref, out_ref, send_sems, recv_sems, local_sem):
        x, y, c = _coords()
        me, sibling = (x, y, c), (x, y, 1 - c)
        chips = [(1 - x, y), (x, 1 - y), (1 - x, 1 - y)]

        def slot(px, py, pc):
            return out_ref.at[4 * px + 2 * py + pc]

        def copy(k, block, to, src=None):
            return pltpu.make_async_remote_copy(
                src_ref=slot(*block) if src is None else src, dst_ref=slot(*block),
                send_sem=send_sems.at[k], recv_sem=recv_sems.at[k], device_id=to, device_id_type=MESH)

        mine = pltpu.make_async_copy(x_ref, slot(*me), local_sem)
        mine.start()
        first = [copy(0, me, sibling, src=x_ref)]
        first += [copy(1 + j, me, (*chip, c), src=x_ref) for j, chip in enumerate(chips)]
        for cp in first:
            cp.start()
        passed = [copy(4 + j, (*chip, c), sibling) for j, chip in enumerate(chips)]
        for j, chip in enumerate(chips):
            copy(1 + j, (*chip, c), me).wait_recv()
            passed[j].start()
        copy(0, sibling, me).wait_recv()
        for j, chip in enumerate(chips):
            copy(4 + j, (*chip, 1 - c), me).wait_recv()
        for cp in first + passed:
            cp.wait_send()
        mine.wait()

    return pl.pallas_call(
        body, name=name, out_shape=jax.ShapeDtypeStruct((N_DEV, r, ccols), blk.dtype),
        in_specs=[pl.BlockSpec(memory_space=pltpu.VMEM)], out_specs=pl.BlockSpec(memory_space=pltpu.VMEM),
        scratch_shapes=[pltpu.SemaphoreType.DMA((7,)), pltpu.SemaphoreType.DMA((7,)), pltpu.SemaphoreType.DMA],
    )(blk)


def _comm_semaphores(n):
    return [pltpu.SemaphoreType.DMA((7 * n,)), pltpu.SemaphoreType.DMA((7 * n,)), pltpu.SemaphoreType.DMA((n,))]


def _gather_plan(ins, outs, send_sems, recv_sems, local_sems):
    n = len(ins)
    x, y, c = _coords()
    me, sibling = (x, y, c), (x, y, 1 - c)
    chips = [(1 - x, y), (x, 1 - y), (1 - x, 1 - y)]

    def slot(a, px, py, pc):
        return outs[a].at[4 * px + 2 * py + pc]

    def copy(a, k, block, to, src=None):
        return pltpu.make_async_remote_copy(
            src_ref=slot(a, *block) if src is None else src, dst_ref=slot(a, *block),
            send_sem=send_sems.at[7 * a + k], recv_sem=recv_sems.at[7 * a + k], device_id=to, device_id_type=MESH)

    def local(a):
        return pltpu.make_async_copy(ins[a], slot(a, *me), local_sems.at[a])

    def first(a):
        return [copy(a, 0, me, sibling, src=ins[a])] + [copy(a, 1 + j, me, (*chip, c), src=ins[a])
                                                        for j, chip in enumerate(chips)]

    def passed(a, j):
        return copy(a, 4 + j, (*chips[j], c), sibling)

    def start():
        for a in range(n):
            local(a).start()
            for cp in first(a):
                cp.start()

    def finish():
        for a in range(n):
            for j, chip in enumerate(chips):
                copy(a, 1 + j, (*chip, c), me).wait_recv()
                passed(a, j).start()
        for a in range(n):
            copy(a, 0, sibling, me).wait_recv()
            for j, chip in enumerate(chips):
                copy(a, 4 + j, (*chip, 1 - c), me).wait_recv()
        for a in range(n):
            for cp in first(a) + [passed(a, j) for j in range(3)]:
                cp.wait_send()
            local(a).wait()

    return start, finish


def _exchange_plan(ins, outs, send_sems, recv_sems, local_sems):
    n = len(ins)
    x, y, c = _coords()
    my_idx = 4 * x + 2 * y + c

    def local(a):
        return pltpu.make_async_copy(ins[a].at[my_idx], outs[a].at[my_idx], local_sems.at[a])

    def pair(a, rel):
        px, py, pc = _flip(x, rel & 4), _flip(y, rel & 2), _flip(c, rel & 1)
        peer_idx = 4 * px + 2 * py + pc
        sems = dict(send_sem=send_sems.at[7 * a + rel - 1], recv_sem=recv_sems.at[7 * a + rel - 1],
                    device_id=(px, py, pc), device_id_type=MESH)
        send = pltpu.make_async_remote_copy(src_ref=ins[a].at[peer_idx], dst_ref=outs[a].at[my_idx], **sems)
        recv = pltpu.make_async_remote_copy(src_ref=ins[a].at[my_idx], dst_ref=outs[a].at[peer_idx], **sems)
        return send, recv

    def start():
        for a in range(n):
            local(a).start()
            for rel in range(1, N_DEV):
                pair(a, rel)[0].start()

    def finish():
        for a in range(n):
            for rel in range(1, N_DEV):
                send, recv = pair(a, rel)
                recv.wait_recv()
                send.wait_send()
            local(a).wait()

    return start, finish


def _comm_plan(kind, ins, outs, send_sems, recv_sems, local_sems):
    plan = {"gather": _gather_plan, "exchange": _exchange_plan}[kind]
    return plan(ins, outs, send_sems, recv_sems, local_sems)


def _all_gather_big(name, shards):
    n = len(shards)

    def body(*refs):
        start, finish = _gather_plan(refs[:n], refs[n:2 * n], *refs[2 * n:])
        start()
        finish()

    any_spec = pl.BlockSpec(memory_space=pl.ANY)
    return pl.pallas_call(
        body, name=name, out_shape=[jax.ShapeDtypeStruct((N_DEV,) + s.shape, s.dtype) for s in shards],
        in_specs=[any_spec] * n, out_specs=[any_spec] * n, scratch_shapes=_comm_semaphores(n))(*shards)


def _rope_tables(lx, lc):
    rows = lx // GRID_W
    row = jnp.repeat(jnp.arange(rows, dtype=F32), GRID_W)
    col = jnp.tile(jnp.arange(GRID_W, dtype=F32), rows)
    half = HEAD_DIM // 2
    inv_freq = ROPE_THETA ** (-jnp.arange(0, half, 2, dtype=F32) / half)
    ang = jnp.concatenate([row[:, None] * inv_freq, col[:, None] * inv_freq], axis=-1)
    cos, sin = jnp.cos(ang), jnp.sin(ang)
    cosf = jnp.repeat(cos, 2, axis=-1)
    sinf = jnp.stack([-sin, sin], axis=-1).reshape(lx, HEAD_DIM)
    cosf = jnp.concatenate([cosf, jnp.ones((lc, HEAD_DIM), F32)], axis=0)
    sinf = jnp.concatenate([sinf, jnp.zeros((lc, HEAD_DIM), F32)], axis=0)
    return cosf, sinf


def _cols_full(g):
    return jnp.transpose(g, (1, 0, 2)).reshape(g.shape[1], N_DEV * g.shape[2])


def _cols_split(w):
    k, n = w.shape
    return jnp.transpose(w.reshape(k, N_DEV, n // N_DEV), (1, 0, 2))


def _pad_rows(a, rows):
    return jnp.pad(a, ((0, rows - a.shape[0]), (0, 0)))


def kernel(x, c, ctx, c_ctx, w_ada, b_ada, ffn1_w_in, ffn1_w_out, mix_w_in, attn_q_gain, attn_k_gain, ret_decay_logit, w_proj_attn, w_proj_ret, mix_w_out, ffn2_w_in, ffn2_w_out, final_norm, loss_target, m_c_ctx, m_w_ada, m_b_ada, m_ffn1_w_in, m_ffn1_w_out, m_mix_w_in, m_attn_q_gain, m_attn_k_gain, m_ret_decay_logit, m_w_proj_attn, m_w_proj_ret, m_mix_w_out, m_ffn2_w_in, m_ffn2_w_out, m_final_norm, v_c_ctx, v_w_ada, v_b_ada, v_ffn1_w_in, v_ffn1_w_out, v_mix_w_in, v_attn_q_gain, v_attn_k_gain, v_ret_decay_logit, v_w_proj_attn, v_w_proj_ret, v_mix_w_out, v_ffn2_w_in, v_ffn2_w_out, v_final_norm):
    lx, d = x.shape[1], x.shape[2]
    lc = ctx.shape[1]
    t = lx + lc
    aw, rw = w_proj_attn.shape[1], w_proj_ret.shape[1]
    pw = mix_w_in.shape[2] * N_DEV
    kw = (pw - aw - 4 * rw - 2 * d) // 2
    groups = aw // kw
    heads_r = rw // HEAD_DIM
    ka_off, va_off = aw, aw + kw
    qr_off = aw + 2 * kw
    kr_off, vr_off, gr_off = qr_off + rw, qr_off + 2 * rw, qr_off + 3 * rw
    ga_off, gb_off = qr_off + 4 * rw, qr_off + 4 * rw + d
    xi, yi, ci = _coords()
    me = 4 * xi + 2 * yi + ci

    col_names = ["ffn1_w_in", "mix_w_in", "w_proj_attn", "w_proj_ret", "ffn2_w_in"]
    row_names = ["ffn1_w_out", "mix_w_out", "ffn2_w_out"]
    shard = dict(ffn1_w_in=ffn1_w_in[0], mix_w_in=mix_w_in[0], w_proj_attn=w_proj_attn[0], w_proj_ret=w_proj_ret[0],
                 ffn2_w_in=ffn2_w_in[0], ffn1_w_out=ffn1_w_out[0], mix_w_out=mix_w_out[0], ffn2_w_out=ffn2_w_out[0])
    names = col_names + row_names
    bf_shard = {k: shard[k].astype(BF16) for k in names}
    full, landed = {}, {}

    def gather_of(keys):
        return "gather", [bf_shard[k] for k in keys]

    def keep_full(keys, gathered):
        for k, g in zip(keys, gathered):
            full[k] = _cols_full(g) if k in col_names else g.reshape(N_DEV * g.shape[1], g.shape[2])

    def exchange_of(grads):
        return "exchange", [_cols_split(g) if k in col_names else g.reshape(N_DEV, g.shape[0] // N_DEV, g.shape[1])
                            for k, g in grads.items()]

    def keep_landed(grads, got):
        landed.update(zip(grads.keys(), got))

    keep_full(["ffn1_w_in"], _all_gather_big("gather_first", [bf_shard["ffn1_w_in"]]))

    c_all = _all_gather_small("gather_cond", _pad_rows(c, SUBLANES))[:, 0, :]
    cond = _pad_rows(jnp.concatenate([c_all, c_ctx[None, :]], axis=0), 2 * SUBLANES)
    ada_part, s_cond = _ada_fwd("ada_fwd", cond, w_ada[0])
    ada_all = _all_gather_small("gather_ada", ada_part)
    mod_all = jnp.transpose(ada_all, (1, 0, 2)).reshape(2 * SUBLANES, N_MOD * d) + b_ada
    mod_x = lax.dynamic_index_in_dim(mod_all, me, axis=0, keepdims=False).reshape(N_MOD, d)
    mod_c = mod_all[N_DEV].reshape(N_MOD, d)
    mods = jnp.stack([mod_c, mod_x], axis=0)[:, :, None, :]
    sh1, sc1, g1, sh2, sc2, g2, sh3, sc3, g3 = [mods[:, k] for k in range(N_MOD)]

    h0 = jnp.concatenate([x[0], ctx[0]], axis=0)
    u1 = _rmsmod("ffn1_norm", h0, sc1, sh1, lx)
    keys = ["ffn1_w_out", "mix_w_in"]
    (z1, s1), got = _ffn_in("ffn1_in", u1, full["ffn1_w_in"], comm=gather_of(keys))
    keep_full(keys, got)
    keys = ["w_proj_attn", "w_proj_ret", "mix_w_out", "ffn2_w_out"]
    (h1, f1), got = _mm_residual("ffn1_out", s1, full["ffn1_w_out"], h0, g1, 0.5, lx, comm=gather_of(keys))
    keep_full(keys, got)

    u2 = _rmsmod("mix_norm", h1, sc2, sh2, lx)
    keys = ["ffn2_w_in"]
    p, got = _mm_nn("mix_in", u2, full["mix_w_in"], BF16, tm_pref=M_TILE_BIG, tn_pref=512, comm=gather_of(keys))
    keep_full(keys, got)
    cosf, sinf = _rope_tables(lx, lc)
    q_rot, k_rot = _qk_prep("qk_prep", p, cosf, sinf, attn_q_gain, attn_k_gain, aw, kw)
    ya, lse = _attn_fwd("attn_fwd", q_rot, k_rot, p, va_off, lx, groups)

    decay = ret_decay_logit[0].astype(F32)
    log_gamma = jax.nn.log_sigmoid(decay)
    r_offs = (qr_off, kr_off, vr_off, rw)
    tab = [_ret_tables(log_gamma[k], k) for k in range(2)]
    y_f, st_f = _ret_fwd("ret_fwd_a", p, r_offs, tab[0], 0, lx, None)
    y_r, st_b = _ret_fwd("ret_fwd_b", p, r_offs, tab[1], 1, lx, y_f)
    yr = _ret_out("ret_out", y_r, p, gr_off, lx)

    pa = _mm_nn("proj_attn", ya, full["w_proj_attn"], BF16)
    pr = _mm_nn("proj_ret", yr, full["w_proj_ret"], BF16)
    mg = _merge("merge", pa, pr, p, ga_off, gb_off)
    h2, o2 = _mm_residual("mix_out", mg, full["mix_w_out"], h1, g2, 1.0, lx)

    u3 = _rmsmod("ffn2_norm", h2, sc3, sh3, lx)
    z3, s3 = _ffn_in("ffn2_in", u3, full["ffn2_w_in"])
    h3, f3 = _mm_residual("ffn2_out", s3, full["ffn2_w_out"], h2, g3, 0.5, lx)
    dh3, d_fn, loss_tile = _final_loss("final_loss", h3, final_norm[None, :], loss_target[0])

    df3, dg3 = _gate_bwd("ffn2_gate_bwd", dh3, f3, g3, 0.5, lx)
    dz3 = _ffn_out_bwd("ffn2_out_bwd", df3, full["ffn2_w_out"], z3)
    grads = {"ffn2_w_out": _mm_tn("ffn2_out_dw", s3, df3, BF16, tm_pref=M_TILE_BIG)}
    g_ffn2_w_in, got = _ffn_in_bwd_w("ffn2_in_dw", u3, dz3, comm=exchange_of(grads))
    keep_landed(grads, got)
    grads = {"ffn2_w_in": g_ffn2_w_in}
    du3, got = _ffn_in_bwd_x("ffn2_in_bwd", dz3, full["ffn2_w_in"], comm=exchange_of(grads))
    keep_landed(grads, got)
    dh2, dsc3, dsh3 = _rmsmod_bwd("ffn2_norm_bwd", du3, h2, sc3, dh3, lx, lx)

    do2, dg2 = _gate_bwd("mix_gate_bwd", dh2, o2, g2, 1.0, lx)
    dpa, dpr, dga, dgb = _merge_bwd("merge_bwd", do2, full["mix_w_out"], pa, pr, p, ga_off, gb_off)
    dya = _mm_nt("proj_attn_bwd", dpa, full["w_proj_attn"], BF16)
    dyr = _mm_nt("proj_ret_bwd", dpr, full["w_proj_ret"], F32)
    mix_grads = {"mix_w_out": _mm_tn("mix_out_dw", mg, do2, BF16),
                 "w_proj_attn": _mm_tn("proj_attn_dw", ya, dpa, BF16),
                 "w_proj_ret": _mm_tn("proj_ret_dw", yr, dpr, BF16)}

    dy_ret, dgr = _ret_out_bwd("ret_out_bwd", dyr, y_r, p, gr_off, lx)
    dqr, dkr, dvr, dl_f = _ret_bwd("ret_bwd_a", p, r_offs, tab[0], st_f, dy_ret, 0, lx, None)
    dqr, dkr, dvr, dl_b = _ret_bwd("ret_bwd_b", p, r_offs, tab[1], st_b, dy_ret, 1, lx, (dqr, dkr, dvr))
    d_lam = jnp.stack([dl_f[:, 0, 0], dl_b[:, 0, 0]], axis=0)
    d_decay = d_lam * jax.nn.sigmoid(-decay)

    delta = _attn_delta("attn_delta", ya, dya)
    dq_rot, dk_rot, dva = _attn_bwd("attn_bwd", q_rot, k_rot, p, va_off, dya, lse, delta, lx, groups)
    dqa, dka, d_qg, d_kg = _qk_prep_bwd("qk_prep_bwd", dq_rot, dk_rot, p, cosf, sinf, attn_q_gain, attn_k_gain, aw, kw)

    dp = jnp.concatenate([dqa, dka, dva, dqr.astype(BF16), dkr.astype(BF16), dvr.astype(BF16),
                          _pad_rows(dgr, t), _pad_rows(dga, t), _pad_rows(dgb, t)], axis=1)
    grads = {"mix_w_in": _mm_tn("mix_in_dw", u2, dp, BF16, tm_pref=M_TILE_DW, tn_pref=512)}
    du2, got = _mm_nt("mix_in_bwd", dp, full["mix_w_in"], F32, comm=exchange_of(grads))
    keep_landed(grads, got)
    dh1, dsc2, dsh2 = _rmsmod_bwd("mix_norm_bwd", du2, h1, sc2, dh2, lx, t)

    df1, dg1 = _gate_bwd("ffn1_gate_bwd", dh1, f1, g1, 0.5, lx)
    dz1, got = _ffn_out_bwd("ffn1_out_bwd", df1, full["ffn1_w_out"], z1, comm=exchange_of(mix_grads))
    keep_landed(mix_grads, got)
    grads = {"ffn1_w_out": _mm_tn("ffn1_out_dw", s1, df1, BF16, tm_pref=M_TILE_BIG)}
    g_ffn1_w_in, got = _ffn_in_bwd_w("ffn1_in_dw", u1, dz1, comm=exchange_of(grads))
    keep_landed(grads, got)
    grads = {"ffn1_w_in": g_ffn1_w_in}
    du1, got = _ffn_in_bwd_x("ffn1_in_bwd", dz1, full["ffn1_w_in"], comm=exchange_of(grads))
    keep_landed(grads, got)
    grad_x, dsc1, dsh1 = _rmsmod_bwd("ffn1_norm_bwd", du1, h0, sc1, dh1, lx, lx)

    zero = jnp.zeros((1, d), F32)
    dmod_c = jnp.concatenate([dsh1[0], dsc1[0], dg1[0], dsh2[0], dsc2[0], zero, zero, zero, zero], axis=0)
    dmod_x = jnp.concatenate([dsh1[1], dsc1[1], dg1[1], dsh2[1], dsc2[1], dg2[1], dsh3[1], dsc3[1], dg3[1]], axis=0)
    misc = jnp.concatenate([d_qg[0], d_kg[0], d_decay.reshape(-1), loss_tile[0, 0:1]])
    misc = jnp.pad(misc, (0, d - misc.shape[0]))[None, :]
    n_small = 3 * SUBLANES
    small = _pad_rows(jnp.concatenate([dmod_c, dmod_x, d_fn, misc], axis=0), n_small)
    small_all = _all_gather_small("gather_small", small)
    small_sum = _sum_slots("sum_small", small_all)
    dmod_c_sum, dmod_x_sum = small_sum[0:N_MOD], small_sum[N_MOD:2 * N_MOD]
    g_final_norm = small_sum[2 * N_MOD]
    misc_sum = small_sum[2 * N_MOD + 1]
    g_qg = misc_sum[0:HEAD_DIM][None, :]
    g_kg = misc_sum[HEAD_DIM:2 * HEAD_DIM][None, :]
    g_decay = misc_sum[2 * HEAD_DIM:2 * HEAD_DIM + 2 * heads_r].reshape(1, 2, heads_r)
    loss = misc_sum[2 * HEAD_DIM + 2 * heads_r]
    g_b_ada = (dmod_x_sum + dmod_c_sum).reshape(1, N_MOD * d)

    n_ada = w_ada.shape[2]
    dmod_rows = jnp.concatenate([small_all[:, N_MOD:2 * N_MOD, :].reshape(N_DEV, N_MOD * d),
                                 dmod_c_sum.reshape(1, N_MOD * d)], axis=0)
    dmod_mine = _pad_rows(lax.dynamic_slice_in_dim(dmod_rows, me * n_ada, n_ada, axis=1), 2 * SUBLANES)
    g_w_ada, ds_cond = _ada_bwd("ada_bwd", s_cond, dmod_mine, w_ada[0])
    cctx_parts = _all_gather_small("gather_cctx", ds_cond[N_DEV:N_DEV + SUBLANES])
    g_c_ctx = _cctx_grad("cctx_grad", cctx_parts, c_ctx[None, :])[0]

    mom = dict(ffn1_w_in=(m_ffn1_w_in, v_ffn1_w_in), mix_w_in=(m_mix_w_in, v_mix_w_in),
               w_proj_attn=(m_w_proj_attn, v_w_proj_attn), w_proj_ret=(m_w_proj_ret, v_w_proj_ret),
               ffn2_w_in=(m_ffn2_w_in, v_ffn2_w_in), ffn1_w_out=(m_ffn1_w_out, v_ffn1_w_out),
               mix_w_out=(m_mix_w_out, v_mix_w_out), ffn2_w_out=(m_ffn2_w_out, v_ffn2_w_out))
    res = {}
    for k in names:
        res[k] = _adamw("adamw_" + k, landed[k], shard[k], mom[k][0][0], mom[k][1][0])
    res["w_ada"] = _adamw("adamw_w_ada", g_w_ada[None], w_ada[0], m_w_ada[0], v_w_ada[0])

    def pack(cc, ba, qg, kg, dec, fn):
        misc_row = jnp.concatenate([qg.reshape(-1), kg.reshape(-1), dec.reshape(-1)])
        misc_row = jnp.pad(misc_row, (0, d - misc_row.shape[0]))[None, :]
        return _pad_rows(jnp.concatenate([cc.reshape(1, d), ba.reshape(N_MOD, d), fn.reshape(1, d), misc_row], axis=0),
                         2 * SUBLANES)

    sg, sd, sm, sv = _adamw(
        "adamw_small", pack(g_c_ctx, g_b_ada, g_qg, g_kg, g_decay, g_final_norm)[None],
        pack(c_ctx, b_ada, attn_q_gain, attn_k_gain, ret_decay_logit, final_norm),
        pack(m_c_ctx, m_b_ada, m_attn_q_gain, m_attn_k_gain, m_ret_decay_logit, m_final_norm),
        pack(v_c_ctx, v_b_ada, v_attn_q_gain, v_attn_k_gain, v_ret_decay_logit, v_final_norm))

    def unpack(a):
        misc_row = a[N_MOD + 2]
        return dict(c_ctx=a[0], b_ada=a[1:1 + N_MOD].reshape(1, N_MOD * d), final_norm=a[N_MOD + 1],
                    attn_q_gain=misc_row[0:HEAD_DIM][None, :], attn_k_gain=misc_row[HEAD_DIM:2 * HEAD_DIM][None, :],
                    ret_decay_logit=misc_row[2 * HEAD_DIM:2 * HEAD_DIM + 2 * heads_r].reshape(1, 2, heads_r))

    small_out = [unpack(a) for a in (sg, sd, sm, sv)]
    order = ["c_ctx", "w_ada", "b_ada", "ffn1_w_in", "ffn1_w_out", "mix_w_in", "attn_q_gain", "attn_k_gain",
             "ret_decay_logit", "w_proj_attn", "w_proj_ret", "mix_w_out", "ffn2_w_in", "ffn2_w_out", "final_norm"]
    outs = [loss, grad_x[None]]
    for which in range(4):
        for k in order:
            outs.append(res[k][which][None] if k in res else small_out[which][k])
    return tuple(outs)
```

```python
import math

import jax
import jax.numpy as jnp
from jax import lax
from jax.experimental import pallas as pl
from jax.experimental.pallas import tpu as pltpu

F32 = jnp.float32
BF16 = jnp.bfloat16
MESH = pl.DeviceIdType.MESH

N_DEV = 8
HEAD_DIM = 128
GRID_W = 64
ROPE_THETA = 10000.0
NORM_EPS = 1e-6
RET_CHUNK = 128
N_MOD = 9
LANES = 128
SUBLANES = 8
V7X_VMEM_BYTES = 64 * 1024 * 1024
VMEM_LIMIT = V7X_VMEM_BYTES - 8 * 1024 * 1024
K_TILE = 2560
M_TILE_BIG = 1408
M_TILE_DW = 2048
LOG2E = 1.4426950408889634
LN2 = 0.6931471805599453

ADAM_LR = 0.001
ADAM_B1 = 0.9
ADAM_B2 = 0.999
ADAM_EPS = 1e-08
ADAM_WD = 0.01
ADAM_STEP = 10

NN = (((1,), (0,)), ((), ()))
NT = (((1,), (1,)), ((), ()))
TN = (((0,), (0,)), ((), ()))


def _tile(n, pref, align=LANES):
    best = None
    t = align
    while t <= min(n, pref):
        if n % t == 0:
            best = t
        t += align
    return n if best is None else best


def _cp(sem):
    return pltpu.CompilerParams(dimension_semantics=sem, vmem_limit_bytes=VMEM_LIMIT)


def _sigmoid(v):
    return 0.5 * jnp.tanh(0.5 * v) + 0.5


def _dot(a, b, dims):
    return lax.dot_general(a, b, dims, preferred_element_type=F32)


def _mm(name, a, a_spec, b_list, dims, grid, out_shapes, out_specs, acc_shape, epi, extras=(), comm=None):
    nb, ne, no = len(b_list), len(extras), len(out_shapes)
    nk = grid[2]
    kind, c_arrays, c_wide = comm if comm is not None else (None, [], [])
    ncm = len(c_arrays)

    def body(*refs):
        a_ref = refs[0]
        b_refs = refs[1:1 + nb]
        e_refs = refs[1 + nb:1 + nb + ne]
        pos = 1 + nb + ne
        c_ins = refs[pos:pos + ncm]
        o_refs = refs[pos + ncm:pos + ncm + no]
        c_outs = refs[pos + ncm + no:pos + 2 * ncm + no]
        scratch = refs[pos + 2 * ncm + no:]
        accs = scratch[:0 if nk == 1 else nb]
        ids = [pl.program_id(axis) for axis in range(3)]
        if ncm:
            start_comm, finish_comm = _comm_plan(kind, c_ins, c_outs, c_wide, *scratch[len(accs):])
            pl.when(jnp.logical_and(jnp.logical_and(ids[0] == 0, ids[1] == 0), ids[2] == 0))(start_comm)

        def finish(tiles):
            vals = epi(tiles, e_refs)
            for o_ref, v in zip(o_refs, vals):
                if isinstance(v, tuple):
                    for idx, part in enumerate(v):
                        o_ref[idx] = part.astype(o_ref.dtype)
                else:
                    o_ref[...] = v.astype(o_ref.dtype)

        if nk == 1:
            finish([_dot(a_ref[...], b_ref[...], dims) for b_ref in b_refs])
        else:
            @pl.when(ids[2] == 0)
            def _():
                for acc in accs:
                    acc[...] = jnp.zeros(acc.shape, F32)

            av = a_ref[...]
            for b_ref, acc in zip(b_refs, accs):
                acc[...] += _dot(av, b_ref[...], dims)

            @pl.when(ids[2] == nk - 1)
            def _():
                finish([acc[...] for acc in accs])

        if ncm:
            pl.when(jnp.logical_and(jnp.logical_and(ids[0] == grid[0] - 1, ids[1] == grid[1] - 1),
                                    ids[2] == nk - 1))(finish_comm)

    any_spec = pl.BlockSpec(memory_space=pl.ANY)
    c_shapes = [_comm_out_shape(kind, s, w) for s, w in zip(c_arrays, c_wide)]
    scratch_shapes = [] if nk == 1 else [pltpu.VMEM(acc_shape, F32)] * nb
    if ncm:
        scratch_shapes = scratch_shapes + _comm_semaphores(ncm)
    semantics = ("arbitrary",) * 3 if ncm else ("parallel", "parallel", "arbitrary")
    outs = pl.pallas_call(
        body, name=name, grid=grid,
        in_specs=[a_spec] + [s for _, s in b_list] + [s for _, s in extras] + [any_spec] * ncm,
        out_specs=list(out_specs) + [any_spec] * ncm, out_shape=list(out_shapes) + c_shapes,
        scratch_shapes=scratch_shapes, compiler_params=_cp(semantics),
    )(a, *[b for b, _ in b_list], *[e for e, _ in extras], *c_arrays)
    main = outs[0] if no == 1 else tuple(outs[:no])
    return main if comm is None else (main, list(outs[no:]))


def _plain(accs, _):
    return (accs[0],)


def _mm_nn(name, a, b, out_dtype, tm_pref=1024, tn_pref=1024, tk_pref=K_TILE, comm=None):
    m, k = a.shape
    n = b.shape[1]
    tm, tn, tk = _tile(m, tm_pref), _tile(n, tn_pref), _tile(k, tk_pref)
    return _mm(name, a, pl.BlockSpec((tm, tk), lambda i, j, kk: (i, kk)),
               [(b, pl.BlockSpec((tk, tn), lambda i, j, kk: (kk, j)))], NN, (m // tm, n // tn, k // tk),
               [jax.ShapeDtypeStruct((m, n), out_dtype)], [pl.BlockSpec((tm, tn), lambda i, j, kk: (i, j))],
               (tm, tn), _plain, comm=comm)


def _mm_nt(name, a, b, out_dtype, tm_pref=1024, tn_pref=1024, tk_pref=K_TILE, comm=None):
    m, k = a.shape
    n = b.shape[0]
    tm, tn, tk = _tile(m, tm_pref), _tile(n, tn_pref), _tile(k, tk_pref)
    return _mm(name, a, pl.BlockSpec((tm, tk), lambda i, j, kk: (i, kk)),
               [(b, pl.BlockSpec((tn, tk), lambda i, j, kk: (j, kk)))], NT, (m // tm, n // tn, k // tk),
               [jax.ShapeDtypeStruct((m, n), out_dtype)], [pl.BlockSpec((tm, tn), lambda i, j, kk: (i, j))],
               (tm, tn), _plain, comm=comm)


def _mm_tn(name, a, b, out_dtype, rows=None, tm_pref=1024, tn_pref=1024, tk_pref=K_TILE, comm=None):
    k = a.shape[0] if rows is None else rows
    m, n = a.shape[1], b.shape[1]
    tm, tn, tk = _tile(m, tm_pref), _tile(n, tn_pref), _tile(k, tk_pref)
    return _mm(name, a, pl.BlockSpec((tk, tm), lambda i, j, kk: (kk, i)),
               [(b, pl.BlockSpec((tk, tn), lambda i, j, kk: (kk, j)))], TN, (m // tm, n // tn, k // tk),
               [jax.ShapeDtypeStruct((m, n), out_dtype)], [pl.BlockSpec((tm, tn), lambda i, j, kk: (i, j))],
               (tm, tn), _plain, comm=comm)


def _ffn_in(name, u, w_in, comm=None):
    r, d = u.shape
    f = w_in.shape[1] // 2
    tm, tn, tk = _tile(r, 1024), _tile(f, 512), _tile(d, K_TILE)
    nf = f // tn

    def epi(accs, _):
        za, zb = accs
        s = za * _sigmoid(za) * zb
        return (za, zb), s

    return _mm(name, u, pl.BlockSpec((tm, tk), lambda i, j, kk: (i, kk)),
               [(w_in, pl.BlockSpec((tk, tn), lambda i, j, kk: (kk, j))),
                (w_in, pl.BlockSpec((tk, tn), lambda i, j, kk: (kk, j + nf)))],
               NN, (r // tm, nf, d // tk),
               [jax.ShapeDtypeStruct((2, r, f), BF16), jax.ShapeDtypeStruct((r, f), BF16)],
               [pl.BlockSpec((2, tm, tn), lambda i, j, kk: (0, i, j)), pl.BlockSpec((tm, tn), lambda i, j, kk: (i, j))],
               (tm, tn), epi, comm=comm)


def _mm_residual(name, a, w, res, gate, gate_scale, lx, comm=None):
    r, k = a.shape
    n = w.shape[1]
    tm, tn, tk = _tile(r, 1024), _tile(n, 1024), _tile(k, K_TILE)

    def epi(accs, e_refs):
        res_ref, g_ref = e_refs
        rows = pl.program_id(0) * tm + lax.broadcasted_iota(jnp.int32, (tm, 1), 0)
        g = jnp.where(rows < lx, g_ref[1], g_ref[0])
        return res_ref[...] + gate_scale * g * accs[0], accs[0]

    return _mm(name, a, pl.BlockSpec((tm, tk), lambda i, j, kk: (i, kk)),
               [(w, pl.BlockSpec((tk, tn), lambda i, j, kk: (kk, j)))], NN, (r // tm, n // tn, k // tk),
               [jax.ShapeDtypeStruct((r, n), F32), jax.ShapeDtypeStruct((r, n), BF16)],
               [pl.BlockSpec((tm, tn), lambda i, j, kk: (i, j))] * 2, (tm, tn), epi,
               extras=[(res, pl.BlockSpec((tm, tn), lambda i, j, kk: (i, j))),
                       (gate, pl.BlockSpec((2, 1, tn), lambda i, j, kk: (0, 0, j)))], comm=comm)


def _ffn_out_bwd(name, df, w_out, z, comm=None):
    r, d = df.shape
    f = w_out.shape[0]
    tm, tn, tk = _tile(r, M_TILE_BIG), _tile(f, 512), _tile(d, K_TILE)

    def epi(accs, e_refs):
        ds = accs[0]
        za = e_refs[0][0].astype(F32)
        zb = e_refs[0][1].astype(F32)
        sg = _sigmoid(za)
        da = ds * zb * sg * (1.0 + za * (1.0 - sg))
        db = ds * za * sg
        return ((da, db),)

    zspec = pl.BlockSpec((2, tm, tn), lambda i, j, kk: (0, i, j))
    return _mm(name, df, pl.BlockSpec((tm, tk), lambda i, j, kk: (i, kk)),
               [(w_out, pl.BlockSpec((tn, tk), lambda i, j, kk: (j, kk)))], NT, (r // tm, f // tn, d // tk),
               [jax.ShapeDtypeStruct((2, r, f), BF16)], [zspec], (tm, tn), epi, extras=[(z, zspec)], comm=comm)


def _ffn_in_bwd_x(name, dz, w_in, comm=None):
    _, r, f = dz.shape
    d = w_in.shape[0]
    tm, tn, tk = _tile(r, 1024), _tile(d, 2048), _tile(f, K_TILE)
    nkf = f // tk
    return _mm(name, dz, pl.BlockSpec((None, tm, tk), lambda i, j, kk: (kk // nkf, i, kk % nkf)),
               [(w_in, pl.BlockSpec((tn, tk), lambda i, j, kk: (j, kk)))], NT, (r // tm, d // tn, 2 * nkf),
               [jax.ShapeDtypeStruct((r, d), F32)], [pl.BlockSpec((tm, tn), lambda i, j, kk: (i, j))],
               (tm, tn), _plain, comm=comm)


def _ffn_in_bwd_w(name, u, dz, comm=None):
    r, d = u.shape
    f = dz.shape[2]
    tm, tn, tk = _tile(d, M_TILE_DW), _tile(f, 512), _tile(r, K_TILE)
    nf = f // tn
    return _mm(name, u, pl.BlockSpec((tk, tm), lambda i, j, kk: (kk, i)),
               [(dz, pl.BlockSpec((None, tk, tn), lambda i, j, kk: (j // nf, kk, j % nf)))], TN,
               (d // tm, 2 * nf, r // tk),
               [jax.ShapeDtypeStruct((d, 2 * f), BF16)], [pl.BlockSpec((tm, tn), lambda i, j, kk: (i, j))],
               (tm, tn), _plain, comm=comm)


def _merge_bwd(name, dout, w_out, pa, pr, p, ga_off, gb_off):
    r, d = dout.shape
    n = w_out.shape[0]
    cw = math.gcd(math.gcd(ga_off, gb_off), n)
    tm, tn, tk = _tile(r, 1024), _tile(cw, 512), _tile(d, K_TILE)

    def epi(accs, e_refs):
        dm = accs[0]
        pa_ref, pr_ref, ga_ref, gb_ref = e_refs
        sa = _sigmoid(ga_ref[...].astype(F32))
        sb = _sigmoid(gb_ref[...].astype(F32))
        pav = pa_ref[...].astype(F32)
        prv = pr_ref[...].astype(F32)
        return dm * sa, dm * sb, dm * pav * sa * (1.0 - sa), dm * prv * sb * (1.0 - sb)

    o_spec = pl.BlockSpec((tm, tn), lambda i, j, kk: (i, j))
    return _mm(name, dout, pl.BlockSpec((tm, tk), lambda i, j, kk: (i, kk)),
               [(w_out, pl.BlockSpec((tn, tk), lambda i, j, kk: (j, kk)))], NT, (r // tm, n // tn, d // tk),
               [jax.ShapeDtypeStruct((r, n), BF16)] * 4, [o_spec] * 4, (tm, tn), epi,
               extras=[(pa, o_spec), (pr, o_spec),
                       (p, pl.BlockSpec((tm, tn), lambda i, j, kk: (i, ga_off // tn + j))),
                       (p, pl.BlockSpec((tm, tn), lambda i, j, kk: (i, gb_off // tn + j)))])


def _row_tile(r, lx, d):
    pref = 256 if d > 1024 else 512
    return _tile(math.gcd(r, lx), pref, SUBLANES)


def _rmsmod(name, h, scale, shift, lx, rows=None):
    r = h.shape[0] if rows is None else rows
    d = h.shape[1]
    tr = _row_tile(r, lx, d)
    nx = lx // tr
    cls = lambda i: (jnp.where(i < nx, 1, 0), 0, 0)

    def body(h_ref, sc_ref, sh_ref, u_ref):
        hv = h_ref[...]
        rinv = lax.rsqrt(jnp.mean(hv * hv, axis=-1, keepdims=True) + NORM_EPS)
        u_ref[...] = (hv * rinv * (1.0 + sc_ref[...]) + sh_ref[...]).astype(u_ref.dtype)

    return pl.pallas_call(
        body, name=name, grid=(r // tr,),
        in_specs=[pl.BlockSpec((tr, d), lambda i: (i, 0)), pl.BlockSpec((None, 1, d), cls), pl.BlockSpec((None, 1, d), cls)],
        out_specs=pl.BlockSpec((tr, d), lambda i: (i, 0)), out_shape=jax.ShapeDtypeStruct((r, d), BF16),
        compiler_params=_cp(("parallel",)))(h, scale, shift)


def _rmsmod_bwd(name, du, h, scale, dh_in, lx, rows_out):
    r, d = du.shape
    rin = dh_in.shape[0]
    tr = _row_tile(math.gcd(r, math.gcd(rin, rows_out)), lx, d)
    nx, nin, nout = lx // tr, rin // tr, rows_out // tr
    cls = lambda i: (jnp.where(i < nx, 1, 0), 0, 0)

    def body(du_ref, h_ref, sc_ref, dhin_ref, dh_ref, dsc_ref, dsh_ref):
        i = pl.program_id(0)
        hv = h_ref[...]
        duv = du_ref[...]
        rinv = lax.rsqrt(jnp.mean(hv * hv, axis=-1, keepdims=True) + NORM_EPS)
        nv = hv * rinv
        dn = duv * (1.0 + sc_ref[...])

        @pl.when(jnp.logical_or(i == 0, i == nx))
        def _():
            dsc_ref[...] = jnp.zeros(dsc_ref.shape, F32)
            dsh_ref[...] = jnp.zeros(dsh_ref.shape, F32)

        dsc_ref[...] += jnp.sum(duv * nv, axis=0, keepdims=True)
        dsh_ref[...] += jnp.sum(duv, axis=0, keepdims=True)

        @pl.when(i < nout)
        def _():
            dh = rinv * (dn - nv * jnp.mean(dn * nv, axis=-1, keepdims=True))
            dh_ref[...] = dh + jnp.where(i < nin, dhin_ref[...], 0.0)

    return pl.pallas_call(
        body, name=name, grid=(r // tr,),
        in_specs=[pl.BlockSpec((tr, d), lambda i: (i, 0)), pl.BlockSpec((tr, d), lambda i: (i, 0)),
                  pl.BlockSpec((None, 1, d), cls), pl.BlockSpec((tr, d), lambda i: (jnp.minimum(i, nin - 1), 0))],
        out_specs=[pl.BlockSpec((tr, d), lambda i: (jnp.minimum(i, nout - 1), 0)),
                   pl.BlockSpec((None, 1, d), cls), pl.BlockSpec((None, 1, d), cls)],
        out_shape=[jax.ShapeDtypeStruct((rows_out, d), F32), jax.ShapeDtypeStruct((2, 1, d), F32),
                   jax.ShapeDtypeStruct((2, 1, d), F32)],
        compiler_params=_cp(("arbitrary",)))(du, h, scale, dh_in)


def _gate_bwd(name, dh, f, gate, gate_scale, lx):
    r, d = dh.shape
    tr = _row_tile(r, lx, d)
    nx = lx // tr
    cls = lambda i: (jnp.where(i < nx, 1, 0), 0, 0)

    def body(dh_ref, f_ref, g_ref, df_ref, dg_ref):
        i = pl.program_id(0)
        dhv = dh_ref[...]

        @pl.when(jnp.logical_or(i == 0, i == nx))
        def _():
            dg_ref[...] = jnp.zeros(dg_ref.shape, F32)

        df_ref[...] = (gate_scale * g_ref[...] * dhv).astype(df_ref.dtype)
        dg_ref[...] += jnp.sum(gate_scale * f_ref[...].astype(F32) * dhv, axis=0, keepdims=True)

    return pl.pallas_call(
        body, name=name, grid=(r // tr,),
        in_specs=[pl.BlockSpec((tr, d), lambda i: (i, 0)), pl.BlockSpec((tr, d), lambda i: (i, 0)),
                  pl.BlockSpec((None, 1, d), cls)],
        out_specs=[pl.BlockSpec((tr, d), lambda i: (i, 0)), pl.BlockSpec((None, 1, d), cls)],
        out_shape=[jax.ShapeDtypeStruct((r, d), BF16), jax.ShapeDtypeStruct((2, 1, d), F32)],
        compiler_params=_cp(("arbitrary",)))(dh, f, gate)


def _final_loss(name, h, final_norm, target):
    r, d = h.shape
    tr = _row_tile(r, r, d)

    def body(h_ref, fn_ref, t_ref, dh_ref, dfn_ref, loss_ref):
        i = pl.program_id(0)
        hv = h_ref[...]
        rinv = lax.rsqrt(jnp.mean(hv * hv, axis=-1, keepdims=True) + NORM_EPS)
        nv = hv * rinv
        fn = fn_ref[...]
        err = nv * fn - t_ref[...]
        dy = err * (1.0 / d)

        @pl.when(i == 0)
        def _():
            dfn_ref[...] = jnp.zeros(dfn_ref.shape, F32)
            loss_ref[...] = jnp.zeros(loss_ref.shape, F32)

        loss_ref[...] += 0.5 * jnp.sum(jnp.mean(err * err, axis=-1, keepdims=True), axis=0, keepdims=True)
        dfn_ref[...] += jnp.sum(dy * nv, axis=0, keepdims=True)
        dn = dy * fn
        dh_ref[...] = rinv * (dn - nv * jnp.mean(dn * nv, axis=-1, keepdims=True))

    return pl.pallas_call(
        body, name=name, grid=(r // tr,),
        in_specs=[pl.BlockSpec((tr, d), lambda i: (i, 0)), pl.BlockSpec((1, d), lambda i: (0, 0)),
                  pl.BlockSpec((tr, d), lambda i: (i, 0))],
        out_specs=[pl.BlockSpec((tr, d), lambda i: (i, 0)), pl.BlockSpec((1, d), lambda i: (0, 0)),
                   pl.BlockSpec((SUBLANES, LANES), lambda i: (0, 0))],
        out_shape=[jax.ShapeDtypeStruct((r, d), F32), jax.ShapeDtypeStruct((1, d), F32),
                   jax.ShapeDtypeStruct((SUBLANES, LANES), F32)],
        compiler_params=_cp(("arbitrary",)))(h, final_norm, target)


def _swap_pairs(t):
    lane = lax.broadcasted_iota(jnp.int32, t.shape, 1)
    nxt = pltpu.roll(t, HEAD_DIM - 1, 1)
    prv = pltpu.roll(t, 1, 1)
    return jnp.where(lane % 2 == 0, nxt, prv)


def _qk_prep(name, p, cosf, sinf, q_gain, k_gain, aw, kw):
    t = p.shape[0]
    tr = _tile(t, 256, SUBLANES)
    q_prescale = HEAD_DIM ** -0.5 * LOG2E

    def head_fwd(v, gain, cs, sn):
        v = v.astype(F32)
        rinv = lax.rsqrt(jnp.mean(v * v, axis=-1, keepdims=True) + NORM_EPS)
        tt = v * rinv * gain
        return tt * cs + _swap_pairs(tt) * sn

    def body(q_ref, k_ref, cos_ref, sin_ref, qg_ref, kg_ref, qo_ref, ko_ref):
        cs, sn = cos_ref[...], sin_ref[...]
        for hh in range(aw // HEAD_DIM):
            sl = slice(hh * HEAD_DIM, (hh + 1) * HEAD_DIM)
            qo_ref[:, sl] = (head_fwd(q_ref[:, sl], qg_ref[...], cs, sn) * q_prescale).astype(qo_ref.dtype)
        for hh in range(kw // HEAD_DIM):
            sl = slice(hh * HEAD_DIM, (hh + 1) * HEAD_DIM)
            ko_ref[:, sl] = head_fwd(k_ref[:, sl], kg_ref[...], cs, sn).astype(ko_ref.dtype)

    assert aw % kw == 0
    row = lambda i: (i, 0)
    return pl.pallas_call(
        body, name=name, grid=(t // tr,),
        in_specs=[pl.BlockSpec((tr, aw), row), pl.BlockSpec((tr, kw), lambda i: (i, aw // kw)),
                  pl.BlockSpec((tr, HEAD_DIM), row), pl.BlockSpec((tr, HEAD_DIM), row),
                  pl.BlockSpec((1, HEAD_DIM), lambda i: (0, 0)), pl.BlockSpec((1, HEAD_DIM), lambda i: (0, 0))],
        out_specs=[pl.BlockSpec((tr, aw), row), pl.BlockSpec((tr, kw), row)],
        out_shape=[jax.ShapeDtypeStruct((t, aw), BF16), jax.ShapeDtypeStruct((t, kw), BF16)],
        compiler_params=_cp(("parallel",)))(p, p, cosf, sinf, q_gain, k_gain)


def _qk_prep_bwd(name, dq_rot, dk_rot, p, cosf, sinf, q_gain, k_gain, aw, kw):
    t = p.shape[0]
    lq = dq_rot.shape[0]
    tr = _tile(math.gcd(t, lq), 256, SUBLANES)
    nq = lq // tr

    def head_bwd(dout, v, gain, cs, sn):
        v = v.astype(F32)
        rinv = lax.rsqrt(jnp.mean(v * v, axis=-1, keepdims=True) + NORM_EPS)
        vn = v * rinv
        dt = dout * cs - _swap_pairs(dout) * sn
        dvn = dt * gain
        dv = rinv * (dvn - vn * jnp.mean(dvn * vn, axis=-1, keepdims=True))
        return dv, jnp.sum(dt * vn, axis=0, keepdims=True)

    def body(dq_ref, dk_ref, q_ref, k_ref, cos_ref, sin_ref, qg_ref, kg_ref, dqo_ref, dko_ref, dqg_ref, dkg_ref):
        i = pl.program_id(0)
        cs, sn = cos_ref[...], sin_ref[...]

        @pl.when(i == 0)
        def _():
            dqg_ref[...] = jnp.zeros(dqg_ref.shape, F32)
            dkg_ref[...] = jnp.zeros(dkg_ref.shape, F32)

        has_q = i < nq
        for hh in range(aw // HEAD_DIM):
            sl = slice(hh * HEAD_DIM, (hh + 1) * HEAD_DIM)
            dout = jnp.where(has_q, dq_ref[:, sl], 0.0)
            dv, dg = head_bwd(dout, q_ref[:, sl], qg_ref[...], cs, sn)
            dqo_ref[:, sl] = dv.astype(dqo_ref.dtype)
            dqg_ref[...] += dg
        for hh in range(kw // HEAD_DIM):
            sl = slice(hh * HEAD_DIM, (hh + 1) * HEAD_DIM)
            dv, dg = head_bwd(dk_ref[:, sl], k_ref[:, sl], kg_ref[...], cs, sn)
            dko_ref[:, sl] = dv.astype(dko_ref.dtype)
            dkg_ref[...] += dg

    row = lambda i: (i, 0)
    one = lambda i: (0, 0)
    return pl.pallas_call(
        body, name=name, grid=(t // tr,),
        in_specs=[pl.BlockSpec((tr, aw), lambda i: (jnp.minimum(i, nq - 1), 0)), pl.BlockSpec((tr, kw), row),
                  pl.BlockSpec((tr, aw), row), pl.BlockSpec((tr, kw), lambda i: (i, aw // kw)),
                  pl.BlockSpec((tr, HEAD_DIM), row), pl.BlockSpec((tr, HEAD_DIM), row),
                  pl.BlockSpec((1, HEAD_DIM), one), pl.BlockSpec((1, HEAD_DIM), one)],
        out_specs=[pl.BlockSpec((tr, aw), row), pl.BlockSpec((tr, kw), row),
                   pl.BlockSpec((1, HEAD_DIM), one), pl.BlockSpec((1, HEAD_DIM), one)],
        out_shape=[jax.ShapeDtypeStruct((t, aw), BF16), jax.ShapeDtypeStruct((t, kw), BF16),
                   jax.ShapeDtypeStruct((1, HEAD_DIM), F32), jax.ShapeDtypeStruct((1, HEAD_DIM), F32)],
        compiler_params=_cp(("arbitrary",)))(dq_rot, dk_rot, p, p, cosf, sinf, q_gain, k_gain)


def _attn_tiles(lq, t):
    return _tile(lq, 1024), _tile(t, 768)


def _attn_fwd(name, q, k, p, v_off, lq, groups):
    t = k.shape[0]
    hq = q.shape[1] // HEAD_DIM
    hkv = hq // groups
    gw = groups * HEAD_DIM
    tq, tk = _attn_tiles(lq, t)
    nkv = t // tk
    vb = v_off // HEAD_DIM

    def body(q_ref, k_ref, v_ref, o_ref, lse_ref, m_sc, l_sc, acc_sc, s_sc):
        j = pl.program_id(2)

        @pl.when(j == 0)
        def _():
            m_sc[...] = jnp.full(m_sc.shape, -jnp.inf, F32)
            l_sc[...] = jnp.zeros(l_sc.shape, F32)
            acc_sc[...] = jnp.zeros(acc_sc.shape, F32)
            s_sc[...] = jnp.zeros(s_sc.shape, F32)

        has_prev = j > 0
        kv, vv = k_ref[...], v_ref[...]
        for g in range(groups):
            sl = slice(g * HEAD_DIM, (g + 1) * HEAD_DIM)
            s = s_sc[g]
            s_sc[g] = _dot(q_ref[:, sl], kv, NT)
            m_prev = m_sc[g]
            m_new = jnp.where(has_prev, jnp.maximum(m_prev, jnp.max(s, axis=-1, keepdims=True)), jnp.inf)
            alpha = jnp.exp2(m_prev - m_new)
            pexp = jnp.exp2(s - m_new)
            part = pexp[:, 0:LANES]
            for cb in range(1, tk // LANES):
                part = part + pexp[:, cb * LANES:(cb + 1) * LANES]
            l_sc[g] = alpha * l_sc[g] + part
            acc_sc[:, sl] = alpha * acc_sc[:, sl] + _dot(pexp.astype(BF16), vv, NN)
            m_sc[g] = jnp.where(has_prev, m_new, -jnp.inf)

        @pl.when(j == nkv)
        def _():
            for g in range(groups):
                sl = slice(g * HEAD_DIM, (g + 1) * HEAD_DIM)
                l_row = jnp.sum(l_sc[g], axis=-1, keepdims=True)
                o_ref[:, sl] = (acc_sc[:, sl] * (1.0 / l_row)).astype(o_ref.dtype)
                lse_ref[g] = jnp.broadcast_to(m_sc[g] + jnp.log2(l_row), (tq, LANES))

    return pl.pallas_call(
        body, name=name, grid=(hkv, lq // tq, nkv + 1),
        in_specs=[pl.BlockSpec((tq, gw), lambda h, i, j: (i, h)),
                  pl.BlockSpec((tk, HEAD_DIM), lambda h, i, j: (jnp.minimum(j, nkv - 1), h)),
                  pl.BlockSpec((tk, HEAD_DIM), lambda h, i, j: (jnp.maximum(j - 1, 0), vb + h))],
        out_specs=[pl.BlockSpec((tq, gw), lambda h, i, j: (i, h)),
                   pl.BlockSpec((groups, tq, LANES), lambda h, i, j: (h, i, 0))],
        out_shape=[jax.ShapeDtypeStruct((lq, hq * HEAD_DIM), BF16), jax.ShapeDtypeStruct((hq, lq, LANES), F32)],
        scratch_shapes=[pltpu.VMEM((groups, tq, 1), F32), pltpu.VMEM((groups, tq, LANES), F32), pltpu.VMEM((tq, gw), F32),
                        pltpu.VMEM((groups, tq, tk), F32)],
        compiler_params=_cp(("parallel", "parallel", "arbitrary")))(q, k, p)


def _attn_delta(name, o, do):
    lq, aw = o.shape
    hq = aw // HEAD_DIM
    tr = _tile(lq, 512, SUBLANES)

    def body(o_ref, do_ref, d_ref):
        for h in range(hq):
            sl = slice(h * HEAD_DIM, (h + 1) * HEAD_DIM)
            dsum = jnp.sum(do_ref[:, sl].astype(F32) * o_ref[:, sl].astype(F32), axis=-1, keepdims=True)
            d_ref[h] = jnp.broadcast_to(dsum, (tr, LANES))

    spec = pl.BlockSpec((tr, aw), lambda i: (i, 0))
    return pl.pallas_call(
        body, name=name, grid=(lq // tr,), in_specs=[spec, spec],
        out_specs=pl.BlockSpec((hq, tr, LANES), lambda i: (0, i, 0)),
        out_shape=jax.ShapeDtypeStruct((hq, lq, LANES), F32), compiler_params=_cp(("parallel",)))(o, do)


def _attn_bwd(name, q, k, p, v_off, do, lse, delta, lq, groups):
    t = k.shape[0]
    hkv = k.shape[1] // HEAD_DIM
    gw = groups * HEAD_DIM
    tq, tk = _attn_tiles(lq, t)
    nq, nkv = lq // tq, t // tk
    scale = HEAD_DIM ** -0.5
    vb = v_off // HEAD_DIM

    def body(q_ref, k_ref, v_ref, do_ref, lse_ref, delta_ref, dq_ref, dk_ref, dv_ref, dq_sc, dk_sc, dv_sc):
        j, i = pl.program_id(1), pl.program_id(2)

        @pl.when(i == 0)
        def _():
            dk_sc[...] = jnp.zeros(dk_sc.shape, F32)
            dv_sc[...] = jnp.zeros(dv_sc.shape, F32)

        @pl.when(j == 0)
        def _():
            dq_sc[i] = jnp.zeros((tq, gw), F32)

        kv, vv = k_ref[...], v_ref[...]
        dk_part, dv_part = None, None
        for g in range(groups):
            sl = slice(g * HEAD_DIM, (g + 1) * HEAD_DIM)
            qv, dov = q_ref[:, sl], do_ref[:, sl]
            s = _dot(qv, kv, NT)
            pexp = jnp.exp2(s - lse_ref[g, :, 0:1])
            dv_g = _dot(pexp.astype(BF16), dov, TN)
            dp = _dot(dov, vv, NT)
            ds = (pexp * (dp - delta_ref[g, :, 0:1])).astype(BF16)
            dk_g = _dot(ds, qv, TN)
            dq_sc[i, :, sl] += _dot(ds, kv, NN)
            dk_part = dk_g if dk_part is None else dk_part + dk_g
            dv_part = dv_g if dv_part is None else dv_part + dv_g
        dk_sc[...] += dk_part
        dv_sc[...] += dv_part

        @pl.when(i == nq - 1)
        def _():
            dk_ref[...] = dk_sc[...] * LN2
            dv_ref[...] = dv_sc[...].astype(dv_ref.dtype)

        @pl.when(j == nkv - 1)
        def _():
            dq_ref[...] = dq_sc[i] * scale

    qspec = pl.BlockSpec((tq, gw), lambda kh, j, i: (i, kh))
    kspec = pl.BlockSpec((tk, HEAD_DIM), lambda kh, j, i: (j, kh))
    rowspec = pl.BlockSpec((groups, tq, LANES), lambda kh, j, i: (kh, i, 0))
    dqspec = pl.BlockSpec((tq, gw), lambda kh, j, i: (jnp.where(j == nkv - 1, i, 0), kh))
    return pl.pallas_call(
        body, name=name, grid=(hkv, nkv, nq),
        in_specs=[qspec, kspec, pl.BlockSpec((tk, HEAD_DIM), lambda kh, j, i: (j, vb + kh)), qspec, rowspec, rowspec],
        out_specs=[dqspec, kspec, kspec],
        out_shape=[jax.ShapeDtypeStruct((lq, hkv * gw), F32), jax.ShapeDtypeStruct((t, hkv * HEAD_DIM), F32),
                   jax.ShapeDtypeStruct((t, hkv * HEAD_DIM), BF16)],
        scratch_shapes=[pltpu.VMEM((nq, tq, gw), F32), pltpu.VMEM((tk, HEAD_DIM), F32), pltpu.VMEM((tk, HEAD_DIM), F32)],
        compiler_params=_cp(("parallel", "arbitrary", "arbitrary")))(q, k, p, do, lse, delta)


def _ret_tables(log_gamma, direction):
    c = RET_CHUNK
    idx = jnp.arange(c, dtype=F32)
    diff = idx[:, None] - idx[None, :]
    if direction == 1:
        diff = -diff
    keep = diff >= 0
    lg = log_gamma.astype(F32)
    mask = jnp.where(keep[None], jnp.exp(jnp.where(keep, diff, 0.0)[None] * lg[:, None, None]), 0.0)
    q_exp = idx + 1.0 if direction == 0 else c - idx
    k_exp = c - 1.0 - idx if direction == 0 else idx
    sign = 1.0 if direction == 0 else -1.0
    lane = lambda v: jnp.broadcast_to(v[..., None], v.shape + (LANES,))
    qdec = lane(jnp.exp(q_exp[None, :] * lg[:, None]))
    kdec = lane(jnp.exp(k_exp[None, :] * lg[:, None]))
    cdec = jnp.broadcast_to(jnp.exp(c * lg)[:, None, None], (lg.shape[0], SUBLANES, LANES))
    weights = lane(jnp.stack([sign * idx, q_exp, -sign * idx, k_exp], axis=0))
    return mask, qdec, kdec, cdec, weights


def _ret_chunk_of(direction, step, nx, nc):
    if direction == 0:
        return jnp.where(step < nc, nx + step, step - nc)
    return jnp.where(step < nc, nx + nc - 1 - step, nx + nc - 1 - step)


def _ret_fwd(name, p, offs, tables, direction, lx, prev):
    q_off, k_off, v_off, rw = offs
    t = p.shape[0]
    c = RET_CHUNK
    n_steps, nx = t // c, lx // c
    nc = n_steps - nx
    bw = math.gcd(math.gcd(q_off, k_off), math.gcd(v_off, rw))
    bw = _tile(bw, 512)
    hpb = bw // HEAD_DIM
    heads = rw // HEAD_DIM
    k_scale = HEAD_DIM ** -0.5
    mask, qdec, kdec, cdec, _ = tables
    rc = lambda n: _ret_chunk_of(direction, n, nx, nc)

    ng = heads // hpb

    def body(*refs):
        q_refs, k_refs, v_refs = refs[0:ng], refs[ng:2 * ng], refs[2 * ng:3 * ng]
        if prev is None:
            m_ref, qd_ref, kd_ref, cd_ref, y_ref, st_ref, s_sc = refs[3 * ng:]
        else:
            m_ref, qd_ref, kd_ref, cd_ref, prev_ref, y_ref, st_ref, s_sc = refs[3 * ng:]
        n = pl.program_id(0)

        @pl.when(n == 0)
        def _():
            s_sc[...] = jnp.zeros(s_sc.shape, F32)

        for hd in range(heads):
            gg, hh = divmod(hd, hpb)
            sl = slice(hh * HEAD_DIM, (hh + 1) * HEAD_DIM)
            osl = slice(hd * HEAD_DIM, (hd + 1) * HEAD_DIM)
            qv = q_refs[gg][:, sl]
            kf = k_refs[gg][:, sl].astype(F32) * k_scale
            kv = kf.astype(BF16)
            vv = v_refs[gg][:, sl]
            state = s_sc[hd]
            st_ref[hd] = state
            a = _dot(qv, kv, NT) * m_ref[hd]
            y = _dot(a.astype(BF16), vv, NN) + _dot(qv, state.astype(BF16), NN) * qd_ref[hd]
            s_sc[hd] = state * cd_ref[hd, 0:1, :] + _dot((kf * kd_ref[hd]).astype(BF16), vv, TN)
            if prev is not None:
                y = y + prev_ref[:, osl]
            y_ref[:, osl] = y

    col = lambda off, gg: (lambda n: (rc(n), off // bw + gg))
    tab3 = lambda n: (0, 0, 0)
    in_specs = [pl.BlockSpec((c, bw), col(off, gg)) for off in (q_off, k_off, v_off) for gg in range(ng)]
    in_specs += [pl.BlockSpec((heads, c, c), tab3), pl.BlockSpec((heads, c, LANES), tab3),
                 pl.BlockSpec((heads, c, LANES), tab3), pl.BlockSpec((heads, SUBLANES, LANES), tab3)]
    args = [p] * (3 * ng) + [mask, qdec, kdec, cdec]
    aliases = {}
    yspec = pl.BlockSpec((c, rw), lambda n: (rc(n), 0))
    if prev is not None:
        in_specs.append(yspec)
        args.append(prev)
        aliases = {len(args) - 1: 0}
    return pl.pallas_call(
        body, name=name, grid=(n_steps,), in_specs=in_specs,
        out_specs=[yspec, pl.BlockSpec((None, heads, HEAD_DIM, HEAD_DIM), lambda n: (n, 0, 0, 0))],
        out_shape=[jax.ShapeDtypeStruct((t, rw), F32), jax.ShapeDtypeStruct((n_steps, heads, HEAD_DIM, HEAD_DIM), F32)],
        scratch_shapes=[pltpu.VMEM((heads, HEAD_DIM, HEAD_DIM), F32)], input_output_aliases=aliases,
        compiler_params=_cp(("arbitrary",)))(*args)


def _ret_bwd(name, p, offs, tables, states, dy, direction, lx, prev):
    q_off, k_off, v_off, rw = offs
    t = p.shape[0]
    c = RET_CHUNK
    n_steps, nx = t // c, lx // c
    nc = n_steps - nx
    bw = math.gcd(math.gcd(q_off, k_off), math.gcd(v_off, rw))
    bw = _tile(bw, 512)
    hpb = bw // HEAD_DIM
    heads = rw // HEAD_DIM
    k_scale = HEAD_DIM ** -0.5
    mask, qdec, kdec, cdec, weights = tables
    step_of = lambda n: n_steps - 1 - n
    rc = lambda n: _ret_chunk_of(direction, step_of(n), nx, nc)

    ng = heads // hpb

    def body(*refs):
        q_refs, k_refs, v_refs = refs[0:ng], refs[ng:2 * ng], refs[2 * ng:3 * ng]
        if prev is None:
            (dy_ref, st_ref, m_ref, qd_ref, kd_ref, cd_ref, w_ref,
             dq_ref, dk_ref, dv_ref, dl_ref, ds_sc, lam_sc) = refs[3 * ng:]
        else:
            (dy_ref, st_ref, m_ref, qd_ref, kd_ref, cd_ref, w_ref, pq_ref, pk_ref, pv_ref,
             dq_ref, dk_ref, dv_ref, dl_ref, ds_sc, lam_sc) = refs[3 * ng:]
        n = pl.program_id(0)

        @pl.when(n == 0)
        def _():
            ds_sc[...] = jnp.zeros(ds_sc.shape, F32)
            lam_sc[...] = jnp.zeros(lam_sc.shape, F32)

        is_x = rc(n) < nx
        for hd in range(heads):
            gg, hh = divmod(hd, hpb)
            sl = slice(hh * HEAD_DIM, (hh + 1) * HEAD_DIM)
            osl = slice(hd * HEAD_DIM, (hd + 1) * HEAD_DIM)
            qv = q_refs[gg][:, sl]
            qf = qv.astype(F32)
            kf = k_refs[gg][:, sl].astype(F32) * k_scale
            kv = kf.astype(BF16)
            vv = v_refs[gg][:, sl]
            dyv = jnp.where(is_x, dy_ref[:, osl], 0.0).astype(BF16)
            state = st_ref[hd]
            dstate = ds_sc[hd]
            dstate_b = dstate.astype(BF16)
            msk, qd, kd = m_ref[hd], qd_ref[hd], kd_ref[hd]
            cd = cd_ref[hd, 0:1, :]
            a = _dot(qv, kv, NT) * msk
            da = (_dot(dyv, vv, NT) * msk).astype(BF16)
            dq_intra = _dot(da, kv, NN)
            dk_intra = _dot(da, qv, TN)
            dq_inter = _dot(dyv, state.astype(BF16), NT) * qd
            dk_inter = _dot(vv, dstate_b, NT) * kd
            dv = _dot(a.astype(BF16), dyv, TN) + _dot((kf * kd).astype(BF16), dstate_b, NN)
            lam_sc[hd] += (qf * (w_ref[0] * dq_intra + w_ref[1] * dq_inter)
                           + kf * (w_ref[2] * dk_intra + w_ref[3] * dk_inter)
                           + (c * cd) * state * dstate)
            ds_sc[hd] = _dot((qf * qd).astype(BF16), dyv, TN) + cd * dstate
            dq = dq_intra + dq_inter
            dk = (dk_intra + dk_inter) * k_scale
            if prev is not None:
                dq = dq + pq_ref[:, osl]
                dk = dk + pk_ref[:, osl]
                dv = dv + pv_ref[:, osl]
            dq_ref[:, osl] = dq
            dk_ref[:, osl] = dk
            dv_ref[:, osl] = dv

        @pl.when(n == n_steps - 1)
        def _():
            for hd in range(heads):
                dl_ref[hd] = jnp.broadcast_to(jnp.sum(lam_sc[hd]), (SUBLANES, LANES))

    col = lambda off, gg: (lambda n: (rc(n), off // bw + gg))
    tab3 = lambda n: (0, 0, 0)
    ospec = pl.BlockSpec((c, rw), lambda n: (rc(n), 0))
    in_specs = [pl.BlockSpec((c, bw), col(off, gg)) for off in (q_off, k_off, v_off) for gg in range(ng)]
    in_specs += [pl.BlockSpec((c, rw), lambda n: (jnp.minimum(rc(n), nx - 1), 0)),
                 pl.BlockSpec((None, heads, HEAD_DIM, HEAD_DIM), lambda n: (step_of(n), 0, 0, 0)),
                 pl.BlockSpec((heads, c, c), tab3), pl.BlockSpec((heads, c, LANES), tab3), pl.BlockSpec((heads, c, LANES), tab3),
                 pl.BlockSpec((heads, SUBLANES, LANES), tab3), pl.BlockSpec((4, c, LANES), tab3)]
    args = [p] * (3 * ng) + [dy, states, mask, qdec, kdec, cdec, weights]
    aliases = {}
    if prev is not None:
        for k_out, arr in enumerate(prev):
            in_specs.append(ospec)
            args.append(arr)
            aliases[len(args) - 1] = k_out
    big = jax.ShapeDtypeStruct((t, rw), F32)
    return pl.pallas_call(
        body, name=name, grid=(n_steps,), in_specs=in_specs,
        out_specs=[ospec, ospec, ospec, pl.BlockSpec((heads, SUBLANES, LANES), tab3)],
        out_shape=[big, big, big, jax.ShapeDtypeStruct((heads, SUBLANES, LANES), F32)],
        scratch_shapes=[pltpu.VMEM((heads, HEAD_DIM, HEAD_DIM), F32), pltpu.VMEM((heads, HEAD_DIM, HEAD_DIM), F32)],
        input_output_aliases=aliases, compiler_params=_cp(("arbitrary",)))(*args)


def _ret_out(name, y, p, g_off, lx):
    rw = y.shape[1]
    bw = _tile(math.gcd(g_off, rw), 512)
    tr = _tile(lx, 512, SUBLANES)

    def body(y_ref, g_ref, o_ref):
        for hh in range(bw // HEAD_DIM):
            sl = slice(hh * HEAD_DIM, (hh + 1) * HEAD_DIM)
            yv = y_ref[:, sl]
            gv = g_ref[:, sl].astype(F32)
            rinv = lax.rsqrt(jnp.mean(yv * yv, axis=-1, keepdims=True) + NORM_EPS)
            o_ref[:, sl] = (gv * _sigmoid(gv) * yv * rinv).astype(o_ref.dtype)

    spec = pl.BlockSpec((tr, bw), lambda i, g: (i, g))
    return pl.pallas_call(
        body, name=name, grid=(lx // tr, rw // bw),
        in_specs=[spec, pl.BlockSpec((tr, bw), lambda i, g: (i, g_off // bw + g))],
        out_specs=spec, out_shape=jax.ShapeDtypeStruct((lx, rw), BF16),
        compiler_params=_cp(("parallel", "parallel")))(y, p)


def _ret_out_bwd(name, dyr, y, p, g_off, lx):
    rw = y.shape[1]
    bw = _tile(math.gcd(g_off, rw), 512)
    tr = _tile(lx, 512, SUBLANES)

    def body(d_ref, y_ref, g_ref, dy_ref, dg_ref):
        for hh in range(bw // HEAD_DIM):
            sl = slice(hh * HEAD_DIM, (hh + 1) * HEAD_DIM)
            yv = y_ref[:, sl]
            gv = g_ref[:, sl].astype(F32)
            dv = d_ref[:, sl]
            rinv = lax.rsqrt(jnp.mean(yv * yv, axis=-1, keepdims=True) + NORM_EPS)
            yn = yv * rinv
            sg = _sigmoid(gv)
            dg_ref[:, sl] = (dv * yn * sg * (1.0 + gv * (1.0 - sg))).astype(dg_ref.dtype)
            dyn = dv * gv * sg
            dy_ref[:, sl] = rinv * (dyn - yn * jnp.mean(dyn * yn, axis=-1, keepdims=True))

    spec = pl.BlockSpec((tr, bw), lambda i, g: (i, g))
    return pl.pallas_call(
        body, name=name, grid=(lx // tr, rw // bw),
        in_specs=[spec, spec, pl.BlockSpec((tr, bw), lambda i, g: (i, g_off // bw + g))],
        out_specs=[spec, spec],
        out_shape=[jax.ShapeDtypeStruct((lx, rw), F32), jax.ShapeDtypeStruct((lx, rw), BF16)],
        compiler_params=_cp(("parallel", "parallel")))(dyr, y, p)


def _merge(name, pa, pr, p, ga_off, gb_off):
    r, d = pa.shape
    cw = _tile(math.gcd(math.gcd(ga_off, gb_off), d), 1024)
    tr = _tile(r, 512, SUBLANES)

    def body(pa_ref, pr_ref, ga_ref, gb_ref, o_ref):
        o_ref[...] = (_sigmoid(ga_ref[...].astype(F32)) * pa_ref[...].astype(F32)
                      + _sigmoid(gb_ref[...].astype(F32)) * pr_ref[...].astype(F32)).astype(o_ref.dtype)

    spec = pl.BlockSpec((tr, cw), lambda i, j: (i, j))
    return pl.pallas_call(
        body, name=name, grid=(r // tr, d // cw),
        in_specs=[spec, spec, pl.BlockSpec((tr, cw), lambda i, j: (i, ga_off // cw + j)),
                  pl.BlockSpec((tr, cw), lambda i, j: (i, gb_off // cw + j))],
        out_specs=spec, out_shape=jax.ShapeDtypeStruct((r, d), BF16),
        compiler_params=_cp(("parallel", "parallel")))(pa, pr, p, p)


def _ada_fwd(name, cond, w):
    rows, d = cond.shape
    n = w.shape[1]
    tn = _tile(n, 768)

    def body(c_ref, w_ref, o_ref, s_ref):
        cv = c_ref[...]
        sv = cv * _sigmoid(cv)
        s_ref[...] = sv
        o_ref[...] = _dot(sv.astype(BF16), w_ref[...].astype(BF16), NN)

    return pl.pallas_call(
        body, name=name, grid=(n // tn,),
        in_specs=[pl.BlockSpec((rows, d), lambda j: (0, 0)), pl.BlockSpec((d, tn), lambda j: (0, j))],
        out_specs=[pl.BlockSpec((rows, tn), lambda j: (0, j)), pl.BlockSpec((rows, d), lambda j: (0, 0))],
        out_shape=[jax.ShapeDtypeStruct((rows, n), F32), jax.ShapeDtypeStruct((rows, d), F32)],
        compiler_params=_cp(("arbitrary",)))(cond, w)


def _ada_bwd(name, s_cond, dmod, w):
    rows, d = s_cond.shape
    n = w.shape[1]
    tn = _tile(n, 768)

    def body(s_ref, dm_ref, w_ref, gw_ref, ds_ref):
        j = pl.program_id(0)

        @pl.when(j == 0)
        def _():
            ds_ref[...] = jnp.zeros(ds_ref.shape, F32)

        dmv = dm_ref[...].astype(BF16)
        gw_ref[...] = _dot(s_ref[...].astype(BF16), dmv, TN)
        ds_ref[...] += _dot(dmv, w_ref[...].astype(BF16), NT)

    return pl.pallas_call(
        body, name=name, grid=(n // tn,),
        in_specs=[pl.BlockSpec((rows, d), lambda j: (0, 0)), pl.BlockSpec((rows, tn), lambda j: (0, j)),
                  pl.BlockSpec((d, tn), lambda j: (0, j))],
        out_specs=[pl.BlockSpec((d, tn), lambda j: (0, j)), pl.BlockSpec((rows, d), lambda j: (0, 0))],
        out_shape=[jax.ShapeDtypeStruct((d, n), F32), jax.ShapeDtypeStruct((rows, d), F32)],
        compiler_params=_cp(("arbitrary",)))(s_cond, dmod, w)


def _sum_slots(name, a):
    s, r, c = a.shape

    def body(a_ref, o_ref):
        acc = a_ref[0]
        for k in range(1, s):
            acc = acc + a_ref[k]
        o_ref[...] = acc

    return pl.pallas_call(
        body, name=name, grid=(1,), in_specs=[pl.BlockSpec((s, r, c), lambda i: (0, 0, 0))],
        out_specs=pl.BlockSpec((r, c), lambda i: (0, 0)), out_shape=jax.ShapeDtypeStruct((r, c), F32),
        compiler_params=_cp(("arbitrary",)))(a)


def _cctx_grad(name, parts, c_ctx):
    s, r, d = parts.shape

    def body(p_ref, c_ref, o_ref):
        acc = p_ref[0]
        for k in range(1, s):
            acc = acc + p_ref[k]
        cv = c_ref[...]
        sg = _sigmoid(cv)
        o_ref[...] = acc[0:1, :] * sg * (1.0 + cv * (1.0 - sg))

    return pl.pallas_call(
        body, name=name, grid=(1,),
        in_specs=[pl.BlockSpec((s, r, d), lambda i: (0, 0, 0)), pl.BlockSpec((1, d), lambda i: (0, 0))],
        out_specs=pl.BlockSpec((1, d), lambda i: (0, 0)), out_shape=jax.ShapeDtypeStruct((1, d), F32),
        compiler_params=_cp(("arbitrary",)))(parts, c_ctx)


def _adamw(name, slots, w, m, v):
    s, r, c = slots.shape
    tr = _tile(r, 256 if c > 1024 else 512, SUBLANES)
    c1 = 1.0 - ADAM_B1 ** ADAM_STEP
    c2 = 1.0 - ADAM_B2 ** ADAM_STEP

    def body(s_ref, w_ref, m_ref, v_ref, g_ref, d_ref, mo_ref, vo_ref):
        g = s_ref[0].astype(F32)
        for k in range(1, s):
            g = g + s_ref[k].astype(F32)
        mn = ADAM_B1 * m_ref[...] + (1.0 - ADAM_B1) * g
        vn = ADAM_B2 * v_ref[...] + (1.0 - ADAM_B2) * (g * g)
        m_hat = mn / c1
        v_hat = vn / c2
        g_ref[...] = g
        mo_ref[...] = mn
        vo_ref[...] = vn
        d_ref[...] = -ADAM_LR * (m_hat / (jnp.sqrt(v_hat) + ADAM_EPS) + ADAM_WD * w_ref[...])

    spec = pl.BlockSpec((tr, c), lambda i: (i, 0))
    shp = jax.ShapeDtypeStruct((r, c), F32)
    return pl.pallas_call(
        body, name=name, grid=(r // tr,),
        in_specs=[pl.BlockSpec((s, tr, c), lambda i: (0, i, 0)), spec, spec, spec],
        out_specs=[spec] * 4, out_shape=[shp] * 4, compiler_params=_cp(("parallel",)))(slots, w, m, v)


def _coords():
    return lax.axis_index("x"), lax.axis_index("y"), lax.axis_index("c")


def _flip(coord, bit):
    return 1 - coord if bit else coord


def _all_gather_small(name, blk):
    r, ccols = blk.shape

    def body(x_ref, out_ref, send_sems, recv_sems, local_sem):
        x, y, c = _coords()
        me, sibling = (x, y, c), (x, y, 1 - c)
        chips = [(1 - x, y), (x, 1 - y), (1 - x, 1 - y)]

        def slot(px, py, pc):
            return out_ref.at[4 * px + 2 * py + pc]

        def copy(k, block, to, src=None):
            return pltpu.make_async_remote_copy(
                src_ref=slot(*block) if src is None else src, dst_ref=slot(*block),
                send_sem=send_sems.at[k], recv_sem=recv_sems.at[k], device_id=to, device_id_type=MESH)

        mine = pltpu.make_async_copy(x_ref, slot(*me), local_sem)
        mine.start()
        first = [copy(0, me, sibling, src=x_ref)]
        first += [copy(1 + j, me, (*chip, c), src=x_ref) for j, chip in enumerate(chips)]
        for cp in first:
            cp.start()
        passed = [copy(4 + j, (*chip, c), sibling) for j, chip in enumerate(chips)]
        for j, chip in enumerate(chips):
            copy(1 + j, (*chip, c), me).wait_recv()
            passed[j].start()
        copy(0, sibling, me).wait_recv()
        for j, chip in enumerate(chips):
            copy(4 + j, (*chip, 1 - c), me).wait_recv()
        for cp in first + passed:
            cp.wait_send()
        mine.wait()

    return pl.pallas_call(
        body, name=name, out_shape=jax.ShapeDtypeStruct((N_DEV, r, ccols), blk.dtype),
        in_specs=[pl.BlockSpec(memory_space=pltpu.VMEM)], out_specs=pl.BlockSpec(memory_space=pltpu.VMEM),
        scratch_shapes=[pltpu.SemaphoreType.DMA((7,)), pltpu.SemaphoreType.DMA((7,)), pltpu.SemaphoreType.DMA],
    )(blk)


def _comm_semaphores(n):
    return [pltpu.SemaphoreType.DMA((7 * n,)), pltpu.SemaphoreType.DMA((7 * n,)), pltpu.SemaphoreType.DMA((n,))]


def _comm_out_shape(kind, s, wide):
    if kind == "gather":
        shape = (s.shape[0], N_DEV * s.shape[1]) if wide else (N_DEV,) + s.shape
    else:
        shape = (N_DEV, s.shape[0], s.shape[1] // N_DEV) if wide else s.shape
    return jax.ShapeDtypeStruct(shape, s.dtype)


def _block_of(ref, idx, wide, cols):
    if not wide:
        return ref.at[idx]
    return ref.at[:, pl.ds(pl.multiple_of(idx * cols, LANES), cols)]


def _gather_plan(ins, outs, wide, send_sems, recv_sems, local_sems):
    n = len(ins)
    x, y, c = _coords()
    me, sibling = (x, y, c), (x, y, 1 - c)
    chips = [(1 - x, y), (x, 1 - y), (1 - x, 1 - y)]

    def slot(a, px, py, pc):
        return _block_of(outs[a], 4 * px + 2 * py + pc, wide[a], ins[a].shape[-1])

    def copy(a, k, block, to, src=None):
        return pltpu.make_async_remote_copy(
            src_ref=slot(a, *block) if src is None else src, dst_ref=slot(a, *block),
            send_sem=send_sems.at[7 * a + k], recv_sem=recv_sems.at[7 * a + k], device_id=to, device_id_type=MESH)

    def local(a):
        return pltpu.make_async_copy(ins[a], slot(a, *me), local_sems.at[a])

    def first(a):
        return [copy(a, 0, me, sibling, src=ins[a])] + [copy(a, 1 + j, me, (*chip, c), src=ins[a])
                                                        for j, chip in enumerate(chips)]

    def passed(a, j):
        return copy(a, 4 + j, (*chips[j], c), sibling)

    def start():
        for a in range(n):
            local(a).start()
            for cp in first(a):
                cp.start()

    def finish():
        for a in range(n):
            for j, chip in enumerate(chips):
                copy(a, 1 + j, (*chip, c), me).wait_recv()
                passed(a, j).start()
        for a in range(n):
            copy(a, 0, sibling, me).wait_recv()
            for j, chip in enumerate(chips):
                copy(a, 4 + j, (*chip, 1 - c), me).wait_recv()
        for a in range(n):
            for cp in first(a) + [passed(a, j) for j in range(3)]:
                cp.wait_send()
            local(a).wait()

    return start, finish


def _exchange_plan(ins, outs, wide, send_sems, recv_sems, local_sems):
    n = len(ins)
    x, y, c = _coords()
    my_idx = 4 * x + 2 * y + c

    def src(a, idx):
        return _block_of(ins[a], idx, wide[a], outs[a].shape[-1])

    def local(a):
        return pltpu.make_async_copy(src(a, my_idx), outs[a].at[my_idx], local_sems.at[a])

    def pair(a, rel):
        px, py, pc = _flip(x, rel & 4), _flip(y, rel & 2), _flip(c, rel & 1)
        peer_idx = 4 * px + 2 * py + pc
        sems = dict(send_sem=send_sems.at[7 * a + rel - 1], recv_sem=recv_sems.at[7 * a + rel - 1],
                    device_id=(px, py, pc), device_id_type=MESH)
        send = pltpu.make_async_remote_copy(src_ref=src(a, peer_idx), dst_ref=outs[a].at[my_idx], **sems)
        recv = pltpu.make_async_remote_copy(src_ref=src(a, my_idx), dst_ref=outs[a].at[peer_idx], **sems)
        return send, recv

    def start():
        for a in range(n):
            local(a).start()
            for rel in range(1, N_DEV):
                pair(a, rel)[0].start()

    def finish():
        for a in range(n):
            for rel in range(1, N_DEV):
                send, recv = pair(a, rel)
                recv.wait_recv()
                send.wait_send()
            local(a).wait()

    return start, finish


def _comm_plan(kind, ins, outs, wide, send_sems, recv_sems, local_sems):
    plan = {"gather": _gather_plan, "exchange": _exchange_plan}[kind]
    return plan(ins, outs, wide, send_sems, recv_sems, local_sems)


def _all_gather_big(name, shards, wide):
    n = len(shards)

    def body(*refs):
        start, finish = _gather_plan(refs[:n], refs[n:2 * n], wide, *refs[2 * n:])
        start()
        finish()

    any_spec = pl.BlockSpec(memory_space=pl.ANY)
    return pl.pallas_call(
        body, name=name, out_shape=[_comm_out_shape("gather", s, w) for s, w in zip(shards, wide)],
        in_specs=[any_spec] * n, out_specs=[any_spec] * n, scratch_shapes=_comm_semaphores(n))(*shards)


def _rope_tables(lx, lc):
    rows = lx // GRID_W
    row = jnp.repeat(jnp.arange(rows, dtype=F32), GRID_W)
    col = jnp.tile(jnp.arange(GRID_W, dtype=F32), rows)
    half = HEAD_DIM // 2
    inv_freq = ROPE_THETA ** (-jnp.arange(0, half, 2, dtype=F32) / half)
    ang = jnp.concatenate([row[:, None] * inv_freq, col[:, None] * inv_freq], axis=-1)
    cos, sin = jnp.cos(ang), jnp.sin(ang)
    cosf = jnp.repeat(cos, 2, axis=-1)
    sinf = jnp.stack([-sin, sin], axis=-1).reshape(lx, HEAD_DIM)
    cosf = jnp.concatenate([cosf, jnp.ones((lc, HEAD_DIM), F32)], axis=0)
    sinf = jnp.concatenate([sinf, jnp.zeros((lc, HEAD_DIM), F32)], axis=0)
    return cosf, sinf


def _cols_full(g):
    return jnp.transpose(g, (1, 0, 2)).reshape(g.shape[1], N_DEV * g.shape[2])


def _cols_split(w):
    k, n = w.shape
    return jnp.transpose(w.reshape(k, N_DEV, n // N_DEV), (1, 0, 2))


def _pad_rows(a, rows):
    return jnp.pad(a, ((0, rows - a.shape[0]), (0, 0)))


def kernel(x, c, ctx, c_ctx, w_ada, b_ada, ffn1_w_in, ffn1_w_out, mix_w_in, attn_q_gain, attn_k_gain, ret_decay_logit, w_proj_attn, w_proj_ret, mix_w_out, ffn2_w_in, ffn2_w_out, final_norm, loss_target, m_c_ctx, m_w_ada, m_b_ada, m_ffn1_w_in, m_ffn1_w_out, m_mix_w_in, m_attn_q_gain, m_attn_k_gain, m_ret_decay_logit, m_w_proj_attn, m_w_proj_ret, m_mix_w_out, m_ffn2_w_in, m_ffn2_w_out, m_final_norm, v_c_ctx, v_w_ada, v_b_ada, v_ffn1_w_in, v_ffn1_w_out, v_mix_w_in, v_attn_q_gain, v_attn_k_gain, v_ret_decay_logit, v_w_proj_attn, v_w_proj_ret, v_mix_w_out, v_ffn2_w_in, v_ffn2_w_out, v_final_norm):
    lx, d = x.shape[1], x.shape[2]
    lc = ctx.shape[1]
    t = lx + lc
    aw, rw = w_proj_attn.shape[1], w_proj_ret.shape[1]
    pw = mix_w_in.shape[2] * N_DEV
    kw = (pw - aw - 4 * rw - 2 * d) // 2
    groups = aw // kw
    heads_r = rw // HEAD_DIM
    ka_off, va_off = aw, aw + kw
    qr_off = aw + 2 * kw
    kr_off, vr_off, gr_off = qr_off + rw, qr_off + 2 * rw, qr_off + 3 * rw
    ga_off, gb_off = qr_off + 4 * rw, qr_off + 4 * rw + d
    xi, yi, ci = _coords()
    me = 4 * xi + 2 * yi + ci

    col_names = ["ffn1_w_in", "mix_w_in", "w_proj_attn", "w_proj_ret", "ffn2_w_in"]
    row_names = ["ffn1_w_out", "mix_w_out", "ffn2_w_out"]
    shard = dict(ffn1_w_in=ffn1_w_in[0], mix_w_in=mix_w_in[0], w_proj_attn=w_proj_attn[0], w_proj_ret=w_proj_ret[0],
                 ffn2_w_in=ffn2_w_in[0], ffn1_w_out=ffn1_w_out[0], mix_w_out=mix_w_out[0], ffn2_w_out=ffn2_w_out[0])
    names = col_names + row_names
    bf_shard = {k: shard[k].astype(BF16) for k in names}
    full, landed = {}, {}

    def is_wide(k):
        return k in col_names and shard[k].shape[1] % LANES == 0

    def gather_of(keys):
        return "gather", [bf_shard[k] for k in keys], [is_wide(k) for k in keys]

    def keep_full(keys, gathered):
        for k, g in zip(keys, gathered):
            if is_wide(k):
                full[k] = g
            else:
                full[k] = _cols_full(g) if k in col_names else g.reshape(N_DEV * g.shape[1], g.shape[2])

    def exchange_of(grads):
        blocks = [g if is_wide(k) else _cols_split(g) if k in col_names
                  else g.reshape(N_DEV, g.shape[0] // N_DEV, g.shape[1]) for k, g in grads.items()]
        return "exchange", blocks, [is_wide(k) for k in grads]

    def keep_landed(grads, got):
        landed.update(zip(grads.keys(), got))

    keep_full(["ffn1_w_in"], _all_gather_big("gather_first", *gather_of(["ffn1_w_in"])[1:]))

    c_all = _all_gather_small("gather_cond", _pad_rows(c, SUBLANES))[:, 0, :]
    cond = _pad_rows(jnp.concatenate([c_all, c_ctx[None, :]], axis=0), 2 * SUBLANES)
    ada_part, s_cond = _ada_fwd("ada_fwd", cond, w_ada[0])
    ada_all = _all_gather_small("gather_ada", ada_part)
    mod_all = jnp.transpose(ada_all, (1, 0, 2)).reshape(2 * SUBLANES, N_MOD * d) + b_ada
    mod_x = lax.dynamic_index_in_dim(mod_all, me, axis=0, keepdims=False).reshape(N_MOD, d)
    mod_c = mod_all[N_DEV].reshape(N_MOD, d)
    mods = jnp.stack([mod_c, mod_x], axis=0)[:, :, None, :]
    sh1, sc1, g1, sh2, sc2, g2, sh3, sc3, g3 = [mods[:, k] for k in range(N_MOD)]

    h0 = jnp.concatenate([x[0], ctx[0]], axis=0)
    u1 = _rmsmod("ffn1_norm", h0, sc1, sh1, lx)
    keys = ["ffn1_w_out", "mix_w_in"]
    (z1, s1), got = _ffn_in("ffn1_in", u1, full["ffn1_w_in"], comm=gather_of(keys))
    keep_full(keys, got)
    keys = ["w_proj_attn", "w_proj_ret", "mix_w_out", "ffn2_w_out"]
    (h1, f1), got = _mm_residual("ffn1_out", s1, full["ffn1_w_out"], h0, g1, 0.5, lx, comm=gather_of(keys))
    keep_full(keys, got)

    u2 = _rmsmod("mix_norm", h1, sc2, sh2, lx)
    keys = ["ffn2_w_in"]
    p, got = _mm_nn("mix_in", u2, full["mix_w_in"], BF16, tm_pref=M_TILE_BIG, tn_pref=512, comm=gather_of(keys))
    keep_full(keys, got)
    cosf, sinf = _rope_tables(lx, lc)
    q_rot, k_rot = _qk_prep("qk_prep", p, cosf, sinf, attn_q_gain, attn_k_gain, aw, kw)
    ya, lse = _attn_fwd("attn_fwd", q_rot, k_rot, p, va_off, lx, groups)

    decay = ret_decay_logit[0].astype(F32)
    log_gamma = jax.nn.log_sigmoid(decay)
    r_offs = (qr_off, kr_off, vr_off, rw)
    tab = [_ret_tables(log_gamma[k], k) for k in range(2)]
    y_f, st_f = _ret_fwd("ret_fwd_a", p, r_offs, tab[0], 0, lx, None)
    y_r, st_b = _ret_fwd("ret_fwd_b", p, r_offs, tab[1], 1, lx, y_f)
    yr = _ret_out("ret_out", y_r, p, gr_off, lx)

    pa = _mm_nn("proj_attn", ya, full["w_proj_attn"], BF16)
    pr = _mm_nn("proj_ret", yr, full["w_proj_ret"], BF16)
    mg = _merge("merge", pa, pr, p, ga_off, gb_off)
    h2, o2 = _mm_residual("mix_out", mg, full["mix_w_out"], h1, g2, 1.0, lx)

    u3 = _rmsmod("ffn2_norm", h2, sc3, sh3, lx)
    z3, s3 = _ffn_in("ffn2_in", u3, full["ffn2_w_in"])
    h3, f3 = _mm_residual("ffn2_out", s3, full["ffn2_w_out"], h2, g3, 0.5, lx)
    dh3, d_fn, loss_tile = _final_loss("final_loss", h3, final_norm[None, :], loss_target[0])

    df3, dg3 = _gate_bwd("ffn2_gate_bwd", dh3, f3, g3, 0.5, lx)
    dz3 = _ffn_out_bwd("ffn2_out_bwd", df3, full["ffn2_w_out"], z3)
    grads = {"ffn2_w_out": _mm_tn("ffn2_out_dw", s3, df3, BF16, tm_pref=M_TILE_BIG)}
    g_ffn2_w_in, got = _ffn_in_bwd_w("ffn2_in_dw", u3, dz3, comm=exchange_of(grads))
    keep_landed(grads, got)
    grads = {"ffn2_w_in": g_ffn2_w_in}
    du3, got = _ffn_in_bwd_x("ffn2_in_bwd", dz3, full["ffn2_w_in"], comm=exchange_of(grads))
    keep_landed(grads, got)
    dh2, dsc3, dsh3 = _rmsmod_bwd("ffn2_norm_bwd", du3, h2, sc3, dh3, lx, lx)

    do2, dg2 = _gate_bwd("mix_gate_bwd", dh2, o2, g2, 1.0, lx)
    dpa, dpr, dga, dgb = _merge_bwd("merge_bwd", do2, full["mix_w_out"], pa, pr, p, ga_off, gb_off)
    dya = _mm_nt("proj_attn_bwd", dpa, full["w_proj_attn"], BF16)
    dyr = _mm_nt("proj_ret_bwd", dpr, full["w_proj_ret"], F32)
    mix_grads = {"mix_w_out": _mm_tn("mix_out_dw", mg, do2, BF16),
                 "w_proj_attn": _mm_tn("proj_attn_dw", ya, dpa, BF16),
                 "w_proj_ret": _mm_tn("proj_ret_dw", yr, dpr, BF16)}

    dy_ret, dgr = _ret_out_bwd("ret_out_bwd", dyr, y_r, p, gr_off, lx)
    dqr, dkr, dvr, dl_f = _ret_bwd("ret_bwd_a", p, r_offs, tab[0], st_f, dy_ret, 0, lx, None)
    dqr, dkr, dvr, dl_b = _ret_bwd("ret_bwd_b", p, r_offs, tab[1], st_b, dy_ret, 1, lx, (dqr, dkr, dvr))
    d_lam = jnp.stack([dl_f[:, 0, 0], dl_b[:, 0, 0]], axis=0)
    d_decay = d_lam * jax.nn.sigmoid(-decay)

    delta = _attn_delta("attn_delta", ya, dya)
    dq_rot, dk_rot, dva = _attn_bwd("attn_bwd", q_rot, k_rot, p, va_off, dya, lse, delta, lx, groups)
    dqa, dka, d_qg, d_kg = _qk_prep_bwd("qk_prep_bwd", dq_rot, dk_rot, p, cosf, sinf, attn_q_gain, attn_k_gain, aw, kw)

    dp = jnp.concatenate([dqa, dka, dva, dqr.astype(BF16), dkr.astype(BF16), dvr.astype(BF16),
                          _pad_rows(dgr, t), _pad_rows(dga, t), _pad_rows(dgb, t)], axis=1)
    grads = {"mix_w_in": _mm_tn("mix_in_dw", u2, dp, BF16, tm_pref=M_TILE_DW, tn_pref=512)}
    du2, got = _mm_nt("mix_in_bwd", dp, full["mix_w_in"], F32, comm=exchange_of(grads))
    keep_landed(grads, got)
    dh1, dsc2, dsh2 = _rmsmod_bwd("mix_norm_bwd", du2, h1, sc2, dh2, lx, t)

    df1, dg1 = _gate_bwd("ffn1_gate_bwd", dh1, f1, g1, 0.5, lx)
    dz1, got = _ffn_out_bwd("ffn1_out_bwd", df1, full["ffn1_w_out"], z1, comm=exchange_of(mix_grads))
    keep_landed(mix_grads, got)
    grads = {"ffn1_w_out": _mm_tn("ffn1_out_dw", s1, df1, BF16, tm_pref=M_TILE_BIG)}
    g_ffn1_w_in, got = _ffn_in_bwd_w("ffn1_in_dw", u1, dz1, comm=exchange_of(grads))
    keep_landed(grads, got)
    grads = {"ffn1_w_in": g_ffn1_w_in}
    du1, got = _ffn_in_bwd_x("ffn1_in_bwd", dz1, full["ffn1_w_in"], comm=exchange_of(grads))
    keep_landed(grads, got)
    grad_x, dsc1, dsh1 = _rmsmod_bwd("ffn1_norm_bwd", du1, h0, sc1, dh1, lx, lx)

    zero = jnp.zeros((1, d), F32)
    dmod_c = jnp.concatenate([dsh1[0], dsc1[0], dg1[0], dsh2[0], dsc2[0], zero, zero, zero, zero], axis=0)
    dmod_x = jnp.concatenate([dsh1[1], dsc1[1], dg1[1], dsh2[1], dsc2[1], dg2[1], dsh3[1], dsc3[1], dg3[1]], axis=0)
    misc = jnp.concatenate([d_qg[0], d_kg[0], d_decay.reshape(-1), loss_tile[0, 0:1]])
    misc = jnp.pad(misc, (0, d - misc.shape[0]))[None, :]
    n_small = 3 * SUBLANES
    small = _pad_rows(jnp.concatenate([dmod_c, dmod_x, d_fn, misc], axis=0), n_small)
    small_all = _all_gather_small("gather_small", small)
    small_sum = _sum_slots("sum_small", small_all)
    dmod_c_sum, dmod_x_sum = small_sum[0:N_MOD], small_sum[N_MOD:2 * N_MOD]
    g_final_norm = small_sum[2 * N_MOD]
    misc_sum = small_sum[2 * N_MOD + 1]
    g_qg = misc_sum[0:HEAD_DIM][None, :]
    g_kg = misc_sum[HEAD_DIM:2 * HEAD_DIM][None, :]
    g_decay = misc_sum[2 * HEAD_DIM:2 * HEAD_DIM + 2 * heads_r].reshape(1, 2, heads_r)
    loss = misc_sum[2 * HEAD_DIM + 2 * heads_r]
    g_b_ada = (dmod_x_sum + dmod_c_sum).reshape(1, N_MOD * d)

    n_ada = w_ada.shape[2]
    dmod_rows = jnp.concatenate([small_all[:, N_MOD:2 * N_MOD, :].reshape(N_DEV, N_MOD * d),
                                 dmod_c_sum.reshape(1, N_MOD * d)], axis=0)
    dmod_mine = _pad_rows(lax.dynamic_slice_in_dim(dmod_rows, me * n_ada, n_ada, axis=1), 2 * SUBLANES)
    g_w_ada, ds_cond = _ada_bwd("ada_bwd", s_cond, dmod_mine, w_ada[0])
    cctx_parts = _all_gather_small("gather_cctx", ds_cond[N_DEV:N_DEV + SUBLANES])
    g_c_ctx = _cctx_grad("cctx_grad", cctx_parts, c_ctx[None, :])[0]

    mom = dict(ffn1_w_in=(m_ffn1_w_in, v_ffn1_w_in), mix_w_in=(m_mix_w_in, v_mix_w_in),
               w_proj_attn=(m_w_proj_attn, v_w_proj_attn), w_proj_ret=(m_w_proj_ret, v_w_proj_ret),
               ffn2_w_in=(m_ffn2_w_in, v_ffn2_w_in), ffn1_w_out=(m_ffn1_w_out, v_ffn1_w_out),
               mix_w_out=(m_mix_w_out, v_mix_w_out), ffn2_w_out=(m_ffn2_w_out, v_ffn2_w_out))
    res = {}
    for k in names:
        res[k] = _adamw("adamw_" + k, landed[k], shard[k], mom[k][0][0], mom[k][1][0])
    res["w_ada"] = _adamw("adamw_w_ada", g_w_ada[None], w_ada[0], m_w_ada[0], v_w_ada[0])

    def pack(cc, ba, qg, kg, dec, fn):
        misc_row = jnp.concatenate([qg.reshape(-1), kg.reshape(-1), dec.reshape(-1)])
        misc_row = jnp.pad(misc_row, (0, d - misc_row.shape[0]))[None, :]
        return _pad_rows(jnp.concatenate([cc.reshape(1, d), ba.reshape(N_MOD, d), fn.reshape(1, d), misc_row], axis=0),
                         2 * SUBLANES)

    sg, sd, sm, sv = _adamw(
        "adamw_small", pack(g_c_ctx, g_b_ada, g_qg, g_kg, g_decay, g_final_norm)[None],
        pack(c_ctx, b_ada, attn_q_gain, attn_k_gain, ret_decay_logit, final_norm),
        pack(m_c_ctx, m_b_ada, m_attn_q_gain, m_attn_k_gain, m_ret_decay_logit, m_final_norm),
        pack(v_c_ctx, v_b_ada, v_attn_q_gain, v_attn_k_gain, v_ret_decay_logit, v_final_norm))

    def unpack(a):
        misc_row = a[N_MOD + 2]
        return dict(c_ctx=a[0], b_ada=a[1:1 + N_MOD].reshape(1, N_MOD * d), final_norm=a[N_MOD + 1],
                    attn_q_gain=misc_row[0:HEAD_DIM][None, :], attn_k_gain=misc_row[HEAD_DIM:2 * HEAD_DIM][None, :],
                    ret_decay_logit=misc_row[2 * HEAD_DIM:2 * HEAD_DIM + 2 * heads_r].reshape(1, 2, heads_r))

    small_out = [unpack(a) for a in (sg, sd, sm, sv)]
    order = ["c_ctx", "w_ada", "b_ada", "ffn1_w_in", "ffn1_w_out", "mix_w_in", "attn_q_gain", "attn_k_gain",
             "ret_decay_logit", "w_proj_attn", "w_proj_ret", "mix_w_out", "ffn2_w_in", "ffn2_w_out", "final_norm"]
    outs = [loss, grad_x[None]]
    for which in range(4):
        for k in order:
            outs.append(res[k][which][None] if k in res else small_out[which][k])
    return tuple(outs)
```

```python
import math

import jax
import jax.numpy as jnp
from jax import lax
from jax.experimental import pallas as pl
from jax.experimental.pallas import tpu as pltpu

F32 = jnp.float32
BF16 = jnp.bfloat16
MESH = pl.DeviceIdType.MESH

N_DEV = 8
HEAD_DIM = 128
GRID_W = 64
ROPE_THETA = 10000.0
NORM_EPS = 1e-6
RET_CHUNK = 128
N_MOD = 9
LANES = 128
SUBLANES = 8
V7X_VMEM_BYTES = 64 * 1024 * 1024
VMEM_LIMIT = V7X_VMEM_BYTES - 8 * 1024 * 1024
K_TILE = 2560
M_TILE_BIG = 1408
M_TILE_DW = 2048
LOG2E = 1.4426950408889634
LN2 = 0.6931471805599453

ADAM_LR = 0.001
ADAM_B1 = 0.9
ADAM_B2 = 0.999
ADAM_EPS = 1e-08
ADAM_WD = 0.01
ADAM_STEP = 10

NN = (((1,), (0,)), ((), ()))
NT = (((1,), (1,)), ((), ()))
TN = (((0,), (0,)), ((), ()))


def _tile(n, pref, align=LANES):
    best = None
    t = align
    while t <= min(n, pref):
        if n % t == 0:
            best = t
        t += align
    return n if best is None else best


def _cp(sem):
    return pltpu.CompilerParams(dimension_semantics=sem, vmem_limit_bytes=VMEM_LIMIT)


def _sigmoid(v):
    return 0.5 * jnp.tanh(0.5 * v) + 0.5


def _dot(a, b, dims):
    return lax.dot_general(a, b, dims, preferred_element_type=F32)


def _mm(name, a, a_spec, b_list, dims, grid, out_shapes, out_specs, acc_shape, epi, extras=(), comm=None):
    nb, ne, no = len(b_list), len(extras), len(out_shapes)
    nk = grid[2]
    kind, c_arrays, c_wide = comm if comm is not None else (None, [], [])
    ncm = len(c_arrays)

    def body(*refs):
        a_ref = refs[0]
        b_refs = refs[1:1 + nb]
        e_refs = refs[1 + nb:1 + nb + ne]
        pos = 1 + nb + ne
        c_ins = refs[pos:pos + ncm]
        o_refs = refs[pos + ncm:pos + ncm + no]
        c_outs = refs[pos + ncm + no:pos + 2 * ncm + no]
        scratch = refs[pos + 2 * ncm + no:]
        accs = scratch[:0 if nk == 1 else nb]
        ids = [pl.program_id(axis) for axis in range(3)]
        if ncm:
            start_comm, mid_comm, finish_comm = _comm_plan(kind, c_ins, c_outs, c_wide, *scratch[len(accs):])
            step = (ids[0] * grid[1] + ids[1]) * nk + ids[2]
            pl.when(step == 0)(start_comm)
            if mid_comm is not None:
                pl.when(step == (3 * grid[0] * grid[1] * nk) // 4)(mid_comm)

        def finish(tiles):
            vals = epi(tiles, e_refs)
            for o_ref, v in zip(o_refs, vals):
                if isinstance(v, tuple):
                    for idx, part in enumerate(v):
                        o_ref[idx] = part.astype(o_ref.dtype)
                else:
                    o_ref[...] = v.astype(o_ref.dtype)

        if nk == 1:
            finish([_dot(a_ref[...], b_ref[...], dims) for b_ref in b_refs])
        else:
            @pl.when(ids[2] == 0)
            def _():
                for acc in accs:
                    acc[...] = jnp.zeros(acc.shape, F32)

            av = a_ref[...]
            for b_ref, acc in zip(b_refs, accs):
                acc[...] += _dot(av, b_ref[...], dims)

            @pl.when(ids[2] == nk - 1)
            def _():
                finish([acc[...] for acc in accs])

        if ncm:
            pl.when(jnp.logical_and(jnp.logical_and(ids[0] == grid[0] - 1, ids[1] == grid[1] - 1),
                                    ids[2] == nk - 1))(finish_comm)

    any_spec = pl.BlockSpec(memory_space=pl.ANY)
    c_shapes = [_comm_out_shape(kind, s, w) for s, w in zip(c_arrays, c_wide)]
    scratch_shapes = [] if nk == 1 else [pltpu.VMEM(acc_shape, F32)] * nb
    if ncm:
        scratch_shapes = scratch_shapes + _comm_semaphores(ncm)
    semantics = ("arbitrary",) * 3 if ncm else ("parallel", "parallel", "arbitrary")
    outs = pl.pallas_call(
        body, name=name, grid=grid,
        in_specs=[a_spec] + [s for _, s in b_list] + [s for _, s in extras] + [any_spec] * ncm,
        out_specs=list(out_specs) + [any_spec] * ncm, out_shape=list(out_shapes) + c_shapes,
        scratch_shapes=scratch_shapes, compiler_params=_cp(semantics),
    )(a, *[b for b, _ in b_list], *[e for e, _ in extras], *c_arrays)
    main = outs[0] if no == 1 else tuple(outs[:no])
    return main if comm is None else (main, list(outs[no:]))


def _plain(accs, _):
    return (accs[0],)


def _mm_nn(name, a, b, out_dtype, tm_pref=1024, tn_pref=1024, tk_pref=K_TILE, comm=None):
    m, k = a.shape
    n = b.shape[1]
    tm, tn, tk = _tile(m, tm_pref), _tile(n, tn_pref), _tile(k, tk_pref)
    return _mm(name, a, pl.BlockSpec((tm, tk), lambda i, j, kk: (i, kk)),
               [(b, pl.BlockSpec((tk, tn), lambda i, j, kk: (kk, j)))], NN, (m // tm, n // tn, k // tk),
               [jax.ShapeDtypeStruct((m, n), out_dtype)], [pl.BlockSpec((tm, tn), lambda i, j, kk: (i, j))],
               (tm, tn), _plain, comm=comm)


def _mm_nt(name, a, b, out_dtype, tm_pref=1024, tn_pref=1024, tk_pref=K_TILE, comm=None):
    m, k = a.shape
    n = b.shape[0]
    tm, tn, tk = _tile(m, tm_pref), _tile(n, tn_pref), _tile(k, tk_pref)
    return _mm(name, a, pl.BlockSpec((tm, tk), lambda i, j, kk: (i, kk)),
               [(b, pl.BlockSpec((tn, tk), lambda i, j, kk: (j, kk)))], NT, (m // tm, n // tn, k // tk),
               [jax.ShapeDtypeStruct((m, n), out_dtype)], [pl.BlockSpec((tm, tn), lambda i, j, kk: (i, j))],
               (tm, tn), _plain, comm=comm)


def _mm_tn(name, a, b, out_dtype, rows=None, tm_pref=1024, tn_pref=1024, tk_pref=K_TILE, comm=None):
    k = a.shape[0] if rows is None else rows
    m, n = a.shape[1], b.shape[1]
    tm, tn, tk = _tile(m, tm_pref), _tile(n, tn_pref), _tile(k, tk_pref)
    return _mm(name, a, pl.BlockSpec((tk, tm), lambda i, j, kk: (kk, i)),
               [(b, pl.BlockSpec((tk, tn), lambda i, j, kk: (kk, j)))], TN, (m // tm, n // tn, k // tk),
               [jax.ShapeDtypeStruct((m, n), out_dtype)], [pl.BlockSpec((tm, tn), lambda i, j, kk: (i, j))],
               (tm, tn), _plain, comm=comm)


def _ffn_in(name, u, w_in, comm=None):
    r, d = u.shape
    f = w_in.shape[1] // 2
    tm, tn, tk = _tile(r, 1024), _tile(f, 512), _tile(d, K_TILE)
    nf = f // tn

    def epi(accs, _):
        za, zb = accs
        s = za * _sigmoid(za) * zb
        return (za, zb), s

    return _mm(name, u, pl.BlockSpec((tm, tk), lambda i, j, kk: (i, kk)),
               [(w_in, pl.BlockSpec((tk, tn), lambda i, j, kk: (kk, j))),
                (w_in, pl.BlockSpec((tk, tn), lambda i, j, kk: (kk, j + nf)))],
               NN, (r // tm, nf, d // tk),
               [jax.ShapeDtypeStruct((2, r, f), BF16), jax.ShapeDtypeStruct((r, f), BF16)],
               [pl.BlockSpec((2, tm, tn), lambda i, j, kk: (0, i, j)), pl.BlockSpec((tm, tn), lambda i, j, kk: (i, j))],
               (tm, tn), epi, comm=comm)


def _mm_residual(name, a, w, res, gate, gate_scale, lx, comm=None):
    r, k = a.shape
    n = w.shape[1]
    tm, tn, tk = _tile(r, 1024), _tile(n, 1024), _tile(k, K_TILE)

    def epi(accs, e_refs):
        res_ref, g_ref = e_refs
        rows = pl.program_id(0) * tm + lax.broadcasted_iota(jnp.int32, (tm, 1), 0)
        g = jnp.where(rows < lx, g_ref[1], g_ref[0])
        return res_ref[...] + gate_scale * g * accs[0], accs[0]

    return _mm(name, a, pl.BlockSpec((tm, tk), lambda i, j, kk: (i, kk)),
               [(w, pl.BlockSpec((tk, tn), lambda i, j, kk: (kk, j)))], NN, (r // tm, n // tn, k // tk),
               [jax.ShapeDtypeStruct((r, n), F32), jax.ShapeDtypeStruct((r, n), BF16)],
               [pl.BlockSpec((tm, tn), lambda i, j, kk: (i, j))] * 2, (tm, tn), epi,
               extras=[(res, pl.BlockSpec((tm, tn), lambda i, j, kk: (i, j))),
                       (gate, pl.BlockSpec((2, 1, tn), lambda i, j, kk: (0, 0, j)))], comm=comm)


def _ffn_out_bwd(name, df, w_out, z, comm=None):
    r, d = df.shape
    f = w_out.shape[0]
    tm, tn, tk = _tile(r, M_TILE_BIG), _tile(f, 512), _tile(d, K_TILE)

    def epi(accs, e_refs):
        ds = accs[0]
        za = e_refs[0][0].astype(F32)
        zb = e_refs[0][1].astype(F32)
        sg = _sigmoid(za)
        da = ds * zb * sg * (1.0 + za * (1.0 - sg))
        db = ds * za * sg
        return ((da, db),)

    zspec = pl.BlockSpec((2, tm, tn), lambda i, j, kk: (0, i, j))
    return _mm(name, df, pl.BlockSpec((tm, tk), lambda i, j, kk: (i, kk)),
               [(w_out, pl.BlockSpec((tn, tk), lambda i, j, kk: (j, kk)))], NT, (r // tm, f // tn, d // tk),
               [jax.ShapeDtypeStruct((2, r, f), BF16)], [zspec], (tm, tn), epi, extras=[(z, zspec)], comm=comm)


def _ffn_in_bwd_x(name, dz, w_in, comm=None):
    _, r, f = dz.shape
    d = w_in.shape[0]
    tm, tn, tk = _tile(r, 1024), _tile(d, 2048), _tile(f, K_TILE)
    nkf = f // tk
    return _mm(name, dz, pl.BlockSpec((None, tm, tk), lambda i, j, kk: (kk // nkf, i, kk % nkf)),
               [(w_in, pl.BlockSpec((tn, tk), lambda i, j, kk: (j, kk)))], NT, (r // tm, d // tn, 2 * nkf),
               [jax.ShapeDtypeStruct((r, d), F32)], [pl.BlockSpec((tm, tn), lambda i, j, kk: (i, j))],
               (tm, tn), _plain, comm=comm)


def _ffn_in_bwd_w(name, u, dz, comm=None):
    r, d = u.shape
    f = dz.shape[2]
    tm, tn, tk = _tile(d, M_TILE_DW), _tile(f, 512), _tile(r, K_TILE)
    nf = f // tn
    return _mm(name, u, pl.BlockSpec((tk, tm), lambda i, j, kk: (kk, i)),
               [(dz, pl.BlockSpec((None, tk, tn), lambda i, j, kk: (j // nf, kk, j % nf)))], TN,
               (d // tm, 2 * nf, r // tk),
               [jax.ShapeDtypeStruct((d, 2 * f), BF16)], [pl.BlockSpec((tm, tn), lambda i, j, kk: (i, j))],
               (tm, tn), _plain, comm=comm)


def _merge_bwd(name, dout, w_out, pa, pr, p, ga_off, gb_off):
    r, d = dout.shape
    n = w_out.shape[0]
    cw = math.gcd(math.gcd(ga_off, gb_off), n)
    tm, tn, tk = _tile(r, 1024), _tile(cw, 512), _tile(d, K_TILE)

    def epi(accs, e_refs):
        dm = accs[0]
        pa_ref, pr_ref, ga_ref, gb_ref = e_refs
        sa = _sigmoid(ga_ref[...].astype(F32))
        sb = _sigmoid(gb_ref[...].astype(F32))
        pav = pa_ref[...].astype(F32)
        prv = pr_ref[...].astype(F32)
        return dm * sa, dm * sb, dm * pav * sa * (1.0 - sa), dm * prv * sb * (1.0 - sb)

    o_spec = pl.BlockSpec((tm, tn), lambda i, j, kk: (i, j))
    return _mm(name, dout, pl.BlockSpec((tm, tk), lambda i, j, kk: (i, kk)),
               [(w_out, pl.BlockSpec((tn, tk), lambda i, j, kk: (j, kk)))], NT, (r // tm, n // tn, d // tk),
               [jax.ShapeDtypeStruct((r, n), BF16)] * 4, [o_spec] * 4, (tm, tn), epi,
               extras=[(pa, o_spec), (pr, o_spec),
                       (p, pl.BlockSpec((tm, tn), lambda i, j, kk: (i, ga_off // tn + j))),
                       (p, pl.BlockSpec((tm, tn), lambda i, j, kk: (i, gb_off // tn + j)))])


def _row_tile(r, lx, d):
    pref = 256 if d > 1024 else 512
    return _tile(math.gcd(r, lx), pref, SUBLANES)


def _rmsmod(name, h, scale, shift, lx, rows=None):
    r = h.shape[0] if rows is None else rows
    d = h.shape[1]
    tr = _row_tile(r, lx, d)
    nx = lx // tr
    cls = lambda i: (jnp.where(i < nx, 1, 0), 0, 0)

    def body(h_ref, sc_ref, sh_ref, u_ref):
        hv = h_ref[...]
        rinv = lax.rsqrt(jnp.mean(hv * hv, axis=-1, keepdims=True) + NORM_EPS)
        u_ref[...] = (hv * rinv * (1.0 + sc_ref[...]) + sh_ref[...]).astype(u_ref.dtype)

    return pl.pallas_call(
        body, name=name, grid=(r // tr,),
        in_specs=[pl.BlockSpec((tr, d), lambda i: (i, 0)), pl.BlockSpec((None, 1, d), cls), pl.BlockSpec((None, 1, d), cls)],
        out_specs=pl.BlockSpec((tr, d), lambda i: (i, 0)), out_shape=jax.ShapeDtypeStruct((r, d), BF16),
        compiler_params=_cp(("parallel",)))(h, scale, shift)


def _rmsmod_bwd(name, du, h, scale, dh_in, lx, rows_out):
    r, d = du.shape
    rin = dh_in.shape[0]
    tr = _row_tile(math.gcd(r, math.gcd(rin, rows_out)), lx, d)
    nx, nin, nout = lx // tr, rin // tr, rows_out // tr
    cls = lambda i: (jnp.where(i < nx, 1, 0), 0, 0)

    def body(du_ref, h_ref, sc_ref, dhin_ref, dh_ref, dsc_ref, dsh_ref):
        i = pl.program_id(0)
        hv = h_ref[...]
        duv = du_ref[...]
        rinv = lax.rsqrt(jnp.mean(hv * hv, axis=-1, keepdims=True) + NORM_EPS)
        nv = hv * rinv
        dn = duv * (1.0 + sc_ref[...])

        @pl.when(jnp.logical_or(i == 0, i == nx))
        def _():
            dsc_ref[...] = jnp.zeros(dsc_ref.shape, F32)
            dsh_ref[...] = jnp.zeros(dsh_ref.shape, F32)

        dsc_ref[...] += jnp.sum(duv * nv, axis=0, keepdims=True)
        dsh_ref[...] += jnp.sum(duv, axis=0, keepdims=True)

        @pl.when(i < nout)
        def _():
            dh = rinv * (dn - nv * jnp.mean(dn * nv, axis=-1, keepdims=True))
            dh_ref[...] = dh + jnp.where(i < nin, dhin_ref[...], 0.0)

    return pl.pallas_call(
        body, name=name, grid=(r // tr,),
        in_specs=[pl.BlockSpec((tr, d), lambda i: (i, 0)), pl.BlockSpec((tr, d), lambda i: (i, 0)),
                  pl.BlockSpec((None, 1, d), cls), pl.BlockSpec((tr, d), lambda i: (jnp.minimum(i, nin - 1), 0))],
        out_specs=[pl.BlockSpec((tr, d), lambda i: (jnp.minimum(i, nout - 1), 0)),
                   pl.BlockSpec((None, 1, d), cls), pl.BlockSpec((None, 1, d), cls)],
        out_shape=[jax.ShapeDtypeStruct((rows_out, d), F32), jax.ShapeDtypeStruct((2, 1, d), F32),
                   jax.ShapeDtypeStruct((2, 1, d), F32)],
        compiler_params=_cp(("arbitrary",)))(du, h, scale, dh_in)


def _gate_bwd(name, dh, f, gate, gate_scale, lx):
    r, d = dh.shape
    tr = _row_tile(r, lx, d)
    nx = lx // tr
    cls = lambda i: (jnp.where(i < nx, 1, 0), 0, 0)

    def body(dh_ref, f_ref, g_ref, df_ref, dg_ref):
        i = pl.program_id(0)
        dhv = dh_ref[...]

        @pl.when(jnp.logical_or(i == 0, i == nx))
        def _():
            dg_ref[...] = jnp.zeros(dg_ref.shape, F32)

        df_ref[...] = (gate_scale * g_ref[...] * dhv).astype(df_ref.dtype)
        dg_ref[...] += jnp.sum(gate_scale * f_ref[...].astype(F32) * dhv, axis=0, keepdims=True)

    return pl.pallas_call(
        body, name=name, grid=(r // tr,),
        in_specs=[pl.BlockSpec((tr, d), lambda i: (i, 0)), pl.BlockSpec((tr, d), lambda i: (i, 0)),
                  pl.BlockSpec((None, 1, d), cls)],
        out_specs=[pl.BlockSpec((tr, d), lambda i: (i, 0)), pl.BlockSpec((None, 1, d), cls)],
        out_shape=[jax.ShapeDtypeStruct((r, d), BF16), jax.ShapeDtypeStruct((2, 1, d), F32)],
        compiler_params=_cp(("arbitrary",)))(dh, f, gate)


def _final_loss(name, h, final_norm, target):
    r, d = h.shape
    tr = _row_tile(r, r, d)

    def body(h_ref, fn_ref, t_ref, dh_ref, dfn_ref, loss_ref):
        i = pl.program_id(0)
        hv = h_ref[...]
        rinv = lax.rsqrt(jnp.mean(hv * hv, axis=-1, keepdims=True) + NORM_EPS)
        nv = hv * rinv
        fn = fn_ref[...]
        err = nv * fn - t_ref[...]
        dy = err * (1.0 / d)

        @pl.when(i == 0)
        def _():
            dfn_ref[...] = jnp.zeros(dfn_ref.shape, F32)
            loss_ref[...] = jnp.zeros(loss_ref.shape, F32)

        loss_ref[...] += 0.5 * jnp.sum(jnp.mean(err * err, axis=-1, keepdims=True), axis=0, keepdims=True)
        dfn_ref[...] += jnp.sum(dy * nv, axis=0, keepdims=True)
        dn = dy * fn
        dh_ref[...] = rinv * (dn - nv * jnp.mean(dn * nv, axis=-1, keepdims=True))

    return pl.pallas_call(
        body, name=name, grid=(r // tr,),
        in_specs=[pl.BlockSpec((tr, d), lambda i: (i, 0)), pl.BlockSpec((1, d), lambda i: (0, 0)),
                  pl.BlockSpec((tr, d), lambda i: (i, 0))],
        out_specs=[pl.BlockSpec((tr, d), lambda i: (i, 0)), pl.BlockSpec((1, d), lambda i: (0, 0)),
                   pl.BlockSpec((SUBLANES, LANES), lambda i: (0, 0))],
        out_shape=[jax.ShapeDtypeStruct((r, d), F32), jax.ShapeDtypeStruct((1, d), F32),
                   jax.ShapeDtypeStruct((SUBLANES, LANES), F32)],
        compiler_params=_cp(("arbitrary",)))(h, final_norm, target)


def _swap_pairs(t):
    lane = lax.broadcasted_iota(jnp.int32, t.shape, 1)
    nxt = pltpu.roll(t, HEAD_DIM - 1, 1)
    prv = pltpu.roll(t, 1, 1)
    return jnp.where(lane % 2 == 0, nxt, prv)


def _qk_prep(name, p, cosf, sinf, q_gain, k_gain, aw, kw):
    t = p.shape[0]
    tr = _tile(t, 256, SUBLANES)
    q_prescale = HEAD_DIM ** -0.5 * LOG2E

    def head_fwd(v, gain, cs, sn):
        v = v.astype(F32)
        rinv = lax.rsqrt(jnp.mean(v * v, axis=-1, keepdims=True) + NORM_EPS)
        tt = v * rinv * gain
        return tt * cs + _swap_pairs(tt) * sn

    def body(q_ref, k_ref, cos_ref, sin_ref, qg_ref, kg_ref, qo_ref, ko_ref):
        cs, sn = cos_ref[...], sin_ref[...]
        for hh in range(aw // HEAD_DIM):
            sl = slice(hh * HEAD_DIM, (hh + 1) * HEAD_DIM)
            qo_ref[:, sl] = (head_fwd(q_ref[:, sl], qg_ref[...], cs, sn) * q_prescale).astype(qo_ref.dtype)
        for hh in range(kw // HEAD_DIM):
            sl = slice(hh * HEAD_DIM, (hh + 1) * HEAD_DIM)
            ko_ref[:, sl] = head_fwd(k_ref[:, sl], kg_ref[...], cs, sn).astype(ko_ref.dtype)

    assert aw % kw == 0
    row = lambda i: (i, 0)
    return pl.pallas_call(
        body, name=name, grid=(t // tr,),
        in_specs=[pl.BlockSpec((tr, aw), row), pl.BlockSpec((tr, kw), lambda i: (i, aw // kw)),
                  pl.BlockSpec((tr, HEAD_DIM), row), pl.BlockSpec((tr, HEAD_DIM), row),
                  pl.BlockSpec((1, HEAD_DIM), lambda i: (0, 0)), pl.BlockSpec((1, HEAD_DIM), lambda i: (0, 0))],
        out_specs=[pl.BlockSpec((tr, aw), row), pl.BlockSpec((tr, kw), row)],
        out_shape=[jax.ShapeDtypeStruct((t, aw), BF16), jax.ShapeDtypeStruct((t, kw), BF16)],
        compiler_params=_cp(("parallel",)))(p, p, cosf, sinf, q_gain, k_gain)


def _qk_prep_bwd(name, dq_rot, dk_rot, p, cosf, sinf, q_gain, k_gain, aw, kw):
    t = p.shape[0]
    lq = dq_rot.shape[0]
    tr = _tile(math.gcd(t, lq), 256, SUBLANES)
    nq = lq // tr

    def head_bwd(dout, v, gain, cs, sn):
        v = v.astype(F32)
        rinv = lax.rsqrt(jnp.mean(v * v, axis=-1, keepdims=True) + NORM_EPS)
        vn = v * rinv
        dt = dout * cs - _swap_pairs(dout) * sn
        dvn = dt * gain
        dv = rinv * (dvn - vn * jnp.mean(dvn * vn, axis=-1, keepdims=True))
        return dv, jnp.sum(dt * vn, axis=0, keepdims=True)

    def body(dq_ref, dk_ref, q_ref, k_ref, cos_ref, sin_ref, qg_ref, kg_ref, dqo_ref, dko_ref, dqg_ref, dkg_ref):
        i = pl.program_id(0)
        cs, sn = cos_ref[...], sin_ref[...]

        @pl.when(i == 0)
        def _():
            dqg_ref[...] = jnp.zeros(dqg_ref.shape, F32)
            dkg_ref[...] = jnp.zeros(dkg_ref.shape, F32)

        has_q = i < nq
        for hh in range(aw // HEAD_DIM):
            sl = slice(hh * HEAD_DIM, (hh + 1) * HEAD_DIM)
            dout = jnp.where(has_q, dq_ref[:, sl], 0.0)
            dv, dg = head_bwd(dout, q_ref[:, sl], qg_ref[...], cs, sn)
            dqo_ref[:, sl] = dv.astype(dqo_ref.dtype)
            dqg_ref[...] += dg
        for hh in range(kw // HEAD_DIM):
            sl = slice(hh * HEAD_DIM, (hh + 1) * HEAD_DIM)
            dv, dg = head_bwd(dk_ref[:, sl], k_ref[:, sl], kg_ref[...], cs, sn)
            dko_ref[:, sl] = dv.astype(dko_ref.dtype)
            dkg_ref[...] += dg

    row = lambda i: (i, 0)
    one = lambda i: (0, 0)
    return pl.pallas_call(
        body, name=name, grid=(t // tr,),
        in_specs=[pl.BlockSpec((tr, aw), lambda i: (jnp.minimum(i, nq - 1), 0)), pl.BlockSpec((tr, kw), row),
                  pl.BlockSpec((tr, aw), row), pl.BlockSpec((tr, kw), lambda i: (i, aw // kw)),
                  pl.BlockSpec((tr, HEAD_DIM), row), pl.BlockSpec((tr, HEAD_DIM), row),
                  pl.BlockSpec((1, HEAD_DIM), one), pl.BlockSpec((1, HEAD_DIM), one)],
        out_specs=[pl.BlockSpec((tr, aw), row), pl.BlockSpec((tr, kw), row),
                   pl.BlockSpec((1, HEAD_DIM), one), pl.BlockSpec((1, HEAD_DIM), one)],
        out_shape=[jax.ShapeDtypeStruct((t, aw), BF16), jax.ShapeDtypeStruct((t, kw), BF16),
                   jax.ShapeDtypeStruct((1, HEAD_DIM), F32), jax.ShapeDtypeStruct((1, HEAD_DIM), F32)],
        compiler_params=_cp(("arbitrary",)))(dq_rot, dk_rot, p, p, cosf, sinf, q_gain, k_gain)


def _attn_tiles(lq, t):
    return _tile(lq, 1024), _tile(t, 768)


def _attn_fwd(name, q, k, p, v_off, lq, groups):
    t = k.shape[0]
    hq = q.shape[1] // HEAD_DIM
    hkv = hq // groups
    gw = groups * HEAD_DIM
    tq, tk = _attn_tiles(lq, t)
    nkv = t // tk
    vb = v_off // HEAD_DIM

    def body(q_ref, k_ref, v_ref, o_ref, lse_ref, m_sc, l_sc, acc_sc, s_sc):
        j = pl.program_id(2)

        @pl.when(j == 0)
        def _():
            m_sc[...] = jnp.full(m_sc.shape, -jnp.inf, F32)
            l_sc[...] = jnp.zeros(l_sc.shape, F32)
            acc_sc[...] = jnp.zeros(acc_sc.shape, F32)
            s_sc[...] = jnp.zeros(s_sc.shape, F32)

        has_prev = j > 0
        kv, vv = k_ref[...], v_ref[...]
        for g in range(groups):
            sl = slice(g * HEAD_DIM, (g + 1) * HEAD_DIM)
            s = s_sc[g]
            s_sc[g] = _dot(q_ref[:, sl], kv, NT)
            m_prev = m_sc[g]
            m_new = jnp.where(has_prev, jnp.maximum(m_prev, jnp.max(s, axis=-1, keepdims=True)), jnp.inf)
            alpha = jnp.exp2(m_prev - m_new)
            pexp = jnp.exp2(s - m_new)
            part = pexp[:, 0:LANES]
            for cb in range(1, tk // LANES):
                part = part + pexp[:, cb * LANES:(cb + 1) * LANES]
            l_sc[g] = alpha * l_sc[g] + part
            acc_sc[:, sl] = alpha * acc_sc[:, sl] + _dot(pexp.astype(BF16), vv, NN)
            m_sc[g] = jnp.where(has_prev, m_new, -jnp.inf)

        @pl.when(j == nkv)
        def _():
            for g in range(groups):
                sl = slice(g * HEAD_DIM, (g + 1) * HEAD_DIM)
                l_row = jnp.sum(l_sc[g], axis=-1, keepdims=True)
                o_ref[:, sl] = (acc_sc[:, sl] * (1.0 / l_row)).astype(o_ref.dtype)
                lse_ref[g] = jnp.broadcast_to(m_sc[g] + jnp.log2(l_row), (tq, LANES))

    return pl.pallas_call(
        body, name=name, grid=(hkv, lq // tq, nkv + 1),
        in_specs=[pl.BlockSpec((tq, gw), lambda h, i, j: (i, h)),
                  pl.BlockSpec((tk, HEAD_DIM), lambda h, i, j: (jnp.minimum(j, nkv - 1), h)),
                  pl.BlockSpec((tk, HEAD_DIM), lambda h, i, j: (jnp.maximum(j - 1, 0), vb + h))],
        out_specs=[pl.BlockSpec((tq, gw), lambda h, i, j: (i, h)),
                   pl.BlockSpec((groups, tq, LANES), lambda h, i, j: (h, i, 0))],
        out_shape=[jax.ShapeDtypeStruct((lq, hq * HEAD_DIM), BF16), jax.ShapeDtypeStruct((hq, lq, LANES), F32)],
        scratch_shapes=[pltpu.VMEM((groups, tq, 1), F32), pltpu.VMEM((groups, tq, LANES), F32), pltpu.VMEM((tq, gw), F32),
                        pltpu.VMEM((groups, tq, tk), F32)],
        compiler_params=_cp(("parallel", "parallel", "arbitrary")))(q, k, p)


def _attn_delta(name, o, do):
    lq, aw = o.shape
    hq = aw // HEAD_DIM
    tr = _tile(lq, 512, SUBLANES)

    def body(o_ref, do_ref, d_ref):
        for h in range(hq):
            sl = slice(h * HEAD_DIM, (h + 1) * HEAD_DIM)
            dsum = jnp.sum(do_ref[:, sl].astype(F32) * o_ref[:, sl].astype(F32), axis=-1, keepdims=True)
            d_ref[h] = jnp.broadcast_to(dsum, (tr, LANES))

    spec = pl.BlockSpec((tr, aw), lambda i: (i, 0))
    return pl.pallas_call(
        body, name=name, grid=(lq // tr,), in_specs=[spec, spec],
        out_specs=pl.BlockSpec((hq, tr, LANES), lambda i: (0, i, 0)),
        out_shape=jax.ShapeDtypeStruct((hq, lq, LANES), F32), compiler_params=_cp(("parallel",)))(o, do)


def _attn_bwd(name, q, k, p, v_off, do, lse, delta, lq, groups):
    t = k.shape[0]
    hkv = k.shape[1] // HEAD_DIM
    gw = groups * HEAD_DIM
    tq, tk = _attn_tiles(lq, t)
    nq, nkv = lq // tq, t // tk
    scale = HEAD_DIM ** -0.5
    vb = v_off // HEAD_DIM

    def body(q_ref, k_ref, v_ref, do_ref, lse_ref, delta_ref, dq_ref, dk_ref, dv_ref, dq_sc, dk_sc, dv_sc):
        j, i = pl.program_id(1), pl.program_id(2)

        @pl.when(i == 0)
        def _():
            dk_sc[...] = jnp.zeros(dk_sc.shape, F32)
            dv_sc[...] = jnp.zeros(dv_sc.shape, F32)

        @pl.when(j == 0)
        def _():
            dq_sc[i] = jnp.zeros((tq, gw), F32)

        kv, vv = k_ref[...], v_ref[...]
        dk_part, dv_part = None, None
        for g in range(groups):
            sl = slice(g * HEAD_DIM, (g + 1) * HEAD_DIM)
            qv, dov = q_ref[:, sl], do_ref[:, sl]
            s = _dot(qv, kv, NT)
            pexp = jnp.exp2(s - lse_ref[g, :, 0:1])
            dv_g = _dot(pexp.astype(BF16), dov, TN)
            dp = _dot(dov, vv, NT)
            ds = (pexp * (dp - delta_ref[g, :, 0:1])).astype(BF16)
            dk_g = _dot(ds, qv, TN)
            dq_sc[i, :, sl] += _dot(ds, kv, NN)
            dk_part = dk_g if dk_part is None else dk_part + dk_g
            dv_part = dv_g if dv_part is None else dv_part + dv_g
        dk_sc[...] += dk_part
        dv_sc[...] += dv_part

        @pl.when(i == nq - 1)
        def _():
            dk_ref[...] = dk_sc[...] * LN2
            dv_ref[...] = dv_sc[...].astype(dv_ref.dtype)

        @pl.when(j == nkv - 1)
        def _():
            dq_ref[...] = dq_sc[i] * scale

    qspec = pl.BlockSpec((tq, gw), lambda kh, j, i: (i, kh))
    kspec = pl.BlockSpec((tk, HEAD_DIM), lambda kh, j, i: (j, kh))
    rowspec = pl.BlockSpec((groups, tq, LANES), lambda kh, j, i: (kh, i, 0))
    dqspec = pl.BlockSpec((tq, gw), lambda kh, j, i: (jnp.where(j == nkv - 1, i, 0), kh))
    return pl.pallas_call(
        body, name=name, grid=(hkv, nkv, nq),
        in_specs=[qspec, kspec, pl.BlockSpec((tk, HEAD_DIM), lambda kh, j, i: (j, vb + kh)), qspec, rowspec, rowspec],
        out_specs=[dqspec, kspec, kspec],
        out_shape=[jax.ShapeDtypeStruct((lq, hkv * gw), F32), jax.ShapeDtypeStruct((t, hkv * HEAD_DIM), F32),
                   jax.ShapeDtypeStruct((t, hkv * HEAD_DIM), BF16)],
        scratch_shapes=[pltpu.VMEM((nq, tq, gw), F32), pltpu.VMEM((tk, HEAD_DIM), F32), pltpu.VMEM((tk, HEAD_DIM), F32)],
        compiler_params=_cp(("parallel", "arbitrary", "arbitrary")))(q, k, p, do, lse, delta)


def _ret_tables(log_gamma, direction):
    c = RET_CHUNK
    idx = jnp.arange(c, dtype=F32)
    diff = idx[:, None] - idx[None, :]
    if direction == 1:
        diff = -diff
    keep = diff >= 0
    lg = log_gamma.astype(F32)
    mask = jnp.where(keep[None], jnp.exp(jnp.where(keep, diff, 0.0)[None] * lg[:, None, None]), 0.0)
    q_exp = idx + 1.0 if direction == 0 else c - idx
    k_exp = c - 1.0 - idx if direction == 0 else idx
    sign = 1.0 if direction == 0 else -1.0
    lane = lambda v: jnp.broadcast_to(v[..., None], v.shape + (LANES,))
    qdec = lane(jnp.exp(q_exp[None, :] * lg[:, None]))
    kdec = lane(jnp.exp(k_exp[None, :] * lg[:, None]))
    cdec = jnp.broadcast_to(jnp.exp(c * lg)[:, None, None], (lg.shape[0], SUBLANES, LANES))
    weights = lane(jnp.stack([sign * idx, q_exp, -sign * idx, k_exp], axis=0))
    return mask, qdec, kdec, cdec, weights


def _ret_chunk_of(direction, step, nx, nc):
    if direction == 0:
        return jnp.where(step < nc, nx + step, step - nc)
    return jnp.where(step < nc, nx + nc - 1 - step, nx + nc - 1 - step)


def _ret_fwd(name, p, offs, tables, direction, lx, prev):
    q_off, k_off, v_off, rw = offs
    t = p.shape[0]
    c = RET_CHUNK
    n_steps, nx = t // c, lx // c
    nc = n_steps - nx
    bw = math.gcd(math.gcd(q_off, k_off), math.gcd(v_off, rw))
    bw = _tile(bw, 512)
    hpb = bw // HEAD_DIM
    heads = rw // HEAD_DIM
    k_scale = HEAD_DIM ** -0.5
    mask, qdec, kdec, cdec, _ = tables
    rc = lambda n: _ret_chunk_of(direction, n, nx, nc)

    ng = heads // hpb

    def body(*refs):
        q_refs, k_refs, v_refs = refs[0:ng], refs[ng:2 * ng], refs[2 * ng:3 * ng]
        if prev is None:
            m_ref, qd_ref, kd_ref, cd_ref, y_ref, st_ref, s_sc = refs[3 * ng:]
        else:
            m_ref, qd_ref, kd_ref, cd_ref, prev_ref, y_ref, st_ref, s_sc = refs[3 * ng:]
        n = pl.program_id(0)

        @pl.when(n == 0)
        def _():
            s_sc[...] = jnp.zeros(s_sc.shape, F32)

        for hd in range(heads):
            gg, hh = divmod(hd, hpb)
            sl = slice(hh * HEAD_DIM, (hh + 1) * HEAD_DIM)
            osl = slice(hd * HEAD_DIM, (hd + 1) * HEAD_DIM)
            qv = q_refs[gg][:, sl]
            kf = k_refs[gg][:, sl].astype(F32) * k_scale
            kv = kf.astype(BF16)
            vv = v_refs[gg][:, sl]
            state = s_sc[hd]
            st_ref[hd] = state
            a = _dot(qv, kv, NT) * m_ref[hd]
            y = _dot(a.astype(BF16), vv, NN) + _dot(qv, state.astype(BF16), NN) * qd_ref[hd]
            s_sc[hd] = state * cd_ref[hd, 0:1, :] + _dot((kf * kd_ref[hd]).astype(BF16), vv, TN)
            if prev is not None:
                y = y + prev_ref[:, osl]
            y_ref[:, osl] = y

    col = lambda off, gg: (lambda n: (rc(n), off // bw + gg))
    tab3 = lambda n: (0, 0, 0)
    in_specs = [pl.BlockSpec((c, bw), col(off, gg)) for off in (q_off, k_off, v_off) for gg in range(ng)]
    in_specs += [pl.BlockSpec((heads, c, c), tab3), pl.BlockSpec((heads, c, LANES), tab3),
                 pl.BlockSpec((heads, c, LANES), tab3), pl.BlockSpec((heads, SUBLANES, LANES), tab3)]
    args = [p] * (3 * ng) + [mask, qdec, kdec, cdec]
    aliases = {}
    yspec = pl.BlockSpec((c, rw), lambda n: (rc(n), 0))
    if prev is not None:
        in_specs.append(yspec)
        args.append(prev)
        aliases = {len(args) - 1: 0}
    return pl.pallas_call(
        body, name=name, grid=(n_steps,), in_specs=in_specs,
        out_specs=[yspec, pl.BlockSpec((None, heads, HEAD_DIM, HEAD_DIM), lambda n: (n, 0, 0, 0))],
        out_shape=[jax.ShapeDtypeStruct((t, rw), F32), jax.ShapeDtypeStruct((n_steps, heads, HEAD_DIM, HEAD_DIM), F32)],
        scratch_shapes=[pltpu.VMEM((heads, HEAD_DIM, HEAD_DIM), F32)], input_output_aliases=aliases,
        compiler_params=_cp(("arbitrary",)))(*args)


def _ret_bwd(name, p, offs, tables, states, dy, direction, lx, prev):
    q_off, k_off, v_off, rw = offs
    t = p.shape[0]
    c = RET_CHUNK
    n_steps, nx = t // c, lx // c
    nc = n_steps - nx
    bw = math.gcd(math.gcd(q_off, k_off), math.gcd(v_off, rw))
    bw = _tile(bw, 512)
    hpb = bw // HEAD_DIM
    heads = rw // HEAD_DIM
    k_scale = HEAD_DIM ** -0.5
    mask, qdec, kdec, cdec, weights = tables
    step_of = lambda n: n_steps - 1 - n
    rc = lambda n: _ret_chunk_of(direction, step_of(n), nx, nc)

    ng = heads // hpb

    def body(*refs):
        q_refs, k_refs, v_refs = refs[0:ng], refs[ng:2 * ng], refs[2 * ng:3 * ng]
        if prev is None:
            (dy_ref, st_ref, m_ref, qd_ref, kd_ref, cd_ref, w_ref,
             dq_ref, dk_ref, dv_ref, dl_ref, ds_sc, lam_sc) = refs[3 * ng:]
        else:
            (dy_ref, st_ref, m_ref, qd_ref, kd_ref, cd_ref, w_ref, pq_ref, pk_ref, pv_ref,
             dq_ref, dk_ref, dv_ref, dl_ref, ds_sc, lam_sc) = refs[3 * ng:]
        n = pl.program_id(0)

        @pl.when(n == 0)
        def _():
            ds_sc[...] = jnp.zeros(ds_sc.shape, F32)
            lam_sc[...] = jnp.zeros(lam_sc.shape, F32)

        is_x = rc(n) < nx
        for hd in range(heads):
            gg, hh = divmod(hd, hpb)
            sl = slice(hh * HEAD_DIM, (hh + 1) * HEAD_DIM)
            osl = slice(hd * HEAD_DIM, (hd + 1) * HEAD_DIM)
            qv = q_refs[gg][:, sl]
            qf = qv.astype(F32)
            kf = k_refs[gg][:, sl].astype(F32) * k_scale
            kv = kf.astype(BF16)
            vv = v_refs[gg][:, sl]
            dyv = jnp.where(is_x, dy_ref[:, osl], 0.0).astype(BF16)
            state = st_ref[hd]
            dstate = ds_sc[hd]
            dstate_b = dstate.astype(BF16)
            msk, qd, kd = m_ref[hd], qd_ref[hd], kd_ref[hd]
            cd = cd_ref[hd, 0:1, :]
            a = _dot(qv, kv, NT) * msk
            da = (_dot(dyv, vv, NT) * msk).astype(BF16)
            dq_intra = _dot(da, kv, NN)
            dk_intra = _dot(da, qv, TN)
            dq_inter = _dot(dyv, state.astype(BF16), NT) * qd
            dk_inter = _dot(vv, dstate_b, NT) * kd
            dv = _dot(a.astype(BF16), dyv, TN) + _dot((kf * kd).astype(BF16), dstate_b, NN)
            lam_sc[hd] += (qf * (w_ref[0] * dq_intra + w_ref[1] * dq_inter)
                           + kf * (w_ref[2] * dk_intra + w_ref[3] * dk_inter)
                           + (c * cd) * state * dstate)
            ds_sc[hd] = _dot((qf * qd).astype(BF16), dyv, TN) + cd * dstate
            dq = dq_intra + dq_inter
            dk = (dk_intra + dk_inter) * k_scale
            if prev is not None:
                dq = dq + pq_ref[:, osl]
                dk = dk + pk_ref[:, osl]
                dv = dv + pv_ref[:, osl]
            dq_ref[:, osl] = dq
            dk_ref[:, osl] = dk
            dv_ref[:, osl] = dv

        @pl.when(n == n_steps - 1)
        def _():
            for hd in range(heads):
                dl_ref[hd] = jnp.broadcast_to(jnp.sum(lam_sc[hd]), (SUBLANES, LANES))

    col = lambda off, gg: (lambda n: (rc(n), off // bw + gg))
    tab3 = lambda n: (0, 0, 0)
    ospec = pl.BlockSpec((c, rw), lambda n: (rc(n), 0))
    in_specs = [pl.BlockSpec((c, bw), col(off, gg)) for off in (q_off, k_off, v_off) for gg in range(ng)]
    in_specs += [pl.BlockSpec((c, rw), lambda n: (jnp.minimum(rc(n), nx - 1), 0)),
                 pl.BlockSpec((None, heads, HEAD_DIM, HEAD_DIM), lambda n: (step_of(n), 0, 0, 0)),
                 pl.BlockSpec((heads, c, c), tab3), pl.BlockSpec((heads, c, LANES), tab3), pl.BlockSpec((heads, c, LANES), tab3),
                 pl.BlockSpec((heads, SUBLANES, LANES), tab3), pl.BlockSpec((4, c, LANES), tab3)]
    args = [p] * (3 * ng) + [dy, states, mask, qdec, kdec, cdec, weights]
    aliases = {}
    if prev is not None:
        for k_out, arr in enumerate(prev):
            in_specs.append(ospec)
            args.append(arr)
            aliases[len(args) - 1] = k_out
    big = jax.ShapeDtypeStruct((t, rw), F32)
    return pl.pallas_call(
        body, name=name, grid=(n_steps,), in_specs=in_specs,
        out_specs=[ospec, ospec, ospec, pl.BlockSpec((heads, SUBLANES, LANES), tab3)],
        out_shape=[big, big, big, jax.ShapeDtypeStruct((heads, SUBLANES, LANES), F32)],
        scratch_shapes=[pltpu.VMEM((heads, HEAD_DIM, HEAD_DIM), F32), pltpu.VMEM((heads, HEAD_DIM, HEAD_DIM), F32)],
        input_output_aliases=aliases, compiler_params=_cp(("arbitrary",)))(*args)


def _ret_out(name, y, p, g_off, lx):
    rw = y.shape[1]
    bw = _tile(math.gcd(g_off, rw), 512)
    tr = _tile(lx, 512, SUBLANES)

    def body(y_ref, g_ref, o_ref):
        for hh in range(bw // HEAD_DIM):
            sl = slice(hh * HEAD_DIM, (hh + 1) * HEAD_DIM)
            yv = y_ref[:, sl]
            gv = g_ref[:, sl].astype(F32)
            rinv = lax.rsqrt(jnp.mean(yv * yv, axis=-1, keepdims=True) + NORM_EPS)
            o_ref[:, sl] = (gv * _sigmoid(gv) * yv * rinv).astype(o_ref.dtype)

    spec = pl.BlockSpec((tr, bw), lambda i, g: (i, g))
    return pl.pallas_call(
        body, name=name, grid=(lx // tr, rw // bw),
        in_specs=[spec, pl.BlockSpec((tr, bw), lambda i, g: (i, g_off // bw + g))],
        out_specs=spec, out_shape=jax.ShapeDtypeStruct((lx, rw), BF16),
        compiler_params=_cp(("parallel", "parallel")))(y, p)


def _ret_out_bwd(name, dyr, y, p, g_off, lx):
    rw = y.shape[1]
    bw = _tile(math.gcd(g_off, rw), 512)
    tr = _tile(lx, 512, SUBLANES)

    def body(d_ref, y_ref, g_ref, dy_ref, dg_ref):
        for hh in range(bw // HEAD_DIM):
            sl = slice(hh * HEAD_DIM, (hh + 1) * HEAD_DIM)
            yv = y_ref[:, sl]
            gv = g_ref[:, sl].astype(F32)
            dv = d_ref[:, sl]
            rinv = lax.rsqrt(jnp.mean(yv * yv, axis=-1, keepdims=True) + NORM_EPS)
            yn = yv * rinv
            sg = _sigmoid(gv)
            dg_ref[:, sl] = (dv * yn * sg * (1.0 + gv * (1.0 - sg))).astype(dg_ref.dtype)
            dyn = dv * gv * sg
            dy_ref[:, sl] = rinv * (dyn - yn * jnp.mean(dyn * yn, axis=-1, keepdims=True))

    spec = pl.BlockSpec((tr, bw), lambda i, g: (i, g))
    return pl.pallas_call(
        body, name=name, grid=(lx // tr, rw // bw),
        in_specs=[spec, spec, pl.BlockSpec((tr, bw), lambda i, g: (i, g_off // bw + g))],
        out_specs=[spec, spec],
        out_shape=[jax.ShapeDtypeStruct((lx, rw), F32), jax.ShapeDtypeStruct((lx, rw), BF16)],
        compiler_params=_cp(("parallel", "parallel")))(dyr, y, p)


def _merge(name, pa, pr, p, ga_off, gb_off):
    r, d = pa.shape
    cw = _tile(math.gcd(math.gcd(ga_off, gb_off), d), 1024)
    tr = _tile(r, 512, SUBLANES)

    def body(pa_ref, pr_ref, ga_ref, gb_ref, o_ref):
        o_ref[...] = (_sigmoid(ga_ref[...].astype(F32)) * pa_ref[...].astype(F32)
                      + _sigmoid(gb_ref[...].astype(F32)) * pr_ref[...].astype(F32)).astype(o_ref.dtype)

    spec = pl.BlockSpec((tr, cw), lambda i, j: (i, j))
    return pl.pallas_call(
        body, name=name, grid=(r // tr, d // cw),
        in_specs=[spec, spec, pl.BlockSpec((tr, cw), lambda i, j: (i, ga_off // cw + j)),
                  pl.BlockSpec((tr, cw), lambda i, j: (i, gb_off // cw + j))],
        out_specs=spec, out_shape=jax.ShapeDtypeStruct((r, d), BF16),
        compiler_params=_cp(("parallel", "parallel")))(pa, pr, p, p)


def _ada_fwd(name, cond, w):
    rows, d = cond.shape
    n = w.shape[1]
    tn = _tile(n, 768)

    def body(c_ref, w_ref, o_ref, s_ref):
        cv = c_ref[...]
        sv = cv * _sigmoid(cv)
        s_ref[...] = sv
        o_ref[...] = _dot(sv.astype(BF16), w_ref[...].astype(BF16), NN)

    return pl.pallas_call(
        body, name=name, grid=(n // tn,),
        in_specs=[pl.BlockSpec((rows, d), lambda j: (0, 0)), pl.BlockSpec((d, tn), lambda j: (0, j))],
        out_specs=[pl.BlockSpec((rows, tn), lambda j: (0, j)), pl.BlockSpec((rows, d), lambda j: (0, 0))],
        out_shape=[jax.ShapeDtypeStruct((rows, n), F32), jax.ShapeDtypeStruct((rows, d), F32)],
        compiler_params=_cp(("arbitrary",)))(cond, w)


def _ada_bwd(name, s_cond, dmod, w):
    rows, d = s_cond.shape
    n = w.shape[1]
    tn = _tile(n, 768)

    def body(s_ref, dm_ref, w_ref, gw_ref, ds_ref):
        j = pl.program_id(0)

        @pl.when(j == 0)
        def _():
            ds_ref[...] = jnp.zeros(ds_ref.shape, F32)

        dmv = dm_ref[...].astype(BF16)
        gw_ref[...] = _dot(s_ref[...].astype(BF16), dmv, TN)
        ds_ref[...] += _dot(dmv, w_ref[...].astype(BF16), NT)

    return pl.pallas_call(
        body, name=name, grid=(n // tn,),
        in_specs=[pl.BlockSpec((rows, d), lambda j: (0, 0)), pl.BlockSpec((rows, tn), lambda j: (0, j)),
                  pl.BlockSpec((d, tn), lambda j: (0, j))],
        out_specs=[pl.BlockSpec((d, tn), lambda j: (0, j)), pl.BlockSpec((rows, d), lambda j: (0, 0))],
        out_shape=[jax.ShapeDtypeStruct((d, n), F32), jax.ShapeDtypeStruct((rows, d), F32)],
        compiler_params=_cp(("arbitrary",)))(s_cond, dmod, w)


def _sum_slots(name, a):
    s, r, c = a.shape

    def body(a_ref, o_ref):
        acc = a_ref[0]
        for k in range(1, s):
            acc = acc + a_ref[k]
        o_ref[...] = acc

    return pl.pallas_call(
        body, name=name, grid=(1,), in_specs=[pl.BlockSpec((s, r, c), lambda i: (0, 0, 0))],
        out_specs=pl.BlockSpec((r, c), lambda i: (0, 0)), out_shape=jax.ShapeDtypeStruct((r, c), F32),
        compiler_params=_cp(("arbitrary",)))(a)


def _cctx_grad(name, parts, c_ctx):
    s, r, d = parts.shape

    def body(p_ref, c_ref, o_ref):
        acc = p_ref[0]
        for k in range(1, s):
            acc = acc + p_ref[k]
        cv = c_ref[...]
        sg = _sigmoid(cv)
        o_ref[...] = acc[0:1, :] * sg * (1.0 + cv * (1.0 - sg))

    return pl.pallas_call(
        body, name=name, grid=(1,),
        in_specs=[pl.BlockSpec((s, r, d), lambda i: (0, 0, 0)), pl.BlockSpec((1, d), lambda i: (0, 0))],
        out_specs=pl.BlockSpec((1, d), lambda i: (0, 0)), out_shape=jax.ShapeDtypeStruct((1, d), F32),
        compiler_params=_cp(("arbitrary",)))(parts, c_ctx)


def _adamw(name, slots, w, m, v):
    s, r, c = slots.shape
    tr = _tile(r, 256 if c > 1024 else 512, SUBLANES)
    c1 = 1.0 - ADAM_B1 ** ADAM_STEP
    c2 = 1.0 - ADAM_B2 ** ADAM_STEP

    def body(s_ref, w_ref, m_ref, v_ref, g_ref, d_ref, mo_ref, vo_ref):
        g = s_ref[0].astype(F32)
        for k in range(1, s):
            g = g + s_ref[k].astype(F32)
        mn = ADAM_B1 * m_ref[...] + (1.0 - ADAM_B1) * g
        vn = ADAM_B2 * v_ref[...] + (1.0 - ADAM_B2) * (g * g)
        m_hat = mn / c1
        v_hat = vn / c2
        g_ref[...] = g
        mo_ref[...] = mn
        vo_ref[...] = vn
        d_ref[...] = -ADAM_LR * (m_hat / (jnp.sqrt(v_hat) + ADAM_EPS) + ADAM_WD * w_ref[...])

    spec = pl.BlockSpec((tr, c), lambda i: (i, 0))
    shp = jax.ShapeDtypeStruct((r, c), F32)
    return pl.pallas_call(
        body, name=name, grid=(r // tr,),
        in_specs=[pl.BlockSpec((s, tr, c), lambda i: (0, i, 0)), spec, spec, spec],
        out_specs=[spec] * 4, out_shape=[shp] * 4, compiler_params=_cp(("parallel",)))(slots, w, m, v)


def _coords():
    return lax.axis_index("x"), lax.axis_index("y"), lax.axis_index("c")


def _flip(coord, bit):
    return 1 - coord if bit else coord


def _all_gather_small(name, blk):
    r, ccols = blk.shape

    def body(x_ref, out_ref, send_sems, recv_sems, local_sem):
        x, y, c = _coords()
        me, sibling = (x, y, c), (x, y, 1 - c)
        chips = [(1 - x, y), (x, 1 - y), (1 - x, 1 - y)]

        def slot(px, py, pc):
            return out_ref.at[4 * px + 2 * py + pc]

        def copy(k, block, to, src=None):
            return pltpu.make_async_remote_copy(
                src_ref=slot(*block) if src is None else src, dst_ref=slot(*block),
                send_sem=send_sems.at[k], recv_sem=recv_sems.at[k], device_id=to, device_id_type=MESH)

        mine = pltpu.make_async_copy(x_ref, slot(*me), local_sem)
        mine.start()
        first = [copy(0, me, sibling, src=x_ref)]
        first += [copy(1 + j, me, (*chip, c), src=x_ref) for j, chip in enumerate(chips)]
        for cp in first:
            cp.start()
        passed = [copy(4 + j, (*chip, c), sibling) for j, chip in enumerate(chips)]
        for j, chip in enumerate(chips):
            copy(1 + j, (*chip, c), me).wait_recv()
            passed[j].start()
        copy(0, sibling, me).wait_recv()
        for j, chip in enumerate(chips):
            copy(4 + j, (*chip, 1 - c), me).wait_recv()
        for cp in first + passed:
            cp.wait_send()
        mine.wait()

    return pl.pallas_call(
        body, name=name, out_shape=jax.ShapeDtypeStruct((N_DEV, r, ccols), blk.dtype),
        in_specs=[pl.BlockSpec(memory_space=pltpu.VMEM)], out_specs=pl.BlockSpec(memory_space=pltpu.VMEM),
        scratch_shapes=[pltpu.SemaphoreType.DMA((7,)), pltpu.SemaphoreType.DMA((7,)), pltpu.SemaphoreType.DMA],
    )(blk)


def _comm_semaphores(n):
    return [pltpu.SemaphoreType.DMA((7 * n,)), pltpu.SemaphoreType.DMA((7 * n,)), pltpu.SemaphoreType.DMA((n,))]


def _comm_out_shape(kind, s, wide):
    if kind == "gather":
        shape = (s.shape[0], N_DEV * s.shape[1]) if wide else (N_DEV,) + s.shape
    else:
        shape = (N_DEV, s.shape[0], s.shape[1] // N_DEV) if wide else s.shape
    return jax.ShapeDtypeStruct(shape, s.dtype)


def _block_of(ref, idx, wide, cols):
    if not wide:
        return ref.at[idx]
    return ref.at[:, pl.ds(pl.multiple_of(idx * cols, LANES), cols)]


def _gather_plan(ins, outs, wide, send_sems, recv_sems, local_sems):
    n = len(ins)
    x, y, c = _coords()
    me, sibling = (x, y, c), (x, y, 1 - c)
    chips = [(1 - x, y), (x, 1 - y), (1 - x, 1 - y)]

    def slot(a, px, py, pc):
        return _block_of(outs[a], 4 * px + 2 * py + pc, wide[a], ins[a].shape[-1])

    def copy(a, k, block, to, src=None):
        return pltpu.make_async_remote_copy(
            src_ref=slot(a, *block) if src is None else src, dst_ref=slot(a, *block),
            send_sem=send_sems.at[7 * a + k], recv_sem=recv_sems.at[7 * a + k], device_id=to, device_id_type=MESH)

    def local(a):
        return pltpu.make_async_copy(ins[a], slot(a, *me), local_sems.at[a])

    def first(a):
        return [copy(a, 0, me, sibling, src=ins[a])] + [copy(a, 1 + j, me, (*chip, c), src=ins[a])
                                                        for j, chip in enumerate(chips)]

    def passed(a, j):
        return copy(a, 4 + j, (*chips[j], c), sibling)

    def start():
        for a in range(n):
            local(a).start()
            for cp in first(a):
                cp.start()

    def pass_on():
        for a in range(n):
            for j, chip in enumerate(chips):
                copy(a, 1 + j, (*chip, c), me).wait_recv()
                passed(a, j).start()

    def finish():
        for a in range(n):
            copy(a, 0, sibling, me).wait_recv()
            for j, chip in enumerate(chips):
                copy(a, 4 + j, (*chip, 1 - c), me).wait_recv()
        for a in range(n):
            for cp in first(a) + [passed(a, j) for j in range(3)]:
                cp.wait_send()
            local(a).wait()

    return start, pass_on, finish


def _exchange_plan(ins, outs, wide, send_sems, recv_sems, local_sems):
    n = len(ins)
    x, y, c = _coords()
    my_idx = 4 * x + 2 * y + c

    def src(a, idx):
        return _block_of(ins[a], idx, wide[a], outs[a].shape[-1])

    def local(a):
        return pltpu.make_async_copy(src(a, my_idx), outs[a].at[my_idx], local_sems.at[a])

    def pair(a, rel):
        px, py, pc = _flip(x, rel & 4), _flip(y, rel & 2), _flip(c, rel & 1)
        peer_idx = 4 * px + 2 * py + pc
        sems = dict(send_sem=send_sems.at[7 * a + rel - 1], recv_sem=recv_sems.at[7 * a + rel - 1],
                    device_id=(px, py, pc), device_id_type=MESH)
        send = pltpu.make_async_remote_copy(src_ref=src(a, peer_idx), dst_ref=outs[a].at[my_idx], **sems)
        recv = pltpu.make_async_remote_copy(src_ref=src(a, my_idx), dst_ref=outs[a].at[peer_idx], **sems)
        return send, recv

    def start():
        for a in range(n):
            local(a).start()
            for rel in range(1, N_DEV):
                pair(a, rel)[0].start()

    def finish():
        for a in range(n):
            for rel in range(1, N_DEV):
                send, recv = pair(a, rel)
                recv.wait_recv()
                send.wait_send()
            local(a).wait()

    return start, None, finish


def _comm_plan(kind, ins, outs, wide, send_sems, recv_sems, local_sems):
    plan = {"gather": _gather_plan, "exchange": _exchange_plan}[kind]
    return plan(ins, outs, wide, send_sems, recv_sems, local_sems)


def _all_gather_big(name, shards, wide):
    n = len(shards)

    def body(*refs):
        start, pass_on, finish = _gather_plan(refs[:n], refs[n:2 * n], wide, *refs[2 * n:])
        start()
        pass_on()
        finish()

    any_spec = pl.BlockSpec(memory_space=pl.ANY)
    return pl.pallas_call(
        body, name=name, out_shape=[_comm_out_shape("gather", s, w) for s, w in zip(shards, wide)],
        in_specs=[any_spec] * n, out_specs=[any_spec] * n, scratch_shapes=_comm_semaphores(n))(*shards)


def _rope_tables(lx, lc):
    rows = lx // GRID_W
    row = jnp.repeat(jnp.arange(rows, dtype=F32), GRID_W)
    col = jnp.tile(jnp.arange(GRID_W, dtype=F32), rows)
    half = HEAD_DIM // 2
    inv_freq = ROPE_THETA ** (-jnp.arange(0, half, 2, dtype=F32) / half)
    ang = jnp.concatenate([row[:, None] * inv_freq, col[:, None] * inv_freq], axis=-1)
    cos, sin = jnp.cos(ang), jnp.sin(ang)
    cosf = jnp.repeat(cos, 2, axis=-1)
    sinf = jnp.stack([-sin, sin], axis=-1).reshape(lx, HEAD_DIM)
    cosf = jnp.concatenate([cosf, jnp.ones((lc, HEAD_DIM), F32)], axis=0)
    sinf = jnp.concatenate([sinf, jnp.zeros((lc, HEAD_DIM), F32)], axis=0)
    return cosf, sinf


def _cols_full(g):
    return jnp.transpose(g, (1, 0, 2)).reshape(g.shape[1], N_DEV * g.shape[2])


def _cols_split(w):
    k, n = w.shape
    return jnp.transpose(w.reshape(k, N_DEV, n // N_DEV), (1, 0, 2))


def _pad_rows(a, rows):
    return jnp.pad(a, ((0, rows - a.shape[0]), (0, 0)))


def kernel(x, c, ctx, c_ctx, w_ada, b_ada, ffn1_w_in, ffn1_w_out, mix_w_in, attn_q_gain, attn_k_gain, ret_decay_logit, w_proj_attn, w_proj_ret, mix_w_out, ffn2_w_in, ffn2_w_out, final_norm, loss_target, m_c_ctx, m_w_ada, m_b_ada, m_ffn1_w_in, m_ffn1_w_out, m_mix_w_in, m_attn_q_gain, m_attn_k_gain, m_ret_decay_logit, m_w_proj_attn, m_w_proj_ret, m_mix_w_out, m_ffn2_w_in, m_ffn2_w_out, m_final_norm, v_c_ctx, v_w_ada, v_b_ada, v_ffn1_w_in, v_ffn1_w_out, v_mix_w_in, v_attn_q_gain, v_attn_k_gain, v_ret_decay_logit, v_w_proj_attn, v_w_proj_ret, v_mix_w_out, v_ffn2_w_in, v_ffn2_w_out, v_final_norm):
    lx, d = x.shape[1], x.shape[2]
    lc = ctx.shape[1]
    t = lx + lc
    aw, rw = w_proj_attn.shape[1], w_proj_ret.shape[1]
    pw = mix_w_in.shape[2] * N_DEV
    kw = (pw - aw - 4 * rw - 2 * d) // 2
    groups = aw // kw
    heads_r = rw // HEAD_DIM
    ka_off, va_off = aw, aw + kw
    qr_off = aw + 2 * kw
    kr_off, vr_off, gr_off = qr_off + rw, qr_off + 2 * rw, qr_off + 3 * rw
    ga_off, gb_off = qr_off + 4 * rw, qr_off + 4 * rw + d
    xi, yi, ci = _coords()
    me = 4 * xi + 2 * yi + ci

    col_names = ["ffn1_w_in", "mix_w_in", "w_proj_attn", "w_proj_ret", "ffn2_w_in"]
    row_names = ["ffn1_w_out", "mix_w_out", "ffn2_w_out"]
    shard = dict(ffn1_w_in=ffn1_w_in[0], mix_w_in=mix_w_in[0], w_proj_attn=w_proj_attn[0], w_proj_ret=w_proj_ret[0],
                 ffn2_w_in=ffn2_w_in[0], ffn1_w_out=ffn1_w_out[0], mix_w_out=mix_w_out[0], ffn2_w_out=ffn2_w_out[0])
    names = col_names + row_names
    bf_shard = {k: shard[k].astype(BF16) for k in names}
    full, landed = {}, {}

    def is_wide(k):
        return k in col_names and shard[k].shape[1] % LANES == 0

    def gather_of(keys):
        return "gather", [bf_shard[k] for k in keys], [is_wide(k) for k in keys]

    def keep_full(keys, gathered):
        for k, g in zip(keys, gathered):
            if is_wide(k):
                full[k] = g
            else:
                full[k] = _cols_full(g) if k in col_names else g.reshape(N_DEV * g.shape[1], g.shape[2])

    def exchange_of(grads):
        blocks = [g if is_wide(k) else _cols_split(g) if k in col_names
                  else g.reshape(N_DEV, g.shape[0] // N_DEV, g.shape[1]) for k, g in grads.items()]
        return "exchange", blocks, [is_wide(k) for k in grads]

    def keep_landed(grads, got):
        landed.update(zip(grads.keys(), got))

    keep_full(["ffn1_w_in"], _all_gather_big("gather_first", *gather_of(["ffn1_w_in"])[1:]))

    c_all = _all_gather_small("gather_cond", _pad_rows(c, SUBLANES))[:, 0, :]
    cond = _pad_rows(jnp.concatenate([c_all, c_ctx[None, :]], axis=0), 2 * SUBLANES)
    ada_part, s_cond = _ada_fwd("ada_fwd", cond, w_ada[0])
    ada_all = _all_gather_small("gather_ada", ada_part)
    mod_all = jnp.transpose(ada_all, (1, 0, 2)).reshape(2 * SUBLANES, N_MOD * d) + b_ada
    mod_x = lax.dynamic_index_in_dim(mod_all, me, axis=0, keepdims=False).reshape(N_MOD, d)
    mod_c = mod_all[N_DEV].reshape(N_MOD, d)
    mods = jnp.stack([mod_c, mod_x], axis=0)[:, :, None, :]
    sh1, sc1, g1, sh2, sc2, g2, sh3, sc3, g3 = [mods[:, k] for k in range(N_MOD)]

    h0 = jnp.concatenate([x[0], ctx[0]], axis=0)
    u1 = _rmsmod("ffn1_norm", h0, sc1, sh1, lx)
    keys = ["ffn1_w_out", "mix_w_in"]
    (z1, s1), got = _ffn_in("ffn1_in", u1, full["ffn1_w_in"], comm=gather_of(keys))
    keep_full(keys, got)
    keys = ["w_proj_attn", "w_proj_ret", "mix_w_out", "ffn2_w_out"]
    (h1, f1), got = _mm_residual("ffn1_out", s1, full["ffn1_w_out"], h0, g1, 0.5, lx, comm=gather_of(keys))
    keep_full(keys, got)

    u2 = _rmsmod("mix_norm", h1, sc2, sh2, lx)
    keys = ["ffn2_w_in"]
    p, got = _mm_nn("mix_in", u2, full["mix_w_in"], BF16, tm_pref=M_TILE_BIG, tn_pref=512, comm=gather_of(keys))
    keep_full(keys, got)
    cosf, sinf = _rope_tables(lx, lc)
    q_rot, k_rot = _qk_prep("qk_prep", p, cosf, sinf, attn_q_gain, attn_k_gain, aw, kw)
    ya, lse = _attn_fwd("attn_fwd", q_rot, k_rot, p, va_off, lx, groups)

    decay = ret_decay_logit[0].astype(F32)
    log_gamma = jax.nn.log_sigmoid(decay)
    r_offs = (qr_off, kr_off, vr_off, rw)
    tab = [_ret_tables(log_gamma[k], k) for k in range(2)]
    y_f, st_f = _ret_fwd("ret_fwd_a", p, r_offs, tab[0], 0, lx, None)
    y_r, st_b = _ret_fwd("ret_fwd_b", p, r_offs, tab[1], 1, lx, y_f)
    yr = _ret_out("ret_out", y_r, p, gr_off, lx)

    pa = _mm_nn("proj_attn", ya, full["w_proj_attn"], BF16)
    pr = _mm_nn("proj_ret", yr, full["w_proj_ret"], BF16)
    mg = _merge("merge", pa, pr, p, ga_off, gb_off)
    h2, o2 = _mm_residual("mix_out", mg, full["mix_w_out"], h1, g2, 1.0, lx)

    u3 = _rmsmod("ffn2_norm", h2, sc3, sh3, lx)
    z3, s3 = _ffn_in("ffn2_in", u3, full["ffn2_w_in"])
    h3, f3 = _mm_residual("ffn2_out", s3, full["ffn2_w_out"], h2, g3, 0.5, lx)
    dh3, d_fn, loss_tile = _final_loss("final_loss", h3, final_norm[None, :], loss_target[0])

    df3, dg3 = _gate_bwd("ffn2_gate_bwd", dh3, f3, g3, 0.5, lx)
    dz3 = _ffn_out_bwd("ffn2_out_bwd", df3, full["ffn2_w_out"], z3)
    grads = {"ffn2_w_out": _mm_tn("ffn2_out_dw", s3, df3, BF16, tm_pref=M_TILE_BIG)}
    g_ffn2_w_in, got = _ffn_in_bwd_w("ffn2_in_dw", u3, dz3, comm=exchange_of(grads))
    keep_landed(grads, got)
    grads = {"ffn2_w_in": g_ffn2_w_in}
    du3, got = _ffn_in_bwd_x("ffn2_in_bwd", dz3, full["ffn2_w_in"], comm=exchange_of(grads))
    keep_landed(grads, got)
    dh2, dsc3, dsh3 = _rmsmod_bwd("ffn2_norm_bwd", du3, h2, sc3, dh3, lx, lx)

    do2, dg2 = _gate_bwd("mix_gate_bwd", dh2, o2, g2, 1.0, lx)
    dpa, dpr, dga, dgb = _merge_bwd("merge_bwd", do2, full["mix_w_out"], pa, pr, p, ga_off, gb_off)
    dya = _mm_nt("proj_attn_bwd", dpa, full["w_proj_attn"], BF16)
    dyr = _mm_nt("proj_ret_bwd", dpr, full["w_proj_ret"], F32)
    mix_grads = {"mix_w_out": _mm_tn("mix_out_dw", mg, do2, BF16),
                 "w_proj_attn": _mm_tn("proj_attn_dw", ya, dpa, BF16),
                 "w_proj_ret": _mm_tn("proj_ret_dw", yr, dpr, BF16)}

    dy_ret, dgr = _ret_out_bwd("ret_out_bwd", dyr, y_r, p, gr_off, lx)
    dqr, dkr, dvr, dl_f = _ret_bwd("ret_bwd_a", p, r_offs, tab[0], st_f, dy_ret, 0, lx, None)
    dqr, dkr, dvr, dl_b = _ret_bwd("ret_bwd_b", p, r_offs, tab[1], st_b, dy_ret, 1, lx, (dqr, dkr, dvr))
    d_lam = jnp.stack([dl_f[:, 0, 0], dl_b[:, 0, 0]], axis=0)
    d_decay = d_lam * jax.nn.sigmoid(-decay)

    delta = _attn_delta("attn_delta", ya, dya)
    dq_rot, dk_rot, dva = _attn_bwd("attn_bwd", q_rot, k_rot, p, va_off, dya, lse, delta, lx, groups)
    dqa, dka, d_qg, d_kg = _qk_prep_bwd("qk_prep_bwd", dq_rot, dk_rot, p, cosf, sinf, attn_q_gain, attn_k_gain, aw, kw)

    dp = jnp.concatenate([dqa, dka, dva, dqr.astype(BF16), dkr.astype(BF16), dvr.astype(BF16),
                          _pad_rows(dgr, t), _pad_rows(dga, t), _pad_rows(dgb, t)], axis=1)
    grads = {"mix_w_in": _mm_tn("mix_in_dw", u2, dp, BF16, tm_pref=M_TILE_DW, tn_pref=512)}
    du2, got = _mm_nt("mix_in_bwd", dp, full["mix_w_in"], F32, comm=exchange_of(grads))
    keep_landed(grads, got)
    dh1, dsc2, dsh2 = _rmsmod_bwd("mix_norm_bwd", du2, h1, sc2, dh2, lx, t)

    df1, dg1 = _gate_bwd("ffn1_gate_bwd", dh1, f1, g1, 0.5, lx)
    dz1, got = _ffn_out_bwd("ffn1_out_bwd", df1, full["ffn1_w_out"], z1, comm=exchange_of(mix_grads))
    keep_landed(mix_grads, got)
    grads = {"ffn1_w_out": _mm_tn("ffn1_out_dw", s1, df1, BF16, tm_pref=M_TILE_BIG)}
    g_ffn1_w_in, got = _ffn_in_bwd_w("ffn1_in_dw", u1, dz1, comm=exchange_of(grads))
    keep_landed(grads, got)
    grads = {"ffn1_w_in": g_ffn1_w_in}
    du1, got = _ffn_in_bwd_x("ffn1_in_bwd", dz1, full["ffn1_w_in"], comm=exchange_of(grads))
    keep_landed(grads, got)
    grad_x, dsc1, dsh1 = _rmsmod_bwd("ffn1_norm_bwd", du1, h0, sc1, dh1, lx, lx)

    zero = jnp.zeros((1, d), F32)
    dmod_c = jnp.concatenate([dsh1[0], dsc1[0], dg1[0], dsh2[0], dsc2[0], zero, zero, zero, zero], axis=0)
    dmod_x = jnp.concatenate([dsh1[1], dsc1[1], dg1[1], dsh2[1], dsc2[1], dg2[1], dsh3[1], dsc3[1], dg3[1]], axis=0)
    misc = jnp.concatenate([d_qg[0], d_kg[0], d_decay.reshape(-1), loss_tile[0, 0:1]])
    misc = jnp.pad(misc, (0, d - misc.shape[0]))[None, :]
    n_small = 3 * SUBLANES
    small = _pad_rows(jnp.concatenate([dmod_c, dmod_x, d_fn, misc], axis=0), n_small)
    small_all = _all_gather_small("gather_small", small)
    small_sum = _sum_slots("sum_small", small_all)
    dmod_c_sum, dmod_x_sum = small_sum[0:N_MOD], small_sum[N_MOD:2 * N_MOD]
    g_final_norm = small_sum[2 * N_MOD]
    misc_sum = small_sum[2 * N_MOD + 1]
    g_qg = misc_sum[0:HEAD_DIM][None, :]
    g_kg = misc_sum[HEAD_DIM:2 * HEAD_DIM][None, :]
    g_decay = misc_sum[2 * HEAD_DIM:2 * HEAD_DIM + 2 * heads_r].reshape(1, 2, heads_r)
    loss = misc_sum[2 * HEAD_DIM + 2 * heads_r]
    g_b_ada = (dmod_x_sum + dmod_c_sum).reshape(1, N_MOD * d)

    n_ada = w_ada.shape[2]
    dmod_rows = jnp.concatenate([small_all[:, N_MOD:2 * N_MOD, :].reshape(N_DEV, N_MOD * d),
                                 dmod_c_sum.reshape(1, N_MOD * d)], axis=0)
    dmod_mine = _pad_rows(lax.dynamic_slice_in_dim(dmod_rows, me * n_ada, n_ada, axis=1), 2 * SUBLANES)
    g_w_ada, ds_cond = _ada_bwd("ada_bwd", s_cond, dmod_mine, w_ada[0])
    cctx_parts = _all_gather_small("gather_cctx", ds_cond[N_DEV:N_DEV + SUBLANES])
    g_c_ctx = _cctx_grad("cctx_grad", cctx_parts, c_ctx[None, :])[0]

    mom = dict(ffn1_w_in=(m_ffn1_w_in, v_ffn1_w_in), mix_w_in=(m_mix_w_in, v_mix_w_in),
               w_proj_attn=(m_w_proj_attn, v_w_proj_attn), w_proj_ret=(m_w_proj_ret, v_w_proj_ret),
               ffn2_w_in=(m_ffn2_w_in, v_ffn2_w_in), ffn1_w_out=(m_ffn1_w_out, v_ffn1_w_out),
               mix_w_out=(m_mix_w_out, v_mix_w_out), ffn2_w_out=(m_ffn2_w_out, v_ffn2_w_out))
    res = {}
    for k in names:
        res[k] = _adamw("adamw_" + k, landed[k], shard[k], mom[k][0][0], mom[k][1][0])
    res["w_ada"] = _adamw("adamw_w_ada", g_w_ada[None], w_ada[0], m_w_ada[0], v_w_ada[0])

    def pack(cc, ba, qg, kg, dec, fn):
        misc_row = jnp.concatenate([qg.reshape(-1), kg.reshape(-1), dec.reshape(-1)])
        misc_row = jnp.pad(misc_row, (0, d - misc_row.shape[0]))[None, :]
        return _pad_rows(jnp.concatenate([cc.reshape(1, d), ba.reshape(N_MOD, d), fn.reshape(1, d), misc_row], axis=0),
                         2 * SUBLANES)

    sg, sd, sm, sv = _adamw(
        "adamw_small", pack(g_c_ctx, g_b_ada, g_qg, g_kg, g_decay, g_final_norm)[None],
        pack(c_ctx, b_ada, attn_q_gain, attn_k_gain, ret_decay_logit, final_norm),
        pack(m_c_ctx, m_b_ada, m_attn_q_gain, m_attn_k_gain, m_ret_decay_logit, m_final_norm),
        pack(v_c_ctx, v_b_ada, v_attn_q_gain, v_attn_k_gain, v_ret_decay_logit, v_final_norm))

    def unpack(a):
        misc_row = a[N_MOD + 2]
        return dict(c_ctx=a[0], b_ada=a[1:1 + N_MOD].reshape(1, N_MOD * d), final_norm=a[N_MOD + 1],
                    attn_q_gain=misc_row[0:HEAD_DIM][None, :], attn_k_gain=misc_row[HEAD_DIM:2 * HEAD_DIM][None, :],
                    ret_decay_logit=misc_row[2 * HEAD_DIM:2 * HEAD_DIM + 2 * heads_r].reshape(1, 2, heads_r))

    small_out = [unpack(a) for a in (sg, sd, sm, sv)]
    order = ["c_ctx", "w_ada", "b_ada", "ffn1_w_in", "ffn1_w_out", "mix_w_in", "attn_q_gain", "attn_k_gain",
             "ret_decay_logit", "w_proj_attn", "w_proj_ret", "mix_w_out", "ffn2_w_in", "ffn2_w_out", "final_norm"]
    outs = [loss, grad_x[None]]
    for which in range(4):
        for k in order:
            outs.append(res[k][which][None] if k in res else small_out[which][k])
    return tuple(outs)
```

```python
import math

import jax
import jax.numpy as jnp
from jax import lax
from jax.experimental import pallas as pl
from jax.experimental.pallas import tpu as pltpu

F32 = jnp.float32
BF16 = jnp.bfloat16
MESH = pl.DeviceIdType.MESH

N_DEV = 8
HEAD_DIM = 128
GRID_W = 64
ROPE_THETA = 10000.0
NORM_EPS = 1e-6
RET_CHUNK = 128
N_MOD = 9
LANES = 128
SUBLANES = 8
V7X_VMEM_BYTES = 64 * 1024 * 1024
VMEM_LIMIT = V7X_VMEM_BYTES - 8 * 1024 * 1024
K_TILE = 2560
M_TILE_BIG = 1408
M_TILE_DW = 2048
LOG2E = 1.4426950408889634
LN2 = 0.6931471805599453

ADAM_LR = 0.001
ADAM_B1 = 0.9
ADAM_B2 = 0.999
ADAM_EPS = 1e-08
ADAM_WD = 0.01
ADAM_STEP = 10

NN = (((1,), (0,)), ((), ()))
NT = (((1,), (1,)), ((), ()))
TN = (((0,), (0,)), ((), ()))


def _tile(n, pref, align=LANES):
    best = None
    t = align
    while t <= min(n, pref):
        if n % t == 0:
            best = t
        t += align
    return n if best is None else best


def _cp(sem):
    return pltpu.CompilerParams(dimension_semantics=sem, vmem_limit_bytes=VMEM_LIMIT)


def _sigmoid(v):
    return 0.5 * jnp.tanh(0.5 * v) + 0.5


def _dot(a, b, dims):
    return lax.dot_general(a, b, dims, preferred_element_type=F32)


def _mm(name, a, a_spec, b_list, dims, grid, out_shapes, out_specs, acc_shape, epi, extras=(), comm=None):
    nb, ne, no = len(b_list), len(extras), len(out_shapes)
    nk = grid[2]
    kind, c_arrays, c_wide = comm if comm is not None else (None, [], [])
    ncm = len(c_arrays)

    def body(*refs):
        a_ref = refs[0]
        b_refs = refs[1:1 + nb]
        e_refs = refs[1 + nb:1 + nb + ne]
        pos = 1 + nb + ne
        c_ins = refs[pos:pos + ncm]
        o_refs = refs[pos + ncm:pos + ncm + no]
        c_outs = refs[pos + ncm + no:pos + 2 * ncm + no]
        scratch = refs[pos + 2 * ncm + no:]
        accs = scratch[:0 if nk == 1 else nb]
        ids = [pl.program_id(axis) for axis in range(3)]
        if ncm:
            start_comm, mid_comm, finish_comm = _comm_plan(kind, c_ins, c_outs, c_wide, *scratch[len(accs):])
            step = (ids[0] * grid[1] + ids[1]) * nk + ids[2]
            pl.when(step == 0)(start_comm)
            if mid_comm is not None:
                pl.when(step == (3 * grid[0] * grid[1] * nk) // 4)(mid_comm)

        def finish(tiles):
            vals = epi(tiles, e_refs)
            for o_ref, v in zip(o_refs, vals):
                if isinstance(v, tuple):
                    for idx, part in enumerate(v):
                        o_ref[idx] = part.astype(o_ref.dtype)
                else:
                    o_ref[...] = v.astype(o_ref.dtype)

        if nk == 1:
            finish([_dot(a_ref[...], b_ref[...], dims) for b_ref in b_refs])
        else:
            @pl.when(ids[2] == 0)
            def _():
                for acc in accs:
                    acc[...] = jnp.zeros(acc.shape, F32)

            av = a_ref[...]
            for b_ref, acc in zip(b_refs, accs):
                acc[...] += _dot(av, b_ref[...], dims)

            @pl.when(ids[2] == nk - 1)
            def _():
                finish([acc[...] for acc in accs])

        if ncm:
            pl.when(jnp.logical_and(jnp.logical_and(ids[0] == grid[0] - 1, ids[1] == grid[1] - 1),
                                    ids[2] == nk - 1))(finish_comm)

    any_spec = pl.BlockSpec(memory_space=pl.ANY)
    c_shapes = [_comm_out_shape(kind, s, w) for s, w in zip(c_arrays, c_wide)]
    scratch_shapes = [] if nk == 1 else [pltpu.VMEM(acc_shape, F32)] * nb
    if ncm:
        scratch_shapes = scratch_shapes + _comm_semaphores(ncm)
    semantics = ("arbitrary",) * 3 if ncm else ("parallel", "parallel", "arbitrary")
    outs = pl.pallas_call(
        body, name=name, grid=grid,
        in_specs=[a_spec] + [s for _, s in b_list] + [s for _, s in extras] + [any_spec] * ncm,
        out_specs=list(out_specs) + [any_spec] * ncm, out_shape=list(out_shapes) + c_shapes,
        scratch_shapes=scratch_shapes, compiler_params=_cp(semantics),
    )(a, *[b for b, _ in b_list], *[e for e, _ in extras], *c_arrays)
    main = outs[0] if no == 1 else tuple(outs[:no])
    return main if comm is None else (main, list(outs[no:]))


def _plain(accs, _):
    return (accs[0],)


def _mm_nn(name, a, b, out_dtype, tm_pref=1024, tn_pref=1024, tk_pref=K_TILE, comm=None):
    m, k = a.shape
    n = b.shape[1]
    tm, tn, tk = _tile(m, tm_pref), _tile(n, tn_pref), _tile(k, tk_pref)
    return _mm(name, a, pl.BlockSpec((tm, tk), lambda i, j, kk: (i, kk)),
               [(b, pl.BlockSpec((tk, tn), lambda i, j, kk: (kk, j)))], NN, (m // tm, n // tn, k // tk),
               [jax.ShapeDtypeStruct((m, n), out_dtype)], [pl.BlockSpec((tm, tn), lambda i, j, kk: (i, j))],
               (tm, tn), _plain, comm=comm)


def _mm_nt(name, a, b, out_dtype, tm_pref=1024, tn_pref=1024, tk_pref=K_TILE, comm=None):
    m, k = a.shape
    n = b.shape[0]
    tm, tn, tk = _tile(m, tm_pref), _tile(n, tn_pref), _tile(k, tk_pref)
    return _mm(name, a, pl.BlockSpec((tm, tk), lambda i, j, kk: (i, kk)),
               [(b, pl.BlockSpec((tn, tk), lambda i, j, kk: (j, kk)))], NT, (m // tm, n // tn, k // tk),
               [jax.ShapeDtypeStruct((m, n), out_dtype)], [pl.BlockSpec((tm, tn), lambda i, j, kk: (i, j))],
               (tm, tn), _plain, comm=comm)


def _mm_tn(name, a, b, out_dtype, rows=None, tm_pref=1024, tn_pref=1024, tk_pref=K_TILE, comm=None):
    k = a.shape[0] if rows is None else rows
    m, n = a.shape[1], b.shape[1]
    tm, tn, tk = _tile(m, tm_pref), _tile(n, tn_pref), _tile(k, tk_pref)
    return _mm(name, a, pl.BlockSpec((tk, tm), lambda i, j, kk: (kk, i)),
               [(b, pl.BlockSpec((tk, tn), lambda i, j, kk: (kk, j)))], TN, (m // tm, n // tn, k // tk),
               [jax.ShapeDtypeStruct((m, n), out_dtype)], [pl.BlockSpec((tm, tn), lambda i, j, kk: (i, j))],
               (tm, tn), _plain, comm=comm)


def _ffn_in(name, u, w_in, comm=None):
    r, d = u.shape
    f = w_in.shape[1] // 2
    tm, tn, tk = _tile(r, 1024), _tile(f, 512), _tile(d, K_TILE)
    nf = f // tn

    def epi(accs, _):
        za, zb = accs
        s = za * _sigmoid(za) * zb
        return (za, zb), s

    return _mm(name, u, pl.BlockSpec((tm, tk), lambda i, j, kk: (i, kk)),
               [(w_in, pl.BlockSpec((tk, tn), lambda i, j, kk: (kk, j))),
                (w_in, pl.BlockSpec((tk, tn), lambda i, j, kk: (kk, j + nf)))],
               NN, (r // tm, nf, d // tk),
               [jax.ShapeDtypeStruct((2, r, f), BF16), jax.ShapeDtypeStruct((r, f), BF16)],
               [pl.BlockSpec((2, tm, tn), lambda i, j, kk: (0, i, j)), pl.BlockSpec((tm, tn), lambda i, j, kk: (i, j))],
               (tm, tn), epi, comm=comm)


def _mm_residual(name, a, w, res, gate, gate_scale, lx, comm=None):
    r, k = a.shape
    n = w.shape[1]
    tm, tn, tk = _tile(r, 1024), _tile(n, 1024), _tile(k, K_TILE)

    def epi(accs, e_refs):
        res_ref, g_ref = e_refs
        rows = pl.program_id(0) * tm + lax.broadcasted_iota(jnp.int32, (tm, 1), 0)
        g = jnp.where(rows < lx, g_ref[1], g_ref[0])
        return res_ref[...] + gate_scale * g * accs[0], accs[0]

    return _mm(name, a, pl.BlockSpec((tm, tk), lambda i, j, kk: (i, kk)),
               [(w, pl.BlockSpec((tk, tn), lambda i, j, kk: (kk, j)))], NN, (r // tm, n // tn, k // tk),
               [jax.ShapeDtypeStruct((r, n), F32), jax.ShapeDtypeStruct((r, n), BF16)],
               [pl.BlockSpec((tm, tn), lambda i, j, kk: (i, j))] * 2, (tm, tn), epi,
               extras=[(res, pl.BlockSpec((tm, tn), lambda i, j, kk: (i, j))),
                       (gate, pl.BlockSpec((2, 1, tn), lambda i, j, kk: (0, 0, j)))], comm=comm)


def _ffn_out_bwd(name, df, w_out, z, comm=None):
    r, d = df.shape
    f = w_out.shape[0]
    tm, tn, tk = _tile(r, M_TILE_BIG), _tile(f, 512), _tile(d, K_TILE)

    def epi(accs, e_refs):
        ds = accs[0]
        za = e_refs[0][0].astype(F32)
        zb = e_refs[0][1].astype(F32)
        sg = _sigmoid(za)
        da = ds * zb * sg * (1.0 + za * (1.0 - sg))
        db = ds * za * sg
        return ((da, db),)

    zspec = pl.BlockSpec((2, tm, tn), lambda i, j, kk: (0, i, j))
    return _mm(name, df, pl.BlockSpec((tm, tk), lambda i, j, kk: (i, kk)),
               [(w_out, pl.BlockSpec((tn, tk), lambda i, j, kk: (j, kk)))], NT, (r // tm, f // tn, d // tk),
               [jax.ShapeDtypeStruct((2, r, f), BF16)], [zspec], (tm, tn), epi, extras=[(z, zspec)], comm=comm)


def _ffn_in_bwd_x(name, dz, w_in, comm=None):
    _, r, f = dz.shape
    d = w_in.shape[0]
    tm, tn, tk = _tile(r, 1024), _tile(d, 2048), _tile(f, K_TILE)
    nkf = f // tk
    return _mm(name, dz, pl.BlockSpec((None, tm, tk), lambda i, j, kk: (kk // nkf, i, kk % nkf)),
               [(w_in, pl.BlockSpec((tn, tk), lambda i, j, kk: (j, kk)))], NT, (r // tm, d // tn, 2 * nkf),
               [jax.ShapeDtypeStruct((r, d), F32)], [pl.BlockSpec((tm, tn), lambda i, j, kk: (i, j))],
               (tm, tn), _plain, comm=comm)


def _ffn_in_bwd_w(name, u, dz, comm=None):
    r, d = u.shape
    f = dz.shape[2]
    tm, tn, tk = _tile(d, M_TILE_DW), _tile(f, 512), _tile(r, K_TILE)
    nf = f // tn
    return _mm(name, u, pl.BlockSpec((tk, tm), lambda i, j, kk: (kk, i)),
               [(dz, pl.BlockSpec((None, tk, tn), lambda i, j, kk: (j // nf, kk, j % nf)))], TN,
               (d // tm, 2 * nf, r // tk),
               [jax.ShapeDtypeStruct((d, 2 * f), BF16)], [pl.BlockSpec((tm, tn), lambda i, j, kk: (i, j))],
               (tm, tn), _plain, comm=comm)


def _merge_bwd(name, dout, w_out, pa, pr, p, ga_off, gb_off):
    r, d = dout.shape
    n = w_out.shape[0]
    cw = math.gcd(math.gcd(ga_off, gb_off), n)
    tm, tn, tk = _tile(r, 1024), _tile(cw, 512), _tile(d, K_TILE)

    def epi(accs, e_refs):
        dm = accs[0]
        pa_ref, pr_ref, ga_ref, gb_ref = e_refs
        sa = _sigmoid(ga_ref[...].astype(F32))
        sb = _sigmoid(gb_ref[...].astype(F32))
        pav = pa_ref[...].astype(F32)
        prv = pr_ref[...].astype(F32)
        return dm * sa, dm * sb, dm * pav * sa * (1.0 - sa), dm * prv * sb * (1.0 - sb)

    o_spec = pl.BlockSpec((tm, tn), lambda i, j, kk: (i, j))
    return _mm(name, dout, pl.BlockSpec((tm, tk), lambda i, j, kk: (i, kk)),
               [(w_out, pl.BlockSpec((tn, tk), lambda i, j, kk: (j, kk)))], NT, (r // tm, n // tn, d // tk),
               [jax.ShapeDtypeStruct((r, n), BF16)] * 4, [o_spec] * 4, (tm, tn), epi,
               extras=[(pa, o_spec), (pr, o_spec),
                       (p, pl.BlockSpec((tm, tn), lambda i, j, kk: (i, ga_off // tn + j))),
                       (p, pl.BlockSpec((tm, tn), lambda i, j, kk: (i, gb_off // tn + j)))])


def _row_tile(r, lx, d):
    pref = 256 if d > 1024 else 512
    return _tile(math.gcd(r, lx), pref, SUBLANES)


def _rmsmod(name, h, scale, shift, lx, rows=None):
    r = h.shape[0] if rows is None else rows
    d = h.shape[1]
    tr = _row_tile(r, lx, d)
    nx = lx // tr
    cls = lambda i: (jnp.where(i < nx, 1, 0), 0, 0)

    def body(h_ref, sc_ref, sh_ref, u_ref):
        hv = h_ref[...]
        rinv = lax.rsqrt(jnp.mean(hv * hv, axis=-1, keepdims=True) + NORM_EPS)
        u_ref[...] = (hv * rinv * (1.0 + sc_ref[...]) + sh_ref[...]).astype(u_ref.dtype)

    return pl.pallas_call(
        body, name=name, grid=(r // tr,),
        in_specs=[pl.BlockSpec((tr, d), lambda i: (i, 0)), pl.BlockSpec((None, 1, d), cls), pl.BlockSpec((None, 1, d), cls)],
        out_specs=pl.BlockSpec((tr, d), lambda i: (i, 0)), out_shape=jax.ShapeDtypeStruct((r, d), BF16),
        compiler_params=_cp(("parallel",)))(h, scale, shift)


def _rmsmod_bwd(name, du, h, scale, dh_in, lx, rows_out, gate=None):
    r, d = du.shape
    rin = dh_in.shape[0]
    tr = _row_tile(math.gcd(r, math.gcd(rin, rows_out)), lx, d)
    nx, nin, nout = lx // tr, rin // tr, rows_out // tr
    cls = lambda i: (jnp.where(i < nx, 1, 0), 0, 0)
    assert gate is None or rows_out == r

    def body(du_ref, h_ref, sc_ref, dhin_ref, *rest):
        if gate is None:
            dh_ref, dsc_ref, dsh_ref = rest
        else:
            f_ref, g_ref, dh_ref, dsc_ref, dsh_ref, df_ref, dg_ref = rest
        i = pl.program_id(0)
        hv = h_ref[...]
        duv = du_ref[...]
        rinv = lax.rsqrt(jnp.mean(hv * hv, axis=-1, keepdims=True) + NORM_EPS)
        nv = hv * rinv
        dn = duv * (1.0 + sc_ref[...])

        @pl.when(jnp.logical_or(i == 0, i == nx))
        def _():
            dsc_ref[...] = jnp.zeros(dsc_ref.shape, F32)
            dsh_ref[...] = jnp.zeros(dsh_ref.shape, F32)

        dsc_ref[...] += jnp.sum(duv * nv, axis=0, keepdims=True)
        dsh_ref[...] += jnp.sum(duv, axis=0, keepdims=True)

        def dh_out():
            dh = rinv * (dn - nv * jnp.mean(dn * nv, axis=-1, keepdims=True))
            return dh + jnp.where(i < nin, dhin_ref[...], 0.0)

        if gate is None:
            @pl.when(i < nout)
            def _():
                dh_ref[...] = dh_out()
        else:
            dhv = dh_out()
            dh_ref[...] = dhv

            @pl.when(jnp.logical_or(i == 0, i == nx))
            def _():
                dg_ref[...] = jnp.zeros(dg_ref.shape, F32)

            df_ref[...] = (gate[2] * g_ref[...] * dhv).astype(df_ref.dtype)
            dg_ref[...] += jnp.sum(gate[2] * f_ref[...].astype(F32) * dhv, axis=0, keepdims=True)

    row = pl.BlockSpec((tr, d), lambda i: (i, 0))
    per_class = pl.BlockSpec((None, 1, d), cls)
    in_specs = [row, row, per_class, pl.BlockSpec((tr, d), lambda i: (jnp.minimum(i, nin - 1), 0))]
    out_specs = [pl.BlockSpec((tr, d), lambda i: (jnp.minimum(i, nout - 1), 0)), per_class, per_class]
    out_shape = [jax.ShapeDtypeStruct((rows_out, d), F32), jax.ShapeDtypeStruct((2, 1, d), F32),
                 jax.ShapeDtypeStruct((2, 1, d), F32)]
    args = [du, h, scale, dh_in]
    if gate is not None:
        in_specs += [row, per_class]
        out_specs += [row, per_class]
        out_shape += [jax.ShapeDtypeStruct((r, d), BF16), jax.ShapeDtypeStruct((2, 1, d), F32)]
        args += [gate[0], gate[1]]
    return pl.pallas_call(
        body, name=name, grid=(r // tr,), in_specs=in_specs, out_specs=out_specs, out_shape=out_shape,
        compiler_params=_cp(("arbitrary",)))(*args)


def _final_loss(name, h, final_norm, target, f, gate, gate_scale):
    r, d = h.shape
    tr = _row_tile(r, r, d)

    def body(h_ref, fn_ref, t_ref, f_ref, g_ref, dh_ref, dfn_ref, loss_ref, df_ref, dg_ref):
        i = pl.program_id(0)
        hv = h_ref[...]
        rinv = lax.rsqrt(jnp.mean(hv * hv, axis=-1, keepdims=True) + NORM_EPS)
        nv = hv * rinv
        fn = fn_ref[...]
        err = nv * fn - t_ref[...]
        dy = err * (1.0 / d)

        @pl.when(i == 0)
        def _():
            dfn_ref[...] = jnp.zeros(dfn_ref.shape, F32)
            loss_ref[...] = jnp.zeros(loss_ref.shape, F32)
            dg_ref[...] = jnp.zeros(dg_ref.shape, F32)

        loss_ref[...] += 0.5 * jnp.sum(jnp.mean(err * err, axis=-1, keepdims=True), axis=0, keepdims=True)
        dfn_ref[...] += jnp.sum(dy * nv, axis=0, keepdims=True)
        dn = dy * fn
        dhv = rinv * (dn - nv * jnp.mean(dn * nv, axis=-1, keepdims=True))
        dh_ref[...] = dhv
        df_ref[...] = (gate_scale * g_ref[...] * dhv).astype(df_ref.dtype)
        dg_ref[...] += jnp.sum(gate_scale * f_ref[...].astype(F32) * dhv, axis=0, keepdims=True)

    row = pl.BlockSpec((tr, d), lambda i: (i, 0))
    one = pl.BlockSpec((1, d), lambda i: (0, 0))
    return pl.pallas_call(
        body, name=name, grid=(r // tr,),
        in_specs=[row, one, row, row, pl.BlockSpec((None, 1, d), lambda i: (1, 0, 0))],
        out_specs=[row, one, pl.BlockSpec((SUBLANES, LANES), lambda i: (0, 0)), row, one],
        out_shape=[jax.ShapeDtypeStruct((r, d), F32), jax.ShapeDtypeStruct((1, d), F32),
                   jax.ShapeDtypeStruct((SUBLANES, LANES), F32), jax.ShapeDtypeStruct((r, d), BF16),
                   jax.ShapeDtypeStruct((1, d), F32)],
        compiler_params=_cp(("arbitrary",)))(h, final_norm, target, f, gate)


def _swap_pairs(t):
    lane = lax.broadcasted_iota(jnp.int32, t.shape, 1)
    nxt = pltpu.roll(t, HEAD_DIM - 1, 1)
    prv = pltpu.roll(t, 1, 1)
    return jnp.where(lane % 2 == 0, nxt, prv)


def _qk_prep(name, p, cosf, sinf, q_gain, k_gain, aw, kw):
    t = p.shape[0]
    tr = _tile(t, 256, SUBLANES)
    q_prescale = HEAD_DIM ** -0.5 * LOG2E

    def head_fwd(v, gain, cs, sn):
        v = v.astype(F32)
        rinv = lax.rsqrt(jnp.mean(v * v, axis=-1, keepdims=True) + NORM_EPS)
        tt = v * rinv * gain
        return tt * cs + _swap_pairs(tt) * sn

    def body(q_ref, k_ref, cos_ref, sin_ref, qg_ref, kg_ref, qo_ref, ko_ref):
        cs, sn = cos_ref[...], sin_ref[...]
        for hh in range(aw // HEAD_DIM):
            sl = slice(hh * HEAD_DIM, (hh + 1) * HEAD_DIM)
            qo_ref[:, sl] = (head_fwd(q_ref[:, sl], qg_ref[...], cs, sn) * q_prescale).astype(qo_ref.dtype)
        for hh in range(kw // HEAD_DIM):
            sl = slice(hh * HEAD_DIM, (hh + 1) * HEAD_DIM)
            ko_ref[:, sl] = head_fwd(k_ref[:, sl], kg_ref[...], cs, sn).astype(ko_ref.dtype)

    assert aw % kw == 0
    row = lambda i: (i, 0)
    return pl.pallas_call(
        body, name=name, grid=(t // tr,),
        in_specs=[pl.BlockSpec((tr, aw), row), pl.BlockSpec((tr, kw), lambda i: (i, aw // kw)),
                  pl.BlockSpec((tr, HEAD_DIM), row), pl.BlockSpec((tr, HEAD_DIM), row),
                  pl.BlockSpec((1, HEAD_DIM), lambda i: (0, 0)), pl.BlockSpec((1, HEAD_DIM), lambda i: (0, 0))],
        out_specs=[pl.BlockSpec((tr, aw), row), pl.BlockSpec((tr, kw), row)],
        out_shape=[jax.ShapeDtypeStruct((t, aw), BF16), jax.ShapeDtypeStruct((t, kw), BF16)],
        compiler_params=_cp(("parallel",)))(p, p, cosf, sinf, q_gain, k_gain)


def _qk_prep_bwd(name, dq_rot, dk_rot, p, cosf, sinf, q_gain, k_gain, aw, kw):
    t = p.shape[0]
    lq = dq_rot.shape[0]
    tr = _tile(math.gcd(t, lq), 256, SUBLANES)
    nq = lq // tr

    def head_bwd(dout, v, gain, cs, sn):
        v = v.astype(F32)
        rinv = lax.rsqrt(jnp.mean(v * v, axis=-1, keepdims=True) + NORM_EPS)
        vn = v * rinv
        dt = dout * cs - _swap_pairs(dout) * sn
        dvn = dt * gain
        dv = rinv * (dvn - vn * jnp.mean(dvn * vn, axis=-1, keepdims=True))
        return dv, jnp.sum(dt * vn, axis=0, keepdims=True)

    def body(dq_ref, dk_ref, q_ref, k_ref, cos_ref, sin_ref, qg_ref, kg_ref, dqo_ref, dko_ref, dqg_ref, dkg_ref):
        i = pl.program_id(0)
        cs, sn = cos_ref[...], sin_ref[...]

        @pl.when(i == 0)
        def _():
            dqg_ref[...] = jnp.zeros(dqg_ref.shape, F32)
            dkg_ref[...] = jnp.zeros(dkg_ref.shape, F32)

        has_q = i < nq
        for hh in range(aw // HEAD_DIM):
            sl = slice(hh * HEAD_DIM, (hh + 1) * HEAD_DIM)
            dout = jnp.where(has_q, dq_ref[:, sl], 0.0)
            dv, dg = head_bwd(dout, q_ref[:, sl], qg_ref[...], cs, sn)
            dqo_ref[:, sl] = dv.astype(dqo_ref.dtype)
            dqg_ref[...] += dg
        for hh in range(kw // HEAD_DIM):
            sl = slice(hh * HEAD_DIM, (hh + 1) * HEAD_DIM)
            dv, dg = head_bwd(dk_ref[:, sl], k_ref[:, sl], kg_ref[...], cs, sn)
            dko_ref[:, sl] = dv.astype(dko_ref.dtype)
            dkg_ref[...] += dg

    row = lambda i: (i, 0)
    one = lambda i: (0, 0)
    return pl.pallas_call(
        body, name=name, grid=(t // tr,),
        in_specs=[pl.BlockSpec((tr, aw), lambda i: (jnp.minimum(i, nq - 1), 0)), pl.BlockSpec((tr, kw), row),
                  pl.BlockSpec((tr, aw), row), pl.BlockSpec((tr, kw), lambda i: (i, aw // kw)),
                  pl.BlockSpec((tr, HEAD_DIM), row), pl.BlockSpec((tr, HEAD_DIM), row),
                  pl.BlockSpec((1, HEAD_DIM), one), pl.BlockSpec((1, HEAD_DIM), one)],
        out_specs=[pl.BlockSpec((tr, aw), row), pl.BlockSpec((tr, kw), row),
                   pl.BlockSpec((1, HEAD_DIM), one), pl.BlockSpec((1, HEAD_DIM), one)],
        out_shape=[jax.ShapeDtypeStruct((t, aw), BF16), jax.ShapeDtypeStruct((t, kw), BF16),
                   jax.ShapeDtypeStruct((1, HEAD_DIM), F32), jax.ShapeDtypeStruct((1, HEAD_DIM), F32)],
        compiler_params=_cp(("arbitrary",)))(dq_rot, dk_rot, p, p, cosf, sinf, q_gain, k_gain)


def _attn_tiles(lq, t):
    return _tile(lq, 1024), _tile(t, 768)


def _attn_fwd(name, q, k, p, v_off, lq, groups):
    t = k.shape[0]
    hq = q.shape[1] // HEAD_DIM
    hkv = hq // groups
    gw = groups * HEAD_DIM
    tq, tk = _attn_tiles(lq, t)
    nkv = t // tk
    vb = v_off // HEAD_DIM

    def body(q_ref, k_ref, v_ref, o_ref, lse_ref, m_sc, l_sc, acc_sc, s_sc):
        j = pl.program_id(2)

        @pl.when(j == 0)
        def _():
            m_sc[...] = jnp.full(m_sc.shape, -jnp.inf, F32)
            l_sc[...] = jnp.zeros(l_sc.shape, F32)
            acc_sc[...] = jnp.zeros(acc_sc.shape, F32)
            s_sc[...] = jnp.zeros(s_sc.shape, F32)

        has_prev = j > 0
        kv, vv = k_ref[...], v_ref[...]
        for g in range(groups):
            sl = slice(g * HEAD_DIM, (g + 1) * HEAD_DIM)
            s = s_sc[g]
            s_sc[g] = _dot(q_ref[:, sl], kv, NT)
            m_prev = m_sc[g]
            m_new = jnp.where(has_prev, jnp.maximum(m_prev, jnp.max(s, axis=-1, keepdims=True)), jnp.inf)
            alpha = jnp.exp2(m_prev - m_new)
            pexp = jnp.exp2(s - m_new)
            part = pexp[:, 0:LANES]
            for cb in range(1, tk // LANES):
                part = part + pexp[:, cb * LANES:(cb + 1) * LANES]
            l_sc[g] = alpha * l_sc[g] + part
            acc_sc[:, sl] = alpha * acc_sc[:, sl] + _dot(pexp.astype(BF16), vv, NN)
            m_sc[g] = jnp.where(has_prev, m_new, -jnp.inf)

        @pl.when(j == nkv)
        def _():
            for g in range(groups):
                sl = slice(g * HEAD_DIM, (g + 1) * HEAD_DIM)
                l_row = jnp.sum(l_sc[g], axis=-1, keepdims=True)
                o_ref[:, sl] = (acc_sc[:, sl] * (1.0 / l_row)).astype(o_ref.dtype)
                lse_ref[g] = jnp.broadcast_to(m_sc[g] + jnp.log2(l_row), (tq, LANES))

    return pl.pallas_call(
        body, name=name, grid=(hkv, lq // tq, nkv + 1),
        in_specs=[pl.BlockSpec((tq, gw), lambda h, i, j: (i, h)),
                  pl.BlockSpec((tk, HEAD_DIM), lambda h, i, j: (jnp.minimum(j, nkv - 1), h)),
                  pl.BlockSpec((tk, HEAD_DIM), lambda h, i, j: (jnp.maximum(j - 1, 0), vb + h))],
        out_specs=[pl.BlockSpec((tq, gw), lambda h, i, j: (i, h)),
                   pl.BlockSpec((groups, tq, LANES), lambda h, i, j: (h, i, 0))],
        out_shape=[jax.ShapeDtypeStruct((lq, hq * HEAD_DIM), BF16), jax.ShapeDtypeStruct((hq, lq, LANES), F32)],
        scratch_shapes=[pltpu.VMEM((groups, tq, 1), F32), pltpu.VMEM((groups, tq, LANES), F32), pltpu.VMEM((tq, gw), F32),
                        pltpu.VMEM((groups, tq, tk), F32)],
        compiler_params=_cp(("parallel", "parallel", "arbitrary")))(q, k, p)


def _attn_delta(name, o, do):
    lq, aw = o.shape
    hq = aw // HEAD_DIM
    tr = _tile(lq, 512, SUBLANES)

    def body(o_ref, do_ref, d_ref):
        for h in range(hq):
            sl = slice(h * HEAD_DIM, (h + 1) * HEAD_DIM)
            dsum = jnp.sum(do_ref[:, sl].astype(F32) * o_ref[:, sl].astype(F32), axis=-1, keepdims=True)
            d_ref[h] = jnp.broadcast_to(dsum, (tr, LANES))

    spec = pl.BlockSpec((tr, aw), lambda i: (i, 0))
    return pl.pallas_call(
        body, name=name, grid=(lq // tr,), in_specs=[spec, spec],
        out_specs=pl.BlockSpec((hq, tr, LANES), lambda i: (0, i, 0)),
        out_shape=jax.ShapeDtypeStruct((hq, lq, LANES), F32), compiler_params=_cp(("parallel",)))(o, do)


def _attn_bwd(name, q, k, p, v_off, do, lse, delta, lq, groups):
    t = k.shape[0]
    hkv = k.shape[1] // HEAD_DIM
    gw = groups * HEAD_DIM
    tq, tk = _attn_tiles(lq, t)
    nq, nkv = lq // tq, t // tk
    scale = HEAD_DIM ** -0.5
    vb = v_off // HEAD_DIM

    def body(q_ref, k_ref, v_ref, do_ref, lse_ref, delta_ref, dq_ref, dk_ref, dv_ref, dq_sc, dk_sc, dv_sc):
        j, i = pl.program_id(1), pl.program_id(2)

        @pl.when(i == 0)
        def _():
            dk_sc[...] = jnp.zeros(dk_sc.shape, F32)
            dv_sc[...] = jnp.zeros(dv_sc.shape, F32)

        @pl.when(j == 0)
        def _():
            dq_sc[i] = jnp.zeros((tq, gw), F32)

        kv, vv = k_ref[...], v_ref[...]
        dk_part, dv_part = None, None
        for g in range(groups):
            sl = slice(g * HEAD_DIM, (g + 1) * HEAD_DIM)
            qv, dov = q_ref[:, sl], do_ref[:, sl]
            s = _dot(qv, kv, NT)
            pexp = jnp.exp2(s - lse_ref[g, :, 0:1])
            dv_g = _dot(pexp.astype(BF16), dov, TN)
            dp = _dot(dov, vv, NT)
            ds = (pexp * (dp - delta_ref[g, :, 0:1])).astype(BF16)
            dk_g = _dot(ds, qv, TN)
            dq_sc[i, :, sl] += _dot(ds, kv, NN)
            dk_part = dk_g if dk_part is None else dk_part + dk_g
            dv_part = dv_g if dv_part is None else dv_part + dv_g
        dk_sc[...] += dk_part
        dv_sc[...] += dv_part

        @pl.when(i == nq - 1)
        def _():
            dk_ref[...] = dk_sc[...] * LN2
            dv_ref[...] = dv_sc[...].astype(dv_ref.dtype)

        @pl.when(j == nkv - 1)
        def _():
            dq_ref[...] = dq_sc[i] * scale

    qspec = pl.BlockSpec((tq, gw), lambda kh, j, i: (i, kh))
    kspec = pl.BlockSpec((tk, HEAD_DIM), lambda kh, j, i: (j, kh))
    rowspec = pl.BlockSpec((groups, tq, LANES), lambda kh, j, i: (kh, i, 0))
    dqspec = pl.BlockSpec((tq, gw), lambda kh, j, i: (jnp.where(j == nkv - 1, i, 0), kh))
    return pl.pallas_call(
        body, name=name, grid=(hkv, nkv, nq),
        in_specs=[qspec, kspec, pl.BlockSpec((tk, HEAD_DIM), lambda kh, j, i: (j, vb + kh)), qspec, rowspec, rowspec],
        out_specs=[dqspec, kspec, kspec],
        out_shape=[jax.ShapeDtypeStruct((lq, hkv * gw), F32), jax.ShapeDtypeStruct((t, hkv * HEAD_DIM), F32),
                   jax.ShapeDtypeStruct((t, hkv * HEAD_DIM), BF16)],
        scratch_shapes=[pltpu.VMEM((nq, tq, gw), F32), pltpu.VMEM((tk, HEAD_DIM), F32), pltpu.VMEM((tk, HEAD_DIM), F32)],
        compiler_params=_cp(("parallel", "arbitrary", "arbitrary")))(q, k, p, do, lse, delta)


def _ret_tables(log_gamma, direction):
    c = RET_CHUNK
    idx = jnp.arange(c, dtype=F32)
    diff = idx[:, None] - idx[None, :]
    if direction == 1:
        diff = -diff
    keep = diff >= 0
    lg = log_gamma.astype(F32)
    mask = jnp.where(keep[None], jnp.exp(jnp.where(keep, diff, 0.0)[None] * lg[:, None, None]), 0.0)
    q_exp = idx + 1.0 if direction == 0 else c - idx
    k_exp = c - 1.0 - idx if direction == 0 else idx
    sign = 1.0 if direction == 0 else -1.0
    lane = lambda v: jnp.broadcast_to(v[..., None], v.shape + (LANES,))
    qdec = lane(jnp.exp(q_exp[None, :] * lg[:, None]))
    kdec = lane(jnp.exp(k_exp[None, :] * lg[:, None]))
    cdec = jnp.broadcast_to(jnp.exp(c * lg)[:, None, None], (lg.shape[0], SUBLANES, LANES))
    weights = lane(jnp.stack([sign * idx, q_exp, -sign * idx, k_exp], axis=0))
    return mask, qdec, kdec, cdec, weights


def _ret_chunk_of(direction, step, nx, nc):
    if direction == 0:
        return jnp.where(step < nc, nx + step, step - nc)
    return jnp.where(step < nc, nx + nc - 1 - step, nx + nc - 1 - step)


def _ret_fwd(name, p, offs, tables, direction, lx, prev):
    q_off, k_off, v_off, rw = offs
    t = p.shape[0]
    c = RET_CHUNK
    n_steps, nx = t // c, lx // c
    nc = n_steps - nx
    bw = math.gcd(math.gcd(q_off, k_off), math.gcd(v_off, rw))
    bw = _tile(bw, 512)
    hpb = bw // HEAD_DIM
    heads = rw // HEAD_DIM
    k_scale = HEAD_DIM ** -0.5
    mask, qdec, kdec, cdec, _ = tables
    rc = lambda n: _ret_chunk_of(direction, n, nx, nc)

    ng = heads // hpb

    def body(*refs):
        q_refs, k_refs, v_refs = refs[0:ng], refs[ng:2 * ng], refs[2 * ng:3 * ng]
        if prev is None:
            m_ref, qd_ref, kd_ref, cd_ref, y_ref, st_ref, s_sc = refs[3 * ng:]
        else:
            m_ref, qd_ref, kd_ref, cd_ref, prev_ref, y_ref, st_ref, s_sc = refs[3 * ng:]
        n = pl.program_id(0)

        @pl.when(n == 0)
        def _():
            s_sc[...] = jnp.zeros(s_sc.shape, F32)

        for hd in range(heads):
            gg, hh = divmod(hd, hpb)
            sl = slice(hh * HEAD_DIM, (hh + 1) * HEAD_DIM)
            osl = slice(hd * HEAD_DIM, (hd + 1) * HEAD_DIM)
            qv = q_refs[gg][:, sl]
            kf = k_refs[gg][:, sl].astype(F32) * k_scale
            kv = kf.astype(BF16)
            vv = v_refs[gg][:, sl]
            state = s_sc[hd]
            st_ref[hd] = state
            a = _dot(qv, kv, NT) * m_ref[hd]
            y = _dot(a.astype(BF16), vv, NN) + _dot(qv, state.astype(BF16), NN) * qd_ref[hd]
            s_sc[hd] = state * cd_ref[hd, 0:1, :] + _dot((kf * kd_ref[hd]).astype(BF16), vv, TN)
            if prev is not None:
                y = y + prev_ref[:, osl]
            y_ref[:, osl] = y

    col = lambda off, gg: (lambda n: (rc(n), off // bw + gg))
    tab3 = lambda n: (0, 0, 0)
    in_specs = [pl.BlockSpec((c, bw), col(off, gg)) for off in (q_off, k_off, v_off) for gg in range(ng)]
    in_specs += [pl.BlockSpec((heads, c, c), tab3), pl.BlockSpec((heads, c, LANES), tab3),
                 pl.BlockSpec((heads, c, LANES), tab3), pl.BlockSpec((heads, SUBLANES, LANES), tab3)]
    args = [p] * (3 * ng) + [mask, qdec, kdec, cdec]
    aliases = {}
    yspec = pl.BlockSpec((c, rw), lambda n: (rc(n), 0))
    if prev is not None:
        in_specs.append(yspec)
        args.append(prev)
        aliases = {len(args) - 1: 0}
    return pl.pallas_call(
        body, name=name, grid=(n_steps,), in_specs=in_specs,
        out_specs=[yspec, pl.BlockSpec((None, heads, HEAD_DIM, HEAD_DIM), lambda n: (n, 0, 0, 0))],
        out_shape=[jax.ShapeDtypeStruct((t, rw), F32), jax.ShapeDtypeStruct((n_steps, heads, HEAD_DIM, HEAD_DIM), F32)],
        scratch_shapes=[pltpu.VMEM((heads, HEAD_DIM, HEAD_DIM), F32)], input_output_aliases=aliases,
        compiler_params=_cp(("arbitrary",)))(*args)


def _ret_bwd(name, p, offs, tables, states, dy, direction, lx, prev):
    q_off, k_off, v_off, rw = offs
    t = p.shape[0]
    c = RET_CHUNK
    n_steps, nx = t // c, lx // c
    nc = n_steps - nx
    bw = math.gcd(math.gcd(q_off, k_off), math.gcd(v_off, rw))
    bw = _tile(bw, 512)
    hpb = bw // HEAD_DIM
    heads = rw // HEAD_DIM
    k_scale = HEAD_DIM ** -0.5
    mask, qdec, kdec, cdec, weights = tables
    step_of = lambda n: n_steps - 1 - n
    rc = lambda n: _ret_chunk_of(direction, step_of(n), nx, nc)

    ng = heads // hpb

    def body(*refs):
        q_refs, k_refs, v_refs = refs[0:ng], refs[ng:2 * ng], refs[2 * ng:3 * ng]
        if prev is None:
            (dy_ref, st_ref, m_ref, qd_ref, kd_ref, cd_ref, w_ref,
             dq_ref, dk_ref, dv_ref, dl_ref, ds_sc, lam_sc) = refs[3 * ng:]
        else:
            (dy_ref, st_ref, m_ref, qd_ref, kd_ref, cd_ref, w_ref, pq_ref, pk_ref, pv_ref,
             dq_ref, dk_ref, dv_ref, dl_ref, ds_sc, lam_sc) = refs[3 * ng:]
        n = pl.program_id(0)

        @pl.when(n == 0)
        def _():
            ds_sc[...] = jnp.zeros(ds_sc.shape, F32)
            lam_sc[...] = jnp.zeros(lam_sc.shape, F32)

        is_x = rc(n) < nx
        for hd in range(heads):
            gg, hh = divmod(hd, hpb)
            sl = slice(hh * HEAD_DIM, (hh + 1) * HEAD_DIM)
            osl = slice(hd * HEAD_DIM, (hd + 1) * HEAD_DIM)
            qv = q_refs[gg][:, sl]
            qf = qv.astype(F32)
            kf = k_refs[gg][:, sl].astype(F32) * k_scale
            kv = kf.astype(BF16)
            vv = v_refs[gg][:, sl]
            dyv = jnp.where(is_x, dy_ref[:, osl], 0.0).astype(BF16)
            state = st_ref[hd]
            dstate = ds_sc[hd]
            dstate_b = dstate.astype(BF16)
            msk, qd, kd = m_ref[hd], qd_ref[hd], kd_ref[hd]
            cd = cd_ref[hd, 0:1, :]
            a = _dot(qv, kv, NT) * msk
            da = (_dot(dyv, vv, NT) * msk).astype(BF16)
            dq_intra = _dot(da, kv, NN)
            dk_intra = _dot(da, qv, TN)
            dq_inter = _dot(dyv, state.astype(BF16), NT) * qd
            dk_inter = _dot(vv, dstate_b, NT) * kd
            dv = _dot(a.astype(BF16), dyv, TN) + _dot((kf * kd).astype(BF16), dstate_b, NN)
            lam_sc[hd] += (qf * (w_ref[0] * dq_intra + w_ref[1] * dq_inter)
                           + kf * (w_ref[2] * dk_intra + w_ref[3] * dk_inter)
                           + (c * cd) * state * dstate)
            ds_sc[hd] = _dot((qf * qd).astype(BF16), dyv, TN) + cd * dstate
            dq = dq_intra + dq_inter
            dk = (dk_intra + dk_inter) * k_scale
            if prev is not None:
                dq = dq + pq_ref[:, osl]
                dk = dk + pk_ref[:, osl]
                dv = dv + pv_ref[:, osl]
            dq_ref[:, osl] = dq
            dk_ref[:, osl] = dk
            dv_ref[:, osl] = dv

        @pl.when(n == n_steps - 1)
        def _():
            for hd in range(heads):
                dl_ref[hd] = jnp.broadcast_to(jnp.sum(lam_sc[hd]), (SUBLANES, LANES))

    col = lambda off, gg: (lambda n: (rc(n), off // bw + gg))
    tab3 = lambda n: (0, 0, 0)
    ospec = pl.BlockSpec((c, rw), lambda n: (rc(n), 0))
    in_specs = [pl.BlockSpec((c, bw), col(off, gg)) for off in (q_off, k_off, v_off) for gg in range(ng)]
    in_specs += [pl.BlockSpec((c, rw), lambda n: (jnp.minimum(rc(n), nx - 1), 0)),
                 pl.BlockSpec((None, heads, HEAD_DIM, HEAD_DIM), lambda n: (step_of(n), 0, 0, 0)),
                 pl.BlockSpec((heads, c, c), tab3), pl.BlockSpec((heads, c, LANES), tab3), pl.BlockSpec((heads, c, LANES), tab3),
                 pl.BlockSpec((heads, SUBLANES, LANES), tab3), pl.BlockSpec((4, c, LANES), tab3)]
    args = [p] * (3 * ng) + [dy, states, mask, qdec, kdec, cdec, weights]
    aliases = {}
    if prev is not None:
        for k_out, arr in enumerate(prev):
            in_specs.append(ospec)
            args.append(arr)
            aliases[len(args) - 1] = k_out
    big = jax.ShapeDtypeStruct((t, rw), F32)
    return pl.pallas_call(
        body, name=name, grid=(n_steps,), in_specs=in_specs,
        out_specs=[ospec, ospec, ospec, pl.BlockSpec((heads, SUBLANES, LANES), tab3)],
        out_shape=[big, big, big, jax.ShapeDtypeStruct((heads, SUBLANES, LANES), F32)],
        scratch_shapes=[pltpu.VMEM((heads, HEAD_DIM, HEAD_DIM), F32), pltpu.VMEM((heads, HEAD_DIM, HEAD_DIM), F32)],
        input_output_aliases=aliases, compiler_params=_cp(("arbitrary",)))(*args)


def _ret_out(name, y, p, g_off, lx):
    rw = y.shape[1]
    bw = _tile(math.gcd(g_off, rw), 512)
    tr = _tile(lx, 512, SUBLANES)

    def body(y_ref, g_ref, o_ref):
        for hh in range(bw // HEAD_DIM):
            sl = slice(hh * HEAD_DIM, (hh + 1) * HEAD_DIM)
            yv = y_ref[:, sl]
            gv = g_ref[:, sl].astype(F32)
            rinv = lax.rsqrt(jnp.mean(yv * yv, axis=-1, keepdims=True) + NORM_EPS)
            o_ref[:, sl] = (gv * _sigmoid(gv) * yv * rinv).astype(o_ref.dtype)

    spec = pl.BlockSpec((tr, bw), lambda i, g: (i, g))
    return pl.pallas_call(
        body, name=name, grid=(lx // tr, rw // bw),
        in_specs=[spec, pl.BlockSpec((tr, bw), lambda i, g: (i, g_off // bw + g))],
        out_specs=spec, out_shape=jax.ShapeDtypeStruct((lx, rw), BF16),
        compiler_params=_cp(("parallel", "parallel")))(y, p)


def _ret_out_bwd(name, dyr, y, p, g_off, lx):
    rw = y.shape[1]
    bw = _tile(math.gcd(g_off, rw), 512)
    tr = _tile(lx, 512, SUBLANES)

    def body(d_ref, y_ref, g_ref, dy_ref, dg_ref):
        for hh in range(bw // HEAD_DIM):
            sl = slice(hh * HEAD_DIM, (hh + 1) * HEAD_DIM)
            yv = y_ref[:, sl]
            gv = g_ref[:, sl].astype(F32)
            dv = d_ref[:, sl]
            rinv = lax.rsqrt(jnp.mean(yv * yv, axis=-1, keepdims=True) + NORM_EPS)
            yn = yv * rinv
            sg = _sigmoid(gv)
            dg_ref[:, sl] = (dv * yn * sg * (1.0 + gv * (1.0 - sg))).astype(dg_ref.dtype)
            dyn = dv * gv * sg
            dy_ref[:, sl] = rinv * (dyn - yn * jnp.mean(dyn * yn, axis=-1, keepdims=True))

    spec = pl.BlockSpec((tr, bw), lambda i, g: (i, g))
    return pl.pallas_call(
        body, name=name, grid=(lx // tr, rw // bw),
        in_specs=[spec, spec, pl.BlockSpec((tr, bw), lambda i, g: (i, g_off // bw + g))],
        out_specs=[spec, spec],
        out_shape=[jax.ShapeDtypeStruct((lx, rw), F32), jax.ShapeDtypeStruct((lx, rw), BF16)],
        compiler_params=_cp(("parallel", "parallel")))(dyr, y, p)


def _merge(name, pa, pr, p, ga_off, gb_off):
    r, d = pa.shape
    cw = _tile(math.gcd(math.gcd(ga_off, gb_off), d), 1024)
    tr = _tile(r, 512, SUBLANES)

    def body(pa_ref, pr_ref, ga_ref, gb_ref, o_ref):
        o_ref[...] = (_sigmoid(ga_ref[...].astype(F32)) * pa_ref[...].astype(F32)
                      + _sigmoid(gb_ref[...].astype(F32)) * pr_ref[...].astype(F32)).astype(o_ref.dtype)

    spec = pl.BlockSpec((tr, cw), lambda i, j: (i, j))
    return pl.pallas_call(
        body, name=name, grid=(r // tr, d // cw),
        in_specs=[spec, spec, pl.BlockSpec((tr, cw), lambda i, j: (i, ga_off // cw + j)),
                  pl.BlockSpec((tr, cw), lambda i, j: (i, gb_off // cw + j))],
        out_specs=spec, out_shape=jax.ShapeDtypeStruct((r, d), BF16),
        compiler_params=_cp(("parallel", "parallel")))(pa, pr, p, p)


def _ada_fwd(name, cond, w):
    rows, d = cond.shape
    n = w.shape[1]
    tn = _tile(n, 768)

    def body(c_ref, w_ref, o_ref, s_ref):
        cv = c_ref[...]
        sv = cv * _sigmoid(cv)
        s_ref[...] = sv
        o_ref[...] = _dot(sv.astype(BF16), w_ref[...].astype(BF16), NN)

    return pl.pallas_call(
        body, name=name, grid=(n // tn,),
        in_specs=[pl.BlockSpec((rows, d), lambda j: (0, 0)), pl.BlockSpec((d, tn), lambda j: (0, j))],
        out_specs=[pl.BlockSpec((rows, tn), lambda j: (0, j)), pl.BlockSpec((rows, d), lambda j: (0, 0))],
        out_shape=[jax.ShapeDtypeStruct((rows, n), F32), jax.ShapeDtypeStruct((rows, d), F32)],
        compiler_params=_cp(("arbitrary",)))(cond, w)


def _ada_bwd(name, s_cond, dmod, w):
    rows, d = s_cond.shape
    n = w.shape[1]
    tn = _tile(n, 768)

    def body(s_ref, dm_ref, w_ref, gw_ref, ds_ref):
        j = pl.program_id(0)

        @pl.when(j == 0)
        def _():
            ds_ref[...] = jnp.zeros(ds_ref.shape, F32)

        dmv = dm_ref[...].astype(BF16)
        gw_ref[...] = _dot(s_ref[...].astype(BF16), dmv, TN)
        ds_ref[...] += _dot(dmv, w_ref[...].astype(BF16), NT)

    return pl.pallas_call(
        body, name=name, grid=(n // tn,),
        in_specs=[pl.BlockSpec((rows, d), lambda j: (0, 0)), pl.BlockSpec((rows, tn), lambda j: (0, j)),
                  pl.BlockSpec((d, tn), lambda j: (0, j))],
        out_specs=[pl.BlockSpec((d, tn), lambda j: (0, j)), pl.BlockSpec((rows, d), lambda j: (0, 0))],
        out_shape=[jax.ShapeDtypeStruct((d, n), F32), jax.ShapeDtypeStruct((rows, d), F32)],
        compiler_params=_cp(("arbitrary",)))(s_cond, dmod, w)


def _sum_slots(name, a):
    s, r, c = a.shape

    def body(a_ref, o_ref):
        acc = a_ref[0]
        for k in range(1, s):
            acc = acc + a_ref[k]
        o_ref[...] = acc

    return pl.pallas_call(
        body, name=name, grid=(1,), in_specs=[pl.BlockSpec((s, r, c), lambda i: (0, 0, 0))],
        out_specs=pl.BlockSpec((r, c), lambda i: (0, 0)), out_shape=jax.ShapeDtypeStruct((r, c), F32),
        compiler_params=_cp(("arbitrary",)))(a)


def _cctx_grad(name, parts, c_ctx):
    s, r, d = parts.shape

    def body(p_ref, c_ref, o_ref):
        acc = p_ref[0]
        for k in range(1, s):
            acc = acc + p_ref[k]
        cv = c_ref[...]
        sg = _sigmoid(cv)
        o_ref[...] = acc[0:1, :] * sg * (1.0 + cv * (1.0 - sg))

    return pl.pallas_call(
        body, name=name, grid=(1,),
        in_specs=[pl.BlockSpec((s, r, d), lambda i: (0, 0, 0)), pl.BlockSpec((1, d), lambda i: (0, 0))],
        out_specs=pl.BlockSpec((1, d), lambda i: (0, 0)), out_shape=jax.ShapeDtypeStruct((1, d), F32),
        compiler_params=_cp(("arbitrary",)))(parts, c_ctx)


def _adamw(name, slots, w, m, v):
    s, r, c = slots.shape
    tr = _tile(r, 256 if c > 1024 else 512, SUBLANES)
    c1 = 1.0 - ADAM_B1 ** ADAM_STEP
    c2 = 1.0 - ADAM_B2 ** ADAM_STEP

    def body(s_ref, w_ref, m_ref, v_ref, g_ref, d_ref, mo_ref, vo_ref):
        g = s_ref[0].astype(F32)
        for k in range(1, s):
            g = g + s_ref[k].astype(F32)
        mn = ADAM_B1 * m_ref[...] + (1.0 - ADAM_B1) * g
        vn = ADAM_B2 * v_ref[...] + (1.0 - ADAM_B2) * (g * g)
        m_hat = mn / c1
        v_hat = vn / c2
        g_ref[...] = g
        mo_ref[...] = mn
        vo_ref[...] = vn
        d_ref[...] = -ADAM_LR * (m_hat / (jnp.sqrt(v_hat) + ADAM_EPS) + ADAM_WD * w_ref[...])

    spec = pl.BlockSpec((tr, c), lambda i: (i, 0))
    shp = jax.ShapeDtypeStruct((r, c), F32)
    return pl.pallas_call(
        body, name=name, grid=(r // tr,),
        in_specs=[pl.BlockSpec((s, tr, c), lambda i: (0, i, 0)), spec, spec, spec],
        out_specs=[spec] * 4, out_shape=[shp] * 4, compiler_params=_cp(("parallel",)))(slots, w, m, v)


def _coords():
    return lax.axis_index("x"), lax.axis_index("y"), lax.axis_index("c")


def _flip(coord, bit):
    return 1 - coord if bit else coord


def _all_gather_small(name, blk):
    r, ccols = blk.shape

    def body(x_ref, out_ref, send_sems, recv_sems, local_sem):
        x, y, c = _coords()
        me, sibling = (x, y, c), (x, y, 1 - c)
        chips = [(1 - x, y), (x, 1 - y), (1 - x, 1 - y)]

        def slot(px, py, pc):
            return out_ref.at[4 * px + 2 * py + pc]

        def copy(k, block, to, src=None):
            return pltpu.make_async_remote_copy(
                src_ref=slot(*block) if src is None else src, dst_ref=slot(*block),
                send_sem=send_sems.at[k], recv_sem=recv_sems.at[k], device_id=to, device_id_type=MESH)

        mine = pltpu.make_async_copy(x_ref, slot(*me), local_sem)
        mine.start()
        first = [copy(0, me, sibling, src=x_ref)]
        first += [copy(1 + j, me, (*chip, c), src=x_ref) for j, chip in enumerate(chips)]
        for cp in first:
            cp.start()
        passed = [copy(4 + j, (*chip, c), sibling) for j, chip in enumerate(chips)]
        for j, chip in enumerate(chips):
            copy(1 + j, (*chip, c), me).wait_recv()
            passed[j].start()
        copy(0, sibling, me).wait_recv()
        for j, chip in enumerate(chips):
            copy(4 + j, (*chip, 1 - c), me).wait_recv()
        for cp in first + passed:
            cp.wait_send()
        mine.wait()

    return pl.pallas_call(
        body, name=name, out_shape=jax.ShapeDtypeStruct((N_DEV, r, ccols), blk.dtype),
        in_specs=[pl.BlockSpec(memory_space=pltpu.VMEM)], out_specs=pl.BlockSpec(memory_space=pltpu.VMEM),
        scratch_shapes=[pltpu.SemaphoreType.DMA((7,)), pltpu.SemaphoreType.DMA((7,)), pltpu.SemaphoreType.DMA],
    )(blk)


def _comm_semaphores(n):
    return [pltpu.SemaphoreType.DMA((7 * n,)), pltpu.SemaphoreType.DMA((7 * n,)), pltpu.SemaphoreType.DMA((n,))]


def _comm_out_shape(kind, s, wide):
    if kind == "gather":
        shape = (s.shape[0], N_DEV * s.shape[1]) if wide else (N_DEV,) + s.shape
    else:
        shape = (N_DEV, s.shape[0], s.shape[1] // N_DEV) if wide else s.shape
    return jax.ShapeDtypeStruct(shape, s.dtype)


def _block_of(ref, idx, wide, cols):
    if not wide:
        return ref.at[idx]
    return ref.at[:, pl.ds(pl.multiple_of(idx * cols, LANES), cols)]


def _gather_plan(ins, outs, wide, send_sems, recv_sems, local_sems):
    n = len(ins)
    x, y, c = _coords()
    me, sibling = (x, y, c), (x, y, 1 - c)
    chips = [(1 - x, y), (x, 1 - y), (1 - x, 1 - y)]

    def slot(a, px, py, pc):
        return _block_of(outs[a], 4 * px + 2 * py + pc, wide[a], ins[a].shape[-1])

    def copy(a, k, block, to, src=None):
        return pltpu.make_async_remote_copy(
            src_ref=slot(a, *block) if src is None else src, dst_ref=slot(a, *block),
            send_sem=send_sems.at[7 * a + k], recv_sem=recv_sems.at[7 * a + k], device_id=to, device_id_type=MESH)

    def local(a):
        return pltpu.make_async_copy(ins[a], slot(a, *me), local_sems.at[a])

    def first(a):
        return [copy(a, 0, me, sibling, src=ins[a])] + [copy(a, 1 + j, me, (*chip, c), src=ins[a])
                                                        for j, chip in enumerate(chips)]

    def passed(a, j):
        return copy(a, 4 + j, (*chips[j], c), sibling)

    def start():
        for a in range(n):
            local(a).start()
            for cp in first(a):
                cp.start()

    def pass_on():
        for a in range(n):
            for j, chip in enumerate(chips):
                copy(a, 1 + j, (*chip, c), me).wait_recv()
                passed(a, j).start()

    def finish():
        for a in range(n):
            copy(a, 0, sibling, me).wait_recv()
            for j, chip in enumerate(chips):
                copy(a, 4 + j, (*chip, 1 - c), me).wait_recv()
        for a in range(n):
            for cp in first(a) + [passed(a, j) for j in range(3)]:
                cp.wait_send()
            local(a).wait()

    return start, pass_on, finish


def _exchange_plan(ins, outs, wide, send_sems, recv_sems, local_sems):
    n = len(ins)
    x, y, c = _coords()
    my_idx = 4 * x + 2 * y + c

    def src(a, idx):
        return _block_of(ins[a], idx, wide[a], outs[a].shape[-1])

    def local(a):
        return pltpu.make_async_copy(src(a, my_idx), outs[a].at[my_idx], local_sems.at[a])

    def pair(a, rel):
        px, py, pc = _flip(x, rel & 4), _flip(y, rel & 2), _flip(c, rel & 1)
        peer_idx = 4 * px + 2 * py + pc
        sems = dict(send_sem=send_sems.at[7 * a + rel - 1], recv_sem=recv_sems.at[7 * a + rel - 1],
                    device_id=(px, py, pc), device_id_type=MESH)
        send = pltpu.make_async_remote_copy(src_ref=src(a, peer_idx), dst_ref=outs[a].at[my_idx], **sems)
        recv = pltpu.make_async_remote_copy(src_ref=src(a, my_idx), dst_ref=outs[a].at[peer_idx], **sems)
        return send, recv

    def start():
        for a in range(n):
            local(a).start()
            for rel in range(1, N_DEV):
                pair(a, rel)[0].start()

    def finish():
        for a in range(n):
            for rel in range(1, N_DEV):
                send, recv = pair(a, rel)
                recv.wait_recv()
                send.wait_send()
            local(a).wait()

    return start, None, finish


def _comm_plan(kind, ins, outs, wide, send_sems, recv_sems, local_sems):
    plan = {"gather": _gather_plan, "exchange": _exchange_plan}[kind]
    return plan(ins, outs, wide, send_sems, recv_sems, local_sems)


def _all_gather_big(name, shards, wide):
    n = len(shards)

    def body(*refs):
        start, pass_on, finish = _gather_plan(refs[:n], refs[n:2 * n], wide, *refs[2 * n:])
        start()
        pass_on()
        finish()

    any_spec = pl.BlockSpec(memory_space=pl.ANY)
    return pl.pallas_call(
        body, name=name, out_shape=[_comm_out_shape("gather", s, w) for s, w in zip(shards, wide)],
        in_specs=[any_spec] * n, out_specs=[any_spec] * n, scratch_shapes=_comm_semaphores(n))(*shards)


def _rope_tables(lx, lc):
    rows = lx // GRID_W
    row = jnp.repeat(jnp.arange(rows, dtype=F32), GRID_W)
    col = jnp.tile(jnp.arange(GRID_W, dtype=F32), rows)
    half = HEAD_DIM // 2
    inv_freq = ROPE_THETA ** (-jnp.arange(0, half, 2, dtype=F32) / half)
    ang = jnp.concatenate([row[:, None] * inv_freq, col[:, None] * inv_freq], axis=-1)
    cos, sin = jnp.cos(ang), jnp.sin(ang)
    cosf = jnp.repeat(cos, 2, axis=-1)
    sinf = jnp.stack([-sin, sin], axis=-1).reshape(lx, HEAD_DIM)
    cosf = jnp.concatenate([cosf, jnp.ones((lc, HEAD_DIM), F32)], axis=0)
    sinf = jnp.concatenate([sinf, jnp.zeros((lc, HEAD_DIM), F32)], axis=0)
    return cosf, sinf


def _cols_full(g):
    return jnp.transpose(g, (1, 0, 2)).reshape(g.shape[1], N_DEV * g.shape[2])


def _cols_split(w):
    k, n = w.shape
    return jnp.transpose(w.reshape(k, N_DEV, n // N_DEV), (1, 0, 2))


def _pad_rows(a, rows):
    return jnp.pad(a, ((0, rows - a.shape[0]), (0, 0)))


def kernel(x, c, ctx, c_ctx, w_ada, b_ada, ffn1_w_in, ffn1_w_out, mix_w_in, attn_q_gain, attn_k_gain, ret_decay_logit, w_proj_attn, w_proj_ret, mix_w_out, ffn2_w_in, ffn2_w_out, final_norm, loss_target, m_c_ctx, m_w_ada, m_b_ada, m_ffn1_w_in, m_ffn1_w_out, m_mix_w_in, m_attn_q_gain, m_attn_k_gain, m_ret_decay_logit, m_w_proj_attn, m_w_proj_ret, m_mix_w_out, m_ffn2_w_in, m_ffn2_w_out, m_final_norm, v_c_ctx, v_w_ada, v_b_ada, v_ffn1_w_in, v_ffn1_w_out, v_mix_w_in, v_attn_q_gain, v_attn_k_gain, v_ret_decay_logit, v_w_proj_attn, v_w_proj_ret, v_mix_w_out, v_ffn2_w_in, v_ffn2_w_out, v_final_norm):
    lx, d = x.shape[1], x.shape[2]
    lc = ctx.shape[1]
    t = lx + lc
    aw, rw = w_proj_attn.shape[1], w_proj_ret.shape[1]
    pw = mix_w_in.shape[2] * N_DEV
    kw = (pw - aw - 4 * rw - 2 * d) // 2
    groups = aw // kw
    heads_r = rw // HEAD_DIM
    ka_off, va_off = aw, aw + kw
    qr_off = aw + 2 * kw
    kr_off, vr_off, gr_off = qr_off + rw, qr_off + 2 * rw, qr_off + 3 * rw
    ga_off, gb_off = qr_off + 4 * rw, qr_off + 4 * rw + d
    xi, yi, ci = _coords()
    me = 4 * xi + 2 * yi + ci

    col_names = ["ffn1_w_in", "mix_w_in", "w_proj_attn", "w_proj_ret", "ffn2_w_in"]
    row_names = ["ffn1_w_out", "mix_w_out", "ffn2_w_out"]
    shard = dict(ffn1_w_in=ffn1_w_in[0], mix_w_in=mix_w_in[0], w_proj_attn=w_proj_attn[0], w_proj_ret=w_proj_ret[0],
                 ffn2_w_in=ffn2_w_in[0], ffn1_w_out=ffn1_w_out[0], mix_w_out=mix_w_out[0], ffn2_w_out=ffn2_w_out[0])
    names = col_names + row_names
    bf_shard = {k: shard[k].astype(BF16) for k in names}
    full, landed = {}, {}

    def is_wide(k):
        return k in col_names and shard[k].shape[1] % LANES == 0

    def gather_of(keys):
        return "gather", [bf_shard[k] for k in keys], [is_wide(k) for k in keys]

    def keep_full(keys, gathered):
        for k, g in zip(keys, gathered):
            if is_wide(k):
                full[k] = g
            else:
                full[k] = _cols_full(g) if k in col_names else g.reshape(N_DEV * g.shape[1], g.shape[2])

    def exchange_of(grads):
        blocks = [g if is_wide(k) else _cols_split(g) if k in col_names
                  else g.reshape(N_DEV, g.shape[0] // N_DEV, g.shape[1]) for k, g in grads.items()]
        return "exchange", blocks, [is_wide(k) for k in grads]

    def keep_landed(grads, got):
        landed.update(zip(grads.keys(), got))

    keep_full(["ffn1_w_in"], _all_gather_big("gather_first", *gather_of(["ffn1_w_in"])[1:]))

    c_all = _all_gather_small("gather_cond", _pad_rows(c, SUBLANES))[:, 0, :]
    cond = _pad_rows(jnp.concatenate([c_all, c_ctx[None, :]], axis=0), 2 * SUBLANES)
    ada_part, s_cond = _ada_fwd("ada_fwd", cond, w_ada[0])
    ada_all = _all_gather_small("gather_ada", ada_part)
    mod_all = jnp.transpose(ada_all, (1, 0, 2)).reshape(2 * SUBLANES, N_MOD * d) + b_ada
    mod_x = lax.dynamic_index_in_dim(mod_all, me, axis=0, keepdims=False).reshape(N_MOD, d)
    mod_c = mod_all[N_DEV].reshape(N_MOD, d)
    mods = jnp.stack([mod_c, mod_x], axis=0)[:, :, None, :]
    sh1, sc1, g1, sh2, sc2, g2, sh3, sc3, g3 = [mods[:, k] for k in range(N_MOD)]

    h0 = jnp.concatenate([x[0], ctx[0]], axis=0)
    u1 = _rmsmod("ffn1_norm", h0, sc1, sh1, lx)
    keys = ["ffn1_w_out", "mix_w_in"]
    (z1, s1), got = _ffn_in("ffn1_in", u1, full["ffn1_w_in"], comm=gather_of(keys))
    keep_full(keys, got)
    keys = ["w_proj_attn", "w_proj_ret", "mix_w_out"]
    (h1, f1), got = _mm_residual("ffn1_out", s1, full["ffn1_w_out"], h0, g1, 0.5, lx, comm=gather_of(keys))
    keep_full(keys, got)

    u2 = _rmsmod("mix_norm", h1, sc2, sh2, lx)
    keys = ["ffn2_w_in"]
    p, got = _mm_nn("mix_in", u2, full["mix_w_in"], BF16, tm_pref=M_TILE_BIG, tn_pref=512, comm=gather_of(keys))
    keep_full(keys, got)
    cosf, sinf = _rope_tables(lx, lc)
    q_rot, k_rot = _qk_prep("qk_prep", p, cosf, sinf, attn_q_gain, attn_k_gain, aw, kw)
    ya, lse = _attn_fwd("attn_fwd", q_rot, k_rot, p, va_off, lx, groups)

    decay = ret_decay_logit[0].astype(F32)
    log_gamma = jax.nn.log_sigmoid(decay)
    r_offs = (qr_off, kr_off, vr_off, rw)
    tab = [_ret_tables(log_gamma[k], k) for k in range(2)]
    y_f, st_f = _ret_fwd("ret_fwd_a", p, r_offs, tab[0], 0, lx, None)
    y_r, st_b = _ret_fwd("ret_fwd_b", p, r_offs, tab[1], 1, lx, y_f)
    yr = _ret_out("ret_out", y_r, p, gr_off, lx)

    pa = _mm_nn("proj_attn", ya, full["w_proj_attn"], BF16)
    pr = _mm_nn("proj_ret", yr, full["w_proj_ret"], BF16)
    mg = _merge("merge", pa, pr, p, ga_off, gb_off)
    h2, o2 = _mm_residual("mix_out", mg, full["mix_w_out"], h1, g2, 1.0, lx)

    u3 = _rmsmod("ffn2_norm", h2, sc3, sh3, lx)
    keys = ["ffn2_w_out"]
    (z3, s3), got = _ffn_in("ffn2_in", u3, full["ffn2_w_in"], comm=gather_of(keys))
    keep_full(keys, got)
    h3, f3 = _mm_residual("ffn2_out", s3, full["ffn2_w_out"], h2, g3, 0.5, lx)
    dh3, d_fn, loss_tile, df3, dg3 = _final_loss("final_loss", h3, final_norm[None, :], loss_target[0], f3, g3, 0.5)

    dz3 = _ffn_out_bwd("ffn2_out_bwd", df3, full["ffn2_w_out"], z3)
    grads = {"ffn2_w_out": _mm_tn("ffn2_out_dw", s3, df3, BF16, tm_pref=M_TILE_BIG)}
    g_ffn2_w_in, got = _ffn_in_bwd_w("ffn2_in_dw", u3, dz3, comm=exchange_of(grads))
    keep_landed(grads, got)
    grads = {"ffn2_w_in": g_ffn2_w_in}
    du3, got = _ffn_in_bwd_x("ffn2_in_bwd", dz3, full["ffn2_w_in"], comm=exchange_of(grads))
    keep_landed(grads, got)
    dh2, dsc3, dsh3, do2, dg2 = _rmsmod_bwd("ffn2_norm_bwd", du3, h2, sc3, dh3, lx, lx, gate=(o2, g2, 1.0))

    dpa, dpr, dga, dgb = _merge_bwd("merge_bwd", do2, full["mix_w_out"], pa, pr, p, ga_off, gb_off)
    dya = _mm_nt("proj_attn_bwd", dpa, full["w_proj_attn"], BF16)
    dyr = _mm_nt("proj_ret_bwd", dpr, full["w_proj_ret"], F32)
    mix_grads = {"mix_w_out": _mm_tn("mix_out_dw", mg, do2, BF16),
                 "w_proj_attn": _mm_tn("proj_attn_dw", ya, dpa, BF16),
                 "w_proj_ret": _mm_tn("proj_ret_dw", yr, dpr, BF16)}

    dy_ret, dgr = _ret_out_bwd("ret_out_bwd", dyr, y_r, p, gr_off, lx)
    dqr, dkr, dvr, dl_f = _ret_bwd("ret_bwd_a", p, r_offs, tab[0], st_f, dy_ret, 0, lx, None)
    dqr, dkr, dvr, dl_b = _ret_bwd("ret_bwd_b", p, r_offs, tab[1], st_b, dy_ret, 1, lx, (dqr, dkr, dvr))
    d_lam = jnp.stack([dl_f[:, 0, 0], dl_b[:, 0, 0]], axis=0)
    d_decay = d_lam * jax.nn.sigmoid(-decay)

    delta = _attn_delta("attn_delta", ya, dya)
    dq_rot, dk_rot, dva = _attn_bwd("attn_bwd", q_rot, k_rot, p, va_off, dya, lse, delta, lx, groups)
    dqa, dka, d_qg, d_kg = _qk_prep_bwd("qk_prep_bwd", dq_rot, dk_rot, p, cosf, sinf, attn_q_gain, attn_k_gain, aw, kw)

    dp = jnp.concatenate([dqa, dka, dva, dqr.astype(BF16), dkr.astype(BF16), dvr.astype(BF16),
                          _pad_rows(dgr, t), _pad_rows(dga, t), _pad_rows(dgb, t)], axis=1)
    grads = {"mix_w_in": _mm_tn("mix_in_dw", u2, dp, BF16, tm_pref=M_TILE_DW, tn_pref=512)}
    du2, got = _mm_nt("mix_in_bwd", dp, full["mix_w_in"], F32, comm=exchange_of(grads))
    keep_landed(grads, got)
    dh1, dsc2, dsh2, df1, dg1 = _rmsmod_bwd("mix_norm_bwd", du2, h1, sc2, dh2, lx, t, gate=(f1, g1, 0.5))

    dz1, got = _ffn_out_bwd("ffn1_out_bwd", df1, full["ffn1_w_out"], z1, comm=exchange_of(mix_grads))
    keep_landed(mix_grads, got)
    grads = {"ffn1_w_out": _mm_tn("ffn1_out_dw", s1, df1, BF16, tm_pref=M_TILE_BIG)}
    g_ffn1_w_in, got = _ffn_in_bwd_w("ffn1_in_dw", u1, dz1, comm=exchange_of(grads))
    keep_landed(grads, got)
    grads = {"ffn1_w_in": g_ffn1_w_in}
    du1, got = _ffn_in_bwd_x("ffn1_in_bwd", dz1, full["ffn1_w_in"], comm=exchange_of(grads))
    keep_landed(grads, got)
    grad_x, dsc1, dsh1 = _rmsmod_bwd("ffn1_norm_bwd", du1, h0, sc1, dh1, lx, lx)

    zero = jnp.zeros((1, d), F32)
    dmod_c = jnp.concatenate([dsh1[0], dsc1[0], dg1[0], dsh2[0], dsc2[0], zero, zero, zero, zero], axis=0)
    dmod_x = jnp.concatenate([dsh1[1], dsc1[1], dg1[1], dsh2[1], dsc2[1], dg2[1], dsh3[1], dsc3[1], dg3], axis=0)
    misc = jnp.concatenate([d_qg[0], d_kg[0], d_decay.reshape(-1), loss_tile[0, 0:1]])
    misc = jnp.pad(misc, (0, d - misc.shape[0]))[None, :]
    n_small = 3 * SUBLANES
    small = _pad_rows(jnp.concatenate([dmod_c, dmod_x, d_fn, misc], axis=0), n_small)
    small_all = _all_gather_small("gather_small", small)
    small_sum = _sum_slots("sum_small", small_all)
    dmod_c_sum, dmod_x_sum = small_sum[0:N_MOD], small_sum[N_MOD:2 * N_MOD]
    g_final_norm = small_sum[2 * N_MOD]
    misc_sum = small_sum[2 * N_MOD + 1]
    g_qg = misc_sum[0:HEAD_DIM][None, :]
    g_kg = misc_sum[HEAD_DIM:2 * HEAD_DIM][None, :]
    g_decay = misc_sum[2 * HEAD_DIM:2 * HEAD_DIM + 2 * heads_r].reshape(1, 2, heads_r)
    loss = misc_sum[2 * HEAD_DIM + 2 * heads_r]
    g_b_ada = (dmod_x_sum + dmod_c_sum).reshape(1, N_MOD * d)

    n_ada = w_ada.shape[2]
    dmod_rows = jnp.concatenate([small_all[:, N_MOD:2 * N_MOD, :].reshape(N_DEV, N_MOD * d),
                                 dmod_c_sum.reshape(1, N_MOD * d)], axis=0)
    dmod_mine = _pad_rows(lax.dynamic_slice_in_dim(dmod_rows, me * n_ada, n_ada, axis=1), 2 * SUBLANES)
    g_w_ada, ds_cond = _ada_bwd("ada_bwd", s_cond, dmod_mine, w_ada[0])
    cctx_parts = _all_gather_small("gather_cctx", ds_cond[N_DEV:N_DEV + SUBLANES])
    g_c_ctx = _cctx_grad("cctx_grad", cctx_parts, c_ctx[None, :])[0]

    mom = dict(ffn1_w_in=(m_ffn1_w_in, v_ffn1_w_in), mix_w_in=(m_mix_w_in, v_mix_w_in),
               w_proj_attn=(m_w_proj_attn, v_w_proj_attn), w_proj_ret=(m_w_proj_ret, v_w_proj_ret),
               ffn2_w_in=(m_ffn2_w_in, v_ffn2_w_in), ffn1_w_out=(m_ffn1_w_out, v_ffn1_w_out),
               mix_w_out=(m_mix_w_out, v_mix_w_out), ffn2_w_out=(m_ffn2_w_out, v_ffn2_w_out))
    res = {}
    for k in names:
        res[k] = _adamw("adamw_" + k, landed[k], shard[k], mom[k][0][0], mom[k][1][0])
    res["w_ada"] = _adamw("adamw_w_ada", g_w_ada[None], w_ada[0], m_w_ada[0], v_w_ada[0])

    def pack(cc, ba, qg, kg, dec, fn):
        misc_row = jnp.concatenate([qg.reshape(-1), kg.reshape(-1), dec.reshape(-1)])
        misc_row = jnp.pad(misc_row, (0, d - misc_row.shape[0]))[None, :]
        return _pad_rows(jnp.concatenate([cc.reshape(1, d), ba.reshape(N_MOD, d), fn.reshape(1, d), misc_row], axis=0),
                         2 * SUBLANES)

    sg, sd, sm, sv = _adamw(
        "adamw_small", pack(g_c_ctx, g_b_ada, g_qg, g_kg, g_decay, g_final_norm)[None],
        pack(c_ctx, b_ada, attn_q_gain, attn_k_gain, ret_decay_logit, final_norm),
        pack(m_c_ctx, m_b_ada, m_attn_q_gain, m_attn_k_gain, m_ret_decay_logit, m_final_norm),
        pack(v_c_ctx, v_b_ada, v_attn_q_gain, v_attn_k_gain, v_ret_decay_logit, v_final_norm))

    def unpack(a):
        misc_row = a[N_MOD + 2]
        return dict(c_ctx=a[0], b_ada=a[1:1 + N_MOD].reshape(1, N_MOD * d), final_norm=a[N_MOD + 1],
                    attn_q_gain=misc_row[0:HEAD_DIM][None, :], attn_k_gain=misc_row[HEAD_DIM:2 * HEAD_DIM][None, :],
                    ret_decay_logit=misc_row[2 * HEAD_DIM:2 * HEAD_DIM + 2 * heads_r].reshape(1, 2, heads_r))

    small_out = [unpack(a) for a in (sg, sd, sm, sv)]
    order = ["c_ctx", "w_ada", "b_ada", "ffn1_w_in", "ffn1_w_out", "mix_w_in", "attn_q_gain", "attn_k_gain",
             "ret_decay_logit", "w_proj_attn", "w_proj_ret", "mix_w_out", "ffn2_w_in", "ffn2_w_out", "final_norm"]
    outs = [loss, grad_x[None]]
    for which in range(4):
        for k in order:
            outs.append(res[k][which][None] if k in res else small_out[which][k])
    return tuple(outs)
```

```python
import math

import jax
import jax.numpy as jnp
from jax import lax
from jax.experimental import pallas as pl
from jax.experimental.pallas import tpu as pltpu

F32 = jnp.float32
BF16 = jnp.bfloat16
MESH = pl.DeviceIdType.MESH

N_DEV = 8
HEAD_DIM = 128
GRID_W = 64
ROPE_THETA = 10000.0
NORM_EPS = 1e-6
RET_CHUNK = 128
N_MOD = 9
LANES = 128
SUBLANES = 8
V7X_VMEM_BYTES = 64 * 1024 * 1024
VMEM_LIMIT = V7X_VMEM_BYTES - 8 * 1024 * 1024
K_TILE = 2560
M_TILE_BIG = 1408
M_TILE_DW = 2048
LOG2E = 1.4426950408889634
LN2 = 0.6931471805599453

ADAM_LR = 0.001
ADAM_B1 = 0.9
ADAM_B2 = 0.999
ADAM_EPS = 1e-08
ADAM_WD = 0.01
ADAM_STEP = 10

NN = (((1,), (0,)), ((), ()))
NT = (((1,), (1,)), ((), ()))
TN = (((0,), (0,)), ((), ()))


def _tile(n, pref, align=LANES):
    best = None
    t = align
    while t <= min(n, pref):
        if n % t == 0:
            best = t
        t += align
    return n if best is None else best


def _cp(sem):
    return pltpu.CompilerParams(dimension_semantics=sem, vmem_limit_bytes=VMEM_LIMIT)


def _sigmoid(v):
    return 0.5 * jnp.tanh(0.5 * v) + 0.5


def _dot(a, b, dims):
    return lax.dot_general(a, b, dims, preferred_element_type=F32)


def _mm(name, a, a_spec, b_list, dims, grid, out_shapes, out_specs, acc_shape, epi, extras=(), comm=None):
    nb, ne, no = len(b_list), len(extras), len(out_shapes)
    nk = grid[2]
    kind, c_arrays, c_wide = comm if comm is not None else (None, [], [])
    ncm = len(c_arrays)

    def body(*refs):
        a_ref = refs[0]
        b_refs = refs[1:1 + nb]
        e_refs = refs[1 + nb:1 + nb + ne]
        pos = 1 + nb + ne
        c_ins = refs[pos:pos + ncm]
        o_refs = refs[pos + ncm:pos + ncm + no]
        c_outs = refs[pos + ncm + no:pos + 2 * ncm + no]
        scratch = refs[pos + 2 * ncm + no:]
        accs = scratch[:0 if nk == 1 else nb]
        ids = [pl.program_id(axis) for axis in range(3)]
        if ncm:
            start_comm, mid_comm, finish_comm = _comm_plan(kind, c_ins, c_outs, c_wide, *scratch[len(accs):])
            step = (ids[0] * grid[1] + ids[1]) * nk + ids[2]
            pl.when(step == 0)(start_comm)
            if mid_comm is not None:
                pl.when(step == (3 * grid[0] * grid[1] * nk) // 4)(mid_comm)

        def finish(tiles):
            vals = epi(tiles, e_refs)
            for o_ref, v in zip(o_refs, vals):
                if isinstance(v, tuple):
                    for idx, part in enumerate(v):
                        o_ref[idx] = part.astype(o_ref.dtype)
                else:
                    o_ref[...] = v.astype(o_ref.dtype)

        if nk == 1:
            finish([_dot(a_ref[...], b_ref[...], dims) for b_ref in b_refs])
        else:
            @pl.when(ids[2] == 0)
            def _():
                for acc in accs:
                    acc[...] = jnp.zeros(acc.shape, F32)

            av = a_ref[...]
            for b_ref, acc in zip(b_refs, accs):
                acc[...] += _dot(av, b_ref[...], dims)

            @pl.when(ids[2] == nk - 1)
            def _():
                finish([acc[...] for acc in accs])

        if ncm:
            pl.when(jnp.logical_and(jnp.logical_and(ids[0] == grid[0] - 1, ids[1] == grid[1] - 1),
                                    ids[2] == nk - 1))(finish_comm)

    any_spec = pl.BlockSpec(memory_space=pl.ANY)
    c_shapes = [_comm_out_shape(kind, s, w) for s, w in zip(c_arrays, c_wide)]
    scratch_shapes = [] if nk == 1 else [pltpu.VMEM(acc_shape, F32)] * nb
    if ncm:
        scratch_shapes = scratch_shapes + _comm_semaphores(ncm)
    semantics = ("arbitrary",) * 3 if ncm else ("parallel", "parallel", "arbitrary")
    outs = pl.pallas_call(
        body, name=name, grid=grid,
        in_specs=[a_spec] + [s for _, s in b_list] + [s for _, s in extras] + [any_spec] * ncm,
        out_specs=list(out_specs) + [any_spec] * ncm, out_shape=list(out_shapes) + c_shapes,
        scratch_shapes=scratch_shapes, compiler_params=_cp(semantics),
    )(a, *[b for b, _ in b_list], *[e for e, _ in extras], *c_arrays)
    main = outs[0] if no == 1 else tuple(outs[:no])
    return main if comm is None else (main, list(outs[no:]))


def _plain(accs, _):
    return (accs[0],)


def _mm_nn(name, a, b, out_dtype, tm_pref=1024, tn_pref=1024, tk_pref=K_TILE, comm=None):
    m, k = a.shape
    n = b.shape[1]
    tm, tn, tk = _tile(m, tm_pref), _tile(n, tn_pref), _tile(k, tk_pref)
    return _mm(name, a, pl.BlockSpec((tm, tk), lambda i, j, kk: (i, kk)),
               [(b, pl.BlockSpec((tk, tn), lambda i, j, kk: (kk, j)))], NN, (m // tm, n // tn, k // tk),
               [jax.ShapeDtypeStruct((m, n), out_dtype)], [pl.BlockSpec((tm, tn), lambda i, j, kk: (i, j))],
               (tm, tn), _plain, comm=comm)


def _mm_nt(name, a, b, out_dtype, tm_pref=1024, tn_pref=1024, tk_pref=K_TILE, comm=None):
    m, k = a.shape
    n = b.shape[0]
    tm, tn, tk = _tile(m, tm_pref), _tile(n, tn_pref), _tile(k, tk_pref)
    return _mm(name, a, pl.BlockSpec((tm, tk), lambda i, j, kk: (i, kk)),
               [(b, pl.BlockSpec((tn, tk), lambda i, j, kk: (j, kk)))], NT, (m // tm, n // tn, k // tk),
               [jax.ShapeDtypeStruct((m, n), out_dtype)], [pl.BlockSpec((tm, tn), lambda i, j, kk: (i, j))],
               (tm, tn), _plain, comm=comm)


def _mm_tn(name, a, b, out_dtype, rows=None, tm_pref=1024, tn_pref=1024, tk_pref=K_TILE, comm=None):
    k = a.shape[0] if rows is None else rows
    m, n = a.shape[1], b.shape[1]
    tm, tn, tk = _tile(m, tm_pref), _tile(n, tn_pref), _tile(k, tk_pref)
    return _mm(name, a, pl.BlockSpec((tk, tm), lambda i, j, kk: (kk, i)),
               [(b, pl.BlockSpec((tk, tn), lambda i, j, kk: (kk, j)))], TN, (m // tm, n // tn, k // tk),
               [jax.ShapeDtypeStruct((m, n), out_dtype)], [pl.BlockSpec((tm, tn), lambda i, j, kk: (i, j))],
               (tm, tn), _plain, comm=comm)


def _ffn_in(name, u, w_in, comm=None):
    r, d = u.shape
    f = w_in.shape[1] // 2
    tm, tn, tk = _tile(r, M_TILE_BIG), _tile(f, 512), _tile(d, K_TILE)
    nf = f // tn

    def epi(accs, _):
        za, zb = accs
        s = za * _sigmoid(za) * zb
        return (za, zb), s

    return _mm(name, u, pl.BlockSpec((tm, tk), lambda i, j, kk: (i, kk)),
               [(w_in, pl.BlockSpec((tk, tn), lambda i, j, kk: (kk, j))),
                (w_in, pl.BlockSpec((tk, tn), lambda i, j, kk: (kk, j + nf)))],
               NN, (r // tm, nf, d // tk),
               [jax.ShapeDtypeStruct((2, r, f), BF16), jax.ShapeDtypeStruct((r, f), BF16)],
               [pl.BlockSpec((2, tm, tn), lambda i, j, kk: (0, i, j)), pl.BlockSpec((tm, tn), lambda i, j, kk: (i, j))],
               (tm, tn), epi, comm=comm)


def _mm_residual(name, a, w, res, gate, gate_scale, lx, comm=None):
    r, k = a.shape
    n = w.shape[1]
    tm, tn, tk = _tile(r, 1024), _tile(n, 1024), _tile(k, K_TILE)

    def epi(accs, e_refs):
        res_ref, g_ref = e_refs
        rows = pl.program_id(0) * tm + lax.broadcasted_iota(jnp.int32, (tm, 1), 0)
        g = jnp.where(rows < lx, g_ref[1], g_ref[0])
        return res_ref[...] + gate_scale * g * accs[0], accs[0]

    return _mm(name, a, pl.BlockSpec((tm, tk), lambda i, j, kk: (i, kk)),
               [(w, pl.BlockSpec((tk, tn), lambda i, j, kk: (kk, j)))], NN, (r // tm, n // tn, k // tk),
               [jax.ShapeDtypeStruct((r, n), F32), jax.ShapeDtypeStruct((r, n), BF16)],
               [pl.BlockSpec((tm, tn), lambda i, j, kk: (i, j))] * 2, (tm, tn), epi,
               extras=[(res, pl.BlockSpec((tm, tn), lambda i, j, kk: (i, j))),
                       (gate, pl.BlockSpec((2, 1, tn), lambda i, j, kk: (0, 0, j)))], comm=comm)


def _ffn_out_bwd(name, df, w_out, z, comm=None):
    r, d = df.shape
    f = w_out.shape[0]
    tm, tn, tk = _tile(r, M_TILE_BIG), _tile(f, 512), _tile(d, K_TILE)

    def epi(accs, e_refs):
        ds = accs[0]
        za = e_refs[0][0].astype(F32)
        zb = e_refs[0][1].astype(F32)
        sg = _sigmoid(za)
        da = ds * zb * sg * (1.0 + za * (1.0 - sg))
        db = ds * za * sg
        return ((da, db),)

    zspec = pl.BlockSpec((2, tm, tn), lambda i, j, kk: (0, i, j))
    return _mm(name, df, pl.BlockSpec((tm, tk), lambda i, j, kk: (i, kk)),
               [(w_out, pl.BlockSpec((tn, tk), lambda i, j, kk: (j, kk)))], NT, (r // tm, f // tn, d // tk),
               [jax.ShapeDtypeStruct((2, r, f), BF16)], [zspec], (tm, tn), epi, extras=[(z, zspec)], comm=comm)


def _ffn_in_bwd_x(name, dz, w_in, comm=None):
    _, r, f = dz.shape
    d = w_in.shape[0]
    tm, tn, tk = _tile(r, 1024), _tile(d, 2048), _tile(f, K_TILE)
    nkf = f // tk
    return _mm(name, dz, pl.BlockSpec((None, tm, tk), lambda i, j, kk: (kk // nkf, i, kk % nkf)),
               [(w_in, pl.BlockSpec((tn, tk), lambda i, j, kk: (j, kk)))], NT, (r // tm, d // tn, 2 * nkf),
               [jax.ShapeDtypeStruct((r, d), F32)], [pl.BlockSpec((tm, tn), lambda i, j, kk: (i, j))],
               (tm, tn), _plain, comm=comm)


def _ffn_in_bwd_w(name, u, dz, comm=None):
    r, d = u.shape
    f = dz.shape[2]
    tm, tn, tk = _tile(d, M_TILE_DW), _tile(f, 512), _tile(r, K_TILE)
    nf = f // tn
    return _mm(name, u, pl.BlockSpec((tk, tm), lambda i, j, kk: (kk, i)),
               [(dz, pl.BlockSpec((None, tk, tn), lambda i, j, kk: (j // nf, kk, j % nf)))], TN,
               (d // tm, 2 * nf, r // tk),
               [jax.ShapeDtypeStruct((d, 2 * f), BF16)], [pl.BlockSpec((tm, tn), lambda i, j, kk: (i, j))],
               (tm, tn), _plain, comm=comm)


def _merge_bwd(name, dout, w_out, pa, pr, p, ga_off, gb_off):
    r, d = dout.shape
    n = w_out.shape[0]
    cw = math.gcd(math.gcd(ga_off, gb_off), n)
    tm, tn, tk = _tile(r, 1024), _tile(cw, 512), _tile(d, K_TILE)

    def epi(accs, e_refs):
        dm = accs[0]
        pa_ref, pr_ref, ga_ref, gb_ref = e_refs
        sa = _sigmoid(ga_ref[...].astype(F32))
        sb = _sigmoid(gb_ref[...].astype(F32))
        pav = pa_ref[...].astype(F32)
        prv = pr_ref[...].astype(F32)
        return dm * sa, dm * sb, dm * pav * sa * (1.0 - sa), dm * prv * sb * (1.0 - sb)

    o_spec = pl.BlockSpec((tm, tn), lambda i, j, kk: (i, j))
    return _mm(name, dout, pl.BlockSpec((tm, tk), lambda i, j, kk: (i, kk)),
               [(w_out, pl.BlockSpec((tn, tk), lambda i, j, kk: (j, kk)))], NT, (r // tm, n // tn, d // tk),
               [jax.ShapeDtypeStruct((r, n), BF16)] * 4, [o_spec] * 4, (tm, tn), epi,
               extras=[(pa, o_spec), (pr, o_spec),
                       (p, pl.BlockSpec((tm, tn), lambda i, j, kk: (i, ga_off // tn + j))),
                       (p, pl.BlockSpec((tm, tn), lambda i, j, kk: (i, gb_off // tn + j)))])


def _row_tile(r, lx, d):
    pref = 256 if d > 1024 else 512
    return _tile(math.gcd(r, lx), pref, SUBLANES)


def _rmsmod(name, h, scale, shift, lx, rows=None, comm=None):
    r = h.shape[0] if rows is None else rows
    d = h.shape[1]
    tr = _row_tile(r, lx, d)
    nx = lx // tr
    steps = r // tr
    cls = lambda i: (jnp.where(i < nx, 1, 0), 0, 0)
    kind, c_arrays, c_wide = comm if comm is not None else (None, [], [])
    ncm = len(c_arrays)

    def body(h_ref, sc_ref, sh_ref, *rest):
        u_ref = rest[ncm]
        i = pl.program_id(0)
        if ncm:
            start_comm, mid_comm, finish_comm = _comm_plan(kind, rest[:ncm], rest[ncm + 1:2 * ncm + 1], c_wide,
                                                           *rest[2 * ncm + 1:])
            pl.when(i == 0)(start_comm)
            if mid_comm is not None:
                pl.when(i == steps - 1)(mid_comm)
        hv = h_ref[...]
        rinv = lax.rsqrt(jnp.mean(hv * hv, axis=-1, keepdims=True) + NORM_EPS)
        u_ref[...] = (hv * rinv * (1.0 + sc_ref[...]) + sh_ref[...]).astype(u_ref.dtype)
        if ncm:
            pl.when(i == steps - 1)(finish_comm)

    any_spec = pl.BlockSpec(memory_space=pl.ANY)
    outs = pl.pallas_call(
        body, name=name, grid=(steps,),
        in_specs=[pl.BlockSpec((tr, d), lambda i: (i, 0)), pl.BlockSpec((None, 1, d), cls),
                  pl.BlockSpec((None, 1, d), cls)] + [any_spec] * ncm,
        out_specs=[pl.BlockSpec((tr, d), lambda i: (i, 0))] + [any_spec] * ncm,
        out_shape=[jax.ShapeDtypeStruct((r, d), BF16)] + [_comm_out_shape(kind, s, w) for s, w in zip(c_arrays, c_wide)],
        scratch_shapes=_comm_semaphores(ncm) if ncm else [],
        compiler_params=_cp(("arbitrary",) if ncm else ("parallel",)))(h, scale, shift, *c_arrays)
    return outs[0] if comm is None else (outs[0], list(outs[1:]))


def _rmsmod_bwd(name, du, h, scale, dh_in, lx, rows_out, gate=None):
    r, d = du.shape
    rin = dh_in.shape[0]
    tr = _row_tile(math.gcd(r, math.gcd(rin, rows_out)), lx, d)
    nx, nin, nout = lx // tr, rin // tr, rows_out // tr
    cls = lambda i: (jnp.where(i < nx, 1, 0), 0, 0)
    assert gate is None or rows_out == r

    def body(du_ref, h_ref, sc_ref, dhin_ref, *rest):
        if gate is None:
            dh_ref, dsc_ref, dsh_ref = rest
        else:
            f_ref, g_ref, dh_ref, dsc_ref, dsh_ref, df_ref, dg_ref = rest
        i = pl.program_id(0)
        hv = h_ref[...]
        duv = du_ref[...]
        rinv = lax.rsqrt(jnp.mean(hv * hv, axis=-1, keepdims=True) + NORM_EPS)
        nv = hv * rinv
        dn = duv * (1.0 + sc_ref[...])

        @pl.when(jnp.logical_or(i == 0, i == nx))
        def _():
            dsc_ref[...] = jnp.zeros(dsc_ref.shape, F32)
            dsh_ref[...] = jnp.zeros(dsh_ref.shape, F32)

        dsc_ref[...] += jnp.sum(duv * nv, axis=0, keepdims=True)
        dsh_ref[...] += jnp.sum(duv, axis=0, keepdims=True)

        def dh_out():
            dh = rinv * (dn - nv * jnp.mean(dn * nv, axis=-1, keepdims=True))
            return dh + jnp.where(i < nin, dhin_ref[...], 0.0)

        if gate is None:
            @pl.when(i < nout)
            def _():
                dh_ref[...] = dh_out()
        else:
            dhv = dh_out()
            dh_ref[...] = dhv

            @pl.when(jnp.logical_or(i == 0, i == nx))
            def _():
                dg_ref[...] = jnp.zeros(dg_ref.shape, F32)

            df_ref[...] = (gate[2] * g_ref[...] * dhv).astype(df_ref.dtype)
            dg_ref[...] += jnp.sum(gate[2] * f_ref[...].astype(F32) * dhv, axis=0, keepdims=True)

    row = pl.BlockSpec((tr, d), lambda i: (i, 0))
    per_class = pl.BlockSpec((None, 1, d), cls)
    in_specs = [row, row, per_class, pl.BlockSpec((tr, d), lambda i: (jnp.minimum(i, nin - 1), 0))]
    out_specs = [pl.BlockSpec((tr, d), lambda i: (jnp.minimum(i, nout - 1), 0)), per_class, per_class]
    out_shape = [jax.ShapeDtypeStruct((rows_out, d), F32), jax.ShapeDtypeStruct((2, 1, d), F32),
                 jax.ShapeDtypeStruct((2, 1, d), F32)]
    args = [du, h, scale, dh_in]
    if gate is not None:
        in_specs += [row, per_class]
        out_specs += [row, per_class]
        out_shape += [jax.ShapeDtypeStruct((r, d), BF16), jax.ShapeDtypeStruct((2, 1, d), F32)]
        args += [gate[0], gate[1]]
    return pl.pallas_call(
        body, name=name, grid=(r // tr,), in_specs=in_specs, out_specs=out_specs, out_shape=out_shape,
        compiler_params=_cp(("arbitrary",)))(*args)


def _final_loss(name, h, final_norm, target, f, gate, gate_scale):
    r, d = h.shape
    tr = _row_tile(r, r, d)

    def body(h_ref, fn_ref, t_ref, f_ref, g_ref, dh_ref, dfn_ref, loss_ref, df_ref, dg_ref):
        i = pl.program_id(0)
        hv = h_ref[...]
        rinv = lax.rsqrt(jnp.mean(hv * hv, axis=-1, keepdims=True) + NORM_EPS)
        nv = hv * rinv
        fn = fn_ref[...]
        err = nv * fn - t_ref[...]
        dy = err * (1.0 / d)

        @pl.when(i == 0)
        def _():
            dfn_ref[...] = jnp.zeros(dfn_ref.shape, F32)
            loss_ref[...] = jnp.zeros(loss_ref.shape, F32)
            dg_ref[...] = jnp.zeros(dg_ref.shape, F32)

        loss_ref[...] += 0.5 * jnp.sum(jnp.mean(err * err, axis=-1, keepdims=True), axis=0, keepdims=True)
        dfn_ref[...] += jnp.sum(dy * nv, axis=0, keepdims=True)
        dn = dy * fn
        dhv = rinv * (dn - nv * jnp.mean(dn * nv, axis=-1, keepdims=True))
        dh_ref[...] = dhv
        df_ref[...] = (gate_scale * g_ref[...] * dhv).astype(df_ref.dtype)
        dg_ref[...] += jnp.sum(gate_scale * f_ref[...].astype(F32) * dhv, axis=0, keepdims=True)

    row = pl.BlockSpec((tr, d), lambda i: (i, 0))
    one = pl.BlockSpec((1, d), lambda i: (0, 0))
    return pl.pallas_call(
        body, name=name, grid=(r // tr,),
        in_specs=[row, one, row, row, pl.BlockSpec((None, 1, d), lambda i: (1, 0, 0))],
        out_specs=[row, one, pl.BlockSpec((SUBLANES, LANES), lambda i: (0, 0)), row, one],
        out_shape=[jax.ShapeDtypeStruct((r, d), F32), jax.ShapeDtypeStruct((1, d), F32),
                   jax.ShapeDtypeStruct((SUBLANES, LANES), F32), jax.ShapeDtypeStruct((r, d), BF16),
                   jax.ShapeDtypeStruct((1, d), F32)],
        compiler_params=_cp(("arbitrary",)))(h, final_norm, target, f, gate)


def _swap_pairs(t):
    lane = lax.broadcasted_iota(jnp.int32, t.shape, 1)
    nxt = pltpu.roll(t, HEAD_DIM - 1, 1)
    prv = pltpu.roll(t, 1, 1)
    return jnp.where(lane % 2 == 0, nxt, prv)


def _qk_prep(name, p, cosf, sinf, q_gain, k_gain, aw, kw):
    t = p.shape[0]
    tr = _tile(t, 256, SUBLANES)
    q_prescale = HEAD_DIM ** -0.5 * LOG2E

    def head_fwd(v, gain, cs, sn):
        v = v.astype(F32)
        rinv = lax.rsqrt(jnp.mean(v * v, axis=-1, keepdims=True) + NORM_EPS)
        tt = v * rinv * gain
        return tt * cs + _swap_pairs(tt) * sn

    def body(q_ref, k_ref, cos_ref, sin_ref, qg_ref, kg_ref, qo_ref, ko_ref):
        cs, sn = cos_ref[...], sin_ref[...]
        for hh in range(aw // HEAD_DIM):
            sl = slice(hh * HEAD_DIM, (hh + 1) * HEAD_DIM)
            qo_ref[:, sl] = (head_fwd(q_ref[:, sl], qg_ref[...], cs, sn) * q_prescale).astype(qo_ref.dtype)
        for hh in range(kw // HEAD_DIM):
            sl = slice(hh * HEAD_DIM, (hh + 1) * HEAD_DIM)
            ko_ref[:, sl] = head_fwd(k_ref[:, sl], kg_ref[...], cs, sn).astype(ko_ref.dtype)

    assert aw % kw == 0
    row = lambda i: (i, 0)
    return pl.pallas_call(
        body, name=name, grid=(t // tr,),
        in_specs=[pl.BlockSpec((tr, aw), row), pl.BlockSpec((tr, kw), lambda i: (i, aw // kw)),
                  pl.BlockSpec((tr, HEAD_DIM), row), pl.BlockSpec((tr, HEAD_DIM), row),
                  pl.BlockSpec((1, HEAD_DIM), lambda i: (0, 0)), pl.BlockSpec((1, HEAD_DIM), lambda i: (0, 0))],
        out_specs=[pl.BlockSpec((tr, aw), row), pl.BlockSpec((tr, kw), row)],
        out_shape=[jax.ShapeDtypeStruct((t, aw), BF16), jax.ShapeDtypeStruct((t, kw), BF16)],
        compiler_params=_cp(("parallel",)))(p, p, cosf, sinf, q_gain, k_gain)


def _qk_prep_bwd(name, dq_rot, dk_rot, p, cosf, sinf, q_gain, k_gain, aw, kw):
    t = p.shape[0]
    lq = dq_rot.shape[0]
    tr = _tile(math.gcd(t, lq), 256, SUBLANES)
    nq = lq // tr

    def head_bwd(dout, v, gain, cs, sn):
        v = v.astype(F32)
        rinv = lax.rsqrt(jnp.mean(v * v, axis=-1, keepdims=True) + NORM_EPS)
        vn = v * rinv
        dt = dout * cs - _swap_pairs(dout) * sn
        dvn = dt * gain
        dv = rinv * (dvn - vn * jnp.mean(dvn * vn, axis=-1, keepdims=True))
        return dv, jnp.sum(dt * vn, axis=0, keepdims=True)

    def body(dq_ref, dk_ref, q_ref, k_ref, cos_ref, sin_ref, qg_ref, kg_ref, dqo_ref, dko_ref, dqg_ref, dkg_ref):
        i = pl.program_id(0)
        cs, sn = cos_ref[...], sin_ref[...]

        @pl.when(i == 0)
        def _():
            dqg_ref[...] = jnp.zeros(dqg_ref.shape, F32)
            dkg_ref[...] = jnp.zeros(dkg_ref.shape, F32)

        has_q = i < nq
        for hh in range(aw // HEAD_DIM):
            sl = slice(hh * HEAD_DIM, (hh + 1) * HEAD_DIM)
            dout = jnp.where(has_q, dq_ref[:, sl], 0.0)
            dv, dg = head_bwd(dout, q_ref[:, sl], qg_ref[...], cs, sn)
            dqo_ref[:, sl] = dv.astype(dqo_ref.dtype)
            dqg_ref[...] += dg
        for hh in range(kw // HEAD_DIM):
            sl = slice(hh * HEAD_DIM, (hh + 1) * HEAD_DIM)
            dv, dg = head_bwd(dk_ref[:, sl], k_ref[:, sl], kg_ref[...], cs, sn)
            dko_ref[:, sl] = dv.astype(dko_ref.dtype)
            dkg_ref[...] += dg

    row = lambda i: (i, 0)
    one = lambda i: (0, 0)
    return pl.pallas_call(
        body, name=name, grid=(t // tr,),
        in_specs=[pl.BlockSpec((tr, aw), lambda i: (jnp.minimum(i, nq - 1), 0)), pl.BlockSpec((tr, kw), row),
                  pl.BlockSpec((tr, aw), row), pl.BlockSpec((tr, kw), lambda i: (i, aw // kw)),
                  pl.BlockSpec((tr, HEAD_DIM), row), pl.BlockSpec((tr, HEAD_DIM), row),
                  pl.BlockSpec((1, HEAD_DIM), one), pl.BlockSpec((1, HEAD_DIM), one)],
        out_specs=[pl.BlockSpec((tr, aw), row), pl.BlockSpec((tr, kw), row),
                   pl.BlockSpec((1, HEAD_DIM), one), pl.BlockSpec((1, HEAD_DIM), one)],
        out_shape=[jax.ShapeDtypeStruct((t, aw), BF16), jax.ShapeDtypeStruct((t, kw), BF16),
                   jax.ShapeDtypeStruct((1, HEAD_DIM), F32), jax.ShapeDtypeStruct((1, HEAD_DIM), F32)],
        compiler_params=_cp(("arbitrary",)))(dq_rot, dk_rot, p, p, cosf, sinf, q_gain, k_gain)


def _attn_tiles(lq, t):
    return _tile(lq, 1024), _tile(t, 768)


def _attn_fwd(name, q, k, p, v_off, lq, groups):
    t = k.shape[0]
    hq = q.shape[1] // HEAD_DIM
    hkv = hq // groups
    gw = groups * HEAD_DIM
    tq, tk = _attn_tiles(lq, t)
    nkv = t // tk
    vb = v_off // HEAD_DIM

    def body(q_ref, k_ref, v_ref, o_ref, lse_ref, m_sc, l_sc, acc_sc, s_sc):
        j = pl.program_id(2)

        @pl.when(j == 0)
        def _():
            m_sc[...] = jnp.full(m_sc.shape, -jnp.inf, F32)
            l_sc[...] = jnp.zeros(l_sc.shape, F32)
            acc_sc[...] = jnp.zeros(acc_sc.shape, F32)
            s_sc[...] = jnp.zeros(s_sc.shape, F32)

        has_prev = j > 0
        kv, vv = k_ref[...], v_ref[...]
        for g in range(groups):
            sl = slice(g * HEAD_DIM, (g + 1) * HEAD_DIM)
            s = s_sc[g]
            s_sc[g] = _dot(q_ref[:, sl], kv, NT)
            m_prev = m_sc[g]
            m_new = jnp.where(has_prev, jnp.maximum(m_prev, jnp.max(s, axis=-1, keepdims=True)), jnp.inf)
            alpha = jnp.exp2(m_prev - m_new)
            pexp = jnp.exp2(s - m_new)
            part = pexp[:, 0:LANES]
            for cb in range(1, tk // LANES):
                part = part + pexp[:, cb * LANES:(cb + 1) * LANES]
            l_sc[g] = alpha * l_sc[g] + part
            acc_sc[:, sl] = alpha * acc_sc[:, sl] + _dot(pexp.astype(BF16), vv, NN)
            m_sc[g] = jnp.where(has_prev, m_new, -jnp.inf)

        @pl.when(j == nkv)
        def _():
            for g in range(groups):
                sl = slice(g * HEAD_DIM, (g + 1) * HEAD_DIM)
                l_row = jnp.sum(l_sc[g], axis=-1, keepdims=True)
                o_ref[:, sl] = (acc_sc[:, sl] * (1.0 / l_row)).astype(o_ref.dtype)
                lse_ref[g] = jnp.broadcast_to(m_sc[g] + jnp.log2(l_row), (tq, LANES))

    return pl.pallas_call(
        body, name=name, grid=(hkv, lq // tq, nkv + 1),
        in_specs=[pl.BlockSpec((tq, gw), lambda h, i, j: (i, h)),
                  pl.BlockSpec((tk, HEAD_DIM), lambda h, i, j: (jnp.minimum(j, nkv - 1), h)),
                  pl.BlockSpec((tk, HEAD_DIM), lambda h, i, j: (jnp.maximum(j - 1, 0), vb + h))],
        out_specs=[pl.BlockSpec((tq, gw), lambda h, i, j: (i, h)),
                   pl.BlockSpec((groups, tq, LANES), lambda h, i, j: (h, i, 0))],
        out_shape=[jax.ShapeDtypeStruct((lq, hq * HEAD_DIM), BF16), jax.ShapeDtypeStruct((hq, lq, LANES), F32)],
        scratch_shapes=[pltpu.VMEM((groups, tq, 1), F32), pltpu.VMEM((groups, tq, LANES), F32), pltpu.VMEM((tq, gw), F32),
                        pltpu.VMEM((groups, tq, tk), F32)],
        compiler_params=_cp(("parallel", "parallel", "arbitrary")))(q, k, p)


def _attn_delta(name, o, do):
    lq, aw = o.shape
    hq = aw // HEAD_DIM
    tr = _tile(lq, 512, SUBLANES)

    def body(o_ref, do_ref, d_ref):
        for h in range(hq):
            sl = slice(h * HEAD_DIM, (h + 1) * HEAD_DIM)
            dsum = jnp.sum(do_ref[:, sl].astype(F32) * o_ref[:, sl].astype(F32), axis=-1, keepdims=True)
            d_ref[h] = jnp.broadcast_to(dsum, (tr, LANES))

    spec = pl.BlockSpec((tr, aw), lambda i: (i, 0))
    return pl.pallas_call(
        body, name=name, grid=(lq // tr,), in_specs=[spec, spec],
        out_specs=pl.BlockSpec((hq, tr, LANES), lambda i: (0, i, 0)),
        out_shape=jax.ShapeDtypeStruct((hq, lq, LANES), F32), compiler_params=_cp(("parallel",)))(o, do)


def _attn_bwd(name, q, k, p, v_off, do, lse, delta, lq, groups):
    t = k.shape[0]
    hkv = k.shape[1] // HEAD_DIM
    gw = groups * HEAD_DIM
    tq, tk = _attn_tiles(lq, t)
    nq, nkv = lq // tq, t // tk
    scale = HEAD_DIM ** -0.5
    vb = v_off // HEAD_DIM

    def body(q_ref, k_ref, v_ref, do_ref, lse_ref, delta_ref, dq_ref, dk_ref, dv_ref, dq_sc, dk_sc, dv_sc):
        j, i = pl.program_id(1), pl.program_id(2)

        @pl.when(i == 0)
        def _():
            dk_sc[...] = jnp.zeros(dk_sc.shape, F32)
            dv_sc[...] = jnp.zeros(dv_sc.shape, F32)

        @pl.when(j == 0)
        def _():
            dq_sc[i] = jnp.zeros((tq, gw), F32)

        kv, vv = k_ref[...], v_ref[...]
        dk_part, dv_part = None, None
        for g in range(groups):
            sl = slice(g * HEAD_DIM, (g + 1) * HEAD_DIM)
            qv, dov = q_ref[:, sl], do_ref[:, sl]
            s = _dot(qv, kv, NT)
            pexp = jnp.exp2(s - lse_ref[g, :, 0:1])
            dv_g = _dot(pexp.astype(BF16), dov, TN)
            dp = _dot(dov, vv, NT)
            ds = (pexp * (dp - delta_ref[g, :, 0:1])).astype(BF16)
            dk_g = _dot(ds, qv, TN)
            dq_sc[i, :, sl] += _dot(ds, kv, NN)
            dk_part = dk_g if dk_part is None else dk_part + dk_g
            dv_part = dv_g if dv_part is None else dv_part + dv_g
        dk_sc[...] += dk_part
        dv_sc[...] += dv_part

        @pl.when(i == nq - 1)
        def _():
            dk_ref[...] = dk_sc[...] * LN2
            dv_ref[...] = dv_sc[...].astype(dv_ref.dtype)

        @pl.when(j == nkv - 1)
        def _():
            dq_ref[...] = dq_sc[i] * scale

    qspec = pl.BlockSpec((tq, gw), lambda kh, j, i: (i, kh))
    kspec = pl.BlockSpec((tk, HEAD_DIM), lambda kh, j, i: (j, kh))
    rowspec = pl.BlockSpec((groups, tq, LANES), lambda kh, j, i: (kh, i, 0))
    dqspec = pl.BlockSpec((tq, gw), lambda kh, j, i: (jnp.where(j == nkv - 1, i, 0), kh))
    return pl.pallas_call(
        body, name=name, grid=(hkv, nkv, nq),
        in_specs=[qspec, kspec, pl.BlockSpec((tk, HEAD_DIM), lambda kh, j, i: (j, vb + kh)), qspec, rowspec, rowspec],
        out_specs=[dqspec, kspec, kspec],
        out_shape=[jax.ShapeDtypeStruct((lq, hkv * gw), F32), jax.ShapeDtypeStruct((t, hkv * HEAD_DIM), F32),
                   jax.ShapeDtypeStruct((t, hkv * HEAD_DIM), BF16)],
        scratch_shapes=[pltpu.VMEM((nq, tq, gw), F32), pltpu.VMEM((tk, HEAD_DIM), F32), pltpu.VMEM((tk, HEAD_DIM), F32)],
        compiler_params=_cp(("parallel", "arbitrary", "arbitrary")))(q, k, p, do, lse, delta)


def _ret_tables(log_gamma, direction):
    c = RET_CHUNK
    idx = jnp.arange(c, dtype=F32)
    diff = idx[:, None] - idx[None, :]
    if direction == 1:
        diff = -diff
    keep = diff >= 0
    lg = log_gamma.astype(F32)
    mask = jnp.where(keep[None], jnp.exp(jnp.where(keep, diff, 0.0)[None] * lg[:, None, None]), 0.0)
    q_exp = idx + 1.0 if direction == 0 else c - idx
    k_exp = c - 1.0 - idx if direction == 0 else idx
    sign = 1.0 if direction == 0 else -1.0
    lane = lambda v: jnp.broadcast_to(v[..., None], v.shape + (LANES,))
    qdec = lane(jnp.exp(q_exp[None, :] * lg[:, None]))
    kdec = lane(jnp.exp(k_exp[None, :] * lg[:, None]))
    cdec = jnp.broadcast_to(jnp.exp(c * lg)[:, None, None], (lg.shape[0], SUBLANES, LANES))
    weights = lane(jnp.stack([sign * idx, q_exp, -sign * idx, k_exp], axis=0))
    return mask, qdec, kdec, cdec, weights


def _ret_chunk_of(direction, step, nx, nc):
    if direction == 0:
        return jnp.where(step < nc, nx + step, step - nc)
    return jnp.where(step < nc, nx + nc - 1 - step, nx + nc - 1 - step)


def _ret_fwd(name, p, offs, tables, direction, lx, prev):
    q_off, k_off, v_off, rw = offs
    t = p.shape[0]
    c = RET_CHUNK
    n_steps, nx = t // c, lx // c
    nc = n_steps - nx
    bw = math.gcd(math.gcd(q_off, k_off), math.gcd(v_off, rw))
    bw = _tile(bw, 512)
    hpb = bw // HEAD_DIM
    heads = rw // HEAD_DIM
    k_scale = HEAD_DIM ** -0.5
    mask, qdec, kdec, cdec, _ = tables
    rc = lambda n: _ret_chunk_of(direction, n, nx, nc)

    ng = heads // hpb

    def body(*refs):
        q_refs, k_refs, v_refs = refs[0:ng], refs[ng:2 * ng], refs[2 * ng:3 * ng]
        if prev is None:
            m_ref, qd_ref, kd_ref, cd_ref, y_ref, st_ref, s_sc = refs[3 * ng:]
        else:
            m_ref, qd_ref, kd_ref, cd_ref, prev_ref, y_ref, st_ref, s_sc = refs[3 * ng:]
        n = pl.program_id(0)

        @pl.when(n == 0)
        def _():
            s_sc[...] = jnp.zeros(s_sc.shape, F32)

        for hd in range(heads):
            gg, hh = divmod(hd, hpb)
            sl = slice(hh * HEAD_DIM, (hh + 1) * HEAD_DIM)
            osl = slice(hd * HEAD_DIM, (hd + 1) * HEAD_DIM)
            qv = q_refs[gg][:, sl]
            kf = k_refs[gg][:, sl].astype(F32) * k_scale
            kv = kf.astype(BF16)
            vv = v_refs[gg][:, sl]
            state = s_sc[hd]
            st_ref[hd] = state
            a = _dot(qv, kv, NT) * m_ref[hd]
            y = _dot(a.astype(BF16), vv, NN) + _dot(qv, state.astype(BF16), NN) * qd_ref[hd]
            s_sc[hd] = state * cd_ref[hd, 0:1, :] + _dot((kf * kd_ref[hd]).astype(BF16), vv, TN)
            if prev is not None:
                y = y + prev_ref[:, osl]
            y_ref[:, osl] = y

    col = lambda off, gg: (lambda n: (rc(n), off // bw + gg))
    tab3 = lambda n: (0, 0, 0)
    in_specs = [pl.BlockSpec((c, bw), col(off, gg)) for off in (q_off, k_off, v_off) for gg in range(ng)]
    in_specs += [pl.BlockSpec((heads, c, c), tab3), pl.BlockSpec((heads, c, LANES), tab3),
                 pl.BlockSpec((heads, c, LANES), tab3), pl.BlockSpec((heads, SUBLANES, LANES), tab3)]
    args = [p] * (3 * ng) + [mask, qdec, kdec, cdec]
    aliases = {}
    yspec = pl.BlockSpec((c, rw), lambda n: (rc(n), 0))
    if prev is not None:
        in_specs.append(yspec)
        args.append(prev)
        aliases = {len(args) - 1: 0}
    return pl.pallas_call(
        body, name=name, grid=(n_steps,), in_specs=in_specs,
        out_specs=[yspec, pl.BlockSpec((None, heads, HEAD_DIM, HEAD_DIM), lambda n: (n, 0, 0, 0))],
        out_shape=[jax.ShapeDtypeStruct((t, rw), F32), jax.ShapeDtypeStruct((n_steps, heads, HEAD_DIM, HEAD_DIM), F32)],
        scratch_shapes=[pltpu.VMEM((heads, HEAD_DIM, HEAD_DIM), F32)], input_output_aliases=aliases,
        compiler_params=_cp(("arbitrary",)))(*args)


def _ret_bwd(name, p, offs, tables, states, dy, direction, lx, prev):
    q_off, k_off, v_off, rw = offs
    t = p.shape[0]
    c = RET_CHUNK
    n_steps, nx = t // c, lx // c
    nc = n_steps - nx
    bw = math.gcd(math.gcd(q_off, k_off), math.gcd(v_off, rw))
    bw = _tile(bw, 512)
    hpb = bw // HEAD_DIM
    heads = rw // HEAD_DIM
    k_scale = HEAD_DIM ** -0.5
    mask, qdec, kdec, cdec, weights = tables
    step_of = lambda n: n_steps - 1 - n
    rc = lambda n: _ret_chunk_of(direction, step_of(n), nx, nc)

    ng = heads // hpb

    def body(*refs):
        q_refs, k_refs, v_refs = refs[0:ng], refs[ng:2 * ng], refs[2 * ng:3 * ng]
        if prev is None:
            (dy_ref, st_ref, m_ref, qd_ref, kd_ref, cd_ref, w_ref,
             dq_ref, dk_ref, dv_ref, dl_ref, ds_sc, lam_sc) = refs[3 * ng:]
        else:
            (dy_ref, st_ref, m_ref, qd_ref, kd_ref, cd_ref, w_ref, pq_ref, pk_ref, pv_ref,
             dq_ref, dk_ref, dv_ref, dl_ref, ds_sc, lam_sc) = refs[3 * ng:]
        n = pl.program_id(0)

        @pl.when(n == 0)
        def _():
            ds_sc[...] = jnp.zeros(ds_sc.shape, F32)
            lam_sc[...] = jnp.zeros(lam_sc.shape, F32)

        is_x = rc(n) < nx
        for hd in range(heads):
            gg, hh = divmod(hd, hpb)
            sl = slice(hh * HEAD_DIM, (hh + 1) * HEAD_DIM)
            osl = slice(hd * HEAD_DIM, (hd + 1) * HEAD_DIM)
            qv = q_refs[gg][:, sl]
            qf = qv.astype(F32)
            kf = k_refs[gg][:, sl].astype(F32) * k_scale
            kv = kf.astype(BF16)
            vv = v_refs[gg][:, sl]
            dyv = jnp.where(is_x, dy_ref[:, osl], 0.0).astype(BF16)
            state = st_ref[hd]
            dstate = ds_sc[hd]
            dstate_b = dstate.astype(BF16)
            msk, qd, kd = m_ref[hd], qd_ref[hd], kd_ref[hd]
            cd = cd_ref[hd, 0:1, :]
            a = _dot(qv, kv, NT) * msk
            da = (_dot(dyv, vv, NT) * msk).astype(BF16)
            dq_intra = _dot(da, kv, NN)
            dk_intra = _dot(da, qv, TN)
            dq_inter = _dot(dyv, state.astype(BF16), NT) * qd
            dk_inter = _dot(vv, dstate_b, NT) * kd
            dv = _dot(a.astype(BF16), dyv, TN) + _dot((kf * kd).astype(BF16), dstate_b, NN)
            lam_sc[hd] += (qf * (w_ref[0] * dq_intra + w_ref[1] * dq_inter)
                           + kf * (w_ref[2] * dk_intra + w_ref[3] * dk_inter)
                           + (c * cd) * state * dstate)
            ds_sc[hd] = _dot((qf * qd).astype(BF16), dyv, TN) + cd * dstate
            dq = dq_intra + dq_inter
            dk = (dk_intra + dk_inter) * k_scale
            if prev is not None:
                dq = dq + pq_ref[:, osl]
                dk = dk + pk_ref[:, osl]
                dv = dv + pv_ref[:, osl]
            dq_ref[:, osl] = dq
            dk_ref[:, osl] = dk
            dv_ref[:, osl] = dv

        @pl.when(n == n_steps - 1)
        def _():
            for hd in range(heads):
                dl_ref[hd] = jnp.broadcast_to(jnp.sum(lam_sc[hd]), (SUBLANES, LANES))

    col = lambda off, gg: (lambda n: (rc(n), off // bw + gg))
    tab3 = lambda n: (0, 0, 0)
    ospec = pl.BlockSpec((c, rw), lambda n: (rc(n), 0))
    in_specs = [pl.BlockSpec((c, bw), col(off, gg)) for off in (q_off, k_off, v_off) for gg in range(ng)]
    in_specs += [pl.BlockSpec((c, rw), lambda n: (jnp.minimum(rc(n), nx - 1), 0)),
                 pl.BlockSpec((None, heads, HEAD_DIM, HEAD_DIM), lambda n: (step_of(n), 0, 0, 0)),
                 pl.BlockSpec((heads, c, c), tab3), pl.BlockSpec((heads, c, LANES), tab3), pl.BlockSpec((heads, c, LANES), tab3),
                 pl.BlockSpec((heads, SUBLANES, LANES), tab3), pl.BlockSpec((4, c, LANES), tab3)]
    args = [p] * (3 * ng) + [dy, states, mask, qdec, kdec, cdec, weights]
    aliases = {}
    if prev is not None:
        for k_out, arr in enumerate(prev):
            in_specs.append(ospec)
            args.append(arr)
            aliases[len(args) - 1] = k_out
    big = jax.ShapeDtypeStruct((t, rw), F32)
    return pl.pallas_call(
        body, name=name, grid=(n_steps,), in_specs=in_specs,
        out_specs=[ospec, ospec, ospec, pl.BlockSpec((heads, SUBLANES, LANES), tab3)],
        out_shape=[big, big, big, jax.ShapeDtypeStruct((heads, SUBLANES, LANES), F32)],
        scratch_shapes=[pltpu.VMEM((heads, HEAD_DIM, HEAD_DIM), F32), pltpu.VMEM((heads, HEAD_DIM, HEAD_DIM), F32)],
        input_output_aliases=aliases, compiler_params=_cp(("arbitrary",)))(*args)


def _ret_out(name, y, p, g_off, lx):
    rw = y.shape[1]
    bw = _tile(math.gcd(g_off, rw), 512)
    tr = _tile(lx, 512, SUBLANES)

    def body(y_ref, g_ref, o_ref):
        for hh in range(bw // HEAD_DIM):
            sl = slice(hh * HEAD_DIM, (hh + 1) * HEAD_DIM)
            yv = y_ref[:, sl]
            gv = g_ref[:, sl].astype(F32)
            rinv = lax.rsqrt(jnp.mean(yv * yv, axis=-1, keepdims=True) + NORM_EPS)
            o_ref[:, sl] = (gv * _sigmoid(gv) * yv * rinv).astype(o_ref.dtype)

    spec = pl.BlockSpec((tr, bw), lambda i, g: (i, g))
    return pl.pallas_call(
        body, name=name, grid=(lx // tr, rw // bw),
        in_specs=[spec, pl.BlockSpec((tr, bw), lambda i, g: (i, g_off // bw + g))],
        out_specs=spec, out_shape=jax.ShapeDtypeStruct((lx, rw), BF16),
        compiler_params=_cp(("parallel", "parallel")))(y, p)


def _ret_out_bwd(name, dyr, y, p, g_off, lx):
    rw = y.shape[1]
    bw = _tile(math.gcd(g_off, rw), 512)
    tr = _tile(lx, 512, SUBLANES)

    def body(d_ref, y_ref, g_ref, dy_ref, dg_ref):
        for hh in range(bw // HEAD_DIM):
            sl = slice(hh * HEAD_DIM, (hh + 1) * HEAD_DIM)
            yv = y_ref[:, sl]
            gv = g_ref[:, sl].astype(F32)
            dv = d_ref[:, sl]
            rinv = lax.rsqrt(jnp.mean(yv * yv, axis=-1, keepdims=True) + NORM_EPS)
            yn = yv * rinv
            sg = _sigmoid(gv)
            dg_ref[:, sl] = (dv * yn * sg * (1.0 + gv * (1.0 - sg))).astype(dg_ref.dtype)
            dyn = dv * gv * sg
            dy_ref[:, sl] = rinv * (dyn - yn * jnp.mean(dyn * yn, axis=-1, keepdims=True))

    spec = pl.BlockSpec((tr, bw), lambda i, g: (i, g))
    return pl.pallas_call(
        body, name=name, grid=(lx // tr, rw // bw),
        in_specs=[spec, spec, pl.BlockSpec((tr, bw), lambda i, g: (i, g_off // bw + g))],
        out_specs=[spec, spec],
        out_shape=[jax.ShapeDtypeStruct((lx, rw), F32), jax.ShapeDtypeStruct((lx, rw), BF16)],
        compiler_params=_cp(("parallel", "parallel")))(dyr, y, p)


def _merge(name, pa, pr, p, ga_off, gb_off):
    r, d = pa.shape
    cw = _tile(math.gcd(math.gcd(ga_off, gb_off), d), 1024)
    tr = _tile(r, 512, SUBLANES)

    def body(pa_ref, pr_ref, ga_ref, gb_ref, o_ref):
        o_ref[...] = (_sigmoid(ga_ref[...].astype(F32)) * pa_ref[...].astype(F32)
                      + _sigmoid(gb_ref[...].astype(F32)) * pr_ref[...].astype(F32)).astype(o_ref.dtype)

    spec = pl.BlockSpec((tr, cw), lambda i, j: (i, j))
    return pl.pallas_call(
        body, name=name, grid=(r // tr, d // cw),
        in_specs=[spec, spec, pl.BlockSpec((tr, cw), lambda i, j: (i, ga_off // cw + j)),
                  pl.BlockSpec((tr, cw), lambda i, j: (i, gb_off // cw + j))],
        out_specs=spec, out_shape=jax.ShapeDtypeStruct((r, d), BF16),
        compiler_params=_cp(("parallel", "parallel")))(pa, pr, p, p)


def _ada_fwd(name, cond, w):
    rows, d = cond.shape
    n = w.shape[1]
    tn = _tile(n, 768)

    def body(c_ref, w_ref, o_ref, s_ref):
        cv = c_ref[...]
        sv = cv * _sigmoid(cv)
        s_ref[...] = sv
        o_ref[...] = _dot(sv.astype(BF16), w_ref[...].astype(BF16), NN)

    return pl.pallas_call(
        body, name=name, grid=(n // tn,),
        in_specs=[pl.BlockSpec((rows, d), lambda j: (0, 0)), pl.BlockSpec((d, tn), lambda j: (0, j))],
        out_specs=[pl.BlockSpec((rows, tn), lambda j: (0, j)), pl.BlockSpec((rows, d), lambda j: (0, 0))],
        out_shape=[jax.ShapeDtypeStruct((rows, n), F32), jax.ShapeDtypeStruct((rows, d), F32)],
        compiler_params=_cp(("arbitrary",)))(cond, w)


def _ada_bwd(name, s_cond, dmod, w):
    rows, d = s_cond.shape
    n = w.shape[1]
    tn = _tile(n, 768)

    def body(s_ref, dm_ref, w_ref, gw_ref, ds_ref):
        j = pl.program_id(0)

        @pl.when(j == 0)
        def _():
            ds_ref[...] = jnp.zeros(ds_ref.shape, F32)

        dmv = dm_ref[...].astype(BF16)
        gw_ref[...] = _dot(s_ref[...].astype(BF16), dmv, TN)
        ds_ref[...] += _dot(dmv, w_ref[...].astype(BF16), NT)

    return pl.pallas_call(
        body, name=name, grid=(n // tn,),
        in_specs=[pl.BlockSpec((rows, d), lambda j: (0, 0)), pl.BlockSpec((rows, tn), lambda j: (0, j)),
                  pl.BlockSpec((d, tn), lambda j: (0, j))],
        out_specs=[pl.BlockSpec((d, tn), lambda j: (0, j)), pl.BlockSpec((rows, d), lambda j: (0, 0))],
        out_shape=[jax.ShapeDtypeStruct((d, n), F32), jax.ShapeDtypeStruct((rows, d), F32)],
        compiler_params=_cp(("arbitrary",)))(s_cond, dmod, w)


def _sum_slots(name, a):
    s, r, c = a.shape

    def body(a_ref, o_ref):
        acc = a_ref[0]
        for k in range(1, s):
            acc = acc + a_ref[k]
        o_ref[...] = acc

    return pl.pallas_call(
        body, name=name, grid=(1,), in_specs=[pl.BlockSpec((s, r, c), lambda i: (0, 0, 0))],
        out_specs=pl.BlockSpec((r, c), lambda i: (0, 0)), out_shape=jax.ShapeDtypeStruct((r, c), F32),
        compiler_params=_cp(("arbitrary",)))(a)


def _cctx_grad(name, parts, c_ctx):
    s, r, d = parts.shape

    def body(p_ref, c_ref, o_ref):
        acc = p_ref[0]
        for k in range(1, s):
            acc = acc + p_ref[k]
        cv = c_ref[...]
        sg = _sigmoid(cv)
        o_ref[...] = acc[0:1, :] * sg * (1.0 + cv * (1.0 - sg))

    return pl.pallas_call(
        body, name=name, grid=(1,),
        in_specs=[pl.BlockSpec((s, r, d), lambda i: (0, 0, 0)), pl.BlockSpec((1, d), lambda i: (0, 0))],
        out_specs=pl.BlockSpec((1, d), lambda i: (0, 0)), out_shape=jax.ShapeDtypeStruct((1, d), F32),
        compiler_params=_cp(("arbitrary",)))(parts, c_ctx)


def _adamw(name, slots, w, m, v):
    s, r, c = slots.shape
    tr = _tile(r, 256 if c > 1024 else 512, SUBLANES)
    c1 = 1.0 - ADAM_B1 ** ADAM_STEP
    c2 = 1.0 - ADAM_B2 ** ADAM_STEP

    def body(s_ref, w_ref, m_ref, v_ref, g_ref, d_ref, mo_ref, vo_ref):
        g = s_ref[0].astype(F32)
        for k in range(1, s):
            g = g + s_ref[k].astype(F32)
        mn = ADAM_B1 * m_ref[...] + (1.0 - ADAM_B1) * g
        vn = ADAM_B2 * v_ref[...] + (1.0 - ADAM_B2) * (g * g)
        m_hat = mn / c1
        v_hat = vn / c2
        g_ref[...] = g
        mo_ref[...] = mn
        vo_ref[...] = vn
        d_ref[...] = -ADAM_LR * (m_hat / (jnp.sqrt(v_hat) + ADAM_EPS) + ADAM_WD * w_ref[...])

    spec = pl.BlockSpec((tr, c), lambda i: (i, 0))
    shp = jax.ShapeDtypeStruct((r, c), F32)
    return pl.pallas_call(
        body, name=name, grid=(r // tr,),
        in_specs=[pl.BlockSpec((s, tr, c), lambda i: (0, i, 0)), spec, spec, spec],
        out_specs=[spec] * 4, out_shape=[shp] * 4, compiler_params=_cp(("parallel",)))(slots, w, m, v)


def _coords():
    return lax.axis_index("x"), lax.axis_index("y"), lax.axis_index("c")


def _flip(coord, bit):
    return 1 - coord if bit else coord


def _all_gather_small(name, blk):
    r, ccols = blk.shape

    def body(x_ref, out_ref, send_sems, recv_sems, local_sem):
        x, y, c = _coords()
        me, sibling = (x, y, c), (x, y, 1 - c)
        chips = [(1 - x, y), (x, 1 - y), (1 - x, 1 - y)]

        def slot(px, py, pc):
            return out_ref.at[4 * px + 2 * py + pc]

        def copy(k, block, to, src=None):
            return pltpu.make_async_remote_copy(
                src_ref=slot(*block) if src is None else src, dst_ref=slot(*block),
                send_sem=send_sems.at[k], recv_sem=recv_sems.at[k], device_id=to, device_id_type=MESH)

        mine = pltpu.make_async_copy(x_ref, slot(*me), local_sem)
        mine.start()
        first = [copy(0, me, sibling, src=x_ref)]
        first += [copy(1 + j, me, (*chip, c), src=x_ref) for j, chip in enumerate(chips)]
        for cp in first:
            cp.start()
        passed = [copy(4 + j, (*chip, c), sibling) for j, chip in enumerate(chips)]
        for j, chip in enumerate(chips):
            copy(1 + j, (*chip, c), me).wait_recv()
            passed[j].start()
        copy(0, sibling, me).wait_recv()
        for j, chip in enumerate(chips):
            copy(4 + j, (*chip, 1 - c), me).wait_recv()
        for cp in first + passed:
            cp.wait_send()
        mine.wait()

    return pl.pallas_call(
        body, name=name, out_shape=jax.ShapeDtypeStruct((N_DEV, r, ccols), blk.dtype),
        in_specs=[pl.BlockSpec(memory_space=pltpu.VMEM)], out_specs=pl.BlockSpec(memory_space=pltpu.VMEM),
        scratch_shapes=[pltpu.SemaphoreType.DMA((7,)), pltpu.SemaphoreType.DMA((7,)), pltpu.SemaphoreType.DMA],
    )(blk)


def _comm_semaphores(n):
    return [pltpu.SemaphoreType.DMA((7 * n,)), pltpu.SemaphoreType.DMA((7 * n,)), pltpu.SemaphoreType.DMA((n,))]


def _comm_out_shape(kind, s, wide):
    if kind == "gather":
        shape = (s.shape[0], N_DEV * s.shape[1]) if wide else (N_DEV,) + s.shape
    else:
        shape = (N_DEV, s.shape[0], s.shape[1] // N_DEV) if wide else s.shape
    return jax.ShapeDtypeStruct(shape, s.dtype)


def _block_of(ref, idx, wide, cols):
    if not wide:
        return ref.at[idx]
    return ref.at[:, pl.ds(pl.multiple_of(idx * cols, LANES), cols)]


def _gather_plan(ins, outs, wide, send_sems, recv_sems, local_sems):
    n = len(ins)
    x, y, c = _coords()
    me, sibling = (x, y, c), (x, y, 1 - c)
    chips = [(1 - x, y), (x, 1 - y), (1 - x, 1 - y)]

    def slot(a, px, py, pc):
        return _block_of(outs[a], 4 * px + 2 * py + pc, wide[a], ins[a].shape[-1])

    def copy(a, k, block, to, src=None):
        return pltpu.make_async_remote_copy(
            src_ref=slot(a, *block) if src is None else src, dst_ref=slot(a, *block),
            send_sem=send_sems.at[7 * a + k], recv_sem=recv_sems.at[7 * a + k], device_id=to, device_id_type=MESH)

    def local(a):
        return pltpu.make_async_copy(ins[a], slot(a, *me), local_sems.at[a])

    def first(a):
        return [copy(a, 0, me, sibling, src=ins[a])] + [copy(a, 1 + j, me, (*chip, c), src=ins[a])
                                                        for j, chip in enumerate(chips)]

    def passed(a, j):
        return copy(a, 4 + j, (*chips[j], c), sibling)

    def start():
        for a in range(n):
            local(a).start()
            for cp in first(a):
                cp.start()

    def pass_on():
        for a in range(n):
            for j, chip in enumerate(chips):
                copy(a, 1 + j, (*chip, c), me).wait_recv()
                passed(a, j).start()

    def finish():
        for a in range(n):
            copy(a, 0, sibling, me).wait_recv()
            for j, chip in enumerate(chips):
                copy(a, 4 + j, (*chip, 1 - c), me).wait_recv()
        for a in range(n):
            for cp in first(a) + [passed(a, j) for j in range(3)]:
                cp.wait_send()
            local(a).wait()

    return start, pass_on, finish


def _exchange_plan(ins, outs, wide, send_sems, recv_sems, local_sems):
    n = len(ins)
    x, y, c = _coords()
    my_idx = 4 * x + 2 * y + c

    def src(a, idx):
        return _block_of(ins[a], idx, wide[a], outs[a].shape[-1])

    def local(a):
        return pltpu.make_async_copy(src(a, my_idx), outs[a].at[my_idx], local_sems.at[a])

    def pair(a, rel):
        px, py, pc = _flip(x, rel & 4), _flip(y, rel & 2), _flip(c, rel & 1)
        peer_idx = 4 * px + 2 * py + pc
        sems = dict(send_sem=send_sems.at[7 * a + rel - 1], recv_sem=recv_sems.at[7 * a + rel - 1],
                    device_id=(px, py, pc), device_id_type=MESH)
        send = pltpu.make_async_remote_copy(src_ref=src(a, peer_idx), dst_ref=outs[a].at[my_idx], **sems)
        recv = pltpu.make_async_remote_copy(src_ref=src(a, my_idx), dst_ref=outs[a].at[peer_idx], **sems)
        return send, recv

    def start():
        for a in range(n):
            local(a).start()
            for rel in range(1, N_DEV):
                pair(a, rel)[0].start()

    def finish():
        for a in range(n):
            for rel in range(1, N_DEV):
                send, recv = pair(a, rel)
                recv.wait_recv()
                send.wait_send()
            local(a).wait()

    return start, None, finish


def _comm_plan(kind, ins, outs, wide, send_sems, recv_sems, local_sems):
    plan = {"gather": _gather_plan, "exchange": _exchange_plan}[kind]
    return plan(ins, outs, wide, send_sems, recv_sems, local_sems)


def _rope_tables(lx, lc):
    rows = lx // GRID_W
    row = jnp.repeat(jnp.arange(rows, dtype=F32), GRID_W)
    col = jnp.tile(jnp.arange(GRID_W, dtype=F32), rows)
    half = HEAD_DIM // 2
    inv_freq = ROPE_THETA ** (-jnp.arange(0, half, 2, dtype=F32) / half)
    ang = jnp.concatenate([row[:, None] * inv_freq, col[:, None] * inv_freq], axis=-1)
    cos, sin = jnp.cos(ang), jnp.sin(ang)
    cosf = jnp.repeat(cos, 2, axis=-1)
    sinf = jnp.stack([-sin, sin], axis=-1).reshape(lx, HEAD_DIM)
    cosf = jnp.concatenate([cosf, jnp.ones((lc, HEAD_DIM), F32)], axis=0)
    sinf = jnp.concatenate([sinf, jnp.zeros((lc, HEAD_DIM), F32)], axis=0)
    return cosf, sinf


def _cols_full(g):
    return jnp.transpose(g, (1, 0, 2)).reshape(g.shape[1], N_DEV * g.shape[2])


def _cols_split(w):
    k, n = w.shape
    return jnp.transpose(w.reshape(k, N_DEV, n // N_DEV), (1, 0, 2))


def _pad_rows(a, rows):
    return jnp.pad(a, ((0, rows - a.shape[0]), (0, 0)))


def kernel(x, c, ctx, c_ctx, w_ada, b_ada, ffn1_w_in, ffn1_w_out, mix_w_in, attn_q_gain, attn_k_gain, ret_decay_logit, w_proj_attn, w_proj_ret, mix_w_out, ffn2_w_in, ffn2_w_out, final_norm, loss_target, m_c_ctx, m_w_ada, m_b_ada, m_ffn1_w_in, m_ffn1_w_out, m_mix_w_in, m_attn_q_gain, m_attn_k_gain, m_ret_decay_logit, m_w_proj_attn, m_w_proj_ret, m_mix_w_out, m_ffn2_w_in, m_ffn2_w_out, m_final_norm, v_c_ctx, v_w_ada, v_b_ada, v_ffn1_w_in, v_ffn1_w_out, v_mix_w_in, v_attn_q_gain, v_attn_k_gain, v_ret_decay_logit, v_w_proj_attn, v_w_proj_ret, v_mix_w_out, v_ffn2_w_in, v_ffn2_w_out, v_final_norm):
    lx, d = x.shape[1], x.shape[2]
    lc = ctx.shape[1]
    t = lx + lc
    aw, rw = w_proj_attn.shape[1], w_proj_ret.shape[1]
    pw = mix_w_in.shape[2] * N_DEV
    kw = (pw - aw - 4 * rw - 2 * d) // 2
    groups = aw // kw
    heads_r = rw // HEAD_DIM
    ka_off, va_off = aw, aw + kw
    qr_off = aw + 2 * kw
    kr_off, vr_off, gr_off = qr_off + rw, qr_off + 2 * rw, qr_off + 3 * rw
    ga_off, gb_off = qr_off + 4 * rw, qr_off + 4 * rw + d
    xi, yi, ci = _coords()
    me = 4 * xi + 2 * yi + ci

    col_names = ["ffn1_w_in", "mix_w_in", "w_proj_attn", "w_proj_ret", "ffn2_w_in"]
    row_names = ["ffn1_w_out", "mix_w_out", "ffn2_w_out"]
    shard = dict(ffn1_w_in=ffn1_w_in[0], mix_w_in=mix_w_in[0], w_proj_attn=w_proj_attn[0], w_proj_ret=w_proj_ret[0],
                 ffn2_w_in=ffn2_w_in[0], ffn1_w_out=ffn1_w_out[0], mix_w_out=mix_w_out[0], ffn2_w_out=ffn2_w_out[0])
    names = col_names + row_names
    bf_shard = {k: shard[k].astype(BF16) for k in names}
    full, landed = {}, {}

    def is_wide(k):
        return k in col_names and shard[k].shape[1] % LANES == 0

    def gather_of(keys):
        return "gather", [bf_shard[k] for k in keys], [is_wide(k) for k in keys]

    def keep_full(keys, gathered):
        for k, g in zip(keys, gathered):
            if is_wide(k):
                full[k] = g
            else:
                full[k] = _cols_full(g) if k in col_names else g.reshape(N_DEV * g.shape[1], g.shape[2])

    def exchange_of(grads):
        blocks = [g if is_wide(k) else _cols_split(g) if k in col_names
                  else g.reshape(N_DEV, g.shape[0] // N_DEV, g.shape[1]) for k, g in grads.items()]
        return "exchange", blocks, [is_wide(k) for k in grads]

    def keep_landed(grads, got):
        landed.update(zip(grads.keys(), got))


    c_all = _all_gather_small("gather_cond", _pad_rows(c, SUBLANES))[:, 0, :]
    cond = _pad_rows(jnp.concatenate([c_all, c_ctx[None, :]], axis=0), 2 * SUBLANES)
    ada_part, s_cond = _ada_fwd("ada_fwd", cond, w_ada[0])
    ada_all = _all_gather_small("gather_ada", ada_part)
    mod_all = jnp.transpose(ada_all, (1, 0, 2)).reshape(2 * SUBLANES, N_MOD * d) + b_ada
    mod_x = lax.dynamic_index_in_dim(mod_all, me, axis=0, keepdims=False).reshape(N_MOD, d)
    mod_c = mod_all[N_DEV].reshape(N_MOD, d)
    mods = jnp.stack([mod_c, mod_x], axis=0)[:, :, None, :]
    sh1, sc1, g1, sh2, sc2, g2, sh3, sc3, g3 = [mods[:, k] for k in range(N_MOD)]

    h0 = jnp.concatenate([x[0], ctx[0]], axis=0)
    u1, got = _rmsmod("ffn1_norm", h0, sc1, sh1, lx, comm=gather_of(["ffn1_w_in"]))
    keep_full(["ffn1_w_in"], got)
    keys = ["ffn1_w_out", "mix_w_in"]
    (z1, s1), got = _ffn_in("ffn1_in", u1, full["ffn1_w_in"], comm=gather_of(keys))
    keep_full(keys, got)
    keys = ["w_proj_attn", "w_proj_ret", "mix_w_out"]
    (h1, f1), got = _mm_residual("ffn1_out", s1, full["ffn1_w_out"], h0, g1, 0.5, lx, comm=gather_of(keys))
    keep_full(keys, got)

    u2 = _rmsmod("mix_norm", h1, sc2, sh2, lx)
    keys = ["ffn2_w_in"]
    p, got = _mm_nn("mix_in", u2, full["mix_w_in"], BF16, tm_pref=M_TILE_BIG, tn_pref=512, comm=gather_of(keys))
    keep_full(keys, got)
    cosf, sinf = _rope_tables(lx, lc)
    q_rot, k_rot = _qk_prep("qk_prep", p, cosf, sinf, attn_q_gain, attn_k_gain, aw, kw)
    ya, lse = _attn_fwd("attn_fwd", q_rot, k_rot, p, va_off, lx, groups)

    decay = ret_decay_logit[0].astype(F32)
    log_gamma = jax.nn.log_sigmoid(decay)
    r_offs = (qr_off, kr_off, vr_off, rw)
    tab = [_ret_tables(log_gamma[k], k) for k in range(2)]
    y_f, st_f = _ret_fwd("ret_fwd_a", p, r_offs, tab[0], 0, lx, None)
    y_r, st_b = _ret_fwd("ret_fwd_b", p, r_offs, tab[1], 1, lx, y_f)
    yr = _ret_out("ret_out", y_r, p, gr_off, lx)

    pa = _mm_nn("proj_attn", ya, full["w_proj_attn"], BF16)
    pr = _mm_nn("proj_ret", yr, full["w_proj_ret"], BF16)
    mg = _merge("merge", pa, pr, p, ga_off, gb_off)
    h2, o2 = _mm_residual("mix_out", mg, full["mix_w_out"], h1, g2, 1.0, lx)

    u3 = _rmsmod("ffn2_norm", h2, sc3, sh3, lx)
    keys = ["ffn2_w_out"]
    (z3, s3), got = _ffn_in("ffn2_in", u3, full["ffn2_w_in"], comm=gather_of(keys))
    keep_full(keys, got)
    h3, f3 = _mm_residual("ffn2_out", s3, full["ffn2_w_out"], h2, g3, 0.5, lx)
    dh3, d_fn, loss_tile, df3, dg3 = _final_loss("final_loss", h3, final_norm[None, :], loss_target[0], f3, g3, 0.5)

    dz3 = _ffn_out_bwd("ffn2_out_bwd", df3, full["ffn2_w_out"], z3)
    grads = {"ffn2_w_out": _mm_tn("ffn2_out_dw", s3, df3, BF16, tm_pref=M_TILE_BIG)}
    g_ffn2_w_in, got = _ffn_in_bwd_w("ffn2_in_dw", u3, dz3, comm=exchange_of(grads))
    keep_landed(grads, got)
    grads = {"ffn2_w_in": g_ffn2_w_in}
    du3, got = _ffn_in_bwd_x("ffn2_in_bwd", dz3, full["ffn2_w_in"], comm=exchange_of(grads))
    keep_landed(grads, got)
    dh2, dsc3, dsh3, do2, dg2 = _rmsmod_bwd("ffn2_norm_bwd", du3, h2, sc3, dh3, lx, lx, gate=(o2, g2, 1.0))

    dpa, dpr, dga, dgb = _merge_bwd("merge_bwd", do2, full["mix_w_out"], pa, pr, p, ga_off, gb_off)
    dya = _mm_nt("proj_attn_bwd", dpa, full["w_proj_attn"], BF16)
    dyr = _mm_nt("proj_ret_bwd", dpr, full["w_proj_ret"], F32)
    mix_grads = {"mix_w_out": _mm_tn("mix_out_dw", mg, do2, BF16),
                 "w_proj_attn": _mm_tn("proj_attn_dw", ya, dpa, BF16),
                 "w_proj_ret": _mm_tn("proj_ret_dw", yr, dpr, BF16)}

    dy_ret, dgr = _ret_out_bwd("ret_out_bwd", dyr, y_r, p, gr_off, lx)
    dqr, dkr, dvr, dl_f = _ret_bwd("ret_bwd_a", p, r_offs, tab[0], st_f, dy_ret, 0, lx, None)
    dqr, dkr, dvr, dl_b = _ret_bwd("ret_bwd_b", p, r_offs, tab[1], st_b, dy_ret, 1, lx, (dqr, dkr, dvr))
    d_lam = jnp.stack([dl_f[:, 0, 0], dl_b[:, 0, 0]], axis=0)
    d_decay = d_lam * jax.nn.sigmoid(-decay)

    delta = _attn_delta("attn_delta", ya, dya)
    dq_rot, dk_rot, dva = _attn_bwd("attn_bwd", q_rot, k_rot, p, va_off, dya, lse, delta, lx, groups)
    dqa, dka, d_qg, d_kg = _qk_prep_bwd("qk_prep_bwd", dq_rot, dk_rot, p, cosf, sinf, attn_q_gain, attn_k_gain, aw, kw)

    dp = jnp.concatenate([dqa, dka, dva, dqr.astype(BF16), dkr.astype(BF16), dvr.astype(BF16),
                          _pad_rows(dgr, t), _pad_rows(dga, t), _pad_rows(dgb, t)], axis=1)
    grads = {"mix_w_in": _mm_tn("mix_in_dw", u2, dp, BF16, tm_pref=M_TILE_DW, tn_pref=512)}
    du2, got = _mm_nt("mix_in_bwd", dp, full["mix_w_in"], F32, comm=exchange_of(grads))
    keep_landed(grads, got)
    dh1, dsc2, dsh2, df1, dg1 = _rmsmod_bwd("mix_norm_bwd", du2, h1, sc2, dh2, lx, t, gate=(f1, g1, 0.5))

    dz1, got = _ffn_out_bwd("ffn1_out_bwd", df1, full["ffn1_w_out"], z1, comm=exchange_of(mix_grads))
    keep_landed(mix_grads, got)
    grads = {"ffn1_w_out": _mm_tn("ffn1_out_dw", s1, df1, BF16, tm_pref=M_TILE_BIG)}
    g_ffn1_w_in, got = _ffn_in_bwd_w("ffn1_in_dw", u1, dz1, comm=exchange_of(grads))
    keep_landed(grads, got)
    grads = {"ffn1_w_in": g_ffn1_w_in}
    du1, got = _ffn_in_bwd_x("ffn1_in_bwd", dz1, full["ffn1_w_in"], comm=exchange_of(grads))
    keep_landed(grads, got)
    grad_x, dsc1, dsh1 = _rmsmod_bwd("ffn1_norm_bwd", du1, h0, sc1, dh1, lx, lx)

    zero = jnp.zeros((1, d), F32)
    dmod_c = jnp.concatenate([dsh1[0], dsc1[0], dg1[0], dsh2[0], dsc2[0], zero, zero, zero, zero], axis=0)
    dmod_x = jnp.concatenate([dsh1[1], dsc1[1], dg1[1], dsh2[1], dsc2[1], dg2[1], dsh3[1], dsc3[1], dg3], axis=0)
    misc = jnp.concatenate([d_qg[0], d_kg[0], d_decay.reshape(-1), loss_tile[0, 0:1]])
    misc = jnp.pad(misc, (0, d - misc.shape[0]))[None, :]
    n_small = 3 * SUBLANES
    small = _pad_rows(jnp.concatenate([dmod_c, dmod_x, d_fn, misc], axis=0), n_small)
    small_all = _all_gather_small("gather_small", small)
    small_sum = _sum_slots("sum_small", small_all)
    dmod_c_sum, dmod_x_sum = small_sum[0:N_MOD], small_sum[N_MOD:2 * N_MOD]
    g_final_norm = small_sum[2 * N_MOD]
    misc_sum = small_sum[2 * N_MOD + 1]
    g_qg = misc_sum[0:HEAD_DIM][None, :]
    g_kg = misc_sum[HEAD_DIM:2 * HEAD_DIM][None, :]
    g_decay = misc_sum[2 * HEAD_DIM:2 * HEAD_DIM + 2 * heads_r].reshape(1, 2, heads_r)
    loss = misc_sum[2 * HEAD_DIM + 2 * heads_r]
    g_b_ada = (dmod_x_sum + dmod_c_sum).reshape(1, N_MOD * d)

    n_ada = w_ada.shape[2]
    dmod_rows = jnp.concatenate([small_all[:, N_MOD:2 * N_MOD, :].reshape(N_DEV, N_MOD * d),
                                 dmod_c_sum.reshape(1, N_MOD * d)], axis=0)
    dmod_mine = _pad_rows(lax.dynamic_slice_in_dim(dmod_rows, me * n_ada, n_ada, axis=1), 2 * SUBLANES)
    g_w_ada, ds_cond = _ada_bwd("ada_bwd", s_cond, dmod_mine, w_ada[0])
    cctx_parts = _all_gather_small("gather_cctx", ds_cond[N_DEV:N_DEV + SUBLANES])
    g_c_ctx = _cctx_grad("cctx_grad", cctx_parts, c_ctx[None, :])[0]

    mom = dict(ffn1_w_in=(m_ffn1_w_in, v_ffn1_w_in), mix_w_in=(m_mix_w_in, v_mix_w_in),
               w_proj_attn=(m_w_proj_attn, v_w_proj_attn), w_proj_ret=(m_w_proj_ret, v_w_proj_ret),
               ffn2_w_in=(m_ffn2_w_in, v_ffn2_w_in), ffn1_w_out=(m_ffn1_w_out, v_ffn1_w_out),
               mix_w_out=(m_mix_w_out, v_mix_w_out), ffn2_w_out=(m_ffn2_w_out, v_ffn2_w_out))
    res = {}
    for k in names:
        res[k] = _adamw("adamw_" + k, landed[k], shard[k], mom[k][0][0], mom[k][1][0])
    res["w_ada"] = _adamw("adamw_w_ada", g_w_ada[None], w_ada[0], m_w_ada[0], v_w_ada[0])

    def pack(cc, ba, qg, kg, dec, fn):
        misc_row = jnp.concatenate([qg.reshape(-1), kg.reshape(-1), dec.reshape(-1)])
        misc_row = jnp.pad(misc_row, (0, d - misc_row.shape[0]))[None, :]
        return _pad_rows(jnp.concatenate([cc.reshape(1, d), ba.reshape(N_MOD, d), fn.reshape(1, d), misc_row], axis=0),
                         2 * SUBLANES)

    sg, sd, sm, sv = _adamw(
        "adamw_small", pack(g_c_ctx, g_b_ada, g_qg, g_kg, g_decay, g_final_norm)[None],
        pack(c_ctx, b_ada, attn_q_gain, attn_k_gain, ret_decay_logit, final_norm),
        pack(m_c_ctx, m_b_ada, m_attn_q_gain, m_attn_k_gain, m_ret_decay_logit, m_final_norm),
        pack(v_c_ctx, v_b_ada, v_attn_q_gain, v_attn_k_gain, v_ret_decay_logit, v_final_norm))

    def unpack(a):
        misc_row = a[N_MOD + 2]
        return dict(c_ctx=a[0], b_ada=a[1:1 + N_MOD].reshape(1, N_MOD * d), final_norm=a[N_MOD + 1],
                    attn_q_gain=misc_row[0:HEAD_DIM][None, :], attn_k_gain=misc_row[HEAD_DIM:2 * HEAD_DIM][None, :],
                    ret_decay_logit=misc_row[2 * HEAD_DIM:2 * HEAD_DIM + 2 * heads_r].reshape(1, 2, heads_r))

    small_out = [unpack(a) for a in (sg, sd, sm, sv)]
    order = ["c_ctx", "w_ada", "b_ada", "ffn1_w_in", "ffn1_w_out", "mix_w_in", "attn_q_gain", "attn_k_gain",
             "ret_decay_logit", "w_proj_attn", "w_proj_ret", "mix_w_out", "ffn2_w_in", "ffn2_w_out", "final_norm"]
    outs = [loss, grad_x[None]]
    for which in range(4):
        for k in order:
            outs.append(res[k][which][None] if k in res else small_out[which][k])
    return tuple(outs)
```

```python
import math

import jax
import jax.numpy as jnp
from jax import lax
from jax.experimental import pallas as pl
from jax.experimental.pallas import tpu as pltpu

F32 = jnp.float32
BF16 = jnp.bfloat16
MESH = pl.DeviceIdType.MESH

N_DEV = 8
HEAD_DIM = 128
GRID_W = 64
ROPE_THETA = 10000.0
NORM_EPS = 1e-6
RET_CHUNK = 128
N_MOD = 9
LANES = 128
SUBLANES = 8
V7X_VMEM_BYTES = 64 * 1024 * 1024
VMEM_LIMIT = V7X_VMEM_BYTES - 8 * 1024 * 1024
K_TILE = 2560
M_TILE_BIG = 1408
M_TILE_DW = 2048
LOG2E = 1.4426950408889634
LN2 = 0.6931471805599453

ADAM_LR = 0.001
ADAM_B1 = 0.9
ADAM_B2 = 0.999
ADAM_EPS = 1e-08
ADAM_WD = 0.01
ADAM_STEP = 10

NN = (((1,), (0,)), ((), ()))
NT = (((1,), (1,)), ((), ()))
TN = (((0,), (0,)), ((), ()))


def _tile(n, pref, align=LANES):
    best = None
    t = align
    while t <= min(n, pref):
        if n % t == 0:
            best = t
        t += align
    return n if best is None else best


def _cp(sem):
    return pltpu.CompilerParams(dimension_semantics=sem, vmem_limit_bytes=VMEM_LIMIT)


def _sigmoid(v):
    return 0.5 * jnp.tanh(0.5 * v) + 0.5


def _dot(a, b, dims):
    return lax.dot_general(a, b, dims, preferred_element_type=F32)


def _mm(name, a, a_spec, b_list, dims, grid, out_shapes, out_specs, acc_shape, epi, extras=(), comm=None):
    nb, ne, no = len(b_list), len(extras), len(out_shapes)
    nk = grid[2]
    kind, c_arrays, c_wide = comm if comm is not None else (None, [], [])
    ncm = len(c_arrays)

    def body(*refs):
        a_ref = refs[0]
        b_refs = refs[1:1 + nb]
        e_refs = refs[1 + nb:1 + nb + ne]
        pos = 1 + nb + ne
        c_ins = refs[pos:pos + ncm]
        o_refs = refs[pos + ncm:pos + ncm + no]
        c_outs = refs[pos + ncm + no:pos + 2 * ncm + no]
        scratch = refs[pos + 2 * ncm + no:]
        accs = scratch[:0 if nk == 1 else nb]
        ids = [pl.program_id(axis) for axis in range(3)]
        if ncm:
            start_comm, mid_comm, finish_comm = _comm_plan(kind, c_ins, c_outs, c_wide, *scratch[len(accs):])
            step = (ids[0] * grid[1] + ids[1]) * nk + ids[2]
            pl.when(step == 0)(start_comm)
            if mid_comm is not None:
                pl.when(step == (3 * grid[0] * grid[1] * nk) // 4)(mid_comm)

        def finish(tiles):
            vals = epi(tiles, e_refs)
            for o_ref, v in zip(o_refs, vals):
                if isinstance(v, tuple):
                    for idx, part in enumerate(v):
                        o_ref[idx] = part.astype(o_ref.dtype)
                else:
                    o_ref[...] = v.astype(o_ref.dtype)

        if nk == 1:
            finish([_dot(a_ref[...], b_ref[...], dims) for b_ref in b_refs])
        else:
            @pl.when(ids[2] == 0)
            def _():
                for acc in accs:
                    acc[...] = jnp.zeros(acc.shape, F32)

            av = a_ref[...]
            for b_ref, acc in zip(b_refs, accs):
                acc[...] += _dot(av, b_ref[...], dims)

            @pl.when(ids[2] == nk - 1)
            def _():
                finish([acc[...] for acc in accs])

        if ncm:
            pl.when(jnp.logical_and(jnp.logical_and(ids[0] == grid[0] - 1, ids[1] == grid[1] - 1),
                                    ids[2] == nk - 1))(finish_comm)

    any_spec = pl.BlockSpec(memory_space=pl.ANY)
    c_shapes = [_comm_out_shape(kind, s, w) for s, w in zip(c_arrays, c_wide)]
    scratch_shapes = [] if nk == 1 else [pltpu.VMEM(acc_shape, F32)] * nb
    if ncm:
        scratch_shapes = scratch_shapes + _comm_semaphores(ncm)
    semantics = ("arbitrary",) * 3 if ncm else ("parallel", "parallel", "arbitrary")
    outs = pl.pallas_call(
        body, name=name, grid=grid,
        in_specs=[a_spec] + [s for _, s in b_list] + [s for _, s in extras] + [any_spec] * ncm,
        out_specs=list(out_specs) + [any_spec] * ncm, out_shape=list(out_shapes) + c_shapes,
        scratch_shapes=scratch_shapes, compiler_params=_cp(semantics),
    )(a, *[b for b, _ in b_list], *[e for e, _ in extras], *c_arrays)
    main = outs[0] if no == 1 else tuple(outs[:no])
    return main if comm is None else (main, list(outs[no:]))


def _plain(accs, _):
    return (accs[0],)


def _mm_nn(name, a, b, out_dtype, tm_pref=1024, tn_pref=1024, tk_pref=K_TILE, comm=None):
    m, k = a.shape
    n = b.shape[1]
    tm, tn, tk = _tile(m, tm_pref), _tile(n, tn_pref), _tile(k, tk_pref)
    return _mm(name, a, pl.BlockSpec((tm, tk), lambda i, j, kk: (i, kk)),
               [(b, pl.BlockSpec((tk, tn), lambda i, j, kk: (kk, j)))], NN, (m // tm, n // tn, k // tk),
               [jax.ShapeDtypeStruct((m, n), out_dtype)], [pl.BlockSpec((tm, tn), lambda i, j, kk: (i, j))],
               (tm, tn), _plain, comm=comm)


def _mm_nt(name, a, b, out_dtype, tm_pref=1024, tn_pref=1024, tk_pref=K_TILE, comm=None):
    m, k = a.shape
    n = b.shape[0]
    tm, tn, tk = _tile(m, tm_pref), _tile(n, tn_pref), _tile(k, tk_pref)
    return _mm(name, a, pl.BlockSpec((tm, tk), lambda i, j, kk: (i, kk)),
               [(b, pl.BlockSpec((tn, tk), lambda i, j, kk: (j, kk)))], NT, (m // tm, n // tn, k // tk),
               [jax.ShapeDtypeStruct((m, n), out_dtype)], [pl.BlockSpec((tm, tn), lambda i, j, kk: (i, j))],
               (tm, tn), _plain, comm=comm)


def _mm_tn(name, a, b, out_dtype, rows=None, tm_pref=1024, tn_pref=1024, tk_pref=K_TILE, comm=None):
    k = a.shape[0] if rows is None else rows
    m, n = a.shape[1], b.shape[1]
    tm, tn, tk = _tile(m, tm_pref), _tile(n, tn_pref), _tile(k, tk_pref)
    return _mm(name, a, pl.BlockSpec((tk, tm), lambda i, j, kk: (kk, i)),
               [(b, pl.BlockSpec((tk, tn), lambda i, j, kk: (kk, j)))], TN, (m // tm, n // tn, k // tk),
               [jax.ShapeDtypeStruct((m, n), out_dtype)], [pl.BlockSpec((tm, tn), lambda i, j, kk: (i, j))],
               (tm, tn), _plain, comm=comm)


def _ffn_in(name, u, w_in, comm=None):
    r, d = u.shape
    f = w_in.shape[1] // 2
    tm, tn, tk = _tile(r, M_TILE_BIG), _tile(f, 512), _tile(d, K_TILE)
    nf = f // tn

    def epi(accs, _):
        za, zb = accs
        s = za * _sigmoid(za) * zb
        return (za, zb), s

    return _mm(name, u, pl.BlockSpec((tm, tk), lambda i, j, kk: (i, kk)),
               [(w_in, pl.BlockSpec((tk, tn), lambda i, j, kk: (kk, j))),
                (w_in, pl.BlockSpec((tk, tn), lambda i, j, kk: (kk, j + nf)))],
               NN, (r // tm, nf, d // tk),
               [jax.ShapeDtypeStruct((2, r, f), BF16), jax.ShapeDtypeStruct((r, f), BF16)],
               [pl.BlockSpec((2, tm, tn), lambda i, j, kk: (0, i, j)), pl.BlockSpec((tm, tn), lambda i, j, kk: (i, j))],
               (tm, tn), epi, comm=comm)


def _mm_residual(name, a, w, res, gate, gate_scale, lx, comm=None):
    r, k = a.shape
    n = w.shape[1]
    tm, tn, tk = _tile(r, 1024), _tile(n, 1024), _tile(k, K_TILE)

    def epi(accs, e_refs):
        res_ref, g_ref = e_refs
        rows = pl.program_id(0) * tm + lax.broadcasted_iota(jnp.int32, (tm, 1), 0)
        g = jnp.where(rows < lx, g_ref[1], g_ref[0])
        return res_ref[...] + gate_scale * g * accs[0], accs[0]

    return _mm(name, a, pl.BlockSpec((tm, tk), lambda i, j, kk: (i, kk)),
               [(w, pl.BlockSpec((tk, tn), lambda i, j, kk: (kk, j)))], NN, (r // tm, n // tn, k // tk),
               [jax.ShapeDtypeStruct((r, n), F32), jax.ShapeDtypeStruct((r, n), BF16)],
               [pl.BlockSpec((tm, tn), lambda i, j, kk: (i, j))] * 2, (tm, tn), epi,
               extras=[(res, pl.BlockSpec((tm, tn), lambda i, j, kk: (i, j))),
                       (gate, pl.BlockSpec((2, 1, tn), lambda i, j, kk: (0, 0, j)))], comm=comm)


def _ffn_out_bwd(name, df, w_out, z, comm=None):
    r, d = df.shape
    f = w_out.shape[0]
    tm, tn, tk = _tile(r, M_TILE_BIG), _tile(f, 512), _tile(d, K_TILE)

    def epi(accs, e_refs):
        ds = accs[0]
        za = e_refs[0][0].astype(F32)
        zb = e_refs[0][1].astype(F32)
        sg = _sigmoid(za)
        da = ds * zb * sg * (1.0 + za * (1.0 - sg))
        db = ds * za * sg
        return ((da, db),)

    zspec = pl.BlockSpec((2, tm, tn), lambda i, j, kk: (0, i, j))
    return _mm(name, df, pl.BlockSpec((tm, tk), lambda i, j, kk: (i, kk)),
               [(w_out, pl.BlockSpec((tn, tk), lambda i, j, kk: (j, kk)))], NT, (r // tm, f // tn, d // tk),
               [jax.ShapeDtypeStruct((2, r, f), BF16)], [zspec], (tm, tn), epi, extras=[(z, zspec)], comm=comm)


def _ffn_in_bwd_x(name, dz, w_in, comm=None):
    _, r, f = dz.shape
    d = w_in.shape[0]
    tm, tn, tk = _tile(r, 1024), _tile(d, 2048), _tile(f, K_TILE)
    nkf = f // tk
    return _mm(name, dz, pl.BlockSpec((None, tm, tk), lambda i, j, kk: (kk // nkf, i, kk % nkf)),
               [(w_in, pl.BlockSpec((tn, tk), lambda i, j, kk: (j, kk)))], NT, (r // tm, d // tn, 2 * nkf),
               [jax.ShapeDtypeStruct((r, d), F32)], [pl.BlockSpec((tm, tn), lambda i, j, kk: (i, j))],
               (tm, tn), _plain, comm=comm)


def _ffn_in_bwd_w(name, u, dz, comm=None):
    r, d = u.shape
    f = dz.shape[2]
    tm, tn, tk = _tile(d, M_TILE_DW), _tile(f, 512), _tile(r, K_TILE)
    nf = f // tn
    return _mm(name, u, pl.BlockSpec((tk, tm), lambda i, j, kk: (kk, i)),
               [(dz, pl.BlockSpec((None, tk, tn), lambda i, j, kk: (j // nf, kk, j % nf)))], TN,
               (d // tm, 2 * nf, r // tk),
               [jax.ShapeDtypeStruct((d, 2 * f), BF16)], [pl.BlockSpec((tm, tn), lambda i, j, kk: (i, j))],
               (tm, tn), _plain, comm=comm)


def _merge_bwd(name, dout, w_out, pa, pr, p, ga_off, gb_off):
    r, d = dout.shape
    n = w_out.shape[0]
    cw = math.gcd(math.gcd(ga_off, gb_off), n)
    tm, tn, tk = _tile(r, 1024), _tile(cw, 512), _tile(d, K_TILE)

    def epi(accs, e_refs):
        dm = accs[0]
        pa_ref, pr_ref, ga_ref, gb_ref = e_refs
        sa = _sigmoid(ga_ref[...].astype(F32))
        sb = _sigmoid(gb_ref[...].astype(F32))
        pav = pa_ref[...].astype(F32)
        prv = pr_ref[...].astype(F32)
        return dm * sa, dm * sb, dm * pav * sa * (1.0 - sa), dm * prv * sb * (1.0 - sb)

    o_spec = pl.BlockSpec((tm, tn), lambda i, j, kk: (i, j))
    return _mm(name, dout, pl.BlockSpec((tm, tk), lambda i, j, kk: (i, kk)),
               [(w_out, pl.BlockSpec((tn, tk), lambda i, j, kk: (j, kk)))], NT, (r // tm, n // tn, d // tk),
               [jax.ShapeDtypeStruct((r, n), BF16)] * 4, [o_spec] * 4, (tm, tn), epi,
               extras=[(pa, o_spec), (pr, o_spec),
                       (p, pl.BlockSpec((tm, tn), lambda i, j, kk: (i, ga_off // tn + j))),
                       (p, pl.BlockSpec((tm, tn), lambda i, j, kk: (i, gb_off // tn + j)))])


def _row_tile(r, lx, d):
    pref = 256 if d > 1024 else 512
    return _tile(math.gcd(r, lx), pref, SUBLANES)


def _rmsmod(name, h, scale, shift, lx, rows=None, comm=None):
    r = h.shape[0] if rows is None else rows
    d = h.shape[1]
    tr = _row_tile(r, lx, d)
    nx = lx // tr
    steps = r // tr
    cls = lambda i: (jnp.where(i < nx, 1, 0), 0, 0)
    kind, c_arrays, c_wide = comm if comm is not None else (None, [], [])
    ncm = len(c_arrays)

    def body(h_ref, sc_ref, sh_ref, *rest):
        u_ref = rest[ncm]
        i = pl.program_id(0)
        if ncm:
            start_comm, mid_comm, finish_comm = _comm_plan(kind, rest[:ncm], rest[ncm + 1:2 * ncm + 1], c_wide,
                                                           *rest[2 * ncm + 1:])
            pl.when(i == 0)(start_comm)
            if mid_comm is not None:
                pl.when(i == steps - 1)(mid_comm)
        hv = h_ref[...]
        rinv = lax.rsqrt(jnp.mean(hv * hv, axis=-1, keepdims=True) + NORM_EPS)
        u_ref[...] = (hv * rinv * (1.0 + sc_ref[...]) + sh_ref[...]).astype(u_ref.dtype)
        if ncm:
            pl.when(i == steps - 1)(finish_comm)

    any_spec = pl.BlockSpec(memory_space=pl.ANY)
    outs = pl.pallas_call(
        body, name=name, grid=(steps,),
        in_specs=[pl.BlockSpec((tr, d), lambda i: (i, 0)), pl.BlockSpec((None, 1, d), cls),
                  pl.BlockSpec((None, 1, d), cls)] + [any_spec] * ncm,
        out_specs=[pl.BlockSpec((tr, d), lambda i: (i, 0))] + [any_spec] * ncm,
        out_shape=[jax.ShapeDtypeStruct((r, d), BF16)] + [_comm_out_shape(kind, s, w) for s, w in zip(c_arrays, c_wide)],
        scratch_shapes=_comm_semaphores(ncm) if ncm else [],
        compiler_params=_cp(("arbitrary",) if ncm else ("parallel",)))(h, scale, shift, *c_arrays)
    return outs[0] if comm is None else (outs[0], list(outs[1:]))


def _rmsmod_bwd(name, du, h, scale, dh_in, lx, rows_out, gate=None):
    r, d = du.shape
    rin = dh_in.shape[0]
    tr = _row_tile(math.gcd(r, math.gcd(rin, rows_out)), lx, d)
    nx, nin, nout = lx // tr, rin // tr, rows_out // tr
    cls = lambda i: (jnp.where(i < nx, 1, 0), 0, 0)
    assert gate is None or rows_out == r

    def body(du_ref, h_ref, sc_ref, dhin_ref, *rest):
        if gate is None:
            dh_ref, dsc_ref, dsh_ref = rest
        else:
            f_ref, g_ref, dh_ref, dsc_ref, dsh_ref, df_ref, dg_ref = rest
        i = pl.program_id(0)
        hv = h_ref[...]
        duv = du_ref[...]
        rinv = lax.rsqrt(jnp.mean(hv * hv, axis=-1, keepdims=True) + NORM_EPS)
        nv = hv * rinv
        dn = duv * (1.0 + sc_ref[...])

        @pl.when(jnp.logical_or(i == 0, i == nx))
        def _():
            dsc_ref[...] = jnp.zeros(dsc_ref.shape, F32)
            dsh_ref[...] = jnp.zeros(dsh_ref.shape, F32)

        dsc_ref[...] += jnp.sum(duv * nv, axis=0, keepdims=True)
        dsh_ref[...] += jnp.sum(duv, axis=0, keepdims=True)

        def dh_out():
            dh = rinv * (dn - nv * jnp.mean(dn * nv, axis=-1, keepdims=True))
            return dh + jnp.where(i < nin, dhin_ref[...], 0.0)

        if gate is None:
            @pl.when(i < nout)
            def _():
                dh_ref[...] = dh_out()
        else:
            dhv = dh_out()
            dh_ref[...] = dhv

            @pl.when(jnp.logical_or(i == 0, i == nx))
            def _():
                dg_ref[...] = jnp.zeros(dg_ref.shape, F32)

            df_ref[...] = (gate[2] * g_ref[...] * dhv).astype(df_ref.dtype)
            dg_ref[...] += jnp.sum(gate[2] * f_ref[...].astype(F32) * dhv, axis=0, keepdims=True)

    row = pl.BlockSpec((tr, d), lambda i: (i, 0))
    per_class = pl.BlockSpec((None, 1, d), cls)
    in_specs = [row, row, per_class, pl.BlockSpec((tr, d), lambda i: (jnp.minimum(i, nin - 1), 0))]
    out_specs = [pl.BlockSpec((tr, d), lambda i: (jnp.minimum(i, nout - 1), 0)), per_class, per_class]
    out_shape = [jax.ShapeDtypeStruct((rows_out, d), F32), jax.ShapeDtypeStruct((2, 1, d), F32),
                 jax.ShapeDtypeStruct((2, 1, d), F32)]
    args = [du, h, scale, dh_in]
    if gate is not None:
        in_specs += [row, per_class]
        out_specs += [row, per_class]
        out_shape += [jax.ShapeDtypeStruct((r, d), BF16), jax.ShapeDtypeStruct((2, 1, d), F32)]
        args += [gate[0], gate[1]]
    return pl.pallas_call(
        body, name=name, grid=(r // tr,), in_specs=in_specs, out_specs=out_specs, out_shape=out_shape,
        compiler_params=_cp(("arbitrary",)))(*args)


def _final_loss(name, h, final_norm, target, f, gate, gate_scale):
    r, d = h.shape
    tr = _row_tile(r, r, d)

    def body(h_ref, fn_ref, t_ref, f_ref, g_ref, dh_ref, dfn_ref, loss_ref, df_ref, dg_ref):
        i = pl.program_id(0)
        hv = h_ref[...]
        rinv = lax.rsqrt(jnp.mean(hv * hv, axis=-1, keepdims=True) + NORM_EPS)
        nv = hv * rinv
        fn = fn_ref[...]
        err = nv * fn - t_ref[...]
        dy = err * (1.0 / d)

        @pl.when(i == 0)
        def _():
            dfn_ref[...] = jnp.zeros(dfn_ref.shape, F32)
            loss_ref[...] = jnp.zeros(loss_ref.shape, F32)
            dg_ref[...] = jnp.zeros(dg_ref.shape, F32)

        loss_ref[...] += 0.5 * jnp.sum(jnp.mean(err * err, axis=-1, keepdims=True), axis=0, keepdims=True)
        dfn_ref[...] += jnp.sum(dy * nv, axis=0, keepdims=True)
        dn = dy * fn
        dhv = rinv * (dn - nv * jnp.mean(dn * nv, axis=-1, keepdims=True))
        dh_ref[...] = dhv
        df_ref[...] = (gate_scale * g_ref[...] * dhv).astype(df_ref.dtype)
        dg_ref[...] += jnp.sum(gate_scale * f_ref[...].astype(F32) * dhv, axis=0, keepdims=True)

    row = pl.BlockSpec((tr, d), lambda i: (i, 0))
    one = pl.BlockSpec((1, d), lambda i: (0, 0))
    return pl.pallas_call(
        body, name=name, grid=(r // tr,),
        in_specs=[row, one, row, row, pl.BlockSpec((None, 1, d), lambda i: (1, 0, 0))],
        out_specs=[row, one, pl.BlockSpec((SUBLANES, LANES), lambda i: (0, 0)), row, one],
        out_shape=[jax.ShapeDtypeStruct((r, d), F32), jax.ShapeDtypeStruct((1, d), F32),
                   jax.ShapeDtypeStruct((SUBLANES, LANES), F32), jax.ShapeDtypeStruct((r, d), BF16),
                   jax.ShapeDtypeStruct((1, d), F32)],
        compiler_params=_cp(("arbitrary",)))(h, final_norm, target, f, gate)


def _swap_pairs(t):
    lane = lax.broadcasted_iota(jnp.int32, t.shape, 1)
    nxt = pltpu.roll(t, HEAD_DIM - 1, 1)
    prv = pltpu.roll(t, 1, 1)
    return jnp.where(lane % 2 == 0, nxt, prv)


def _qk_prep(name, p, cosf, sinf, q_gain, k_gain, aw, kw):
    t = p.shape[0]
    tr = _tile(t, 256, SUBLANES)
    q_prescale = HEAD_DIM ** -0.5 * LOG2E

    def head_fwd(v, gain, cs, sn):
        v = v.astype(F32)
        rinv = lax.rsqrt(jnp.mean(v * v, axis=-1, keepdims=True) + NORM_EPS)
        tt = v * rinv * gain
        return tt * cs + _swap_pairs(tt) * sn

    def body(q_ref, k_ref, cos_ref, sin_ref, qg_ref, kg_ref, qo_ref, ko_ref):
        cs, sn = cos_ref[...], sin_ref[...]
        for hh in range(aw // HEAD_DIM):
            sl = slice(hh * HEAD_DIM, (hh + 1) * HEAD_DIM)
            qo_ref[:, sl] = (head_fwd(q_ref[:, sl], qg_ref[...], cs, sn) * q_prescale).astype(qo_ref.dtype)
        for hh in range(kw // HEAD_DIM):
            sl = slice(hh * HEAD_DIM, (hh + 1) * HEAD_DIM)
            ko_ref[:, sl] = head_fwd(k_ref[:, sl], kg_ref[...], cs, sn).astype(ko_ref.dtype)

    assert aw % kw == 0
    row = lambda i: (i, 0)
    return pl.pallas_call(
        body, name=name, grid=(t // tr,),
        in_specs=[pl.BlockSpec((tr, aw), row), pl.BlockSpec((tr, kw), lambda i: (i, aw // kw)),
                  pl.BlockSpec((tr, HEAD_DIM), row), pl.BlockSpec((tr, HEAD_DIM), row),
                  pl.BlockSpec((1, HEAD_DIM), lambda i: (0, 0)), pl.BlockSpec((1, HEAD_DIM), lambda i: (0, 0))],
        out_specs=[pl.BlockSpec((tr, aw), row), pl.BlockSpec((tr, kw), row)],
        out_shape=[jax.ShapeDtypeStruct((t, aw), BF16), jax.ShapeDtypeStruct((t, kw), BF16)],
        compiler_params=_cp(("parallel",)))(p, p, cosf, sinf, q_gain, k_gain)


def _qk_prep_bwd(name, dq_rot, dk_rot, p, cosf, sinf, q_gain, k_gain, aw, kw):
    t = p.shape[0]
    lq = dq_rot.shape[0]
    tr = _tile(math.gcd(t, lq), 256, SUBLANES)
    nq = lq // tr

    def head_bwd(dout, v, gain, cs, sn):
        v = v.astype(F32)
        rinv = lax.rsqrt(jnp.mean(v * v, axis=-1, keepdims=True) + NORM_EPS)
        vn = v * rinv
        dt = dout * cs - _swap_pairs(dout) * sn
        dvn = dt * gain
        dv = rinv * (dvn - vn * jnp.mean(dvn * vn, axis=-1, keepdims=True))
        return dv, jnp.sum(dt * vn, axis=0, keepdims=True)

    def body(dq_ref, dk_ref, q_ref, k_ref, cos_ref, sin_ref, qg_ref, kg_ref, dqo_ref, dko_ref, dqg_ref, dkg_ref):
        i = pl.program_id(0)
        cs, sn = cos_ref[...], sin_ref[...]

        @pl.when(i == 0)
        def _():
            dqg_ref[...] = jnp.zeros(dqg_ref.shape, F32)
            dkg_ref[...] = jnp.zeros(dkg_ref.shape, F32)

        has_q = i < nq
        for hh in range(aw // HEAD_DIM):
            sl = slice(hh * HEAD_DIM, (hh + 1) * HEAD_DIM)
            dout = jnp.where(has_q, dq_ref[:, sl], 0.0)
            dv, dg = head_bwd(dout, q_ref[:, sl], qg_ref[...], cs, sn)
            dqo_ref[:, sl] = dv.astype(dqo_ref.dtype)
            dqg_ref[...] += dg
        for hh in range(kw // HEAD_DIM):
            sl = slice(hh * HEAD_DIM, (hh + 1) * HEAD_DIM)
            dv, dg = head_bwd(dk_ref[:, sl], k_ref[:, sl], kg_ref[...], cs, sn)
            dko_ref[:, sl] = dv.astype(dko_ref.dtype)
            dkg_ref[...] += dg

    row = lambda i: (i, 0)
    one = lambda i: (0, 0)
    return pl.pallas_call(
        body, name=name, grid=(t // tr,),
        in_specs=[pl.BlockSpec((tr, aw), lambda i: (jnp.minimum(i, nq - 1), 0)), pl.BlockSpec((tr, kw), row),
                  pl.BlockSpec((tr, aw), row), pl.BlockSpec((tr, kw), lambda i: (i, aw // kw)),
                  pl.BlockSpec((tr, HEAD_DIM), row), pl.BlockSpec((tr, HEAD_DIM), row),
                  pl.BlockSpec((1, HEAD_DIM), one), pl.BlockSpec((1, HEAD_DIM), one)],
        out_specs=[pl.BlockSpec((tr, aw), row), pl.BlockSpec((tr, kw), row),
                   pl.BlockSpec((1, HEAD_DIM), one), pl.BlockSpec((1, HEAD_DIM), one)],
        out_shape=[jax.ShapeDtypeStruct((t, aw), BF16), jax.ShapeDtypeStruct((t, kw), BF16),
                   jax.ShapeDtypeStruct((1, HEAD_DIM), F32), jax.ShapeDtypeStruct((1, HEAD_DIM), F32)],
        compiler_params=_cp(("arbitrary",)))(dq_rot, dk_rot, p, p, cosf, sinf, q_gain, k_gain)


def _attn_tiles(lq, t):
    return _tile(lq, 1024), _tile(t, 768)


def _attn_fwd(name, q, k, p, v_off, lq, groups):
    t = k.shape[0]
    hq = q.shape[1] // HEAD_DIM
    hkv = hq // groups
    gw = groups * HEAD_DIM
    tq, tk = _attn_tiles(lq, t)
    nkv = t // tk
    vb = v_off // HEAD_DIM

    def body(q_ref, k_ref, v_ref, o_ref, lse_ref, m_sc, l_sc, acc_sc, s_sc):
        j = pl.program_id(2)

        @pl.when(j == 0)
        def _():
            m_sc[...] = jnp.full(m_sc.shape, -jnp.inf, F32)
            l_sc[...] = jnp.zeros(l_sc.shape, F32)
            acc_sc[...] = jnp.zeros(acc_sc.shape, F32)
            s_sc[...] = jnp.zeros(s_sc.shape, F32)

        has_prev = j > 0
        kv, vv = k_ref[...], v_ref[...]
        for g in range(groups):
            sl = slice(g * HEAD_DIM, (g + 1) * HEAD_DIM)
            s = s_sc[g]
            s_sc[g] = _dot(q_ref[:, sl], kv, NT)
            m_prev = m_sc[g]
            m_new = jnp.where(has_prev, jnp.maximum(m_prev, jnp.max(s, axis=-1, keepdims=True)), jnp.inf)
            alpha = jnp.exp2(m_prev - m_new)
            pexp = jnp.exp2((s - m_new).astype(BF16))
            part = pexp[:, 0:LANES].astype(F32)
            for cb in range(1, tk // LANES):
                part = part + pexp[:, cb * LANES:(cb + 1) * LANES].astype(F32)
            l_sc[g] = alpha * l_sc[g] + part
            acc_sc[:, sl] = alpha * acc_sc[:, sl] + _dot(pexp, vv, NN)
            m_sc[g] = jnp.where(has_prev, m_new, -jnp.inf)

        @pl.when(j == nkv)
        def _():
            for g in range(groups):
                sl = slice(g * HEAD_DIM, (g + 1) * HEAD_DIM)
                l_row = jnp.sum(l_sc[g], axis=-1, keepdims=True)
                o_ref[:, sl] = (acc_sc[:, sl] * (1.0 / l_row)).astype(o_ref.dtype)
                lse_ref[g] = jnp.broadcast_to(m_sc[g] + jnp.log2(l_row), (tq, LANES))

    return pl.pallas_call(
        body, name=name, grid=(hkv, lq // tq, nkv + 1),
        in_specs=[pl.BlockSpec((tq, gw), lambda h, i, j: (i, h)),
                  pl.BlockSpec((tk, HEAD_DIM), lambda h, i, j: (jnp.minimum(j, nkv - 1), h)),
                  pl.BlockSpec((tk, HEAD_DIM), lambda h, i, j: (jnp.maximum(j - 1, 0), vb + h))],
        out_specs=[pl.BlockSpec((tq, gw), lambda h, i, j: (i, h)),
                   pl.BlockSpec((groups, tq, LANES), lambda h, i, j: (h, i, 0))],
        out_shape=[jax.ShapeDtypeStruct((lq, hq * HEAD_DIM), BF16), jax.ShapeDtypeStruct((hq, lq, LANES), F32)],
        scratch_shapes=[pltpu.VMEM((groups, tq, 1), F32), pltpu.VMEM((groups, tq, LANES), F32), pltpu.VMEM((tq, gw), F32),
                        pltpu.VMEM((groups, tq, tk), F32)],
        compiler_params=_cp(("parallel", "parallel", "arbitrary")))(q, k, p)


def _attn_delta(name, o, do):
    lq, aw = o.shape
    hq = aw // HEAD_DIM
    tr = _tile(lq, 512, SUBLANES)

    def body(o_ref, do_ref, d_ref):
        for h in range(hq):
            sl = slice(h * HEAD_DIM, (h + 1) * HEAD_DIM)
            dsum = jnp.sum(do_ref[:, sl].astype(F32) * o_ref[:, sl].astype(F32), axis=-1, keepdims=True)
            d_ref[h] = jnp.broadcast_to(dsum, (tr, LANES))

    spec = pl.BlockSpec((tr, aw), lambda i: (i, 0))
    return pl.pallas_call(
        body, name=name, grid=(lq // tr,), in_specs=[spec, spec],
        out_specs=pl.BlockSpec((hq, tr, LANES), lambda i: (0, i, 0)),
        out_shape=jax.ShapeDtypeStruct((hq, lq, LANES), F32), compiler_params=_cp(("parallel",)))(o, do)


def _attn_bwd(name, q, k, p, v_off, do, lse, delta, lq, groups):
    t = k.shape[0]
    hkv = k.shape[1] // HEAD_DIM
    gw = groups * HEAD_DIM
    tq, tk = _attn_tiles(lq, t)
    nq, nkv = lq // tq, t // tk
    scale = HEAD_DIM ** -0.5
    vb = v_off // HEAD_DIM

    def body(q_ref, k_ref, v_ref, do_ref, lse_ref, delta_ref, dq_ref, dk_ref, dv_ref, dq_sc, dk_sc, dv_sc):
        j, i = pl.program_id(1), pl.program_id(2)

        @pl.when(i == 0)
        def _():
            dk_sc[...] = jnp.zeros(dk_sc.shape, F32)
            dv_sc[...] = jnp.zeros(dv_sc.shape, F32)

        @pl.when(j == 0)
        def _():
            dq_sc[i] = jnp.zeros((tq, gw), F32)

        kv, vv = k_ref[...], v_ref[...]
        dk_part, dv_part = None, None
        for g in range(groups):
            sl = slice(g * HEAD_DIM, (g + 1) * HEAD_DIM)
            qv, dov = q_ref[:, sl], do_ref[:, sl]
            s = _dot(qv, kv, NT)
            pexp = jnp.exp2(s - lse_ref[g, :, 0:1])
            dv_g = _dot(pexp.astype(BF16), dov, TN)
            dp = _dot(dov, vv, NT)
            ds = (pexp * (dp - delta_ref[g, :, 0:1])).astype(BF16)
            dk_g = _dot(ds, qv, TN)
            dq_sc[i, :, sl] += _dot(ds, kv, NN)
            dk_part = dk_g if dk_part is None else dk_part + dk_g
            dv_part = dv_g if dv_part is None else dv_part + dv_g
        dk_sc[...] += dk_part
        dv_sc[...] += dv_part

        @pl.when(i == nq - 1)
        def _():
            dk_ref[...] = dk_sc[...] * LN2
            dv_ref[...] = dv_sc[...].astype(dv_ref.dtype)

        @pl.when(j == nkv - 1)
        def _():
            dq_ref[...] = dq_sc[i] * scale

    qspec = pl.BlockSpec((tq, gw), lambda kh, j, i: (i, kh))
    kspec = pl.BlockSpec((tk, HEAD_DIM), lambda kh, j, i: (j, kh))
    rowspec = pl.BlockSpec((groups, tq, LANES), lambda kh, j, i: (kh, i, 0))
    dqspec = pl.BlockSpec((tq, gw), lambda kh, j, i: (jnp.where(j == nkv - 1, i, 0), kh))
    return pl.pallas_call(
        body, name=name, grid=(hkv, nkv, nq),
        in_specs=[qspec, kspec, pl.BlockSpec((tk, HEAD_DIM), lambda kh, j, i: (j, vb + kh)), qspec, rowspec, rowspec],
        out_specs=[dqspec, kspec, kspec],
        out_shape=[jax.ShapeDtypeStruct((lq, hkv * gw), F32), jax.ShapeDtypeStruct((t, hkv * HEAD_DIM), F32),
                   jax.ShapeDtypeStruct((t, hkv * HEAD_DIM), BF16)],
        scratch_shapes=[pltpu.VMEM((nq, tq, gw), F32), pltpu.VMEM((tk, HEAD_DIM), F32), pltpu.VMEM((tk, HEAD_DIM), F32)],
        compiler_params=_cp(("parallel", "arbitrary", "arbitrary")))(q, k, p, do, lse, delta)


def _ret_tables(log_gamma, direction):
    c = RET_CHUNK
    idx = jnp.arange(c, dtype=F32)
    diff = idx[:, None] - idx[None, :]
    if direction == 1:
        diff = -diff
    keep = diff >= 0
    lg = log_gamma.astype(F32)
    mask = jnp.where(keep[None], jnp.exp(jnp.where(keep, diff, 0.0)[None] * lg[:, None, None]), 0.0)
    q_exp = idx + 1.0 if direction == 0 else c - idx
    k_exp = c - 1.0 - idx if direction == 0 else idx
    sign = 1.0 if direction == 0 else -1.0
    lane = lambda v: jnp.broadcast_to(v[..., None], v.shape + (LANES,))
    qdec = lane(jnp.exp(q_exp[None, :] * lg[:, None]))
    kdec = lane(jnp.exp(k_exp[None, :] * lg[:, None]))
    cdec = jnp.broadcast_to(jnp.exp(c * lg)[:, None, None], (lg.shape[0], SUBLANES, LANES))
    weights = lane(jnp.stack([sign * idx, q_exp, -sign * idx, k_exp], axis=0))
    return mask, qdec, kdec, cdec, weights


def _ret_chunk_of(direction, step, nx, nc):
    if direction == 0:
        return jnp.where(step < nc, nx + step, step - nc)
    return jnp.where(step < nc, nx + nc - 1 - step, nx + nc - 1 - step)


def _ret_fwd(name, p, offs, tables, direction, lx, prev):
    q_off, k_off, v_off, rw = offs
    t = p.shape[0]
    c = RET_CHUNK
    n_steps, nx = t // c, lx // c
    nc = n_steps - nx
    bw = math.gcd(math.gcd(q_off, k_off), math.gcd(v_off, rw))
    bw = _tile(bw, 512)
    hpb = bw // HEAD_DIM
    heads = rw // HEAD_DIM
    k_scale = HEAD_DIM ** -0.5
    mask, qdec, kdec, cdec, _ = tables
    rc = lambda n: _ret_chunk_of(direction, n, nx, nc)

    ng = heads // hpb

    def body(*refs):
        q_refs, k_refs, v_refs = refs[0:ng], refs[ng:2 * ng], refs[2 * ng:3 * ng]
        if prev is None:
            m_ref, qd_ref, kd_ref, cd_ref, y_ref, st_ref, s_sc = refs[3 * ng:]
        else:
            m_ref, qd_ref, kd_ref, cd_ref, prev_ref, y_ref, st_ref, s_sc = refs[3 * ng:]
        n = pl.program_id(0)

        @pl.when(n == 0)
        def _():
            s_sc[...] = jnp.zeros(s_sc.shape, F32)

        for hd in range(heads):
            gg, hh = divmod(hd, hpb)
            sl = slice(hh * HEAD_DIM, (hh + 1) * HEAD_DIM)
            osl = slice(hd * HEAD_DIM, (hd + 1) * HEAD_DIM)
            qv = q_refs[gg][:, sl]
            kf = k_refs[gg][:, sl].astype(F32) * k_scale
            kv = kf.astype(BF16)
            vv = v_refs[gg][:, sl]
            state = s_sc[hd]
            st_ref[hd] = state
            a = _dot(qv, kv, NT) * m_ref[hd]
            y = _dot(a.astype(BF16), vv, NN) + _dot(qv, state.astype(BF16), NN) * qd_ref[hd]
            s_sc[hd] = state * cd_ref[hd, 0:1, :] + _dot((kf * kd_ref[hd]).astype(BF16), vv, TN)
            if prev is not None:
                y = y + prev_ref[:, osl]
            y_ref[:, osl] = y

    col = lambda off, gg: (lambda n: (rc(n), off // bw + gg))
    tab3 = lambda n: (0, 0, 0)
    in_specs = [pl.BlockSpec((c, bw), col(off, gg)) for off in (q_off, k_off, v_off) for gg in range(ng)]
    in_specs += [pl.BlockSpec((heads, c, c), tab3), pl.BlockSpec((heads, c, LANES), tab3),
                 pl.BlockSpec((heads, c, LANES), tab3), pl.BlockSpec((heads, SUBLANES, LANES), tab3)]
    args = [p] * (3 * ng) + [mask, qdec, kdec, cdec]
    aliases = {}
    yspec = pl.BlockSpec((c, rw), lambda n: (rc(n), 0))
    if prev is not None:
        in_specs.append(yspec)
        args.append(prev)
        aliases = {len(args) - 1: 0}
    return pl.pallas_call(
        body, name=name, grid=(n_steps,), in_specs=in_specs,
        out_specs=[yspec, pl.BlockSpec((None, heads, HEAD_DIM, HEAD_DIM), lambda n: (n, 0, 0, 0))],
        out_shape=[jax.ShapeDtypeStruct((t, rw), F32), jax.ShapeDtypeStruct((n_steps, heads, HEAD_DIM, HEAD_DIM), F32)],
        scratch_shapes=[pltpu.VMEM((heads, HEAD_DIM, HEAD_DIM), F32)], input_output_aliases=aliases,
        compiler_params=_cp(("arbitrary",)))(*args)


def _ret_bwd(name, p, offs, tables, states, dy, direction, lx, prev):
    q_off, k_off, v_off, rw = offs
    t = p.shape[0]
    c = RET_CHUNK
    n_steps, nx = t // c, lx // c
    nc = n_steps - nx
    bw = math.gcd(math.gcd(q_off, k_off), math.gcd(v_off, rw))
    bw = _tile(bw, 512)
    hpb = bw // HEAD_DIM
    heads = rw // HEAD_DIM
    k_scale = HEAD_DIM ** -0.5
    mask, qdec, kdec, cdec, weights = tables
    step_of = lambda n: n_steps - 1 - n
    rc = lambda n: _ret_chunk_of(direction, step_of(n), nx, nc)

    ng = heads // hpb

    def body(*refs):
        q_refs, k_refs, v_refs = refs[0:ng], refs[ng:2 * ng], refs[2 * ng:3 * ng]
        if prev is None:
            (dy_ref, st_ref, m_ref, qd_ref, kd_ref, cd_ref, w_ref,
             dq_ref, dk_ref, dv_ref, dl_ref, ds_sc, lam_sc) = refs[3 * ng:]
        else:
            (dy_ref, st_ref, m_ref, qd_ref, kd_ref, cd_ref, w_ref, pq_ref, pk_ref, pv_ref,
             dq_ref, dk_ref, dv_ref, dl_ref, ds_sc, lam_sc) = refs[3 * ng:]
        n = pl.program_id(0)

        @pl.when(n == 0)
        def _():
            ds_sc[...] = jnp.zeros(ds_sc.shape, F32)
            lam_sc[...] = jnp.zeros(lam_sc.shape, F32)

        is_x = rc(n) < nx
        for hd in range(heads):
            gg, hh = divmod(hd, hpb)
            sl = slice(hh * HEAD_DIM, (hh + 1) * HEAD_DIM)
            osl = slice(hd * HEAD_DIM, (hd + 1) * HEAD_DIM)
            qv = q_refs[gg][:, sl]
            qf = qv.astype(F32)
            kf = k_refs[gg][:, sl].astype(F32) * k_scale
            kv = kf.astype(BF16)
            vv = v_refs[gg][:, sl]
            dyv = jnp.where(is_x, dy_ref[:, osl], 0.0).astype(BF16)
            state = st_ref[hd]
            dstate = ds_sc[hd]
            dstate_b = dstate.astype(BF16)
            msk, qd, kd = m_ref[hd], qd_ref[hd], kd_ref[hd]
            cd = cd_ref[hd, 0:1, :]
            a = _dot(qv, kv, NT) * msk
            da = (_dot(dyv, vv, NT) * msk).astype(BF16)
            dq_intra = _dot(da, kv, NN)
            dk_intra = _dot(da, qv, TN)
            dq_inter = _dot(dyv, state.astype(BF16), NT) * qd
            dk_inter = _dot(vv, dstate_b, NT) * kd
            dv = _dot(a.astype(BF16), dyv, TN) + _dot((kf * kd).astype(BF16), dstate_b, NN)
            lam_sc[hd] += (qf * (w_ref[0] * dq_intra + w_ref[1] * dq_inter)
                           + kf * (w_ref[2] * dk_intra + w_ref[3] * dk_inter)
                           + (c * cd) * state * dstate)
            ds_sc[hd] = _dot((qf * qd).astype(BF16), dyv, TN) + cd * dstate
            dq = dq_intra + dq_inter
            dk = (dk_intra + dk_inter) * k_scale
            if prev is not None:
                dq = dq + pq_ref[:, osl]
                dk = dk + pk_ref[:, osl]
                dv = dv + pv_ref[:, osl]
            dq_ref[:, osl] = dq
            dk_ref[:, osl] = dk
            dv_ref[:, osl] = dv

        @pl.when(n == n_steps - 1)
        def _():
            for hd in range(heads):
                dl_ref[hd] = jnp.broadcast_to(jnp.sum(lam_sc[hd]), (SUBLANES, LANES))

    col = lambda off, gg: (lambda n: (rc(n), off // bw + gg))
    tab3 = lambda n: (0, 0, 0)
    ospec = pl.BlockSpec((c, rw), lambda n: (rc(n), 0))
    in_specs = [pl.BlockSpec((c, bw), col(off, gg)) for off in (q_off, k_off, v_off) for gg in range(ng)]
    in_specs += [pl.BlockSpec((c, rw), lambda n: (jnp.minimum(rc(n), nx - 1), 0)),
                 pl.BlockSpec((None, heads, HEAD_DIM, HEAD_DIM), lambda n: (step_of(n), 0, 0, 0)),
                 pl.BlockSpec((heads, c, c), tab3), pl.BlockSpec((heads, c, LANES), tab3), pl.BlockSpec((heads, c, LANES), tab3),
                 pl.BlockSpec((heads, SUBLANES, LANES), tab3), pl.BlockSpec((4, c, LANES), tab3)]
    args = [p] * (3 * ng) + [dy, states, mask, qdec, kdec, cdec, weights]
    aliases = {}
    if prev is not None:
        for k_out, arr in enumerate(prev):
            in_specs.append(ospec)
            args.append(arr)
            aliases[len(args) - 1] = k_out
    big = jax.ShapeDtypeStruct((t, rw), F32)
    return pl.pallas_call(
        body, name=name, grid=(n_steps,), in_specs=in_specs,
        out_specs=[ospec, ospec, ospec, pl.BlockSpec((heads, SUBLANES, LANES), tab3)],
        out_shape=[big, big, big, jax.ShapeDtypeStruct((heads, SUBLANES, LANES), F32)],
        scratch_shapes=[pltpu.VMEM((heads, HEAD_DIM, HEAD_DIM), F32), pltpu.VMEM((heads, HEAD_DIM, HEAD_DIM), F32)],
        input_output_aliases=aliases, compiler_params=_cp(("arbitrary",)))(*args)


def _ret_out(name, y, p, g_off, lx):
    rw = y.shape[1]
    bw = _tile(math.gcd(g_off, rw), 512)
    tr = _tile(lx, 512, SUBLANES)

    def body(y_ref, g_ref, o_ref):
        for hh in range(bw // HEAD_DIM):
            sl = slice(hh * HEAD_DIM, (hh + 1) * HEAD_DIM)
            yv = y_ref[:, sl]
            gv = g_ref[:, sl].astype(F32)
            rinv = lax.rsqrt(jnp.mean(yv * yv, axis=-1, keepdims=True) + NORM_EPS)
            o_ref[:, sl] = (gv * _sigmoid(gv) * yv * rinv).astype(o_ref.dtype)

    spec = pl.BlockSpec((tr, bw), lambda i, g: (i, g))
    return pl.pallas_call(
        body, name=name, grid=(lx // tr, rw // bw),
        in_specs=[spec, pl.BlockSpec((tr, bw), lambda i, g: (i, g_off // bw + g))],
        out_specs=spec, out_shape=jax.ShapeDtypeStruct((lx, rw), BF16),
        compiler_params=_cp(("parallel", "parallel")))(y, p)


def _ret_out_bwd(name, dyr, y, p, g_off, lx):
    rw = y.shape[1]
    bw = _tile(math.gcd(g_off, rw), 512)
    tr = _tile(lx, 512, SUBLANES)

    def body(d_ref, y_ref, g_ref, dy_ref, dg_ref):
        for hh in range(bw // HEAD_DIM):
            sl = slice(hh * HEAD_DIM, (hh + 1) * HEAD_DIM)
            yv = y_ref[:, sl]
            gv = g_ref[:, sl].astype(F32)
            dv = d_ref[:, sl]
            rinv = lax.rsqrt(jnp.mean(yv * yv, axis=-1, keepdims=True) + NORM_EPS)
            yn = yv * rinv
            sg = _sigmoid(gv)
            dg_ref[:, sl] = (dv * yn * sg * (1.0 + gv * (1.0 - sg))).astype(dg_ref.dtype)
            dyn = dv * gv * sg
            dy_ref[:, sl] = rinv * (dyn - yn * jnp.mean(dyn * yn, axis=-1, keepdims=True))

    spec = pl.BlockSpec((tr, bw), lambda i, g: (i, g))
    return pl.pallas_call(
        body, name=name, grid=(lx // tr, rw // bw),
        in_specs=[spec, spec, pl.BlockSpec((tr, bw), lambda i, g: (i, g_off // bw + g))],
        out_specs=[spec, spec],
        out_shape=[jax.ShapeDtypeStruct((lx, rw), F32), jax.ShapeDtypeStruct((lx, rw), BF16)],
        compiler_params=_cp(("parallel", "parallel")))(dyr, y, p)


def _merge(name, pa, pr, p, ga_off, gb_off):
    r, d = pa.shape
    cw = _tile(math.gcd(math.gcd(ga_off, gb_off), d), 1024)
    tr = _tile(r, 512, SUBLANES)

    def body(pa_ref, pr_ref, ga_ref, gb_ref, o_ref):
        o_ref[...] = (_sigmoid(ga_ref[...].astype(F32)) * pa_ref[...].astype(F32)
                      + _sigmoid(gb_ref[...].astype(F32)) * pr_ref[...].astype(F32)).astype(o_ref.dtype)

    spec = pl.BlockSpec((tr, cw), lambda i, j: (i, j))
    return pl.pallas_call(
        body, name=name, grid=(r // tr, d // cw),
        in_specs=[spec, spec, pl.BlockSpec((tr, cw), lambda i, j: (i, ga_off // cw + j)),
                  pl.BlockSpec((tr, cw), lambda i, j: (i, gb_off // cw + j))],
        out_specs=spec, out_shape=jax.ShapeDtypeStruct((r, d), BF16),
        compiler_params=_cp(("parallel", "parallel")))(pa, pr, p, p)


def _ada_fwd(name, cond, w):
    rows, d = cond.shape
    n = w.shape[1]
    tn = _tile(n, 768)

    def body(c_ref, w_ref, o_ref, s_ref):
        cv = c_ref[...]
        sv = cv * _sigmoid(cv)
        s_ref[...] = sv
        o_ref[...] = _dot(sv.astype(BF16), w_ref[...].astype(BF16), NN)

    return pl.pallas_call(
        body, name=name, grid=(n // tn,),
        in_specs=[pl.BlockSpec((rows, d), lambda j: (0, 0)), pl.BlockSpec((d, tn), lambda j: (0, j))],
        out_specs=[pl.BlockSpec((rows, tn), lambda j: (0, j)), pl.BlockSpec((rows, d), lambda j: (0, 0))],
        out_shape=[jax.ShapeDtypeStruct((rows, n), F32), jax.ShapeDtypeStruct((rows, d), F32)],
        compiler_params=_cp(("arbitrary",)))(cond, w)


def _ada_bwd(name, s_cond, dmod, w):
    rows, d = s_cond.shape
    n = w.shape[1]
    tn = _tile(n, 768)

    def body(s_ref, dm_ref, w_ref, gw_ref, ds_ref):
        j = pl.program_id(0)

        @pl.when(j == 0)
        def _():
            ds_ref[...] = jnp.zeros(ds_ref.shape, F32)

        dmv = dm_ref[...].astype(BF16)
        gw_ref[...] = _dot(s_ref[...].astype(BF16), dmv, TN)
        ds_ref[...] += _dot(dmv, w_ref[...].astype(BF16), NT)

    return pl.pallas_call(
        body, name=name, grid=(n // tn,),
        in_specs=[pl.BlockSpec((rows, d), lambda j: (0, 0)), pl.BlockSpec((rows, tn), lambda j: (0, j)),
                  pl.BlockSpec((d, tn), lambda j: (0, j))],
        out_specs=[pl.BlockSpec((d, tn), lambda j: (0, j)), pl.BlockSpec((rows, d), lambda j: (0, 0))],
        out_shape=[jax.ShapeDtypeStruct((d, n), F32), jax.ShapeDtypeStruct((rows, d), F32)],
        compiler_params=_cp(("arbitrary",)))(s_cond, dmod, w)


def _sum_slots(name, a):
    s, r, c = a.shape

    def body(a_ref, o_ref):
        acc = a_ref[0]
        for k in range(1, s):
            acc = acc + a_ref[k]
        o_ref[...] = acc

    return pl.pallas_call(
        body, name=name, grid=(1,), in_specs=[pl.BlockSpec((s, r, c), lambda i: (0, 0, 0))],
        out_specs=pl.BlockSpec((r, c), lambda i: (0, 0)), out_shape=jax.ShapeDtypeStruct((r, c), F32),
        compiler_params=_cp(("arbitrary",)))(a)


def _cctx_grad(name, parts, c_ctx):
    s, r, d = parts.shape

    def body(p_ref, c_ref, o_ref):
        acc = p_ref[0]
        for k in range(1, s):
            acc = acc + p_ref[k]
        cv = c_ref[...]
        sg = _sigmoid(cv)
        o_ref[...] = acc[0:1, :] * sg * (1.0 + cv * (1.0 - sg))

    return pl.pallas_call(
        body, name=name, grid=(1,),
        in_specs=[pl.BlockSpec((s, r, d), lambda i: (0, 0, 0)), pl.BlockSpec((1, d), lambda i: (0, 0))],
        out_specs=pl.BlockSpec((1, d), lambda i: (0, 0)), out_shape=jax.ShapeDtypeStruct((1, d), F32),
        compiler_params=_cp(("arbitrary",)))(parts, c_ctx)


def _adamw(name, slots, w, m, v):
    s, r, c = slots.shape
    tr = _tile(r, 256 if c > 1024 else 512, SUBLANES)
    c1 = 1.0 - ADAM_B1 ** ADAM_STEP
    c2 = 1.0 - ADAM_B2 ** ADAM_STEP

    def body(s_ref, w_ref, m_ref, v_ref, g_ref, d_ref, mo_ref, vo_ref):
        g = s_ref[0].astype(F32)
        for k in range(1, s):
            g = g + s_ref[k].astype(F32)
        mn = ADAM_B1 * m_ref[...] + (1.0 - ADAM_B1) * g
        vn = ADAM_B2 * v_ref[...] + (1.0 - ADAM_B2) * (g * g)
        m_hat = mn / c1
        v_hat = vn / c2
        g_ref[...] = g
        mo_ref[...] = mn
        vo_ref[...] = vn
        d_ref[...] = -ADAM_LR * (m_hat / (jnp.sqrt(v_hat) + ADAM_EPS) + ADAM_WD * w_ref[...])

    spec = pl.BlockSpec((tr, c), lambda i: (i, 0))
    shp = jax.ShapeDtypeStruct((r, c), F32)
    return pl.pallas_call(
        body, name=name, grid=(r // tr,),
        in_specs=[pl.BlockSpec((s, tr, c), lambda i: (0, i, 0)), spec, spec, spec],
        out_specs=[spec] * 4, out_shape=[shp] * 4, compiler_params=_cp(("parallel",)))(slots, w, m, v)


def _coords():
    return lax.axis_index("x"), lax.axis_index("y"), lax.axis_index("c")


def _flip(coord, bit):
    return 1 - coord if bit else coord


def _all_gather_small(name, blk):
    r, ccols = blk.shape

    def body(x_ref, out_ref, send_sems, recv_sems, local_sem):
        x, y, c = _coords()
        me, sibling = (x, y, c), (x, y, 1 - c)
        chips = [(1 - x, y), (x, 1 - y), (1 - x, 1 - y)]

        def slot(px, py, pc):
            return out_ref.at[4 * px + 2 * py + pc]

        def copy(k, block, to, src=None):
            return pltpu.make_async_remote_copy(
                src_ref=slot(*block) if src is None else src, dst_ref=slot(*block),
                send_sem=send_sems.at[k], recv_sem=recv_sems.at[k], device_id=to, device_id_type=MESH)

        mine = pltpu.make_async_copy(x_ref, slot(*me), local_sem)
        mine.start()
        first = [copy(0, me, sibling, src=x_ref)]
        first += [copy(1 + j, me, (*chip, c), src=x_ref) for j, chip in enumerate(chips)]
        for cp in first:
            cp.start()
        passed = [copy(4 + j, (*chip, c), sibling) for j, chip in enumerate(chips)]
        for j, chip in enumerate(chips):
            copy(1 + j, (*chip, c), me).wait_recv()
            passed[j].start()
        copy(0, sibling, me).wait_recv()
        for j, chip in enumerate(chips):
            copy(4 + j, (*chip, 1 - c), me).wait_recv()
        for cp in first + passed:
            cp.wait_send()
        mine.wait()

    return pl.pallas_call(
        body, name=name, out_shape=jax.ShapeDtypeStruct((N_DEV, r, ccols), blk.dtype),
        in_specs=[pl.BlockSpec(memory_space=pltpu.VMEM)], out_specs=pl.BlockSpec(memory_space=pltpu.VMEM),
        scratch_shapes=[pltpu.SemaphoreType.DMA((7,)), pltpu.SemaphoreType.DMA((7,)), pltpu.SemaphoreType.DMA],
    )(blk)


def _comm_semaphores(n):
    return [pltpu.SemaphoreType.DMA((7 * n,)), pltpu.SemaphoreType.DMA((7 * n,)), pltpu.SemaphoreType.DMA((n,))]


def _comm_out_shape(kind, s, wide):
    if kind == "gather":
        shape = (s.shape[0], N_DEV * s.shape[1]) if wide else (N_DEV,) + s.shape
    else:
        shape = (N_DEV, s.shape[0], s.shape[1] // N_DEV) if wide else s.shape
    return jax.ShapeDtypeStruct(shape, s.dtype)


def _block_of(ref, idx, wide, cols):
    if not wide:
        return ref.at[idx]
    return ref.at[:, pl.ds(pl.multiple_of(idx * cols, LANES), cols)]


def _gather_plan(ins, outs, wide, send_sems, recv_sems, local_sems):
    n = len(ins)
    x, y, c = _coords()
    me, sibling = (x, y, c), (x, y, 1 - c)
    chips = [(1 - x, y), (x, 1 - y), (1 - x, 1 - y)]

    def slot(a, px, py, pc):
        return _block_of(outs[a], 4 * px + 2 * py + pc, wide[a], ins[a].shape[-1])

    def copy(a, k, block, to, src=None):
        return pltpu.make_async_remote_copy(
            src_ref=slot(a, *block) if src is None else src, dst_ref=slot(a, *block),
            send_sem=send_sems.at[7 * a + k], recv_sem=recv_sems.at[7 * a + k], device_id=to, device_id_type=MESH)

    def local(a):
        return pltpu.make_async_copy(ins[a], slot(a, *me), local_sems.at[a])

    def first(a):
        return [copy(a, 0, me, sibling, src=ins[a])] + [copy(a, 1 + j, me, (*chip, c), src=ins[a])
                                                        for j, chip in enumerate(chips)]

    def passed(a, j):
        return copy(a, 4 + j, (*chips[j], c), sibling)

    def start():
        for a in range(n):
            local(a).start()
            for cp in first(a):
                cp.start()

    def pass_on():
        for a in range(n):
            for j, chip in enumerate(chips):
                copy(a, 1 + j, (*chip, c), me).wait_recv()
                passed(a, j).start()

    def finish():
        for a in range(n):
            copy(a, 0, sibling, me).wait_recv()
            for j, chip in enumerate(chips):
                copy(a, 4 + j, (*chip, 1 - c), me).wait_recv()
        for a in range(n):
            for cp in first(a) + [passed(a, j) for j in range(3)]:
                cp.wait_send()
            local(a).wait()

    return start, pass_on, finish


def _exchange_plan(ins, outs, wide, send_sems, recv_sems, local_sems):
    n = len(ins)
    x, y, c = _coords()
    my_idx = 4 * x + 2 * y + c

    def src(a, idx):
        return _block_of(ins[a], idx, wide[a], outs[a].shape[-1])

    def local(a):
        return pltpu.make_async_copy(src(a, my_idx), outs[a].at[my_idx], local_sems.at[a])

    def pair(a, rel):
        px, py, pc = _flip(x, rel & 4), _flip(y, rel & 2), _flip(c, rel & 1)
        peer_idx = 4 * px + 2 * py + pc
        sems = dict(send_sem=send_sems.at[7 * a + rel - 1], recv_sem=recv_sems.at[7 * a + rel - 1],
                    device_id=(px, py, pc), device_id_type=MESH)
        send = pltpu.make_async_remote_copy(src_ref=src(a, peer_idx), dst_ref=outs[a].at[my_idx], **sems)
        recv = pltpu.make_async_remote_copy(src_ref=src(a, my_idx), dst_ref=outs[a].at[peer_idx], **sems)
        return send, recv

    def start():
        for a in range(n):
            local(a).start()
            for rel in range(1, N_DEV):
                pair(a, rel)[0].start()

    def finish():
        for a in range(n):
            for rel in range(1, N_DEV):
                send, recv = pair(a, rel)
                recv.wait_recv()
                send.wait_send()
            local(a).wait()

    return start, None, finish


def _comm_plan(kind, ins, outs, wide, send_sems, recv_sems, local_sems):
    plan = {"gather": _gather_plan, "exchange": _exchange_plan}[kind]
    return plan(ins, outs, wide, send_sems, recv_sems, local_sems)


def _rope_tables(lx, lc):
    rows = lx // GRID_W
    row = jnp.repeat(jnp.arange(rows, dtype=F32), GRID_W)
    col = jnp.tile(jnp.arange(GRID_W, dtype=F32), rows)
    half = HEAD_DIM // 2
    inv_freq = ROPE_THETA ** (-jnp.arange(0, half, 2, dtype=F32) / half)
    ang = jnp.concatenate([row[:, None] * inv_freq, col[:, None] * inv_freq], axis=-1)
    cos, sin = jnp.cos(ang), jnp.sin(ang)
    cosf = jnp.repeat(cos, 2, axis=-1)
    sinf = jnp.stack([-sin, sin], axis=-1).reshape(lx, HEAD_DIM)
    cosf = jnp.concatenate([cosf, jnp.ones((lc, HEAD_DIM), F32)], axis=0)
    sinf = jnp.concatenate([sinf, jnp.zeros((lc, HEAD_DIM), F32)], axis=0)
    return cosf, sinf


def _cols_full(g):
    return jnp.transpose(g, (1, 0, 2)).reshape(g.shape[1], N_DEV * g.shape[2])


def _cols_split(w):
    k, n = w.shape
    return jnp.transpose(w.reshape(k, N_DEV, n // N_DEV), (1, 0, 2))


def _pad_rows(a, rows):
    return jnp.pad(a, ((0, rows - a.shape[0]), (0, 0)))


def kernel(x, c, ctx, c_ctx, w_ada, b_ada, ffn1_w_in, ffn1_w_out, mix_w_in, attn_q_gain, attn_k_gain, ret_decay_logit, w_proj_attn, w_proj_ret, mix_w_out, ffn2_w_in, ffn2_w_out, final_norm, loss_target, m_c_ctx, m_w_ada, m_b_ada, m_ffn1_w_in, m_ffn1_w_out, m_mix_w_in, m_attn_q_gain, m_attn_k_gain, m_ret_decay_logit, m_w_proj_attn, m_w_proj_ret, m_mix_w_out, m_ffn2_w_in, m_ffn2_w_out, m_final_norm, v_c_ctx, v_w_ada, v_b_ada, v_ffn1_w_in, v_ffn1_w_out, v_mix_w_in, v_attn_q_gain, v_attn_k_gain, v_ret_decay_logit, v_w_proj_attn, v_w_proj_ret, v_mix_w_out, v_ffn2_w_in, v_ffn2_w_out, v_final_norm):
    lx, d = x.shape[1], x.shape[2]
    lc = ctx.shape[1]
    t = lx + lc
    aw, rw = w_proj_attn.shape[1], w_proj_ret.shape[1]
    pw = mix_w_in.shape[2] * N_DEV
    kw = (pw - aw - 4 * rw - 2 * d) // 2
    groups = aw // kw
    heads_r = rw // HEAD_DIM
    ka_off, va_off = aw, aw + kw
    qr_off = aw + 2 * kw
    kr_off, vr_off, gr_off = qr_off + rw, qr_off + 2 * rw, qr_off + 3 * rw
    ga_off, gb_off = qr_off + 4 * rw, qr_off + 4 * rw + d
    xi, yi, ci = _coords()
    me = 4 * xi + 2 * yi + ci

    col_names = ["ffn1_w_in", "mix_w_in", "w_proj_attn", "w_proj_ret", "ffn2_w_in"]
    row_names = ["ffn1_w_out", "mix_w_out", "ffn2_w_out"]
    shard = dict(ffn1_w_in=ffn1_w_in[0], mix_w_in=mix_w_in[0], w_proj_attn=w_proj_attn[0], w_proj_ret=w_proj_ret[0],
                 ffn2_w_in=ffn2_w_in[0], ffn1_w_out=ffn1_w_out[0], mix_w_out=mix_w_out[0], ffn2_w_out=ffn2_w_out[0])
    names = col_names + row_names
    bf_shard = {k: shard[k].astype(BF16) for k in names}
    full, landed = {}, {}

    def is_wide(k):
        return k in col_names and shard[k].shape[1] % LANES == 0

    def gather_of(keys):
        return "gather", [bf_shard[k] for k in keys], [is_wide(k) for k in keys]

    def keep_full(keys, gathered):
        for k, g in zip(keys, gathered):
            if is_wide(k):
                full[k] = g
            else:
                full[k] = _cols_full(g) if k in col_names else g.reshape(N_DEV * g.shape[1], g.shape[2])

    def exchange_of(grads):
        blocks = [g if is_wide(k) else _cols_split(g) if k in col_names
                  else g.reshape(N_DEV, g.shape[0] // N_DEV, g.shape[1]) for k, g in grads.items()]
        return "exchange", blocks, [is_wide(k) for k in grads]

    def keep_landed(grads, got):
        landed.update(zip(grads.keys(), got))


    c_all = _all_gather_small("gather_cond", _pad_rows(c, SUBLANES))[:, 0, :]
    cond = _pad_rows(jnp.concatenate([c_all, c_ctx[None, :]], axis=0), 2 * SUBLANES)
    ada_part, s_cond = _ada_fwd("ada_fwd", cond, w_ada[0])
    ada_all = _all_gather_small("gather_ada", ada_part)
    mod_all = jnp.transpose(ada_all, (1, 0, 2)).reshape(2 * SUBLANES, N_MOD * d) + b_ada
    mod_x = lax.dynamic_index_in_dim(mod_all, me, axis=0, keepdims=False).reshape(N_MOD, d)
    mod_c = mod_all[N_DEV].reshape(N_MOD, d)
    mods = jnp.stack([mod_c, mod_x], axis=0)[:, :, None, :]
    sh1, sc1, g1, sh2, sc2, g2, sh3, sc3, g3 = [mods[:, k] for k in range(N_MOD)]

    h0 = jnp.concatenate([x[0], ctx[0]], axis=0)
    u1, got = _rmsmod("ffn1_norm", h0, sc1, sh1, lx, comm=gather_of(["ffn1_w_in"]))
    keep_full(["ffn1_w_in"], got)
    keys = ["ffn1_w_out", "mix_w_in"]
    (z1, s1), got = _ffn_in("ffn1_in", u1, full["ffn1_w_in"], comm=gather_of(keys))
    keep_full(keys, got)
    keys = ["w_proj_attn", "w_proj_ret", "mix_w_out"]
    (h1, f1), got = _mm_residual("ffn1_out", s1, full["ffn1_w_out"], h0, g1, 0.5, lx, comm=gather_of(keys))
    keep_full(keys, got)

    u2 = _rmsmod("mix_norm", h1, sc2, sh2, lx)
    keys = ["ffn2_w_in"]
    p, got = _mm_nn("mix_in", u2, full["mix_w_in"], BF16, tm_pref=M_TILE_BIG, tn_pref=512, comm=gather_of(keys))
    keep_full(keys, got)
    cosf, sinf = _rope_tables(lx, lc)
    q_rot, k_rot = _qk_prep("qk_prep", p, cosf, sinf, attn_q_gain, attn_k_gain, aw, kw)
    ya, lse = _attn_fwd("attn_fwd", q_rot, k_rot, p, va_off, lx, groups)

    decay = ret_decay_logit[0].astype(F32)
    log_gamma = jax.nn.log_sigmoid(decay)
    r_offs = (qr_off, kr_off, vr_off, rw)
    tab = [_ret_tables(log_gamma[k], k) for k in range(2)]
    y_f, st_f = _ret_fwd("ret_fwd_a", p, r_offs, tab[0], 0, lx, None)
    y_r, st_b = _ret_fwd("ret_fwd_b", p, r_offs, tab[1], 1, lx, y_f)
    yr = _ret_out("ret_out", y_r, p, gr_off, lx)

    pa = _mm_nn("proj_attn", ya, full["w_proj_attn"], BF16)
    pr = _mm_nn("proj_ret", yr, full["w_proj_ret"], BF16)
    mg = _merge("merge", pa, pr, p, ga_off, gb_off)
    h2, o2 = _mm_residual("mix_out", mg, full["mix_w_out"], h1, g2, 1.0, lx)

    u3 = _rmsmod("ffn2_norm", h2, sc3, sh3, lx)
    keys = ["ffn2_w_out"]
    (z3, s3), got = _ffn_in("ffn2_in", u3, full["ffn2_w_in"], comm=gather_of(keys))
    keep_full(keys, got)
    h3, f3 = _mm_residual("ffn2_out", s3, full["ffn2_w_out"], h2, g3, 0.5, lx)
    dh3, d_fn, loss_tile, df3, dg3 = _final_loss("final_loss", h3, final_norm[None, :], loss_target[0], f3, g3, 0.5)

    dz3 = _ffn_out_bwd("ffn2_out_bwd", df3, full["ffn2_w_out"], z3)
    grads = {"ffn2_w_out": _mm_tn("ffn2_out_dw", s3, df3, BF16, tm_pref=M_TILE_BIG)}
    g_ffn2_w_in, got = _ffn_in_bwd_w("ffn2_in_dw", u3, dz3, comm=exchange_of(grads))
    keep_landed(grads, got)
    grads = {"ffn2_w_in": g_ffn2_w_in}
    du3, got = _ffn_in_bwd_x("ffn2_in_bwd", dz3, full["ffn2_w_in"], comm=exchange_of(grads))
    keep_landed(grads, got)
    dh2, dsc3, dsh3, do2, dg2 = _rmsmod_bwd("ffn2_norm_bwd", du3, h2, sc3, dh3, lx, lx, gate=(o2, g2, 1.0))

    dpa, dpr, dga, dgb = _merge_bwd("merge_bwd", do2, full["mix_w_out"], pa, pr, p, ga_off, gb_off)
    dya = _mm_nt("proj_attn_bwd", dpa, full["w_proj_attn"], BF16)
    dyr = _mm_nt("proj_ret_bwd", dpr, full["w_proj_ret"], F32)
    mix_grads = {"mix_w_out": _mm_tn("mix_out_dw", mg, do2, BF16),
                 "w_proj_attn": _mm_tn("proj_attn_dw", ya, dpa, BF16),
                 "w_proj_ret": _mm_tn("proj_ret_dw", yr, dpr, BF16)}

    dy_ret, dgr = _ret_out_bwd("ret_out_bwd", dyr, y_r, p, gr_off, lx)
    dqr, dkr, dvr, dl_f = _ret_bwd("ret_bwd_a", p, r_offs, tab[0], st_f, dy_ret, 0, lx, None)
    dqr, dkr, dvr, dl_b = _ret_bwd("ret_bwd_b", p, r_offs, tab[1], st_b, dy_ret, 1, lx, (dqr, dkr, dvr))
    d_lam = jnp.stack([dl_f[:, 0, 0], dl_b[:, 0, 0]], axis=0)
    d_decay = d_lam * jax.nn.sigmoid(-decay)

    delta = _attn_delta("attn_delta", ya, dya)
    dq_rot, dk_rot, dva = _attn_bwd("attn_bwd", q_rot, k_rot, p, va_off, dya, lse, delta, lx, groups)
    dqa, dka, d_qg, d_kg = _qk_prep_bwd("qk_prep_bwd", dq_rot, dk_rot, p, cosf, sinf, attn_q_gain, attn_k_gain, aw, kw)

    dp = jnp.concatenate([dqa, dka, dva, dqr.astype(BF16), dkr.astype(BF16), dvr.astype(BF16),
                          _pad_rows(dgr, t), _pad_rows(dga, t), _pad_rows(dgb, t)], axis=1)
    grads = {"mix_w_in": _mm_tn("mix_in_dw", u2, dp, BF16, tm_pref=M_TILE_DW, tn_pref=512)}
    du2, got = _mm_nt("mix_in_bwd", dp, full["mix_w_in"], F32, comm=exchange_of(grads))
    keep_landed(grads, got)
    dh1, dsc2, dsh2, df1, dg1 = _rmsmod_bwd("mix_norm_bwd", du2, h1, sc2, dh2, lx, t, gate=(f1, g1, 0.5))

    dz1, got = _ffn_out_bwd("ffn1_out_bwd", df1, full["ffn1_w_out"], z1, comm=exchange_of(mix_grads))
    keep_landed(mix_grads, got)
    grads = {"ffn1_w_out": _mm_tn("ffn1_out_dw", s1, df1, BF16, tm_pref=M_TILE_BIG)}
    g_ffn1_w_in, got = _ffn_in_bwd_w("ffn1_in_dw", u1, dz1, comm=exchange_of(grads))
    keep_landed(grads, got)
    grads = {"ffn1_w_in": g_ffn1_w_in}
    du1, got = _ffn_in_bwd_x("ffn1_in_bwd", dz1, full["ffn1_w_in"], comm=exchange_of(grads))
    keep_landed(grads, got)
    grad_x, dsc1, dsh1 = _rmsmod_bwd("ffn1_norm_bwd", du1, h0, sc1, dh1, lx, lx)

    zero = jnp.zeros((1, d), F32)
    dmod_c = jnp.concatenate([dsh1[0], dsc1[0], dg1[0], dsh2[0], dsc2[0], zero, zero, zero, zero], axis=0)
    dmod_x = jnp.concatenate([dsh1[1], dsc1[1], dg1[1], dsh2[1], dsc2[1], dg2[1], dsh3[1], dsc3[1], dg3], axis=0)
    misc = jnp.concatenate([d_qg[0], d_kg[0], d_decay.reshape(-1), loss_tile[0, 0:1]])
    misc = jnp.pad(misc, (0, d - misc.shape[0]))[None, :]
    n_small = 3 * SUBLANES
    small = _pad_rows(jnp.concatenate([dmod_c, dmod_x, d_fn, misc], axis=0), n_small)
    small_all = _all_gather_small("gather_small", small)
    small_sum = _sum_slots("sum_small", small_all)
    dmod_c_sum, dmod_x_sum = small_sum[0:N_MOD], small_sum[N_MOD:2 * N_MOD]
    g_final_norm = small_sum[2 * N_MOD]
    misc_sum = small_sum[2 * N_MOD + 1]
    g_qg = misc_sum[0:HEAD_DIM][None, :]
    g_kg = misc_sum[HEAD_DIM:2 * HEAD_DIM][None, :]
    g_decay = misc_sum[2 * HEAD_DIM:2 * HEAD_DIM + 2 * heads_r].reshape(1, 2, heads_r)
    loss = misc_sum[2 * HEAD_DIM + 2 * heads_r]
    g_b_ada = (dmod_x_sum + dmod_c_sum).reshape(1, N_MOD * d)

    n_ada = w_ada.shape[2]
    dmod_rows = jnp.concatenate([small_all[:, N_MOD:2 * N_MOD, :].reshape(N_DEV, N_MOD * d),
                                 dmod_c_sum.reshape(1, N_MOD * d)], axis=0)
    dmod_mine = _pad_rows(lax.dynamic_slice_in_dim(dmod_rows, me * n_ada, n_ada, axis=1), 2 * SUBLANES)
    g_w_ada, ds_cond = _ada_bwd("ada_bwd", s_cond, dmod_mine, w_ada[0])
    cctx_parts = _all_gather_small("gather_cctx", ds_cond[N_DEV:N_DEV + SUBLANES])
    g_c_ctx = _cctx_grad("cctx_grad", cctx_parts, c_ctx[None, :])[0]

    mom = dict(ffn1_w_in=(m_ffn1_w_in, v_ffn1_w_in), mix_w_in=(m_mix_w_in, v_mix_w_in),
               w_proj_attn=(m_w_proj_attn, v_w_proj_attn), w_proj_ret=(m_w_proj_ret, v_w_proj_ret),
               ffn2_w_in=(m_ffn2_w_in, v_ffn2_w_in), ffn1_w_out=(m_ffn1_w_out, v_ffn1_w_out),
               mix_w_out=(m_mix_w_out, v_mix_w_out), ffn2_w_out=(m_ffn2_w_out, v_ffn2_w_out))
    res = {}
    for k in names:
        res[k] = _adamw("adamw_" + k, landed[k], shard[k], mom[k][0][0], mom[k][1][0])
    res["w_ada"] = _adamw("adamw_w_ada", g_w_ada[None], w_ada[0], m_w_ada[0], v_w_ada[0])

    def pack(cc, ba, qg, kg, dec, fn):
        misc_row = jnp.concatenate([qg.reshape(-1), kg.reshape(-1), dec.reshape(-1)])
        misc_row = jnp.pad(misc_row, (0, d - misc_row.shape[0]))[None, :]
        return _pad_rows(jnp.concatenate([cc.reshape(1, d), ba.reshape(N_MOD, d), fn.reshape(1, d), misc_row], axis=0),
                         2 * SUBLANES)

    sg, sd, sm, sv = _adamw(
        "adamw_small", pack(g_c_ctx, g_b_ada, g_qg, g_kg, g_decay, g_final_norm)[None],
        pack(c_ctx, b_ada, attn_q_gain, attn_k_gain, ret_decay_logit, final_norm),
        pack(m_c_ctx, m_b_ada, m_attn_q_gain, m_attn_k_gain, m_ret_decay_logit, m_final_norm),
        pack(v_c_ctx, v_b_ada, v_attn_q_gain, v_attn_k_gain, v_ret_decay_logit, v_final_norm))

    def unpack(a):
        misc_row = a[N_MOD + 2]
        return dict(c_ctx=a[0], b_ada=a[1:1 + N_MOD].reshape(1, N_MOD * d), final_norm=a[N_MOD + 1],
                    attn_q_gain=misc_row[0:HEAD_DIM][None, :], attn_k_gain=misc_row[HEAD_DIM:2 * HEAD_DIM][None, :],
                    ret_decay_logit=misc_row[2 * HEAD_DIM:2 * HEAD_DIM + 2 * heads_r].reshape(1, 2, heads_r))

    small_out = [unpack(a) for a in (sg, sd, sm, sv)]
    order = ["c_ctx", "w_ada", "b_ada", "ffn1_w_in", "ffn1_w_out", "mix_w_in", "attn_q_gain", "attn_k_gain",
             "ret_decay_logit", "w_proj_attn", "w_proj_ret", "mix_w_out", "ffn2_w_in", "ffn2_w_out", "final_norm"]
    outs = [loss, grad_x[None]]
    for which in range(4):
        for k in order:
            outs.append(res[k][which][None] if k in res else small_out[which][k])
    return tuple(outs)
```

```python
import math

import jax
import jax.numpy as jnp
from jax import lax
from jax.experimental import pallas as pl
from jax.experimental.pallas import tpu as pltpu

F32 = jnp.float32
BF16 = jnp.bfloat16
MESH = pl.DeviceIdType.MESH

N_DEV = 8
HEAD_DIM = 128
GRID_W = 64
ROPE_THETA = 10000.0
NORM_EPS = 1e-6
RET_CHUNK = 128
N_MOD = 9
LANES = 128
SUBLANES = 8
V7X_VMEM_BYTES = 64 * 1024 * 1024
VMEM_LIMIT = V7X_VMEM_BYTES - 8 * 1024 * 1024
K_TILE = 2560
K_TILE_LONG = 2816
M_TILE_BIG = 1408
M_TILE_DW = 2048
LOG2E = 1.4426950408889634
LN2 = 0.6931471805599453

ADAM_LR = 0.001
ADAM_B1 = 0.9
ADAM_B2 = 0.999
ADAM_EPS = 1e-08
ADAM_WD = 0.01
ADAM_STEP = 10

NN = (((1,), (0,)), ((), ()))
NT = (((1,), (1,)), ((), ()))
TN = (((0,), (0,)), ((), ()))


def _tile(n, pref, align=LANES):
    best = None
    t = align
    while t <= min(n, pref):
        if n % t == 0:
            best = t
        t += align
    return n if best is None else best


def _cp(sem):
    return pltpu.CompilerParams(dimension_semantics=sem, vmem_limit_bytes=VMEM_LIMIT)


def _sigmoid(v):
    return 0.5 * jnp.tanh(0.5 * v) + 0.5


def _dot(a, b, dims):
    return lax.dot_general(a, b, dims, preferred_element_type=F32)


def _mm(name, a, a_spec, b_list, dims, grid, out_shapes, out_specs, acc_shape, epi, extras=(), comm=None):
    nb, ne, no = len(b_list), len(extras), len(out_shapes)
    nk = grid[2]
    kind, c_arrays, c_wide = comm if comm is not None else (None, [], [])
    ncm = len(c_arrays)

    def body(*refs):
        a_ref = refs[0]
        b_refs = refs[1:1 + nb]
        e_refs = refs[1 + nb:1 + nb + ne]
        pos = 1 + nb + ne
        c_ins = refs[pos:pos + ncm]
        o_refs = refs[pos + ncm:pos + ncm + no]
        c_outs = refs[pos + ncm + no:pos + 2 * ncm + no]
        scratch = refs[pos + 2 * ncm + no:]
        accs = scratch[:0 if nk == 1 else nb]
        ids = [pl.program_id(axis) for axis in range(3)]
        if ncm:
            start_comm, mid_comm, finish_comm = _comm_plan(kind, c_ins, c_outs, c_wide, *scratch[len(accs):])
            step = (ids[0] * grid[1] + ids[1]) * nk + ids[2]
            pl.when(step == 0)(start_comm)
            if mid_comm is not None:
                pl.when(step == (3 * grid[0] * grid[1] * nk) // 4)(mid_comm)

        def finish(tiles):
            vals = epi(tiles, e_refs)
            for o_ref, v in zip(o_refs, vals):
                if isinstance(v, tuple):
                    for idx, part in enumerate(v):
                        o_ref[idx] = part.astype(o_ref.dtype)
                else:
                    o_ref[...] = v.astype(o_ref.dtype)

        if nk == 1:
            finish([_dot(a_ref[...], b_ref[...], dims) for b_ref in b_refs])
        else:
            @pl.when(ids[2] == 0)
            def _():
                for acc in accs:
                    acc[...] = jnp.zeros(acc.shape, F32)

            av = a_ref[...]
            for b_ref, acc in zip(b_refs, accs):
                acc[...] += _dot(av, b_ref[...], dims)

            @pl.when(ids[2] == nk - 1)
            def _():
                finish([acc[...] for acc in accs])

        if ncm:
            pl.when(jnp.logical_and(jnp.logical_and(ids[0] == grid[0] - 1, ids[1] == grid[1] - 1),
                                    ids[2] == nk - 1))(finish_comm)

    any_spec = pl.BlockSpec(memory_space=pl.ANY)
    c_shapes = [_comm_out_shape(kind, s, w) for s, w in zip(c_arrays, c_wide)]
    scratch_shapes = [] if nk == 1 else [pltpu.VMEM(acc_shape, F32)] * nb
    if ncm:
        scratch_shapes = scratch_shapes + _comm_semaphores(ncm)
    semantics = ("arbitrary",) * 3 if ncm else ("parallel", "parallel", "arbitrary")
    outs = pl.pallas_call(
        body, name=name, grid=grid,
        in_specs=[a_spec] + [s for _, s in b_list] + [s for _, s in extras] + [any_spec] * ncm,
        out_specs=list(out_specs) + [any_spec] * ncm, out_shape=list(out_shapes) + c_shapes,
        scratch_shapes=scratch_shapes, compiler_params=_cp(semantics),
    )(a, *[b for b, _ in b_list], *[e for e, _ in extras], *c_arrays)
    main = outs[0] if no == 1 else tuple(outs[:no])
    return main if comm is None else (main, list(outs[no:]))


def _plain(accs, _):
    return (accs[0],)


def _mm_nn(name, a, b, out_dtype, tm_pref=1024, tn_pref=1024, tk_pref=K_TILE, comm=None):
    m, k = a.shape
    n = b.shape[1]
    tm, tn, tk = _tile(m, tm_pref), _tile(n, tn_pref), _tile(k, tk_pref)
    return _mm(name, a, pl.BlockSpec((tm, tk), lambda i, j, kk: (i, kk)),
               [(b, pl.BlockSpec((tk, tn), lambda i, j, kk: (kk, j)))], NN, (m // tm, n // tn, k // tk),
               [jax.ShapeDtypeStruct((m, n), out_dtype)], [pl.BlockSpec((tm, tn), lambda i, j, kk: (i, j))],
               (tm, tn), _plain, comm=comm)


def _mm_nt(name, a, b, out_dtype, tm_pref=1024, tn_pref=1024, tk_pref=K_TILE, comm=None):
    m, k = a.shape
    n = b.shape[0]
    tm, tn, tk = _tile(m, tm_pref), _tile(n, tn_pref), _tile(k, tk_pref)
    return _mm(name, a, pl.BlockSpec((tm, tk), lambda i, j, kk: (i, kk)),
               [(b, pl.BlockSpec((tn, tk), lambda i, j, kk: (j, kk)))], NT, (m // tm, n // tn, k // tk),
               [jax.ShapeDtypeStruct((m, n), out_dtype)], [pl.BlockSpec((tm, tn), lambda i, j, kk: (i, j))],
               (tm, tn), _plain, comm=comm)


def _mm_tn(name, a, b, out_dtype, rows=None, tm_pref=1024, tn_pref=1024, tk_pref=K_TILE, comm=None):
    k = a.shape[0] if rows is None else rows
    m, n = a.shape[1], b.shape[1]
    tm, tn, tk = _tile(m, tm_pref), _tile(n, tn_pref), _tile(k, tk_pref)
    return _mm(name, a, pl.BlockSpec((tk, tm), lambda i, j, kk: (kk, i)),
               [(b, pl.BlockSpec((tk, tn), lambda i, j, kk: (kk, j)))], TN, (m // tm, n // tn, k // tk),
               [jax.ShapeDtypeStruct((m, n), out_dtype)], [pl.BlockSpec((tm, tn), lambda i, j, kk: (i, j))],
               (tm, tn), _plain, comm=comm)


def _ffn_in(name, u, w_in, comm=None):
    r, d = u.shape
    f = w_in.shape[1] // 2
    tm, tn, tk = _tile(r, M_TILE_BIG), _tile(f, 512), _tile(d, K_TILE)
    nf = f // tn

    def epi(accs, _):
        za, zb = accs
        s = za * _sigmoid(za) * zb
        return (za, zb), s

    return _mm(name, u, pl.BlockSpec((tm, tk), lambda i, j, kk: (i, kk)),
               [(w_in, pl.BlockSpec((tk, tn), lambda i, j, kk: (kk, j))),
                (w_in, pl.BlockSpec((tk, tn), lambda i, j, kk: (kk, j + nf)))],
               NN, (r // tm, nf, d // tk),
               [jax.ShapeDtypeStruct((2, r, f), BF16), jax.ShapeDtypeStruct((r, f), BF16)],
               [pl.BlockSpec((2, tm, tn), lambda i, j, kk: (0, i, j)), pl.BlockSpec((tm, tn), lambda i, j, kk: (i, j))],
               (tm, tn), epi, comm=comm)


def _mm_residual(name, a, w, res, gate, gate_scale, lx, comm=None):
    r, k = a.shape
    n = w.shape[1]
    tm, tn, tk = _tile(r, 1024), _tile(n, 1024), _tile(k, K_TILE_LONG)

    def epi(accs, e_refs):
        res_ref, g_ref = e_refs
        rows = pl.program_id(0) * tm + lax.broadcasted_iota(jnp.int32, (tm, 1), 0)
        g = jnp.where(rows < lx, g_ref[1], g_ref[0])
        return res_ref[...] + gate_scale * g * accs[0], accs[0]

    return _mm(name, a, pl.BlockSpec((tm, tk), lambda i, j, kk: (i, kk)),
               [(w, pl.BlockSpec((tk, tn), lambda i, j, kk: (kk, j)))], NN, (r // tm, n // tn, k // tk),
               [jax.ShapeDtypeStruct((r, n), F32), jax.ShapeDtypeStruct((r, n), BF16)],
               [pl.BlockSpec((tm, tn), lambda i, j, kk: (i, j))] * 2, (tm, tn), epi,
               extras=[(res, pl.BlockSpec((tm, tn), lambda i, j, kk: (i, j))),
                       (gate, pl.BlockSpec((2, 1, tn), lambda i, j, kk: (0, 0, j)))], comm=comm)


def _ffn_out_bwd(name, df, w_out, z, comm=None):
    r, d = df.shape
    f = w_out.shape[0]
    tm, tn, tk = _tile(r, M_TILE_BIG), _tile(f, 512), _tile(d, K_TILE)

    def epi(accs, e_refs):
        ds = accs[0]
        za = e_refs[0][0].astype(F32)
        zb = e_refs[0][1].astype(F32)
        sg = _sigmoid(za)
        da = ds * zb * sg * (1.0 + za * (1.0 - sg))
        db = ds * za * sg
        return ((da, db),)

    zspec = pl.BlockSpec((2, tm, tn), lambda i, j, kk: (0, i, j))
    return _mm(name, df, pl.BlockSpec((tm, tk), lambda i, j, kk: (i, kk)),
               [(w_out, pl.BlockSpec((tn, tk), lambda i, j, kk: (j, kk)))], NT, (r // tm, f // tn, d // tk),
               [jax.ShapeDtypeStruct((2, r, f), BF16)], [zspec], (tm, tn), epi, extras=[(z, zspec)], comm=comm)


def _ffn_in_bwd_x(name, dz, w_in, comm=None):
    _, r, f = dz.shape
    d = w_in.shape[0]
    tm, tn, tk = _tile(r, 1024), _tile(d, 2048), _tile(f, K_TILE)
    nkf = f // tk
    return _mm(name, dz, pl.BlockSpec((None, tm, tk), lambda i, j, kk: (kk // nkf, i, kk % nkf)),
               [(w_in, pl.BlockSpec((tn, tk), lambda i, j, kk: (j, kk)))], NT, (r // tm, d // tn, 2 * nkf),
               [jax.ShapeDtypeStruct((r, d), F32)], [pl.BlockSpec((tm, tn), lambda i, j, kk: (i, j))],
               (tm, tn), _plain, comm=comm)


def _ffn_in_bwd_w(name, u, dz, comm=None):
    r, d = u.shape
    f = dz.shape[2]
    tm, tn, tk = _tile(d, M_TILE_DW), _tile(f, 512), _tile(r, K_TILE)
    nf = f // tn
    return _mm(name, u, pl.BlockSpec((tk, tm), lambda i, j, kk: (kk, i)),
               [(dz, pl.BlockSpec((None, tk, tn), lambda i, j, kk: (j // nf, kk, j % nf)))], TN,
               (d // tm, 2 * nf, r // tk),
               [jax.ShapeDtypeStruct((d, 2 * f), BF16)], [pl.BlockSpec((tm, tn), lambda i, j, kk: (i, j))],
               (tm, tn), _plain, comm=comm)


def _merge_bwd(name, dout, w_out, pa, pr, p, ga_off, gb_off):
    r, d = dout.shape
    n = w_out.shape[0]
    cw = math.gcd(math.gcd(ga_off, gb_off), n)
    tm, tn, tk = _tile(r, 1024), _tile(cw, 512), _tile(d, K_TILE)

    def epi(accs, e_refs):
        dm = accs[0]
        pa_ref, pr_ref, ga_ref, gb_ref = e_refs
        sa = _sigmoid(ga_ref[...].astype(F32))
        sb = _sigmoid(gb_ref[...].astype(F32))
        pav = pa_ref[...].astype(F32)
        prv = pr_ref[...].astype(F32)
        return dm * sa, dm * sb, dm * pav * sa * (1.0 - sa), dm * prv * sb * (1.0 - sb)

    o_spec = pl.BlockSpec((tm, tn), lambda i, j, kk: (i, j))
    return _mm(name, dout, pl.BlockSpec((tm, tk), lambda i, j, kk: (i, kk)),
               [(w_out, pl.BlockSpec((tn, tk), lambda i, j, kk: (j, kk)))], NT, (r // tm, n // tn, d // tk),
               [jax.ShapeDtypeStruct((r, n), BF16)] * 4, [o_spec] * 4, (tm, tn), epi,
               extras=[(pa, o_spec), (pr, o_spec),
                       (p, pl.BlockSpec((tm, tn), lambda i, j, kk: (i, ga_off // tn + j))),
                       (p, pl.BlockSpec((tm, tn), lambda i, j, kk: (i, gb_off // tn + j)))])


def _row_tile(r, lx, d):
    pref = 256 if d > 1024 else 512
    return _tile(math.gcd(r, lx), pref, SUBLANES)


def _rmsmod(name, h, scale, shift, lx, rows=None, comm=None):
    r = h.shape[0] if rows is None else rows
    d = h.shape[1]
    tr = _row_tile(r, lx, d)
    nx = lx // tr
    steps = r // tr
    cls = lambda i: (jnp.where(i < nx, 1, 0), 0, 0)
    kind, c_arrays, c_wide = comm if comm is not None else (None, [], [])
    ncm = len(c_arrays)

    def body(h_ref, sc_ref, sh_ref, *rest):
        u_ref = rest[ncm]
        i = pl.program_id(0)
        if ncm:
            start_comm, mid_comm, finish_comm = _comm_plan(kind, rest[:ncm], rest[ncm + 1:2 * ncm + 1], c_wide,
                                                           *rest[2 * ncm + 1:])
            pl.when(i == 0)(start_comm)
            if mid_comm is not None:
                pl.when(i == steps - 1)(mid_comm)
        hv = h_ref[...]
        rinv = lax.rsqrt(jnp.mean(hv * hv, axis=-1, keepdims=True) + NORM_EPS)
        u_ref[...] = (hv * rinv * (1.0 + sc_ref[...]) + sh_ref[...]).astype(u_ref.dtype)
        if ncm:
            pl.when(i == steps - 1)(finish_comm)

    any_spec = pl.BlockSpec(memory_space=pl.ANY)
    outs = pl.pallas_call(
        body, name=name, grid=(steps,),
        in_specs=[pl.BlockSpec((tr, d), lambda i: (i, 0)), pl.BlockSpec((None, 1, d), cls),
                  pl.BlockSpec((None, 1, d), cls)] + [any_spec] * ncm,
        out_specs=[pl.BlockSpec((tr, d), lambda i: (i, 0))] + [any_spec] * ncm,
        out_shape=[jax.ShapeDtypeStruct((r, d), BF16)] + [_comm_out_shape(kind, s, w) for s, w in zip(c_arrays, c_wide)],
        scratch_shapes=_comm_semaphores(ncm) if ncm else [],
        compiler_params=_cp(("arbitrary",) if ncm else ("parallel",)))(h, scale, shift, *c_arrays)
    return outs[0] if comm is None else (outs[0], list(outs[1:]))


def _rmsmod_bwd(name, du, h, scale, dh_in, lx, rows_out, gate=None):
    r, d = du.shape
    rin = dh_in.shape[0]
    tr = _row_tile(math.gcd(r, math.gcd(rin, rows_out)), lx, d)
    nx, nin, nout = lx // tr, rin // tr, rows_out // tr
    cls = lambda i: (jnp.where(i < nx, 1, 0), 0, 0)
    assert gate is None or rows_out == r

    def body(du_ref, h_ref, sc_ref, dhin_ref, *rest):
        if gate is None:
            dh_ref, dsc_ref, dsh_ref = rest
        else:
            f_ref, g_ref, dh_ref, dsc_ref, dsh_ref, df_ref, dg_ref = rest
        i = pl.program_id(0)
        hv = h_ref[...]
        duv = du_ref[...]
        rinv = lax.rsqrt(jnp.mean(hv * hv, axis=-1, keepdims=True) + NORM_EPS)
        nv = hv * rinv
        dn = duv * (1.0 + sc_ref[...])

        @pl.when(jnp.logical_or(i == 0, i == nx))
        def _():
            dsc_ref[...] = jnp.zeros(dsc_ref.shape, F32)
            dsh_ref[...] = jnp.zeros(dsh_ref.shape, F32)

        dsc_ref[...] += jnp.sum(duv * nv, axis=0, keepdims=True)
        dsh_ref[...] += jnp.sum(duv, axis=0, keepdims=True)

        def dh_out():
            dh = rinv * (dn - nv * jnp.mean(dn * nv, axis=-1, keepdims=True))
            return dh + jnp.where(i < nin, dhin_ref[...], 0.0)

        if gate is None:
            @pl.when(i < nout)
            def _():
                dh_ref[...] = dh_out()
        else:
            dhv = dh_out()
            dh_ref[...] = dhv

            @pl.when(jnp.logical_or(i == 0, i == nx))
            def _():
                dg_ref[...] = jnp.zeros(dg_ref.shape, F32)

            df_ref[...] = (gate[2] * g_ref[...] * dhv).astype(df_ref.dtype)
            dg_ref[...] += jnp.sum(gate[2] * f_ref[...].astype(F32) * dhv, axis=0, keepdims=True)

    row = pl.BlockSpec((tr, d), lambda i: (i, 0))
    per_class = pl.BlockSpec((None, 1, d), cls)
    in_specs = [row, row, per_class, pl.BlockSpec((tr, d), lambda i: (jnp.minimum(i, nin - 1), 0))]
    out_specs = [pl.BlockSpec((tr, d), lambda i: (jnp.minimum(i, nout - 1), 0)), per_class, per_class]
    out_shape = [jax.ShapeDtypeStruct((rows_out, d), F32), jax.ShapeDtypeStruct((2, 1, d), F32),
                 jax.ShapeDtypeStruct((2, 1, d), F32)]
    args = [du, h, scale, dh_in]
    if gate is not None:
        in_specs += [row, per_class]
        out_specs += [row, per_class]
        out_shape += [jax.ShapeDtypeStruct((r, d), BF16), jax.ShapeDtypeStruct((2, 1, d), F32)]
        args += [gate[0], gate[1]]
    return pl.pallas_call(
        body, name=name, grid=(r // tr,), in_specs=in_specs, out_specs=out_specs, out_shape=out_shape,
        compiler_params=_cp(("arbitrary",)))(*args)


def _final_loss(name, h, final_norm, target, f, gate, gate_scale):
    r, d = h.shape
    tr = _row_tile(r, r, d)

    def body(h_ref, fn_ref, t_ref, f_ref, g_ref, dh_ref, dfn_ref, loss_ref, df_ref, dg_ref):
        i = pl.program_id(0)
        hv = h_ref[...]
        rinv = lax.rsqrt(jnp.mean(hv * hv, axis=-1, keepdims=True) + NORM_EPS)
        nv = hv * rinv
        fn = fn_ref[...]
        err = nv * fn - t_ref[...]
        dy = err * (1.0 / d)

        @pl.when(i == 0)
        def _():
            dfn_ref[...] = jnp.zeros(dfn_ref.shape, F32)
            loss_ref[...] = jnp.zeros(loss_ref.shape, F32)
            dg_ref[...] = jnp.zeros(dg_ref.shape, F32)

        loss_ref[...] += 0.5 * jnp.sum(jnp.mean(err * err, axis=-1, keepdims=True), axis=0, keepdims=True)
        dfn_ref[...] += jnp.sum(dy * nv, axis=0, keepdims=True)
        dn = dy * fn
        dhv = rinv * (dn - nv * jnp.mean(dn * nv, axis=-1, keepdims=True))
        dh_ref[...] = dhv
        df_ref[...] = (gate_scale * g_ref[...] * dhv).astype(df_ref.dtype)
        dg_ref[...] += jnp.sum(gate_scale * f_ref[...].astype(F32) * dhv, axis=0, keepdims=True)

    row = pl.BlockSpec((tr, d), lambda i: (i, 0))
    one = pl.BlockSpec((1, d), lambda i: (0, 0))
    return pl.pallas_call(
        body, name=name, grid=(r // tr,),
        in_specs=[row, one, row, row, pl.BlockSpec((None, 1, d), lambda i: (1, 0, 0))],
        out_specs=[row, one, pl.BlockSpec((SUBLANES, LANES), lambda i: (0, 0)), row, one],
        out_shape=[jax.ShapeDtypeStruct((r, d), F32), jax.ShapeDtypeStruct((1, d), F32),
                   jax.ShapeDtypeStruct((SUBLANES, LANES), F32), jax.ShapeDtypeStruct((r, d), BF16),
                   jax.ShapeDtypeStruct((1, d), F32)],
        compiler_params=_cp(("arbitrary",)))(h, final_norm, target, f, gate)


def _swap_pairs(t):
    lane = lax.broadcasted_iota(jnp.int32, t.shape, 1)
    nxt = pltpu.roll(t, HEAD_DIM - 1, 1)
    prv = pltpu.roll(t, 1, 1)
    return jnp.where(lane % 2 == 0, nxt, prv)


def _qk_prep(name, p, cosf, sinf, q_gain, k_gain, aw, kw):
    t = p.shape[0]
    tr = _tile(t, 256, SUBLANES)
    q_prescale = HEAD_DIM ** -0.5 * LOG2E

    def head_fwd(v, gain, cs, sn):
        v = v.astype(F32)
        rinv = lax.rsqrt(jnp.mean(v * v, axis=-1, keepdims=True) + NORM_EPS)
        tt = v * rinv * gain
        return tt * cs + _swap_pairs(tt) * sn

    def body(q_ref, k_ref, cos_ref, sin_ref, qg_ref, kg_ref, qo_ref, ko_ref):
        cs, sn = cos_ref[...], sin_ref[...]
        for hh in range(aw // HEAD_DIM):
            sl = slice(hh * HEAD_DIM, (hh + 1) * HEAD_DIM)
            qo_ref[:, sl] = (head_fwd(q_ref[:, sl], qg_ref[...], cs, sn) * q_prescale).astype(qo_ref.dtype)
        for hh in range(kw // HEAD_DIM):
            sl = slice(hh * HEAD_DIM, (hh + 1) * HEAD_DIM)
            ko_ref[:, sl] = head_fwd(k_ref[:, sl], kg_ref[...], cs, sn).astype(ko_ref.dtype)

    assert aw % kw == 0
    row = lambda i: (i, 0)
    return pl.pallas_call(
        body, name=name, grid=(t // tr,),
        in_specs=[pl.BlockSpec((tr, aw), row), pl.BlockSpec((tr, kw), lambda i: (i, aw // kw)),
                  pl.BlockSpec((tr, HEAD_DIM), row), pl.BlockSpec((tr, HEAD_DIM), row),
                  pl.BlockSpec((1, HEAD_DIM), lambda i: (0, 0)), pl.BlockSpec((1, HEAD_DIM), lambda i: (0, 0))],
        out_specs=[pl.BlockSpec((tr, aw), row), pl.BlockSpec((tr, kw), row)],
        out_shape=[jax.ShapeDtypeStruct((t, aw), BF16), jax.ShapeDtypeStruct((t, kw), BF16)],
        compiler_params=_cp(("parallel",)))(p, p, cosf, sinf, q_gain, k_gain)


def _qk_prep_bwd(name, dq_rot, dk_rot, p, cosf, sinf, q_gain, k_gain, aw, kw):
    t = p.shape[0]
    lq = dq_rot.shape[0]
    tr = _tile(math.gcd(t, lq), 256, SUBLANES)
    nq = lq // tr

    def head_bwd(dout, v, gain, cs, sn):
        v = v.astype(F32)
        rinv = lax.rsqrt(jnp.mean(v * v, axis=-1, keepdims=True) + NORM_EPS)
        vn = v * rinv
        dt = dout * cs - _swap_pairs(dout) * sn
        dvn = dt * gain
        dv = rinv * (dvn - vn * jnp.mean(dvn * vn, axis=-1, keepdims=True))
        return dv, jnp.sum(dt * vn, axis=0, keepdims=True)

    def body(dq_ref, dk_ref, q_ref, k_ref, cos_ref, sin_ref, qg_ref, kg_ref, dqo_ref, dko_ref, dqg_ref, dkg_ref):
        i = pl.program_id(0)
        cs, sn = cos_ref[...], sin_ref[...]

        @pl.when(i == 0)
        def _():
            dqg_ref[...] = jnp.zeros(dqg_ref.shape, F32)
            dkg_ref[...] = jnp.zeros(dkg_ref.shape, F32)

        has_q = i < nq
        for hh in range(aw // HEAD_DIM):
            sl = slice(hh * HEAD_DIM, (hh + 1) * HEAD_DIM)
            dout = jnp.where(has_q, dq_ref[:, sl], 0.0)
            dv, dg = head_bwd(dout, q_ref[:, sl], qg_ref[...], cs, sn)
            dqo_ref[:, sl] = dv.astype(dqo_ref.dtype)
            dqg_ref[...] += dg
        for hh in range(kw // HEAD_DIM):
            sl = slice(hh * HEAD_DIM, (hh + 1) * HEAD_DIM)
            dv, dg = head_bwd(dk_ref[:, sl], k_ref[:, sl], kg_ref[...], cs, sn)
            dko_ref[:, sl] = dv.astype(dko_ref.dtype)
            dkg_ref[...] += dg

    row = lambda i: (i, 0)
    one = lambda i: (0, 0)
    return pl.pallas_call(
        body, name=name, grid=(t // tr,),
        in_specs=[pl.BlockSpec((tr, aw), lambda i: (jnp.minimum(i, nq - 1), 0)), pl.BlockSpec((tr, kw), row),
                  pl.BlockSpec((tr, aw), row), pl.BlockSpec((tr, kw), lambda i: (i, aw // kw)),
                  pl.BlockSpec((tr, HEAD_DIM), row), pl.BlockSpec((tr, HEAD_DIM), row),
                  pl.BlockSpec((1, HEAD_DIM), one), pl.BlockSpec((1, HEAD_DIM), one)],
        out_specs=[pl.BlockSpec((tr, aw), row), pl.BlockSpec((tr, kw), row),
                   pl.BlockSpec((1, HEAD_DIM), one), pl.BlockSpec((1, HEAD_DIM), one)],
        out_shape=[jax.ShapeDtypeStruct((t, aw), BF16), jax.ShapeDtypeStruct((t, kw), BF16),
                   jax.ShapeDtypeStruct((1, HEAD_DIM), F32), jax.ShapeDtypeStruct((1, HEAD_DIM), F32)],
        compiler_params=_cp(("arbitrary",)))(dq_rot, dk_rot, p, p, cosf, sinf, q_gain, k_gain)


def _attn_tiles(lq, t):
    return _tile(lq, 1024), _tile(t, 768)


def _attn_fwd(name, q, k, p, v_off, lq, groups):
    t = k.shape[0]
    hq = q.shape[1] // HEAD_DIM
    hkv = hq // groups
    gw = groups * HEAD_DIM
    tq, tk = _attn_tiles(lq, t)
    nkv = t // tk
    vb = v_off // HEAD_DIM

    def body(q_ref, k_ref, v_ref, o_ref, lse_ref, m_sc, l_sc, acc_sc, s_sc):
        j = pl.program_id(2)

        @pl.when(j == 0)
        def _():
            m_sc[...] = jnp.full(m_sc.shape, -jnp.inf, F32)
            l_sc[...] = jnp.zeros(l_sc.shape, F32)
            acc_sc[...] = jnp.zeros(acc_sc.shape, F32)
            s_sc[...] = jnp.zeros(s_sc.shape, F32)

        has_prev = j > 0
        kv, vv = k_ref[...], v_ref[...]
        for g in range(groups):
            sl = slice(g * HEAD_DIM, (g + 1) * HEAD_DIM)
            s = s_sc[g]
            s_sc[g] = _dot(q_ref[:, sl], kv, NT)
            m_prev = m_sc[g]
            m_new = jnp.where(has_prev, jnp.maximum(m_prev, jnp.max(s, axis=-1, keepdims=True)), jnp.inf)
            alpha = jnp.exp2(m_prev - m_new)
            pexp = jnp.exp2((s - m_new).astype(BF16))
            part = pexp[:, 0:LANES].astype(F32)
            for cb in range(1, tk // LANES):
                part = part + pexp[:, cb * LANES:(cb + 1) * LANES].astype(F32)
            l_sc[g] = alpha * l_sc[g] + part
            acc_sc[:, sl] = alpha * acc_sc[:, sl] + _dot(pexp, vv, NN)
            m_sc[g] = jnp.where(has_prev, m_new, -jnp.inf)

        @pl.when(j == nkv)
        def _():
            for g in range(groups):
                sl = slice(g * HEAD_DIM, (g + 1) * HEAD_DIM)
                l_row = jnp.sum(l_sc[g], axis=-1, keepdims=True)
                o_ref[:, sl] = (acc_sc[:, sl] * (1.0 / l_row)).astype(o_ref.dtype)
                lse_ref[g] = jnp.broadcast_to(m_sc[g] + jnp.log2(l_row), (tq, LANES))

    return pl.pallas_call(
        body, name=name, grid=(hkv, lq // tq, nkv + 1),
        in_specs=[pl.BlockSpec((tq, gw), lambda h, i, j: (i, h)),
                  pl.BlockSpec((tk, HEAD_DIM), lambda h, i, j: (jnp.minimum(j, nkv - 1), h)),
                  pl.BlockSpec((tk, HEAD_DIM), lambda h, i, j: (jnp.maximum(j - 1, 0), vb + h))],
        out_specs=[pl.BlockSpec((tq, gw), lambda h, i, j: (i, h)),
                   pl.BlockSpec((groups, tq, LANES), lambda h, i, j: (h, i, 0))],
        out_shape=[jax.ShapeDtypeStruct((lq, hq * HEAD_DIM), BF16), jax.ShapeDtypeStruct((hq, lq, LANES), F32)],
        scratch_shapes=[pltpu.VMEM((groups, tq, 1), F32), pltpu.VMEM((groups, tq, LANES), F32), pltpu.VMEM((tq, gw), F32),
                        pltpu.VMEM((groups, tq, tk), F32)],
        compiler_params=_cp(("parallel", "parallel", "arbitrary")))(q, k, p)


def _attn_delta(name, o, do):
    lq, aw = o.shape
    hq = aw // HEAD_DIM
    tr = _tile(lq, 512, SUBLANES)

    def body(o_ref, do_ref, d_ref):
        for h in range(hq):
            sl = slice(h * HEAD_DIM, (h + 1) * HEAD_DIM)
            dsum = jnp.sum(do_ref[:, sl].astype(F32) * o_ref[:, sl].astype(F32), axis=-1, keepdims=True)
            d_ref[h] = jnp.broadcast_to(dsum, (tr, LANES))

    spec = pl.BlockSpec((tr, aw), lambda i: (i, 0))
    return pl.pallas_call(
        body, name=name, grid=(lq // tr,), in_specs=[spec, spec],
        out_specs=pl.BlockSpec((hq, tr, LANES), lambda i: (0, i, 0)),
        out_shape=jax.ShapeDtypeStruct((hq, lq, LANES), F32), compiler_params=_cp(("parallel",)))(o, do)


def _attn_bwd(name, q, k, p, v_off, do, lse, delta, lq, groups):
    t = k.shape[0]
    hkv = k.shape[1] // HEAD_DIM
    gw = groups * HEAD_DIM
    tq, tk = _attn_tiles(lq, t)
    nq, nkv = lq // tq, t // tk
    scale = HEAD_DIM ** -0.5
    vb = v_off // HEAD_DIM

    def body(q_ref, k_ref, v_ref, do_ref, lse_ref, delta_ref, dq_ref, dk_ref, dv_ref, dq_sc, dk_sc, dv_sc):
        j, i = pl.program_id(1), pl.program_id(2)

        @pl.when(i == 0)
        def _():
            dk_sc[...] = jnp.zeros(dk_sc.shape, F32)
            dv_sc[...] = jnp.zeros(dv_sc.shape, F32)

        @pl.when(j == 0)
        def _():
            dq_sc[i] = jnp.zeros((tq, gw), F32)

        kv, vv = k_ref[...], v_ref[...]
        dk_part, dv_part = None, None
        for g in range(groups):
            sl = slice(g * HEAD_DIM, (g + 1) * HEAD_DIM)
            qv, dov = q_ref[:, sl], do_ref[:, sl]
            s = _dot(qv, kv, NT)
            pexp = jnp.exp2(s - lse_ref[g, :, 0:1])
            dv_g = _dot(pexp.astype(BF16), dov, TN)
            dp = _dot(dov, vv, NT)
            ds = (pexp * (dp - delta_ref[g, :, 0:1])).astype(BF16)
            dk_g = _dot(ds, qv, TN)
            dq_sc[i, :, sl] += _dot(ds, kv, NN)
            dk_part = dk_g if dk_part is None else dk_part + dk_g
            dv_part = dv_g if dv_part is None else dv_part + dv_g
        dk_sc[...] += dk_part
        dv_sc[...] += dv_part

        @pl.when(i == nq - 1)
        def _():
            dk_ref[...] = dk_sc[...] * LN2
            dv_ref[...] = dv_sc[...].astype(dv_ref.dtype)

        @pl.when(j == nkv - 1)
        def _():
            dq_ref[...] = dq_sc[i] * scale

    qspec = pl.BlockSpec((tq, gw), lambda kh, j, i: (i, kh))
    kspec = pl.BlockSpec((tk, HEAD_DIM), lambda kh, j, i: (j, kh))
    rowspec = pl.BlockSpec((groups, tq, LANES), lambda kh, j, i: (kh, i, 0))
    dqspec = pl.BlockSpec((tq, gw), lambda kh, j, i: (jnp.where(j == nkv - 1, i, 0), kh))
    return pl.pallas_call(
        body, name=name, grid=(hkv, nkv, nq),
        in_specs=[qspec, kspec, pl.BlockSpec((tk, HEAD_DIM), lambda kh, j, i: (j, vb + kh)), qspec, rowspec, rowspec],
        out_specs=[dqspec, kspec, kspec],
        out_shape=[jax.ShapeDtypeStruct((lq, hkv * gw), F32), jax.ShapeDtypeStruct((t, hkv * HEAD_DIM), F32),
                   jax.ShapeDtypeStruct((t, hkv * HEAD_DIM), BF16)],
        scratch_shapes=[pltpu.VMEM((nq, tq, gw), F32), pltpu.VMEM((tk, HEAD_DIM), F32), pltpu.VMEM((tk, HEAD_DIM), F32)],
        compiler_params=_cp(("parallel", "arbitrary", "arbitrary")))(q, k, p, do, lse, delta)


def _ret_tables(log_gamma, direction):
    c = RET_CHUNK
    idx = jnp.arange(c, dtype=F32)
    diff = idx[:, None] - idx[None, :]
    if direction == 1:
        diff = -diff
    keep = diff >= 0
    lg = log_gamma.astype(F32)
    mask = jnp.where(keep[None], jnp.exp(jnp.where(keep, diff, 0.0)[None] * lg[:, None, None]), 0.0)
    q_exp = idx + 1.0 if direction == 0 else c - idx
    k_exp = c - 1.0 - idx if direction == 0 else idx
    sign = 1.0 if direction == 0 else -1.0
    lane = lambda v: jnp.broadcast_to(v[..., None], v.shape + (LANES,))
    qdec = lane(jnp.exp(q_exp[None, :] * lg[:, None]))
    kdec = lane(jnp.exp(k_exp[None, :] * lg[:, None]))
    cdec = jnp.broadcast_to(jnp.exp(c * lg)[:, None, None], (lg.shape[0], SUBLANES, LANES))
    weights = lane(jnp.stack([sign * idx, q_exp, -sign * idx, k_exp], axis=0))
    return mask, qdec, kdec, cdec, weights


def _ret_chunk_of(direction, step, nx, nc):
    if direction == 0:
        return jnp.where(step < nc, nx + step, step - nc)
    return jnp.where(step < nc, nx + nc - 1 - step, nx + nc - 1 - step)


def _ret_fwd(name, p, offs, tables, direction, lx, prev):
    q_off, k_off, v_off, rw = offs
    t = p.shape[0]
    c = RET_CHUNK
    n_steps, nx = t // c, lx // c
    nc = n_steps - nx
    bw = math.gcd(math.gcd(q_off, k_off), math.gcd(v_off, rw))
    bw = _tile(bw, 512)
    hpb = bw // HEAD_DIM
    heads = rw // HEAD_DIM
    k_scale = HEAD_DIM ** -0.5
    mask, qdec, kdec, cdec, _ = tables
    rc = lambda n: _ret_chunk_of(direction, n, nx, nc)

    ng = heads // hpb

    def body(*refs):
        q_refs, k_refs, v_refs = refs[0:ng], refs[ng:2 * ng], refs[2 * ng:3 * ng]
        if prev is None:
            m_ref, qd_ref, kd_ref, cd_ref, y_ref, st_ref, s_sc = refs[3 * ng:]
        else:
            m_ref, qd_ref, kd_ref, cd_ref, prev_ref, y_ref, st_ref, s_sc = refs[3 * ng:]
        n = pl.program_id(0)

        @pl.when(n == 0)
        def _():
            s_sc[...] = jnp.zeros(s_sc.shape, F32)

        for hd in range(heads):
            gg, hh = divmod(hd, hpb)
            sl = slice(hh * HEAD_DIM, (hh + 1) * HEAD_DIM)
            osl = slice(hd * HEAD_DIM, (hd + 1) * HEAD_DIM)
            qv = q_refs[gg][:, sl]
            kf = k_refs[gg][:, sl].astype(F32) * k_scale
            kv = kf.astype(BF16)
            vv = v_refs[gg][:, sl]
            state = s_sc[hd]
            st_ref[hd] = state
            a = _dot(qv, kv, NT) * m_ref[hd]
            y = _dot(a.astype(BF16), vv, NN) + _dot(qv, state.astype(BF16), NN) * qd_ref[hd]
            s_sc[hd] = state * cd_ref[hd, 0:1, :] + _dot((kf * kd_ref[hd]).astype(BF16), vv, TN)
            if prev is not None:
                y = y + prev_ref[:, osl]
            y_ref[:, osl] = y

    col = lambda off, gg: (lambda n: (rc(n), off // bw + gg))
    tab3 = lambda n: (0, 0, 0)
    in_specs = [pl.BlockSpec((c, bw), col(off, gg)) for off in (q_off, k_off, v_off) for gg in range(ng)]
    in_specs += [pl.BlockSpec((heads, c, c), tab3), pl.BlockSpec((heads, c, LANES), tab3),
                 pl.BlockSpec((heads, c, LANES), tab3), pl.BlockSpec((heads, SUBLANES, LANES), tab3)]
    args = [p] * (3 * ng) + [mask, qdec, kdec, cdec]
    aliases = {}
    yspec = pl.BlockSpec((c, rw), lambda n: (rc(n), 0))
    if prev is not None:
        in_specs.append(yspec)
        args.append(prev)
        aliases = {len(args) - 1: 0}
    return pl.pallas_call(
        body, name=name, grid=(n_steps,), in_specs=in_specs,
        out_specs=[yspec, pl.BlockSpec((None, heads, HEAD_DIM, HEAD_DIM), lambda n: (n, 0, 0, 0))],
        out_shape=[jax.ShapeDtypeStruct((t, rw), F32), jax.ShapeDtypeStruct((n_steps, heads, HEAD_DIM, HEAD_DIM), F32)],
        scratch_shapes=[pltpu.VMEM((heads, HEAD_DIM, HEAD_DIM), F32)], input_output_aliases=aliases,
        compiler_params=_cp(("arbitrary",)))(*args)


def _ret_bwd(name, p, offs, tables, states, dy, direction, lx, prev):
    q_off, k_off, v_off, rw = offs
    t = p.shape[0]
    c = RET_CHUNK
    n_steps, nx = t // c, lx // c
    nc = n_steps - nx
    bw = math.gcd(math.gcd(q_off, k_off), math.gcd(v_off, rw))
    bw = _tile(bw, 512)
    hpb = bw // HEAD_DIM
    heads = rw // HEAD_DIM
    k_scale = HEAD_DIM ** -0.5
    mask, qdec, kdec, cdec, weights = tables
    step_of = lambda n: n_steps - 1 - n
    rc = lambda n: _ret_chunk_of(direction, step_of(n), nx, nc)

    ng = heads // hpb

    def body(*refs):
        q_refs, k_refs, v_refs = refs[0:ng], refs[ng:2 * ng], refs[2 * ng:3 * ng]
        if prev is None:
            (dy_ref, st_ref, m_ref, qd_ref, kd_ref, cd_ref, w_ref,
             dq_ref, dk_ref, dv_ref, dl_ref, ds_sc, lam_sc) = refs[3 * ng:]
        else:
            (dy_ref, st_ref, m_ref, qd_ref, kd_ref, cd_ref, w_ref, pq_ref, pk_ref, pv_ref,
             dq_ref, dk_ref, dv_ref, dl_ref, ds_sc, lam_sc) = refs[3 * ng:]
        n = pl.program_id(0)

        @pl.when(n == 0)
        def _():
            ds_sc[...] = jnp.zeros(ds_sc.shape, F32)
            lam_sc[...] = jnp.zeros(lam_sc.shape, F32)

        is_x = rc(n) < nx
        for hd in range(heads):
            gg, hh = divmod(hd, hpb)
            sl = slice(hh * HEAD_DIM, (hh + 1) * HEAD_DIM)
            osl = slice(hd * HEAD_DIM, (hd + 1) * HEAD_DIM)
            qv = q_refs[gg][:, sl]
            qf = qv.astype(F32)
            kf = k_refs[gg][:, sl].astype(F32) * k_scale
            kv = kf.astype(BF16)
            vv = v_refs[gg][:, sl]
            dyv = jnp.where(is_x, dy_ref[:, osl], 0.0).astype(BF16)
            state = st_ref[hd]
            dstate = ds_sc[hd]
            dstate_b = dstate.astype(BF16)
            msk, qd, kd = m_ref[hd], qd_ref[hd], kd_ref[hd]
            cd = cd_ref[hd, 0:1, :]
            a = _dot(qv, kv, NT) * msk
            da = (_dot(dyv, vv, NT) * msk).astype(BF16)
            dq_intra = _dot(da, kv, NN)
            dk_intra = _dot(da, qv, TN)
            dq_inter = _dot(dyv, state.astype(BF16), NT) * qd
            dk_inter = _dot(vv, dstate_b, NT) * kd
            dv = _dot(a.astype(BF16), dyv, TN) + _dot((kf * kd).astype(BF16), dstate_b, NN)
            lam_sc[hd] += (qf * (w_ref[0] * dq_intra + w_ref[1] * dq_inter)
                           + kf * (w_ref[2] * dk_intra + w_ref[3] * dk_inter)
                           + (c * cd) * state * dstate)
            ds_sc[hd] = _dot((qf * qd).astype(BF16), dyv, TN) + cd * dstate
            dq = dq_intra + dq_inter
            dk = (dk_intra + dk_inter) * k_scale
            if prev is not None:
                dq = dq + pq_ref[:, osl]
                dk = dk + pk_ref[:, osl]
                dv = dv + pv_ref[:, osl]
            dq_ref[:, osl] = dq
            dk_ref[:, osl] = dk
            dv_ref[:, osl] = dv

        @pl.when(n == n_steps - 1)
        def _():
            for hd in range(heads):
                dl_ref[hd] = jnp.broadcast_to(jnp.sum(lam_sc[hd]), (SUBLANES, LANES))

    col = lambda off, gg: (lambda n: (rc(n), off // bw + gg))
    tab3 = lambda n: (0, 0, 0)
    ospec = pl.BlockSpec((c, rw), lambda n: (rc(n), 0))
    in_specs = [pl.BlockSpec((c, bw), col(off, gg)) for off in (q_off, k_off, v_off) for gg in range(ng)]
    in_specs += [pl.BlockSpec((c, rw), lambda n: (jnp.minimum(rc(n), nx - 1), 0)),
                 pl.BlockSpec((None, heads, HEAD_DIM, HEAD_DIM), lambda n: (step_of(n), 0, 0, 0)),
                 pl.BlockSpec((heads, c, c), tab3), pl.BlockSpec((heads, c, LANES), tab3), pl.BlockSpec((heads, c, LANES), tab3),
                 pl.BlockSpec((heads, SUBLANES, LANES), tab3), pl.BlockSpec((4, c, LANES), tab3)]
    args = [p] * (3 * ng) + [dy, states, mask, qdec, kdec, cdec, weights]
    aliases = {}
    if prev is not None:
        for k_out, arr in enumerate(prev):
            in_specs.append(ospec)
            args.append(arr)
            aliases[len(args) - 1] = k_out
    big = jax.ShapeDtypeStruct((t, rw), F32)
    return pl.pallas_call(
        body, name=name, grid=(n_steps,), in_specs=in_specs,
        out_specs=[ospec, ospec, ospec, pl.BlockSpec((heads, SUBLANES, LANES), tab3)],
        out_shape=[big, big, big, jax.ShapeDtypeStruct((heads, SUBLANES, LANES), F32)],
        scratch_shapes=[pltpu.VMEM((heads, HEAD_DIM, HEAD_DIM), F32), pltpu.VMEM((heads, HEAD_DIM, HEAD_DIM), F32)],
        input_output_aliases=aliases, compiler_params=_cp(("arbitrary",)))(*args)


def _ret_out(name, y, p, g_off, lx):
    rw = y.shape[1]
    bw = _tile(math.gcd(g_off, rw), 512)
    tr = _tile(lx, 512, SUBLANES)

    def body(y_ref, g_ref, o_ref):
        for hh in range(bw // HEAD_DIM):
            sl = slice(hh * HEAD_DIM, (hh + 1) * HEAD_DIM)
            yv = y_ref[:, sl]
            gv = g_ref[:, sl].astype(F32)
            rinv = lax.rsqrt(jnp.mean(yv * yv, axis=-1, keepdims=True) + NORM_EPS)
            o_ref[:, sl] = (gv * _sigmoid(gv) * yv * rinv).astype(o_ref.dtype)

    spec = pl.BlockSpec((tr, bw), lambda i, g: (i, g))
    return pl.pallas_call(
        body, name=name, grid=(lx // tr, rw // bw),
        in_specs=[spec, pl.BlockSpec((tr, bw), lambda i, g: (i, g_off // bw + g))],
        out_specs=spec, out_shape=jax.ShapeDtypeStruct((lx, rw), BF16),
        compiler_params=_cp(("parallel", "parallel")))(y, p)


def _ret_out_bwd(name, dyr, y, p, g_off, lx):
    rw = y.shape[1]
    bw = _tile(math.gcd(g_off, rw), 512)
    tr = _tile(lx, 512, SUBLANES)

    def body(d_ref, y_ref, g_ref, dy_ref, dg_ref):
        for hh in range(bw // HEAD_DIM):
            sl = slice(hh * HEAD_DIM, (hh + 1) * HEAD_DIM)
            yv = y_ref[:, sl]
            gv = g_ref[:, sl].astype(F32)
            dv = d_ref[:, sl]
            rinv = lax.rsqrt(jnp.mean(yv * yv, axis=-1, keepdims=True) + NORM_EPS)
            yn = yv * rinv
            sg = _sigmoid(gv)
            dg_ref[:, sl] = (dv * yn * sg * (1.0 + gv * (1.0 - sg))).astype(dg_ref.dtype)
            dyn = dv * gv * sg
            dy_ref[:, sl] = rinv * (dyn - yn * jnp.mean(dyn * yn, axis=-1, keepdims=True))

    spec = pl.BlockSpec((tr, bw), lambda i, g: (i, g))
    return pl.pallas_call(
        body, name=name, grid=(lx // tr, rw // bw),
        in_specs=[spec, spec, pl.BlockSpec((tr, bw), lambda i, g: (i, g_off // bw + g))],
        out_specs=[spec, spec],
        out_shape=[jax.ShapeDtypeStruct((lx, rw), F32), jax.ShapeDtypeStruct((lx, rw), BF16)],
        compiler_params=_cp(("parallel", "parallel")))(dyr, y, p)


def _merge(name, pa, pr, p, ga_off, gb_off):
    r, d = pa.shape
    cw = _tile(math.gcd(math.gcd(ga_off, gb_off), d), 1024)
    tr = _tile(r, 512, SUBLANES)

    def body(pa_ref, pr_ref, ga_ref, gb_ref, o_ref):
        o_ref[...] = (_sigmoid(ga_ref[...].astype(F32)) * pa_ref[...].astype(F32)
                      + _sigmoid(gb_ref[...].astype(F32)) * pr_ref[...].astype(F32)).astype(o_ref.dtype)

    spec = pl.BlockSpec((tr, cw), lambda i, j: (i, j))
    return pl.pallas_call(
        body, name=name, grid=(r // tr, d // cw),
        in_specs=[spec, spec, pl.BlockSpec((tr, cw), lambda i, j: (i, ga_off // cw + j)),
                  pl.BlockSpec((tr, cw), lambda i, j: (i, gb_off // cw + j))],
        out_specs=spec, out_shape=jax.ShapeDtypeStruct((r, d), BF16),
        compiler_params=_cp(("parallel", "parallel")))(pa, pr, p, p)


def _ada_fwd(name, cond, w):
    rows, d = cond.shape
    n = w.shape[1]
    tn = _tile(n, 768)

    def body(c_ref, w_ref, o_ref, s_ref):
        cv = c_ref[...]
        sv = cv * _sigmoid(cv)
        s_ref[...] = sv
        o_ref[...] = _dot(sv.astype(BF16), w_ref[...].astype(BF16), NN)

    return pl.pallas_call(
        body, name=name, grid=(n // tn,),
        in_specs=[pl.BlockSpec((rows, d), lambda j: (0, 0)), pl.BlockSpec((d, tn), lambda j: (0, j))],
        out_specs=[pl.BlockSpec((rows, tn), lambda j: (0, j)), pl.BlockSpec((rows, d), lambda j: (0, 0))],
        out_shape=[jax.ShapeDtypeStruct((rows, n), F32), jax.ShapeDtypeStruct((rows, d), F32)],
        compiler_params=_cp(("arbitrary",)))(cond, w)


def _ada_bwd(name, s_cond, dmod, w):
    rows, d = s_cond.shape
    n = w.shape[1]
    tn = _tile(n, 768)

    def body(s_ref, dm_ref, w_ref, gw_ref, ds_ref):
        j = pl.program_id(0)

        @pl.when(j == 0)
        def _():
            ds_ref[...] = jnp.zeros(ds_ref.shape, F32)

        dmv = dm_ref[...].astype(BF16)
        gw_ref[...] = _dot(s_ref[...].astype(BF16), dmv, TN)
        ds_ref[...] += _dot(dmv, w_ref[...].astype(BF16), NT)

    return pl.pallas_call(
        body, name=name, grid=(n // tn,),
        in_specs=[pl.BlockSpec((rows, d), lambda j: (0, 0)), pl.BlockSpec((rows, tn), lambda j: (0, j)),
                  pl.BlockSpec((d, tn), lambda j: (0, j))],
        out_specs=[pl.BlockSpec((d, tn), lambda j: (0, j)), pl.BlockSpec((rows, d), lambda j: (0, 0))],
        out_shape=[jax.ShapeDtypeStruct((d, n), F32), jax.ShapeDtypeStruct((rows, d), F32)],
        compiler_params=_cp(("arbitrary",)))(s_cond, dmod, w)


def _sum_slots(name, a):
    s, r, c = a.shape

    def body(a_ref, o_ref):
        acc = a_ref[0]
        for k in range(1, s):
            acc = acc + a_ref[k]
        o_ref[...] = acc

    return pl.pallas_call(
        body, name=name, grid=(1,), in_specs=[pl.BlockSpec((s, r, c), lambda i: (0, 0, 0))],
        out_specs=pl.BlockSpec((r, c), lambda i: (0, 0)), out_shape=jax.ShapeDtypeStruct((r, c), F32),
        compiler_params=_cp(("arbitrary",)))(a)


def _cctx_grad(name, parts, c_ctx):
    s, r, d = parts.shape

    def body(p_ref, c_ref, o_ref):
        acc = p_ref[0]
        for k in range(1, s):
            acc = acc + p_ref[k]
        cv = c_ref[...]
        sg = _sigmoid(cv)
        o_ref[...] = acc[0:1, :] * sg * (1.0 + cv * (1.0 - sg))

    return pl.pallas_call(
        body, name=name, grid=(1,),
        in_specs=[pl.BlockSpec((s, r, d), lambda i: (0, 0, 0)), pl.BlockSpec((1, d), lambda i: (0, 0))],
        out_specs=pl.BlockSpec((1, d), lambda i: (0, 0)), out_shape=jax.ShapeDtypeStruct((1, d), F32),
        compiler_params=_cp(("arbitrary",)))(parts, c_ctx)


def _adamw(name, slots, w, m, v):
    s, r, c = slots.shape
    tr = _tile(r, 256 if c > 1024 else 512, SUBLANES)
    c1 = 1.0 - ADAM_B1 ** ADAM_STEP
    c2 = 1.0 - ADAM_B2 ** ADAM_STEP

    def body(s_ref, w_ref, m_ref, v_ref, g_ref, d_ref, mo_ref, vo_ref):
        g = s_ref[0].astype(F32)
        for k in range(1, s):
            g = g + s_ref[k].astype(F32)
        mn = ADAM_B1 * m_ref[...] + (1.0 - ADAM_B1) * g
        vn = ADAM_B2 * v_ref[...] + (1.0 - ADAM_B2) * (g * g)
        m_hat = mn / c1
        v_hat = vn / c2
        g_ref[...] = g
        mo_ref[...] = mn
        vo_ref[...] = vn
        d_ref[...] = -ADAM_LR * (m_hat / (jnp.sqrt(v_hat) + ADAM_EPS) + ADAM_WD * w_ref[...])

    spec = pl.BlockSpec((tr, c), lambda i: (i, 0))
    shp = jax.ShapeDtypeStruct((r, c), F32)
    return pl.pallas_call(
        body, name=name, grid=(r // tr,),
        in_specs=[pl.BlockSpec((s, tr, c), lambda i: (0, i, 0)), spec, spec, spec],
        out_specs=[spec] * 4, out_shape=[shp] * 4, compiler_params=_cp(("parallel",)))(slots, w, m, v)


def _coords():
    return lax.axis_index("x"), lax.axis_index("y"), lax.axis_index("c")


def _flip(coord, bit):
    return 1 - coord if bit else coord


def _all_gather_small(name, blk):
    r, ccols = blk.shape

    def body(x_ref, out_ref, send_sems, recv_sems, local_sem):
        x, y, c = _coords()
        me, sibling = (x, y, c), (x, y, 1 - c)
        chips = [(1 - x, y), (x, 1 - y), (1 - x, 1 - y)]

        def slot(px, py, pc):
            return out_ref.at[4 * px + 2 * py + pc]

        def copy(k, block, to, src=None):
            return pltpu.make_async_remote_copy(
                src_ref=slot(*block) if src is None else src, dst_ref=slot(*block),
                send_sem=send_sems.at[k], recv_sem=recv_sems.at[k], device_id=to, device_id_type=MESH)

        mine = pltpu.make_async_copy(x_ref, slot(*me), local_sem)
        mine.start()
        first = [copy(0, me, sibling, src=x_ref)]
        first += [copy(1 + j, me, (*chip, c), src=x_ref) for j, chip in enumerate(chips)]
        for cp in first:
            cp.start()
        passed = [copy(4 + j, (*chip, c), sibling) for j, chip in enumerate(chips)]
        for j, chip in enumerate(chips):
            copy(1 + j, (*chip, c), me).wait_recv()
            passed[j].start()
        copy(0, sibling, me).wait_recv()
        for j, chip in enumerate(chips):
            copy(4 + j, (*chip, 1 - c), me).wait_recv()
        for cp in first + passed:
            cp.wait_send()
        mine.wait()

    return pl.pallas_call(
        body, name=name, out_shape=jax.ShapeDtypeStruct((N_DEV, r, ccols), blk.dtype),
        in_specs=[pl.BlockSpec(memory_space=pltpu.VMEM)], out_specs=pl.BlockSpec(memory_space=pltpu.VMEM),
        scratch_shapes=[pltpu.SemaphoreType.DMA((7,)), pltpu.SemaphoreType.DMA((7,)), pltpu.SemaphoreType.DMA],
    )(blk)


def _comm_semaphores(n):
    return [pltpu.SemaphoreType.DMA((7 * n,)), pltpu.SemaphoreType.DMA((7 * n,)), pltpu.SemaphoreType.DMA((n,))]


def _comm_out_shape(kind, s, wide):
    if kind == "gather":
        shape = (s.shape[0], N_DEV * s.shape[1]) if wide else (N_DEV,) + s.shape
    else:
        shape = (N_DEV, s.shape[0], s.shape[1] // N_DEV) if wide else s.shape
    return jax.ShapeDtypeStruct(shape, s.dtype)


def _block_of(ref, idx, wide, cols):
    if not wide:
        return ref.at[idx]
    return ref.at[:, pl.ds(pl.multiple_of(idx * cols, LANES), cols)]


def _gather_plan(ins, outs, wide, send_sems, recv_sems, local_sems):
    n = len(ins)
    x, y, c = _coords()
    me, sibling = (x, y, c), (x, y, 1 - c)
    chips = [(1 - x, y), (x, 1 - y), (1 - x, 1 - y)]

    def slot(a, px, py, pc):
        return _block_of(outs[a], 4 * px + 2 * py + pc, wide[a], ins[a].shape[-1])

    def copy(a, k, block, to, src=None):
        return pltpu.make_async_remote_copy(
            src_ref=slot(a, *block) if src is None else src, dst_ref=slot(a, *block),
            send_sem=send_sems.at[7 * a + k], recv_sem=recv_sems.at[7 * a + k], device_id=to, device_id_type=MESH)

    def local(a):
        return pltpu.make_async_copy(ins[a], slot(a, *me), local_sems.at[a])

    def first(a):
        return [copy(a, 0, me, sibling, src=ins[a])] + [copy(a, 1 + j, me, (*chip, c), src=ins[a])
                                                        for j, chip in enumerate(chips)]

    def passed(a, j):
        return copy(a, 4 + j, (*chips[j], c), sibling)

    def start():
        for a in range(n):
            local(a).start()
            for cp in first(a):
                cp.start()

    def pass_on():
        for a in range(n):
            for j, chip in enumerate(chips):
                copy(a, 1 + j, (*chip, c), me).wait_recv()
                passed(a, j).start()

    def finish():
        for a in range(n):
            copy(a, 0, sibling, me).wait_recv()
            for j, chip in enumerate(chips):
                copy(a, 4 + j, (*chip, 1 - c), me).wait_recv()
        for a in range(n):
            for cp in first(a) + [passed(a, j) for j in range(3)]:
                cp.wait_send()
            local(a).wait()

    return start, pass_on, finish


def _exchange_plan(ins, outs, wide, send_sems, recv_sems, local_sems):
    n = len(ins)
    x, y, c = _coords()
    my_idx = 4 * x + 2 * y + c

    def src(a, idx):
        return _block_of(ins[a], idx, wide[a], outs[a].shape[-1])

    def local(a):
        return pltpu.make_async_copy(src(a, my_idx), outs[a].at[my_idx], local_sems.at[a])

    def pair(a, rel):
        px, py, pc = _flip(x, rel & 4), _flip(y, rel & 2), _flip(c, rel & 1)
        peer_idx = 4 * px + 2 * py + pc
        sems = dict(send_sem=send_sems.at[7 * a + rel - 1], recv_sem=recv_sems.at[7 * a + rel - 1],
                    device_id=(px, py, pc), device_id_type=MESH)
        send = pltpu.make_async_remote_copy(src_ref=src(a, peer_idx), dst_ref=outs[a].at[my_idx], **sems)
        recv = pltpu.make_async_remote_copy(src_ref=src(a, my_idx), dst_ref=outs[a].at[peer_idx], **sems)
        return send, recv

    def start():
        for a in range(n):
            local(a).start()
            for rel in range(1, N_DEV):
                pair(a, rel)[0].start()

    def finish():
        for a in range(n):
            for rel in range(1, N_DEV):
                send, recv = pair(a, rel)
                recv.wait_recv()
                send.wait_send()
            local(a).wait()

    return start, None, finish


def _comm_plan(kind, ins, outs, wide, send_sems, recv_sems, local_sems):
    plan = {"gather": _gather_plan, "exchange": _exchange_plan}[kind]
    return plan(ins, outs, wide, send_sems, recv_sems, local_sems)


def _rope_tables(lx, lc):
    rows = lx // GRID_W
    row = jnp.repeat(jnp.arange(rows, dtype=F32), GRID_W)
    col = jnp.tile(jnp.arange(GRID_W, dtype=F32), rows)
    half = HEAD_DIM // 2
    inv_freq = ROPE_THETA ** (-jnp.arange(0, half, 2, dtype=F32) / half)
    ang = jnp.concatenate([row[:, None] * inv_freq, col[:, None] * inv_freq], axis=-1)
    cos, sin = jnp.cos(ang), jnp.sin(ang)
    cosf = jnp.repeat(cos, 2, axis=-1)
    sinf = jnp.stack([-sin, sin], axis=-1).reshape(lx, HEAD_DIM)
    cosf = jnp.concatenate([cosf, jnp.ones((lc, HEAD_DIM), F32)], axis=0)
    sinf = jnp.concatenate([sinf, jnp.zeros((lc, HEAD_DIM), F32)], axis=0)
    return cosf, sinf


def _cols_full(g):
    return jnp.transpose(g, (1, 0, 2)).reshape(g.shape[1], N_DEV * g.shape[2])


def _cols_split(w):
    k, n = w.shape
    return jnp.transpose(w.reshape(k, N_DEV, n // N_DEV), (1, 0, 2))


def _pad_rows(a, rows):
    return jnp.pad(a, ((0, rows - a.shape[0]), (0, 0)))


def kernel(x, c, ctx, c_ctx, w_ada, b_ada, ffn1_w_in, ffn1_w_out, mix_w_in, attn_q_gain, attn_k_gain, ret_decay_logit, w_proj_attn, w_proj_ret, mix_w_out, ffn2_w_in, ffn2_w_out, final_norm, loss_target, m_c_ctx, m_w_ada, m_b_ada, m_ffn1_w_in, m_ffn1_w_out, m_mix_w_in, m_attn_q_gain, m_attn_k_gain, m_ret_decay_logit, m_w_proj_attn, m_w_proj_ret, m_mix_w_out, m_ffn2_w_in, m_ffn2_w_out, m_final_norm, v_c_ctx, v_w_ada, v_b_ada, v_ffn1_w_in, v_ffn1_w_out, v_mix_w_in, v_attn_q_gain, v_attn_k_gain, v_ret_decay_logit, v_w_proj_attn, v_w_proj_ret, v_mix_w_out, v_ffn2_w_in, v_ffn2_w_out, v_final_norm):
    lx, d = x.shape[1], x.shape[2]
    lc = ctx.shape[1]
    t = lx + lc
    aw, rw = w_proj_attn.shape[1], w_proj_ret.shape[1]
    pw = mix_w_in.shape[2] * N_DEV
    kw = (pw - aw - 4 * rw - 2 * d) // 2
    groups = aw // kw
    heads_r = rw // HEAD_DIM
    ka_off, va_off = aw, aw + kw
    qr_off = aw + 2 * kw
    kr_off, vr_off, gr_off = qr_off + rw, qr_off + 2 * rw, qr_off + 3 * rw
    ga_off, gb_off = qr_off + 4 * rw, qr_off + 4 * rw + d
    xi, yi, ci = _coords()
    me = 4 * xi + 2 * yi + ci

    col_names = ["ffn1_w_in", "mix_w_in", "w_proj_attn", "w_proj_ret", "ffn2_w_in"]
    row_names = ["ffn1_w_out", "mix_w_out", "ffn2_w_out"]
    shard = dict(ffn1_w_in=ffn1_w_in[0], mix_w_in=mix_w_in[0], w_proj_attn=w_proj_attn[0], w_proj_ret=w_proj_ret[0],
                 ffn2_w_in=ffn2_w_in[0], ffn1_w_out=ffn1_w_out[0], mix_w_out=mix_w_out[0], ffn2_w_out=ffn2_w_out[0])
    names = col_names + row_names
    bf_shard = {k: shard[k].astype(BF16) for k in names}
    full, landed = {}, {}

    def is_wide(k):
        return k in col_names and shard[k].shape[1] % LANES == 0

    def gather_of(keys):
        return "gather", [bf_shard[k] for k in keys], [is_wide(k) for k in keys]

    def keep_full(keys, gathered):
        for k, g in zip(keys, gathered):
            if is_wide(k):
                full[k] = g
            else:
                full[k] = _cols_full(g) if k in col_names else g.reshape(N_DEV * g.shape[1], g.shape[2])

    def exchange_of(grads):
        blocks = [g if is_wide(k) else _cols_split(g) if k in col_names
                  else g.reshape(N_DEV, g.shape[0] // N_DEV, g.shape[1]) for k, g in grads.items()]
        return "exchange", blocks, [is_wide(k) for k in grads]

    def keep_landed(grads, got):
        landed.update(zip(grads.keys(), got))


    c_all = _all_gather_small("gather_cond", _pad_rows(c, SUBLANES))[:, 0, :]
    cond = _pad_rows(jnp.concatenate([c_all, c_ctx[None, :]], axis=0), 2 * SUBLANES)
    ada_part, s_cond = _ada_fwd("ada_fwd", cond, w_ada[0])
    ada_all = _all_gather_small("gather_ada", ada_part)
    mod_all = jnp.transpose(ada_all, (1, 0, 2)).reshape(2 * SUBLANES, N_MOD * d) + b_ada
    mod_x = lax.dynamic_index_in_dim(mod_all, me, axis=0, keepdims=False).reshape(N_MOD, d)
    mod_c = mod_all[N_DEV].reshape(N_MOD, d)
    mods = jnp.stack([mod_c, mod_x], axis=0)[:, :, None, :]
    sh1, sc1, g1, sh2, sc2, g2, sh3, sc3, g3 = [mods[:, k] for k in range(N_MOD)]

    h0 = jnp.concatenate([x[0], ctx[0]], axis=0)
    u1, got = _rmsmod("ffn1_norm", h0, sc1, sh1, lx, comm=gather_of(["ffn1_w_in"]))
    keep_full(["ffn1_w_in"], got)
    keys = ["ffn1_w_out", "mix_w_in"]
    (z1, s1), got = _ffn_in("ffn1_in", u1, full["ffn1_w_in"], comm=gather_of(keys))
    keep_full(keys, got)
    keys = ["w_proj_attn", "w_proj_ret", "mix_w_out"]
    (h1, f1), got = _mm_residual("ffn1_out", s1, full["ffn1_w_out"], h0, g1, 0.5, lx, comm=gather_of(keys))
    keep_full(keys, got)

    u2 = _rmsmod("mix_norm", h1, sc2, sh2, lx)
    keys = ["ffn2_w_in"]
    p, got = _mm_nn("mix_in", u2, full["mix_w_in"], BF16, tm_pref=M_TILE_BIG, tn_pref=512, comm=gather_of(keys))
    keep_full(keys, got)
    cosf, sinf = _rope_tables(lx, lc)
    q_rot, k_rot = _qk_prep("qk_prep", p, cosf, sinf, attn_q_gain, attn_k_gain, aw, kw)
    ya, lse = _attn_fwd("attn_fwd", q_rot, k_rot, p, va_off, lx, groups)

    decay = ret_decay_logit[0].astype(F32)
    log_gamma = jax.nn.log_sigmoid(decay)
    r_offs = (qr_off, kr_off, vr_off, rw)
    tab = [_ret_tables(log_gamma[k], k) for k in range(2)]
    y_f, st_f = _ret_fwd("ret_fwd_a", p, r_offs, tab[0], 0, lx, None)
    y_r, st_b = _ret_fwd("ret_fwd_b", p, r_offs, tab[1], 1, lx, y_f)
    yr = _ret_out("ret_out", y_r, p, gr_off, lx)

    pa = _mm_nn("proj_attn", ya, full["w_proj_attn"], BF16)
    pr = _mm_nn("proj_ret", yr, full["w_proj_ret"], BF16)
    mg = _merge("merge", pa, pr, p, ga_off, gb_off)
    h2, o2 = _mm_residual("mix_out", mg, full["mix_w_out"], h1, g2, 1.0, lx)

    u3 = _rmsmod("ffn2_norm", h2, sc3, sh3, lx)
    keys = ["ffn2_w_out"]
    (z3, s3), got = _ffn_in("ffn2_in", u3, full["ffn2_w_in"], comm=gather_of(keys))
    keep_full(keys, got)
    h3, f3 = _mm_residual("ffn2_out", s3, full["ffn2_w_out"], h2, g3, 0.5, lx)
    dh3, d_fn, loss_tile, df3, dg3 = _final_loss("final_loss", h3, final_norm[None, :], loss_target[0], f3, g3, 0.5)

    dz3 = _ffn_out_bwd("ffn2_out_bwd", df3, full["ffn2_w_out"], z3)
    grads = {"ffn2_w_out": _mm_tn("ffn2_out_dw", s3, df3, BF16, tm_pref=M_TILE_BIG, tk_pref=K_TILE_LONG)}
    g_ffn2_w_in, got = _ffn_in_bwd_w("ffn2_in_dw", u3, dz3, comm=exchange_of(grads))
    keep_landed(grads, got)
    grads = {"ffn2_w_in": g_ffn2_w_in}
    du3, got = _ffn_in_bwd_x("ffn2_in_bwd", dz3, full["ffn2_w_in"], comm=exchange_of(grads))
    keep_landed(grads, got)
    dh2, dsc3, dsh3, do2, dg2 = _rmsmod_bwd("ffn2_norm_bwd", du3, h2, sc3, dh3, lx, lx, gate=(o2, g2, 1.0))

    dpa, dpr, dga, dgb = _merge_bwd("merge_bwd", do2, full["mix_w_out"], pa, pr, p, ga_off, gb_off)
    dya = _mm_nt("proj_attn_bwd", dpa, full["w_proj_attn"], BF16)
    dyr = _mm_nt("proj_ret_bwd", dpr, full["w_proj_ret"], F32)
    mix_grads = {"mix_w_out": _mm_tn("mix_out_dw", mg, do2, BF16),
                 "w_proj_attn": _mm_tn("proj_attn_dw", ya, dpa, BF16),
                 "w_proj_ret": _mm_tn("proj_ret_dw", yr, dpr, BF16)}

    dy_ret, dgr = _ret_out_bwd("ret_out_bwd", dyr, y_r, p, gr_off, lx)
    dqr, dkr, dvr, dl_f = _ret_bwd("ret_bwd_a", p, r_offs, tab[0], st_f, dy_ret, 0, lx, None)
    dqr, dkr, dvr, dl_b = _ret_bwd("ret_bwd_b", p, r_offs, tab[1], st_b, dy_ret, 1, lx, (dqr, dkr, dvr))
    d_lam = jnp.stack([dl_f[:, 0, 0], dl_b[:, 0, 0]], axis=0)
    d_decay = d_lam * jax.nn.sigmoid(-decay)

    delta = _attn_delta("attn_delta", ya, dya)
    dq_rot, dk_rot, dva = _attn_bwd("attn_bwd", q_rot, k_rot, p, va_off, dya, lse, delta, lx, groups)
    dqa, dka, d_qg, d_kg = _qk_prep_bwd("qk_prep_bwd", dq_rot, dk_rot, p, cosf, sinf, attn_q_gain, attn_k_gain, aw, kw)

    dp = jnp.concatenate([dqa, dka, dva, dqr.astype(BF16), dkr.astype(BF16), dvr.astype(BF16),
                          _pad_rows(dgr, t), _pad_rows(dga, t), _pad_rows(dgb, t)], axis=1)
    grads = {"mix_w_in": _mm_tn("mix_in_dw", u2, dp, BF16, tm_pref=M_TILE_DW, tn_pref=512)}
    du2, got = _mm_nt("mix_in_bwd", dp, full["mix_w_in"], F32, comm=exchange_of(grads))
    keep_landed(grads, got)
    dh1, dsc2, dsh2, df1, dg1 = _rmsmod_bwd("mix_norm_bwd", du2, h1, sc2, dh2, lx, t, gate=(f1, g1, 0.5))

    dz1, got = _ffn_out_bwd("ffn1_out_bwd", df1, full["ffn1_w_out"], z1, comm=exchange_of(mix_grads))
    keep_landed(mix_grads, got)
    grads = {"ffn1_w_out": _mm_tn("ffn1_out_dw", s1, df1, BF16, tm_pref=M_TILE_BIG, tk_pref=K_TILE_LONG)}
    g_ffn1_w_in, got = _ffn_in_bwd_w("ffn1_in_dw", u1, dz1, comm=exchange_of(grads))
    keep_landed(grads, got)
    grads = {"ffn1_w_in": g_ffn1_w_in}
    du1, got = _ffn_in_bwd_x("ffn1_in_bwd", dz1, full["ffn1_w_in"], comm=exchange_of(grads))
    keep_landed(grads, got)
    grad_x, dsc1, dsh1 = _rmsmod_bwd("ffn1_norm_bwd", du1, h0, sc1, dh1, lx, lx)

    zero = jnp.zeros((1, d), F32)
    dmod_c = jnp.concatenate([dsh1[0], dsc1[0], dg1[0], dsh2[0], dsc2[0], zero, zero, zero, zero], axis=0)
    dmod_x = jnp.concatenate([dsh1[1], dsc1[1], dg1[1], dsh2[1], dsc2[1], dg2[1], dsh3[1], dsc3[1], dg3], axis=0)
    misc = jnp.concatenate([d_qg[0], d_kg[0], d_decay.reshape(-1), loss_tile[0, 0:1]])
    misc = jnp.pad(misc, (0, d - misc.shape[0]))[None, :]
    n_small = 3 * SUBLANES
    small = _pad_rows(jnp.concatenate([dmod_c, dmod_x, d_fn, misc], axis=0), n_small)
    small_all = _all_gather_small("gather_small", small)
    small_sum = _sum_slots("sum_small", small_all)
    dmod_c_sum, dmod_x_sum = small_sum[0:N_MOD], small_sum[N_MOD:2 * N_MOD]
    g_final_norm = small_sum[2 * N_MOD]
    misc_sum = small_sum[2 * N_MOD + 1]
    g_qg = misc_sum[0:HEAD_DIM][None, :]
    g_kg = misc_sum[HEAD_DIM:2 * HEAD_DIM][None, :]
    g_decay = misc_sum[2 * HEAD_DIM:2 * HEAD_DIM + 2 * heads_r].reshape(1, 2, heads_r)
    loss = misc_sum[2 * HEAD_DIM + 2 * heads_r]
    g_b_ada = (dmod_x_sum + dmod_c_sum).reshape(1, N_MOD * d)

    n_ada = w_ada.shape[2]
    dmod_rows = jnp.concatenate([small_all[:, N_MOD:2 * N_MOD, :].reshape(N_DEV, N_MOD * d),
                                 dmod_c_sum.reshape(1, N_MOD * d)], axis=0)
    dmod_mine = _pad_rows(lax.dynamic_slice_in_dim(dmod_rows, me * n_ada, n_ada, axis=1), 2 * SUBLANES)
    g_w_ada, ds_cond = _ada_bwd("ada_bwd", s_cond, dmod_mine, w_ada[0])
    cctx_parts = _all_gather_small("gather_cctx", ds_cond[N_DEV:N_DEV + SUBLANES])
    g_c_ctx = _cctx_grad("cctx_grad", cctx_parts, c_ctx[None, :])[0]

    mom = dict(ffn1_w_in=(m_ffn1_w_in, v_ffn1_w_in), mix_w_in=(m_mix_w_in, v_mix_w_in),
               w_proj_attn=(m_w_proj_attn, v_w_proj_attn), w_proj_ret=(m_w_proj_ret, v_w_proj_ret),
               ffn2_w_in=(m_ffn2_w_in, v_ffn2_w_in), ffn1_w_out=(m_ffn1_w_out, v_ffn1_w_out),
               mix_w_out=(m_mix_w_out, v_mix_w_out), ffn2_w_out=(m_ffn2_w_out, v_ffn2_w_out))
    res = {}
    for k in names:
        res[k] = _adamw("adamw_" + k, landed[k], shard[k], mom[k][0][0], mom[k][1][0])
    res["w_ada"] = _adamw("adamw_w_ada", g_w_ada[None], w_ada[0], m_w_ada[0], v_w_ada[0])

    def pack(cc, ba, qg, kg, dec, fn):
        misc_row = jnp.concatenate([qg.reshape(-1), kg.reshape(-1), dec.reshape(-1)])
        misc_row = jnp.pad(misc_row, (0, d - misc_row.shape[0]))[None, :]
        return _pad_rows(jnp.concatenate([cc.reshape(1, d), ba.reshape(N_MOD, d), fn.reshape(1, d), misc_row], axis=0),
                         2 * SUBLANES)

    sg, sd, sm, sv = _adamw(
        "adamw_small", pack(g_c_ctx, g_b_ada, g_qg, g_kg, g_decay, g_final_norm)[None],
        pack(c_ctx, b_ada, attn_q_gain, attn_k_gain, ret_decay_logit, final_norm),
        pack(m_c_ctx, m_b_ada, m_attn_q_gain, m_attn_k_gain, m_ret_decay_logit, m_final_norm),
        pack(v_c_ctx, v_b_ada, v_attn_q_gain, v_attn_k_gain, v_ret_decay_logit, v_final_norm))

    def unpack(a):
        misc_row = a[N_MOD + 2]
        return dict(c_ctx=a[0], b_ada=a[1:1 + N_MOD].reshape(1, N_MOD * d), final_norm=a[N_MOD + 1],
                    attn_q_gain=misc_row[0:HEAD_DIM][None, :], attn_k_gain=misc_row[HEAD_DIM:2 * HEAD_DIM][None, :],
                    ret_decay_logit=misc_row[2 * HEAD_DIM:2 * HEAD_DIM + 2 * heads_r].reshape(1, 2, heads_r))

    small_out = [unpack(a) for a in (sg, sd, sm, sv)]
    order = ["c_ctx", "w_ada", "b_ada", "ffn1_w_in", "ffn1_w_out", "mix_w_in", "attn_q_gain", "attn_k_gain",
             "ret_decay_logit", "w_proj_attn", "w_proj_ret", "mix_w_out", "ffn2_w_in", "ffn2_w_out", "final_norm"]
    outs = [loss, grad_x[None]]
    for which in range(4):
        for k in order:
            outs.append(res[k][which][None] if k in res else small_out[which][k])
    return tuple(outs)
```
